```python
import math
import jax
import jax.numpy as jnp
from jax import lax
import numpy as np

D_MODEL = 1024
BATCH = 32
SEQ = 256
DEPTH = 2
DEC_BATCH = 2
DEC_SEQ = 4096
PAST_LEN = 512

GRID_W = 64
ROPE_THETA = 10000.0
EPS = 1e-6
Q_BLOCK = 128
N_MOD = 6

MLA_HEADS = 8
MLA_NOPE = 64
MLA_ROPE = 32
MLA_V = 64
MLA_Q_RANK = 384
MLA_KV_RANK = 256
MLA_SCALE = 1.0 / math.sqrt(MLA_NOPE + MLA_ROPE)

CONV_CH = 512
CONV_WIDTH = 31

GQA_HEADS = 8
GQA_KV_HEADS = 2
GQA_HEAD_DIM = 64
GQA_SCALE = 1.0 / math.sqrt(GQA_HEAD_DIM)

CHUNK = 128
GMLP_GROUPS = 4
GMLP_GROUP_CH = 128
GMLP_CH = GMLP_GROUPS * GMLP_GROUP_CH

N_EXPERTS = 32
TOP_K = 4
D_EXPERT = 1024
SWIGLU_LIMIT = 7.0
SWIGLU_ALPHA = 1.702

IN0 = MLA_Q_RANK + MLA_KV_RANK + MLA_ROPE + 2 * CONV_CH
OUT0 = MLA_HEADS * MLA_V + CONV_CH
IN1 = (GQA_HEADS + 2 * GQA_KV_HEADS) * GQA_HEAD_DIM + 2 * GMLP_CH
OUT1 = GQA_HEADS * GQA_HEAD_DIM + GMLP_CH

kernel_name = 'hybrid_dit_prefix_step'


def rms_norm(x, g):
    xf = x.astype(jnp.float32)
    y = xf * lax.rsqrt(jnp.mean(xf * xf, axis=-1, keepdims=True) + EPS)
    return (y * g.astype(jnp.float32)).astype(x.dtype)


def layer_norm(x, g, b):
    xf = x.astype(jnp.float32)
    mu = jnp.mean(xf, axis=-1, keepdims=True)
    var = jnp.mean(jnp.square(xf - mu), axis=-1, keepdims=True)
    y = (xf - mu) * lax.rsqrt(var + EPS)
    return (y * g.astype(jnp.float32) + b.astype(jnp.float32)).astype(x.dtype)


def adaln_modulation(cond, ada_w, ada_b):
    m = jax.nn.silu(cond) @ ada_w + ada_b
    return jnp.split(m[..., None, :], N_MOD, axis=-1)


def modulate(x, g, shift, scale):
    return rms_norm(x, g) * (1.0 + scale) + shift


def axial_rope_tables(n_tokens, rot_dim):
    t = jnp.arange(n_tokens)
    rows = (t // GRID_W).astype(jnp.float32)
    cols = (t % GRID_W).astype(jnp.float32)
    n_freq = rot_dim // 4
    inv = ROPE_THETA ** (-jnp.arange(n_freq, dtype=jnp.float32) / n_freq)
    ang = jnp.concatenate([rows[:, None] * inv, cols[:, None] * inv], axis=-1)
    return jnp.cos(ang), jnp.sin(ang)


def apply_rope(x, cos, sin):
    half = x.shape[-1] // 2
    x1, x2 = x[..., :half], x[..., half:]
    c = cos[:, None, :].astype(x.dtype)
    s = sin[:, None, :].astype(x.dtype)
    return jnp.concatenate([x1 * c - x2 * s, x1 * s + x2 * c], axis=-1)


def block_attention(q, k, v, scale):
    B, Sq, H, dk = q.shape
    G = k.shape[2]
    rep = H // G
    nb = Sq // Q_BLOCK
    qb = q.reshape(B, nb, Q_BLOCK, G, rep, dk).transpose(1, 0, 2, 3, 4, 5)

    def one_block(qblk):
        s = jnp.einsum('bqgrd,bkgd->bgrqk', qblk, k, preferred_element_type=jnp.float32) * scale
        p = jax.nn.softmax(s, axis=-1)
        return jnp.einsum('bgrqk,bkgd->bqgrd', p.astype(v.dtype), v)

    o = lax.map(one_block, qb)
    return o.transpose(1, 0, 2, 3, 4, 5).reshape(B, Sq, H, v.shape[-1])


def conformer_conv(a, conv_w, conv_b, ln_g, ln_b):
    val, gate = jnp.split(a, 2, axis=-1)
    u = val * jax.nn.sigmoid(gate)
    u = lax.conv_general_dilated(
        u, conv_w.astype(u.dtype), window_strides=(1,),
        padding=[(CONV_WIDTH // 2, CONV_WIDTH // 2)],
        dimension_numbers=('NWC', 'WIO', 'NWC'),
        feature_group_count=CONV_CH) + conv_b
    return jax.nn.silu(layer_norm(u, ln_g, ln_b))


def chunk_spatial_gating(u, vg, ln_g, ln_b, w_s, b_s):
    B, S, _ = u.shape
    n = S // CHUNK
    vn = layer_norm(vg, ln_g, ln_b).reshape(B, n, CHUNK, GMLP_GROUPS, GMLP_GROUP_CH)
    mixed = jnp.einsum('gpq,bnqgc->bnpgc', w_s, vn) + b_s.T[None, None, :, :, None]
    return u * mixed.reshape(B, S, GMLP_CH)


def mla_conv_mixer(h, rope, ctx, w_in, q_norm_g, kv_norm_g, w_uq, w_uk, w_uv,
                   conv_w, conv_b, conv_ln_g, conv_ln_b, w_out):
    B, S, _ = h.shape
    z = h @ w_in
    c_q, c_kv, k_r, conv_in = jnp.split(
        z, [MLA_Q_RANK, MLA_Q_RANK + MLA_KV_RANK, MLA_Q_RANK + MLA_KV_RANK + MLA_ROPE], axis=-1)
    q = (rms_norm(c_q, q_norm_g) @ w_uq).reshape(B, S, MLA_HEADS, MLA_NOPE + MLA_ROPE)
    ckv = rms_norm(c_kv, kv_norm_g)
    ckv_all, kr_all = ckv, k_r
    if rope is not None:
        cos, sin = rope
        q = jnp.concatenate([q[..., :MLA_NOPE], apply_rope(q[..., MLA_NOPE:], cos, sin)], axis=-1)
        kr_all = apply_rope(k_r[:, :, None, :], cos, sin)[:, :, 0, :]
    if ctx is not None:
        ckv_all = jnp.concatenate([ctx[0], ckv_all], axis=1)
        kr_all = jnp.concatenate([ctx[1], kr_all], axis=1)
    Sk = ckv_all.shape[1]
    k_nope = (ckv_all @ w_uk).reshape(B, Sk, MLA_HEADS, MLA_NOPE)
    v = (ckv_all @ w_uv).reshape(B, Sk, MLA_HEADS, MLA_V)
    k = jnp.concatenate(
        [k_nope, jnp.broadcast_to(kr_all[:, :, None, :], (B, Sk, MLA_HEADS, MLA_ROPE))], axis=-1)
    attn = block_attention(q, k, v, MLA_SCALE)
    conv = conformer_conv(conv_in, conv_w, conv_b, conv_ln_g, conv_ln_b)
    y = jnp.concatenate([attn.reshape(B, S, MLA_HEADS * MLA_V), conv], axis=-1) @ w_out
    return y, (ckv, k_r)


def gqa_gmlp_mixer(h, rope, ctx, w_in, q_norm_g, k_norm_g, ln_g, ln_b, w_s, b_s, w_out):
    B, S, _ = h.shape
    qd = GQA_HEADS * GQA_HEAD_DIM
    kd = GQA_KV_HEADS * GQA_HEAD_DIM
    q, k, v, u, vg = jnp.split(h @ w_in, [qd, qd + kd, qd + 2 * kd, qd + 2 * kd + GMLP_CH], axis=-1)
    q = rms_norm(q.reshape(B, S, GQA_HEADS, GQA_HEAD_DIM), q_norm_g)
    k = rms_norm(k.reshape(B, S, GQA_KV_HEADS, GQA_HEAD_DIM), k_norm_g)
    v = v.reshape(B, S, GQA_KV_HEADS, GQA_HEAD_DIM)
    k_all, v_all = k, v
    if rope is not None:
        cos, sin = rope
        q = apply_rope(q, cos, sin)
        k_all = apply_rope(k, cos, sin)
    if ctx is not None:
        k_all = jnp.concatenate([ctx[0], k_all], axis=1)
        v_all = jnp.concatenate([ctx[1], v_all], axis=1)
    attn = block_attention(q, k_all, v_all, GQA_SCALE)
    gated = chunk_spatial_gating(u, vg, ln_g, ln_b, w_s, b_s)
    y = jnp.concatenate([attn.reshape(B, S, qd), gated], axis=-1) @ w_out
    return y, (k, v)


def routed_ffn(x, router_w, router_b, w1, b1, w2, b2):
    shp = x.shape
    xt = x.reshape(-1, shp[-1])
    logits = (xt @ router_w + router_b).astype(jnp.float32)
    top_val, top_idx = lax.top_k(logits, TOP_K)
    wts = jax.nn.softmax(top_val, axis=-1)
    combine = jnp.sum(jax.nn.one_hot(top_idx, N_EXPERTS, dtype=jnp.float32) * wts[..., None], axis=1)

    def expert(acc, ex):
        w1e, b1e, w2e, b2e, ce = ex
        hh = xt @ w1e + b1e
        g, lin = jnp.split(hh, 2, axis=-1)
        g = jnp.minimum(g, SWIGLU_LIMIT)
        lin = jnp.clip(lin, -SWIGLU_LIMIT, SWIGLU_LIMIT)
        y = ((lin + 1.0) * (g * jax.nn.sigmoid(SWIGLU_ALPHA * g))) @ w2e + b2e
        return acc + ce[:, None].astype(y.dtype) * y, None

    acc, _ = lax.scan(expert, jnp.zeros_like(xt), (w1, b1, w2, b2, combine.T))
    return acc.reshape(shp)


def trunk_layer(x, cond, mixer_fn, mix_p, rope, ctx, ada_p, moe_p):
    ada_w, ada_b, norm1_g, norm2_g = ada_p
    sh1, sc1, g1, sh2, sc2, g2 = adaln_modulation(cond, ada_w, ada_b)
    y, ctx_out = mixer_fn(modulate(x, norm1_g, sh1, sc1), rope, ctx, *mix_p)
    x = x + g1 * y
    x = x + g2 * routed_ffn(modulate(x, norm2_g, sh2, sc2), *moe_p)
    return x, ctx_out


def setup_inputs(seed: int = 0) -> dict:
    key = jax.random.key(seed)
    keys = iter(jax.random.split(key, 80))
    D = D_MODEL

    def nrm(shape, scale=1.0):
        return scale * jax.random.normal(next(keys), shape, jnp.float32)

    def gain(n):
        return 1.0 + 0.02 * nrm((n,))

    inp = {}
    inp['x_prompt'] = nrm((BATCH, SEQ, D))
    inp['x_sample'] = nrm((DEC_BATCH, DEC_SEQ, D))
    inp['cache_l0_ckv'] = nrm((DEC_BATCH, PAST_LEN, MLA_KV_RANK))
    inp['cache_l0_krope'] = nrm((DEC_BATCH, PAST_LEN, MLA_ROPE))
    inp['cache_l1_k'] = nrm((DEC_BATCH, PAST_LEN, GQA_KV_HEADS, GQA_HEAD_DIM))
    inp['cache_l1_v'] = nrm((DEC_BATCH, PAST_LEN, GQA_KV_HEADS, GQA_HEAD_DIM))
    inp['c'] = nrm((DEC_BATCH, D))
    inp['c_ctx'] = nrm((D,))
    inp['l0_ada_w'] = nrm((D, N_MOD * D), 0.5 * D ** -0.5)
    inp['l0_ada_b'] = nrm((N_MOD * D,), 0.02)
    inp['l0_norm1'] = gain(D)
    inp['l0_w_in'] = nrm((D, IN0), D ** -0.5)
    inp['l0_q_norm'] = gain(MLA_Q_RANK)
    inp['l0_kv_norm'] = gain(MLA_KV_RANK)
    inp['l0_w_uq'] = nrm((MLA_Q_RANK, MLA_HEADS * (MLA_NOPE + MLA_ROPE)), MLA_Q_RANK ** -0.5)
    inp['l0_w_uk'] = nrm((MLA_KV_RANK, MLA_HEADS * MLA_NOPE), MLA_KV_RANK ** -0.5)
    inp['l0_w_uv'] = nrm((MLA_KV_RANK, MLA_HEADS * MLA_V), MLA_KV_RANK ** -0.5)
    inp['l0_conv_w'] = nrm((CONV_WIDTH, 1, CONV_CH), CONV_WIDTH ** -0.5)
    inp['l0_conv_b'] = nrm((CONV_CH,), 0.02)
    inp['l0_conv_ln_g'] = gain(CONV_CH)
    inp['l0_conv_ln_b'] = nrm((CONV_CH,), 0.02)
    inp['l0_w_out'] = nrm((OUT0, D), OUT0 ** -0.5)
    inp['l0_norm2'] = gain(D)
    inp['l0_router_w'] = nrm((D, N_EXPERTS), D ** -0.5)
    inp['l0_router_b'] = nrm((N_EXPERTS,), 0.01)
    inp['l0_w1'] = nrm((N_EXPERTS, D, 2 * D_EXPERT), D ** -0.5)
    inp['l0_b1'] = nrm((N_EXPERTS, 2 * D_EXPERT), 0.02)
    inp['l0_w2'] = nrm((N_EXPERTS, D_EXPERT, D), D_EXPERT ** -0.5)
    inp['l0_b2'] = nrm((N_EXPERTS, D), 0.02)
    inp['l1_ada_w'] = nrm((D, N_MOD * D), 0.5 * D ** -0.5)
    inp['l1_ada_b'] = nrm((N_MOD * D,), 0.02)
    inp['l1_norm1'] = gain(D)
    inp['l1_w_in'] = nrm((D, IN1), D ** -0.5)
    inp['l1_q_norm'] = gain(GQA_HEAD_DIM)
    inp['l1_k_norm'] = gain(GQA_HEAD_DIM)
    inp['l1_gmlp_ln_g'] = gain(GMLP_CH)
    inp['l1_gmlp_ln_b'] = nrm((GMLP_CH,), 0.02)
    inp['l1_w_s'] = nrm((GMLP_GROUPS, CHUNK, CHUNK), CHUNK ** -0.5)
    inp['l1_b_s'] = nrm((GMLP_GROUPS, CHUNK), 0.02)
    inp['l1_w_out'] = nrm((OUT1, D), OUT1 ** -0.5)
    inp['l1_norm2'] = gain(D)
    inp['l1_router_w'] = nrm((D, N_EXPERTS), D ** -0.5)
    inp['l1_router_b'] = nrm((N_EXPERTS,), 0.01)
    inp['l1_w1'] = nrm((N_EXPERTS, D, 2 * D_EXPERT), D ** -0.5)
    inp['l1_b1'] = nrm((N_EXPERTS, 2 * D_EXPERT), 0.02)
    inp['l1_w2'] = nrm((N_EXPERTS, D_EXPERT, D), D_EXPERT ** -0.5)
    inp['l1_b2'] = nrm((N_EXPERTS, D), 0.02)
    inp['final_norm'] = gain(D)
    return inp


def reference(x_prompt, x_sample, cache_l0_ckv, cache_l0_krope, cache_l1_k, cache_l1_v, c, c_ctx,
              l0_ada_w, l0_ada_b, l0_norm1, l0_w_in, l0_q_norm, l0_kv_norm, l0_w_uq, l0_w_uk, l0_w_uv,
              l0_conv_w, l0_conv_b, l0_conv_ln_g, l0_conv_ln_b, l0_w_out, l0_norm2,
              l0_router_w, l0_router_b, l0_w1, l0_b1, l0_w2, l0_b2,
              l1_ada_w, l1_ada_b, l1_norm1, l1_w_in, l1_q_norm, l1_k_norm, l1_gmlp_ln_g, l1_gmlp_ln_b,
              l1_w_s, l1_b_s, l1_w_out, l1_norm2,
              l1_router_w, l1_router_b, l1_w1, l1_b1, l1_w2, l1_b2,
              final_norm):
    n_lat = x_sample.shape[1]
    rope_mla = axial_rope_tables(n_lat, MLA_ROPE)
    rope_gqa = axial_rope_tables(n_lat, GQA_HEAD_DIM)
    layers = (
        (mla_conv_mixer,
         (l0_w_in, l0_q_norm, l0_kv_norm, l0_w_uq, l0_w_uk, l0_w_uv,
          l0_conv_w, l0_conv_b, l0_conv_ln_g, l0_conv_ln_b, l0_w_out),
         (l0_ada_w, l0_ada_b, l0_norm1, l0_norm2),
         (l0_router_w, l0_router_b, l0_w1, l0_b1, l0_w2, l0_b2),
         rope_mla, (cache_l0_ckv, cache_l0_krope)),
        (gqa_gmlp_mixer,
         (l1_w_in, l1_q_norm, l1_k_norm, l1_gmlp_ln_g, l1_gmlp_ln_b, l1_w_s, l1_b_s, l1_w_out),
         (l1_ada_w, l1_ada_b, l1_norm1, l1_norm2),
         (l1_router_w, l1_router_b, l1_w1, l1_b1, l1_w2, l1_b2),
         rope_gqa, (cache_l1_k, cache_l1_v)),
    )
    h, s = x_prompt, x_sample
    ctx_states = []
    for i in range(DEPTH):
        mixer_fn, mix_p, ada_p, moe_p, rope, cache = layers[i]
        h, ctx_out = trunk_layer(h, c_ctx, mixer_fn, mix_p, None, None, ada_p, moe_p)
        ctx_states.append(ctx_out)
        s, _ = trunk_layer(s, c, mixer_fn, mix_p, rope, cache, ada_p, moe_p)
    y_prompt = rms_norm(h, final_norm)
    y_sample = rms_norm(s, final_norm)
    (new_l0_ckv, new_l0_krope), (new_l1_k, new_l1_v) = ctx_states
    return (y_prompt, y_sample, new_l0_ckv, new_l0_krope, new_l1_k, new_l1_v)
```

```python
import functools
import math

import jax
import jax.numpy as jnp
from jax import lax
from jax.experimental import pallas as pl
from jax.experimental.pallas import tpu as pltpu

F32 = jnp.float32
BF16 = jnp.bfloat16
HIGHEST = lax.Precision.HIGHEST

LANES = 128
SUBLANES = 8
VMEM_LIMIT = 56 * 1024 * 1024

D_MODEL = 1024
GRID_W = 64
ROPE_THETA = 10000.0
EPS = 1e-6
N_MOD = 6

MLA_HEADS = 8
MLA_NOPE = 64
MLA_ROPE = 32
MLA_V = 64
MLA_Q_RANK = 384
MLA_KV_RANK = 256
MLA_SCALE = 1.0 / math.sqrt(MLA_NOPE + MLA_ROPE)
CONV_CH = 512
CONV_WIDTH = 31
CONV_HALO = 16

GQA_HEADS = 8
GQA_KV_HEADS = 2
GQA_HEAD_DIM = 64
GQA_SCALE = 1.0 / math.sqrt(GQA_HEAD_DIM)
CHUNK = 128
GMLP_GROUPS = 4
GMLP_CH = 512

N_EXPERTS = 32
TOP_K = 4
D_EXPERT = 1024
SWIGLU_LIMIT = 7.0
SWIGLU_ALPHA = 1.702

ROW_TILE = 512
FFN_TILE = 256
MOVE_TILE = 256
MOVE_UNROLL = 8
ATT_Q_TILE = 256
NEG_BIG = -1e30


def _params(sem, vmem=None):
    return pltpu.CompilerParams(dimension_semantics=sem, vmem_limit_bytes=vmem)


def _rms(x, g):
    return x * lax.rsqrt(jnp.mean(x * x, axis=-1, keepdims=True) + EPS) * g


def _const_spec(shape):
    nd = len(shape)
    return pl.BlockSpec(shape, lambda *_: (0,) * nd)


def _adaln_kernel(c_ref, w_ref, b_ref, o_ref):
    c = c_ref[...]
    s = c * jax.nn.sigmoid(c)
    o_ref[...] = jnp.dot(s, w_ref[...], preferred_element_type=F32, precision=HIGHEST) + b_ref[...]


def adaln(cond8, ada_w, ada_b):
    d, n = ada_w.shape
    bn = n // 4
    m = pl.pallas_call(
        _adaln_kernel,
        out_shape=jax.ShapeDtypeStruct((SUBLANES, n), F32),
        grid=(n // bn,),
        in_specs=[_const_spec((SUBLANES, d)),
                  pl.BlockSpec((d, bn), lambda j: (0, j)),
                  pl.BlockSpec((1, bn), lambda j: (0, j))],
        out_specs=pl.BlockSpec((SUBLANES, bn), lambda j: (0, j)),
        compiler_params=_params(("arbitrary",), VMEM_LIMIT),
        name="adaln",
    )(cond8, ada_w, ada_b.reshape(1, n))
    m = m.reshape(SUBLANES, N_MOD, d)
    return jnp.pad(m, ((0, 0), (0, SUBLANES - N_MOD), (0, 0)))


def _l0_inproj_kernel(*refs, rope):
    if rope:
        (x_ref, m_ref, n1_ref, win_ref, qg_ref, kvg_ref, wuq_ref, wuqs_ref, wuk_ref, e_ref, wuv_ref,
         cq_ref, sq_ref, ck_ref, sk_ref, q_out, k_out, v_out, ckv_out, kr_out, u_out) = refs
    else:
        (x_ref, m_ref, n1_ref, win_ref, qg_ref, kvg_ref, wuq_ref, wuk_ref, e_ref, wuv_ref,
         q_out, k_out, v_out, ckv_out, kr_out, u_out) = refs
    x = x_ref[0]
    m = m_ref[0]
    h = _rms(x, n1_ref[...]) * (1.0 + m[1:2]) + m[0:1]
    z = jnp.dot(h.astype(BF16), win_ref[...], preferred_element_type=F32)
    c_q = z[:, 0:MLA_Q_RANK]
    c_kv = z[:, MLA_Q_RANK:MLA_Q_RANK + MLA_KV_RANK]
    kr_blk = z[:, 640:768]
    val = z[:, 768:768 + CONV_CH]
    gate = z[:, 768 + CONV_CH:768 + 2 * CONV_CH]

    cqn = _rms(c_q, qg_ref[...]).astype(BF16)
    q = jnp.dot(cqn, wuq_ref[...], preferred_element_type=F32)
    if rope:
        qs = jnp.dot(cqn, wuqs_ref[...], preferred_element_type=F32)
        cq = cq_ref[...]
        sq = sq_ref[...]
        for hd in range(MLA_HEADS):
            sl = slice(hd * LANES, (hd + 1) * LANES)
            q_out[0, :, sl] = (q[:, sl] * cq + qs[:, sl] * sq).astype(BF16)
        kr = kr_blk * ck_ref[...] + pltpu.roll(kr_blk, LANES - MLA_ROPE, 1) * sk_ref[...]
    else:
        q_out[0] = q.astype(BF16)
        kr = kr_blk

    ckv = _rms(c_kv, kvg_ref[...])
    ckv_out[0] = ckv
    kr_out[0] = kr_blk[:, 0:MLA_ROPE]
    ckv_b = ckv.astype(BF16)
    k = (jnp.dot(ckv_b, wuk_ref[...], preferred_element_type=F32)
         + jnp.dot(kr.astype(BF16), e_ref[...], preferred_element_type=F32))
    k_out[0] = k.astype(BF16)
    v_out[0] = jnp.dot(ckv_b, wuv_ref[...], preferred_element_type=F32).astype(BF16)
    u_out[0] = val * jax.nn.sigmoid(gate)


def l0_inproj(x, mods, mod_off, n1, w, tables):
    bm, sm, d = x.shape
    tr = min(ROW_TILE, sm)
    rope = tables is not None
    hp = MLA_HEADS * LANES
    row = lambda width: pl.BlockSpec((1, tr, width), lambda b, i: (b, i, 0))
    in_specs = [row(d),
                pl.BlockSpec((1, SUBLANES, d), lambda b, i: (b + mod_off, 0, 0)),
                _const_spec((1, d)), _const_spec(w["win"].shape),
                _const_spec((1, MLA_Q_RANK)), _const_spec((1, MLA_KV_RANK)),
                _const_spec(w["wuq"].shape)]
    args = [x, mods, n1, w["win"], w["qg"], w["kvg"], w["wuq"]]
    if rope:
        in_specs.append(_const_spec(w["wuqs"].shape))
        args.append(w["wuqs"])
    in_specs += [_const_spec(w["wuk"].shape), _const_spec(w["e"].shape), _const_spec(w["wuv"].shape)]
    args += [w["wuk"], w["e"], w["wuv"]]
    if rope:
        in_specs += [pl.BlockSpec((tr, LANES), lambda b, i: (i, 0))] * 4
        args += list(tables)
    out_shape = [jax.ShapeDtypeStruct((bm, sm, hp), BF16),
                 jax.ShapeDtypeStruct((bm, sm, hp), BF16),
                 jax.ShapeDtypeStruct((bm, sm, hp), BF16),
                 jax.ShapeDtypeStruct((bm, sm, MLA_KV_RANK), F32),
                 jax.ShapeDtypeStruct((bm, sm, MLA_ROPE), F32),
                 jax.ShapeDtypeStruct((bm, sm, CONV_CH), F32)]
    out_specs = [row(hp), row(hp), row(hp), row(MLA_KV_RANK), row(MLA_ROPE), row(CONV_CH)]
    return pl.pallas_call(
        functools.partial(_l0_inproj_kernel, rope=rope),
        out_shape=out_shape, grid=(bm, sm // tr), in_specs=in_specs, out_specs=out_specs,
        compiler_params=_params(("parallel", "parallel"), VMEM_LIMIT),
        name="l0_inproj_rope" if rope else "l0_inproj",
    )(*args)


def _mla_ctx_kv_kernel(ckv_ref, kr_ref, wuk_ref, e_ref, wuv_ref, k_out, v_out):
    ckv_b = ckv_ref[0].astype(BF16)
    k = (jnp.dot(ckv_b, wuk_ref[...], preferred_element_type=F32)
         + jnp.dot(kr_ref[0].astype(BF16), e_ref[...], preferred_element_type=F32))
    k_out[0] = k.astype(BF16)
    v_out[0] = jnp.dot(ckv_b, wuv_ref[...], preferred_element_type=F32).astype(BF16)


def mla_ctx_kv(ckv, kr128, w):
    b, s, _ = ckv.shape
    hp = MLA_HEADS * LANES
    blk = lambda width: pl.BlockSpec((1, s, width), lambda i: (i, 0, 0))
    return pl.pallas_call(
        _mla_ctx_kv_kernel,
        out_shape=[jax.ShapeDtypeStruct((b, s, hp), BF16)] * 2,
        grid=(b,),
        in_specs=[blk(MLA_KV_RANK), blk(LANES), _const_spec(w["wuk"].shape),
                  _const_spec(w["e"].shape), _const_spec(w["wuv"].shape)],
        out_specs=[blk(hp), blk(hp)],
        compiler_params=_params(("parallel",), VMEM_LIMIT),
        name="mla_ctx_kv",
    )(ckv, kr128, w["wuk"], w["e"], w["wuv"])


def _conv_kernel(prev_ref, cur_ref, next_ref, w_ref, b_ref, g_ref, beta_ref, o_ref, pad_ref, *, rb):
    i = pl.program_id(1)
    last = pl.num_programs(1) - 1
    zeros = jnp.zeros((CONV_HALO, CONV_CH), F32)
    pad_ref[0:CONV_HALO, :] = jnp.where(i == 0, zeros, prev_ref[0])
    pad_ref[CONV_HALO:CONV_HALO + rb, :] = cur_ref[0]
    pad_ref[CONV_HALO + rb:CONV_HALO + rb + CONV_HALO, :] = jnp.where(i == last, zeros, next_ref[0])
    w = w_ref[...]
    shift = CONV_HALO - CONV_WIDTH // 2
    acc = jnp.zeros((rb, CONV_CH), F32) + b_ref[...]
    for k in range(CONV_WIDTH):
        acc = acc + pad_ref[k + shift:k + shift + rb, :] * w[k:k + 1, :]
    mu = jnp.mean(acc, axis=-1, keepdims=True)
    cen = acc - mu
    var = jnp.mean(cen * cen, axis=-1, keepdims=True)
    y = cen * lax.rsqrt(var + EPS) * g_ref[...] + beta_ref[...]
    o_ref[0] = (y * jax.nn.sigmoid(y)).astype(BF16)


def conformer_conv(u, conv_w, conv_b, ln_g, ln_b):
    b, s, c = u.shape
    rb = min(256, s)
    nh = rb // CONV_HALO
    n_halo_blocks = s // CONV_HALO
    wpad = jnp.pad(conv_w.reshape(CONV_WIDTH, c), ((0, 32 - CONV_WIDTH), (0, 0)))
    return pl.pallas_call(
        functools.partial(_conv_kernel, rb=rb),
        out_shape=jax.ShapeDtypeStruct((b, s, c), BF16),
        grid=(b, s // rb),
        in_specs=[pl.BlockSpec((1, CONV_HALO, c), lambda bi, i: (bi, jnp.maximum(i * nh - 1, 0), 0)),
                  pl.BlockSpec((1, rb, c), lambda bi, i: (bi, i, 0)),
                  pl.BlockSpec((1, CONV_HALO, c),
                               lambda bi, i: (bi, jnp.minimum((i + 1) * nh, n_halo_blocks - 1), 0)),
                  _const_spec((32, c)), _const_spec((1, c)), _const_spec((1, c)), _const_spec((1, c))],
        out_specs=pl.BlockSpec((1, rb, c), lambda bi, i: (bi, i, 0)),
        scratch_shapes=[pltpu.VMEM((rb + 2 * CONV_HALO, c), F32)],
        compiler_params=_params(("parallel", "parallel"), VMEM_LIMIT),
        name="conformer_conv",
    )(u, u, u, wpad, conv_b.reshape(1, c), ln_g.reshape(1, c), ln_b.reshape(1, c))


def _attn_kernel(q_ref, k_ref, v_ref, o_ref, *, heads, rep, scale):
    for hd in range(heads):
        g = hd // rep
        q = q_ref[0, :, hd * LANES:(hd + 1) * LANES]
        k = k_ref[0, :, g * LANES:(g + 1) * LANES]
        s = lax.dot_general(q, k, (((1,), (1,)), ((), ())), preferred_element_type=F32) * scale
        m = jnp.max(s, axis=-1, keepdims=True)
        p = jnp.exp(s - m)
        l = jnp.sum(p, axis=-1, keepdims=True)
        o = jnp.dot(p.astype(BF16), v_ref[0, :, g * LANES:(g + 1) * LANES], preferred_element_type=F32)
        o_ref[0, :, hd * LANES:(hd + 1) * LANES] = (o / l).astype(BF16)


def attention(q, k, v, *, n_heads, n_kv, scale, heads_per_step):
    b, sq, _ = q.shape
    sk = k.shape[1]
    rep = n_heads // n_kv
    tq = min(ATT_Q_TILE, sq)
    if heads_per_step == n_heads:
        grid = (b, 1, sq // tq)
        kv_spec = pl.BlockSpec((1, sk, n_kv * LANES), lambda bi, h, i: (bi, 0, 0))
        kern = functools.partial(_attn_kernel, heads=n_heads, rep=rep, scale=scale)
    else:
        assert heads_per_step == 1
        grid = (b, n_heads, sq // tq)
        kv_spec = pl.BlockSpec((1, sk, LANES), lambda bi, h, i: (bi, 0, h // rep))
        kern = functools.partial(_attn_kernel, heads=1, rep=1, scale=scale)
    q_spec = pl.BlockSpec((1, tq, heads_per_step * LANES), lambda bi, h, i: (bi, i, h))
    return pl.pallas_call(
        kern,
        out_shape=jax.ShapeDtypeStruct(q.shape, BF16),
        grid=grid, in_specs=[q_spec, kv_spec, kv_spec], out_specs=q_spec,
        compiler_params=_params(("parallel", "parallel", "parallel"), VMEM_LIMIT),
        name="attention",
    )(q, k, v)


def _l1_inproj_kernel(*refs, rope):
    if rope:
        (x_ref, m_ref, n1_ref, win_ref, wsw_ref, qg_ref, qgs_ref, kg_ref, kgs_ref, lng_ref, lnb_ref,
         ws_ref, bs_ref, c_ref, s_ref, q_out, k_out, kn_out, vp_out, v_out, g_out) = refs
    else:
        (x_ref, m_ref, n1_ref, win_ref, qg_ref, kg_ref, lng_ref, lnb_ref,
         ws_ref, bs_ref, q_out, k_out, kn_out, vp_out, v_out, g_out) = refs
    x = x_ref[0]
    m = m_ref[0]
    hb = (_rms(x, n1_ref[...]) * (1.0 + m[1:2]) + m[0:1]).astype(BF16)
    z = jnp.dot(hb, win_ref[...], preferred_element_type=F32)
    qw = GQA_HEADS * LANES
    kw = GQA_KV_HEADS * LANES
    o_k, o_vp, o_v, o_u, o_vg = qw, qw + kw, qw + 2 * kw, qw + 2 * kw + LANES, qw + 2 * kw + LANES + GMLP_CH
    if rope:
        zs = jnp.dot(hb, wsw_ref[...], preferred_element_type=F32)
        cos = c_ref[...]
        sin = s_ref[...]

    def head(col, zcol, g_ref, gs_ref):
        t = z[:, col:col + LANES]
        r = lax.rsqrt(jnp.sum(t * t, axis=-1, keepdims=True) * (1.0 / GQA_HEAD_DIM) + EPS)
        normed = t * r * g_ref[...]
        if not rope:
            return normed, normed
        ts = zs[:, zcol:zcol + LANES]
        return normed, normed * cos + ts * r * gs_ref[...] * sin

    for hd in range(GQA_HEADS):
        _, rot = head(hd * LANES, hd * LANES, qg_ref, qgs_ref if rope else None)
        q_out[0, :, hd * LANES:(hd + 1) * LANES] = rot.astype(BF16)
    for hd in range(GQA_KV_HEADS):
        normed, rot = head(o_k + hd * LANES, qw + hd * LANES, kg_ref, kgs_ref if rope else None)
        k_out[0, :, hd * LANES:(hd + 1) * LANES] = rot.astype(BF16)
        kn_out[0, :, hd * LANES:(hd + 1) * LANES] = normed
    vp_out[0] = z[:, o_vp:o_vp + kw].astype(BF16)
    v_out[0] = z[:, o_v:o_v + LANES]

    u = z[:, o_u:o_u + GMLP_CH]
    vg = z[:, o_vg:o_vg + GMLP_CH]
    mu = jnp.mean(vg, axis=-1, keepdims=True)
    cen = vg - mu
    var = jnp.mean(cen * cen, axis=-1, keepdims=True)
    vn = (cen * lax.rsqrt(var + EPS) * lng_ref[...] + lnb_ref[...]).astype(BF16)
    bias = bs_ref[...]
    rows = x.shape[0]
    for cidx in range(rows // CHUNK):
        r0 = cidx * CHUNK
        for g in range(GMLP_GROUPS):
            c0 = g * LANES
            mixed = jnp.dot(ws_ref[g], vn[r0:r0 + CHUNK, c0:c0 + LANES], preferred_element_type=F32)
            g_out[0, r0:r0 + CHUNK, c0:c0 + LANES] = (
                u[r0:r0 + CHUNK, c0:c0 + LANES] * (mixed + bias[:, c0:c0 + LANES])).astype(BF16)


def l1_inproj(x, mods, mod_off, n1, w, tables):
    bm, sm, d = x.shape
    tr = min(ROW_TILE, sm)
    rope = tables is not None
    qw = GQA_HEADS * LANES
    kw = GQA_KV_HEADS * LANES
    row = lambda width: pl.BlockSpec((1, tr, width), lambda b, i: (b, i, 0))
    vec = _const_spec((1, LANES))
    in_specs = [row(d), pl.BlockSpec((1, SUBLANES, d), lambda b, i: (b + mod_off, 0, 0)),
                _const_spec((1, d)), _const_spec(w["win"].shape)]
    args = [x, mods, n1, w["win"]]
    if rope:
        in_specs += [_const_spec(w["wsw"].shape), vec, vec, vec, vec]
        args += [w["wsw"], w["qg"], w["qgs"], w["kg"], w["kgs"]]
    else:
        in_specs += [vec, vec]
        args += [w["qg"], w["kg"]]
    in_specs += [_const_spec((1, GMLP_CH)), _const_spec((1, GMLP_CH)),
                 _const_spec(w["ws"].shape), _const_spec((CHUNK, GMLP_CH))]
    args += [w["lng"], w["lnb"], w["ws"], w["bs"]]
    if rope:
        in_specs += [pl.BlockSpec((tr, LANES), lambda b, i: (i, 0))] * 2
        args += list(tables)
    out_shape = [jax.ShapeDtypeStruct((bm, sm, qw), BF16),
                 jax.ShapeDtypeStruct((bm, sm, kw), BF16),
                 jax.ShapeDtypeStruct((bm, sm, kw), F32),
                 jax.ShapeDtypeStruct((bm, sm, kw), BF16),
                 jax.ShapeDtypeStruct((bm, sm, LANES), F32),
                 jax.ShapeDtypeStruct((bm, sm, GMLP_CH), BF16)]
    out_specs = [row(qw), row(kw), row(kw), row(kw), row(LANES), row(GMLP_CH)]
    return pl.pallas_call(
        functools.partial(_l1_inproj_kernel, rope=rope),
        out_shape=out_shape, grid=(bm, sm // tr), in_specs=in_specs, out_specs=out_specs,
        compiler_params=_params(("parallel", "parallel"), VMEM_LIMIT),
        name="l1_inproj_rope" if rope else "l1_inproj",
    )(*args)


def _post_kernel(a_ref, c_ref, x_ref, m_ref, n2_ref, wa_ref, wc_ref, rw_ref, rb_ref, tri_ref, base_ref,
                 x1_out, xm_out, idx_out, wts_out, rank_out, cnt_out, run_ref):
    first = (pl.program_id(0) == 0) & (pl.program_id(1) == 0)

    @pl.when(first)
    def _():
        run_ref[...] = base_ref[...]

    m = m_ref[0]
    y = (jnp.dot(a_ref[0], wa_ref[...], preferred_element_type=F32)
         + jnp.dot(c_ref[0], wc_ref[...], preferred_element_type=F32))
    x1 = x_ref[0] + m[2:3] * y
    x1_out[0] = x1
    xm = _rms(x1, n2_ref[...]) * (1.0 + m[4:5]) + m[3:4]
    xm_out[0] = xm

    logits = jnp.dot(xm, rw_ref[...], preferred_element_type=F32, precision=HIGHEST) + rb_ref[...]
    rows = logits.shape[0]
    lane = lax.broadcasted_iota(jnp.int32, (rows, LANES), 1).astype(F32)
    work = logits
    vals, hots = [], []
    idx_acc = jnp.zeros((rows, LANES), F32)
    for k in range(TOP_K):
        top = jnp.max(work, axis=-1, keepdims=True)
        sel = jnp.min(jnp.where(work == top, lane, float(LANES)), axis=-1, keepdims=True)
        hot = lane == sel
        vals.append(top)
        hots.append(hot)
        idx_acc = idx_acc + jnp.where(lane == float(k), sel, 0.0)
        work = jnp.where(hot, -jnp.inf, work)
    exps = [jnp.exp(v - vals[0]) for v in vals]
    denom = exps[0] + exps[1] + exps[2] + exps[3]
    wts = jnp.zeros((rows, LANES), F32)
    for k in range(TOP_K):
        wts = wts + jnp.where(lane == float(k), exps[k] / denom, 0.0)

    chosen = jnp.zeros((rows, LANES), F32)
    for hot in hots:
        chosen = chosen + hot.astype(F32)
    before = jnp.dot(tri_ref[...], chosen.astype(BF16), preferred_element_type=F32) + run_ref[0:1, :]
    rank = jnp.zeros((rows, LANES), F32)
    for k in range(TOP_K):
        rk = jnp.sum(jnp.where(hots[k], before, 0.0), axis=-1, keepdims=True)
        rank = rank + jnp.where(lane == float(k), rk, 0.0)
    run_ref[0:1, :] = run_ref[0:1, :] + jnp.sum(chosen, axis=0, keepdims=True)
    idx_out[0] = idx_acc.astype(jnp.int32)
    wts_out[0] = wts
    rank_out[0] = rank.astype(jnp.int32)
    cnt_out[...] = run_ref[...]


def post_mixer(attn, other, x, mods, mod_off, n2, wa, wc, rw, rb, tri, base):
    bm, sm, d = x.shape
    tr = tri.shape[0]
    row = lambda width: pl.BlockSpec((1, tr, width), lambda b, i: (b, i, 0))
    out_shape = [jax.ShapeDtypeStruct((bm, sm, d), F32),
                 jax.ShapeDtypeStruct((bm, sm, d), F32),
                 jax.ShapeDtypeStruct((bm, sm, LANES), jnp.int32),
                 jax.ShapeDtypeStruct((bm, sm, LANES), F32),
                 jax.ShapeDtypeStruct((bm, sm, LANES), jnp.int32),
                 jax.ShapeDtypeStruct((SUBLANES, LANES), F32)]
    return pl.pallas_call(
        _post_kernel,
        out_shape=out_shape, grid=(bm, sm // tr),
        in_specs=[row(attn.shape[-1]), row(other.shape[-1]), row(d),
                  pl.BlockSpec((1, SUBLANES, d), lambda b, i: (b + mod_off, 0, 0)),
                  _const_spec((1, d)), _const_spec(wa.shape), _const_spec(wc.shape),
                  _const_spec(rw.shape), _const_spec((1, LANES)), _const_spec(tri.shape),
                  _const_spec((SUBLANES, LANES))],
        out_specs=[row(d), row(d), row(LANES), row(LANES), row(LANES), _const_spec((SUBLANES, LANES))],
        scratch_shapes=[pltpu.VMEM((SUBLANES, LANES), F32)],
        compiler_params=_params(("arbitrary", "arbitrary"), VMEM_LIMIT),
        name="post_mixer_route",
    )(attn, other, x, mods, n2, wa, wc, rw, rb, tri, base)


def _plan_kernel(off_ref, idx_ref, rank_ref, pos_out):
    idx = idx_ref[...]
    pos = rank_ref[...]
    for e in range(N_EXPERTS):
        pos = pos + jnp.where(idx == e, off_ref[e], 0)
    pos_out[...] = pos


def plan_positions(offsets, idx, rank):
    n = idx.shape[0]
    tr = min(2048, n)
    spec = pl.BlockSpec((tr, LANES), lambda i, off: (i, 0))
    return pl.pallas_call(
        _plan_kernel,
        out_shape=jax.ShapeDtypeStruct((n, LANES), jnp.int32),
        grid_spec=pltpu.PrefetchScalarGridSpec(
            num_scalar_prefetch=1, grid=(n // tr,), in_specs=[spec, spec], out_specs=spec),
        compiler_params=_params(("parallel",)),
        name="plan_positions",
    )(offsets, idx, rank)


def _dispatch_kernel(pos_ref, xm_ref, xs_in, xs_out, sem):
    del xs_in
    t0 = pl.program_id(0) * MOVE_TILE

    def row_copy(t, p):
        return pltpu.make_async_copy(xm_ref.at[pl.ds(t, 1), :], xs_out.at[pl.ds(p, 1), :], sem)

    def issue(c, carry):
        for j in range(MOVE_UNROLL):
            t = c * MOVE_UNROLL + j
            for k in range(TOP_K):
                row_copy(t, pos_ref[(t0 + t) * TOP_K + k]).start()
        return carry

    lax.fori_loop(0, MOVE_TILE // MOVE_UNROLL, issue, 0)
    for k in range(TOP_K):
        pltpu.make_async_copy(xm_ref, xs_out.at[pl.ds(0, MOVE_TILE), :], sem).wait()


def dispatch_rows(pos_flat, xm, xs):
    n, d = xm.shape
    return pl.pallas_call(
        _dispatch_kernel,
        out_shape=jax.ShapeDtypeStruct(xs.shape, xs.dtype),
        grid_spec=pltpu.PrefetchScalarGridSpec(
            num_scalar_prefetch=1, grid=(n // MOVE_TILE,),
            in_specs=[pl.BlockSpec((MOVE_TILE, d), lambda i, pos: (i, 0)),
                      pl.BlockSpec(memory_space=pl.ANY)],
            out_specs=pl.BlockSpec(memory_space=pl.ANY),
            scratch_shapes=[pltpu.SemaphoreType.DMA]),
        input_output_aliases={2: 0},
        compiler_params=_params(("arbitrary",)),
        name="dispatch_rows",
    )(pos_flat, xm, xs)


def _ffn_kernel(te_ref, nu_ref, xs_ref, w1_ref, b1_ref, w2_ref, b2_ref, y_ref, w1b, w2b):
    i = pl.program_id(0)
    e = te_ref[i]
    prev = te_ref[jnp.maximum(i - 1, 0)]

    @pl.when((i == 0) | (e != prev))
    def _():
        for c in range(D_MODEL // LANES):
            w1b[c * LANES:(c + 1) * LANES, :] = w1_ref[0, c * LANES:(c + 1) * LANES, :].astype(BF16)
        for c in range(D_EXPERT // LANES):
            w2b[c * LANES:(c + 1) * LANES, :] = w2_ref[0, c * LANES:(c + 1) * LANES, :].astype(BF16)

    @pl.when(i < nu_ref[0])
    def _():
        x = xs_ref[...].astype(BF16)
        h = jnp.dot(x, w1b[...], preferred_element_type=F32) + b1_ref[0]
        g = jnp.minimum(h[:, :D_EXPERT], SWIGLU_LIMIT)
        lin = jnp.clip(h[:, D_EXPERT:], -SWIGLU_LIMIT, SWIGLU_LIMIT)
        a = (lin + 1.0) * (g * jax.nn.sigmoid(SWIGLU_ALPHA * g))
        y_ref[...] = jnp.dot(a.astype(BF16), w2b[...], preferred_element_type=F32) + b2_ref[0]

    @pl.when(i >= nu_ref[0])
    def _():
        y_ref[...] = jnp.zeros(y_ref.shape, F32)


def grouped_ffn(tile_expert, n_used, xs, w1, b1, w2, b2):
    r, d = xs.shape
    nt = r // FFN_TILE
    rows = lambda i, te, nu: (jnp.minimum(i, nu[0] - 1), 0)
    wsel = lambda i, te, nu: (te[i], 0, 0)
    return pl.pallas_call(
        _ffn_kernel,
        out_shape=jax.ShapeDtypeStruct((r, d), F32),
        grid_spec=pltpu.PrefetchScalarGridSpec(
            num_scalar_prefetch=2, grid=(nt,),
            in_specs=[pl.BlockSpec((FFN_TILE, d), rows),
                      pl.BlockSpec((1, d, 2 * D_EXPERT), wsel),
                      pl.BlockSpec((1, 1, 2 * D_EXPERT), wsel),
                      pl.BlockSpec((1, D_EXPERT, d), wsel),
                      pl.BlockSpec((1, 1, d), wsel)],
            out_specs=pl.BlockSpec((FFN_TILE, d), lambda i, te, nu: (i, 0)),
            scratch_shapes=[pltpu.VMEM((d, 2 * D_EXPERT), BF16), pltpu.VMEM((D_EXPERT, d), BF16)]),
        compiler_params=_params(("arbitrary",), VMEM_LIMIT),
        name="grouped_ffn",
    )(tile_expert, n_used, xs, w1, b1.reshape(N_EXPERTS, 1, -1), w2, b2.reshape(N_EXPERTS, 1, -1))


def _combine_kernel(pos_ref, x1_ref, wts_ref, m_ref, fn_ref, y_hbm, o_ref, buf, sem, *, final):
    t0 = pl.program_id(1) * MOVE_TILE + pl.program_id(0) * pl.num_programs(1) * MOVE_TILE

    def issue(c, carry):
        for j in range(MOVE_UNROLL):
            t = c * MOVE_UNROLL + j
            for k in range(TOP_K):
                p = pos_ref[(t0 + t) * TOP_K + k]
                pltpu.make_async_copy(y_hbm.at[pl.ds(p, 1), :], buf.at[k, pl.ds(t, 1), :], sem).start()
        return carry

    lax.fori_loop(0, MOVE_TILE // MOVE_UNROLL, issue, 0)
    for k in range(TOP_K):
        pltpu.make_async_copy(y_hbm.at[pl.ds(0, MOVE_TILE), :], buf.at[k], sem).wait()
    w = wts_ref[0]
    acc = w[:, 0:1] * buf[0]
    for k in range(1, TOP_K):
        acc = acc + w[:, k:k + 1] * buf[k]
    out = x1_ref[0] + m_ref[0][5:6] * acc
    if final:
        out = _rms(out, fn_ref[...])
    o_ref[0] = out


def combine_rows(pos_flat, x1, wts, mods, mod_off, fn, y, *, final):
    bm, sm, d = x1.shape
    row = lambda width: pl.BlockSpec((1, MOVE_TILE, width), lambda b, i, pos: (b, i, 0))
    return pl.pallas_call(
        functools.partial(_combine_kernel, final=final),
        out_shape=jax.ShapeDtypeStruct((bm, sm, d), F32),
        grid_spec=pltpu.PrefetchScalarGridSpec(
            num_scalar_prefetch=1, grid=(bm, sm // MOVE_TILE),
            in_specs=[row(d), row(LANES),
                      pl.BlockSpec((1, SUBLANES, d), lambda b, i, pos: (b + mod_off, 0, 0)),
                      pl.BlockSpec((1, d), lambda b, i, pos: (0, 0)),
                      pl.BlockSpec(memory_space=pl.ANY)],
            out_specs=row(d),
            scratch_shapes=[pltpu.VMEM((TOP_K, MOVE_TILE, d), F32), pltpu.SemaphoreType.DMA]),
        compiler_params=_params(("arbitrary", "arbitrary"), VMEM_LIMIT),
        name="combine_rows",
    )(pos_flat, x1, wts, mods, fn, y)


def _axial_angles(n_tokens, rot_dim):
    t = jnp.arange(n_tokens)
    rows = (t // GRID_W).astype(F32)
    cols = (t % GRID_W).astype(F32)
    n_freq = rot_dim // 4
    inv = ROPE_THETA ** (-jnp.arange(n_freq, dtype=F32) / n_freq)
    return jnp.concatenate([rows[:, None] * inv, cols[:, None] * inv], axis=-1)


def _lane_table(parts, n):
    cols = []
    for p in parts:
        cols.append(jnp.broadcast_to(jnp.asarray(p, F32), (n, p.shape[-1])) if hasattr(p, "shape") else p)
    return jnp.concatenate(cols, axis=-1)


def _swap_halves(w):
    half = w.shape[-1] // 2
    return jnp.concatenate([-w[..., half:], w[..., :half]], axis=-1)


def _prep_l0(w_in, q_norm, kv_norm, w_uq, w_uk, w_uv, w_out):
    d = w_in.shape[0]
    o_kr = MLA_Q_RANK + MLA_KV_RANK
    kr_cols = w_in[:, o_kr:o_kr + MLA_ROPE]
    win = jnp.concatenate(
        [w_in[:, :o_kr], kr_cols, _swap_halves(kr_cols), jnp.zeros((d, LANES - 2 * MLA_ROPE), F32),
         w_in[:, o_kr + MLA_ROPE:]], axis=1).astype(BF16)
    qk = MLA_NOPE + MLA_ROPE
    wuq3 = w_uq.reshape(MLA_Q_RANK, MLA_HEADS, qk)
    wuq = jnp.pad(wuq3, ((0, 0), (0, 0), (0, LANES - qk))).reshape(MLA_Q_RANK, -1).astype(BF16)
    wuqs3 = jnp.concatenate(
        [jnp.zeros((MLA_Q_RANK, MLA_HEADS, MLA_NOPE), F32), _swap_halves(wuq3[:, :, MLA_NOPE:]),
         jnp.zeros((MLA_Q_RANK, MLA_HEADS, LANES - qk), F32)], axis=-1)
    wuqs = wuqs3.reshape(MLA_Q_RANK, -1).astype(BF16)
    wuk3 = w_uk.reshape(MLA_KV_RANK, MLA_HEADS, MLA_NOPE)
    wuk = jnp.pad(wuk3, ((0, 0), (0, 0), (0, LANES - MLA_NOPE))).reshape(MLA_KV_RANK, -1).astype(BF16)
    wuv3 = w_uv.reshape(MLA_KV_RANK, MLA_HEADS, MLA_V)
    wuv = jnp.pad(wuv3, ((0, 0), (0, 0), (0, LANES - MLA_V))).reshape(MLA_KV_RANK, -1).astype(BF16)
    eye = jnp.eye(MLA_ROPE, dtype=F32)
    e_head = jnp.concatenate([jnp.zeros((MLA_ROPE, MLA_NOPE), F32), eye,
                              jnp.zeros((MLA_ROPE, LANES - qk), F32)], axis=1)
    e = jnp.pad(jnp.tile(e_head, (1, MLA_HEADS)), ((0, LANES - MLA_ROPE), (0, 0))).astype(BF16)
    wa3 = w_out[:MLA_HEADS * MLA_V].reshape(MLA_HEADS, MLA_V, d)
    wa = jnp.pad(wa3, ((0, 0), (0, LANES - MLA_V), (0, 0))).reshape(MLA_HEADS * LANES, d).astype(BF16)
    wc = w_out[MLA_HEADS * MLA_V:].astype(BF16)
    return dict(win=win, qg=q_norm.reshape(1, -1), kvg=kv_norm.reshape(1, -1), wuq=wuq, wuqs=wuqs,
                wuk=wuk, e=e, wuv=wuv, wa=wa, wc=wc)


def _l0_tables(n):
    ang = _axial_angles(n, MLA_ROPE)
    cos, sin = jnp.cos(ang), jnp.sin(ang)
    one = jnp.ones((n, 1), F32)
    zero = jnp.zeros((n, 1), F32)
    rest = LANES - MLA_NOPE - MLA_ROPE
    cq = jnp.concatenate([jnp.tile(one, (1, MLA_NOPE)), cos, cos, jnp.tile(one, (1, rest))], axis=1)
    sq = jnp.concatenate([jnp.tile(zero, (1, MLA_NOPE)), sin, sin, jnp.tile(zero, (1, rest))], axis=1)
    ck = jnp.concatenate([cos, cos, jnp.tile(zero, (1, LANES - MLA_ROPE))], axis=1)
    sk = jnp.concatenate([sin, sin, jnp.tile(zero, (1, LANES - MLA_ROPE))], axis=1)
    return cq, sq, ck, sk


def _pad_heads(w, n_heads, dim):
    d = w.shape[0]
    return jnp.pad(w.reshape(d, n_heads, dim), ((0, 0), (0, 0), (0, LANES - dim))).reshape(d, n_heads * LANES)


def _prep_l1(w_in, q_norm, k_norm, ln_g, ln_b, w_s, b_s, w_out):
    d = w_in.shape[0]
    qd = GQA_HEADS * GQA_HEAD_DIM
    kd = GQA_KV_HEADS * GQA_HEAD_DIM
    wq, wk, wv = w_in[:, :qd], w_in[:, qd:qd + kd], w_in[:, qd + kd:qd + 2 * kd]
    rest = w_in[:, qd + 2 * kd:]
    win = jnp.concatenate([_pad_heads(wq, GQA_HEADS, GQA_HEAD_DIM), _pad_heads(wk, GQA_KV_HEADS, GQA_HEAD_DIM),
                           _pad_heads(wv, GQA_KV_HEADS, GQA_HEAD_DIM), wv, rest], axis=1).astype(BF16)
    swq = _swap_halves(wq.reshape(d, GQA_HEADS, GQA_HEAD_DIM)).reshape(d, qd)
    swk = _swap_halves(wk.reshape(d, GQA_KV_HEADS, GQA_HEAD_DIM)).reshape(d, kd)
    wsw = jnp.concatenate([_pad_heads(swq, GQA_HEADS, GQA_HEAD_DIM),
                           _pad_heads(swk, GQA_KV_HEADS, GQA_HEAD_DIM)], axis=1).astype(BF16)
    half = GQA_HEAD_DIM // 2
    padg = lambda g: jnp.pad(g, (0, LANES - GQA_HEAD_DIM)).reshape(1, LANES)
    swapg = lambda g: jnp.concatenate([g[half:], g[:half]])
    wa3 = w_out[:qd].reshape(GQA_HEADS, GQA_HEAD_DIM, d)
    wa = jnp.pad(wa3, ((0, 0), (0, LANES - GQA_HEAD_DIM), (0, 0))).reshape(GQA_HEADS * LANES, d).astype(BF16)
    wc = w_out[qd:].astype(BF16)
    bs = jnp.repeat(b_s.T, LANES, axis=1)
    return dict(win=win, wsw=wsw, qg=padg(q_norm), qgs=padg(swapg(q_norm)), kg=padg(k_norm),
                kgs=padg(swapg(k_norm)), lng=ln_g.reshape(1, -1), lnb=ln_b.reshape(1, -1),
                ws=w_s.astype(BF16), bs=bs, wa=wa, wc=wc)


def _l1_tables(n):
    ang = _axial_angles(n, GQA_HEAD_DIM)
    cos, sin = jnp.cos(ang), jnp.sin(ang)
    pad = LANES - GQA_HEAD_DIM
    c = jnp.concatenate([cos, cos, jnp.ones((n, pad), F32)], axis=1)
    s = jnp.concatenate([sin, sin, jnp.zeros((n, pad), F32)], axis=1)
    return c, s


def _pad_lanes(x, width):
    return jnp.pad(x, [(0, 0)] * (x.ndim - 1) + [(0, width - x.shape[-1])])


def routed_ffn(groups, mods, n2, moe, tri, final_norm, *, final):
    router_w, router_b, w1, b1, w2, b2 = moe
    rw = _pad_lanes(router_w, LANES)
    rb = jnp.concatenate([router_b, jnp.full((LANES - N_EXPERTS,), NEG_BIG, F32)]).reshape(1, LANES)
    base = jnp.zeros((SUBLANES, LANES), F32)
    routed = []
    for g in groups:
        x1, xm, idx, wts, rank, base = post_mixer(
            g["attn"], g["other"], g["x"], mods, g["mod_off"], n2, g["wa"], g["wc"], rw, rb, tri, base)
        routed.append((x1, xm, idx, wts, rank))
    counts = base[0, :N_EXPERTS].astype(jnp.int32)
    tiles = (counts + FFN_TILE - 1) // FFN_TILE
    tile_end = jnp.cumsum(tiles)
    offsets = (tile_end - tiles) * FFN_TILE
    n_pairs = sum(r[0].shape[0] * r[0].shape[1] for r in routed) * TOP_K
    nt = n_pairs // FFN_TILE + N_EXPERTS
    n_used = tile_end[-1:]
    tile_expert = jnp.sum(jnp.arange(nt)[:, None] >= tile_end[None, :], axis=1)
    tile_expert = jnp.minimum(tile_expert, tile_expert[n_used[0] - 1]).astype(jnp.int32)

    xs = jnp.zeros((nt * FFN_TILE, D_MODEL), F32)
    pos_flats = []
    for (x1, xm, idx, wts, rank) in routed:
        n = idx.shape[0] * idx.shape[1]
        pos = plan_positions(offsets, idx.reshape(n, LANES), rank.reshape(n, LANES))
        pos_flat = pos[:, :TOP_K].reshape(-1)
        pos_flats.append(pos_flat)
        xs = dispatch_rows(pos_flat, xm.reshape(n, D_MODEL), xs)
    y = grouped_ffn(tile_expert, n_used.astype(jnp.int32), xs, w1, b1, w2, b2)
    outs = []
    for g, (x1, xm, idx, wts, rank), pos_flat in zip(groups, routed, pos_flats):
        outs.append(combine_rows(pos_flat, x1, wts, mods, g["mod_off"], final_norm, y, final=final))
    return outs


def kernel(x_prompt, x_sample, cache_l0_ckv, cache_l0_krope, cache_l1_k, cache_l1_v, c, c_ctx,
           l0_ada_w, l0_ada_b, l0_norm1, l0_w_in, l0_q_norm, l0_kv_norm, l0_w_uq, l0_w_uk, l0_w_uv,
           l0_conv_w, l0_conv_b, l0_conv_ln_g, l0_conv_ln_b, l0_w_out, l0_norm2,
           l0_router_w, l0_router_b, l0_w1, l0_b1, l0_w2, l0_b2,
           l1_ada_w, l1_ada_b, l1_norm1, l1_w_in, l1_q_norm, l1_k_norm, l1_gmlp_ln_g, l1_gmlp_ln_b,
           l1_w_s, l1_b_s, l1_w_out, l1_norm2,
           l1_router_w, l1_router_b, l1_w1, l1_b1, l1_w2, l1_b2,
           final_norm):
    bp, sp, d = x_prompt.shape
    bs, ss, _ = x_sample.shape
    past = cache_l0_ckv.shape[1]
    n_p = bp * sp

    cond8 = jnp.concatenate([c_ctx[None], c, jnp.zeros((SUBLANES - 1 - bs, d), F32)], axis=0)
    mods0 = adaln(cond8, l0_ada_w, l0_ada_b)
    mods1 = adaln(cond8, l1_ada_w, l1_ada_b)
    tri = jnp.tril(jnp.ones((ROW_TILE, ROW_TILE), F32), -1).astype(BF16)
    fn = final_norm.reshape(1, d)

    w0 = _prep_l0(l0_w_in, l0_q_norm, l0_kv_norm, l0_w_uq, l0_w_uk, l0_w_uv, l0_w_out)
    n1 = l0_norm1.reshape(1, d)
    hp = x_prompt.reshape(1, n_p, d)
    q_p, k_p, v_p, ckv_p, kr_p, u_p = l0_inproj(hp, mods0, 0, n1, w0, None)
    q_s, k_s, v_s, _, _, u_s = l0_inproj(x_sample, mods0, 1, n1, w0, _l0_tables(ss))
    k_c, v_c = mla_ctx_kv(cache_l0_ckv, _pad_lanes(cache_l0_krope, LANES), w0)
    hw = MLA_HEADS * LANES
    att_p = attention(q_p.reshape(bp, sp, hw), k_p.reshape(bp, sp, hw), v_p.reshape(bp, sp, hw),
                      n_heads=MLA_HEADS, n_kv=MLA_HEADS, scale=MLA_SCALE, heads_per_step=MLA_HEADS)
    att_s = attention(q_s, jnp.concatenate([k_c, k_s], axis=1), jnp.concatenate([v_c, v_s], axis=1),
                      n_heads=MLA_HEADS, n_kv=MLA_HEADS, scale=MLA_SCALE, heads_per_step=1)
    conv_p = conformer_conv(u_p.reshape(bp, sp, CONV_CH), l0_conv_w, l0_conv_b, l0_conv_ln_g, l0_conv_ln_b)
    conv_s = conformer_conv(u_s, l0_conv_w, l0_conv_b, l0_conv_ln_g, l0_conv_ln_b)
    groups = [dict(attn=att_p.reshape(1, n_p, hw), other=conv_p.reshape(1, n_p, CONV_CH), x=hp, mod_off=0,
                   wa=w0["wa"], wc=w0["wc"]),
              dict(attn=att_s, other=conv_s, x=x_sample, mod_off=1, wa=w0["wa"], wc=w0["wc"])]
    hp, hs = routed_ffn(groups, mods0, l0_norm2.reshape(1, d),
                        (l0_router_w, l0_router_b, l0_w1, l0_b1, l0_w2, l0_b2), tri, fn, final=False)
    new_l0_ckv = ckv_p.reshape(bp, sp, MLA_KV_RANK)
    new_l0_krope = kr_p.reshape(bp, sp, MLA_ROPE)

    w1p = _prep_l1(l1_w_in, l1_q_norm, l1_k_norm, l1_gmlp_ln_g, l1_gmlp_ln_b, l1_w_s, l1_b_s, l1_w_out)
    n1 = l1_norm1.reshape(1, d)
    q_p, k_p, kn_p, vp_p, vraw_p, gat_p = l1_inproj(hp, mods1, 0, n1, w1p, None)
    q_s, k_s, _, vp_s, _, gat_s = l1_inproj(hs, mods1, 1, n1, w1p, _l1_tables(ss))
    qw = GQA_HEADS * LANES
    kw = GQA_KV_HEADS * LANES
    pad_kv = lambda t: _pad_lanes(t, LANES).reshape(bs, past, kw).astype(BF16)
    att_p = attention(q_p.reshape(bp, sp, qw), k_p.reshape(bp, sp, kw), vp_p.reshape(bp, sp, kw),
                      n_heads=GQA_HEADS, n_kv=GQA_KV_HEADS, scale=GQA_SCALE, heads_per_step=GQA_HEADS)
    att_s = attention(q_s, jnp.concatenate([pad_kv(cache_l1_k), k_s], axis=1),
                      jnp.concatenate([pad_kv(cache_l1_v), vp_s], axis=1),
                      n_heads=GQA_HEADS, n_kv=GQA_KV_HEADS, scale=GQA_SCALE, heads_per_step=1)
    groups = [dict(attn=att_p.reshape(1, n_p, qw), other=gat_p, x=hp, mod_off=0, wa=w1p["wa"], wc=w1p["wc"]),
              dict(attn=att_s, other=gat_s, x=hs, mod_off=1, wa=w1p["wa"], wc=w1p["wc"])]
    yp, ys = routed_ffn(groups, mods1, l1_norm2.reshape(1, d),
                        (l1_router_w, l1_router_b, l1_w1, l1_b1, l1_w2, l1_b2), tri, fn, final=True)
    new_l1_k = kn_p.reshape(bp, sp, GQA_KV_HEADS, LANES)[..., :GQA_HEAD_DIM]
    new_l1_v = vraw_p.reshape(bp, sp, GQA_KV_HEADS, GQA_HEAD_DIM)
    return (yp.reshape(bp, sp, d), ys, new_l0_ckv, new_l0_krope, new_l1_k, new_l1_v)
```

```python
import functools
import math

import jax
import jax.numpy as jnp
from jax import lax
from jax.experimental import pallas as pl
from jax.experimental.pallas import tpu as pltpu

F32 = jnp.float32
BF16 = jnp.bfloat16
HIGHEST = lax.Precision.HIGHEST

LANES = 128
SUBLANES = 8
VMEM_LIMIT = 56 * 1024 * 1024

D_MODEL = 1024
GRID_W = 64
ROPE_THETA = 10000.0
EPS = 1e-6
N_MOD = 6

MLA_HEADS = 8
MLA_NOPE = 64
MLA_ROPE = 32
MLA_V = 64
MLA_Q_RANK = 384
MLA_KV_RANK = 256
MLA_SCALE = 1.0 / math.sqrt(MLA_NOPE + MLA_ROPE)
CONV_CH = 512
CONV_WIDTH = 31
CONV_HALO = 16

GQA_HEADS = 8
GQA_KV_HEADS = 2
GQA_HEAD_DIM = 64
GQA_SCALE = 1.0 / math.sqrt(GQA_HEAD_DIM)
CHUNK = 128
GMLP_GROUPS = 4
GMLP_CH = 512

N_EXPERTS = 32
TOP_K = 4
D_EXPERT = 1024
SWIGLU_LIMIT = 7.0
SWIGLU_ALPHA = 1.702

ROW_TILE = 512
FFN_TILE = 256
INVERT_PARTS = 32
MOVE_UNROLL = 8
ATT_Q_TILE = 256
NEG_BIG = -1e30


def _params(sem, vmem=None):
    return pltpu.CompilerParams(dimension_semantics=sem, vmem_limit_bytes=vmem)


def _rms(x, g):
    return x * lax.rsqrt(jnp.mean(x * x, axis=-1, keepdims=True) + EPS) * g


def _const_spec(shape):
    nd = len(shape)
    return pl.BlockSpec(shape, lambda *_: (0,) * nd)


def _adaln_kernel(c_ref, w_ref, b_ref, o_ref):
    c = c_ref[...]
    s = c * jax.nn.sigmoid(c)
    o_ref[...] = jnp.dot(s, w_ref[...], preferred_element_type=F32, precision=HIGHEST) + b_ref[...]


def adaln(cond8, ada_w, ada_b):
    d, n = ada_w.shape
    bn = n // 4
    m = pl.pallas_call(
        _adaln_kernel,
        out_shape=jax.ShapeDtypeStruct((SUBLANES, n), F32),
        grid=(n // bn,),
        in_specs=[_const_spec((SUBLANES, d)),
                  pl.BlockSpec((d, bn), lambda j: (0, j)),
                  pl.BlockSpec((1, bn), lambda j: (0, j))],
        out_specs=pl.BlockSpec((SUBLANES, bn), lambda j: (0, j)),
        compiler_params=_params(("arbitrary",), VMEM_LIMIT),
        name="adaln",
    )(cond8, ada_w, ada_b.reshape(1, n))
    m = m.reshape(SUBLANES, N_MOD, d)
    return jnp.pad(m, ((0, 0), (0, SUBLANES - N_MOD), (0, 0)))


def _l0_inproj_kernel(*refs, rope):
    if rope:
        (x_ref, m_ref, n1_ref, win_ref, qg_ref, kvg_ref, wuq_ref, wuqs_ref, wuk_ref, e_ref, wuv_ref,
         cq_ref, sq_ref, ck_ref, sk_ref, q_out, k_out, v_out, ckv_out, kr_out, u_out) = refs
    else:
        (x_ref, m_ref, n1_ref, win_ref, qg_ref, kvg_ref, wuq_ref, wuk_ref, e_ref, wuv_ref,
         q_out, k_out, v_out, ckv_out, kr_out, u_out) = refs
    x = x_ref[0]
    m = m_ref[0]
    h = _rms(x, n1_ref[...]) * (1.0 + m[1:2]) + m[0:1]
    z = jnp.dot(h.astype(BF16), win_ref[...], preferred_element_type=F32)
    c_q = z[:, 0:MLA_Q_RANK]
    c_kv = z[:, MLA_Q_RANK:MLA_Q_RANK + MLA_KV_RANK]
    kr_blk = z[:, 640:768]
    val = z[:, 768:768 + CONV_CH]
    gate = z[:, 768 + CONV_CH:768 + 2 * CONV_CH]

    cqn = _rms(c_q, qg_ref[...]).astype(BF16)
    q = jnp.dot(cqn, wuq_ref[...], preferred_element_type=F32)
    if rope:
        qs = jnp.dot(cqn, wuqs_ref[...], preferred_element_type=F32)
        cq = cq_ref[...]
        sq = sq_ref[...]
        for hd in range(MLA_HEADS):
            sl = slice(hd * LANES, (hd + 1) * LANES)
            q_out[0, :, sl] = (q[:, sl] * cq + qs[:, sl] * sq).astype(BF16)
        kr = kr_blk * ck_ref[...] + pltpu.roll(kr_blk, LANES - MLA_ROPE, 1) * sk_ref[...]
    else:
        q_out[0] = q.astype(BF16)
        kr = kr_blk

    ckv = _rms(c_kv, kvg_ref[...])
    ckv_out[0] = ckv
    kr_out[0] = kr_blk[:, 0:MLA_ROPE]
    ckv_b = ckv.astype(BF16)
    k = (jnp.dot(ckv_b, wuk_ref[...], preferred_element_type=F32)
         + jnp.dot(kr.astype(BF16), e_ref[...], preferred_element_type=F32))
    k_out[0] = k.astype(BF16)
    v_out[0] = jnp.dot(ckv_b, wuv_ref[...], preferred_element_type=F32).astype(BF16)
    u_out[0] = val * jax.nn.sigmoid(gate)


def l0_inproj(x, mods, mod_off, n1, w, tables):
    bm, sm, d = x.shape
    tr = min(ROW_TILE, sm)
    rope = tables is not None
    hp = MLA_HEADS * LANES
    row = lambda width: pl.BlockSpec((1, tr, width), lambda b, i: (b, i, 0))
    in_specs = [row(d),
                pl.BlockSpec((1, SUBLANES, d), lambda b, i: (b + mod_off, 0, 0)),
                _const_spec((1, d)), _const_spec(w["win"].shape),
                _const_spec((1, MLA_Q_RANK)), _const_spec((1, MLA_KV_RANK)),
                _const_spec(w["wuq"].shape)]
    args = [x, mods, n1, w["win"], w["qg"], w["kvg"], w["wuq"]]
    if rope:
        in_specs.append(_const_spec(w["wuqs"].shape))
        args.append(w["wuqs"])
    in_specs += [_const_spec(w["wuk"].shape), _const_spec(w["e"].shape), _const_spec(w["wuv"].shape)]
    args += [w["wuk"], w["e"], w["wuv"]]
    if rope:
        in_specs += [pl.BlockSpec((tr, LANES), lambda b, i: (i, 0))] * 4
        args += list(tables)
    out_shape = [jax.ShapeDtypeStruct((bm, sm, hp), BF16),
                 jax.ShapeDtypeStruct((bm, sm, hp), BF16),
                 jax.ShapeDtypeStruct((bm, sm, hp), BF16),
                 jax.ShapeDtypeStruct((bm, sm, MLA_KV_RANK), F32),
                 jax.ShapeDtypeStruct((bm, sm, MLA_ROPE), F32),
                 jax.ShapeDtypeStruct((bm, sm, CONV_CH), F32)]
    out_specs = [row(hp), row(hp), row(hp), row(MLA_KV_RANK), row(MLA_ROPE), row(CONV_CH)]
    return pl.pallas_call(
        functools.partial(_l0_inproj_kernel, rope=rope),
        out_shape=out_shape, grid=(bm, sm // tr), in_specs=in_specs, out_specs=out_specs,
        compiler_params=_params(("parallel", "parallel"), VMEM_LIMIT),
        name="l0_inproj_rope" if rope else "l0_inproj",
    )(*args)


def _mla_ctx_kv_kernel(ckv_ref, kr_ref, wuk_ref, e_ref, wuv_ref, k_out, v_out):
    ckv_b = ckv_ref[0].astype(BF16)
    k = (jnp.dot(ckv_b, wuk_ref[...], preferred_element_type=F32)
         + jnp.dot(kr_ref[0].astype(BF16), e_ref[...], preferred_element_type=F32))
    k_out[0] = k.astype(BF16)
    v_out[0] = jnp.dot(ckv_b, wuv_ref[...], preferred_element_type=F32).astype(BF16)


def mla_ctx_kv(ckv, kr128, w):
    b, s, _ = ckv.shape
    hp = MLA_HEADS * LANES
    blk = lambda width: pl.BlockSpec((1, s, width), lambda i: (i, 0, 0))
    return pl.pallas_call(
        _mla_ctx_kv_kernel,
        out_shape=[jax.ShapeDtypeStruct((b, s, hp), BF16)] * 2,
        grid=(b,),
        in_specs=[blk(MLA_KV_RANK), blk(LANES), _const_spec(w["wuk"].shape),
                  _const_spec(w["e"].shape), _const_spec(w["wuv"].shape)],
        out_specs=[blk(hp), blk(hp)],
        compiler_params=_params(("parallel",), VMEM_LIMIT),
        name="mla_ctx_kv",
    )(ckv, kr128, w["wuk"], w["e"], w["wuv"])


def _conv_kernel(prev_ref, cur_ref, next_ref, w_ref, b_ref, g_ref, beta_ref, o_ref, pad_ref, *, rb):
    i = pl.program_id(1)
    last = pl.num_programs(1) - 1
    zeros = jnp.zeros((CONV_HALO, CONV_CH), F32)
    pad_ref[0:CONV_HALO, :] = jnp.where(i == 0, zeros, prev_ref[0])
    pad_ref[CONV_HALO:CONV_HALO + rb, :] = cur_ref[0]
    pad_ref[CONV_HALO + rb:CONV_HALO + rb + CONV_HALO, :] = jnp.where(i == last, zeros, next_ref[0])
    w = w_ref[...]
    shift = CONV_HALO - CONV_WIDTH // 2
    acc = jnp.zeros((rb, CONV_CH), F32) + b_ref[...]
    for k in range(CONV_WIDTH):
        acc = acc + pad_ref[k + shift:k + shift + rb, :] * w[k:k + 1, :]
    mu = jnp.mean(acc, axis=-1, keepdims=True)
    cen = acc - mu
    var = jnp.mean(cen * cen, axis=-1, keepdims=True)
    y = cen * lax.rsqrt(var + EPS) * g_ref[...] + beta_ref[...]
    o_ref[0] = (y * jax.nn.sigmoid(y)).astype(BF16)


def conformer_conv(u, conv_w, conv_b, ln_g, ln_b):
    b, s, c = u.shape
    rb = min(256, s)
    nh = rb // CONV_HALO
    n_halo_blocks = s // CONV_HALO
    wpad = jnp.pad(conv_w.reshape(CONV_WIDTH, c), ((0, 32 - CONV_WIDTH), (0, 0)))
    return pl.pallas_call(
        functools.partial(_conv_kernel, rb=rb),
        out_shape=jax.ShapeDtypeStruct((b, s, c), BF16),
        grid=(b, s // rb),
        in_specs=[pl.BlockSpec((1, CONV_HALO, c), lambda bi, i: (bi, jnp.maximum(i * nh - 1, 0), 0)),
                  pl.BlockSpec((1, rb, c), lambda bi, i: (bi, i, 0)),
                  pl.BlockSpec((1, CONV_HALO, c),
                               lambda bi, i: (bi, jnp.minimum((i + 1) * nh, n_halo_blocks - 1), 0)),
                  _const_spec((32, c)), _const_spec((1, c)), _const_spec((1, c)), _const_spec((1, c))],
        out_specs=pl.BlockSpec((1, rb, c), lambda bi, i: (bi, i, 0)),
        scratch_shapes=[pltpu.VMEM((rb + 2 * CONV_HALO, c), F32)],
        compiler_params=_params(("parallel", "parallel"), VMEM_LIMIT),
        name="conformer_conv",
    )(u, u, u, wpad, conv_b.reshape(1, c), ln_g.reshape(1, c), ln_b.reshape(1, c))


def _attn_kernel(q_ref, k_ref, v_ref, o_ref, *, heads, rep, scale):
    for hd in range(heads):
        g = hd // rep
        q = q_ref[0, :, hd * LANES:(hd + 1) * LANES]
        k = k_ref[0, :, g * LANES:(g + 1) * LANES]
        s = lax.dot_general(q, k, (((1,), (1,)), ((), ())), preferred_element_type=F32) * scale
        m = jnp.max(s, axis=-1, keepdims=True)
        p = jnp.exp(s - m)
        l = jnp.sum(p, axis=-1, keepdims=True)
        o = jnp.dot(p.astype(BF16), v_ref[0, :, g * LANES:(g + 1) * LANES], preferred_element_type=F32)
        o_ref[0, :, hd * LANES:(hd + 1) * LANES] = (o / l).astype(BF16)


def attention(q, k, v, *, n_heads, n_kv, scale, heads_per_step):
    b, sq, _ = q.shape
    sk = k.shape[1]
    rep = n_heads // n_kv
    tq = min(ATT_Q_TILE, sq)
    if heads_per_step == n_heads:
        grid = (b, 1, sq // tq)
        kv_spec = pl.BlockSpec((1, sk, n_kv * LANES), lambda bi, h, i: (bi, 0, 0))
        kern = functools.partial(_attn_kernel, heads=n_heads, rep=rep, scale=scale)
    else:
        assert heads_per_step == 1
        grid = (b, n_heads, sq // tq)
        kv_spec = pl.BlockSpec((1, sk, LANES), lambda bi, h, i: (bi, 0, h // rep))
        kern = functools.partial(_attn_kernel, heads=1, rep=1, scale=scale)
    q_spec = pl.BlockSpec((1, tq, heads_per_step * LANES), lambda bi, h, i: (bi, i, h))
    return pl.pallas_call(
        kern,
        out_shape=jax.ShapeDtypeStruct(q.shape, BF16),
        grid=grid, in_specs=[q_spec, kv_spec, kv_spec], out_specs=q_spec,
        compiler_params=_params(("parallel", "parallel", "parallel"), VMEM_LIMIT),
        name="attention",
    )(q, k, v)


def _l1_inproj_kernel(*refs, rope):
    if rope:
        (x_ref, m_ref, n1_ref, win_ref, wsw_ref, qg_ref, qgs_ref, kg_ref, kgs_ref, lng_ref, lnb_ref,
         ws_ref, bs_ref, c_ref, s_ref, q_out, k_out, kn_out, vp_out, v_out, g_out) = refs
    else:
        (x_ref, m_ref, n1_ref, win_ref, qg_ref, kg_ref, lng_ref, lnb_ref,
         ws_ref, bs_ref, q_out, k_out, kn_out, vp_out, v_out, g_out) = refs
    x = x_ref[0]
    m = m_ref[0]
    hb = (_rms(x, n1_ref[...]) * (1.0 + m[1:2]) + m[0:1]).astype(BF16)
    z = jnp.dot(hb, win_ref[...], preferred_element_type=F32)
    qw = GQA_HEADS * LANES
    kw = GQA_KV_HEADS * LANES
    o_k, o_vp, o_v, o_u, o_vg = qw, qw + kw, qw + 2 * kw, qw + 2 * kw + LANES, qw + 2 * kw + LANES + GMLP_CH
    if rope:
        zs = jnp.dot(hb, wsw_ref[...], preferred_element_type=F32)
        cos = c_ref[...]
        sin = s_ref[...]

    def head(col, zcol, g_ref, gs_ref):
        t = z[:, col:col + LANES]
        r = lax.rsqrt(jnp.sum(t * t, axis=-1, keepdims=True) * (1.0 / GQA_HEAD_DIM) + EPS)
        normed = t * r * g_ref[...]
        if not rope:
            return normed, normed
        ts = zs[:, zcol:zcol + LANES]
        return normed, normed * cos + ts * r * gs_ref[...] * sin

    for hd in range(GQA_HEADS):
        _, rot = head(hd * LANES, hd * LANES, qg_ref, qgs_ref if rope else None)
        q_out[0, :, hd * LANES:(hd + 1) * LANES] = rot.astype(BF16)
    for hd in range(GQA_KV_HEADS):
        normed, rot = head(o_k + hd * LANES, qw + hd * LANES, kg_ref, kgs_ref if rope else None)
        k_out[0, :, hd * LANES:(hd + 1) * LANES] = rot.astype(BF16)
        kn_out[0, :, hd * LANES:(hd + 1) * LANES] = normed
    vp_out[0] = z[:, o_vp:o_vp + kw].astype(BF16)
    v_out[0] = z[:, o_v:o_v + LANES]

    u = z[:, o_u:o_u + GMLP_CH]
    vg = z[:, o_vg:o_vg + GMLP_CH]
    mu = jnp.mean(vg, axis=-1, keepdims=True)
    cen = vg - mu
    var = jnp.mean(cen * cen, axis=-1, keepdims=True)
    vn = (cen * lax.rsqrt(var + EPS) * lng_ref[...] + lnb_ref[...]).astype(BF16)
    bias = bs_ref[...]
    rows = x.shape[0]
    for cidx in range(rows // CHUNK):
        r0 = cidx * CHUNK
        for g in range(GMLP_GROUPS):
            c0 = g * LANES
            mixed = jnp.dot(ws_ref[g], vn[r0:r0 + CHUNK, c0:c0 + LANES], preferred_element_type=F32)
            g_out[0, r0:r0 + CHUNK, c0:c0 + LANES] = (
                u[r0:r0 + CHUNK, c0:c0 + LANES] * (mixed + bias[:, c0:c0 + LANES])).astype(BF16)


def l1_inproj(x, mods, mod_off, n1, w, tables):
    bm, sm, d = x.shape
    tr = min(ROW_TILE, sm)
    rope = tables is not None
    qw = GQA_HEADS * LANES
    kw = GQA_KV_HEADS * LANES
    row = lambda width: pl.BlockSpec((1, tr, width), lambda b, i: (b, i, 0))
    vec = _const_spec((1, LANES))
    in_specs = [row(d), pl.BlockSpec((1, SUBLANES, d), lambda b, i: (b + mod_off, 0, 0)),
                _const_spec((1, d)), _const_spec(w["win"].shape)]
    args = [x, mods, n1, w["win"]]
    if rope:
        in_specs += [_const_spec(w["wsw"].shape), vec, vec, vec, vec]
        args += [w["wsw"], w["qg"], w["qgs"], w["kg"], w["kgs"]]
    else:
        in_specs += [vec, vec]
        args += [w["qg"], w["kg"]]
    in_specs += [_const_spec((1, GMLP_CH)), _const_spec((1, GMLP_CH)),
                 _const_spec(w["ws"].shape), _const_spec((CHUNK, GMLP_CH))]
    args += [w["lng"], w["lnb"], w["ws"], w["bs"]]
    if rope:
        in_specs += [pl.BlockSpec((tr, LANES), lambda b, i: (i, 0))] * 2
        args += list(tables)
    out_shape = [jax.ShapeDtypeStruct((bm, sm, qw), BF16),
                 jax.ShapeDtypeStruct((bm, sm, kw), BF16),
                 jax.ShapeDtypeStruct((bm, sm, kw), F32),
                 jax.ShapeDtypeStruct((bm, sm, kw), BF16),
                 jax.ShapeDtypeStruct((bm, sm, LANES), F32),
                 jax.ShapeDtypeStruct((bm, sm, GMLP_CH), BF16)]
    out_specs = [row(qw), row(kw), row(kw), row(kw), row(LANES), row(GMLP_CH)]
    return pl.pallas_call(
        functools.partial(_l1_inproj_kernel, rope=rope),
        out_shape=out_shape, grid=(bm, sm // tr), in_specs=in_specs, out_specs=out_specs,
        compiler_params=_params(("parallel", "parallel"), VMEM_LIMIT),
        name="l1_inproj_rope" if rope else "l1_inproj",
    )(*args)


def _post_kernel(a_ref, c_ref, x_ref, m_ref, n2_ref, wa_ref, wc_ref, rw_ref, rb_ref, tri_ref, base_ref,
                 xm_all_in, x1_out, xm_out, idx_out, wts_out, rank_out, cnt_out, run_ref):
    del xm_all_in
    first =(pl.program_id(0) == 0) & (pl.program_id(1) == 0)

    @pl.when(first)
    def _():
        run_ref[...] = base_ref[...]

    m = m_ref[0]
    y = (jnp.dot(a_ref[0], wa_ref[...], preferred_element_type=F32)
         + jnp.dot(c_ref[0], wc_ref[...], preferred_element_type=F32))
    x1 = x_ref[0] + m[2:3] * y
    x1_out[0] = x1
    xm = _rms(x1, n2_ref[...]) * (1.0 + m[4:5]) + m[3:4]
    xm_out[...] = xm

    logits = jnp.dot(xm, rw_ref[...], preferred_element_type=F32, precision=HIGHEST) + rb_ref[...]
    rows = logits.shape[0]
    lane = lax.broadcasted_iota(jnp.int32, (rows, LANES), 1).astype(F32)
    work = logits
    vals, hots = [], []
    idx_acc = jnp.zeros((rows, LANES), F32)
    for k in range(TOP_K):
        top = jnp.max(work, axis=-1, keepdims=True)
        sel = jnp.min(jnp.where(work == top, lane, float(LANES)), axis=-1, keepdims=True)
        hot = lane == sel
        vals.append(top)
        hots.append(hot)
        idx_acc = idx_acc + jnp.where(lane == float(k), sel, 0.0)
        work = jnp.where(hot, -jnp.inf, work)
    exps = [jnp.exp(v - vals[0]) for v in vals]
    denom = exps[0] + exps[1] + exps[2] + exps[3]
    wts = jnp.zeros((rows, LANES), F32)
    for k in range(TOP_K):
        wts = wts + jnp.where(lane == float(k), exps[k] / denom, 0.0)

    chosen = jnp.zeros((rows, LANES), F32)
    for hot in hots:
        chosen = chosen + hot.astype(F32)
    before = jnp.dot(tri_ref[...], chosen.astype(BF16), preferred_element_type=F32) + run_ref[0:1, :]
    rank = jnp.zeros((rows, LANES), F32)
    for k in range(TOP_K):
        rk = jnp.sum(jnp.where(hots[k], before, 0.0), axis=-1, keepdims=True)
        rank = rank + jnp.where(lane == float(k), rk, 0.0)
    run_ref[0:1, :] = run_ref[0:1, :] + jnp.sum(chosen, axis=0, keepdims=True)
    idx_out[0] = idx_acc.astype(jnp.int32)
    wts_out[0] = wts
    rank_out[0] = rank.astype(jnp.int32)
    cnt_out[...] = run_ref[...]


def post_mixer(attn, other, x, mods, mod_off, n2, wa, wc, rw, rb, tri, base, xm_all, row_off, n_total):
    bm, sm, d = x.shape
    tr = tri.shape[0]
    nb = sm // tr
    blk_off = row_off // tr
    row = lambda width: pl.BlockSpec((1, tr, width), lambda b, i: (b, i, 0))
    out_shape = [jax.ShapeDtypeStruct((bm, sm, d), F32),
                 jax.ShapeDtypeStruct((n_total, d), F32),
                 jax.ShapeDtypeStruct((bm, sm, LANES), jnp.int32),
                 jax.ShapeDtypeStruct((bm, sm, LANES), F32),
                 jax.ShapeDtypeStruct((bm, sm, LANES), jnp.int32),
                 jax.ShapeDtypeStruct((SUBLANES, LANES), F32)]
    in_specs = [row(attn.shape[-1]), row(other.shape[-1]), row(d),
                pl.BlockSpec((1, SUBLANES, d), lambda b, i: (b + mod_off, 0, 0)),
                _const_spec((1, d)), _const_spec(wa.shape), _const_spec(wc.shape),
                _const_spec(rw.shape), _const_spec((1, LANES)), _const_spec(tri.shape),
                _const_spec((SUBLANES, LANES)), pl.BlockSpec(memory_space=pl.ANY)]
    args = [attn, other, x, mods, n2, wa, wc, rw, rb, tri, base, xm_all]
    return pl.pallas_call(
        _post_kernel,
        out_shape=out_shape, grid=(bm, nb),
        in_specs=in_specs,
        out_specs=[row(d), pl.BlockSpec((tr, d), lambda b, i: (blk_off + b * nb + i, 0)),
                   row(LANES), row(LANES), row(LANES), _const_spec((SUBLANES, LANES))],
        scratch_shapes=[pltpu.VMEM((SUBLANES, LANES), F32)],
        input_output_aliases={11: 1},
        compiler_params=_params(("arbitrary", "arbitrary"), VMEM_LIMIT),
        name="post_mixer_route",
    )(*args)


def _plan_kernel(off_ref, idx_ref, rank_ref, pos_out):
    idx = idx_ref[...]
    pos = rank_ref[...]
    for e in range(N_EXPERTS):
        pos = pos + jnp.where(idx == e, off_ref[e], 0)
    pos_out[...] = pos


def plan_positions(offsets, idx, rank):
    n = idx.shape[0]
    tr = min(2048, n)
    spec = pl.BlockSpec((tr, LANES), lambda i, off: (i, 0))
    return pl.pallas_call(
        _plan_kernel,
        out_shape=jax.ShapeDtypeStruct((n, LANES), jnp.int32),
        grid_spec=pltpu.PrefetchScalarGridSpec(
            num_scalar_prefetch=1, grid=(n // tr,), in_specs=[spec, spec], out_specs=spec),
        compiler_params=_params(("parallel",)),
        name="plan_positions",
    )(offsets, idx, rank)


def _invert_kernel(pos_ref, inv_ref, *, n_tokens, n_rows):
    n_pairs = TOP_K * n_tokens
    shift = n_tokens.bit_length() - 1
    phase = pl.program_id(0)
    part = pl.program_id(1)
    rows_per = n_rows // INVERT_PARTS
    toks_per = n_tokens // INVERT_PARTS

    @pl.when(phase == 0)
    def _():
        def init(c, carry):
            for j in range(MOVE_UNROLL):
                r = part * rows_per + c * MOVE_UNROLL + j
                inv_ref[r] = n_pairs + (r & (2 * FFN_TILE - 1))
            return carry
        lax.fori_loop(0, rows_per // MOVE_UNROLL, init, 0)

    @pl.when(phase == 1)
    def _():
        def place(c, carry):
            for j in range(MOVE_UNROLL):
                t = part * toks_per + c * MOVE_UNROLL + j
                for k in range(TOP_K):
                    inv_ref[pos_ref[t * TOP_K + k]] = (k << shift) + t
            return carry
        lax.fori_loop(0, toks_per // MOVE_UNROLL, place, 0)


def invert_positions(pos_flat, n_rows):
    n_tokens = pos_flat.shape[0] // TOP_K
    assert n_tokens & (n_tokens - 1) == 0
    assert n_rows % (INVERT_PARTS * MOVE_UNROLL) == 0 and n_tokens % (INVERT_PARTS * MOVE_UNROLL) == 0
    return pl.pallas_call(
        functools.partial(_invert_kernel, n_tokens=n_tokens, n_rows=n_rows),
        out_shape=jax.ShapeDtypeStruct((n_rows,), jnp.int32),
        grid_spec=pltpu.PrefetchScalarGridSpec(
            num_scalar_prefetch=1, grid=(2, INVERT_PARTS), in_specs=[],
            out_specs=pl.BlockSpec(memory_space=pltpu.SMEM)),
        compiler_params=_params(("arbitrary", "arbitrary")),
        name="invert_positions",
    )(pos_flat)


def _ffn_kernel(te_ref, nu_ref, inv_ref, xm_hbm, w1_ref, b1_ref, w2_ref, b2_ref, y_hbm,
                xbuf, ybuf, w1b, w2b, gsem, ssem, *, n_tokens):
    i = pl.program_id(0)
    nu = nu_ref[0]
    slot = i % 2
    n_pairs = TOP_K * n_tokens

    def start_gather(tile, s):
        def body(c, carry):
            for j in range(MOVE_UNROLL):
                m = c * MOVE_UNROLL + j
                tok = inv_ref[tile * FFN_TILE + m] & (n_tokens - 1)
                pltpu.make_async_copy(xm_hbm.at[pl.ds(tok, 1), :], xbuf.at[s, pl.ds(m, 1), :], gsem.at[s]).start()
            return carry
        lax.fori_loop(0, FFN_TILE // MOVE_UNROLL, body, 0)

    def wait_gather(s):
        pltpu.make_async_copy(xm_hbm.at[pl.ds(0, FFN_TILE), :], xbuf.at[s], gsem.at[s]).wait()

    def start_scatter(tile, s):
        def body(c, carry):
            for j in range(MOVE_UNROLL):
                m = c * MOVE_UNROLL + j
                row = inv_ref[tile * FFN_TILE + m]
                pltpu.make_async_copy(ybuf.at[s, pl.ds(m, 1), :], y_hbm.at[pl.ds(row, 1), :], ssem.at[s]).start()
            return carry
        lax.fori_loop(0, FFN_TILE // MOVE_UNROLL, body, 0)

    def wait_scatter(s):
        pltpu.make_async_copy(ybuf.at[s], y_hbm.at[pl.ds(0, FFN_TILE), :], ssem.at[s]).wait()

    @pl.when(i == 0)
    def _():
        ybuf[...] = jnp.zeros(ybuf.shape, F32)
        for s in range(2):
            pltpu.make_async_copy(ybuf.at[s], y_hbm.at[pl.ds(n_pairs + s * FFN_TILE, FFN_TILE), :], ssem.at[s]).start()
        for s in range(2):
            wait_scatter(s)
        start_gather(0, 0)

    @pl.when(i + 1 < nu)
    def _():
        start_gather(i + 1, 1 - slot)

    e = te_ref[i]
    prev = te_ref[jnp.maximum(i - 1, 0)]

    @pl.when((i == 0) | (e != prev))
    def _():
        for c in range(D_MODEL // LANES):
            w1b[c * LANES:(c + 1) * LANES, :] = w1_ref[0, c * LANES:(c + 1) * LANES, :].astype(BF16)
        for c in range(D_EXPERT // LANES):
            w2b[c * LANES:(c + 1) * LANES, :] = w2_ref[0, c * LANES:(c + 1) * LANES, :].astype(BF16)

    @pl.when(i < nu)
    def _():
        wait_gather(slot)

        @pl.when(i >= 2)
        def _():
            wait_scatter(slot)

        x = xbuf[slot].astype(BF16)
        h = jnp.dot(x, w1b[...], preferred_element_type=F32) + b1_ref[0]
        g = jnp.minimum(h[:, :D_EXPERT], SWIGLU_LIMIT)
        lin = jnp.clip(h[:, D_EXPERT:], -SWIGLU_LIMIT, SWIGLU_LIMIT)
        a = (lin + 1.0) * (g * jax.nn.sigmoid(SWIGLU_ALPHA * g))
        ybuf[slot] = jnp.dot(a.astype(BF16), w2b[...], preferred_element_type=F32) + b2_ref[0]
        start_scatter(i, slot)

        @pl.when(i == nu - 1)
        def _():
            @pl.when(i >= 1)
            def _():
                wait_scatter(1 - slot)
            wait_scatter(slot)


def grouped_ffn(tile_expert, n_used, inv, xm, w1, b1, w2, b2):
    n_tokens, d = xm.shape
    nt = tile_expert.shape[0]
    wsel = lambda i, te, nu, inv: (te[i], 0, 0)
    return pl.pallas_call(
        functools.partial(_ffn_kernel, n_tokens=n_tokens),
        out_shape=jax.ShapeDtypeStruct((TOP_K * n_tokens + 2 * FFN_TILE, d), F32),
        grid_spec=pltpu.PrefetchScalarGridSpec(
            num_scalar_prefetch=3, grid=(nt,),
            in_specs=[pl.BlockSpec(memory_space=pl.ANY),
                      pl.BlockSpec((1, d, 2 * D_EXPERT), wsel),
                      pl.BlockSpec((1, 1, 2 * D_EXPERT), wsel),
                      pl.BlockSpec((1, D_EXPERT, d), wsel),
                      pl.BlockSpec((1, 1, d), wsel)],
            out_specs=pl.BlockSpec(memory_space=pl.ANY),
            scratch_shapes=[pltpu.VMEM((2, FFN_TILE, d), F32), pltpu.VMEM((2, FFN_TILE, d), F32),
                            pltpu.VMEM((d, 2 * D_EXPERT), BF16), pltpu.VMEM((D_EXPERT, d), BF16),
                            pltpu.SemaphoreType.DMA((2,)), pltpu.SemaphoreType.DMA((2,))]),
        compiler_params=_params(("arbitrary",), VMEM_LIMIT),
        name="grouped_ffn",
    )(tile_expert, n_used, inv, xm, w1, b1.reshape(N_EXPERTS, 1, -1), w2, b2.reshape(N_EXPERTS, 1, -1))


def _combine_kernel(x1_ref, wts_ref, m_ref, fn_ref, y0_ref, y1_ref, y2_ref, y3_ref, o_ref, *, final):
    w = wts_ref[0]
    acc = w[:, 0:1] * y0_ref[...]
    for k, y_ref in ((1, y1_ref), (2, y2_ref), (3, y3_ref)):
        acc = acc + w[:, k:k + 1] * y_ref[...]
    out = x1_ref[0] + m_ref[0][5:6] * acc
    if final:
        out = _rms(out, fn_ref[...])
    o_ref[0] = out


def combine_rows(x1, wts, mods, mod_off, fn, y, row_off, n_total, *, final):
    bm, sm, d = x1.shape
    tr = min(ROW_TILE, sm)
    nb = sm // tr
    row = lambda width: pl.BlockSpec((1, tr, width), lambda b, i: (b, i, 0))
    ysel = lambda k: pl.BlockSpec((tr, d), lambda b, i: ((k * n_total + row_off) // tr + b * nb + i, 0))
    return pl.pallas_call(
        functools.partial(_combine_kernel, final=final),
        out_shape=jax.ShapeDtypeStruct((bm, sm, d), F32),
        grid=(bm, nb),
        in_specs=[row(d), row(LANES),
                  pl.BlockSpec((1, SUBLANES, d), lambda b, i: (b + mod_off, 0, 0)),
                  _const_spec((1, d))] + [ysel(k) for k in range(TOP_K)],
        out_specs=row(d),
        compiler_params=_params(("parallel", "parallel"), VMEM_LIMIT),
        name="combine_rows",
    )(x1, wts, mods, fn, y, y, y, y)


def _axial_angles(n_tokens, rot_dim):
    t = jnp.arange(n_tokens)
    rows = (t // GRID_W).astype(F32)
    cols = (t % GRID_W).astype(F32)
    n_freq = rot_dim // 4
    inv = ROPE_THETA ** (-jnp.arange(n_freq, dtype=F32) / n_freq)
    return jnp.concatenate([rows[:, None] * inv, cols[:, None] * inv], axis=-1)


def _lane_table(parts, n):
    cols = []
    for p in parts:
        cols.append(jnp.broadcast_to(jnp.asarray(p, F32), (n, p.shape[-1])) if hasattr(p, "shape") else p)
    return jnp.concatenate(cols, axis=-1)


def _swap_halves(w):
    half = w.shape[-1] // 2
    return jnp.concatenate([-w[..., half:], w[..., :half]], axis=-1)


def _prep_l0(w_in, q_norm, kv_norm, w_uq, w_uk, w_uv, w_out):
    d = w_in.shape[0]
    o_kr = MLA_Q_RANK + MLA_KV_RANK
    kr_cols = w_in[:, o_kr:o_kr + MLA_ROPE]
    win = jnp.concatenate(
        [w_in[:, :o_kr], kr_cols, _swap_halves(kr_cols), jnp.zeros((d, LANES - 2 * MLA_ROPE), F32),
         w_in[:, o_kr + MLA_ROPE:]], axis=1).astype(BF16)
    qk = MLA_NOPE + MLA_ROPE
    wuq3 = w_uq.reshape(MLA_Q_RANK, MLA_HEADS, qk)
    wuq = jnp.pad(wuq3, ((0, 0), (0, 0), (0, LANES - qk))).reshape(MLA_Q_RANK, -1).astype(BF16)
    wuqs3 = jnp.concatenate(
        [jnp.zeros((MLA_Q_RANK, MLA_HEADS, MLA_NOPE), F32), _swap_halves(wuq3[:, :, MLA_NOPE:]),
         jnp.zeros((MLA_Q_RANK, MLA_HEADS, LANES - qk), F32)], axis=-1)
    wuqs = wuqs3.reshape(MLA_Q_RANK, -1).astype(BF16)
    wuk3 = w_uk.reshape(MLA_KV_RANK, MLA_HEADS, MLA_NOPE)
    wuk = jnp.pad(wuk3, ((0, 0), (0, 0), (0, LANES - MLA_NOPE))).reshape(MLA_KV_RANK, -1).astype(BF16)
    wuv3 = w_uv.reshape(MLA_KV_RANK, MLA_HEADS, MLA_V)
    wuv = jnp.pad(wuv3, ((0, 0), (0, 0), (0, LANES - MLA_V))).reshape(MLA_KV_RANK, -1).astype(BF16)
    eye = jnp.eye(MLA_ROPE, dtype=F32)
    e_head = jnp.concatenate([jnp.zeros((MLA_ROPE, MLA_NOPE), F32), eye,
                              jnp.zeros((MLA_ROPE, LANES - qk), F32)], axis=1)
    e = jnp.pad(jnp.tile(e_head, (1, MLA_HEADS)), ((0, LANES - MLA_ROPE), (0, 0))).astype(BF16)
    wa3 = w_out[:MLA_HEADS * MLA_V].reshape(MLA_HEADS, MLA_V, d)
    wa = jnp.pad(wa3, ((0, 0), (0, LANES - MLA_V), (0, 0))).reshape(MLA_HEADS * LANES, d).astype(BF16)
    wc = w_out[MLA_HEADS * MLA_V:].astype(BF16)
    return dict(win=win, qg=q_norm.reshape(1, -1), kvg=kv_norm.reshape(1, -1), wuq=wuq, wuqs=wuqs,
                wuk=wuk, e=e, wuv=wuv, wa=wa, wc=wc)


def _l0_tables(n):
    ang = _axial_angles(n, MLA_ROPE)
    cos, sin = jnp.cos(ang), jnp.sin(ang)
    one = jnp.ones((n, 1), F32)
    zero = jnp.zeros((n, 1), F32)
    rest = LANES - MLA_NOPE - MLA_ROPE
    cq = jnp.concatenate([jnp.tile(one, (1, MLA_NOPE)), cos, cos, jnp.tile(one, (1, rest))], axis=1)
    sq = jnp.concatenate([jnp.tile(zero, (1, MLA_NOPE)), sin, sin, jnp.tile(zero, (1, rest))], axis=1)
    ck = jnp.concatenate([cos, cos, jnp.tile(zero, (1, LANES - MLA_ROPE))], axis=1)
    sk = jnp.concatenate([sin, sin, jnp.tile(zero, (1, LANES - MLA_ROPE))], axis=1)
    return cq, sq, ck, sk


def _pad_heads(w, n_heads, dim):
    d = w.shape[0]
    return jnp.pad(w.reshape(d, n_heads, dim), ((0, 0), (0, 0), (0, LANES - dim))).reshape(d, n_heads * LANES)


def _prep_l1(w_in, q_norm, k_norm, ln_g, ln_b, w_s, b_s, w_out):
    d = w_in.shape[0]
    qd = GQA_HEADS * GQA_HEAD_DIM
    kd = GQA_KV_HEADS * GQA_HEAD_DIM
    wq, wk, wv = w_in[:, :qd], w_in[:, qd:qd + kd], w_in[:, qd + kd:qd + 2 * kd]
    rest = w_in[:, qd + 2 * kd:]
    win = jnp.concatenate([_pad_heads(wq, GQA_HEADS, GQA_HEAD_DIM), _pad_heads(wk, GQA_KV_HEADS, GQA_HEAD_DIM),
                           _pad_heads(wv, GQA_KV_HEADS, GQA_HEAD_DIM), wv, rest], axis=1).astype(BF16)
    swq = _swap_halves(wq.reshape(d, GQA_HEADS, GQA_HEAD_DIM)).reshape(d, qd)
    swk = _swap_halves(wk.reshape(d, GQA_KV_HEADS, GQA_HEAD_DIM)).reshape(d, kd)
    wsw = jnp.concatenate([_pad_heads(swq, GQA_HEADS, GQA_HEAD_DIM),
                           _pad_heads(swk, GQA_KV_HEADS, GQA_HEAD_DIM)], axis=1).astype(BF16)
    half = GQA_HEAD_DIM // 2
    padg = lambda g: jnp.pad(g, (0, LANES - GQA_HEAD_DIM)).reshape(1, LANES)
    swapg = lambda g: jnp.concatenate([g[half:], g[:half]])
    wa3 = w_out[:qd].reshape(GQA_HEADS, GQA_HEAD_DIM, d)
    wa = jnp.pad(wa3, ((0, 0), (0, LANES - GQA_HEAD_DIM), (0, 0))).reshape(GQA_HEADS * LANES, d).astype(BF16)
    wc = w_out[qd:].astype(BF16)
    bs = jnp.repeat(b_s.T, LANES, axis=1)
    return dict(win=win, wsw=wsw, qg=padg(q_norm), qgs=padg(swapg(q_norm)), kg=padg(k_norm),
                kgs=padg(swapg(k_norm)), lng=ln_g.reshape(1, -1), lnb=ln_b.reshape(1, -1),
                ws=w_s.astype(BF16), bs=bs, wa=wa, wc=wc)


def _l1_tables(n):
    ang = _axial_angles(n, GQA_HEAD_DIM)
    cos, sin = jnp.cos(ang), jnp.sin(ang)
    pad = LANES - GQA_HEAD_DIM
    c = jnp.concatenate([cos, cos, jnp.ones((n, pad), F32)], axis=1)
    s = jnp.concatenate([sin, sin, jnp.zeros((n, pad), F32)], axis=1)
    return c, s


def _pad_lanes(x, width):
    return jnp.pad(x, [(0, 0)] * (x.ndim - 1) + [(0, width - x.shape[-1])])


def routed_ffn(groups, mods, n2, moe, tri, final_norm, *, final):
    router_w, router_b, w1, b1, w2, b2 = moe
    rw = _pad_lanes(router_w, LANES)
    rb = jnp.concatenate([router_b, jnp.full((LANES - N_EXPERTS,), NEG_BIG, F32)]).reshape(1, LANES)
    base = jnp.zeros((SUBLANES, LANES), F32)
    n_total = sum(g["x"].shape[0] * g["x"].shape[1] for g in groups)
    routed, row_offs, row_off = [], [], 0
    xm_all = jnp.zeros((n_total, D_MODEL), F32)
    for g in groups:
        x1, xm_all, idx, wts, rank, base = post_mixer(
            g["attn"], g["other"], g["x"], mods, g["mod_off"], n2, g["wa"], g["wc"], rw, rb, tri, base,
            xm_all, row_off, n_total)
        routed.append((x1, idx, wts, rank))
        row_offs.append(row_off)
        row_off += x1.shape[0] * x1.shape[1]
    counts = base[0, :N_EXPERTS].astype(jnp.int32)
    tiles = (counts + FFN_TILE - 1) // FFN_TILE
    tile_end = jnp.cumsum(tiles)
    offsets = (tile_end - tiles) * FFN_TILE
    nt = n_total * TOP_K // FFN_TILE + N_EXPERTS
    n_used = tile_end[-1:]
    tile_expert = jnp.sum(jnp.arange(nt)[:, None] >= tile_end[None, :], axis=1)
    tile_expert = jnp.minimum(tile_expert, tile_expert[n_used[0] - 1]).astype(jnp.int32)

    pos_flats = []
    for (x1, idx, wts, rank) in routed:
        n = idx.shape[0] * idx.shape[1]
        pos = plan_positions(offsets, idx.reshape(n, LANES), rank.reshape(n, LANES))
        pos_flats.append(pos[:, :TOP_K].reshape(-1))
    inv = invert_positions(jnp.concatenate(pos_flats), nt * FFN_TILE)
    y = grouped_ffn(tile_expert, n_used.astype(jnp.int32), inv, xm_all, w1, b1, w2, b2)
    outs = []
    for g, (x1, idx, wts, rank), off in zip(groups, routed, row_offs):
        outs.append(combine_rows(x1, wts, mods, g["mod_off"], final_norm, y, off, n_total, final=final))
    return outs


def kernel(x_prompt, x_sample, cache_l0_ckv, cache_l0_krope, cache_l1_k, cache_l1_v, c, c_ctx,
           l0_ada_w, l0_ada_b, l0_norm1, l0_w_in, l0_q_norm, l0_kv_norm, l0_w_uq, l0_w_uk, l0_w_uv,
           l0_conv_w, l0_conv_b, l0_conv_ln_g, l0_conv_ln_b, l0_w_out, l0_norm2,
           l0_router_w, l0_router_b, l0_w1, l0_b1, l0_w2, l0_b2,
           l1_ada_w, l1_ada_b, l1_norm1, l1_w_in, l1_q_norm, l1_k_norm, l1_gmlp_ln_g, l1_gmlp_ln_b,
           l1_w_s, l1_b_s, l1_w_out, l1_norm2,
           l1_router_w, l1_router_b, l1_w1, l1_b1, l1_w2, l1_b2,
           final_norm):
    bp, sp, d = x_prompt.shape
    bs, ss, _ = x_sample.shape
    past = cache_l0_ckv.shape[1]
    n_p = bp * sp

    cond8 = jnp.concatenate([c_ctx[None], c, jnp.zeros((SUBLANES - 1 - bs, d), F32)], axis=0)
    mods0 = adaln(cond8, l0_ada_w, l0_ada_b)
    mods1 = adaln(cond8, l1_ada_w, l1_ada_b)
    tri = jnp.tril(jnp.ones((ROW_TILE, ROW_TILE), F32), -1).astype(BF16)
    fn = final_norm.reshape(1, d)

    w0 = _prep_l0(l0_w_in, l0_q_norm, l0_kv_norm, l0_w_uq, l0_w_uk, l0_w_uv, l0_w_out)
    n1 = l0_norm1.reshape(1, d)
    hp = x_prompt.reshape(1, n_p, d)
    q_p, k_p, v_p, ckv_p, kr_p, u_p = l0_inproj(hp, mods0, 0, n1, w0, None)
    q_s, k_s, v_s, _, _, u_s = l0_inproj(x_sample, mods0, 1, n1, w0, _l0_tables(ss))
    k_c, v_c = mla_ctx_kv(cache_l0_ckv, _pad_lanes(cache_l0_krope, LANES), w0)
    hw = MLA_HEADS * LANES
    att_p = attention(q_p.reshape(bp, sp, hw), k_p.reshape(bp, sp, hw), v_p.reshape(bp, sp, hw),
                      n_heads=MLA_HEADS, n_kv=MLA_HEADS, scale=MLA_SCALE, heads_per_step=MLA_HEADS)
    att_s = attention(q_s, jnp.concatenate([k_c, k_s], axis=1), jnp.concatenate([v_c, v_s], axis=1),
                      n_heads=MLA_HEADS, n_kv=MLA_HEADS, scale=MLA_SCALE, heads_per_step=1)
    conv_p = conformer_conv(u_p.reshape(bp, sp, CONV_CH), l0_conv_w, l0_conv_b, l0_conv_ln_g, l0_conv_ln_b)
    conv_s = conformer_conv(u_s, l0_conv_w, l0_conv_b, l0_conv_ln_g, l0_conv_ln_b)
    groups = [dict(attn=att_p.reshape(1, n_p, hw), other=conv_p.reshape(1, n_p, CONV_CH), x=hp, mod_off=0,
                   wa=w0["wa"], wc=w0["wc"]),
              dict(attn=att_s, other=conv_s, x=x_sample, mod_off=1, wa=w0["wa"], wc=w0["wc"])]
    hp, hs = routed_ffn(groups, mods0, l0_norm2.reshape(1, d),
                        (l0_router_w, l0_router_b, l0_w1, l0_b1, l0_w2, l0_b2), tri, fn, final=False)
    new_l0_ckv = ckv_p.reshape(bp, sp, MLA_KV_RANK)
    new_l0_krope = kr_p.reshape(bp, sp, MLA_ROPE)

    w1p = _prep_l1(l1_w_in, l1_q_norm, l1_k_norm, l1_gmlp_ln_g, l1_gmlp_ln_b, l1_w_s, l1_b_s, l1_w_out)
    n1 = l1_norm1.reshape(1, d)
    q_p, k_p, kn_p, vp_p, vraw_p, gat_p = l1_inproj(hp, mods1, 0, n1, w1p, None)
    q_s, k_s, _, vp_s, _, gat_s = l1_inproj(hs, mods1, 1, n1, w1p, _l1_tables(ss))
    qw = GQA_HEADS * LANES
    kw = GQA_KV_HEADS * LANES
    pad_kv = lambda t: _pad_lanes(t, LANES).reshape(bs, past, kw).astype(BF16)
    att_p = attention(q_p.reshape(bp, sp, qw), k_p.reshape(bp, sp, kw), vp_p.reshape(bp, sp, kw),
                      n_heads=GQA_HEADS, n_kv=GQA_KV_HEADS, scale=GQA_SCALE, heads_per_step=GQA_HEADS)
    att_s = attention(q_s, jnp.concatenate([pad_kv(cache_l1_k), k_s], axis=1),
                      jnp.concatenate([pad_kv(cache_l1_v), vp_s], axis=1),
                      n_heads=GQA_HEADS, n_kv=GQA_KV_HEADS, scale=GQA_SCALE, heads_per_step=1)
    groups = [dict(attn=att_p.reshape(1, n_p, qw), other=gat_p, x=hp, mod_off=0, wa=w1p["wa"], wc=w1p["wc"]),
              dict(attn=att_s, other=gat_s, x=hs, mod_off=1, wa=w1p["wa"], wc=w1p["wc"])]
    yp, ys = routed_ffn(groups, mods1, l1_norm2.reshape(1, d),
                        (l1_router_w, l1_router_b, l1_w1, l1_b1, l1_w2, l1_b2), tri, fn, final=True)
    new_l1_k = kn_p.reshape(bp, sp, GQA_KV_HEADS, LANES)[..., :GQA_HEAD_DIM]
    new_l1_v = vraw_p.reshape(bp, sp, GQA_KV_HEADS, GQA_HEAD_DIM)
    return (yp.reshape(bp, sp, d), ys, new_l0_ckv, new_l0_krope, new_l1_k, new_l1_v)
```

```python
import functools
import math

import jax
import jax.numpy as jnp
from jax import lax
from jax.experimental import pallas as pl
from jax.experimental.pallas import tpu as pltpu

F32 = jnp.float32
BF16 = jnp.bfloat16
HIGHEST = lax.Precision.HIGHEST

LANES = 128
SUBLANES = 8
VMEM_LIMIT = 56 * 1024 * 1024

D_MODEL = 1024
GRID_W = 64
ROPE_THETA = 10000.0
EPS = 1e-6
N_MOD = 6

MLA_HEADS = 8
MLA_NOPE = 64
MLA_ROPE = 32
MLA_V = 64
MLA_Q_RANK = 384
MLA_KV_RANK = 256
MLA_SCALE = 1.0 / math.sqrt(MLA_NOPE + MLA_ROPE)
CONV_CH = 512
CONV_WIDTH = 31
CONV_HALO = 16

GQA_HEADS = 8
GQA_KV_HEADS = 2
GQA_HEAD_DIM = 64
GQA_SCALE = 1.0 / math.sqrt(GQA_HEAD_DIM)
CHUNK = 128
GMLP_GROUPS = 4
GMLP_CH = 512

N_EXPERTS = 32
TOP_K = 4
D_EXPERT = 1024
SWIGLU_LIMIT = 7.0
SWIGLU_ALPHA = 1.702

ROW_TILE = 512
FFN_TILE = 256
MOVE_TILE = 256
MOVE_UNROLL = 8
ATT_Q_TILE = 256
NEG_BIG = -1e30


def _params(sem, vmem=None):
    return pltpu.CompilerParams(dimension_semantics=sem, vmem_limit_bytes=vmem)


def _rms(x, g):
    return x * lax.rsqrt(jnp.mean(x * x, axis=-1, keepdims=True) + EPS) * g


def _const_spec(shape):
    nd = len(shape)
    return pl.BlockSpec(shape, lambda *_: (0,) * nd)


def _adaln_kernel(c_ref, w_ref, b_ref, o_ref):
    c = c_ref[...]
    s = c * jax.nn.sigmoid(c)
    o_ref[...] = jnp.dot(s, w_ref[...], preferred_element_type=F32, precision=HIGHEST) + b_ref[...]


def adaln(cond8, ada_w, ada_b):
    d, n = ada_w.shape
    bn = n // 4
    m = pl.pallas_call(
        _adaln_kernel,
        out_shape=jax.ShapeDtypeStruct((SUBLANES, n), F32),
        grid=(n // bn,),
        in_specs=[_const_spec((SUBLANES, d)),
                  pl.BlockSpec((d, bn), lambda j: (0, j)),
                  pl.BlockSpec((1, bn), lambda j: (0, j))],
        out_specs=pl.BlockSpec((SUBLANES, bn), lambda j: (0, j)),
        compiler_params=_params(("arbitrary",), VMEM_LIMIT),
        name="adaln",
    )(cond8, ada_w, ada_b.reshape(1, n))
    m = m.reshape(SUBLANES, N_MOD, d)
    return jnp.pad(m, ((0, 0), (0, SUBLANES - N_MOD), (0, 0)))


def _l0_inproj_kernel(*refs, rope):
    if rope:
        (x_ref, m_ref, n1_ref, win_ref, qg_ref, kvg_ref, wuq_ref, wuqs_ref, wuk_ref, e_ref, wuv_ref,
         cq_ref, sq_ref, ck_ref, sk_ref, q_out, k_out, v_out, ckv_out, kr_out, u_out) = refs
    else:
        (x_ref, m_ref, n1_ref, win_ref, qg_ref, kvg_ref, wuq_ref, wuk_ref, e_ref, wuv_ref,
         q_out, k_out, v_out, ckv_out, kr_out, u_out) = refs
    x = x_ref[0]
    m = m_ref[0]
    h = _rms(x, n1_ref[...]) * (1.0 + m[1:2]) + m[0:1]
    z = jnp.dot(h.astype(BF16), win_ref[...], preferred_element_type=F32)
    c_q = z[:, 0:MLA_Q_RANK]
    c_kv = z[:, MLA_Q_RANK:MLA_Q_RANK + MLA_KV_RANK]
    kr_blk = z[:, 640:768]
    val = z[:, 768:768 + CONV_CH]
    gate = z[:, 768 + CONV_CH:768 + 2 * CONV_CH]

    cqn = _rms(c_q, qg_ref[...]).astype(BF16)
    q = jnp.dot(cqn, wuq_ref[...], preferred_element_type=F32)
    if rope:
        qs = jnp.dot(cqn, wuqs_ref[...], preferred_element_type=F32)
        cq = cq_ref[...]
        sq = sq_ref[...]
        for hd in range(MLA_HEADS):
            sl = slice(hd * LANES, (hd + 1) * LANES)
            q_out[0, :, sl] = (q[:, sl] * cq + qs[:, sl] * sq).astype(BF16)
        kr = kr_blk * ck_ref[...] + pltpu.roll(kr_blk, LANES - MLA_ROPE, 1) * sk_ref[...]
    else:
        q_out[0] = q.astype(BF16)
        kr = kr_blk

    ckv = _rms(c_kv, kvg_ref[...])
    ckv_out[0] = ckv
    kr_out[0] = kr_blk[:, 0:MLA_ROPE]
    ckv_b = ckv.astype(BF16)
    k = (jnp.dot(ckv_b, wuk_ref[...], preferred_element_type=F32)
         + jnp.dot(kr.astype(BF16), e_ref[...], preferred_element_type=F32))
    k_out[0] = k.astype(BF16)
    v_out[0] = jnp.dot(ckv_b, wuv_ref[...], preferred_element_type=F32).astype(BF16)
    u_out[0] = val * jax.nn.sigmoid(gate)


def l0_inproj(x, mods, mod_off, n1, w, tables):
    bm, sm, d = x.shape
    tr = min(ROW_TILE, sm)
    rope = tables is not None
    hp = MLA_HEADS * LANES
    row = lambda width: pl.BlockSpec((1, tr, width), lambda b, i: (b, i, 0))
    in_specs = [row(d),
                pl.BlockSpec((1, SUBLANES, d), lambda b, i: (b + mod_off, 0, 0)),
                _const_spec((1, d)), _const_spec(w["win"].shape),
                _const_spec((1, MLA_Q_RANK)), _const_spec((1, MLA_KV_RANK)),
                _const_spec(w["wuq"].shape)]
    args = [x, mods, n1, w["win"], w["qg"], w["kvg"], w["wuq"]]
    if rope:
        in_specs.append(_const_spec(w["wuqs"].shape))
        args.append(w["wuqs"])
    in_specs += [_const_spec(w["wuk"].shape), _const_spec(w["e"].shape), _const_spec(w["wuv"].shape)]
    args += [w["wuk"], w["e"], w["wuv"]]
    if rope:
        in_specs += [pl.BlockSpec((tr, LANES), lambda b, i: (i, 0))] * 4
        args += list(tables)
    out_shape = [jax.ShapeDtypeStruct((bm, sm, hp), BF16),
                 jax.ShapeDtypeStruct((bm, sm, hp), BF16),
                 jax.ShapeDtypeStruct((bm, sm, hp), BF16),
                 jax.ShapeDtypeStruct((bm, sm, MLA_KV_RANK), F32),
                 jax.ShapeDtypeStruct((bm, sm, MLA_ROPE), F32),
                 jax.ShapeDtypeStruct((bm, sm, CONV_CH), F32)]
    out_specs = [row(hp), row(hp), row(hp), row(MLA_KV_RANK), row(MLA_ROPE), row(CONV_CH)]
    return pl.pallas_call(
        functools.partial(_l0_inproj_kernel, rope=rope),
        out_shape=out_shape, grid=(bm, sm // tr), in_specs=in_specs, out_specs=out_specs,
        compiler_params=_params(("parallel", "parallel"), VMEM_LIMIT),
        name="l0_inproj_rope" if rope else "l0_inproj",
    )(*args)


def _mla_ctx_kv_kernel(ckv_ref, kr_ref, wuk_ref, e_ref, wuv_ref, k_out, v_out):
    ckv_b = ckv_ref[0].astype(BF16)
    k = (jnp.dot(ckv_b, wuk_ref[...], preferred_element_type=F32)
         + jnp.dot(kr_ref[0].astype(BF16), e_ref[...], preferred_element_type=F32))
    k_out[0] = k.astype(BF16)
    v_out[0] = jnp.dot(ckv_b, wuv_ref[...], preferred_element_type=F32).astype(BF16)


def mla_ctx_kv(ckv, kr128, w):
    b, s, _ = ckv.shape
    hp = MLA_HEADS * LANES
    blk = lambda width: pl.BlockSpec((1, s, width), lambda i: (i, 0, 0))
    return pl.pallas_call(
        _mla_ctx_kv_kernel,
        out_shape=[jax.ShapeDtypeStruct((b, s, hp), BF16)] * 2,
        grid=(b,),
        in_specs=[blk(MLA_KV_RANK), blk(LANES), _const_spec(w["wuk"].shape),
                  _const_spec(w["e"].shape), _const_spec(w["wuv"].shape)],
        out_specs=[blk(hp), blk(hp)],
        compiler_params=_params(("parallel",), VMEM_LIMIT),
        name="mla_ctx_kv",
    )(ckv, kr128, w["wuk"], w["e"], w["wuv"])


def _conv_kernel(prev_ref, cur_ref, next_ref, w_ref, b_ref, g_ref, beta_ref, o_ref, pad_ref, *, rb):
    i = pl.program_id(1)
    last = pl.num_programs(1) - 1
    zeros = jnp.zeros((CONV_HALO, CONV_CH), F32)
    pad_ref[0:CONV_HALO, :] = jnp.where(i == 0, zeros, prev_ref[0])
    pad_ref[CONV_HALO:CONV_HALO + rb, :] = cur_ref[0]
    pad_ref[CONV_HALO + rb:CONV_HALO + rb + CONV_HALO, :] = jnp.where(i == last, zeros, next_ref[0])
    w = w_ref[...]
    shift = CONV_HALO - CONV_WIDTH // 2
    acc = jnp.zeros((rb, CONV_CH), F32) + b_ref[...]
    for k in range(CONV_WIDTH):
        acc = acc + pad_ref[k + shift:k + shift + rb, :] * w[k:k + 1, :]
    mu = jnp.mean(acc, axis=-1, keepdims=True)
    cen = acc - mu
    var = jnp.mean(cen * cen, axis=-1, keepdims=True)
    y = cen * lax.rsqrt(var + EPS) * g_ref[...] + beta_ref[...]
    o_ref[0] = (y * jax.nn.sigmoid(y)).astype(BF16)


def conformer_conv(u, conv_w, conv_b, ln_g, ln_b):
    b, s, c = u.shape
    rb = min(256, s)
    nh = rb // CONV_HALO
    n_halo_blocks = s // CONV_HALO
    wpad = jnp.pad(conv_w.reshape(CONV_WIDTH, c), ((0, 32 - CONV_WIDTH), (0, 0)))
    return pl.pallas_call(
        functools.partial(_conv_kernel, rb=rb),
        out_shape=jax.ShapeDtypeStruct((b, s, c), BF16),
        grid=(b, s // rb),
        in_specs=[pl.BlockSpec((1, CONV_HALO, c), lambda bi, i: (bi, jnp.maximum(i * nh - 1, 0), 0)),
                  pl.BlockSpec((1, rb, c), lambda bi, i: (bi, i, 0)),
                  pl.BlockSpec((1, CONV_HALO, c),
                               lambda bi, i: (bi, jnp.minimum((i + 1) * nh, n_halo_blocks - 1), 0)),
                  _const_spec((32, c)), _const_spec((1, c)), _const_spec((1, c)), _const_spec((1, c))],
        out_specs=pl.BlockSpec((1, rb, c), lambda bi, i: (bi, i, 0)),
        scratch_shapes=[pltpu.VMEM((rb + 2 * CONV_HALO, c), F32)],
        compiler_params=_params(("parallel", "parallel"), VMEM_LIMIT),
        name="conformer_conv",
    )(u, u, u, wpad, conv_b.reshape(1, c), ln_g.reshape(1, c), ln_b.reshape(1, c))


def _attn_kernel(q_ref, k_ref, v_ref, o_ref, *, heads, rep, scale):
    for hd in range(heads):
        g = hd // rep
        q = q_ref[0, :, hd * LANES:(hd + 1) * LANES]
        k = k_ref[0, :, g * LANES:(g + 1) * LANES]
        s = lax.dot_general(q, k, (((1,), (1,)), ((), ())), preferred_element_type=F32) * scale
        m = jnp.max(s, axis=-1, keepdims=True)
        p = jnp.exp(s - m)
        l = jnp.sum(p, axis=-1, keepdims=True)
        o = jnp.dot(p.astype(BF16), v_ref[0, :, g * LANES:(g + 1) * LANES], preferred_element_type=F32)
        o_ref[0, :, hd * LANES:(hd + 1) * LANES] = (o / l).astype(BF16)


def attention(q, k, v, *, n_heads, n_kv, scale, heads_per_step):
    b, sq, _ = q.shape
    sk = k.shape[1]
    rep = n_heads // n_kv
    tq = min(ATT_Q_TILE, sq)
    if heads_per_step == n_heads:
        grid = (b, 1, sq // tq)
        kv_spec = pl.BlockSpec((1, sk, n_kv * LANES), lambda bi, h, i: (bi, 0, 0))
        kern = functools.partial(_attn_kernel, heads=n_heads, rep=rep, scale=scale)
    else:
        assert heads_per_step == 1
        grid = (b, n_heads, sq // tq)
        kv_spec = pl.BlockSpec((1, sk, LANES), lambda bi, h, i: (bi, 0, h // rep))
        kern = functools.partial(_attn_kernel, heads=1, rep=1, scale=scale)
    q_spec = pl.BlockSpec((1, tq, heads_per_step * LANES), lambda bi, h, i: (bi, i, h))
    return pl.pallas_call(
        kern,
        out_shape=jax.ShapeDtypeStruct(q.shape, BF16),
        grid=grid, in_specs=[q_spec, kv_spec, kv_spec], out_specs=q_spec,
        compiler_params=_params(("parallel", "parallel", "parallel"), VMEM_LIMIT),
        name="attention",
    )(q, k, v)


def _l1_inproj_kernel(*refs, rope):
    if rope:
        (x_ref, m_ref, n1_ref, win_ref, wsw_ref, qg_ref, qgs_ref, kg_ref, kgs_ref, lng_ref, lnb_ref,
         ws_ref, bs_ref, c_ref, s_ref, q_out, k_out, kn_out, vp_out, v_out, g_out) = refs
    else:
        (x_ref, m_ref, n1_ref, win_ref, qg_ref, kg_ref, lng_ref, lnb_ref,
         ws_ref, bs_ref, q_out, k_out, kn_out, vp_out, v_out, g_out) = refs
    x = x_ref[0]
    m = m_ref[0]
    hb = (_rms(x, n1_ref[...]) * (1.0 + m[1:2]) + m[0:1]).astype(BF16)
    z = jnp.dot(hb, win_ref[...], preferred_element_type=F32)
    qw = GQA_HEADS * LANES
    kw = GQA_KV_HEADS * LANES
    o_k, o_vp, o_v, o_u, o_vg = qw, qw + kw, qw + 2 * kw, qw + 2 * kw + LANES, qw + 2 * kw + LANES + GMLP_CH
    if rope:
        zs = jnp.dot(hb, wsw_ref[...], preferred_element_type=F32)
        cos = c_ref[...]
        sin = s_ref[...]

    def head(col, zcol, g_ref, gs_ref):
        t = z[:, col:col + LANES]
        r = lax.rsqrt(jnp.sum(t * t, axis=-1, keepdims=True) * (1.0 / GQA_HEAD_DIM) + EPS)
        normed = t * r * g_ref[...]
        if not rope:
            return normed, normed
        ts = zs[:, zcol:zcol + LANES]
        return normed, normed * cos + ts * r * gs_ref[...] * sin

    for hd in range(GQA_HEADS):
        _, rot = head(hd * LANES, hd * LANES, qg_ref, qgs_ref if rope else None)
        q_out[0, :, hd * LANES:(hd + 1) * LANES] = rot.astype(BF16)
    for hd in range(GQA_KV_HEADS):
        normed, rot = head(o_k + hd * LANES, qw + hd * LANES, kg_ref, kgs_ref if rope else None)
        k_out[0, :, hd * LANES:(hd + 1) * LANES] = rot.astype(BF16)
        kn_out[0, :, hd * LANES:(hd + 1) * LANES] = normed
    vp_out[0] = z[:, o_vp:o_vp + kw].astype(BF16)
    v_out[0] = z[:, o_v:o_v + LANES]

    u = z[:, o_u:o_u + GMLP_CH]
    vg = z[:, o_vg:o_vg + GMLP_CH]
    mu = jnp.mean(vg, axis=-1, keepdims=True)
    cen = vg - mu
    var = jnp.mean(cen * cen, axis=-1, keepdims=True)
    vn = (cen * lax.rsqrt(var + EPS) * lng_ref[...] + lnb_ref[...]).astype(BF16)
    bias = bs_ref[...]
    rows = x.shape[0]
    for cidx in range(rows // CHUNK):
        r0 = cidx * CHUNK
        for g in range(GMLP_GROUPS):
            c0 = g * LANES
            mixed = jnp.dot(ws_ref[g], vn[r0:r0 + CHUNK, c0:c0 + LANES], preferred_element_type=F32)
            g_out[0, r0:r0 + CHUNK, c0:c0 + LANES] = (
                u[r0:r0 + CHUNK, c0:c0 + LANES] * (mixed + bias[:, c0:c0 + LANES])).astype(BF16)


def l1_inproj(x, mods, mod_off, n1, w, tables):
    bm, sm, d = x.shape
    tr = min(ROW_TILE, sm)
    rope = tables is not None
    qw = GQA_HEADS * LANES
    kw = GQA_KV_HEADS * LANES
    row = lambda width: pl.BlockSpec((1, tr, width), lambda b, i: (b, i, 0))
    vec = _const_spec((1, LANES))
    in_specs = [row(d), pl.BlockSpec((1, SUBLANES, d), lambda b, i: (b + mod_off, 0, 0)),
                _const_spec((1, d)), _const_spec(w["win"].shape)]
    args = [x, mods, n1, w["win"]]
    if rope:
        in_specs += [_const_spec(w["wsw"].shape), vec, vec, vec, vec]
        args += [w["wsw"], w["qg"], w["qgs"], w["kg"], w["kgs"]]
    else:
        in_specs += [vec, vec]
        args += [w["qg"], w["kg"]]
    in_specs += [_const_spec((1, GMLP_CH)), _const_spec((1, GMLP_CH)),
                 _const_spec(w["ws"].shape), _const_spec((CHUNK, GMLP_CH))]
    args += [w["lng"], w["lnb"], w["ws"], w["bs"]]
    if rope:
        in_specs += [pl.BlockSpec((tr, LANES), lambda b, i: (i, 0))] * 2
        args += list(tables)
    out_shape = [jax.ShapeDtypeStruct((bm, sm, qw), BF16),
                 jax.ShapeDtypeStruct((bm, sm, kw), BF16),
                 jax.ShapeDtypeStruct((bm, sm, kw), F32),
                 jax.ShapeDtypeStruct((bm, sm, kw), BF16),
                 jax.ShapeDtypeStruct((bm, sm, LANES), F32),
                 jax.ShapeDtypeStruct((bm, sm, GMLP_CH), BF16)]
    out_specs = [row(qw), row(kw), row(kw), row(kw), row(LANES), row(GMLP_CH)]
    return pl.pallas_call(
        functools.partial(_l1_inproj_kernel, rope=rope),
        out_shape=out_shape, grid=(bm, sm // tr), in_specs=in_specs, out_specs=out_specs,
        compiler_params=_params(("parallel", "parallel"), VMEM_LIMIT),
        name="l1_inproj_rope" if rope else "l1_inproj",
    )(*args)


def _post_kernel(a_ref, c_ref, x_ref, m_ref, n2_ref, wa_ref, wc_ref, rwh_ref, rwl_ref, rb_ref, tri_ref, base_ref,
                 x1_out, xm_out, idx_out, wts_out, rank_out, cnt_out, run_ref):
    first =(pl.program_id(0) == 0) & (pl.program_id(1) == 0)

    @pl.when(first)
    def _():
        run_ref[...] = base_ref[...]

    m = m_ref[0]
    y = (jnp.dot(a_ref[0], wa_ref[...], preferred_element_type=F32)
         + jnp.dot(c_ref[0], wc_ref[...], preferred_element_type=F32))
    x1 = x_ref[0] + m[2:3] * y
    x1_out[0] = x1
    xm = _rms(x1, n2_ref[...]) * (1.0 + m[4:5]) + m[3:4]
    xm_out[0] = xm

    xh = xm.astype(BF16)
    xl = (xm - xh.astype(F32)).astype(BF16)
    logits = (jnp.dot(xh, rwh_ref[...], preferred_element_type=F32)
              + jnp.dot(xl, rwh_ref[...], preferred_element_type=F32)
              + jnp.dot(xh, rwl_ref[...], preferred_element_type=F32)) + rb_ref[...]
    rows = logits.shape[0]
    lane = lax.broadcasted_iota(jnp.int32, (rows, LANES), 1).astype(F32)
    work = logits
    vals, hots = [], []
    idx_acc = jnp.zeros((rows, LANES), F32)
    for k in range(TOP_K):
        top = jnp.max(work, axis=-1, keepdims=True)
        sel = jnp.min(jnp.where(work == top, lane, float(LANES)), axis=-1, keepdims=True)
        hot = lane == sel
        vals.append(top)
        hots.append(hot)
        idx_acc = idx_acc + jnp.where(lane == float(k), sel, 0.0)
        work = jnp.where(hot, -jnp.inf, work)
    exps = [jnp.exp(v - vals[0]) for v in vals]
    denom = exps[0] + exps[1] + exps[2] + exps[3]
    wts = jnp.zeros((rows, LANES), F32)
    for k in range(TOP_K):
        wts = wts + jnp.where(lane == float(k), exps[k] / denom, 0.0)

    chosen = jnp.zeros((rows, LANES), F32)
    for hot in hots:
        chosen = chosen + hot.astype(F32)
    before = jnp.dot(tri_ref[...], chosen.astype(BF16), preferred_element_type=F32) + run_ref[0:1, :]
    rank = jnp.zeros((rows, LANES), F32)
    for k in range(TOP_K):
        rk = jnp.sum(jnp.where(hots[k], before, 0.0), axis=-1, keepdims=True)
        rank = rank + jnp.where(lane == float(k), rk, 0.0)
    run_ref[0:1, :] = run_ref[0:1, :] + jnp.sum(chosen, axis=0, keepdims=True)
    idx_out[0] = idx_acc.astype(jnp.int32)
    wts_out[0] = wts
    rank_out[0] = rank.astype(jnp.int32)
    cnt_out[...] = run_ref[...]


def post_mixer(attn, other, x, mods, mod_off, n2, wa, wc, rwh, rwl, rb, tri, base):
    bm, sm, d = x.shape
    tr = tri.shape[0]
    row = lambda width: pl.BlockSpec((1, tr, width), lambda b, i: (b, i, 0))
    out_shape = [jax.ShapeDtypeStruct((bm, sm, d), F32),
                 jax.ShapeDtypeStruct((bm, sm, d), F32),
                 jax.ShapeDtypeStruct((bm, sm, LANES), jnp.int32),
                 jax.ShapeDtypeStruct((bm, sm, LANES), F32),
                 jax.ShapeDtypeStruct((bm, sm, LANES), jnp.int32),
                 jax.ShapeDtypeStruct((SUBLANES, LANES), F32)]
    return pl.pallas_call(
        _post_kernel,
        out_shape=out_shape, grid=(bm, sm // tr),
        in_specs=[row(attn.shape[-1]), row(other.shape[-1]), row(d),
                  pl.BlockSpec((1, SUBLANES, d), lambda b, i: (b + mod_off, 0, 0)),
                  _const_spec((1, d)), _const_spec(wa.shape), _const_spec(wc.shape),
                  _const_spec(rwh.shape), _const_spec(rwl.shape), _const_spec((1, LANES)),
                  _const_spec(tri.shape), _const_spec((SUBLANES, LANES))],
        out_specs=[row(d), row(d), row(LANES), row(LANES), row(LANES), _const_spec((SUBLANES, LANES))],
        scratch_shapes=[pltpu.VMEM((SUBLANES, LANES), F32)],
        compiler_params=_params(("arbitrary", "arbitrary"), VMEM_LIMIT),
        name="post_mixer_route",
    )(attn, other, x, mods, n2, wa, wc, rwh, rwl, rb, tri, base)


def _plan_kernel(off_ref, idx_ref, rank_ref, pos_out):
    idx = idx_ref[...]
    pos = rank_ref[...]
    for e in range(N_EXPERTS):
        pos = pos + jnp.where(idx == e, off_ref[e], 0)
    pos_out[...] = pos


def plan_positions(offsets, idx, rank):
    n = idx.shape[0]
    tr = min(2048, n)
    spec = pl.BlockSpec((tr, LANES), lambda i, off: (i, 0))
    return pl.pallas_call(
        _plan_kernel,
        out_shape=jax.ShapeDtypeStruct((n, LANES), jnp.int32),
        grid_spec=pltpu.PrefetchScalarGridSpec(
            num_scalar_prefetch=1, grid=(n // tr,), in_specs=[spec, spec], out_specs=spec),
        compiler_params=_params(("parallel",)),
        name="plan_positions",
    )(offsets, idx, rank)


def _dispatch_kernel(pos_ref, xm_ref, xs_in, xs_out, sem):
    del xs_in
    t0 = pl.program_id(0) * MOVE_TILE

    def issue(c, carry):
        for j in range(MOVE_UNROLL):
            t = c * MOVE_UNROLL + j
            for k in range(TOP_K):
                p = pos_ref[(t0 + t) * TOP_K + k]
                pltpu.make_async_copy(xm_ref.at[pl.ds(t, 1), :], xs_out.at[pl.ds(p, 1), :], sem).start()
        return carry

    lax.fori_loop(0, MOVE_TILE // MOVE_UNROLL, issue, 0)
    for k in range(TOP_K):
        pltpu.make_async_copy(xm_ref, xs_out.at[pl.ds(0, MOVE_TILE), :], sem).wait()


def dispatch_rows(pos_flat, xm, xs):
    n, d = xm.shape
    return pl.pallas_call(
        _dispatch_kernel,
        out_shape=jax.ShapeDtypeStruct(xs.shape, xs.dtype),
        grid_spec=pltpu.PrefetchScalarGridSpec(
            num_scalar_prefetch=1, grid=(n // MOVE_TILE,),
            in_specs=[pl.BlockSpec((MOVE_TILE, d), lambda i, pos: (i, 0)),
                      pl.BlockSpec(memory_space=pl.ANY)],
            out_specs=pl.BlockSpec(memory_space=pl.ANY),
            scratch_shapes=[pltpu.SemaphoreType.DMA]),
        input_output_aliases={2: 0},
        compiler_params=_params(("arbitrary",)),
        name="dispatch_rows",
    )(pos_flat, xm, xs)


def _ffn_kernel(te_ref, nu_ref, first_ref, slot_ref, nxt_ref, xs_ref, w1_hbm, b1_ref, w2_hbm, b2_ref, y_ref,
                w1f, w2f, w1b, w2b, sem):
    i = pl.program_id(0)

    def weight_copies(e, s):
        return (pltpu.make_async_copy(w1_hbm.at[e], w1f.at[s], sem.at[0, s]),
                pltpu.make_async_copy(w2_hbm.at[e], w2f.at[s], sem.at[1, s]))

    @pl.when(i < nu_ref[0])
    def _():
        s = slot_ref[i]

        @pl.when(first_ref[i] == 1)
        def _():
            @pl.when(i == 0)
            def _():
                for cp in weight_copies(te_ref[i], s):
                    cp.start()
            for cp in weight_copies(te_ref[i], s):
                cp.wait()

            @pl.when(nxt_ref[i] >= 0)
            def _():
                for cp in weight_copies(nxt_ref[i], 1 - s):
                    cp.start()
            for c in range(D_MODEL // LANES):
                w1b[c * LANES:(c + 1) * LANES, :] = w1f[s, c * LANES:(c + 1) * LANES, :].astype(BF16)
            for c in range(D_EXPERT // LANES):
                w2b[c * LANES:(c + 1) * LANES, :] = w2f[s, c * LANES:(c + 1) * LANES, :].astype(BF16)

        h = jnp.dot(xs_ref[...].astype(BF16), w1b[...], preferred_element_type=F32) + b1_ref[0]
        g = jnp.minimum(h[:, :D_EXPERT], SWIGLU_LIMIT)
        lin = jnp.clip(h[:, D_EXPERT:], -SWIGLU_LIMIT, SWIGLU_LIMIT)
        a = (lin + 1.0) * (g * jax.nn.sigmoid(SWIGLU_ALPHA * g))
        y_ref[...] = jnp.dot(a.astype(BF16), w2b[...], preferred_element_type=F32) + b2_ref[0]

    @pl.when(i >= nu_ref[0])
    def _():
        y_ref[...] = jnp.zeros(y_ref.shape, F32)


def grouped_ffn(sched, xs, w1, b1, w2, b2):
    r, d = xs.shape
    nt = r // FFN_TILE
    rows = lambda i, te, nu, *_: (jnp.minimum(i, nu[0] - 1), 0)
    bsel = lambda i, te, *_: (te[i], 0, 0)
    return pl.pallas_call(
        _ffn_kernel,
        out_shape=jax.ShapeDtypeStruct((r, d), F32),
        grid_spec=pltpu.PrefetchScalarGridSpec(
            num_scalar_prefetch=5, grid=(nt,),
            in_specs=[pl.BlockSpec((FFN_TILE, d), rows),
                      pl.BlockSpec(memory_space=pl.ANY),
                      pl.BlockSpec((1, 1, 2 * D_EXPERT), bsel),
                      pl.BlockSpec(memory_space=pl.ANY),
                      pl.BlockSpec((1, 1, d), bsel)],
            out_specs=pl.BlockSpec((FFN_TILE, d), lambda i, *_: (i, 0)),
            scratch_shapes=[pltpu.VMEM((2, d, 2 * D_EXPERT), F32), pltpu.VMEM((2, D_EXPERT, d), F32),
                            pltpu.VMEM((d, 2 * D_EXPERT), BF16), pltpu.VMEM((D_EXPERT, d), BF16),
                            pltpu.SemaphoreType.DMA((2, 2))]),
        compiler_params=_params(("arbitrary",), VMEM_LIMIT),
        name="grouped_ffn",
    )(*sched, xs, w1, b1.reshape(N_EXPERTS, 1, -1), w2, b2.reshape(N_EXPERTS, 1, -1))


def _combine_kernel(pos_ref, x1_ref, wts_ref, m_ref, fn_ref, y_hbm, o_ref, buf, sem, *, final):
    t0 = pl.program_id(1) * MOVE_TILE + pl.program_id(0) * pl.num_programs(1) * MOVE_TILE

    def issue(c, carry):
        for j in range(MOVE_UNROLL):
            t = c * MOVE_UNROLL + j
            for k in range(TOP_K):
                p = pos_ref[(t0 + t) * TOP_K + k]
                pltpu.make_async_copy(y_hbm.at[pl.ds(p, 1), :], buf.at[k, pl.ds(t, 1), :], sem).start()
        return carry

    lax.fori_loop(0, MOVE_TILE // MOVE_UNROLL, issue, 0)
    for k in range(TOP_K):
        pltpu.make_async_copy(y_hbm.at[pl.ds(0, MOVE_TILE), :], buf.at[k], sem).wait()
    w = wts_ref[0]
    acc = w[:, 0:1] * buf[0]
    for k in range(1, TOP_K):
        acc = acc + w[:, k:k + 1] * buf[k]
    out = x1_ref[0] + m_ref[0][5:6] * acc
    if final:
        out = _rms(out, fn_ref[...])
    o_ref[0] = out


def combine_rows(pos_flat, x1, wts, mods, mod_off, fn, y, *, final):
    bm, sm, d = x1.shape
    row = lambda width: pl.BlockSpec((1, MOVE_TILE, width), lambda b, i, pos: (b, i, 0))
    return pl.pallas_call(
        functools.partial(_combine_kernel, final=final),
        out_shape=jax.ShapeDtypeStruct((bm, sm, d), F32),
        grid_spec=pltpu.PrefetchScalarGridSpec(
            num_scalar_prefetch=1, grid=(bm, sm // MOVE_TILE),
            in_specs=[row(d), row(LANES),
                      pl.BlockSpec((1, SUBLANES, d), lambda b, i, pos: (b + mod_off, 0, 0)),
                      pl.BlockSpec((1, d), lambda b, i, pos: (0, 0)),
                      pl.BlockSpec(memory_space=pl.ANY)],
            out_specs=row(d),
            scratch_shapes=[pltpu.VMEM((TOP_K, MOVE_TILE, d), F32), pltpu.SemaphoreType.DMA]),
        compiler_params=_params(("arbitrary", "arbitrary"), VMEM_LIMIT),
        name="combine_rows",
    )(pos_flat, x1, wts, mods, fn, y)


def _axial_angles(n_tokens, rot_dim):
    t = jnp.arange(n_tokens)
    rows = (t // GRID_W).astype(F32)
    cols = (t % GRID_W).astype(F32)
    n_freq = rot_dim // 4
    inv = ROPE_THETA ** (-jnp.arange(n_freq, dtype=F32) / n_freq)
    return jnp.concatenate([rows[:, None] * inv, cols[:, None] * inv], axis=-1)


def _lane_table(parts, n):
    cols = []
    for p in parts:
        cols.append(jnp.broadcast_to(jnp.asarray(p, F32), (n, p.shape[-1])) if hasattr(p, "shape") else p)
    return jnp.concatenate(cols, axis=-1)


def _swap_halves(w):
    half = w.shape[-1] // 2
    return jnp.concatenate([-w[..., half:], w[..., :half]], axis=-1)


def _prep_l0(w_in, q_norm, kv_norm, w_uq, w_uk, w_uv, w_out):
    d = w_in.shape[0]
    o_kr = MLA_Q_RANK + MLA_KV_RANK
    kr_cols = w_in[:, o_kr:o_kr + MLA_ROPE]
    win = jnp.concatenate(
        [w_in[:, :o_kr], kr_cols, _swap_halves(kr_cols), jnp.zeros((d, LANES - 2 * MLA_ROPE), F32),
         w_in[:, o_kr + MLA_ROPE:]], axis=1).astype(BF16)
    qk = MLA_NOPE + MLA_ROPE
    wuq3 = w_uq.reshape(MLA_Q_RANK, MLA_HEADS, qk)
    wuq = jnp.pad(wuq3, ((0, 0), (0, 0), (0, LANES - qk))).reshape(MLA_Q_RANK, -1).astype(BF16)
    wuqs3 = jnp.concatenate(
        [jnp.zeros((MLA_Q_RANK, MLA_HEADS, MLA_NOPE), F32), _swap_halves(wuq3[:, :, MLA_NOPE:]),
         jnp.zeros((MLA_Q_RANK, MLA_HEADS, LANES - qk), F32)], axis=-1)
    wuqs = wuqs3.reshape(MLA_Q_RANK, -1).astype(BF16)
    wuk3 = w_uk.reshape(MLA_KV_RANK, MLA_HEADS, MLA_NOPE)
    wuk = jnp.pad(wuk3, ((0, 0), (0, 0), (0, LANES - MLA_NOPE))).reshape(MLA_KV_RANK, -1).astype(BF16)
    wuv3 = w_uv.reshape(MLA_KV_RANK, MLA_HEADS, MLA_V)
    wuv = jnp.pad(wuv3, ((0, 0), (0, 0), (0, LANES - MLA_V))).reshape(MLA_KV_RANK, -1).astype(BF16)
    eye = jnp.eye(MLA_ROPE, dtype=F32)
    e_head = jnp.concatenate([jnp.zeros((MLA_ROPE, MLA_NOPE), F32), eye,
                              jnp.zeros((MLA_ROPE, LANES - qk), F32)], axis=1)
    e = jnp.pad(jnp.tile(e_head, (1, MLA_HEADS)), ((0, LANES - MLA_ROPE), (0, 0))).astype(BF16)
    wa3 = w_out[:MLA_HEADS * MLA_V].reshape(MLA_HEADS, MLA_V, d)
    wa = jnp.pad(wa3, ((0, 0), (0, LANES - MLA_V), (0, 0))).reshape(MLA_HEADS * LANES, d).astype(BF16)
    wc = w_out[MLA_HEADS * MLA_V:].astype(BF16)
    return dict(win=win, qg=q_norm.reshape(1, -1), kvg=kv_norm.reshape(1, -1), wuq=wuq, wuqs=wuqs,
                wuk=wuk, e=e, wuv=wuv, wa=wa, wc=wc)


def _l0_tables(n):
    ang = _axial_angles(n, MLA_ROPE)
    cos, sin = jnp.cos(ang), jnp.sin(ang)
    one = jnp.ones((n, 1), F32)
    zero = jnp.zeros((n, 1), F32)
    rest = LANES - MLA_NOPE - MLA_ROPE
    cq = jnp.concatenate([jnp.tile(one, (1, MLA_NOPE)), cos, cos, jnp.tile(one, (1, rest))], axis=1)
    sq = jnp.concatenate([jnp.tile(zero, (1, MLA_NOPE)), sin, sin, jnp.tile(zero, (1, rest))], axis=1)
    ck = jnp.concatenate([cos, cos, jnp.tile(zero, (1, LANES - MLA_ROPE))], axis=1)
    sk = jnp.concatenate([sin, sin, jnp.tile(zero, (1, LANES - MLA_ROPE))], axis=1)
    return cq, sq, ck, sk


def _pad_heads(w, n_heads, dim):
    d = w.shape[0]
    return jnp.pad(w.reshape(d, n_heads, dim), ((0, 0), (0, 0), (0, LANES - dim))).reshape(d, n_heads * LANES)


def _prep_l1(w_in, q_norm, k_norm, ln_g, ln_b, w_s, b_s, w_out):
    d = w_in.shape[0]
    qd = GQA_HEADS * GQA_HEAD_DIM
    kd = GQA_KV_HEADS * GQA_HEAD_DIM
    wq, wk, wv = w_in[:, :qd], w_in[:, qd:qd + kd], w_in[:, qd + kd:qd + 2 * kd]
    rest = w_in[:, qd + 2 * kd:]
    win = jnp.concatenate([_pad_heads(wq, GQA_HEADS, GQA_HEAD_DIM), _pad_heads(wk, GQA_KV_HEADS, GQA_HEAD_DIM),
                           _pad_heads(wv, GQA_KV_HEADS, GQA_HEAD_DIM), wv, rest], axis=1).astype(BF16)
    swq = _swap_halves(wq.reshape(d, GQA_HEADS, GQA_HEAD_DIM)).reshape(d, qd)
    swk = _swap_halves(wk.reshape(d, GQA_KV_HEADS, GQA_HEAD_DIM)).reshape(d, kd)
    wsw = jnp.concatenate([_pad_heads(swq, GQA_HEADS, GQA_HEAD_DIM),
                           _pad_heads(swk, GQA_KV_HEADS, GQA_HEAD_DIM)], axis=1).astype(BF16)
    half = GQA_HEAD_DIM // 2
    padg = lambda g: jnp.pad(g, (0, LANES - GQA_HEAD_DIM)).reshape(1, LANES)
    swapg = lambda g: jnp.concatenate([g[half:], g[:half]])
    wa3 = w_out[:qd].reshape(GQA_HEADS, GQA_HEAD_DIM, d)
    wa = jnp.pad(wa3, ((0, 0), (0, LANES - GQA_HEAD_DIM), (0, 0))).reshape(GQA_HEADS * LANES, d).astype(BF16)
    wc = w_out[qd:].astype(BF16)
    bs = jnp.repeat(b_s.T, LANES, axis=1)
    return dict(win=win, wsw=wsw, qg=padg(q_norm), qgs=padg(swapg(q_norm)), kg=padg(k_norm),
                kgs=padg(swapg(k_norm)), lng=ln_g.reshape(1, -1), lnb=ln_b.reshape(1, -1),
                ws=w_s.astype(BF16), bs=bs, wa=wa, wc=wc)


def _l1_tables(n):
    ang = _axial_angles(n, GQA_HEAD_DIM)
    cos, sin = jnp.cos(ang), jnp.sin(ang)
    pad = LANES - GQA_HEAD_DIM
    c = jnp.concatenate([cos, cos, jnp.ones((n, pad), F32)], axis=1)
    s = jnp.concatenate([sin, sin, jnp.zeros((n, pad), F32)], axis=1)
    return c, s


def _pad_lanes(x, width):
    return jnp.pad(x, [(0, 0)] * (x.ndim - 1) + [(0, width - x.shape[-1])])


def routed_ffn(groups, mods, n2, moe, tri, final_norm, row_buffer, *, final):
    router_w, router_b, w1, b1, w2, b2 = moe
    rw = _pad_lanes(router_w, LANES)
    rwh = rw.astype(BF16)
    rwl = (rw - rwh.astype(F32)).astype(BF16)
    rb = jnp.concatenate([router_b, jnp.full((LANES - N_EXPERTS,), NEG_BIG, F32)]).reshape(1, LANES)
    base = jnp.zeros((SUBLANES, LANES), F32)
    routed = []
    for g in groups:
        x1, xm, idx, wts, rank, base = post_mixer(
            g["attn"], g["other"], g["x"], mods, g["mod_off"], n2, g["wa"], g["wc"], rwh, rwl, rb, tri, base)
        routed.append((x1, xm, idx, wts, rank))
    n_total = sum(r[0].shape[0] * r[0].shape[1] for r in routed)
    nt = n_total * TOP_K // FFN_TILE + N_EXPERTS

    counts = base[0, :N_EXPERTS].astype(jnp.int32)
    tiles = (counts + FFN_TILE - 1) // FFN_TILE
    tile_end = jnp.cumsum(tiles)
    offsets = (tile_end - tiles) * FFN_TILE
    n_used = tile_end[-1:].astype(jnp.int32)
    experts = jnp.arange(N_EXPERTS)
    busy = tiles > 0
    slot_e = (jnp.cumsum(busy) - 1) % 2
    later = jnp.where(busy[None, :] & (experts[None, :] > experts[:, None]), experts[None, :], N_EXPERTS)
    nxt_e = jnp.min(later, axis=1)
    nxt_e = jnp.where(nxt_e == N_EXPERTS, -1, nxt_e)
    tile_expert = jnp.sum(jnp.arange(nt)[:, None] >= tile_end[None, :], axis=1)
    tile_expert = jnp.minimum(tile_expert, tile_expert[n_used[0] - 1])
    first = jnp.concatenate([jnp.ones((1,), bool), tile_expert[1:] != tile_expert[:-1]])
    sched = tuple(a.astype(jnp.int32) for a in
                  (tile_expert, n_used, first, slot_e[tile_expert], nxt_e[tile_expert]))

    xs = jnp.zeros((nt * FFN_TILE, D_MODEL), F32) if row_buffer is None else row_buffer
    pos_flats = []
    for (x1, xm, idx, wts, rank) in routed:
        n = idx.shape[0] * idx.shape[1]
        pos = plan_positions(offsets, idx.reshape(n, LANES), rank.reshape(n, LANES))
        pos_flat = pos[:, :TOP_K].reshape(-1)
        pos_flats.append(pos_flat)
        xs = dispatch_rows(pos_flat, xm.reshape(n, D_MODEL), xs)
    y = grouped_ffn(sched, xs, w1, b1, w2, b2)
    outs = []
    for g, (x1, xm, idx, wts, rank), pos_flat in zip(groups, routed, pos_flats):
        outs.append(combine_rows(pos_flat, x1, wts, mods, g["mod_off"], final_norm, y, final=final))
    return outs, y


def kernel(x_prompt, x_sample, cache_l0_ckv, cache_l0_krope, cache_l1_k, cache_l1_v, c, c_ctx,
           l0_ada_w, l0_ada_b, l0_norm1, l0_w_in, l0_q_norm, l0_kv_norm, l0_w_uq, l0_w_uk, l0_w_uv,
           l0_conv_w, l0_conv_b, l0_conv_ln_g, l0_conv_ln_b, l0_w_out, l0_norm2,
           l0_router_w, l0_router_b, l0_w1, l0_b1, l0_w2, l0_b2,
           l1_ada_w, l1_ada_b, l1_norm1, l1_w_in, l1_q_norm, l1_k_norm, l1_gmlp_ln_g, l1_gmlp_ln_b,
           l1_w_s, l1_b_s, l1_w_out, l1_norm2,
           l1_router_w, l1_router_b, l1_w1, l1_b1, l1_w2, l1_b2,
           final_norm):
    bp, sp, d = x_prompt.shape
    bs, ss, _ = x_sample.shape
    past = cache_l0_ckv.shape[1]
    n_p = bp * sp

    cond8 = jnp.concatenate([c_ctx[None], c, jnp.zeros((SUBLANES - 1 - bs, d), F32)], axis=0)
    mods0 = adaln(cond8, l0_ada_w, l0_ada_b)
    mods1 = adaln(cond8, l1_ada_w, l1_ada_b)
    tri = jnp.tril(jnp.ones((ROW_TILE, ROW_TILE), F32), -1).astype(BF16)
    fn = final_norm.reshape(1, d)

    w0 = _prep_l0(l0_w_in, l0_q_norm, l0_kv_norm, l0_w_uq, l0_w_uk, l0_w_uv, l0_w_out)
    n1 = l0_norm1.reshape(1, d)
    hp = x_prompt.reshape(1, n_p, d)
    q_p, k_p, v_p, ckv_p, kr_p, u_p = l0_inproj(hp, mods0, 0, n1, w0, None)
    q_s, k_s, v_s, _, _, u_s = l0_inproj(x_sample, mods0, 1, n1, w0, _l0_tables(ss))
    k_c, v_c = mla_ctx_kv(cache_l0_ckv, _pad_lanes(cache_l0_krope, LANES), w0)
    hw = MLA_HEADS * LANES
    att_p = attention(q_p.reshape(bp, sp, hw), k_p.reshape(bp, sp, hw), v_p.reshape(bp, sp, hw),
                      n_heads=MLA_HEADS, n_kv=MLA_HEADS, scale=MLA_SCALE, heads_per_step=MLA_HEADS)
    att_s = attention(q_s, jnp.concatenate([k_c, k_s], axis=1), jnp.concatenate([v_c, v_s], axis=1),
                      n_heads=MLA_HEADS, n_kv=MLA_HEADS, scale=MLA_SCALE, heads_per_step=1)
    conv_p = conformer_conv(u_p.reshape(bp, sp, CONV_CH), l0_conv_w, l0_conv_b, l0_conv_ln_g, l0_conv_ln_b)
    conv_s = conformer_conv(u_s, l0_conv_w, l0_conv_b, l0_conv_ln_g, l0_conv_ln_b)
    groups = [dict(attn=att_p.reshape(1, n_p, hw), other=conv_p.reshape(1, n_p, CONV_CH), x=hp, mod_off=0,
                   wa=w0["wa"], wc=w0["wc"]),
              dict(attn=att_s, other=conv_s, x=x_sample, mod_off=1, wa=w0["wa"], wc=w0["wc"])]
    (hp, hs), rows0 = routed_ffn(groups, mods0, l0_norm2.reshape(1, d),
                                 (l0_router_w, l0_router_b, l0_w1, l0_b1, l0_w2, l0_b2), tri, fn, None, final=False)
    new_l0_ckv = ckv_p.reshape(bp, sp, MLA_KV_RANK)
    new_l0_krope = kr_p.reshape(bp, sp, MLA_ROPE)

    w1p = _prep_l1(l1_w_in, l1_q_norm, l1_k_norm, l1_gmlp_ln_g, l1_gmlp_ln_b, l1_w_s, l1_b_s, l1_w_out)
    n1 = l1_norm1.reshape(1, d)
    q_p, k_p, kn_p, vp_p, vraw_p, gat_p = l1_inproj(hp, mods1, 0, n1, w1p, None)
    q_s, k_s, _, vp_s, _, gat_s = l1_inproj(hs, mods1, 1, n1, w1p, _l1_tables(ss))
    qw = GQA_HEADS * LANES
    kw = GQA_KV_HEADS * LANES
    pad_kv = lambda t: _pad_lanes(t, LANES).reshape(bs, past, kw).astype(BF16)
    att_p = attention(q_p.reshape(bp, sp, qw), k_p.reshape(bp, sp, kw), vp_p.reshape(bp, sp, kw),
                      n_heads=GQA_HEADS, n_kv=GQA_KV_HEADS, scale=GQA_SCALE, heads_per_step=GQA_HEADS)
    att_s = attention(q_s, jnp.concatenate([pad_kv(cache_l1_k), k_s], axis=1),
                      jnp.concatenate([pad_kv(cache_l1_v), vp_s], axis=1),
                      n_heads=GQA_HEADS, n_kv=GQA_KV_HEADS, scale=GQA_SCALE, heads_per_step=1)
    groups = [dict(attn=att_p.reshape(1, n_p, qw), other=gat_p, x=hp, mod_off=0, wa=w1p["wa"], wc=w1p["wc"]),
              dict(attn=att_s, other=gat_s, x=hs, mod_off=1, wa=w1p["wa"], wc=w1p["wc"])]
    (yp, ys), _ = routed_ffn(groups, mods1, l1_norm2.reshape(1, d),
                             (l1_router_w, l1_router_b, l1_w1, l1_b1, l1_w2, l1_b2), tri, fn, rows0, final=True)
    new_l1_k = kn_p.reshape(bp, sp, GQA_KV_HEADS, LANES)[..., :GQA_HEAD_DIM]
    new_l1_v = vraw_p.reshape(bp, sp, GQA_KV_HEADS, GQA_HEAD_DIM)
    return (yp.reshape(bp, sp, d), ys, new_l0_ckv, new_l0_krope, new_l1_k, new_l1_v)
```

```python
import functools
import math

import jax
import jax.numpy as jnp
from jax import lax
from jax.experimental import pallas as pl
from jax.experimental.pallas import tpu as pltpu

F32 = jnp.float32
BF16 = jnp.bfloat16
HIGHEST = lax.Precision.HIGHEST

LANES = 128
SUBLANES = 8
VMEM_LIMIT = 56 * 1024 * 1024

D_MODEL = 1024
GRID_W = 64
ROPE_THETA = 10000.0
EPS = 1e-6
N_MOD = 6

MLA_HEADS = 8
MLA_NOPE = 64
MLA_ROPE = 32
MLA_V = 64
MLA_Q_RANK = 384
MLA_KV_RANK = 256
MLA_SCALE = 1.0 / math.sqrt(MLA_NOPE + MLA_ROPE)
CONV_CH = 512
CONV_WIDTH = 31
CONV_HALO = 16

GQA_HEADS = 8
GQA_KV_HEADS = 2
GQA_HEAD_DIM = 64
GQA_SCALE = 1.0 / math.sqrt(GQA_HEAD_DIM)
CHUNK = 128
GMLP_GROUPS = 4
GMLP_CH = 512

N_EXPERTS = 32
TOP_K = 4
D_EXPERT = 1024
SWIGLU_LIMIT = 7.0
SWIGLU_ALPHA = 1.702

ROW_TILE = 512
FFN_TILE = 256
MOVE_TILE = 256
MOVE_UNROLL = 8
ATT_Q_TILE = 256
ATT_LONG_KEYS = 1024
V_SUM_LANE = 64
LATENT_HEADS_PER_STEP = 2
NEG_BIG = -1e30


def _params(sem, vmem=None):
    return pltpu.CompilerParams(dimension_semantics=sem, vmem_limit_bytes=vmem)


def _rms(x, g):
    return x * lax.rsqrt(jnp.mean(x * x, axis=-1, keepdims=True) + EPS) * g


def _const_spec(shape):
    nd = len(shape)
    return pl.BlockSpec(shape, lambda *_: (0,) * nd)


def _adaln_kernel(c_ref, w_ref, b_ref, o_ref):
    c = c_ref[...]
    s = c * jax.nn.sigmoid(c)
    o_ref[...] = jnp.dot(s, w_ref[...], preferred_element_type=F32, precision=HIGHEST) + b_ref[...]


def adaln(cond8, ada_w, ada_b):
    d, n = ada_w.shape
    bn = n // 4
    m = pl.pallas_call(
        _adaln_kernel,
        out_shape=jax.ShapeDtypeStruct((SUBLANES, n), F32),
        grid=(n // bn,),
        in_specs=[_const_spec((SUBLANES, d)),
                  pl.BlockSpec((d, bn), lambda j: (0, j)),
                  pl.BlockSpec((1, bn), lambda j: (0, j))],
        out_specs=pl.BlockSpec((SUBLANES, bn), lambda j: (0, j)),
        compiler_params=_params(("arbitrary",), VMEM_LIMIT),
        name="adaln",
    )(cond8, ada_w, ada_b.reshape(1, n))
    m = m.reshape(SUBLANES, N_MOD, d)
    return jnp.pad(m, ((0, 0), (0, SUBLANES - N_MOD), (0, 0)))


def _l0_inproj_kernel(*refs, rope):
    if rope:
        (x_ref, m_ref, n1_ref, win_ref, qg_ref, kvg_ref, wuq_ref, wuqs_ref, wuk_ref, e_ref, wuv_ref,
         cq_ref, sq_ref, ck_ref, sk_ref, q_out, k_out, v_out, ckv_out, kr_out, u_out) = refs
    else:
        (x_ref, m_ref, n1_ref, win_ref, qg_ref, kvg_ref, wuq_ref, wuk_ref, e_ref, wuv_ref,
         q_out, k_out, v_out, ckv_out, kr_out, u_out) = refs
    x = x_ref[0]
    m = m_ref[0]
    h = _rms(x, n1_ref[...]) * (1.0 + m[1:2]) + m[0:1]
    z = jnp.dot(h.astype(BF16), win_ref[...], preferred_element_type=F32)
    c_q = z[:, 0:MLA_Q_RANK]
    c_kv = z[:, MLA_Q_RANK:MLA_Q_RANK + MLA_KV_RANK]
    kr_blk = z[:, 640:768]
    val = z[:, 768:768 + CONV_CH]
    gate = z[:, 768 + CONV_CH:768 + 2 * CONV_CH]

    cqn = _rms(c_q, qg_ref[...]).astype(BF16)
    q = jnp.dot(cqn, wuq_ref[...], preferred_element_type=F32)
    if rope:
        qs = jnp.dot(cqn, wuqs_ref[...], preferred_element_type=F32)
        cq = cq_ref[...]
        sq = sq_ref[...]
        for hd in range(MLA_HEADS):
            sl = slice(hd * LANES, (hd + 1) * LANES)
            q_out[0, :, sl] = (q[:, sl] * cq + qs[:, sl] * sq).astype(BF16)
        kr = kr_blk * ck_ref[...] + pltpu.roll(kr_blk, LANES - MLA_ROPE, 1) * sk_ref[...]
    else:
        q_out[0] = q.astype(BF16)
        kr = kr_blk

    ckv = _rms(c_kv, kvg_ref[...])
    ckv_out[0] = ckv
    kr_out[0] = kr_blk[:, 0:MLA_ROPE]
    ckv_b = ckv.astype(BF16)
    k = (jnp.dot(ckv_b, wuk_ref[...], preferred_element_type=F32)
         + jnp.dot(kr.astype(BF16), e_ref[...], preferred_element_type=F32))
    k_out[0] = k.astype(BF16)
    v_out[0] = _with_sum_lane(jnp.dot(ckv_b, wuv_ref[...], preferred_element_type=F32)).astype(BF16)
    u_out[0] = val * jax.nn.sigmoid(gate)


def l0_inproj(x, mods, mod_off, n1, w, tables):
    bm, sm, d = x.shape
    tr = min(ROW_TILE, sm)
    rope = tables is not None
    hp = MLA_HEADS * LANES
    row = lambda width: pl.BlockSpec((1, tr, width), lambda b, i: (b, i, 0))
    in_specs = [row(d),
                pl.BlockSpec((1, SUBLANES, d), lambda b, i: (b + mod_off, 0, 0)),
                _const_spec((1, d)), _const_spec(w["win"].shape),
                _const_spec((1, MLA_Q_RANK)), _const_spec((1, MLA_KV_RANK)),
                _const_spec(w["wuq"].shape)]
    args = [x, mods, n1, w["win"], w["qg"], w["kvg"], w["wuq"]]
    if rope:
        in_specs.append(_const_spec(w["wuqs"].shape))
        args.append(w["wuqs"])
    in_specs += [_const_spec(w["wuk"].shape), _const_spec(w["e"].shape), _const_spec(w["wuv"].shape)]
    args += [w["wuk"], w["e"], w["wuv"]]
    if rope:
        in_specs += [pl.BlockSpec((tr, LANES), lambda b, i: (i, 0))] * 4
        args += list(tables)
    out_shape = [jax.ShapeDtypeStruct((bm, sm, hp), BF16),
                 jax.ShapeDtypeStruct((bm, sm, hp), BF16),
                 jax.ShapeDtypeStruct((bm, sm, hp), BF16),
                 jax.ShapeDtypeStruct((bm, sm, MLA_KV_RANK), F32),
                 jax.ShapeDtypeStruct((bm, sm, MLA_ROPE), F32),
                 jax.ShapeDtypeStruct((bm, sm, CONV_CH), F32)]
    out_specs = [row(hp), row(hp), row(hp), row(MLA_KV_RANK), row(MLA_ROPE), row(CONV_CH)]
    return pl.pallas_call(
        functools.partial(_l0_inproj_kernel, rope=rope),
        out_shape=out_shape, grid=(bm, sm // tr), in_specs=in_specs, out_specs=out_specs,
        compiler_params=_params(("parallel", "parallel"), VMEM_LIMIT),
        name="l0_inproj_rope" if rope else "l0_inproj",
    )(*args)


def _mla_ctx_kv_kernel(ckv_ref, kr_ref, wuk_ref, e_ref, wuv_ref, k_out, v_out):
    ckv_b = ckv_ref[0].astype(BF16)
    k = (jnp.dot(ckv_b, wuk_ref[...], preferred_element_type=F32)
         + jnp.dot(kr_ref[0].astype(BF16), e_ref[...], preferred_element_type=F32))
    k_out[0] = k.astype(BF16)
    v_out[0] = _with_sum_lane(jnp.dot(ckv_b, wuv_ref[...], preferred_element_type=F32)).astype(BF16)


def mla_ctx_kv(ckv, kr128, w):
    b, s, _ = ckv.shape
    hp = MLA_HEADS * LANES
    blk = lambda width: pl.BlockSpec((1, s, width), lambda i: (i, 0, 0))
    return pl.pallas_call(
        _mla_ctx_kv_kernel,
        out_shape=[jax.ShapeDtypeStruct((b, s, hp), BF16)] * 2,
        grid=(b,),
        in_specs=[blk(MLA_KV_RANK), blk(LANES), _const_spec(w["wuk"].shape),
                  _const_spec(w["e"].shape), _const_spec(w["wuv"].shape)],
        out_specs=[blk(hp), blk(hp)],
        compiler_params=_params(("parallel",), VMEM_LIMIT),
        name="mla_ctx_kv",
    )(ckv, kr128, w["wuk"], w["e"], w["wuv"])


def _conv_kernel(prev_ref, cur_ref, next_ref, w_ref, b_ref, g_ref, beta_ref, o_ref, pad_ref, sh_ref, *, rb):
    i = pl.program_id(1)
    last = pl.num_programs(1) - 1
    zeros = jnp.zeros((CONV_HALO, CONV_CH), F32)
    pad_ref[0:CONV_HALO, :] = jnp.where(i == 0, zeros, prev_ref[0])
    pad_ref[CONV_HALO:CONV_HALO + rb, :] = cur_ref[0]
    pad_ref[CONV_HALO + rb:CONV_HALO + rb + CONV_HALO, :] = jnp.where(i == last, zeros, next_ref[0])
    span = rb + 2 * CONV_HALO - SUBLANES
    for r in range(1, SUBLANES):
        sh_ref[r] = pad_ref[r:r + span, :]
    w = w_ref[...]
    shift = CONV_HALO - CONV_WIDTH // 2
    acc = jnp.zeros((rb, CONV_CH), F32) + b_ref[...]
    for k in range(CONV_WIDTH):
        off = k + shift
        r, a = off % SUBLANES, off // SUBLANES * SUBLANES
        window = pad_ref[a:a + rb, :] if r == 0 else sh_ref[r, a:a + rb, :]
        acc = acc + window * w[k:k + 1, :]
    mu = jnp.mean(acc, axis=-1, keepdims=True)
    cen = acc - mu
    var = jnp.mean(cen * cen, axis=-1, keepdims=True)
    y = cen * lax.rsqrt(var + EPS) * g_ref[...] + beta_ref[...]
    o_ref[0] = (y * jax.nn.sigmoid(y)).astype(BF16)


def conformer_conv(u, conv_w, conv_b, ln_g, ln_b):
    b, s, c = u.shape
    rb = min(256, s)
    nh = rb // CONV_HALO
    n_halo_blocks = s // CONV_HALO
    wpad = jnp.pad(conv_w.reshape(CONV_WIDTH, c), ((0, 32 - CONV_WIDTH), (0, 0)))
    return pl.pallas_call(
        functools.partial(_conv_kernel, rb=rb),
        out_shape=jax.ShapeDtypeStruct((b, s, c), BF16),
        grid=(b, s // rb),
        in_specs=[pl.BlockSpec((1, CONV_HALO, c), lambda bi, i: (bi, jnp.maximum(i * nh - 1, 0), 0)),
                  pl.BlockSpec((1, rb, c), lambda bi, i: (bi, i, 0)),
                  pl.BlockSpec((1, CONV_HALO, c),
                               lambda bi, i: (bi, jnp.minimum((i + 1) * nh, n_halo_blocks - 1), 0)),
                  _const_spec((32, c)), _const_spec((1, c)), _const_spec((1, c)), _const_spec((1, c))],
        out_specs=pl.BlockSpec((1, rb, c), lambda bi, i: (bi, i, 0)),
        scratch_shapes=[pltpu.VMEM((rb + 2 * CONV_HALO, c), F32),
                        pltpu.VMEM((SUBLANES, rb + 2 * CONV_HALO - SUBLANES, c), F32)],
        compiler_params=_params(("parallel", "parallel"), VMEM_LIMIT),
        name="conformer_conv",
    )(u, u, u, wpad, conv_b.reshape(1, c), ln_g.reshape(1, c), ln_b.reshape(1, c))


def _with_sum_lane(v):
    lane = lax.broadcasted_iota(jnp.int32, (1, v.shape[-1]), 1)
    return v + ((lane & (LANES - 1)) == V_SUM_LANE).astype(v.dtype)


def _attn_kernel(q_ref, k_ref, v_ref, o_ref, *, heads, rep, scale, mxu_denominator):
    for hd in range(heads):
        g = hd // rep
        q = q_ref[0, :, hd * LANES:(hd + 1) * LANES]
        k = k_ref[0, :, g * LANES:(g + 1) * LANES]
        s = lax.dot_general(q, k, (((1,), (1,)), ((), ())), preferred_element_type=F32) * scale
        m = jnp.max(s, axis=-1, keepdims=True)
        if mxu_denominator:
            p = jnp.exp((s - m).astype(BF16))
            o = jnp.dot(p, v_ref[0, :, g * LANES:(g + 1) * LANES], preferred_element_type=F32)
            l = o[:, V_SUM_LANE:V_SUM_LANE + 1]
        else:
            p = jnp.exp(s - m)
            l = jnp.sum(p, axis=-1, keepdims=True)
            o = jnp.dot(p.astype(BF16), v_ref[0, :, g * LANES:(g + 1) * LANES], preferred_element_type=F32)
        o_ref[0, :, hd * LANES:(hd + 1) * LANES] = (o / l).astype(BF16)


def attention(q, k, v, *, n_heads, n_kv, scale, heads_per_step):
    b, sq, _ = q.shape
    sk = k.shape[1]
    rep = n_heads // n_kv
    tq = min(ATT_Q_TILE, sq)
    hb = heads_per_step
    grid = (b, n_heads // hb, sq // tq)
    if hb >= rep:
        kv_spec = pl.BlockSpec((1, sk, hb // rep * LANES), lambda bi, h, i: (bi, 0, h))
        kern_rep = rep
    else:
        assert rep % hb == 0
        kv_spec = pl.BlockSpec((1, sk, LANES), lambda bi, h, i: (bi, 0, h * hb // rep))
        kern_rep = hb
    kern = functools.partial(_attn_kernel, heads=hb, rep=kern_rep, scale=scale,
                             mxu_denominator=sk >= ATT_LONG_KEYS)
    q_spec = pl.BlockSpec((1, tq, heads_per_step * LANES), lambda bi, h, i: (bi, i, h))
    return pl.pallas_call(
        kern,
        out_shape=jax.ShapeDtypeStruct(q.shape, BF16),
        grid=grid, in_specs=[q_spec, kv_spec, kv_spec], out_specs=q_spec,
        compiler_params=_params(("parallel", "parallel", "parallel"), VMEM_LIMIT),
        name="attention",
    )(q, k, v)


def _l1_inproj_kernel(*refs, rope):
    if rope:
        (x_ref, m_ref, n1_ref, win_ref, wsw_ref, qg_ref, qgs_ref, kg_ref, kgs_ref, lng_ref, lnb_ref,
         ws_ref, bs_ref, c_ref, s_ref, q_out, k_out, kn_out, vp_out, v_out, g_out) = refs
    else:
        (x_ref, m_ref, n1_ref, win_ref, qg_ref, kg_ref, lng_ref, lnb_ref,
         ws_ref, bs_ref, q_out, k_out, kn_out, vp_out, v_out, g_out) = refs
    x = x_ref[0]
    m = m_ref[0]
    hb = (_rms(x, n1_ref[...]) * (1.0 + m[1:2]) + m[0:1]).astype(BF16)
    z = jnp.dot(hb, win_ref[...], preferred_element_type=F32)
    qw = GQA_HEADS * LANES
    kw = GQA_KV_HEADS * LANES
    o_k, o_vp, o_v, o_u, o_vg = qw, qw + kw, qw + 2 * kw, qw + 2 * kw + LANES, qw + 2 * kw + LANES + GMLP_CH
    if rope:
        zs = jnp.dot(hb, wsw_ref[...], preferred_element_type=F32)
        cos = c_ref[...]
        sin = s_ref[...]

    def head(col, zcol, g_ref, gs_ref):
        t = z[:, col:col + LANES]
        r = lax.rsqrt(jnp.sum(t * t, axis=-1, keepdims=True) * (1.0 / GQA_HEAD_DIM) + EPS)
        normed = t * r * g_ref[...]
        if not rope:
            return normed, normed
        ts = zs[:, zcol:zcol + LANES]
        return normed, normed * cos + ts * r * gs_ref[...] * sin

    for hd in range(GQA_HEADS):
        _, rot = head(hd * LANES, hd * LANES, qg_ref, qgs_ref if rope else None)
        q_out[0, :, hd * LANES:(hd + 1) * LANES] = rot.astype(BF16)
    for hd in range(GQA_KV_HEADS):
        normed, rot = head(o_k + hd * LANES, qw + hd * LANES, kg_ref, kgs_ref if rope else None)
        k_out[0, :, hd * LANES:(hd + 1) * LANES] = rot.astype(BF16)
        kn_out[0, :, hd * LANES:(hd + 1) * LANES] = normed
    vp_out[0] = _with_sum_lane(z[:, o_vp:o_vp + kw]).astype(BF16)
    v_out[0] = z[:, o_v:o_v + LANES]

    u = z[:, o_u:o_u + GMLP_CH]
    vg = z[:, o_vg:o_vg + GMLP_CH]
    mu = jnp.mean(vg, axis=-1, keepdims=True)
    cen = vg - mu
    var = jnp.mean(cen * cen, axis=-1, keepdims=True)
    vn = (cen * lax.rsqrt(var + EPS) * lng_ref[...] + lnb_ref[...]).astype(BF16)
    bias = bs_ref[...]
    rows = x.shape[0]
    for cidx in range(rows // CHUNK):
        r0 = cidx * CHUNK
        for g in range(GMLP_GROUPS):
            c0 = g * LANES
            mixed = jnp.dot(ws_ref[g], vn[r0:r0 + CHUNK, c0:c0 + LANES], preferred_element_type=F32)
            g_out[0, r0:r0 + CHUNK, c0:c0 + LANES] = (
                u[r0:r0 + CHUNK, c0:c0 + LANES] * (mixed + bias[:, c0:c0 + LANES])).astype(BF16)


def l1_inproj(x, mods, mod_off, n1, w, tables):
    bm, sm, d = x.shape
    tr = min(ROW_TILE, sm)
    rope = tables is not None
    qw = GQA_HEADS * LANES
    kw = GQA_KV_HEADS * LANES
    row = lambda width: pl.BlockSpec((1, tr, width), lambda b, i: (b, i, 0))
    vec = _const_spec((1, LANES))
    in_specs = [row(d), pl.BlockSpec((1, SUBLANES, d), lambda b, i: (b + mod_off, 0, 0)),
                _const_spec((1, d)), _const_spec(w["win"].shape)]
    args = [x, mods, n1, w["win"]]
    if rope:
        in_specs += [_const_spec(w["wsw"].shape), vec, vec, vec, vec]
        args += [w["wsw"], w["qg"], w["qgs"], w["kg"], w["kgs"]]
    else:
        in_specs += [vec, vec]
        args += [w["qg"], w["kg"]]
    in_specs += [_const_spec((1, GMLP_CH)), _const_spec((1, GMLP_CH)),
                 _const_spec(w["ws"].shape), _const_spec((CHUNK, GMLP_CH))]
    args += [w["lng"], w["lnb"], w["ws"], w["bs"]]
    if rope:
        in_specs += [pl.BlockSpec((tr, LANES), lambda b, i: (i, 0))] * 2
        args += list(tables)
    out_shape = [jax.ShapeDtypeStruct((bm, sm, qw), BF16),
                 jax.ShapeDtypeStruct((bm, sm, kw), BF16),
                 jax.ShapeDtypeStruct((bm, sm, kw), F32),
                 jax.ShapeDtypeStruct((bm, sm, kw), BF16),
                 jax.ShapeDtypeStruct((bm, sm, LANES), F32),
                 jax.ShapeDtypeStruct((bm, sm, GMLP_CH), BF16)]
    out_specs = [row(qw), row(kw), row(kw), row(kw), row(LANES), row(GMLP_CH)]
    return pl.pallas_call(
        functools.partial(_l1_inproj_kernel, rope=rope),
        out_shape=out_shape, grid=(bm, sm // tr), in_specs=in_specs, out_specs=out_specs,
        compiler_params=_params(("parallel", "parallel"), VMEM_LIMIT),
        name="l1_inproj_rope" if rope else "l1_inproj",
    )(*args)


def _post_kernel(a_ref, c_ref, x_ref, m_ref, n2_ref, wa_ref, wc_ref, rwh_ref, rwl_ref, rb_ref, tri_ref, base_ref,
                 x1_out, xm_out, idx_out, wts_out, rank_out, cnt_out, run_ref):
    first =(pl.program_id(0) == 0) & (pl.program_id(1) == 0)

    @pl.when(first)
    def _():
        run_ref[...] = base_ref[...]

    m = m_ref[0]
    y = (jnp.dot(a_ref[0], wa_ref[...], preferred_element_type=F32)
         + jnp.dot(c_ref[0], wc_ref[...], preferred_element_type=F32))
    x1 = x_ref[0] + m[2:3] * y
    x1_out[0] = x1
    xm = _rms(x1, n2_ref[...]) * (1.0 + m[4:5]) + m[3:4]
    xm_out[0] = xm.reshape(xm.shape[0], SUBLANES, LANES)

    xh = xm.astype(BF16)
    xl = (xm - xh.astype(F32)).astype(BF16)
    logits = (jnp.dot(xh, rwh_ref[...], preferred_element_type=F32)
              + jnp.dot(xl, rwh_ref[...], preferred_element_type=F32)
              + jnp.dot(xh, rwl_ref[...], preferred_element_type=F32)) + rb_ref[...]
    rows = logits.shape[0]
    lane = lax.broadcasted_iota(jnp.int32, (rows, LANES), 1).astype(F32)
    work = logits
    vals, hots = [], []
    idx_acc = jnp.zeros((rows, LANES), F32)
    for k in range(TOP_K):
        top = jnp.max(work, axis=-1, keepdims=True)
        sel = jnp.min(jnp.where(work == top, lane, float(LANES)), axis=-1, keepdims=True)
        hot = lane == sel
        vals.append(top)
        hots.append(hot)
        idx_acc = idx_acc + jnp.where(lane == float(k), sel, 0.0)
        work = jnp.where(hot, -jnp.inf, work)
    exps = [jnp.exp(v - vals[0]) for v in vals]
    denom = exps[0] + exps[1] + exps[2] + exps[3]
    wcols = [jnp.broadcast_to(exps[k] / denom, (rows, LANES)) for k in range(TOP_K)]
    wcols += [jnp.zeros((rows, LANES), F32)] * (SUBLANES - TOP_K)
    wts = jnp.concatenate(wcols, axis=1).reshape(rows, SUBLANES, LANES)

    chosen = jnp.zeros((rows, LANES), F32)
    for hot in hots:
        chosen = chosen + hot.astype(F32)
    before = jnp.dot(tri_ref[...], chosen.astype(BF16), preferred_element_type=F32) + run_ref[0:1, :]
    rank = jnp.zeros((rows, LANES), F32)
    for k in range(TOP_K):
        rk = jnp.sum(jnp.where(hots[k], before, 0.0), axis=-1, keepdims=True)
        rank = rank + jnp.where(lane == float(k), rk, 0.0)
    run_ref[0:1, :] = run_ref[0:1, :] + jnp.sum(chosen, axis=0, keepdims=True)
    idx_out[0] = idx_acc.astype(jnp.int32)
    wts_out[0] = wts
    rank_out[0] = rank.astype(jnp.int32)
    cnt_out[...] = run_ref[...]


def post_mixer(attn, other, x, mods, mod_off, n2, wa, wc, rwh, rwl, rb, tri, base):
    bm, sm, d = x.shape
    tr = tri.shape[0]
    row = lambda width: pl.BlockSpec((1, tr, width), lambda b, i: (b, i, 0))
    tile_rows = pl.BlockSpec((1, tr, SUBLANES, d // SUBLANES), lambda b, i: (b, i, 0, 0))
    out_shape = [jax.ShapeDtypeStruct((bm, sm, d), F32),
                 jax.ShapeDtypeStruct((bm, sm, SUBLANES, d // SUBLANES), F32),
                 jax.ShapeDtypeStruct((bm, sm, LANES), jnp.int32),
                 jax.ShapeDtypeStruct((bm, sm, SUBLANES, LANES), F32),
                 jax.ShapeDtypeStruct((bm, sm, LANES), jnp.int32),
                 jax.ShapeDtypeStruct((SUBLANES, LANES), F32)]
    return pl.pallas_call(
        _post_kernel,
        out_shape=out_shape, grid=(bm, sm // tr),
        in_specs=[row(attn.shape[-1]), row(other.shape[-1]), row(d),
                  pl.BlockSpec((1, SUBLANES, d), lambda b, i: (b + mod_off, 0, 0)),
                  _const_spec((1, d)), _const_spec(wa.shape), _const_spec(wc.shape),
                  _const_spec(rwh.shape), _const_spec(rwl.shape), _const_spec((1, LANES)),
                  _const_spec(tri.shape), _const_spec((SUBLANES, LANES))],
        out_specs=[row(d), tile_rows, row(LANES),
                   pl.BlockSpec((1, tr, SUBLANES, LANES), lambda b, i: (b, i, 0, 0)),
                   row(LANES), _const_spec((SUBLANES, LANES))],
        scratch_shapes=[pltpu.VMEM((SUBLANES, LANES), F32)],
        compiler_params=_params(("arbitrary", "arbitrary"), VMEM_LIMIT),
        name="post_mixer_route",
    )(attn, other, x, mods, n2, wa, wc, rwh, rwl, rb, tri, base)


def _plan_kernel(off_ref, idx_ref, rank_ref, pos_out):
    idx = idx_ref[...]
    pos = rank_ref[...]
    for e in range(N_EXPERTS):
        pos = pos + jnp.where(idx == e, off_ref[e], 0)
    pos_out[...] = pos


def plan_positions(offsets, idx, rank):
    n = idx.shape[0]
    tr = min(2048, n)
    spec = pl.BlockSpec((tr, LANES), lambda i, off: (i, 0))
    return pl.pallas_call(
        _plan_kernel,
        out_shape=jax.ShapeDtypeStruct((n, LANES), jnp.int32),
        grid_spec=pltpu.PrefetchScalarGridSpec(
            num_scalar_prefetch=1, grid=(n // tr,), in_specs=[spec, spec], out_specs=spec),
        compiler_params=_params(("parallel",)),
        name="plan_positions",
    )(offsets, idx, rank)


def _dispatch_kernel(pos_ref, xm_ref, xs_in, xs_out, sem):
    del xs_in
    t0 = pl.program_id(0) * MOVE_TILE

    def issue(c, carry):
        for j in range(MOVE_UNROLL):
            t = c * MOVE_UNROLL + j
            for k in range(TOP_K):
                p = pos_ref[(t0 + t) * TOP_K + k]
                pltpu.make_async_copy(xm_ref.at[pl.ds(t, 1)], xs_out.at[pl.ds(p, 1)], sem).start()
        return carry

    lax.fori_loop(0, MOVE_TILE // MOVE_UNROLL, issue, 0)
    for k in range(TOP_K):
        pltpu.make_async_copy(xm_ref, xs_out.at[pl.ds(0, MOVE_TILE)], sem).wait()


def dispatch_rows(pos_flat, xm, xs):
    n = xm.shape[0]
    return pl.pallas_call(
        _dispatch_kernel,
        out_shape=jax.ShapeDtypeStruct(xs.shape, xs.dtype),
        grid_spec=pltpu.PrefetchScalarGridSpec(
            num_scalar_prefetch=1, grid=(n // MOVE_TILE,),
            in_specs=[pl.BlockSpec((MOVE_TILE,) + xm.shape[1:], lambda i, pos: (i, 0, 0)),
                      pl.BlockSpec(memory_space=pl.ANY)],
            out_specs=pl.BlockSpec(memory_space=pl.ANY),
            scratch_shapes=[pltpu.SemaphoreType.DMA]),
        input_output_aliases={2: 0},
        compiler_params=_params(("arbitrary",)),
        name="dispatch_rows",
    )(pos_flat, xm, xs)


def _ffn_kernel(te_ref, nu_ref, first_ref, slot_ref, nxt_ref, xs_ref, w1_hbm, b1_ref, w2_hbm, b2_ref, y_ref,
                w1f, w2f, w1b, w2b, sem):
    i = pl.program_id(0)

    def weight_copies(e, s):
        return (pltpu.make_async_copy(w1_hbm.at[e], w1f.at[s], sem.at[0, s]),
                pltpu.make_async_copy(w2_hbm.at[e], w2f.at[s], sem.at[1, s]))

    @pl.when(i < nu_ref[0])
    def _():
        s = slot_ref[i]

        @pl.when(first_ref[i] == 1)
        def _():
            @pl.when(i == 0)
            def _():
                for cp in weight_copies(te_ref[i], s):
                    cp.start()
            for cp in weight_copies(te_ref[i], s):
                cp.wait()

            @pl.when(nxt_ref[i] >= 0)
            def _():
                for cp in weight_copies(nxt_ref[i], 1 - s):
                    cp.start()
            for c in range(D_MODEL // LANES):
                w1b[c * LANES:(c + 1) * LANES, :] = w1f[s, c * LANES:(c + 1) * LANES, :].astype(BF16)
            for c in range(D_EXPERT // LANES):
                w2b[c * LANES:(c + 1) * LANES, :] = w2f[s, c * LANES:(c + 1) * LANES, :].astype(BF16)

        x = xs_ref[...].reshape(FFN_TILE, D_MODEL).astype(BF16)
        h = jnp.dot(x, w1b[...], preferred_element_type=F32) + b1_ref[0]
        g = jnp.minimum(h[:, :D_EXPERT], SWIGLU_LIMIT)
        lin = jnp.clip(h[:, D_EXPERT:], -SWIGLU_LIMIT, SWIGLU_LIMIT)
        a = (lin + 1.0) * (g * jax.nn.sigmoid(SWIGLU_ALPHA * g))
        y = jnp.dot(a.astype(BF16), w2b[...], preferred_element_type=F32) + b2_ref[0]
        y_ref[...] = y.reshape(y_ref.shape)

    @pl.when(i >= nu_ref[0])
    def _():
        y_ref[...] = jnp.zeros(y_ref.shape, F32)


def grouped_ffn(sched, xs, w1, b1, w2, b2):
    r = xs.shape[0]
    d = w1.shape[1]
    nt = r // FFN_TILE
    tile = (FFN_TILE,) + xs.shape[1:]
    rows = lambda i, te, nu, *_: (jnp.minimum(i, nu[0] - 1), 0, 0)
    bsel = lambda i, te, *_: (te[i], 0, 0)
    return pl.pallas_call(
        _ffn_kernel,
        out_shape=jax.ShapeDtypeStruct(xs.shape, F32),
        grid_spec=pltpu.PrefetchScalarGridSpec(
            num_scalar_prefetch=5, grid=(nt,),
            in_specs=[pl.BlockSpec(tile, rows),
                      pl.BlockSpec(memory_space=pl.ANY),
                      pl.BlockSpec((1, 1, 2 * D_EXPERT), bsel),
                      pl.BlockSpec(memory_space=pl.ANY),
                      pl.BlockSpec((1, 1, d), bsel)],
            out_specs=pl.BlockSpec(tile, lambda i, *_: (i, 0, 0)),
            scratch_shapes=[pltpu.VMEM((2, d, 2 * D_EXPERT), F32), pltpu.VMEM((2, D_EXPERT, d), F32),
                            pltpu.VMEM((d, 2 * D_EXPERT), BF16), pltpu.VMEM((D_EXPERT, d), BF16),
                            pltpu.SemaphoreType.DMA((2, 2))]),
        compiler_params=_params(("arbitrary",), VMEM_LIMIT),
        name="grouped_ffn",
    )(*sched, xs, w1, b1.reshape(N_EXPERTS, 1, -1), w2, b2.reshape(N_EXPERTS, 1, -1))


def _combine_kernel(pos_ref, x1_ref, wts_ref, m_ref, fn_ref, y_hbm, o_ref, buf, sem, *, final):
    t0 = pl.program_id(1) * MOVE_TILE + pl.program_id(0) * pl.num_programs(1) * MOVE_TILE

    def issue(c, carry):
        for j in range(MOVE_UNROLL):
            t = c * MOVE_UNROLL + j
            for k in range(TOP_K):
                p = pos_ref[(t0 + t) * TOP_K + k]
                pltpu.make_async_copy(y_hbm.at[pl.ds(p, 1)], buf.at[k, pl.ds(t, 1)], sem).start()
        return carry

    lax.fori_loop(0, MOVE_TILE // MOVE_UNROLL, issue, 0)
    for k in range(TOP_K):
        pltpu.make_async_copy(y_hbm.at[pl.ds(0, MOVE_TILE)], buf.at[k], sem).wait()
    w = wts_ref[0]
    acc = w[:, 0:1, :] * buf[0]
    for k in range(1, TOP_K):
        acc = acc + w[:, k:k + 1, :] * buf[k]
    out = x1_ref[0] + m_ref[0][5:6] * acc.reshape(MOVE_TILE, D_MODEL)
    if final:
        out = _rms(out, fn_ref[...])
    o_ref[0] = out


def combine_rows(pos_flat, x1, wts, mods, mod_off, fn, y, *, final):
    bm, sm, d = x1.shape
    row = lambda width: pl.BlockSpec((1, MOVE_TILE, width), lambda b, i, pos: (b, i, 0))
    return pl.pallas_call(
        functools.partial(_combine_kernel, final=final),
        out_shape=jax.ShapeDtypeStruct((bm, sm, d), F32),
        grid_spec=pltpu.PrefetchScalarGridSpec(
            num_scalar_prefetch=1, grid=(bm, sm // MOVE_TILE),
            in_specs=[row(d),
                      pl.BlockSpec((1, MOVE_TILE, SUBLANES, LANES), lambda b, i, pos: (b, i, 0, 0)),
                      pl.BlockSpec((1, SUBLANES, d), lambda b, i, pos: (b + mod_off, 0, 0)),
                      pl.BlockSpec((1, d), lambda b, i, pos: (0, 0)),
                      pl.BlockSpec(memory_space=pl.ANY)],
            out_specs=row(d),
            scratch_shapes=[pltpu.VMEM((TOP_K, MOVE_TILE) + y.shape[1:], F32), pltpu.SemaphoreType.DMA]),
        compiler_params=_params(("arbitrary", "arbitrary"), VMEM_LIMIT),
        name="combine_rows",
    )(pos_flat, x1, wts, mods, fn, y)


def _axial_angles(n_tokens, rot_dim):
    t = jnp.arange(n_tokens)
    rows = (t // GRID_W).astype(F32)
    cols = (t % GRID_W).astype(F32)
    n_freq = rot_dim // 4
    inv = ROPE_THETA ** (-jnp.arange(n_freq, dtype=F32) / n_freq)
    return jnp.concatenate([rows[:, None] * inv, cols[:, None] * inv], axis=-1)


def _lane_table(parts, n):
    cols = []
    for p in parts:
        cols.append(jnp.broadcast_to(jnp.asarray(p, F32), (n, p.shape[-1])) if hasattr(p, "shape") else p)
    return jnp.concatenate(cols, axis=-1)


def _swap_halves(w):
    half = w.shape[-1] // 2
    return jnp.concatenate([-w[..., half:], w[..., :half]], axis=-1)


def _prep_l0(w_in, q_norm, kv_norm, w_uq, w_uk, w_uv, w_out):
    d = w_in.shape[0]
    o_kr = MLA_Q_RANK + MLA_KV_RANK
    kr_cols = w_in[:, o_kr:o_kr + MLA_ROPE]
    win = jnp.concatenate(
        [w_in[:, :o_kr], kr_cols, _swap_halves(kr_cols), jnp.zeros((d, LANES - 2 * MLA_ROPE), F32),
         w_in[:, o_kr + MLA_ROPE:]], axis=1).astype(BF16)
    qk = MLA_NOPE + MLA_ROPE
    wuq3 = w_uq.reshape(MLA_Q_RANK, MLA_HEADS, qk)
    wuq = jnp.pad(wuq3, ((0, 0), (0, 0), (0, LANES - qk))).reshape(MLA_Q_RANK, -1).astype(BF16)
    wuqs3 = jnp.concatenate(
        [jnp.zeros((MLA_Q_RANK, MLA_HEADS, MLA_NOPE), F32), _swap_halves(wuq3[:, :, MLA_NOPE:]),
         jnp.zeros((MLA_Q_RANK, MLA_HEADS, LANES - qk), F32)], axis=-1)
    wuqs = wuqs3.reshape(MLA_Q_RANK, -1).astype(BF16)
    wuk3 = w_uk.reshape(MLA_KV_RANK, MLA_HEADS, MLA_NOPE)
    wuk = jnp.pad(wuk3, ((0, 0), (0, 0), (0, LANES - MLA_NOPE))).reshape(MLA_KV_RANK, -1).astype(BF16)
    wuv3 = w_uv.reshape(MLA_KV_RANK, MLA_HEADS, MLA_V)
    wuv = jnp.pad(wuv3, ((0, 0), (0, 0), (0, LANES - MLA_V))).reshape(MLA_KV_RANK, -1).astype(BF16)
    eye = jnp.eye(MLA_ROPE, dtype=F32)
    e_head = jnp.concatenate([jnp.zeros((MLA_ROPE, MLA_NOPE), F32), eye,
                              jnp.zeros((MLA_ROPE, LANES - qk), F32)], axis=1)
    e = jnp.pad(jnp.tile(e_head, (1, MLA_HEADS)), ((0, LANES - MLA_ROPE), (0, 0))).astype(BF16)
    wa3 = w_out[:MLA_HEADS * MLA_V].reshape(MLA_HEADS, MLA_V, d)
    wa = jnp.pad(wa3, ((0, 0), (0, LANES - MLA_V), (0, 0))).reshape(MLA_HEADS * LANES, d).astype(BF16)
    wc = w_out[MLA_HEADS * MLA_V:].astype(BF16)
    return dict(win=win, qg=q_norm.reshape(1, -1), kvg=kv_norm.reshape(1, -1), wuq=wuq, wuqs=wuqs,
                wuk=wuk, e=e, wuv=wuv, wa=wa, wc=wc)


def _l0_tables(n):
    ang = _axial_angles(n, MLA_ROPE)
    cos, sin = jnp.cos(ang), jnp.sin(ang)
    one = jnp.ones((n, 1), F32)
    zero = jnp.zeros((n, 1), F32)
    rest = LANES - MLA_NOPE - MLA_ROPE
    cq = jnp.concatenate([jnp.tile(one, (1, MLA_NOPE)), cos, cos, jnp.tile(one, (1, rest))], axis=1)
    sq = jnp.concatenate([jnp.tile(zero, (1, MLA_NOPE)), sin, sin, jnp.tile(zero, (1, rest))], axis=1)
    ck = jnp.concatenate([cos, cos, jnp.tile(zero, (1, LANES - MLA_ROPE))], axis=1)
    sk = jnp.concatenate([sin, sin, jnp.tile(zero, (1, LANES - MLA_ROPE))], axis=1)
    return cq, sq, ck, sk


def _pad_heads(w, n_heads, dim):
    d = w.shape[0]
    return jnp.pad(w.reshape(d, n_heads, dim), ((0, 0), (0, 0), (0, LANES - dim))).reshape(d, n_heads * LANES)


def _prep_l1(w_in, q_norm, k_norm, ln_g, ln_b, w_s, b_s, w_out):
    d = w_in.shape[0]
    qd = GQA_HEADS * GQA_HEAD_DIM
    kd = GQA_KV_HEADS * GQA_HEAD_DIM
    wq, wk, wv = w_in[:, :qd], w_in[:, qd:qd + kd], w_in[:, qd + kd:qd + 2 * kd]
    rest = w_in[:, qd + 2 * kd:]
    win = jnp.concatenate([_pad_heads(wq, GQA_HEADS, GQA_HEAD_DIM), _pad_heads(wk, GQA_KV_HEADS, GQA_HEAD_DIM),
                           _pad_heads(wv, GQA_KV_HEADS, GQA_HEAD_DIM), wv, rest], axis=1).astype(BF16)
    swq = _swap_halves(wq.reshape(d, GQA_HEADS, GQA_HEAD_DIM)).reshape(d, qd)
    swk = _swap_halves(wk.reshape(d, GQA_KV_HEADS, GQA_HEAD_DIM)).reshape(d, kd)
    wsw = jnp.concatenate([_pad_heads(swq, GQA_HEADS, GQA_HEAD_DIM),
                           _pad_heads(swk, GQA_KV_HEADS, GQA_HEAD_DIM)], axis=1).astype(BF16)
    half = GQA_HEAD_DIM // 2
    padg = lambda g: jnp.pad(g, (0, LANES - GQA_HEAD_DIM)).reshape(1, LANES)
    swapg = lambda g: jnp.concatenate([g[half:], g[:half]])
    wa3 = w_out[:qd].reshape(GQA_HEADS, GQA_HEAD_DIM, d)
    wa = jnp.pad(wa3, ((0, 0), (0, LANES - GQA_HEAD_DIM), (0, 0))).reshape(GQA_HEADS * LANES, d).astype(BF16)
    wc = w_out[qd:].astype(BF16)
    bs = jnp.repeat(b_s.T, LANES, axis=1)
    return dict(win=win, wsw=wsw, qg=padg(q_norm), qgs=padg(swapg(q_norm)), kg=padg(k_norm),
                kgs=padg(swapg(k_norm)), lng=ln_g.reshape(1, -1), lnb=ln_b.reshape(1, -1),
                ws=w_s.astype(BF16), bs=bs, wa=wa, wc=wc)


def _l1_tables(n):
    ang = _axial_angles(n, GQA_HEAD_DIM)
    cos, sin = jnp.cos(ang), jnp.sin(ang)
    pad = LANES - GQA_HEAD_DIM
    c = jnp.concatenate([cos, cos, jnp.ones((n, pad), F32)], axis=1)
    s = jnp.concatenate([sin, sin, jnp.zeros((n, pad), F32)], axis=1)
    return c, s


def _pad_lanes(x, width):
    return jnp.pad(x, [(0, 0)] * (x.ndim - 1) + [(0, width - x.shape[-1])])


def routed_ffn(groups, mods, n2, moe, tri, final_norm, row_buffer, *, final):
    router_w, router_b, w1, b1, w2, b2 = moe
    rw = _pad_lanes(router_w, LANES)
    rwh = rw.astype(BF16)
    rwl = (rw - rwh.astype(F32)).astype(BF16)
    rb = jnp.concatenate([router_b, jnp.full((LANES - N_EXPERTS,), NEG_BIG, F32)]).reshape(1, LANES)
    base = jnp.zeros((SUBLANES, LANES), F32)
    routed = []
    for g in groups:
        x1, xm, idx, wts, rank, base = post_mixer(
            g["attn"], g["other"], g["x"], mods, g["mod_off"], n2, g["wa"], g["wc"], rwh, rwl, rb, tri, base)
        routed.append((x1, xm, idx, wts, rank))
    n_total = sum(r[0].shape[0] * r[0].shape[1] for r in routed)
    nt = n_total * TOP_K // FFN_TILE + N_EXPERTS

    counts = base[0, :N_EXPERTS].astype(jnp.int32)
    tiles = (counts + FFN_TILE - 1) // FFN_TILE
    tile_end = jnp.cumsum(tiles)
    offsets = (tile_end - tiles) * FFN_TILE
    n_used = tile_end[-1:].astype(jnp.int32)
    experts = jnp.arange(N_EXPERTS)
    busy = tiles > 0
    slot_e = (jnp.cumsum(busy) - 1) % 2
    later = jnp.where(busy[None, :] & (experts[None, :] > experts[:, None]), experts[None, :], N_EXPERTS)
    nxt_e = jnp.min(later, axis=1)
    nxt_e = jnp.where(nxt_e == N_EXPERTS, -1, nxt_e)
    tile_expert = jnp.sum(jnp.arange(nt)[:, None] >= tile_end[None, :], axis=1)
    tile_expert = jnp.minimum(tile_expert, tile_expert[n_used[0] - 1])
    first = jnp.concatenate([jnp.ones((1,), bool), tile_expert[1:] != tile_expert[:-1]])
    sched = tuple(a.astype(jnp.int32) for a in
                  (tile_expert, n_used, first, slot_e[tile_expert], nxt_e[tile_expert]))

    row_tile = (SUBLANES, D_MODEL // SUBLANES)
    xs = jnp.zeros((nt * FFN_TILE,) + row_tile, F32) if row_buffer is None else row_buffer
    pos_flats = []
    for (x1, xm, idx, wts, rank) in routed:
        n = idx.shape[0] * idx.shape[1]
        pos = plan_positions(offsets, idx.reshape(n, LANES), rank.reshape(n, LANES))
        pos_flat = pos[:, :TOP_K].reshape(-1)
        pos_flats.append(pos_flat)
        xs = dispatch_rows(pos_flat, xm.reshape((n,) + row_tile), xs)
    y = grouped_ffn(sched, xs, w1, b1, w2, b2)
    outs = []
    for g, (x1, xm, idx, wts, rank), pos_flat in zip(groups, routed, pos_flats):
        outs.append(combine_rows(pos_flat, x1, wts, mods, g["mod_off"], final_norm, y, final=final))
    return outs, y


def kernel(x_prompt, x_sample, cache_l0_ckv, cache_l0_krope, cache_l1_k, cache_l1_v, c, c_ctx,
           l0_ada_w, l0_ada_b, l0_norm1, l0_w_in, l0_q_norm, l0_kv_norm, l0_w_uq, l0_w_uk, l0_w_uv,
           l0_conv_w, l0_conv_b, l0_conv_ln_g, l0_conv_ln_b, l0_w_out, l0_norm2,
           l0_router_w, l0_router_b, l0_w1, l0_b1, l0_w2, l0_b2,
           l1_ada_w, l1_ada_b, l1_norm1, l1_w_in, l1_q_norm, l1_k_norm, l1_gmlp_ln_g, l1_gmlp_ln_b,
           l1_w_s, l1_b_s, l1_w_out, l1_norm2,
           l1_router_w, l1_router_b, l1_w1, l1_b1, l1_w2, l1_b2,
           final_norm):
    bp, sp, d = x_prompt.shape
    bs, ss, _ = x_sample.shape
    past = cache_l0_ckv.shape[1]
    n_p = bp * sp

    cond8 = jnp.concatenate([c_ctx[None], c, jnp.zeros((SUBLANES - 1 - bs, d), F32)], axis=0)
    mods0 = adaln(cond8, l0_ada_w, l0_ada_b)
    mods1 = adaln(cond8, l1_ada_w, l1_ada_b)
    tri = jnp.tril(jnp.ones((ROW_TILE, ROW_TILE), F32), -1).astype(BF16)
    fn = final_norm.reshape(1, d)

    w0 = _prep_l0(l0_w_in, l0_q_norm, l0_kv_norm, l0_w_uq, l0_w_uk, l0_w_uv, l0_w_out)
    n1 = l0_norm1.reshape(1, d)
    hp = x_prompt.reshape(1, n_p, d)
    q_p, k_p, v_p, ckv_p, kr_p, u_p = l0_inproj(hp, mods0, 0, n1, w0, None)
    q_s, k_s, v_s, _, _, u_s = l0_inproj(x_sample, mods0, 1, n1, w0, _l0_tables(ss))
    k_c, v_c = mla_ctx_kv(cache_l0_ckv, _pad_lanes(cache_l0_krope, LANES), w0)
    hw = MLA_HEADS * LANES
    att_p = attention(q_p.reshape(bp, sp, hw), k_p.reshape(bp, sp, hw), v_p.reshape(bp, sp, hw),
                      n_heads=MLA_HEADS, n_kv=MLA_HEADS, scale=MLA_SCALE, heads_per_step=MLA_HEADS)
    att_s = attention(q_s, jnp.concatenate([k_c, k_s], axis=1), jnp.concatenate([v_c, v_s], axis=1),
                      n_heads=MLA_HEADS, n_kv=MLA_HEADS, scale=MLA_SCALE, heads_per_step=LATENT_HEADS_PER_STEP)
    conv_p = conformer_conv(u_p.reshape(bp, sp, CONV_CH), l0_conv_w, l0_conv_b, l0_conv_ln_g, l0_conv_ln_b)
    conv_s = conformer_conv(u_s, l0_conv_w, l0_conv_b, l0_conv_ln_g, l0_conv_ln_b)
    groups = [dict(attn=att_p.reshape(1, n_p, hw), other=conv_p.reshape(1, n_p, CONV_CH), x=hp, mod_off=0,
                   wa=w0["wa"], wc=w0["wc"]),
              dict(attn=att_s, other=conv_s, x=x_sample, mod_off=1, wa=w0["wa"], wc=w0["wc"])]
    (hp, hs), rows0 = routed_ffn(groups, mods0, l0_norm2.reshape(1, d),
                                 (l0_router_w, l0_router_b, l0_w1, l0_b1, l0_w2, l0_b2), tri, fn, None, final=False)
    new_l0_ckv = ckv_p.reshape(bp, sp, MLA_KV_RANK)
    new_l0_krope = kr_p.reshape(bp, sp, MLA_ROPE)

    w1p = _prep_l1(l1_w_in, l1_q_norm, l1_k_norm, l1_gmlp_ln_g, l1_gmlp_ln_b, l1_w_s, l1_b_s, l1_w_out)
    n1 = l1_norm1.reshape(1, d)
    q_p, k_p, kn_p, vp_p, vraw_p, gat_p = l1_inproj(hp, mods1, 0, n1, w1p, None)
    q_s, k_s, _, vp_s, _, gat_s = l1_inproj(hs, mods1, 1, n1, w1p, _l1_tables(ss))
    qw = GQA_HEADS * LANES
    kw = GQA_KV_HEADS * LANES
    pad_kv = lambda t: _pad_lanes(t, LANES).reshape(bs, past, kw).astype(BF16)
    att_p = attention(q_p.reshape(bp, sp, qw), k_p.reshape(bp, sp, kw), vp_p.reshape(bp, sp, kw),
                      n_heads=GQA_HEADS, n_kv=GQA_KV_HEADS, scale=GQA_SCALE, heads_per_step=GQA_HEADS)
    att_s = attention(q_s, jnp.concatenate([pad_kv(cache_l1_k), k_s], axis=1),
                      jnp.concatenate([_with_sum_lane(pad_kv(cache_l1_v)), vp_s], axis=1),
                      n_heads=GQA_HEADS, n_kv=GQA_KV_HEADS, scale=GQA_SCALE, heads_per_step=LATENT_HEADS_PER_STEP)
    groups = [dict(attn=att_p.reshape(1, n_p, qw), other=gat_p, x=hp, mod_off=0, wa=w1p["wa"], wc=w1p["wc"]),
              dict(attn=att_s, other=gat_s, x=hs, mod_off=1, wa=w1p["wa"], wc=w1p["wc"])]
    (yp, ys), _ = routed_ffn(groups, mods1, l1_norm2.reshape(1, d),
                             (l1_router_w, l1_router_b, l1_w1, l1_b1, l1_w2, l1_b2), tri, fn, rows0, final=True)
    new_l1_k = kn_p.reshape(bp, sp, GQA_KV_HEADS, LANES)[..., :GQA_HEAD_DIM]
    new_l1_v = vraw_p.reshape(bp, sp, GQA_KV_HEADS, GQA_HEAD_DIM)
    return (yp.reshape(bp, sp, d), ys, new_l0_ckv, new_l0_krope, new_l1_k, new_l1_v)
```

```python
import functools
import math

import jax
import jax.numpy as jnp
from jax import lax
from jax.experimental import pallas as pl
from jax.experimental.pallas import tpu as pltpu
from jax.experimental.pallas import tpu_sc as plsc

F32 = jnp.float32
BF16 = jnp.bfloat16
HIGHEST = lax.Precision.HIGHEST

LANES = 128
SUBLANES = 8
VMEM_LIMIT = 56 * 1024 * 1024

D_MODEL = 1024
GRID_W = 64
ROPE_THETA = 10000.0
EPS = 1e-6
N_MOD = 6

MLA_HEADS = 8
MLA_NOPE = 64
MLA_ROPE = 32
MLA_V = 64
MLA_Q_RANK = 384
MLA_KV_RANK = 256
MLA_SCALE = 1.0 / math.sqrt(MLA_NOPE + MLA_ROPE)
CONV_CH = 512
CONV_WIDTH = 31
CONV_HALO = 16

GQA_HEADS = 8
GQA_KV_HEADS = 2
GQA_HEAD_DIM = 64
GQA_SCALE = 1.0 / math.sqrt(GQA_HEAD_DIM)
CHUNK = 128
GMLP_GROUPS = 4
GMLP_CH = 512

N_EXPERTS = 32
TOP_K = 4
D_EXPERT = 1024
SWIGLU_LIMIT = 7.0
SWIGLU_ALPHA = 1.702

ROW_TILE = 512
FFN_TILE = 256
SC_CORES = 2
SC_SUBCORES = 16
SC_WORKERS = SC_CORES * SC_SUBCORES
SC_CHUNK = 64
COMBINE_TILE = 256
ATT_Q_TILE = 256
ATT_LONG_KEYS = 1024
V_SUM_LANE = 64
LATENT_HEADS_PER_STEP = 2
NEG_BIG = -1e30


def _params(sem, vmem=None):
    return pltpu.CompilerParams(dimension_semantics=sem, vmem_limit_bytes=vmem)


def _rms(x, g):
    return x * lax.rsqrt(jnp.mean(x * x, axis=-1, keepdims=True) + EPS) * g


def _const_spec(shape):
    nd = len(shape)
    return pl.BlockSpec(shape, lambda *_: (0,) * nd)


def _adaln_kernel(c_ref, w_ref, b_ref, o_ref):
    c = c_ref[...]
    s = c * jax.nn.sigmoid(c)
    o_ref[...] = jnp.dot(s, w_ref[...], preferred_element_type=F32, precision=HIGHEST) + b_ref[...]


def adaln(cond8, ada_w, ada_b):
    d, n = ada_w.shape
    bn = n // 4
    m = pl.pallas_call(
        _adaln_kernel,
        out_shape=jax.ShapeDtypeStruct((SUBLANES, n), F32),
        grid=(n // bn,),
        in_specs=[_const_spec((SUBLANES, d)),
                  pl.BlockSpec((d, bn), lambda j: (0, j)),
                  pl.BlockSpec((1, bn), lambda j: (0, j))],
        out_specs=pl.BlockSpec((SUBLANES, bn), lambda j: (0, j)),
        compiler_params=_params(("arbitrary",), VMEM_LIMIT),
        name="adaln",
    )(cond8, ada_w, ada_b.reshape(1, n))
    m = m.reshape(SUBLANES, N_MOD, d)
    return jnp.pad(m, ((0, 0), (0, SUBLANES - N_MOD), (0, 0)))


def _l0_inproj_kernel(*refs, rope):
    if rope:
        (x_ref, m_ref, n1_ref, win_ref, qg_ref, kvg_ref, wuq_ref, wuqs_ref, wuk_ref, e_ref, wuv_ref,
         cq_ref, sq_ref, ck_ref, sk_ref, q_out, k_out, v_out, ckv_out, kr_out, u_out) = refs
    else:
        (x_ref, m_ref, n1_ref, win_ref, qg_ref, kvg_ref, wuq_ref, wuk_ref, e_ref, wuv_ref,
         q_out, k_out, v_out, ckv_out, kr_out, u_out) = refs
    x = x_ref[0]
    m = m_ref[0]
    h = _rms(x, n1_ref[...]) * (1.0 + m[1:2]) + m[0:1]
    z = jnp.dot(h.astype(BF16), win_ref[...], preferred_element_type=F32)
    c_q = z[:, 0:MLA_Q_RANK]
    c_kv = z[:, MLA_Q_RANK:MLA_Q_RANK + MLA_KV_RANK]
    kr_blk = z[:, 640:768]
    val = z[:, 768:768 + CONV_CH]
    gate = z[:, 768 + CONV_CH:768 + 2 * CONV_CH]

    cqn = _rms(c_q, qg_ref[...]).astype(BF16)
    q = jnp.dot(cqn, wuq_ref[...], preferred_element_type=F32)
    if rope:
        qs = jnp.dot(cqn, wuqs_ref[...], preferred_element_type=F32)
        cq = cq_ref[...]
        sq = sq_ref[...]
        for hd in range(MLA_HEADS):
            sl = slice(hd * LANES, (hd + 1) * LANES)
            q_out[0, :, sl] = (q[:, sl] * cq + qs[:, sl] * sq).astype(BF16)
        kr = kr_blk * ck_ref[...] + pltpu.roll(kr_blk, LANES - MLA_ROPE, 1) * sk_ref[...]
    else:
        q_out[0] = q.astype(BF16)
        kr = kr_blk

    ckv = _rms(c_kv, kvg_ref[...])
    ckv_out[0] = ckv
    kr_out[0] = kr_blk[:, 0:MLA_ROPE]
    ckv_b = ckv.astype(BF16)
    k = (jnp.dot(ckv_b, wuk_ref[...], preferred_element_type=F32)
         + jnp.dot(kr.astype(BF16), e_ref[...], preferred_element_type=F32))
    k_out[0] = k.astype(BF16)
    v_out[0] = _with_sum_lane(jnp.dot(ckv_b, wuv_ref[...], preferred_element_type=F32)).astype(BF16)
    u_out[0] = val * jax.nn.sigmoid(gate)


def l0_inproj(x, mods, mod_off, n1, w, tables):
    bm, sm, d = x.shape
    tr = min(ROW_TILE, sm)
    rope = tables is not None
    hp = MLA_HEADS * LANES
    row = lambda width: pl.BlockSpec((1, tr, width), lambda b, i: (b, i, 0))
    in_specs = [row(d),
                pl.BlockSpec((1, SUBLANES, d), lambda b, i: (b + mod_off, 0, 0)),
                _const_spec((1, d)), _const_spec(w["win"].shape),
                _const_spec((1, MLA_Q_RANK)), _const_spec((1, MLA_KV_RANK)),
                _const_spec(w["wuq"].shape)]
    args = [x, mods, n1, w["win"], w["qg"], w["kvg"], w["wuq"]]
    if rope:
        in_specs.append(_const_spec(w["wuqs"].shape))
        args.append(w["wuqs"])
    in_specs += [_const_spec(w["wuk"].shape), _const_spec(w["e"].shape), _const_spec(w["wuv"].shape)]
    args += [w["wuk"], w["e"], w["wuv"]]
    if rope:
        in_specs += [pl.BlockSpec((tr, LANES), lambda b, i: (i, 0))] * 4
        args += list(tables)
    out_shape = [jax.ShapeDtypeStruct((bm, sm, hp), BF16),
                 jax.ShapeDtypeStruct((bm, sm, hp), BF16),
                 jax.ShapeDtypeStruct((bm, sm, hp), BF16),
                 jax.ShapeDtypeStruct((bm, sm, MLA_KV_RANK), F32),
                 jax.ShapeDtypeStruct((bm, sm, MLA_ROPE), F32),
                 jax.ShapeDtypeStruct((bm, sm, CONV_CH), F32)]
    out_specs = [row(hp), row(hp), row(hp), row(MLA_KV_RANK), row(MLA_ROPE), row(CONV_CH)]
    return pl.pallas_call(
        functools.partial(_l0_inproj_kernel, rope=rope),
        out_shape=out_shape, grid=(bm, sm // tr), in_specs=in_specs, out_specs=out_specs,
        compiler_params=_params(("parallel", "parallel"), VMEM_LIMIT),
        name="l0_inproj_rope" if rope else "l0_inproj",
    )(*args)


def _mla_ctx_kv_kernel(ckv_ref, kr_ref, wuk_ref, e_ref, wuv_ref, k_out, v_out):
    ckv_b = ckv_ref[0].astype(BF16)
    k = (jnp.dot(ckv_b, wuk_ref[...], preferred_element_type=F32)
         + jnp.dot(kr_ref[0].astype(BF16), e_ref[...], preferred_element_type=F32))
    k_out[0] = k.astype(BF16)
    v_out[0] = _with_sum_lane(jnp.dot(ckv_b, wuv_ref[...], preferred_element_type=F32)).astype(BF16)


def mla_ctx_kv(ckv, kr128, w):
    b, s, _ = ckv.shape
    hp = MLA_HEADS * LANES
    blk = lambda width: pl.BlockSpec((1, s, width), lambda i: (i, 0, 0))
    return pl.pallas_call(
        _mla_ctx_kv_kernel,
        out_shape=[jax.ShapeDtypeStruct((b, s, hp), BF16)] * 2,
        grid=(b,),
        in_specs=[blk(MLA_KV_RANK), blk(LANES), _const_spec(w["wuk"].shape),
                  _const_spec(w["e"].shape), _const_spec(w["wuv"].shape)],
        out_specs=[blk(hp), blk(hp)],
        compiler_params=_params(("parallel",), VMEM_LIMIT),
        name="mla_ctx_kv",
    )(ckv, kr128, w["wuk"], w["e"], w["wuv"])


def _conv_kernel(prev_ref, cur_ref, next_ref, w_ref, b_ref, g_ref, beta_ref, o_ref, pad_ref, sh_ref, *, rb):
    i = pl.program_id(1)
    last = pl.num_programs(1) - 1
    zeros = jnp.zeros((CONV_HALO, CONV_CH), F32)
    pad_ref[0:CONV_HALO, :] = jnp.where(i == 0, zeros, prev_ref[0])
    pad_ref[CONV_HALO:CONV_HALO + rb, :] = cur_ref[0]
    pad_ref[CONV_HALO + rb:CONV_HALO + rb + CONV_HALO, :] = jnp.where(i == last, zeros, next_ref[0])
    span = rb + 2 * CONV_HALO - SUBLANES
    for r in range(1, SUBLANES):
        sh_ref[r] = pad_ref[r:r + span, :]
    w = w_ref[...]
    shift = CONV_HALO - CONV_WIDTH // 2
    acc = jnp.zeros((rb, CONV_CH), F32) + b_ref[...]
    for k in range(CONV_WIDTH):
        off = k + shift
        r, a = off % SUBLANES, off // SUBLANES * SUBLANES
        window = pad_ref[a:a + rb, :] if r == 0 else sh_ref[r, a:a + rb, :]
        acc = acc + window * w[k:k + 1, :]
    mu = jnp.mean(acc, axis=-1, keepdims=True)
    cen = acc - mu
    var = jnp.mean(cen * cen, axis=-1, keepdims=True)
    y = cen * lax.rsqrt(var + EPS) * g_ref[...] + beta_ref[...]
    o_ref[0] = (y * jax.nn.sigmoid(y)).astype(BF16)


def conformer_conv(u, conv_w, conv_b, ln_g, ln_b):
    b, s, c = u.shape
    rb = min(256, s)
    nh = rb // CONV_HALO
    n_halo_blocks = s // CONV_HALO
    wpad = jnp.pad(conv_w.reshape(CONV_WIDTH, c), ((0, 32 - CONV_WIDTH), (0, 0)))
    return pl.pallas_call(
        functools.partial(_conv_kernel, rb=rb),
        out_shape=jax.ShapeDtypeStruct((b, s, c), BF16),
        grid=(b, s // rb),
        in_specs=[pl.BlockSpec((1, CONV_HALO, c), lambda bi, i: (bi, jnp.maximum(i * nh - 1, 0), 0)),
                  pl.BlockSpec((1, rb, c), lambda bi, i: (bi, i, 0)),
                  pl.BlockSpec((1, CONV_HALO, c),
                               lambda bi, i: (bi, jnp.minimum((i + 1) * nh, n_halo_blocks - 1), 0)),
                  _const_spec((32, c)), _const_spec((1, c)), _const_spec((1, c)), _const_spec((1, c))],
        out_specs=pl.BlockSpec((1, rb, c), lambda bi, i: (bi, i, 0)),
        scratch_shapes=[pltpu.VMEM((rb + 2 * CONV_HALO, c), F32),
                        pltpu.VMEM((SUBLANES, rb + 2 * CONV_HALO - SUBLANES, c), F32)],
        compiler_params=_params(("parallel", "parallel"), VMEM_LIMIT),
        name="conformer_conv",
    )(u, u, u, wpad, conv_b.reshape(1, c), ln_g.reshape(1, c), ln_b.reshape(1, c))


def _with_sum_lane(v):
    lane = lax.broadcasted_iota(jnp.int32, (1, v.shape[-1]), 1)
    return v + ((lane & (LANES - 1)) == V_SUM_LANE).astype(v.dtype)


def _attn_kernel(q_ref, k_ref, v_ref, o_ref, *, heads, rep, scale, mxu_denominator):
    for hd in range(heads):
        g = hd // rep
        q = q_ref[0, :, hd * LANES:(hd + 1) * LANES]
        k = k_ref[0, :, g * LANES:(g + 1) * LANES]
        s = lax.dot_general(q, k, (((1,), (1,)), ((), ())), preferred_element_type=F32) * scale
        m = jnp.max(s, axis=-1, keepdims=True)
        if mxu_denominator:
            p = jnp.exp((s - m).astype(BF16))
            o = jnp.dot(p, v_ref[0, :, g * LANES:(g + 1) * LANES], preferred_element_type=F32)
            l = o[:, V_SUM_LANE:V_SUM_LANE + 1]
        else:
            p = jnp.exp(s - m)
            l = jnp.sum(p, axis=-1, keepdims=True)
            o = jnp.dot(p.astype(BF16), v_ref[0, :, g * LANES:(g + 1) * LANES], preferred_element_type=F32)
        o_ref[0, :, hd * LANES:(hd + 1) * LANES] = (o / l).astype(BF16)


def attention(q, k, v, *, n_heads, n_kv, scale, heads_per_step):
    b, sq, _ = q.shape
    sk = k.shape[1]
    rep = n_heads // n_kv
    tq = min(ATT_Q_TILE, sq)
    hb = heads_per_step
    grid = (b, n_heads // hb, sq // tq)
    if hb >= rep:
        kv_spec = pl.BlockSpec((1, sk, hb // rep * LANES), lambda bi, h, i: (bi, 0, h))
        kern_rep = rep
    else:
        assert rep % hb == 0
        kv_spec = pl.BlockSpec((1, sk, LANES), lambda bi, h, i: (bi, 0, h * hb // rep))
        kern_rep = hb
    kern = functools.partial(_attn_kernel, heads=hb, rep=kern_rep, scale=scale,
                             mxu_denominator=sk >= ATT_LONG_KEYS)
    q_spec = pl.BlockSpec((1, tq, heads_per_step * LANES), lambda bi, h, i: (bi, i, h))
    return pl.pallas_call(
        kern,
        out_shape=jax.ShapeDtypeStruct(q.shape, BF16),
        grid=grid, in_specs=[q_spec, kv_spec, kv_spec], out_specs=q_spec,
        compiler_params=_params(("parallel", "parallel", "parallel"), VMEM_LIMIT),
        name="attention",
    )(q, k, v)


def _l1_inproj_kernel(*refs, rope):
    if rope:
        (x_ref, m_ref, n1_ref, win_ref, wsw_ref, qg_ref, qgs_ref, kg_ref, kgs_ref, lng_ref, lnb_ref,
         ws_ref, bs_ref, c_ref, s_ref, q_out, k_out, kn_out, vp_out, v_out, g_out) = refs
    else:
        (x_ref, m_ref, n1_ref, win_ref, qg_ref, kg_ref, lng_ref, lnb_ref,
         ws_ref, bs_ref, q_out, k_out, kn_out, vp_out, v_out, g_out) = refs
    x = x_ref[0]
    m = m_ref[0]
    hb = (_rms(x, n1_ref[...]) * (1.0 + m[1:2]) + m[0:1]).astype(BF16)
    z = jnp.dot(hb, win_ref[...], preferred_element_type=F32)
    qw = GQA_HEADS * LANES
    kw = GQA_KV_HEADS * LANES
    o_k, o_vp, o_v, o_u, o_vg = qw, qw + kw, qw + 2 * kw, qw + 2 * kw + LANES, qw + 2 * kw + LANES + GMLP_CH
    if rope:
        zs = jnp.dot(hb, wsw_ref[...], preferred_element_type=F32)
        cos = c_ref[...]
        sin = s_ref[...]

    def head(col, zcol, g_ref, gs_ref):
        t = z[:, col:col + LANES]
        r = lax.rsqrt(jnp.sum(t * t, axis=-1, keepdims=True) * (1.0 / GQA_HEAD_DIM) + EPS)
        normed = t * r * g_ref[...]
        if not rope:
            return normed, normed
        ts = zs[:, zcol:zcol + LANES]
        return normed, normed * cos + ts * r * gs_ref[...] * sin

    for hd in range(GQA_HEADS):
        _, rot = head(hd * LANES, hd * LANES, qg_ref, qgs_ref if rope else None)
        q_out[0, :, hd * LANES:(hd + 1) * LANES] = rot.astype(BF16)
    for hd in range(GQA_KV_HEADS):
        normed, rot = head(o_k + hd * LANES, qw + hd * LANES, kg_ref, kgs_ref if rope else None)
        k_out[0, :, hd * LANES:(hd + 1) * LANES] = rot.astype(BF16)
        kn_out[0, :, hd * LANES:(hd + 1) * LANES] = normed
    vp_out[0] = _with_sum_lane(z[:, o_vp:o_vp + kw]).astype(BF16)
    v_out[0] = z[:, o_v:o_v + LANES]

    u = z[:, o_u:o_u + GMLP_CH]
    vg = z[:, o_vg:o_vg + GMLP_CH]
    mu = jnp.mean(vg, axis=-1, keepdims=True)
    cen = vg - mu
    var = jnp.mean(cen * cen, axis=-1, keepdims=True)
    vn = (cen * lax.rsqrt(var + EPS) * lng_ref[...] + lnb_ref[...]).astype(BF16)
    bias = bs_ref[...]
    rows = x.shape[0]
    for cidx in range(rows // CHUNK):
        r0 = cidx * CHUNK
        for g in range(GMLP_GROUPS):
            c0 = g * LANES
            mixed = jnp.dot(ws_ref[g], vn[r0:r0 + CHUNK, c0:c0 + LANES], preferred_element_type=F32)
            g_out[0, r0:r0 + CHUNK, c0:c0 + LANES] = (
                u[r0:r0 + CHUNK, c0:c0 + LANES] * (mixed + bias[:, c0:c0 + LANES])).astype(BF16)


def l1_inproj(x, mods, mod_off, n1, w, tables):
    bm, sm, d = x.shape
    tr = min(ROW_TILE, sm)
    rope = tables is not None
    qw = GQA_HEADS * LANES
    kw = GQA_KV_HEADS * LANES
    row = lambda width: pl.BlockSpec((1, tr, width), lambda b, i: (b, i, 0))
    vec = _const_spec((1, LANES))
    in_specs = [row(d), pl.BlockSpec((1, SUBLANES, d), lambda b, i: (b + mod_off, 0, 0)),
                _const_spec((1, d)), _const_spec(w["win"].shape)]
    args = [x, mods, n1, w["win"]]
    if rope:
        in_specs += [_const_spec(w["wsw"].shape), vec, vec, vec, vec]
        args += [w["wsw"], w["qg"], w["qgs"], w["kg"], w["kgs"]]
    else:
        in_specs += [vec, vec]
        args += [w["qg"], w["kg"]]
    in_specs += [_const_spec((1, GMLP_CH)), _const_spec((1, GMLP_CH)),
                 _const_spec(w["ws"].shape), _const_spec((CHUNK, GMLP_CH))]
    args += [w["lng"], w["lnb"], w["ws"], w["bs"]]
    if rope:
        in_specs += [pl.BlockSpec((tr, LANES), lambda b, i: (i, 0))] * 2
        args += list(tables)
    out_shape = [jax.ShapeDtypeStruct((bm, sm, qw), BF16),
                 jax.ShapeDtypeStruct((bm, sm, kw), BF16),
                 jax.ShapeDtypeStruct((bm, sm, kw), F32),
                 jax.ShapeDtypeStruct((bm, sm, kw), BF16),
                 jax.ShapeDtypeStruct((bm, sm, LANES), F32),
                 jax.ShapeDtypeStruct((bm, sm, GMLP_CH), BF16)]
    out_specs = [row(qw), row(kw), row(kw), row(kw), row(LANES), row(GMLP_CH)]
    return pl.pallas_call(
        functools.partial(_l1_inproj_kernel, rope=rope),
        out_shape=out_shape, grid=(bm, sm // tr), in_specs=in_specs, out_specs=out_specs,
        compiler_params=_params(("parallel", "parallel"), VMEM_LIMIT),
        name="l1_inproj_rope" if rope else "l1_inproj",
    )(*args)


def _post_kernel(a_ref, c_ref, x_ref, m_ref, n2_ref, wa_ref, wc_ref, rwh_ref, rwl_ref, rb_ref, tri_ref, base_ref,
                 x1_out, xm_out, idx_out, wts_out, rank_out, cnt_out, run_ref):
    first =(pl.program_id(0) == 0) & (pl.program_id(1) == 0)

    @pl.when(first)
    def _():
        run_ref[...] = base_ref[...]

    m = m_ref[0]
    y = (jnp.dot(a_ref[0], wa_ref[...], preferred_element_type=F32)
         + jnp.dot(c_ref[0], wc_ref[...], preferred_element_type=F32))
    x1 = x_ref[0] + m[2:3] * y
    x1_out[0] = x1
    xm = _rms(x1, n2_ref[...]) * (1.0 + m[4:5]) + m[3:4]
    xm_out[0] = xm.reshape(xm.shape[0], SUBLANES, LANES)

    xh = xm.astype(BF16)
    xl = (xm - xh.astype(F32)).astype(BF16)
    logits = (jnp.dot(xh, rwh_ref[...], preferred_element_type=F32)
              + jnp.dot(xl, rwh_ref[...], preferred_element_type=F32)
              + jnp.dot(xh, rwl_ref[...], preferred_element_type=F32)) + rb_ref[...]
    rows = logits.shape[0]
    lane = lax.broadcasted_iota(jnp.int32, (rows, LANES), 1).astype(F32)
    work = logits
    vals, hots = [], []
    idx_acc = jnp.zeros((rows, LANES), F32)
    for k in range(TOP_K):
        top = jnp.max(work, axis=-1, keepdims=True)
        sel = jnp.min(jnp.where(work == top, lane, float(LANES)), axis=-1, keepdims=True)
        hot = lane == sel
        vals.append(top)
        hots.append(hot)
        idx_acc = idx_acc + jnp.where(lane == float(k), sel, 0.0)
        work = jnp.where(hot, -jnp.inf, work)
    exps = [jnp.exp(v - vals[0]) for v in vals]
    denom = exps[0] + exps[1] + exps[2] + exps[3]
    wcols = [jnp.broadcast_to(exps[k] / denom, (rows, LANES)) for k in range(TOP_K)]
    wcols += [jnp.zeros((rows, LANES), F32)] * (SUBLANES - TOP_K)
    wts = jnp.concatenate(wcols, axis=1).reshape(rows, SUBLANES, LANES)

    chosen = jnp.zeros((rows, LANES), F32)
    for hot in hots:
        chosen = chosen + hot.astype(F32)
    before = jnp.dot(tri_ref[...], chosen.astype(BF16), preferred_element_type=F32) + run_ref[0:1, :]
    rank = jnp.zeros((rows, LANES), F32)
    for k in range(TOP_K):
        rk = jnp.sum(jnp.where(hots[k], before, 0.0), axis=-1, keepdims=True)
        rank = rank + jnp.where(lane == float(k), rk, 0.0)
    run_ref[0:1, :] = run_ref[0:1, :] + jnp.sum(chosen, axis=0, keepdims=True)
    idx_out[0] = idx_acc.astype(jnp.int32)
    wts_out[0] = wts
    rank_out[0] = rank.astype(jnp.int32)
    cnt_out[...] = run_ref[...]


def post_mixer(attn, other, x, mods, mod_off, n2, wa, wc, rwh, rwl, rb, tri, base):
    bm, sm, d = x.shape
    tr = tri.shape[0]
    row = lambda width: pl.BlockSpec((1, tr, width), lambda b, i: (b, i, 0))
    tile_rows = pl.BlockSpec((1, tr, SUBLANES, d // SUBLANES), lambda b, i: (b, i, 0, 0))
    out_shape = [jax.ShapeDtypeStruct((bm, sm, d), F32),
                 jax.ShapeDtypeStruct((bm, sm, SUBLANES, d // SUBLANES), F32),
                 jax.ShapeDtypeStruct((bm, sm, LANES), jnp.int32),
                 jax.ShapeDtypeStruct((bm, sm, SUBLANES, LANES), F32),
                 jax.ShapeDtypeStruct((bm, sm, LANES), jnp.int32),
                 jax.ShapeDtypeStruct((SUBLANES, LANES), F32)]
    return pl.pallas_call(
        _post_kernel,
        out_shape=out_shape, grid=(bm, sm // tr),
        in_specs=[row(attn.shape[-1]), row(other.shape[-1]), row(d),
                  pl.BlockSpec((1, SUBLANES, d), lambda b, i: (b + mod_off, 0, 0)),
                  _const_spec((1, d)), _const_spec(wa.shape), _const_spec(wc.shape),
                  _const_spec(rwh.shape), _const_spec(rwl.shape), _const_spec((1, LANES)),
                  _const_spec(tri.shape), _const_spec((SUBLANES, LANES))],
        out_specs=[row(d), tile_rows, row(LANES),
                   pl.BlockSpec((1, tr, SUBLANES, LANES), lambda b, i: (b, i, 0, 0)),
                   row(LANES), _const_spec((SUBLANES, LANES))],
        scratch_shapes=[pltpu.VMEM((SUBLANES, LANES), F32)],
        compiler_params=_params(("arbitrary", "arbitrary"), VMEM_LIMIT),
        name="post_mixer_route",
    )(attn, other, x, mods, n2, wa, wc, rwh, rwl, rb, tri, base)


def _plan_kernel(off_ref, idx_ref, rank_ref, pos_out):
    idx = idx_ref[...]
    pos = rank_ref[...]
    for e in range(N_EXPERTS):
        pos = pos + jnp.where(idx == e, off_ref[e], 0)
    pos_out[...] = pos


def plan_positions(offsets, idx, rank):
    n = idx.shape[0]
    tr = min(2048, n)
    spec = pl.BlockSpec((tr, LANES), lambda i, off: (i, 0))
    return pl.pallas_call(
        _plan_kernel,
        out_shape=jax.ShapeDtypeStruct((n, LANES), jnp.int32),
        grid_spec=pltpu.PrefetchScalarGridSpec(
            num_scalar_prefetch=1, grid=(n // tr,), in_specs=[spec, spec], out_specs=spec),
        compiler_params=_params(("parallel",)),
        name="plan_positions",
    )(offsets, idx, rank)


def _sc_worker_id():
    return lax.axis_index("s") * SC_CORES + lax.axis_index("c")


def dispatch_rows(xms, posk, n_rows):
    n_total = sum(x.shape[0] for x in xms)
    tile = xms[0].shape[1:]
    starts, s0 = [], 0
    for x in xms:
        assert x.shape[0] % (SC_WORKERS * SC_CHUNK) == 0
        starts.append(s0)
        s0 += x.shape[0]
    mesh = plsc.VectorSubcoreMesh(core_axis_name="c", subcore_axis_name="s")

    @functools.partial(
        pl.kernel, mesh=mesh, out_type=jax.ShapeDtypeStruct((n_rows,) + tile, F32),
        scratch_types=[pltpu.VMEM((SC_CHUNK,), jnp.int32), pltpu.VMEM((SC_CHUNK,) + tile, F32),
                       pltpu.SemaphoreType.DMA],
        name="dispatch_rows_sc")
    def scatter(*refs):
        x_refs, pos_hbm, xs_hbm, idx_v, rows_v, sem = refs[:len(xms)], *refs[len(xms):]
        wid = _sc_worker_id()
        for x_hbm, start in zip(x_refs, starts):
            per_worker = x_hbm.shape[0] // SC_WORKERS

            @pl.loop(0, per_worker // SC_CHUNK)
            def _(c):
                t0 = wid * per_worker + c * SC_CHUNK
                pltpu.sync_copy(x_hbm.at[pl.ds(t0, SC_CHUNK)], rows_v)
                for k in range(TOP_K):
                    pltpu.sync_copy(pos_hbm.at[pl.ds(k * n_total + start + t0, SC_CHUNK)], idx_v)
                    pltpu.async_copy(rows_v, xs_hbm.at[idx_v], sem).wait()

    return scatter(*xms, posk)


def gather_rows(y, posk):
    n_pairs = posk.shape[0]
    tile = y.shape[1:]
    per_worker = n_pairs // SC_WORKERS
    assert per_worker % SC_CHUNK == 0
    mesh = plsc.VectorSubcoreMesh(core_axis_name="c", subcore_axis_name="s")

    @functools.partial(
        pl.kernel, mesh=mesh, out_type=jax.ShapeDtypeStruct((n_pairs,) + tile, F32),
        scratch_types=[pltpu.VMEM((SC_CHUNK,), jnp.int32), pltpu.VMEM((SC_CHUNK,) + tile, F32),
                       pltpu.SemaphoreType.DMA],
        name="gather_rows_sc")
    def gather(y_hbm, pos_hbm, out_hbm, idx_v, rows_v, sem):
        wid = _sc_worker_id()

        @pl.loop(0, per_worker // SC_CHUNK)
        def _(c):
            base = wid * per_worker + c * SC_CHUNK
            pltpu.sync_copy(pos_hbm.at[pl.ds(base, SC_CHUNK)], idx_v)
            pltpu.async_copy(y_hbm.at[idx_v], rows_v, sem).wait()
            pltpu.sync_copy(rows_v, out_hbm.at[pl.ds(base, SC_CHUNK)])

    return gather(y, posk)


def _ffn_kernel(te_ref, nu_ref, first_ref, slot_ref, nxt_ref, xs_ref, w1_hbm, b1_ref, w2_hbm, b2_ref, y_ref,
                w1f, w2f, w1b, w2b, sem):
    i = pl.program_id(0)

    def weight_copies(e, s):
        return (pltpu.make_async_copy(w1_hbm.at[e], w1f.at[s], sem.at[0, s]),
                pltpu.make_async_copy(w2_hbm.at[e], w2f.at[s], sem.at[1, s]))

    @pl.when(i < nu_ref[0])
    def _():
        s = slot_ref[i]

        @pl.when(first_ref[i] == 1)
        def _():
            @pl.when(i == 0)
            def _():
                for cp in weight_copies(te_ref[i], s):
                    cp.start()
            for cp in weight_copies(te_ref[i], s):
                cp.wait()

            @pl.when(nxt_ref[i] >= 0)
            def _():
                for cp in weight_copies(nxt_ref[i], 1 - s):
                    cp.start()
            for c in range(D_MODEL // LANES):
                w1b[c * LANES:(c + 1) * LANES, :] = w1f[s, c * LANES:(c + 1) * LANES, :].astype(BF16)
            for c in range(D_EXPERT // LANES):
                w2b[c * LANES:(c + 1) * LANES, :] = w2f[s, c * LANES:(c + 1) * LANES, :].astype(BF16)

        x = xs_ref[...].reshape(FFN_TILE, D_MODEL).astype(BF16)
        h = jnp.dot(x, w1b[...], preferred_element_type=F32) + b1_ref[0]
        g = jnp.minimum(h[:, :D_EXPERT], SWIGLU_LIMIT)
        lin = jnp.clip(h[:, D_EXPERT:], -SWIGLU_LIMIT, SWIGLU_LIMIT)
        a = (lin + 1.0) * (g * jax.nn.sigmoid(SWIGLU_ALPHA * g))
        y = jnp.dot(a.astype(BF16), w2b[...], preferred_element_type=F32) + b2_ref[0]
        y_ref[...] = y.reshape(y_ref.shape)

    @pl.when(i >= nu_ref[0])
    def _():
        y_ref[...] = jnp.zeros(y_ref.shape, F32)


def grouped_ffn(sched, xs, w1, b1, w2, b2):
    r = xs.shape[0]
    d = w1.shape[1]
    nt = r // FFN_TILE
    tile = (FFN_TILE,) + xs.shape[1:]
    rows = lambda i, te, nu, *_: (jnp.minimum(i, nu[0] - 1), 0, 0)
    bsel = lambda i, te, *_: (te[i], 0, 0)
    return pl.pallas_call(
        _ffn_kernel,
        out_shape=jax.ShapeDtypeStruct(xs.shape, F32),
        grid_spec=pltpu.PrefetchScalarGridSpec(
            num_scalar_prefetch=5, grid=(nt,),
            in_specs=[pl.BlockSpec(tile, rows),
                      pl.BlockSpec(memory_space=pl.ANY),
                      pl.BlockSpec((1, 1, 2 * D_EXPERT), bsel),
                      pl.BlockSpec(memory_space=pl.ANY),
                      pl.BlockSpec((1, 1, d), bsel)],
            out_specs=pl.BlockSpec(tile, lambda i, *_: (i, 0, 0)),
            scratch_shapes=[pltpu.VMEM((2, d, 2 * D_EXPERT), F32), pltpu.VMEM((2, D_EXPERT, d), F32),
                            pltpu.VMEM((d, 2 * D_EXPERT), BF16), pltpu.VMEM((D_EXPERT, d), BF16),
                            pltpu.SemaphoreType.DMA((2, 2))]),
        compiler_params=_params(("arbitrary",), VMEM_LIMIT),
        name="grouped_ffn",
    )(*sched, xs, w1, b1.reshape(N_EXPERTS, 1, -1), w2, b2.reshape(N_EXPERTS, 1, -1))


def _combine_kernel(x1_ref, wts_ref, m_ref, fn_ref, y0_ref, y1_ref, y2_ref, y3_ref, o_ref, *, final):
    w = wts_ref[0]
    acc = w[:, 0:1, :] * y0_ref[...]
    for k, y_ref in ((1, y1_ref), (2, y2_ref), (3, y3_ref)):
        acc = acc + w[:, k:k + 1, :] * y_ref[...]
    out = x1_ref[0] + m_ref[0][5:6] * acc.reshape(x1_ref.shape[1], D_MODEL)
    if final:
        out = _rms(out, fn_ref[...])
    o_ref[0] = out


def combine_rows(x1, wts, mods, mod_off, fn, yg, row_off, n_total, *, final):
    bm, sm, d = x1.shape
    tr = min(COMBINE_TILE, sm)
    nb = sm // tr
    row = lambda width: pl.BlockSpec((1, tr, width), lambda b, i: (b, i, 0))
    ysel = lambda k: pl.BlockSpec((tr,) + yg.shape[1:],
                                  lambda b, i: ((k * n_total + row_off) // tr + b * nb + i, 0, 0))
    return pl.pallas_call(
        functools.partial(_combine_kernel, final=final),
        out_shape=jax.ShapeDtypeStruct((bm, sm, d), F32),
        grid=(bm, nb),
        in_specs=[row(d), pl.BlockSpec((1, tr, SUBLANES, LANES), lambda b, i: (b, i, 0, 0)),
                  pl.BlockSpec((1, SUBLANES, d), lambda b, i: (b + mod_off, 0, 0)),
                  _const_spec((1, d))] + [ysel(k) for k in range(TOP_K)],
        out_specs=row(d),
        compiler_params=_params(("parallel", "parallel"), VMEM_LIMIT),
        name="combine_rows",
    )(x1, wts, mods, fn, yg, yg, yg, yg)


def _axial_angles(n_tokens, rot_dim):
    t = jnp.arange(n_tokens)
    rows = (t // GRID_W).astype(F32)
    cols = (t % GRID_W).astype(F32)
    n_freq = rot_dim // 4
    inv = ROPE_THETA ** (-jnp.arange(n_freq, dtype=F32) / n_freq)
    return jnp.concatenate([rows[:, None] * inv, cols[:, None] * inv], axis=-1)


def _lane_table(parts, n):
    cols = []
    for p in parts:
        cols.append(jnp.broadcast_to(jnp.asarray(p, F32), (n, p.shape[-1])) if hasattr(p, "shape") else p)
    return jnp.concatenate(cols, axis=-1)


def _swap_halves(w):
    half = w.shape[-1] // 2
    return jnp.concatenate([-w[..., half:], w[..., :half]], axis=-1)


def _prep_l0(w_in, q_norm, kv_norm, w_uq, w_uk, w_uv, w_out):
    d = w_in.shape[0]
    o_kr = MLA_Q_RANK + MLA_KV_RANK
    kr_cols = w_in[:, o_kr:o_kr + MLA_ROPE]
    win = jnp.concatenate(
        [w_in[:, :o_kr], kr_cols, _swap_halves(kr_cols), jnp.zeros((d, LANES - 2 * MLA_ROPE), F32),
         w_in[:, o_kr + MLA_ROPE:]], axis=1).astype(BF16)
    qk = MLA_NOPE + MLA_ROPE
    wuq3 = w_uq.reshape(MLA_Q_RANK, MLA_HEADS, qk)
    wuq = jnp.pad(wuq3, ((0, 0), (0, 0), (0, LANES - qk))).reshape(MLA_Q_RANK, -1).astype(BF16)
    wuqs3 = jnp.concatenate(
        [jnp.zeros((MLA_Q_RANK, MLA_HEADS, MLA_NOPE), F32), _swap_halves(wuq3[:, :, MLA_NOPE:]),
         jnp.zeros((MLA_Q_RANK, MLA_HEADS, LANES - qk), F32)], axis=-1)
    wuqs = wuqs3.reshape(MLA_Q_RANK, -1).astype(BF16)
    wuk3 = w_uk.reshape(MLA_KV_RANK, MLA_HEADS, MLA_NOPE)
    wuk = jnp.pad(wuk3, ((0, 0), (0, 0), (0, LANES - MLA_NOPE))).reshape(MLA_KV_RANK, -1).astype(BF16)
    wuv3 = w_uv.reshape(MLA_KV_RANK, MLA_HEADS, MLA_V)
    wuv = jnp.pad(wuv3, ((0, 0), (0, 0), (0, LANES - MLA_V))).reshape(MLA_KV_RANK, -1).astype(BF16)
    eye = jnp.eye(MLA_ROPE, dtype=F32)
    e_head = jnp.concatenate([jnp.zeros((MLA_ROPE, MLA_NOPE), F32), eye,
                              jnp.zeros((MLA_ROPE, LANES - qk), F32)], axis=1)
    e = jnp.pad(jnp.tile(e_head, (1, MLA_HEADS)), ((0, LANES - MLA_ROPE), (0, 0))).astype(BF16)
    wa3 = w_out[:MLA_HEADS * MLA_V].reshape(MLA_HEADS, MLA_V, d)
    wa = jnp.pad(wa3, ((0, 0), (0, LANES - MLA_V), (0, 0))).reshape(MLA_HEADS * LANES, d).astype(BF16)
    wc = w_out[MLA_HEADS * MLA_V:].astype(BF16)
    return dict(win=win, qg=q_norm.reshape(1, -1), kvg=kv_norm.reshape(1, -1), wuq=wuq, wuqs=wuqs,
                wuk=wuk, e=e, wuv=wuv, wa=wa, wc=wc)


def _l0_tables(n):
    ang = _axial_angles(n, MLA_ROPE)
    cos, sin = jnp.cos(ang), jnp.sin(ang)
    one = jnp.ones((n, 1), F32)
    zero = jnp.zeros((n, 1), F32)
    rest = LANES - MLA_NOPE - MLA_ROPE
    cq = jnp.concatenate([jnp.tile(one, (1, MLA_NOPE)), cos, cos, jnp.tile(one, (1, rest))], axis=1)
    sq = jnp.concatenate([jnp.tile(zero, (1, MLA_NOPE)), sin, sin, jnp.tile(zero, (1, rest))], axis=1)
    ck = jnp.concatenate([cos, cos, jnp.tile(zero, (1, LANES - MLA_ROPE))], axis=1)
    sk = jnp.concatenate([sin, sin, jnp.tile(zero, (1, LANES - MLA_ROPE))], axis=1)
    return cq, sq, ck, sk


def _pad_heads(w, n_heads, dim):
    d = w.shape[0]
    return jnp.pad(w.reshape(d, n_heads, dim), ((0, 0), (0, 0), (0, LANES - dim))).reshape(d, n_heads * LANES)


def _prep_l1(w_in, q_norm, k_norm, ln_g, ln_b, w_s, b_s, w_out):
    d = w_in.shape[0]
    qd = GQA_HEADS * GQA_HEAD_DIM
    kd = GQA_KV_HEADS * GQA_HEAD_DIM
    wq, wk, wv = w_in[:, :qd], w_in[:, qd:qd + kd], w_in[:, qd + kd:qd + 2 * kd]
    rest = w_in[:, qd + 2 * kd:]
    win = jnp.concatenate([_pad_heads(wq, GQA_HEADS, GQA_HEAD_DIM), _pad_heads(wk, GQA_KV_HEADS, GQA_HEAD_DIM),
                           _pad_heads(wv, GQA_KV_HEADS, GQA_HEAD_DIM), wv, rest], axis=1).astype(BF16)
    swq = _swap_halves(wq.reshape(d, GQA_HEADS, GQA_HEAD_DIM)).reshape(d, qd)
    swk = _swap_halves(wk.reshape(d, GQA_KV_HEADS, GQA_HEAD_DIM)).reshape(d, kd)
    wsw = jnp.concatenate([_pad_heads(swq, GQA_HEADS, GQA_HEAD_DIM),
                           _pad_heads(swk, GQA_KV_HEADS, GQA_HEAD_DIM)], axis=1).astype(BF16)
    half = GQA_HEAD_DIM // 2
    padg = lambda g: jnp.pad(g, (0, LANES - GQA_HEAD_DIM)).reshape(1, LANES)
    swapg = lambda g: jnp.concatenate([g[half:], g[:half]])
    wa3 = w_out[:qd].reshape(GQA_HEADS, GQA_HEAD_DIM, d)
    wa = jnp.pad(wa3, ((0, 0), (0, LANES - GQA_HEAD_DIM), (0, 0))).reshape(GQA_HEADS * LANES, d).astype(BF16)
    wc = w_out[qd:].astype(BF16)
    bs = jnp.repeat(b_s.T, LANES, axis=1)
    return dict(win=win, wsw=wsw, qg=padg(q_norm), qgs=padg(swapg(q_norm)), kg=padg(k_norm),
                kgs=padg(swapg(k_norm)), lng=ln_g.reshape(1, -1), lnb=ln_b.reshape(1, -1),
                ws=w_s.astype(BF16), bs=bs, wa=wa, wc=wc)


def _l1_tables(n):
    ang = _axial_angles(n, GQA_HEAD_DIM)
    cos, sin = jnp.cos(ang), jnp.sin(ang)
    pad = LANES - GQA_HEAD_DIM
    c = jnp.concatenate([cos, cos, jnp.ones((n, pad), F32)], axis=1)
    s = jnp.concatenate([sin, sin, jnp.zeros((n, pad), F32)], axis=1)
    return c, s


def _pad_lanes(x, width):
    return jnp.pad(x, [(0, 0)] * (x.ndim - 1) + [(0, width - x.shape[-1])])


def routed_ffn(groups, mods, n2, moe, tri, final_norm, *, final):
    router_w, router_b, w1, b1, w2, b2 = moe
    rw = _pad_lanes(router_w, LANES)
    rwh = rw.astype(BF16)
    rwl = (rw - rwh.astype(F32)).astype(BF16)
    rb = jnp.concatenate([router_b, jnp.full((LANES - N_EXPERTS,), NEG_BIG, F32)]).reshape(1, LANES)
    base = jnp.zeros((SUBLANES, LANES), F32)
    routed = []
    for g in groups:
        x1, xm, idx, wts, rank, base = post_mixer(
            g["attn"], g["other"], g["x"], mods, g["mod_off"], n2, g["wa"], g["wc"], rwh, rwl, rb, tri, base)
        routed.append((x1, xm, idx, wts, rank))
    n_total = sum(r[0].shape[0] * r[0].shape[1] for r in routed)
    nt = n_total * TOP_K // FFN_TILE + N_EXPERTS

    counts = base[0, :N_EXPERTS].astype(jnp.int32)
    tiles = (counts + FFN_TILE - 1) // FFN_TILE
    tile_end = jnp.cumsum(tiles)
    offsets = (tile_end - tiles) * FFN_TILE
    n_used = tile_end[-1:].astype(jnp.int32)
    experts = jnp.arange(N_EXPERTS)
    busy = tiles > 0
    slot_e = (jnp.cumsum(busy) - 1) % 2
    later = jnp.where(busy[None, :] & (experts[None, :] > experts[:, None]), experts[None, :], N_EXPERTS)
    nxt_e = jnp.min(later, axis=1)
    nxt_e = jnp.where(nxt_e == N_EXPERTS, -1, nxt_e)
    tile_expert = jnp.sum(jnp.arange(nt)[:, None] >= tile_end[None, :], axis=1)
    tile_expert = jnp.minimum(tile_expert, tile_expert[n_used[0] - 1])
    first = jnp.concatenate([jnp.ones((1,), bool), tile_expert[1:] != tile_expert[:-1]])
    sched = tuple(a.astype(jnp.int32) for a in
                  (tile_expert, n_used, first, slot_e[tile_expert], nxt_e[tile_expert]))

    row_tile = (SUBLANES, D_MODEL // SUBLANES)
    positions, xms = [], []
    for (x1, xm, idx, wts, rank) in routed:
        n = idx.shape[0] * idx.shape[1]
        pos = plan_positions(offsets, idx.reshape(n, LANES), rank.reshape(n, LANES))
        positions.append(pos[:, :TOP_K])
        xms.append(xm.reshape((n,) + row_tile))
    posk = jnp.concatenate(positions, axis=0).T.reshape(-1)
    xs = dispatch_rows(xms, posk, nt * FFN_TILE)
    y = grouped_ffn(sched, xs, w1, b1, w2, b2)
    yg = gather_rows(y, posk)
    outs, row_off = [], 0
    for g, (x1, xm, idx, wts, rank) in zip(groups, routed):
        outs.append(combine_rows(x1, wts, mods, g["mod_off"], final_norm, yg, row_off, n_total, final=final))
        row_off += x1.shape[0] * x1.shape[1]
    return outs


def kernel(x_prompt, x_sample, cache_l0_ckv, cache_l0_krope, cache_l1_k, cache_l1_v, c, c_ctx,
           l0_ada_w, l0_ada_b, l0_norm1, l0_w_in, l0_q_norm, l0_kv_norm, l0_w_uq, l0_w_uk, l0_w_uv,
           l0_conv_w, l0_conv_b, l0_conv_ln_g, l0_conv_ln_b, l0_w_out, l0_norm2,
           l0_router_w, l0_router_b, l0_w1, l0_b1, l0_w2, l0_b2,
           l1_ada_w, l1_ada_b, l1_norm1, l1_w_in, l1_q_norm, l1_k_norm, l1_gmlp_ln_g, l1_gmlp_ln_b,
           l1_w_s, l1_b_s, l1_w_out, l1_norm2,
           l1_router_w, l1_router_b, l1_w1, l1_b1, l1_w2, l1_b2,
           final_norm):
    bp, sp, d = x_prompt.shape
    bs, ss, _ = x_sample.shape
    past = cache_l0_ckv.shape[1]
    n_p = bp * sp

    cond8 = jnp.concatenate([c_ctx[None], c, jnp.zeros((SUBLANES - 1 - bs, d), F32)], axis=0)
    mods0 = adaln(cond8, l0_ada_w, l0_ada_b)
    mods1 = adaln(cond8, l1_ada_w, l1_ada_b)
    tri = jnp.tril(jnp.ones((ROW_TILE, ROW_TILE), F32), -1).astype(BF16)
    fn = final_norm.reshape(1, d)

    w0 = _prep_l0(l0_w_in, l0_q_norm, l0_kv_norm, l0_w_uq, l0_w_uk, l0_w_uv, l0_w_out)
    n1 = l0_norm1.reshape(1, d)
    hp = x_prompt.reshape(1, n_p, d)
    q_p, k_p, v_p, ckv_p, kr_p, u_p = l0_inproj(hp, mods0, 0, n1, w0, None)
    q_s, k_s, v_s, _, _, u_s = l0_inproj(x_sample, mods0, 1, n1, w0, _l0_tables(ss))
    k_c, v_c = mla_ctx_kv(cache_l0_ckv, _pad_lanes(cache_l0_krope, LANES), w0)
    hw = MLA_HEADS * LANES
    att_p = attention(q_p.reshape(bp, sp, hw), k_p.reshape(bp, sp, hw), v_p.reshape(bp, sp, hw),
                      n_heads=MLA_HEADS, n_kv=MLA_HEADS, scale=MLA_SCALE, heads_per_step=MLA_HEADS)
    att_s = attention(q_s, jnp.concatenate([k_c, k_s], axis=1), jnp.concatenate([v_c, v_s], axis=1),
                      n_heads=MLA_HEADS, n_kv=MLA_HEADS, scale=MLA_SCALE, heads_per_step=LATENT_HEADS_PER_STEP)
    conv_p = conformer_conv(u_p.reshape(bp, sp, CONV_CH), l0_conv_w, l0_conv_b, l0_conv_ln_g, l0_conv_ln_b)
    conv_s = conformer_conv(u_s, l0_conv_w, l0_conv_b, l0_conv_ln_g, l0_conv_ln_b)
    groups = [dict(attn=att_p.reshape(1, n_p, hw), other=conv_p.reshape(1, n_p, CONV_CH), x=hp, mod_off=0,
                   wa=w0["wa"], wc=w0["wc"]),
              dict(attn=att_s, other=conv_s, x=x_sample, mod_off=1, wa=w0["wa"], wc=w0["wc"])]
    hp, hs = routed_ffn(groups, mods0, l0_norm2.reshape(1, d),
                        (l0_router_w, l0_router_b, l0_w1, l0_b1, l0_w2, l0_b2), tri, fn, final=False)
    new_l0_ckv = ckv_p.reshape(bp, sp, MLA_KV_RANK)
    new_l0_krope = kr_p.reshape(bp, sp, MLA_ROPE)

    w1p = _prep_l1(l1_w_in, l1_q_norm, l1_k_norm, l1_gmlp_ln_g, l1_gmlp_ln_b, l1_w_s, l1_b_s, l1_w_out)
    n1 = l1_norm1.reshape(1, d)
    q_p, k_p, kn_p, vp_p, vraw_p, gat_p = l1_inproj(hp, mods1, 0, n1, w1p, None)
    q_s, k_s, _, vp_s, _, gat_s = l1_inproj(hs, mods1, 1, n1, w1p, _l1_tables(ss))
    qw = GQA_HEADS * LANES
    kw = GQA_KV_HEADS * LANES
    pad_kv = lambda t: _pad_lanes(t, LANES).reshape(bs, past, kw).astype(BF16)
    att_p = attention(q_p.reshape(bp, sp, qw), k_p.reshape(bp, sp, kw), vp_p.reshape(bp, sp, kw),
                      n_heads=GQA_HEADS, n_kv=GQA_KV_HEADS, scale=GQA_SCALE, heads_per_step=GQA_HEADS)
    att_s = attention(q_s, jnp.concatenate([pad_kv(cache_l1_k), k_s], axis=1),
                      jnp.concatenate([_with_sum_lane(pad_kv(cache_l1_v)), vp_s], axis=1),
                      n_heads=GQA_HEADS, n_kv=GQA_KV_HEADS, scale=GQA_SCALE, heads_per_step=LATENT_HEADS_PER_STEP)
    groups = [dict(attn=att_p.reshape(1, n_p, qw), other=gat_p, x=hp, mod_off=0, wa=w1p["wa"], wc=w1p["wc"]),
              dict(attn=att_s, other=gat_s, x=hs, mod_off=1, wa=w1p["wa"], wc=w1p["wc"])]
    yp, ys = routed_ffn(groups, mods1, l1_norm2.reshape(1, d),
                        (l1_router_w, l1_router_b, l1_w1, l1_b1, l1_w2, l1_b2), tri, fn, final=True)
    new_l1_k = kn_p.reshape(bp, sp, GQA_KV_HEADS, LANES)[..., :GQA_HEAD_DIM]
    new_l1_v = vraw_p.reshape(bp, sp, GQA_KV_HEADS, GQA_HEAD_DIM)
    return (yp.reshape(bp, sp, d), ys, new_l0_ckv, new_l0_krope, new_l1_k, new_l1_v)
```

```python
import functools
import math

import jax
import jax.numpy as jnp
from jax import lax
from jax.experimental import pallas as pl
from jax.experimental.pallas import tpu as pltpu
from jax.experimental.pallas import tpu_sc as plsc

F32 = jnp.float32
BF16 = jnp.bfloat16
HIGHEST = lax.Precision.HIGHEST

LANES = 128
SUBLANES = 8
VMEM_LIMIT = 56 * 1024 * 1024

D_MODEL = 1024
GRID_W = 64
ROPE_THETA = 10000.0
EPS = 1e-6
N_MOD = 6

MLA_HEADS = 8
MLA_NOPE = 64
MLA_ROPE = 32
MLA_V = 64
MLA_Q_RANK = 384
MLA_KV_RANK = 256
MLA_SCALE = 1.0 / math.sqrt(MLA_NOPE + MLA_ROPE)
CONV_CH = 512
CONV_WIDTH = 31
CONV_HALO = 16

GQA_HEADS = 8
GQA_KV_HEADS = 2
GQA_HEAD_DIM = 64
GQA_SCALE = 1.0 / math.sqrt(GQA_HEAD_DIM)
CHUNK = 128
GMLP_GROUPS = 4
GMLP_CH = 512

N_EXPERTS = 32
TOP_K = 4
D_EXPERT = 1024
SWIGLU_LIMIT = 7.0
SWIGLU_ALPHA = 1.702

ROW_TILE = 512
FFN_TILE = 256
SC_CORES = 2
SC_SUBCORES = 16
SC_WORKERS = SC_CORES * SC_SUBCORES
SC_CHUNK_BYTES = 256 * 1024
SC_MAX_INDICES = 128
COMBINE_TILE = 256
ATT_Q_TILE = 256
ATT_LONG_KEYS = 1024
V_SUM_LANE = 64
LATENT_HEADS_PER_STEP = 4
NEG_BIG = -1e30


def _params(sem, vmem=None):
    return pltpu.CompilerParams(dimension_semantics=sem, vmem_limit_bytes=vmem)


def _rms(x, g):
    return x * lax.rsqrt(jnp.mean(x * x, axis=-1, keepdims=True) + EPS) * g


ROW_WORDS = (SUBLANES // 2, LANES)


def _pack_rows(x_bf16):
    return pltpu.bitcast(x_bf16.reshape(x_bf16.shape[0], SUBLANES, LANES), jnp.int32)


def _unpack_rows(words):
    return pltpu.bitcast(words, BF16).reshape(words.shape[0], D_MODEL)


def _const_spec(shape):
    nd = len(shape)
    return pl.BlockSpec(shape, lambda *_: (0,) * nd)


def _adaln_kernel(c_ref, w_ref, b_ref, o_ref):
    c = c_ref[...]
    s = c * jax.nn.sigmoid(c)
    o_ref[...] = jnp.dot(s, w_ref[...], preferred_element_type=F32, precision=HIGHEST) + b_ref[...]


def adaln(cond8, ada_w, ada_b):
    d, n = ada_w.shape
    bn = n // 4
    m = pl.pallas_call(
        _adaln_kernel,
        out_shape=jax.ShapeDtypeStruct((SUBLANES, n), F32),
        grid=(n // bn,),
        in_specs=[_const_spec((SUBLANES, d)),
                  pl.BlockSpec((d, bn), lambda j: (0, j)),
                  pl.BlockSpec((1, bn), lambda j: (0, j))],
        out_specs=pl.BlockSpec((SUBLANES, bn), lambda j: (0, j)),
        compiler_params=_params(("arbitrary",), VMEM_LIMIT),
        name="adaln",
    )(cond8, ada_w, ada_b.reshape(1, n))
    m = m.reshape(SUBLANES, N_MOD, d)
    return jnp.pad(m, ((0, 0), (0, SUBLANES - N_MOD), (0, 0)))


def _l0_inproj_kernel(*refs, rope):
    if rope:
        (x_ref, m_ref, n1_ref, win_ref, qg_ref, kvg_ref, wuq_ref, wuqs_ref, wuk_ref, e_ref, wuv_ref,
         cq_ref, sq_ref, ck_ref, sk_ref, q_out, k_out, v_out, ckv_out, kr_out, u_out) = refs
    else:
        (x_ref, m_ref, n1_ref, win_ref, qg_ref, kvg_ref, wuq_ref, wuk_ref, e_ref, wuv_ref,
         q_out, k_out, v_out, ckv_out, kr_out, u_out) = refs
    x = x_ref[0]
    m = m_ref[0]
    h = _rms(x, n1_ref[...]) * (1.0 + m[1:2]) + m[0:1]
    z = jnp.dot(h.astype(BF16), win_ref[...], preferred_element_type=F32)
    c_q = z[:, 0:MLA_Q_RANK]
    c_kv = z[:, MLA_Q_RANK:MLA_Q_RANK + MLA_KV_RANK]
    kr_blk = z[:, 640:768]
    val = z[:, 768:768 + CONV_CH]
    gate = z[:, 768 + CONV_CH:768 + 2 * CONV_CH]

    cqn = _rms(c_q, qg_ref[...]).astype(BF16)
    q = jnp.dot(cqn, wuq_ref[...], preferred_element_type=F32)
    if rope:
        qs = jnp.dot(cqn, wuqs_ref[...], preferred_element_type=F32)
        cq = cq_ref[...]
        sq = sq_ref[...]
        for hd in range(MLA_HEADS):
            sl = slice(hd * LANES, (hd + 1) * LANES)
            q_out[0, :, sl] = (q[:, sl] * cq + qs[:, sl] * sq).astype(BF16)
        kr = kr_blk * ck_ref[...] + pltpu.roll(kr_blk, LANES - MLA_ROPE, 1) * sk_ref[...]
    else:
        q_out[0] = q.astype(BF16)
        kr = kr_blk

    ckv = _rms(c_kv, kvg_ref[...])
    ckv_out[0] = ckv
    kr_out[0] = kr_blk[:, 0:MLA_ROPE]
    ckv_b = ckv.astype(BF16)
    k = (jnp.dot(ckv_b, wuk_ref[...], preferred_element_type=F32)
         + jnp.dot(kr.astype(BF16), e_ref[...], preferred_element_type=F32))
    k_out[0] = k.astype(BF16)
    v_out[0] = _with_sum_lane(jnp.dot(ckv_b, wuv_ref[...], preferred_element_type=F32)).astype(BF16)
    u_out[0] = val * jax.nn.sigmoid(gate)


def l0_inproj(x, mods, mod_off, n1, w, tables):
    bm, sm, d = x.shape
    tr = min(ROW_TILE, sm)
    rope = tables is not None
    hp = MLA_HEADS * LANES
    row = lambda width: pl.BlockSpec((1, tr, width), lambda b, i: (b, i, 0))
    in_specs = [row(d),
                pl.BlockSpec((1, SUBLANES, d), lambda b, i: (b + mod_off, 0, 0)),
                _const_spec((1, d)), _const_spec(w["win"].shape),
                _const_spec((1, MLA_Q_RANK)), _const_spec((1, MLA_KV_RANK)),
                _const_spec(w["wuq"].shape)]
    args = [x, mods, n1, w["win"], w["qg"], w["kvg"], w["wuq"]]
    if rope:
        in_specs.append(_const_spec(w["wuqs"].shape))
        args.append(w["wuqs"])
    in_specs += [_const_spec(w["wuk"].shape), _const_spec(w["e"].shape), _const_spec(w["wuv"].shape)]
    args += [w["wuk"], w["e"], w["wuv"]]
    if rope:
        in_specs += [pl.BlockSpec((tr, LANES), lambda b, i: (i, 0))] * 4
        args += list(tables)
    out_shape = [jax.ShapeDtypeStruct((bm, sm, hp), BF16),
                 jax.ShapeDtypeStruct((bm, sm, hp), BF16),
                 jax.ShapeDtypeStruct((bm, sm, hp), BF16),
                 jax.ShapeDtypeStruct((bm, sm, MLA_KV_RANK), F32),
                 jax.ShapeDtypeStruct((bm, sm, MLA_ROPE), F32),
                 jax.ShapeDtypeStruct((bm, sm, CONV_CH), F32)]
    out_specs = [row(hp), row(hp), row(hp), row(MLA_KV_RANK), row(MLA_ROPE), row(CONV_CH)]
    return pl.pallas_call(
        functools.partial(_l0_inproj_kernel, rope=rope),
        out_shape=out_shape, grid=(bm, sm // tr), in_specs=in_specs, out_specs=out_specs,
        compiler_params=_params(("parallel", "parallel"), VMEM_LIMIT),
        name="l0_inproj_rope" if rope else "l0_inproj",
    )(*args)


def _mla_ctx_kv_kernel(ckv_ref, kr_ref, wuk_ref, e_ref, wuv_ref, k_out, v_out):
    ckv_b = ckv_ref[0].astype(BF16)
    k = (jnp.dot(ckv_b, wuk_ref[...], preferred_element_type=F32)
         + jnp.dot(kr_ref[0].astype(BF16), e_ref[...], preferred_element_type=F32))
    k_out[0] = k.astype(BF16)
    v_out[0] = _with_sum_lane(jnp.dot(ckv_b, wuv_ref[...], preferred_element_type=F32)).astype(BF16)


def mla_ctx_kv(ckv, kr128, w):
    b, s, _ = ckv.shape
    hp = MLA_HEADS * LANES
    blk = lambda width: pl.BlockSpec((1, s, width), lambda i: (i, 0, 0))
    return pl.pallas_call(
        _mla_ctx_kv_kernel,
        out_shape=[jax.ShapeDtypeStruct((b, s, hp), BF16)] * 2,
        grid=(b,),
        in_specs=[blk(MLA_KV_RANK), blk(LANES), _const_spec(w["wuk"].shape),
                  _const_spec(w["e"].shape), _const_spec(w["wuv"].shape)],
        out_specs=[blk(hp), blk(hp)],
        compiler_params=_params(("parallel",), VMEM_LIMIT),
        name="mla_ctx_kv",
    )(ckv, kr128, w["wuk"], w["e"], w["wuv"])


def _conv_kernel(prev_ref, cur_ref, next_ref, w_ref, b_ref, g_ref, beta_ref, o_ref, pad_ref, sh_ref, *, rb):
    i = pl.program_id(1)
    last = pl.num_programs(1) - 1
    zeros = jnp.zeros((CONV_HALO, CONV_CH), F32)
    pad_ref[0:CONV_HALO, :] = jnp.where(i == 0, zeros, prev_ref[0])
    pad_ref[CONV_HALO:CONV_HALO + rb, :] = cur_ref[0]
    pad_ref[CONV_HALO + rb:CONV_HALO + rb + CONV_HALO, :] = jnp.where(i == last, zeros, next_ref[0])
    span = rb + 2 * CONV_HALO - SUBLANES
    for r in range(1, SUBLANES):
        sh_ref[r] = pad_ref[r:r + span, :]
    w = w_ref[...]
    shift = CONV_HALO - CONV_WIDTH // 2
    acc = jnp.zeros((rb, CONV_CH), F32) + b_ref[...]
    for k in range(CONV_WIDTH):
        off = k + shift
        r, a = off % SUBLANES, off // SUBLANES * SUBLANES
        window = pad_ref[a:a + rb, :] if r == 0 else sh_ref[r, a:a + rb, :]
        acc = acc + window * w[k:k + 1, :]
    mu = jnp.mean(acc, axis=-1, keepdims=True)
    cen = acc - mu
    var = jnp.mean(cen * cen, axis=-1, keepdims=True)
    y = cen * lax.rsqrt(var + EPS) * g_ref[...] + beta_ref[...]
    o_ref[0] = (y * jax.nn.sigmoid(y)).astype(BF16)


def conformer_conv(u, conv_w, conv_b, ln_g, ln_b):
    b, s, c = u.shape
    rb = min(256, s)
    nh = rb // CONV_HALO
    n_halo_blocks = s // CONV_HALO
    wpad = jnp.pad(conv_w.reshape(CONV_WIDTH, c), ((0, 32 - CONV_WIDTH), (0, 0)))
    return pl.pallas_call(
        functools.partial(_conv_kernel, rb=rb),
        out_shape=jax.ShapeDtypeStruct((b, s, c), BF16),
        grid=(b, s // rb),
        in_specs=[pl.BlockSpec((1, CONV_HALO, c), lambda bi, i: (bi, jnp.maximum(i * nh - 1, 0), 0)),
                  pl.BlockSpec((1, rb, c), lambda bi, i: (bi, i, 0)),
                  pl.BlockSpec((1, CONV_HALO, c),
                               lambda bi, i: (bi, jnp.minimum((i + 1) * nh, n_halo_blocks - 1), 0)),
                  _const_spec((32, c)), _const_spec((1, c)), _const_spec((1, c)), _const_spec((1, c))],
        out_specs=pl.BlockSpec((1, rb, c), lambda bi, i: (bi, i, 0)),
        scratch_shapes=[pltpu.VMEM((rb + 2 * CONV_HALO, c), F32),
                        pltpu.VMEM((SUBLANES, rb + 2 * CONV_HALO - SUBLANES, c), F32)],
        compiler_params=_params(("parallel", "parallel"), VMEM_LIMIT),
        name="conformer_conv",
    )(u, u, u, wpad, conv_b.reshape(1, c), ln_g.reshape(1, c), ln_b.reshape(1, c))


def _with_sum_lane(v):
    lane = lax.broadcasted_iota(jnp.int32, (1, v.shape[-1]), 1)
    return v + ((lane & (LANES - 1)) == V_SUM_LANE).astype(v.dtype)


def _attn_kernel(q_ref, k_ref, v_ref, o_ref, *, heads, rep, scale, mxu_denominator):
    for hd in range(heads):
        g = hd // rep
        q = q_ref[0, :, hd * LANES:(hd + 1) * LANES]
        k = k_ref[0, :, g * LANES:(g + 1) * LANES]
        s = lax.dot_general(q, k, (((1,), (1,)), ((), ())), preferred_element_type=F32) * scale
        m = jnp.max(s, axis=-1, keepdims=True)
        if mxu_denominator:
            p = jnp.exp((s - m).astype(BF16))
            o = jnp.dot(p, v_ref[0, :, g * LANES:(g + 1) * LANES], preferred_element_type=F32)
            l = o[:, V_SUM_LANE:V_SUM_LANE + 1]
        else:
            p = jnp.exp(s - m)
            l = jnp.sum(p, axis=-1, keepdims=True)
            o = jnp.dot(p.astype(BF16), v_ref[0, :, g * LANES:(g + 1) * LANES], preferred_element_type=F32)
        o_ref[0, :, hd * LANES:(hd + 1) * LANES] = (o / l).astype(BF16)


def attention(q, k, v, *, n_heads, n_kv, scale, heads_per_step):
    b, sq, _ = q.shape
    sk = k.shape[1]
    rep = n_heads // n_kv
    tq = min(ATT_Q_TILE, sq)
    hb = heads_per_step
    grid = (b, n_heads // hb, sq // tq)
    if hb >= rep:
        kv_spec = pl.BlockSpec((1, sk, hb // rep * LANES), lambda bi, h, i: (bi, 0, h))
        kern_rep = rep
    else:
        assert rep % hb == 0
        kv_spec = pl.BlockSpec((1, sk, LANES), lambda bi, h, i: (bi, 0, h * hb // rep))
        kern_rep = hb
    kern = functools.partial(_attn_kernel, heads=hb, rep=kern_rep, scale=scale,
                             mxu_denominator=sk >= ATT_LONG_KEYS)
    q_spec = pl.BlockSpec((1, tq, heads_per_step * LANES), lambda bi, h, i: (bi, i, h))
    return pl.pallas_call(
        kern,
        out_shape=jax.ShapeDtypeStruct(q.shape, BF16),
        grid=grid, in_specs=[q_spec, kv_spec, kv_spec], out_specs=q_spec,
        compiler_params=_params(("parallel", "parallel", "parallel"), VMEM_LIMIT),
        name="attention",
    )(q, k, v)


def _l1_inproj_kernel(*refs, rope):
    if rope:
        (x_ref, m_ref, n1_ref, win_ref, wsw_ref, qg_ref, qgs_ref, kg_ref, kgs_ref, lng_ref, lnb_ref,
         ws_ref, bs_ref, c_ref, s_ref, q_out, k_out, kn_out, vp_out, v_out, g_out) = refs
    else:
        (x_ref, m_ref, n1_ref, win_ref, qg_ref, kg_ref, lng_ref, lnb_ref,
         ws_ref, bs_ref, q_out, k_out, kn_out, vp_out, v_out, g_out) = refs
    x = x_ref[0]
    m = m_ref[0]
    hb = (_rms(x, n1_ref[...]) * (1.0 + m[1:2]) + m[0:1]).astype(BF16)
    z = jnp.dot(hb, win_ref[...], preferred_element_type=F32)
    qw = GQA_HEADS * LANES
    kw = GQA_KV_HEADS * LANES
    o_k, o_vp, o_v, o_u, o_vg = qw, qw + kw, qw + 2 * kw, qw + 2 * kw + LANES, qw + 2 * kw + LANES + GMLP_CH
    if rope:
        zs = jnp.dot(hb, wsw_ref[...], preferred_element_type=F32)
        cos = c_ref[...]
        sin = s_ref[...]

    def head(col, zcol, g_ref, gs_ref):
        t = z[:, col:col + LANES]
        r = lax.rsqrt(jnp.sum(t * t, axis=-1, keepdims=True) * (1.0 / GQA_HEAD_DIM) + EPS)
        normed = t * r * g_ref[...]
        if not rope:
            return normed, normed
        ts = zs[:, zcol:zcol + LANES]
        return normed, normed * cos + ts * r * gs_ref[...] * sin

    for hd in range(GQA_HEADS):
        _, rot = head(hd * LANES, hd * LANES, qg_ref, qgs_ref if rope else None)
        q_out[0, :, hd * LANES:(hd + 1) * LANES] = rot.astype(BF16)
    for hd in range(GQA_KV_HEADS):
        normed, rot = head(o_k + hd * LANES, qw + hd * LANES, kg_ref, kgs_ref if rope else None)
        k_out[0, :, hd * LANES:(hd + 1) * LANES] = rot.astype(BF16)
        kn_out[0, :, hd * LANES:(hd + 1) * LANES] = normed
    vp_out[0] = _with_sum_lane(z[:, o_vp:o_vp + kw]).astype(BF16)
    v_out[0] = z[:, o_v:o_v + LANES]

    u = z[:, o_u:o_u + GMLP_CH]
    vg = z[:, o_vg:o_vg + GMLP_CH]
    mu = jnp.mean(vg, axis=-1, keepdims=True)
    cen = vg - mu
    var = jnp.mean(cen * cen, axis=-1, keepdims=True)
    vn = (cen * lax.rsqrt(var + EPS) * lng_ref[...] + lnb_ref[...]).astype(BF16)
    bias = bs_ref[...]
    rows = x.shape[0]
    for cidx in range(rows // CHUNK):
        r0 = cidx * CHUNK
        for g in range(GMLP_GROUPS):
            c0 = g * LANES
            mixed = jnp.dot(ws_ref[g], vn[r0:r0 + CHUNK, c0:c0 + LANES], preferred_element_type=F32)
            g_out[0, r0:r0 + CHUNK, c0:c0 + LANES] = (
                u[r0:r0 + CHUNK, c0:c0 + LANES] * (mixed + bias[:, c0:c0 + LANES])).astype(BF16)


def l1_inproj(x, mods, mod_off, n1, w, tables):
    bm, sm, d = x.shape
    tr = min(ROW_TILE, sm)
    rope = tables is not None
    qw = GQA_HEADS * LANES
    kw = GQA_KV_HEADS * LANES
    row = lambda width: pl.BlockSpec((1, tr, width), lambda b, i: (b, i, 0))
    vec = _const_spec((1, LANES))
    in_specs = [row(d), pl.BlockSpec((1, SUBLANES, d), lambda b, i: (b + mod_off, 0, 0)),
                _const_spec((1, d)), _const_spec(w["win"].shape)]
    args = [x, mods, n1, w["win"]]
    if rope:
        in_specs += [_const_spec(w["wsw"].shape), vec, vec, vec, vec]
        args += [w["wsw"], w["qg"], w["qgs"], w["kg"], w["kgs"]]
    else:
        in_specs += [vec, vec]
        args += [w["qg"], w["kg"]]
    in_specs += [_const_spec((1, GMLP_CH)), _const_spec((1, GMLP_CH)),
                 _const_spec(w["ws"].shape), _const_spec((CHUNK, GMLP_CH))]
    args += [w["lng"], w["lnb"], w["ws"], w["bs"]]
    if rope:
        in_specs += [pl.BlockSpec((tr, LANES), lambda b, i: (i, 0))] * 2
        args += list(tables)
    out_shape = [jax.ShapeDtypeStruct((bm, sm, qw), BF16),
                 jax.ShapeDtypeStruct((bm, sm, kw), BF16),
                 jax.ShapeDtypeStruct((bm, sm, kw), F32),
                 jax.ShapeDtypeStruct((bm, sm, kw), BF16),
                 jax.ShapeDtypeStruct((bm, sm, LANES), F32),
                 jax.ShapeDtypeStruct((bm, sm, GMLP_CH), BF16)]
    out_specs = [row(qw), row(kw), row(kw), row(kw), row(LANES), row(GMLP_CH)]
    return pl.pallas_call(
        functools.partial(_l1_inproj_kernel, rope=rope),
        out_shape=out_shape, grid=(bm, sm // tr), in_specs=in_specs, out_specs=out_specs,
        compiler_params=_params(("parallel", "parallel"), VMEM_LIMIT),
        name="l1_inproj_rope" if rope else "l1_inproj",
    )(*args)


def _post_kernel(a_ref, c_ref, x_ref, m_ref, n2_ref, wa_ref, wc_ref, rwh_ref, rwl_ref, rb_ref, tri_ref, base_ref,
                 x1_out, xm_out, idx_out, wts_out, rank_out, cnt_out, run_ref):
    first =(pl.program_id(0) == 0) & (pl.program_id(1) == 0)

    @pl.when(first)
    def _():
        run_ref[...] = base_ref[...]

    m = m_ref[0]
    y = (jnp.dot(a_ref[0], wa_ref[...], preferred_element_type=F32)
         + jnp.dot(c_ref[0], wc_ref[...], preferred_element_type=F32))
    x1 = x_ref[0] + m[2:3] * y
    x1_out[0] = x1
    xm = _rms(x1, n2_ref[...]) * (1.0 + m[4:5]) + m[3:4]
    xh = xm.astype(BF16)
    xm_out[0] = _pack_rows(xh)

    xl = (xm - xh.astype(F32)).astype(BF16)
    logits = (jnp.dot(xh, rwh_ref[...], preferred_element_type=F32)
              + jnp.dot(xl, rwh_ref[...], preferred_element_type=F32)
              + jnp.dot(xh, rwl_ref[...], preferred_element_type=F32)) + rb_ref[...]
    rows = logits.shape[0]
    lane = lax.broadcasted_iota(jnp.int32, (rows, LANES), 1).astype(F32)
    work = logits
    vals, hots = [], []
    idx_acc = jnp.zeros((rows, LANES), F32)
    for k in range(TOP_K):
        top = jnp.max(work, axis=-1, keepdims=True)
        sel = jnp.min(jnp.where(work == top, lane, float(LANES)), axis=-1, keepdims=True)
        hot = lane == sel
        vals.append(top)
        hots.append(hot)
        idx_acc = idx_acc + jnp.where(lane == float(k), sel, 0.0)
        work = jnp.where(hot, -jnp.inf, work)
    exps = [jnp.exp(v - vals[0]) for v in vals]
    denom = exps[0] + exps[1] + exps[2] + exps[3]
    wcols = [jnp.broadcast_to(exps[k] / denom, (rows, LANES)) for k in range(TOP_K)]
    wcols += [jnp.zeros((rows, LANES), F32)] * (SUBLANES - TOP_K)
    wts = jnp.concatenate(wcols, axis=1).reshape(rows, SUBLANES, LANES)

    chosen = jnp.zeros((rows, LANES), F32)
    for hot in hots:
        chosen = chosen + hot.astype(F32)
    before = jnp.dot(tri_ref[...], chosen.astype(BF16), preferred_element_type=F32) + run_ref[0:1, :]
    rank = jnp.zeros((rows, LANES), F32)
    for k in range(TOP_K):
        rk = jnp.sum(jnp.where(hots[k], before, 0.0), axis=-1, keepdims=True)
        rank = rank + jnp.where(lane == float(k), rk, 0.0)
    run_ref[0:1, :] = run_ref[0:1, :] + jnp.sum(chosen, axis=0, keepdims=True)
    idx_out[0] = idx_acc.astype(jnp.int32)
    wts_out[0] = wts
    rank_out[0] = rank.astype(jnp.int32)
    cnt_out[...] = run_ref[...]


def post_mixer(attn, other, x, mods, mod_off, n2, wa, wc, rwh, rwl, rb, tri, base):
    bm, sm, d = x.shape
    tr = tri.shape[0]
    row = lambda width: pl.BlockSpec((1, tr, width), lambda b, i: (b, i, 0))
    tile_rows = pl.BlockSpec((1, tr) + ROW_WORDS, lambda b, i: (b, i, 0, 0))
    out_shape = [jax.ShapeDtypeStruct((bm, sm, d), F32),
                 jax.ShapeDtypeStruct((bm, sm) + ROW_WORDS, jnp.int32),
                 jax.ShapeDtypeStruct((bm, sm, LANES), jnp.int32),
                 jax.ShapeDtypeStruct((bm, sm, SUBLANES, LANES), F32),
                 jax.ShapeDtypeStruct((bm, sm, LANES), jnp.int32),
                 jax.ShapeDtypeStruct((SUBLANES, LANES), F32)]
    return pl.pallas_call(
        _post_kernel,
        out_shape=out_shape, grid=(bm, sm // tr),
        in_specs=[row(attn.shape[-1]), row(other.shape[-1]), row(d),
                  pl.BlockSpec((1, SUBLANES, d), lambda b, i: (b + mod_off, 0, 0)),
                  _const_spec((1, d)), _const_spec(wa.shape), _const_spec(wc.shape),
                  _const_spec(rwh.shape), _const_spec(rwl.shape), _const_spec((1, LANES)),
                  _const_spec(tri.shape), _const_spec((SUBLANES, LANES))],
        out_specs=[row(d), tile_rows, row(LANES),
                   pl.BlockSpec((1, tr, SUBLANES, LANES), lambda b, i: (b, i, 0, 0)),
                   row(LANES), _const_spec((SUBLANES, LANES))],
        scratch_shapes=[pltpu.VMEM((SUBLANES, LANES), F32)],
        compiler_params=_params(("arbitrary", "arbitrary"), VMEM_LIMIT),
        name="post_mixer_route",
    )(attn, other, x, mods, n2, wa, wc, rwh, rwl, rb, tri, base)


def _plan_kernel(off_ref, idx_ref, rank_ref, pos_out):
    idx = idx_ref[...]
    pos = rank_ref[...]
    for e in range(N_EXPERTS):
        pos = pos + jnp.where(idx == e, off_ref[e], 0)
    pos_out[...] = pos


def plan_positions(offsets, idx, rank):
    n = idx.shape[0]
    tr = min(2048, n)
    spec = pl.BlockSpec((tr, LANES), lambda i, off: (i, 0))
    return pl.pallas_call(
        _plan_kernel,
        out_shape=jax.ShapeDtypeStruct((n, LANES), jnp.int32),
        grid_spec=pltpu.PrefetchScalarGridSpec(
            num_scalar_prefetch=1, grid=(n // tr,), in_specs=[spec, spec], out_specs=spec),
        compiler_params=_params(("parallel",)),
        name="plan_positions",
    )(offsets, idx, rank)


def _sc_worker_id():
    return lax.axis_index("s") * SC_CORES + lax.axis_index("c")


def _sc_chunk_rows(tile, dtype):
    row_bytes = math.prod(tile) * jnp.dtype(dtype).itemsize
    return min(SC_CHUNK_BYTES // row_bytes, SC_MAX_INDICES)


def dispatch_rows(xms, posk, n_rows):
    n_total = sum(x.shape[0] for x in xms)
    tile, dtype = xms[0].shape[1:], xms[0].dtype
    chunk = _sc_chunk_rows(tile, dtype)
    starts, s0 = [], 0
    for x in xms:
        assert x.shape[0] % (SC_WORKERS * chunk) == 0
        starts.append(s0)
        s0 += x.shape[0]
    mesh = plsc.VectorSubcoreMesh(core_axis_name="c", subcore_axis_name="s")

    @functools.partial(
        pl.kernel, mesh=mesh, out_type=jax.ShapeDtypeStruct((n_rows,) + tile, dtype),
        scratch_types=[pltpu.VMEM((chunk,), jnp.int32), pltpu.VMEM((chunk,) + tile, dtype),
                       pltpu.SemaphoreType.DMA],
        name="dispatch_rows_sc")
    def scatter(*refs):
        x_refs, pos_hbm, xs_hbm, idx_v, rows_v, sem = refs[:len(xms)], *refs[len(xms):]
        wid = _sc_worker_id()
        for x_hbm, start in zip(x_refs, starts):
            per_worker = x_hbm.shape[0] // SC_WORKERS

            @pl.loop(0, per_worker // chunk)
            def _(c):
                t0 = wid * per_worker + c * chunk
                pltpu.sync_copy(x_hbm.at[pl.ds(t0, chunk)], rows_v)
                for k in range(TOP_K):
                    pltpu.sync_copy(pos_hbm.at[pl.ds(k * n_total + start + t0, chunk)], idx_v)
                    pltpu.async_copy(rows_v, xs_hbm.at[idx_v], sem).wait()

    return scatter(*xms, posk)


def gather_rows(y, posk):
    n_pairs = posk.shape[0]
    tile, dtype = y.shape[1:], y.dtype
    chunk = _sc_chunk_rows(tile, dtype)
    per_worker = n_pairs // SC_WORKERS
    assert per_worker % chunk == 0
    mesh = plsc.VectorSubcoreMesh(core_axis_name="c", subcore_axis_name="s")

    @functools.partial(
        pl.kernel, mesh=mesh, out_type=jax.ShapeDtypeStruct((n_pairs,) + tile, dtype),
        scratch_types=[pltpu.VMEM((chunk,), jnp.int32), pltpu.VMEM((chunk,) + tile, dtype),
                       pltpu.SemaphoreType.DMA],
        name="gather_rows_sc")
    def gather(y_hbm, pos_hbm, out_hbm, idx_v, rows_v, sem):
        wid = _sc_worker_id()

        @pl.loop(0, per_worker // chunk)
        def _(c):
            base = wid * per_worker + c * chunk
            pltpu.sync_copy(pos_hbm.at[pl.ds(base, chunk)], idx_v)
            pltpu.async_copy(y_hbm.at[idx_v], rows_v, sem).wait()
            pltpu.sync_copy(rows_v, out_hbm.at[pl.ds(base, chunk)])

    return gather(y, posk)


def _ffn_kernel(te_ref, nu_ref, first_ref, slot_ref, nxt_ref, xs_ref, w1_hbm, b1_ref, w2_hbm, b2_ref, y_ref,
                w1f, w2f, w1b, w2b, sem):
    i = pl.program_id(0)

    def weight_copies(e, s):
        return (pltpu.make_async_copy(w1_hbm.at[e], w1f.at[s], sem.at[0, s]),
                pltpu.make_async_copy(w2_hbm.at[e], w2f.at[s], sem.at[1, s]))

    @pl.when(i < nu_ref[0])
    def _():
        s = slot_ref[i]

        @pl.when(first_ref[i] == 1)
        def _():
            @pl.when(i == 0)
            def _():
                for cp in weight_copies(te_ref[i], s):
                    cp.start()
            for cp in weight_copies(te_ref[i], s):
                cp.wait()

            @pl.when(nxt_ref[i] >= 0)
            def _():
                for cp in weight_copies(nxt_ref[i], 1 - s):
                    cp.start()
            for c in range(D_MODEL // LANES):
                w1b[c * LANES:(c + 1) * LANES, :] = w1f[s, c * LANES:(c + 1) * LANES, :].astype(BF16)
            for c in range(D_EXPERT // LANES):
                w2b[c * LANES:(c + 1) * LANES, :] = w2f[s, c * LANES:(c + 1) * LANES, :].astype(BF16)

        x = _unpack_rows(xs_ref[...])
        h = jnp.dot(x, w1b[...], preferred_element_type=F32) + b1_ref[0]
        g = jnp.minimum(h[:, :D_EXPERT], SWIGLU_LIMIT)
        lin = jnp.clip(h[:, D_EXPERT:], -SWIGLU_LIMIT, SWIGLU_LIMIT)
        a = (lin + 1.0) * (g * jax.nn.sigmoid(SWIGLU_ALPHA * g))
        y = jnp.dot(a.astype(BF16), w2b[...], preferred_element_type=F32) + b2_ref[0]
        y_ref[...] = _pack_rows(y.astype(BF16))

    @pl.when(i >= nu_ref[0])
    def _():
        y_ref[...] = jnp.zeros(y_ref.shape, y_ref.dtype)


def grouped_ffn(sched, xs, w1, b1, w2, b2):
    r = xs.shape[0]
    d = w1.shape[1]
    nt = r // FFN_TILE
    tile = (FFN_TILE,) + xs.shape[1:]
    rows = lambda i, te, nu, *_: (jnp.minimum(i, nu[0] - 1), 0, 0)
    bsel = lambda i, te, *_: (te[i], 0, 0)
    return pl.pallas_call(
        _ffn_kernel,
        out_shape=jax.ShapeDtypeStruct(xs.shape, xs.dtype),
        grid_spec=pltpu.PrefetchScalarGridSpec(
            num_scalar_prefetch=5, grid=(nt,),
            in_specs=[pl.BlockSpec(tile, rows),
                      pl.BlockSpec(memory_space=pl.ANY),
                      pl.BlockSpec((1, 1, 2 * D_EXPERT), bsel),
                      pl.BlockSpec(memory_space=pl.ANY),
                      pl.BlockSpec((1, 1, d), bsel)],
            out_specs=pl.BlockSpec(tile, lambda i, *_: (i, 0, 0)),
            scratch_shapes=[pltpu.VMEM((2, d, 2 * D_EXPERT), F32), pltpu.VMEM((2, D_EXPERT, d), F32),
                            pltpu.VMEM((d, 2 * D_EXPERT), BF16), pltpu.VMEM((D_EXPERT, d), BF16),
                            pltpu.SemaphoreType.DMA((2, 2))]),
        compiler_params=_params(("arbitrary",), VMEM_LIMIT),
        name="grouped_ffn",
    )(*sched, xs, w1, b1.reshape(N_EXPERTS, 1, -1), w2, b2.reshape(N_EXPERTS, 1, -1))


def _combine_kernel(x1_ref, wts_ref, m_ref, fn_ref, y0_ref, y1_ref, y2_ref, y3_ref, o_ref, *, final):
    w = wts_ref[0]
    rows = lambda ref: pltpu.bitcast(ref[...], BF16).astype(F32)
    acc = w[:, 0:1, :] * rows(y0_ref)
    for k, y_ref in ((1, y1_ref), (2, y2_ref), (3, y3_ref)):
        acc = acc + w[:, k:k + 1, :] * rows(y_ref)
    out = x1_ref[0] + m_ref[0][5:6] * acc.reshape(x1_ref.shape[1], D_MODEL)
    if final:
        out = _rms(out, fn_ref[...])
    o_ref[0] = out


def combine_rows(x1, wts, mods, mod_off, fn, yg, row_off, n_total, *, final):
    bm, sm, d = x1.shape
    tr = min(COMBINE_TILE, sm)
    nb = sm // tr
    row = lambda width: pl.BlockSpec((1, tr, width), lambda b, i: (b, i, 0))
    ysel = lambda k: pl.BlockSpec((tr,) + yg.shape[1:],
                                  lambda b, i: ((k * n_total + row_off) // tr + b * nb + i, 0, 0))
    return pl.pallas_call(
        functools.partial(_combine_kernel, final=final),
        out_shape=jax.ShapeDtypeStruct((bm, sm, d), F32),
        grid=(bm, nb),
        in_specs=[row(d), pl.BlockSpec((1, tr, SUBLANES, LANES), lambda b, i: (b, i, 0, 0)),
                  pl.BlockSpec((1, SUBLANES, d), lambda b, i: (b + mod_off, 0, 0)),
                  _const_spec((1, d))] + [ysel(k) for k in range(TOP_K)],
        out_specs=row(d),
        compiler_params=_params(("parallel", "parallel"), VMEM_LIMIT),
        name="combine_rows",
    )(x1, wts, mods, fn, yg, yg, yg, yg)


def _axial_angles(n_tokens, rot_dim):
    t = jnp.arange(n_tokens)
    rows = (t // GRID_W).astype(F32)
    cols = (t % GRID_W).astype(F32)
    n_freq = rot_dim // 4
    inv = ROPE_THETA ** (-jnp.arange(n_freq, dtype=F32) / n_freq)
    return jnp.concatenate([rows[:, None] * inv, cols[:, None] * inv], axis=-1)


def _lane_table(parts, n):
    cols = []
    for p in parts:
        cols.append(jnp.broadcast_to(jnp.asarray(p, F32), (n, p.shape[-1])) if hasattr(p, "shape") else p)
    return jnp.concatenate(cols, axis=-1)


def _swap_halves(w):
    half = w.shape[-1] // 2
    return jnp.concatenate([-w[..., half:], w[..., :half]], axis=-1)


def _prep_l0(w_in, q_norm, kv_norm, w_uq, w_uk, w_uv, w_out):
    d = w_in.shape[0]
    o_kr = MLA_Q_RANK + MLA_KV_RANK
    kr_cols = w_in[:, o_kr:o_kr + MLA_ROPE]
    win = jnp.concatenate(
        [w_in[:, :o_kr], kr_cols, _swap_halves(kr_cols), jnp.zeros((d, LANES - 2 * MLA_ROPE), F32),
         w_in[:, o_kr + MLA_ROPE:]], axis=1).astype(BF16)
    qk = MLA_NOPE + MLA_ROPE
    wuq3 = w_uq.reshape(MLA_Q_RANK, MLA_HEADS, qk)
    wuq = jnp.pad(wuq3, ((0, 0), (0, 0), (0, LANES - qk))).reshape(MLA_Q_RANK, -1).astype(BF16)
    wuqs3 = jnp.concatenate(
        [jnp.zeros((MLA_Q_RANK, MLA_HEADS, MLA_NOPE), F32), _swap_halves(wuq3[:, :, MLA_NOPE:]),
         jnp.zeros((MLA_Q_RANK, MLA_HEADS, LANES - qk), F32)], axis=-1)
    wuqs = wuqs3.reshape(MLA_Q_RANK, -1).astype(BF16)
    wuk3 = w_uk.reshape(MLA_KV_RANK, MLA_HEADS, MLA_NOPE)
    wuk = jnp.pad(wuk3, ((0, 0), (0, 0), (0, LANES - MLA_NOPE))).reshape(MLA_KV_RANK, -1).astype(BF16)
    wuv3 = w_uv.reshape(MLA_KV_RANK, MLA_HEADS, MLA_V)
    wuv = jnp.pad(wuv3, ((0, 0), (0, 0), (0, LANES - MLA_V))).reshape(MLA_KV_RANK, -1).astype(BF16)
    eye = jnp.eye(MLA_ROPE, dtype=F32)
    e_head = jnp.concatenate([jnp.zeros((MLA_ROPE, MLA_NOPE), F32), eye,
                              jnp.zeros((MLA_ROPE, LANES - qk), F32)], axis=1)
    e = jnp.pad(jnp.tile(e_head, (1, MLA_HEADS)), ((0, LANES - MLA_ROPE), (0, 0))).astype(BF16)
    wa3 = w_out[:MLA_HEADS * MLA_V].reshape(MLA_HEADS, MLA_V, d)
    wa = jnp.pad(wa3, ((0, 0), (0, LANES - MLA_V), (0, 0))).reshape(MLA_HEADS * LANES, d).astype(BF16)
    wc = w_out[MLA_HEADS * MLA_V:].astype(BF16)
    return dict(win=win, qg=q_norm.reshape(1, -1), kvg=kv_norm.reshape(1, -1), wuq=wuq, wuqs=wuqs,
                wuk=wuk, e=e, wuv=wuv, wa=wa, wc=wc)


def _l0_tables(n):
    ang = _axial_angles(n, MLA_ROPE)
    cos, sin = jnp.cos(ang), jnp.sin(ang)
    one = jnp.ones((n, 1), F32)
    zero = jnp.zeros((n, 1), F32)
    rest = LANES - MLA_NOPE - MLA_ROPE
    cq = jnp.concatenate([jnp.tile(one, (1, MLA_NOPE)), cos, cos, jnp.tile(one, (1, rest))], axis=1)
    sq = jnp.concatenate([jnp.tile(zero, (1, MLA_NOPE)), sin, sin, jnp.tile(zero, (1, rest))], axis=1)
    ck = jnp.concatenate([cos, cos, jnp.tile(zero, (1, LANES - MLA_ROPE))], axis=1)
    sk = jnp.concatenate([sin, sin, jnp.tile(zero, (1, LANES - MLA_ROPE))], axis=1)
    return cq, sq, ck, sk


def _pad_heads(w, n_heads, dim):
    d = w.shape[0]
    return jnp.pad(w.reshape(d, n_heads, dim), ((0, 0), (0, 0), (0, LANES - dim))).reshape(d, n_heads * LANES)


def _prep_l1(w_in, q_norm, k_norm, ln_g, ln_b, w_s, b_s, w_out):
    d = w_in.shape[0]
    qd = GQA_HEADS * GQA_HEAD_DIM
    kd = GQA_KV_HEADS * GQA_HEAD_DIM
    wq, wk, wv = w_in[:, :qd], w_in[:, qd:qd + kd], w_in[:, qd + kd:qd + 2 * kd]
    rest = w_in[:, qd + 2 * kd:]
    win = jnp.concatenate([_pad_heads(wq, GQA_HEADS, GQA_HEAD_DIM), _pad_heads(wk, GQA_KV_HEADS, GQA_HEAD_DIM),
                           _pad_heads(wv, GQA_KV_HEADS, GQA_HEAD_DIM), wv, rest], axis=1).astype(BF16)
    swq = _swap_halves(wq.reshape(d, GQA_HEADS, GQA_HEAD_DIM)).reshape(d, qd)
    swk = _swap_halves(wk.reshape(d, GQA_KV_HEADS, GQA_HEAD_DIM)).reshape(d, kd)
    wsw = jnp.concatenate([_pad_heads(swq, GQA_HEADS, GQA_HEAD_DIM),
                           _pad_heads(swk, GQA_KV_HEADS, GQA_HEAD_DIM)], axis=1).astype(BF16)
    half = GQA_HEAD_DIM // 2
    padg = lambda g: jnp.pad(g, (0, LANES - GQA_HEAD_DIM)).reshape(1, LANES)
    swapg = lambda g: jnp.concatenate([g[half:], g[:half]])
    wa3 = w_out[:qd].reshape(GQA_HEADS, GQA_HEAD_DIM, d)
    wa = jnp.pad(wa3, ((0, 0), (0, LANES - GQA_HEAD_DIM), (0, 0))).reshape(GQA_HEADS * LANES, d).astype(BF16)
    wc = w_out[qd:].astype(BF16)
    bs = jnp.repeat(b_s.T, LANES, axis=1)
    return dict(win=win, wsw=wsw, qg=padg(q_norm), qgs=padg(swapg(q_norm)), kg=padg(k_norm),
                kgs=padg(swapg(k_norm)), lng=ln_g.reshape(1, -1), lnb=ln_b.reshape(1, -1),
                ws=w_s.astype(BF16), bs=bs, wa=wa, wc=wc)


def _l1_tables(n):
    ang = _axial_angles(n, GQA_HEAD_DIM)
    cos, sin = jnp.cos(ang), jnp.sin(ang)
    pad = LANES - GQA_HEAD_DIM
    c = jnp.concatenate([cos, cos, jnp.ones((n, pad), F32)], axis=1)
    s = jnp.concatenate([sin, sin, jnp.zeros((n, pad), F32)], axis=1)
    return c, s


def _pad_lanes(x, width):
    return jnp.pad(x, [(0, 0)] * (x.ndim - 1) + [(0, width - x.shape[-1])])


def routed_ffn(groups, mods, n2, moe, tri, final_norm, *, final):
    router_w, router_b, w1, b1, w2, b2 = moe
    rw = _pad_lanes(router_w, LANES)
    rwh = rw.astype(BF16)
    rwl = (rw - rwh.astype(F32)).astype(BF16)
    rb = jnp.concatenate([router_b, jnp.full((LANES - N_EXPERTS,), NEG_BIG, F32)]).reshape(1, LANES)
    base = jnp.zeros((SUBLANES, LANES), F32)
    routed = []
    for g in groups:
        x1, xm, idx, wts, rank, base = post_mixer(
            g["attn"], g["other"], g["x"], mods, g["mod_off"], n2, g["wa"], g["wc"], rwh, rwl, rb, tri, base)
        routed.append((x1, xm, idx, wts, rank))
    n_total = sum(r[0].shape[0] * r[0].shape[1] for r in routed)
    nt = n_total * TOP_K // FFN_TILE + N_EXPERTS

    counts = base[0, :N_EXPERTS].astype(jnp.int32)
    tiles = (counts + FFN_TILE - 1) // FFN_TILE
    tile_end = jnp.cumsum(tiles)
    offsets = (tile_end - tiles) * FFN_TILE
    n_used = tile_end[-1:].astype(jnp.int32)
    experts = jnp.arange(N_EXPERTS)
    busy = tiles > 0
    slot_e = (jnp.cumsum(busy) - 1) % 2
    later = jnp.where(busy[None, :] & (experts[None, :] > experts[:, None]), experts[None, :], N_EXPERTS)
    nxt_e = jnp.min(later, axis=1)
    nxt_e = jnp.where(nxt_e == N_EXPERTS, -1, nxt_e)
    tile_expert = jnp.sum(jnp.arange(nt)[:, None] >= tile_end[None, :], axis=1)
    tile_expert = jnp.minimum(tile_expert, tile_expert[n_used[0] - 1])
    first = jnp.concatenate([jnp.ones((1,), bool), tile_expert[1:] != tile_expert[:-1]])
    sched = tuple(a.astype(jnp.int32) for a in
                  (tile_expert, n_used, first, slot_e[tile_expert], nxt_e[tile_expert]))

    row_tile = ROW_WORDS
    positions, xms = [], []
    for (x1, xm, idx, wts, rank) in routed:
        n = idx.shape[0] * idx.shape[1]
        pos = plan_positions(offsets, idx.reshape(n, LANES), rank.reshape(n, LANES))
        positions.append(pos[:, :TOP_K])
        xms.append(xm.reshape((n,) + row_tile))
    posk = jnp.concatenate(positions, axis=0).T.reshape(-1)
    xs = dispatch_rows(xms, posk, nt * FFN_TILE)
    y = grouped_ffn(sched, xs, w1, b1, w2, b2)
    outs = []
    for g, (x1, xm, idx, wts, rank), pos in zip(groups, routed, positions):
        yg = gather_rows(y, pos.T.reshape(-1))
        outs.append(combine_rows(x1, wts, mods, g["mod_off"], final_norm, yg, 0, pos.shape[0], final=final))
    return outs


def kernel(x_prompt, x_sample, cache_l0_ckv, cache_l0_krope, cache_l1_k, cache_l1_v, c, c_ctx,
           l0_ada_w, l0_ada_b, l0_norm1, l0_w_in, l0_q_norm, l0_kv_norm, l0_w_uq, l0_w_uk, l0_w_uv,
           l0_conv_w, l0_conv_b, l0_conv_ln_g, l0_conv_ln_b, l0_w_out, l0_norm2,
           l0_router_w, l0_router_b, l0_w1, l0_b1, l0_w2, l0_b2,
           l1_ada_w, l1_ada_b, l1_norm1, l1_w_in, l1_q_norm, l1_k_norm, l1_gmlp_ln_g, l1_gmlp_ln_b,
           l1_w_s, l1_b_s, l1_w_out, l1_norm2,
           l1_router_w, l1_router_b, l1_w1, l1_b1, l1_w2, l1_b2,
           final_norm):
    bp, sp, d = x_prompt.shape
    bs, ss, _ = x_sample.shape
    past = cache_l0_ckv.shape[1]
    n_p = bp * sp

    cond8 = jnp.concatenate([c_ctx[None], c, jnp.zeros((SUBLANES - 1 - bs, d), F32)], axis=0)
    mods0 = adaln(cond8, l0_ada_w, l0_ada_b)
    mods1 = adaln(cond8, l1_ada_w, l1_ada_b)
    tri = jnp.tril(jnp.ones((ROW_TILE, ROW_TILE), F32), -1).astype(BF16)
    fn = final_norm.reshape(1, d)

    w0 = _prep_l0(l0_w_in, l0_q_norm, l0_kv_norm, l0_w_uq, l0_w_uk, l0_w_uv, l0_w_out)
    n1 = l0_norm1.reshape(1, d)
    hp = x_prompt.reshape(1, n_p, d)
    q_p, k_p, v_p, ckv_p, kr_p, u_p = l0_inproj(hp, mods0, 0, n1, w0, None)
    q_s, k_s, v_s, _, _, u_s = l0_inproj(x_sample, mods0, 1, n1, w0, _l0_tables(ss))
    k_c, v_c = mla_ctx_kv(cache_l0_ckv, _pad_lanes(cache_l0_krope, LANES), w0)
    hw = MLA_HEADS * LANES
    att_p = attention(q_p.reshape(bp, sp, hw), k_p.reshape(bp, sp, hw), v_p.reshape(bp, sp, hw),
                      n_heads=MLA_HEADS, n_kv=MLA_HEADS, scale=MLA_SCALE, heads_per_step=MLA_HEADS)
    att_s = attention(q_s, jnp.concatenate([k_c, k_s], axis=1), jnp.concatenate([v_c, v_s], axis=1),
                      n_heads=MLA_HEADS, n_kv=MLA_HEADS, scale=MLA_SCALE, heads_per_step=LATENT_HEADS_PER_STEP)
    conv_p = conformer_conv(u_p.reshape(bp, sp, CONV_CH), l0_conv_w, l0_conv_b, l0_conv_ln_g, l0_conv_ln_b)
    conv_s = conformer_conv(u_s, l0_conv_w, l0_conv_b, l0_conv_ln_g, l0_conv_ln_b)
    groups = [dict(attn=att_p.reshape(1, n_p, hw), other=conv_p.reshape(1, n_p, CONV_CH), x=hp, mod_off=0,
                   wa=w0["wa"], wc=w0["wc"]),
              dict(attn=att_s, other=conv_s, x=x_sample, mod_off=1, wa=w0["wa"], wc=w0["wc"])]
    hp, hs = routed_ffn(groups, mods0, l0_norm2.reshape(1, d),
                        (l0_router_w, l0_router_b, l0_w1, l0_b1, l0_w2, l0_b2), tri, fn, final=False)
    new_l0_ckv = ckv_p.reshape(bp, sp, MLA_KV_RANK)
    new_l0_krope = kr_p.reshape(bp, sp, MLA_ROPE)

    w1p = _prep_l1(l1_w_in, l1_q_norm, l1_k_norm, l1_gmlp_ln_g, l1_gmlp_ln_b, l1_w_s, l1_b_s, l1_w_out)
    n1 = l1_norm1.reshape(1, d)
    q_p, k_p, kn_p, vp_p, vraw_p, gat_p = l1_inproj(hp, mods1, 0, n1, w1p, None)
    q_s, k_s, _, vp_s, _, gat_s = l1_inproj(hs, mods1, 1, n1, w1p, _l1_tables(ss))
    qw = GQA_HEADS * LANES
    kw = GQA_KV_HEADS * LANES
    pad_kv = lambda t: _pad_lanes(t, LANES).reshape(bs, past, kw).astype(BF16)
    att_p = attention(q_p.reshape(bp, sp, qw), k_p.reshape(bp, sp, kw), vp_p.reshape(bp, sp, kw),
                      n_heads=GQA_HEADS, n_kv=GQA_KV_HEADS, scale=GQA_SCALE, heads_per_step=GQA_HEADS)
    att_s = attention(q_s, jnp.concatenate([pad_kv(cache_l1_k), k_s], axis=1),
                      jnp.concatenate([_with_sum_lane(pad_kv(cache_l1_v)), vp_s], axis=1),
                      n_heads=GQA_HEADS, n_kv=GQA_KV_HEADS, scale=GQA_SCALE, heads_per_step=LATENT_HEADS_PER_STEP)
    groups = [dict(attn=att_p.reshape(1, n_p, qw), other=gat_p, x=hp, mod_off=0, wa=w1p["wa"], wc=w1p["wc"]),
              dict(attn=att_s, other=gat_s, x=hs, mod_off=1, wa=w1p["wa"], wc=w1p["wc"])]
    yp, ys = routed_ffn(groups, mods1, l1_norm2.reshape(1, d),
                        (l1_router_w, l1_router_b, l1_w1, l1_b1, l1_w2, l1_b2), tri, fn, final=True)
    new_l1_k = kn_p.reshape(bp, sp, GQA_KV_HEADS, LANES)[..., :GQA_HEAD_DIM]
    new_l1_v = vraw_p.reshape(bp, sp, GQA_KV_HEADS, GQA_HEAD_DIM)
    return (yp.reshape(bp, sp, d), ys, new_l0_ckv, new_l0_krope, new_l1_k, new_l1_v)
```

```python
import functools
import math

import jax
import jax.numpy as jnp
from jax import lax
from jax.experimental import pallas as pl
from jax.experimental.pallas import tpu as pltpu
from jax.experimental.pallas import tpu_sc as plsc

F32 = jnp.float32
BF16 = jnp.bfloat16
HIGHEST = lax.Precision.HIGHEST

LANES = 128
SUBLANES = 8
VMEM_LIMIT = 56 * 1024 * 1024

D_MODEL = 1024
GRID_W = 64
ROPE_THETA = 10000.0
EPS = 1e-6
N_MOD = 6

MLA_HEADS = 8
MLA_NOPE = 64
MLA_ROPE = 32
MLA_V = 64
MLA_Q_RANK = 384
MLA_KV_RANK = 256
MLA_SCALE = 1.0 / math.sqrt(MLA_NOPE + MLA_ROPE)
CONV_CH = 512
CONV_WIDTH = 31
CONV_HALO = 16

GQA_HEADS = 8
GQA_KV_HEADS = 2
GQA_HEAD_DIM = 64
GQA_SCALE = 1.0 / math.sqrt(GQA_HEAD_DIM)
CHUNK = 128
GMLP_GROUPS = 4
GMLP_CH = 512

N_EXPERTS = 32
TOP_K = 4
D_EXPERT = 1024
SWIGLU_LIMIT = 7.0
SWIGLU_ALPHA = 1.702

ROW_TILE = 512
FFN_TILE = 256
SC_CORES = 2
SC_SUBCORES = 16
SC_WORKERS = SC_CORES * SC_SUBCORES
SC_CHUNK_BYTES = 256 * 1024
SC_MAX_INDICES = 128
COMBINE_TILE = 256
ATT_Q_TILE = 256
ATT_LONG_KEYS = 1024
V_SUM_LANE = 64
LATENT_HEADS_PER_STEP = 4
NEG_BIG = -1e30


def _params(sem, vmem=None):
    return pltpu.CompilerParams(dimension_semantics=sem, vmem_limit_bytes=vmem)


def _rms(x, g):
    return x * lax.rsqrt(jnp.mean(x * x, axis=-1, keepdims=True) + EPS) * g


ROW_WORDS = (SUBLANES // 2, LANES)


def _pack_rows(x_bf16):
    return pltpu.bitcast(x_bf16.reshape(x_bf16.shape[0], SUBLANES, LANES), jnp.int32)


def _unpack_rows(words):
    return pltpu.bitcast(words, BF16).reshape(words.shape[0], D_MODEL)


def _const_spec(shape):
    nd = len(shape)
    return pl.BlockSpec(shape, lambda *_: (0,) * nd)


def _adaln_kernel(c_ref, w_ref, b_ref, o_ref):
    c = c_ref[...]
    s = c * jax.nn.sigmoid(c)
    o_ref[...] = jnp.dot(s, w_ref[...], preferred_element_type=F32, precision=HIGHEST) + b_ref[...]


def adaln(cond8, ada_w, ada_b):
    d, n = ada_w.shape
    bn = n // 4
    m = pl.pallas_call(
        _adaln_kernel,
        out_shape=jax.ShapeDtypeStruct((SUBLANES, n), F32),
        grid=(n // bn,),
        in_specs=[_const_spec((SUBLANES, d)),
                  pl.BlockSpec((d, bn), lambda j: (0, j)),
                  pl.BlockSpec((1, bn), lambda j: (0, j))],
        out_specs=pl.BlockSpec((SUBLANES, bn), lambda j: (0, j)),
        compiler_params=_params(("arbitrary",), VMEM_LIMIT),
        name="adaln",
    )(cond8, ada_w, ada_b.reshape(1, n))
    m = m.reshape(SUBLANES, N_MOD, d)
    return jnp.pad(m, ((0, 0), (0, SUBLANES - N_MOD), (0, 0)))


def _l0_inproj_kernel(*refs, rope):
    if rope:
        (x_ref, m_ref, n1_ref, win_ref, qg_ref, kvg_ref, wuq_ref, wuqs_ref, wuk_ref, e_ref, wuv_ref,
         cq_ref, sq_ref, ck_ref, sk_ref, q_out, k_out, v_out, ckv_out, kr_out, u_out) = refs
    else:
        (x_ref, m_ref, n1_ref, win_ref, qg_ref, kvg_ref, wuq_ref, wuk_ref, e_ref, wuv_ref,
         q_out, k_out, v_out, ckv_out, kr_out, u_out) = refs
    x = x_ref[0]
    m = m_ref[0]
    h = _rms(x, n1_ref[...]) * (1.0 + m[1:2]) + m[0:1]
    z = jnp.dot(h.astype(BF16), win_ref[...], preferred_element_type=F32)
    c_q = z[:, 0:MLA_Q_RANK]
    c_kv = z[:, MLA_Q_RANK:MLA_Q_RANK + MLA_KV_RANK]
    kr_blk = z[:, 640:768]
    val = z[:, 768:768 + CONV_CH]
    gate = z[:, 768 + CONV_CH:768 + 2 * CONV_CH]

    cqn = _rms(c_q, qg_ref[...]).astype(BF16)
    q = jnp.dot(cqn, wuq_ref[...], preferred_element_type=F32)
    if rope:
        qs = jnp.dot(cqn, wuqs_ref[...], preferred_element_type=F32)
        cq = cq_ref[...]
        sq = sq_ref[...]
        for hd in range(MLA_HEADS):
            sl = slice(hd * LANES, (hd + 1) * LANES)
            q_out[0, :, sl] = (q[:, sl] * cq + qs[:, sl] * sq).astype(BF16)
        kr = kr_blk * ck_ref[...] + pltpu.roll(kr_blk, LANES - MLA_ROPE, 1) * sk_ref[...]
    else:
        q_out[0] = q.astype(BF16)
        kr = kr_blk

    ckv = _rms(c_kv, kvg_ref[...])
    ckv_out[0] = ckv
    kr_out[0] = kr_blk[:, 0:MLA_ROPE]
    ckv_b = ckv.astype(BF16)
    k = (jnp.dot(ckv_b, wuk_ref[...], preferred_element_type=F32)
         + jnp.dot(kr.astype(BF16), e_ref[...], preferred_element_type=F32))
    k_out[0] = k.astype(BF16)
    v_out[0] = _with_sum_lane(jnp.dot(ckv_b, wuv_ref[...], preferred_element_type=F32)).astype(BF16)
    u_out[0] = val * jax.nn.sigmoid(gate)


def l0_inproj(x, mods, mod_off, n1, w, tables):
    bm, sm, d = x.shape
    tr = min(ROW_TILE, sm)
    rope = tables is not None
    hp = MLA_HEADS * LANES
    row = lambda width: pl.BlockSpec((1, tr, width), lambda b, i: (b, i, 0))
    in_specs = [row(d),
                pl.BlockSpec((1, SUBLANES, d), lambda b, i: (b + mod_off, 0, 0)),
                _const_spec((1, d)), _const_spec(w["win"].shape),
                _const_spec((1, MLA_Q_RANK)), _const_spec((1, MLA_KV_RANK)),
                _const_spec(w["wuq"].shape)]
    args = [x, mods, n1, w["win"], w["qg"], w["kvg"], w["wuq"]]
    if rope:
        in_specs.append(_const_spec(w["wuqs"].shape))
        args.append(w["wuqs"])
    in_specs += [_const_spec(w["wuk"].shape), _const_spec(w["e"].shape), _const_spec(w["wuv"].shape)]
    args += [w["wuk"], w["e"], w["wuv"]]
    if rope:
        in_specs += [pl.BlockSpec((tr, LANES), lambda b, i: (i, 0))] * 4
        args += list(tables)
    out_shape = [jax.ShapeDtypeStruct((bm, sm, hp), BF16),
                 jax.ShapeDtypeStruct((bm, sm, hp), BF16),
                 jax.ShapeDtypeStruct((bm, sm, hp), BF16),
                 jax.ShapeDtypeStruct((bm, sm, MLA_KV_RANK), F32),
                 jax.ShapeDtypeStruct((bm, sm, MLA_ROPE), F32),
                 jax.ShapeDtypeStruct((bm, sm, CONV_CH), F32)]
    out_specs = [row(hp), row(hp), row(hp), row(MLA_KV_RANK), row(MLA_ROPE), row(CONV_CH)]
    return pl.pallas_call(
        functools.partial(_l0_inproj_kernel, rope=rope),
        out_shape=out_shape, grid=(bm, sm // tr), in_specs=in_specs, out_specs=out_specs,
        compiler_params=_params(("parallel", "parallel"), VMEM_LIMIT),
        name="l0_inproj_rope" if rope else "l0_inproj",
    )(*args)


def _mla_ctx_kv_kernel(ckv_ref, kr_ref, wuk_ref, e_ref, wuv_ref, k_out, v_out):
    ckv_b = ckv_ref[0].astype(BF16)
    k = (jnp.dot(ckv_b, wuk_ref[...], preferred_element_type=F32)
         + jnp.dot(kr_ref[0].astype(BF16), e_ref[...], preferred_element_type=F32))
    k_out[0] = k.astype(BF16)
    v_out[0] = _with_sum_lane(jnp.dot(ckv_b, wuv_ref[...], preferred_element_type=F32)).astype(BF16)


def mla_ctx_kv(ckv, kr128, w):
    b, s, _ = ckv.shape
    hp = MLA_HEADS * LANES
    blk = lambda width: pl.BlockSpec((1, s, width), lambda i: (i, 0, 0))
    return pl.pallas_call(
        _mla_ctx_kv_kernel,
        out_shape=[jax.ShapeDtypeStruct((b, s, hp), BF16)] * 2,
        grid=(b,),
        in_specs=[blk(MLA_KV_RANK), blk(LANES), _const_spec(w["wuk"].shape),
                  _const_spec(w["e"].shape), _const_spec(w["wuv"].shape)],
        out_specs=[blk(hp), blk(hp)],
        compiler_params=_params(("parallel",), VMEM_LIMIT),
        name="mla_ctx_kv",
    )(ckv, kr128, w["wuk"], w["e"], w["wuv"])


def _conv_kernel(prev_ref, cur_ref, next_ref, w_ref, b_ref, g_ref, beta_ref, o_ref, pad_ref, sh_ref, *, rb):
    i = pl.program_id(1)
    last = pl.num_programs(1) - 1
    zeros = jnp.zeros((CONV_HALO, CONV_CH), F32)
    pad_ref[0:CONV_HALO, :] = jnp.where(i == 0, zeros, prev_ref[0])
    pad_ref[CONV_HALO:CONV_HALO + rb, :] = cur_ref[0]
    pad_ref[CONV_HALO + rb:CONV_HALO + rb + CONV_HALO, :] = jnp.where(i == last, zeros, next_ref[0])
    span = rb + 2 * CONV_HALO - SUBLANES
    for r in range(1, SUBLANES):
        sh_ref[r] = pad_ref[r:r + span, :]
    w = w_ref[...]
    shift = CONV_HALO - CONV_WIDTH // 2
    acc = jnp.zeros((rb, CONV_CH), F32) + b_ref[...]
    for k in range(CONV_WIDTH):
        off = k + shift
        r, a = off % SUBLANES, off // SUBLANES * SUBLANES
        window = pad_ref[a:a + rb, :] if r == 0 else sh_ref[r, a:a + rb, :]
        acc = acc + window * w[k:k + 1, :]
    mu = jnp.mean(acc, axis=-1, keepdims=True)
    cen = acc - mu
    var = jnp.mean(cen * cen, axis=-1, keepdims=True)
    y = cen * lax.rsqrt(var + EPS) * g_ref[...] + beta_ref[...]
    o_ref[0] = (y * jax.nn.sigmoid(y)).astype(BF16)


def conformer_conv(u, conv_w, conv_b, ln_g, ln_b):
    b, s, c = u.shape
    rb = min(256, s)
    nh = rb // CONV_HALO
    n_halo_blocks = s // CONV_HALO
    wpad = jnp.pad(conv_w.reshape(CONV_WIDTH, c), ((0, 32 - CONV_WIDTH), (0, 0)))
    return pl.pallas_call(
        functools.partial(_conv_kernel, rb=rb),
        out_shape=jax.ShapeDtypeStruct((b, s, c), BF16),
        grid=(b, s // rb),
        in_specs=[pl.BlockSpec((1, CONV_HALO, c), lambda bi, i: (bi, jnp.maximum(i * nh - 1, 0), 0)),
                  pl.BlockSpec((1, rb, c), lambda bi, i: (bi, i, 0)),
                  pl.BlockSpec((1, CONV_HALO, c),
                               lambda bi, i: (bi, jnp.minimum((i + 1) * nh, n_halo_blocks - 1), 0)),
                  _const_spec((32, c)), _const_spec((1, c)), _const_spec((1, c)), _const_spec((1, c))],
        out_specs=pl.BlockSpec((1, rb, c), lambda bi, i: (bi, i, 0)),
        scratch_shapes=[pltpu.VMEM((rb + 2 * CONV_HALO, c), F32),
                        pltpu.VMEM((SUBLANES, rb + 2 * CONV_HALO - SUBLANES, c), F32)],
        compiler_params=_params(("parallel", "parallel"), VMEM_LIMIT),
        name="conformer_conv",
    )(u, u, u, wpad, conv_b.reshape(1, c), ln_g.reshape(1, c), ln_b.reshape(1, c))


def _with_sum_lane(v):
    lane = lax.broadcasted_iota(jnp.int32, (1, v.shape[-1]), 1)
    return v + ((lane & (LANES - 1)) == V_SUM_LANE).astype(v.dtype)


def _attn_kernel(q_ref, k_ref, v_ref, o_ref, *, heads, rep, scale, mxu_denominator):
    for hd in range(heads):
        g = hd // rep
        q = q_ref[0, :, hd * LANES:(hd + 1) * LANES]
        k = k_ref[0, :, g * LANES:(g + 1) * LANES]
        s = lax.dot_general(q, k, (((1,), (1,)), ((), ())), preferred_element_type=F32) * scale
        m = jnp.max(s, axis=-1, keepdims=True)
        if mxu_denominator:
            p = jnp.exp((s - m).astype(BF16))
            o = jnp.dot(p, v_ref[0, :, g * LANES:(g + 1) * LANES], preferred_element_type=F32)
            l = o[:, V_SUM_LANE:V_SUM_LANE + 1]
        else:
            p = jnp.exp(s - m)
            l = jnp.sum(p, axis=-1, keepdims=True)
            o = jnp.dot(p.astype(BF16), v_ref[0, :, g * LANES:(g + 1) * LANES], preferred_element_type=F32)
        o_ref[0, :, hd * LANES:(hd + 1) * LANES] = (o / l).astype(BF16)


def attention(q, k, v, *, n_heads, n_kv, scale, heads_per_step):
    b, sq, _ = q.shape
    sk = k.shape[1]
    rep = n_heads // n_kv
    tq = min(ATT_Q_TILE, sq)
    hb = heads_per_step
    grid = (b, n_heads // hb, sq // tq)
    if hb >= rep:
        kv_spec = pl.BlockSpec((1, sk, hb // rep * LANES), lambda bi, h, i: (bi, 0, h))
        kern_rep = rep
    else:
        assert rep % hb == 0
        kv_spec = pl.BlockSpec((1, sk, LANES), lambda bi, h, i: (bi, 0, h * hb // rep))
        kern_rep = hb
    kern = functools.partial(_attn_kernel, heads=hb, rep=kern_rep, scale=scale,
                             mxu_denominator=sk >= ATT_LONG_KEYS)
    q_spec = pl.BlockSpec((1, tq, heads_per_step * LANES), lambda bi, h, i: (bi, i, h))
    return pl.pallas_call(
        kern,
        out_shape=jax.ShapeDtypeStruct(q.shape, BF16),
        grid=grid, in_specs=[q_spec, kv_spec, kv_spec], out_specs=q_spec,
        compiler_params=_params(("parallel", "parallel", "parallel"), VMEM_LIMIT),
        name="attention",
    )(q, k, v)


def _l1_inproj_kernel(*refs, rope):
    if rope:
        (x_ref, m_ref, n1_ref, win_ref, wsw_ref, qg_ref, qgs_ref, kg_ref, kgs_ref, lng_ref, lnb_ref,
         ws_ref, bs_ref, c_ref, s_ref, q_out, k_out, vp_out, g_out) = refs
        kt_out = vt_out = None
    else:
        (x_ref, m_ref, n1_ref, win_ref, qg_ref, kg_ref, lng_ref, lnb_ref,
         ws_ref, bs_ref, q_out, k_out, vp_out, g_out, kt_out, vt_out) = refs
    x = x_ref[0]
    m = m_ref[0]
    hb = (_rms(x, n1_ref[...]) * (1.0 + m[1:2]) + m[0:1]).astype(BF16)
    z = jnp.dot(hb, win_ref[...], preferred_element_type=F32)
    qw = GQA_HEADS * LANES
    kw = GQA_KV_HEADS * LANES
    o_k, o_vp, o_v, o_u, o_vg = qw, qw + kw, qw + 2 * kw, qw + 2 * kw + LANES, qw + 2 * kw + LANES + GMLP_CH
    if rope:
        zs = jnp.dot(hb, wsw_ref[...], preferred_element_type=F32)
        cos = c_ref[...]
        sin = s_ref[...]

    def head(col, zcol, g_ref, gs_ref):
        t = z[:, col:col + LANES]
        r = lax.rsqrt(jnp.sum(t * t, axis=-1, keepdims=True) * (1.0 / GQA_HEAD_DIM) + EPS)
        normed = t * r * g_ref[...]
        if not rope:
            return normed, normed
        ts = zs[:, zcol:zcol + LANES]
        return normed, normed * cos + ts * r * gs_ref[...] * sin

    for hd in range(GQA_HEADS):
        _, rot = head(hd * LANES, hd * LANES, qg_ref, qgs_ref if rope else None)
        q_out[0, :, hd * LANES:(hd + 1) * LANES] = rot.astype(BF16)
    for hd in range(GQA_KV_HEADS):
        normed, rot = head(o_k + hd * LANES, qw + hd * LANES, kg_ref, kgs_ref if rope else None)
        k_out[0, :, hd * LANES:(hd + 1) * LANES] = rot.astype(BF16)
        if kt_out is not None:
            seq = kt_out.shape[-1]
            for s in range(kt_out.shape[0]):
                kt_out[s, hd] = normed[s * seq:(s + 1) * seq, :].T[:GQA_HEAD_DIM, :]
    vp_out[0] = _with_sum_lane(z[:, o_vp:o_vp + kw]).astype(BF16)
    if vt_out is not None:
        seq = vt_out.shape[-1]
        for s in range(vt_out.shape[0]):
            vt_out[s] = z[s * seq:(s + 1) * seq, o_v:o_v + LANES].T

    u = z[:, o_u:o_u + GMLP_CH]
    vg = z[:, o_vg:o_vg + GMLP_CH]
    mu = jnp.mean(vg, axis=-1, keepdims=True)
    cen = vg - mu
    var = jnp.mean(cen * cen, axis=-1, keepdims=True)
    vn = (cen * lax.rsqrt(var + EPS) * lng_ref[...] + lnb_ref[...]).astype(BF16)
    bias = bs_ref[...]
    rows = x.shape[0]
    for cidx in range(rows // CHUNK):
        r0 = cidx * CHUNK
        for g in range(GMLP_GROUPS):
            c0 = g * LANES
            mixed = jnp.dot(ws_ref[g], vn[r0:r0 + CHUNK, c0:c0 + LANES], preferred_element_type=F32)
            g_out[0, r0:r0 + CHUNK, c0:c0 + LANES] = (
                u[r0:r0 + CHUNK, c0:c0 + LANES] * (mixed + bias[:, c0:c0 + LANES])).astype(BF16)


def l1_inproj(x, mods, mod_off, n1, w, tables, ctx_seq=None):
    bm, sm, d = x.shape
    tr = min(ROW_TILE, sm)
    rope = tables is not None
    assert rope != (ctx_seq is not None)
    qw = GQA_HEADS * LANES
    kw = GQA_KV_HEADS * LANES
    row = lambda width: pl.BlockSpec((1, tr, width), lambda b, i: (b, i, 0))
    vec = _const_spec((1, LANES))
    in_specs = [row(d), pl.BlockSpec((1, SUBLANES, d), lambda b, i: (b + mod_off, 0, 0)),
                _const_spec((1, d)), _const_spec(w["win"].shape)]
    args = [x, mods, n1, w["win"]]
    if rope:
        in_specs += [_const_spec(w["wsw"].shape), vec, vec, vec, vec]
        args += [w["wsw"], w["qg"], w["qgs"], w["kg"], w["kgs"]]
    else:
        in_specs += [vec, vec]
        args += [w["qg"], w["kg"]]
    in_specs += [_const_spec((1, GMLP_CH)), _const_spec((1, GMLP_CH)),
                 _const_spec(w["ws"].shape), _const_spec((CHUNK, GMLP_CH))]
    args += [w["lng"], w["lnb"], w["ws"], w["bs"]]
    if rope:
        in_specs += [pl.BlockSpec((tr, LANES), lambda b, i: (i, 0))] * 2
        args += list(tables)
    out_shape = [jax.ShapeDtypeStruct((bm, sm, qw), BF16),
                 jax.ShapeDtypeStruct((bm, sm, kw), BF16),
                 jax.ShapeDtypeStruct((bm, sm, kw), BF16),
                 jax.ShapeDtypeStruct((bm, sm, GMLP_CH), BF16)]
    out_specs = [row(qw), row(kw), row(kw), row(GMLP_CH)]
    if not rope:
        assert bm == 1 and tr % ctx_seq == 0
        n_seq, per_step = sm // ctx_seq, tr // ctx_seq
        out_shape += [jax.ShapeDtypeStruct((n_seq, GQA_KV_HEADS, GQA_HEAD_DIM, ctx_seq), F32),
                      jax.ShapeDtypeStruct((n_seq, GQA_KV_HEADS * GQA_HEAD_DIM, ctx_seq), F32)]
        out_specs += [pl.BlockSpec((per_step, GQA_KV_HEADS, GQA_HEAD_DIM, ctx_seq), lambda b, i: (i, 0, 0, 0)),
                      pl.BlockSpec((per_step, GQA_KV_HEADS * GQA_HEAD_DIM, ctx_seq), lambda b, i: (i, 0, 0))]
    return pl.pallas_call(
        functools.partial(_l1_inproj_kernel, rope=rope),
        out_shape=out_shape, grid=(bm, sm // tr), in_specs=in_specs, out_specs=out_specs,
        compiler_params=_params(("parallel", "parallel"), VMEM_LIMIT),
        name="l1_inproj_rope" if rope else "l1_inproj",
    )(*args)


def _post_kernel(a_ref, c_ref, x_ref, m_ref, n2_ref, wa_ref, wc_ref, rwh_ref, rwl_ref, rb_ref, tri_ref, base_ref,
                 x1_out, xm_out, idx_out, wts_out, rank_out, cnt_out, run_ref):
    first =(pl.program_id(0) == 0) & (pl.program_id(1) == 0)

    @pl.when(first)
    def _():
        run_ref[...] = base_ref[...]

    m = m_ref[0]
    y = (jnp.dot(a_ref[0], wa_ref[...], preferred_element_type=F32)
         + jnp.dot(c_ref[0], wc_ref[...], preferred_element_type=F32))
    x1 = x_ref[0] + m[2:3] * y
    x1_out[0] = x1
    xm = _rms(x1, n2_ref[...]) * (1.0 + m[4:5]) + m[3:4]
    xh = xm.astype(BF16)
    xm_out[0] = _pack_rows(xh)

    xl = (xm - xh.astype(F32)).astype(BF16)
    logits = (jnp.dot(xh, rwh_ref[...], preferred_element_type=F32)
              + jnp.dot(xl, rwh_ref[...], preferred_element_type=F32)
              + jnp.dot(xh, rwl_ref[...], preferred_element_type=F32)) + rb_ref[...]
    rows = logits.shape[0]
    lane = lax.broadcasted_iota(jnp.int32, (rows, LANES), 1).astype(F32)
    work = logits
    vals, hots = [], []
    idx_acc = jnp.zeros((rows, LANES), F32)
    for k in range(TOP_K):
        top = jnp.max(work, axis=-1, keepdims=True)
        sel = jnp.min(jnp.where(work == top, lane, float(LANES)), axis=-1, keepdims=True)
        hot = lane == sel
        vals.append(top)
        hots.append(hot)
        idx_acc = idx_acc + jnp.where(lane == float(k), sel, 0.0)
        work = jnp.where(hot, -jnp.inf, work)
    exps = [jnp.exp(v - vals[0]) for v in vals]
    denom = exps[0] + exps[1] + exps[2] + exps[3]
    wcols = [jnp.broadcast_to(exps[k] / denom, (rows, LANES)) for k in range(TOP_K)]
    wcols += [jnp.zeros((rows, LANES), F32)] * (SUBLANES - TOP_K)
    wts = jnp.concatenate(wcols, axis=1).reshape(rows, SUBLANES, LANES)

    chosen = jnp.zeros((rows, LANES), F32)
    for hot in hots:
        chosen = chosen + hot.astype(F32)
    before = jnp.dot(tri_ref[...], chosen.astype(BF16), preferred_element_type=F32) + run_ref[0:1, :]
    rank = jnp.zeros((rows, LANES), F32)
    for k in range(TOP_K):
        rk = jnp.sum(jnp.where(hots[k], before, 0.0), axis=-1, keepdims=True)
        rank = rank + jnp.where(lane == float(k), rk, 0.0)
    run_ref[0:1, :] = run_ref[0:1, :] + jnp.sum(chosen, axis=0, keepdims=True)
    idx_out[0] = idx_acc.astype(jnp.int32)
    wts_out[0] = wts
    rank_out[0] = rank.astype(jnp.int32)
    cnt_out[...] = run_ref[...]


def post_mixer(attn, other, x, mods, mod_off, n2, wa, wc, rwh, rwl, rb, tri, base):
    bm, sm, d = x.shape
    tr = tri.shape[0]
    row = lambda width: pl.BlockSpec((1, tr, width), lambda b, i: (b, i, 0))
    tile_rows = pl.BlockSpec((1, tr) + ROW_WORDS, lambda b, i: (b, i, 0, 0))
    out_shape = [jax.ShapeDtypeStruct((bm, sm, d), F32),
                 jax.ShapeDtypeStruct((bm, sm) + ROW_WORDS, jnp.int32),
                 jax.ShapeDtypeStruct((bm, sm, LANES), jnp.int32),
                 jax.ShapeDtypeStruct((bm, sm, SUBLANES, LANES), F32),
                 jax.ShapeDtypeStruct((bm, sm, LANES), jnp.int32),
                 jax.ShapeDtypeStruct((SUBLANES, LANES), F32)]
    return pl.pallas_call(
        _post_kernel,
        out_shape=out_shape, grid=(bm, sm // tr),
        in_specs=[row(attn.shape[-1]), row(other.shape[-1]), row(d),
                  pl.BlockSpec((1, SUBLANES, d), lambda b, i: (b + mod_off, 0, 0)),
                  _const_spec((1, d)), _const_spec(wa.shape), _const_spec(wc.shape),
                  _const_spec(rwh.shape), _const_spec(rwl.shape), _const_spec((1, LANES)),
                  _const_spec(tri.shape), _const_spec((SUBLANES, LANES))],
        out_specs=[row(d), tile_rows, row(LANES),
                   pl.BlockSpec((1, tr, SUBLANES, LANES), lambda b, i: (b, i, 0, 0)),
                   row(LANES), _const_spec((SUBLANES, LANES))],
        scratch_shapes=[pltpu.VMEM((SUBLANES, LANES), F32)],
        compiler_params=_params(("arbitrary", "arbitrary"), VMEM_LIMIT),
        name="post_mixer_route",
    )(attn, other, x, mods, n2, wa, wc, rwh, rwl, rb, tri, base)


def _plan_kernel(off_ref, idx_ref, rank_ref, pos_out):
    idx = idx_ref[...]
    pos = rank_ref[...]
    for e in range(N_EXPERTS):
        pos = pos + jnp.where(idx == e, off_ref[e], 0)
    pos_out[...] = pos


def plan_positions(offsets, idx, rank):
    n = idx.shape[0]
    tr = min(2048, n)
    spec = pl.BlockSpec((tr, LANES), lambda i, off: (i, 0))
    return pl.pallas_call(
        _plan_kernel,
        out_shape=jax.ShapeDtypeStruct((n, LANES), jnp.int32),
        grid_spec=pltpu.PrefetchScalarGridSpec(
            num_scalar_prefetch=1, grid=(n // tr,), in_specs=[spec, spec], out_specs=spec),
        compiler_params=_params(("parallel",)),
        name="plan_positions",
    )(offsets, idx, rank)


def _sc_worker_id():
    return lax.axis_index("s") * SC_CORES + lax.axis_index("c")


def _sc_chunk_rows(tile, dtype):
    row_bytes = math.prod(tile) * jnp.dtype(dtype).itemsize
    return min(SC_CHUNK_BYTES // row_bytes, SC_MAX_INDICES)


def dispatch_rows(xms, posk, n_rows):
    n_total = sum(x.shape[0] for x in xms)
    tile, dtype = xms[0].shape[1:], xms[0].dtype
    chunk = _sc_chunk_rows(tile, dtype)
    starts, s0 = [], 0
    for x in xms:
        assert x.shape[0] % (SC_WORKERS * chunk) == 0
        starts.append(s0)
        s0 += x.shape[0]
    mesh = plsc.VectorSubcoreMesh(core_axis_name="c", subcore_axis_name="s")

    @functools.partial(
        pl.kernel, mesh=mesh, out_type=jax.ShapeDtypeStruct((n_rows,) + tile, dtype),
        scratch_types=[pltpu.VMEM((chunk,), jnp.int32), pltpu.VMEM((chunk,) + tile, dtype),
                       pltpu.SemaphoreType.DMA],
        name="dispatch_rows_sc")
    def scatter(*refs):
        x_refs, pos_hbm, xs_hbm, idx_v, rows_v, sem = refs[:len(xms)], *refs[len(xms):]
        wid = _sc_worker_id()
        for x_hbm, start in zip(x_refs, starts):
            per_worker = x_hbm.shape[0] // SC_WORKERS

            @pl.loop(0, per_worker // chunk)
            def _(c):
                t0 = wid * per_worker + c * chunk
                pltpu.sync_copy(x_hbm.at[pl.ds(t0, chunk)], rows_v)
                for k in range(TOP_K):
                    pltpu.sync_copy(pos_hbm.at[pl.ds(k * n_total + start + t0, chunk)], idx_v)
                    pltpu.async_copy(rows_v, xs_hbm.at[idx_v], sem).wait()

    return scatter(*xms, posk)


def gather_rows(y, posk):
    n_pairs = posk.shape[0]
    tile, dtype = y.shape[1:], y.dtype
    chunk = _sc_chunk_rows(tile, dtype)
    per_worker = n_pairs // SC_WORKERS
    assert per_worker % chunk == 0
    mesh = plsc.VectorSubcoreMesh(core_axis_name="c", subcore_axis_name="s")

    @functools.partial(
        pl.kernel, mesh=mesh, out_type=jax.ShapeDtypeStruct((n_pairs,) + tile, dtype),
        scratch_types=[pltpu.VMEM((chunk,), jnp.int32), pltpu.VMEM((chunk,) + tile, dtype),
                       pltpu.SemaphoreType.DMA],
        name="gather_rows_sc")
    def gather(y_hbm, pos_hbm, out_hbm, idx_v, rows_v, sem):
        wid = _sc_worker_id()

        @pl.loop(0, per_worker // chunk)
        def _(c):
            base = wid * per_worker + c * chunk
            pltpu.sync_copy(pos_hbm.at[pl.ds(base, chunk)], idx_v)
            pltpu.async_copy(y_hbm.at[idx_v], rows_v, sem).wait()
            pltpu.sync_copy(rows_v, out_hbm.at[pl.ds(base, chunk)])

    return gather(y, posk)


def _ffn_kernel(te_ref, nu_ref, first_ref, slot_ref, nxt_ref, xs_ref, w1_hbm, b1_ref, w2_hbm, b2_ref, y_ref,
                w1f, w2f, w1b, w2b, sem):
    i = pl.program_id(0)

    def weight_copies(e, s):
        return (pltpu.make_async_copy(w1_hbm.at[e], w1f.at[s], sem.at[0, s]),
                pltpu.make_async_copy(w2_hbm.at[e], w2f.at[s], sem.at[1, s]))

    @pl.when(i < nu_ref[0])
    def _():
        s = slot_ref[i]

        @pl.when(first_ref[i] == 1)
        def _():
            @pl.when(i == 0)
            def _():
                for cp in weight_copies(te_ref[i], s):
                    cp.start()
            for cp in weight_copies(te_ref[i], s):
                cp.wait()

            @pl.when(nxt_ref[i] >= 0)
            def _():
                for cp in weight_copies(nxt_ref[i], 1 - s):
                    cp.start()
            for c in range(D_MODEL // LANES):
                w1b[c * LANES:(c + 1) * LANES, :] = w1f[s, c * LANES:(c + 1) * LANES, :].astype(BF16)
            for c in range(D_EXPERT // LANES):
                w2b[c * LANES:(c + 1) * LANES, :] = w2f[s, c * LANES:(c + 1) * LANES, :].astype(BF16)

        x = _unpack_rows(xs_ref[...])
        h = jnp.dot(x, w1b[...], preferred_element_type=F32) + b1_ref[0]
        g = jnp.minimum(h[:, :D_EXPERT], SWIGLU_LIMIT)
        lin = jnp.clip(h[:, D_EXPERT:], -SWIGLU_LIMIT, SWIGLU_LIMIT)
        a = (lin + 1.0) * (g * jax.nn.sigmoid(SWIGLU_ALPHA * g))
        y = jnp.dot(a.astype(BF16), w2b[...], preferred_element_type=F32) + b2_ref[0]
        y_ref[...] = _pack_rows(y.astype(BF16))

    @pl.when(i >= nu_ref[0])
    def _():
        y_ref[...] = jnp.zeros(y_ref.shape, y_ref.dtype)


def grouped_ffn(sched, xs, w1, b1, w2, b2):
    r = xs.shape[0]
    d = w1.shape[1]
    nt = r // FFN_TILE
    tile = (FFN_TILE,) + xs.shape[1:]
    rows = lambda i, te, nu, *_: (jnp.minimum(i, nu[0] - 1), 0, 0)
    bsel = lambda i, te, *_: (te[i], 0, 0)
    return pl.pallas_call(
        _ffn_kernel,
        out_shape=jax.ShapeDtypeStruct(xs.shape, xs.dtype),
        grid_spec=pltpu.PrefetchScalarGridSpec(
            num_scalar_prefetch=5, grid=(nt,),
            in_specs=[pl.BlockSpec(tile, rows),
                      pl.BlockSpec(memory_space=pl.ANY),
                      pl.BlockSpec((1, 1, 2 * D_EXPERT), bsel),
                      pl.BlockSpec(memory_space=pl.ANY),
                      pl.BlockSpec((1, 1, d), bsel)],
            out_specs=pl.BlockSpec(tile, lambda i, *_: (i, 0, 0)),
            scratch_shapes=[pltpu.VMEM((2, d, 2 * D_EXPERT), F32), pltpu.VMEM((2, D_EXPERT, d), F32),
                            pltpu.VMEM((d, 2 * D_EXPERT), BF16), pltpu.VMEM((D_EXPERT, d), BF16),
                            pltpu.SemaphoreType.DMA((2, 2))]),
        compiler_params=_params(("arbitrary",), VMEM_LIMIT),
        name="grouped_ffn",
    )(*sched, xs, w1, b1.reshape(N_EXPERTS, 1, -1), w2, b2.reshape(N_EXPERTS, 1, -1))


def _combine_kernel(x1_ref, wts_ref, m_ref, fn_ref, y0_ref, y1_ref, y2_ref, y3_ref, o_ref, *, final):
    w = wts_ref[0]
    rows = lambda ref: pltpu.bitcast(ref[...], BF16).astype(F32)
    acc = w[:, 0:1, :] * rows(y0_ref)
    for k, y_ref in ((1, y1_ref), (2, y2_ref), (3, y3_ref)):
        acc = acc + w[:, k:k + 1, :] * rows(y_ref)
    out = x1_ref[0] + m_ref[0][5:6] * acc.reshape(x1_ref.shape[1], D_MODEL)
    if final:
        out = _rms(out, fn_ref[...])
    o_ref[0] = out


def combine_rows(x1, wts, mods, mod_off, fn, yg, row_off, n_total, *, final):
    bm, sm, d = x1.shape
    tr = min(COMBINE_TILE, sm)
    nb = sm // tr
    row = lambda width: pl.BlockSpec((1, tr, width), lambda b, i: (b, i, 0))
    ysel = lambda k: pl.BlockSpec((tr,) + yg.shape[1:],
                                  lambda b, i: ((k * n_total + row_off) // tr + b * nb + i, 0, 0))
    return pl.pallas_call(
        functools.partial(_combine_kernel, final=final),
        out_shape=jax.ShapeDtypeStruct((bm, sm, d), F32),
        grid=(bm, nb),
        in_specs=[row(d), pl.BlockSpec((1, tr, SUBLANES, LANES), lambda b, i: (b, i, 0, 0)),
                  pl.BlockSpec((1, SUBLANES, d), lambda b, i: (b + mod_off, 0, 0)),
                  _const_spec((1, d))] + [ysel(k) for k in range(TOP_K)],
        out_specs=row(d),
        compiler_params=_params(("parallel", "parallel"), VMEM_LIMIT),
        name="combine_rows",
    )(x1, wts, mods, fn, yg, yg, yg, yg)


def _axial_angles(n_tokens, rot_dim):
    t = jnp.arange(n_tokens)
    rows = (t // GRID_W).astype(F32)
    cols = (t % GRID_W).astype(F32)
    n_freq = rot_dim // 4
    inv = ROPE_THETA ** (-jnp.arange(n_freq, dtype=F32) / n_freq)
    return jnp.concatenate([rows[:, None] * inv, cols[:, None] * inv], axis=-1)


def _lane_table(parts, n):
    cols = []
    for p in parts:
        cols.append(jnp.broadcast_to(jnp.asarray(p, F32), (n, p.shape[-1])) if hasattr(p, "shape") else p)
    return jnp.concatenate(cols, axis=-1)


def _swap_halves(w):
    half = w.shape[-1] // 2
    return jnp.concatenate([-w[..., half:], w[..., :half]], axis=-1)


def _prep_l0(w_in, q_norm, kv_norm, w_uq, w_uk, w_uv, w_out):
    d = w_in.shape[0]
    o_kr = MLA_Q_RANK + MLA_KV_RANK
    kr_cols = w_in[:, o_kr:o_kr + MLA_ROPE]
    win = jnp.concatenate(
        [w_in[:, :o_kr], kr_cols, _swap_halves(kr_cols), jnp.zeros((d, LANES - 2 * MLA_ROPE), F32),
         w_in[:, o_kr + MLA_ROPE:]], axis=1).astype(BF16)
    qk = MLA_NOPE + MLA_ROPE
    wuq3 = w_uq.reshape(MLA_Q_RANK, MLA_HEADS, qk)
    wuq = jnp.pad(wuq3, ((0, 0), (0, 0), (0, LANES - qk))).reshape(MLA_Q_RANK, -1).astype(BF16)
    wuqs3 = jnp.concatenate(
        [jnp.zeros((MLA_Q_RANK, MLA_HEADS, MLA_NOPE), F32), _swap_halves(wuq3[:, :, MLA_NOPE:]),
         jnp.zeros((MLA_Q_RANK, MLA_HEADS, LANES - qk), F32)], axis=-1)
    wuqs = wuqs3.reshape(MLA_Q_RANK, -1).astype(BF16)
    wuk3 = w_uk.reshape(MLA_KV_RANK, MLA_HEADS, MLA_NOPE)
    wuk = jnp.pad(wuk3, ((0, 0), (0, 0), (0, LANES - MLA_NOPE))).reshape(MLA_KV_RANK, -1).astype(BF16)
    wuv3 = w_uv.reshape(MLA_KV_RANK, MLA_HEADS, MLA_V)
    wuv = jnp.pad(wuv3, ((0, 0), (0, 0), (0, LANES - MLA_V))).reshape(MLA_KV_RANK, -1).astype(BF16)
    eye = jnp.eye(MLA_ROPE, dtype=F32)
    e_head = jnp.concatenate([jnp.zeros((MLA_ROPE, MLA_NOPE), F32), eye,
                              jnp.zeros((MLA_ROPE, LANES - qk), F32)], axis=1)
    e = jnp.pad(jnp.tile(e_head, (1, MLA_HEADS)), ((0, LANES - MLA_ROPE), (0, 0))).astype(BF16)
    wa3 = w_out[:MLA_HEADS * MLA_V].reshape(MLA_HEADS, MLA_V, d)
    wa = jnp.pad(wa3, ((0, 0), (0, LANES - MLA_V), (0, 0))).reshape(MLA_HEADS * LANES, d).astype(BF16)
    wc = w_out[MLA_HEADS * MLA_V:].astype(BF16)
    return dict(win=win, qg=q_norm.reshape(1, -1), kvg=kv_norm.reshape(1, -1), wuq=wuq, wuqs=wuqs,
                wuk=wuk, e=e, wuv=wuv, wa=wa, wc=wc)


def _l0_tables(n):
    ang = _axial_angles(n, MLA_ROPE)
    cos, sin = jnp.cos(ang), jnp.sin(ang)
    one = jnp.ones((n, 1), F32)
    zero = jnp.zeros((n, 1), F32)
    rest = LANES - MLA_NOPE - MLA_ROPE
    cq = jnp.concatenate([jnp.tile(one, (1, MLA_NOPE)), cos, cos, jnp.tile(one, (1, rest))], axis=1)
    sq = jnp.concatenate([jnp.tile(zero, (1, MLA_NOPE)), sin, sin, jnp.tile(zero, (1, rest))], axis=1)
    ck = jnp.concatenate([cos, cos, jnp.tile(zero, (1, LANES - MLA_ROPE))], axis=1)
    sk = jnp.concatenate([sin, sin, jnp.tile(zero, (1, LANES - MLA_ROPE))], axis=1)
    return cq, sq, ck, sk


def _pad_heads(w, n_heads, dim):
    d = w.shape[0]
    return jnp.pad(w.reshape(d, n_heads, dim), ((0, 0), (0, 0), (0, LANES - dim))).reshape(d, n_heads * LANES)


def _prep_l1(w_in, q_norm, k_norm, ln_g, ln_b, w_s, b_s, w_out):
    d = w_in.shape[0]
    qd = GQA_HEADS * GQA_HEAD_DIM
    kd = GQA_KV_HEADS * GQA_HEAD_DIM
    wq, wk, wv = w_in[:, :qd], w_in[:, qd:qd + kd], w_in[:, qd + kd:qd + 2 * kd]
    rest = w_in[:, qd + 2 * kd:]
    win = jnp.concatenate([_pad_heads(wq, GQA_HEADS, GQA_HEAD_DIM), _pad_heads(wk, GQA_KV_HEADS, GQA_HEAD_DIM),
                           _pad_heads(wv, GQA_KV_HEADS, GQA_HEAD_DIM), wv, rest], axis=1).astype(BF16)
    swq = _swap_halves(wq.reshape(d, GQA_HEADS, GQA_HEAD_DIM)).reshape(d, qd)
    swk = _swap_halves(wk.reshape(d, GQA_KV_HEADS, GQA_HEAD_DIM)).reshape(d, kd)
    wsw = jnp.concatenate([_pad_heads(swq, GQA_HEADS, GQA_HEAD_DIM),
                           _pad_heads(swk, GQA_KV_HEADS, GQA_HEAD_DIM)], axis=1).astype(BF16)
    half = GQA_HEAD_DIM // 2
    padg = lambda g: jnp.pad(g, (0, LANES - GQA_HEAD_DIM)).reshape(1, LANES)
    swapg = lambda g: jnp.concatenate([g[half:], g[:half]])
    wa3 = w_out[:qd].reshape(GQA_HEADS, GQA_HEAD_DIM, d)
    wa = jnp.pad(wa3, ((0, 0), (0, LANES - GQA_HEAD_DIM), (0, 0))).reshape(GQA_HEADS * LANES, d).astype(BF16)
    wc = w_out[qd:].astype(BF16)
    bs = jnp.repeat(b_s.T, LANES, axis=1)
    return dict(win=win, wsw=wsw, qg=padg(q_norm), qgs=padg(swapg(q_norm)), kg=padg(k_norm),
                kgs=padg(swapg(k_norm)), lng=ln_g.reshape(1, -1), lnb=ln_b.reshape(1, -1),
                ws=w_s.astype(BF16), bs=bs, wa=wa, wc=wc)


def _l1_tables(n):
    ang = _axial_angles(n, GQA_HEAD_DIM)
    cos, sin = jnp.cos(ang), jnp.sin(ang)
    pad = LANES - GQA_HEAD_DIM
    c = jnp.concatenate([cos, cos, jnp.ones((n, pad), F32)], axis=1)
    s = jnp.concatenate([sin, sin, jnp.zeros((n, pad), F32)], axis=1)
    return c, s


def _pad_lanes(x, width):
    return jnp.pad(x, [(0, 0)] * (x.ndim - 1) + [(0, width - x.shape[-1])])


def routed_ffn(groups, mods, n2, moe, tri, final_norm, *, final):
    router_w, router_b, w1, b1, w2, b2 = moe
    rw = _pad_lanes(router_w, LANES)
    rwh = rw.astype(BF16)
    rwl = (rw - rwh.astype(F32)).astype(BF16)
    rb = jnp.concatenate([router_b, jnp.full((LANES - N_EXPERTS,), NEG_BIG, F32)]).reshape(1, LANES)
    base = jnp.zeros((SUBLANES, LANES), F32)
    routed = []
    for g in groups:
        x1, xm, idx, wts, rank, base = post_mixer(
            g["attn"], g["other"], g["x"], mods, g["mod_off"], n2, g["wa"], g["wc"], rwh, rwl, rb, tri, base)
        routed.append((x1, xm, idx, wts, rank))
    n_total = sum(r[0].shape[0] * r[0].shape[1] for r in routed)
    nt = n_total * TOP_K // FFN_TILE + N_EXPERTS

    counts = base[0, :N_EXPERTS].astype(jnp.int32)
    tiles = (counts + FFN_TILE - 1) // FFN_TILE
    tile_end = jnp.cumsum(tiles)
    offsets = (tile_end - tiles) * FFN_TILE
    n_used = tile_end[-1:].astype(jnp.int32)
    experts = jnp.arange(N_EXPERTS)
    busy = tiles > 0
    slot_e = (jnp.cumsum(busy) - 1) % 2
    later = jnp.where(busy[None, :] & (experts[None, :] > experts[:, None]), experts[None, :], N_EXPERTS)
    nxt_e = jnp.min(later, axis=1)
    nxt_e = jnp.where(nxt_e == N_EXPERTS, -1, nxt_e)
    tile_expert = jnp.sum(jnp.arange(nt)[:, None] >= tile_end[None, :], axis=1)
    tile_expert = jnp.minimum(tile_expert, jnp.max(jnp.where(busy, experts, 0)))
    first = jnp.concatenate([jnp.ones((1,), bool), tile_expert[1:] != tile_expert[:-1]])
    of_tile = tile_expert[:, None] == experts[None, :]
    slot_t = jnp.sum(jnp.where(of_tile, slot_e[None, :], 0), axis=1)
    nxt_t = jnp.sum(jnp.where(of_tile, nxt_e[None, :], 0), axis=1)
    sched = tuple(a.astype(jnp.int32) for a in (tile_expert, n_used, first, slot_t, nxt_t))

    row_tile = ROW_WORDS
    positions, xms = [], []
    for (x1, xm, idx, wts, rank) in routed:
        n = idx.shape[0] * idx.shape[1]
        pos = plan_positions(offsets, idx.reshape(n, LANES), rank.reshape(n, LANES))
        positions.append(pos[:, :TOP_K])
        xms.append(xm.reshape((n,) + row_tile))
    posk = jnp.concatenate(positions, axis=0).T.reshape(-1)
    xs = dispatch_rows(xms, posk, nt * FFN_TILE)
    y = grouped_ffn(sched, xs, w1, b1, w2, b2)
    outs = []
    for g, (x1, xm, idx, wts, rank), pos in zip(groups, routed, positions):
        yg = gather_rows(y, pos.T.reshape(-1))
        outs.append(combine_rows(x1, wts, mods, g["mod_off"], final_norm, yg, 0, pos.shape[0], final=final))
    return outs


def kernel(x_prompt, x_sample, cache_l0_ckv, cache_l0_krope, cache_l1_k, cache_l1_v, c, c_ctx,
           l0_ada_w, l0_ada_b, l0_norm1, l0_w_in, l0_q_norm, l0_kv_norm, l0_w_uq, l0_w_uk, l0_w_uv,
           l0_conv_w, l0_conv_b, l0_conv_ln_g, l0_conv_ln_b, l0_w_out, l0_norm2,
           l0_router_w, l0_router_b, l0_w1, l0_b1, l0_w2, l0_b2,
           l1_ada_w, l1_ada_b, l1_norm1, l1_w_in, l1_q_norm, l1_k_norm, l1_gmlp_ln_g, l1_gmlp_ln_b,
           l1_w_s, l1_b_s, l1_w_out, l1_norm2,
           l1_router_w, l1_router_b, l1_w1, l1_b1, l1_w2, l1_b2,
           final_norm):
    bp, sp, d = x_prompt.shape
    bs, ss, _ = x_sample.shape
    past = cache_l0_ckv.shape[1]
    n_p = bp * sp

    cond8 = jnp.concatenate([c_ctx[None], c, jnp.zeros((SUBLANES - 1 - bs, d), F32)], axis=0)
    mods0 = adaln(cond8, l0_ada_w, l0_ada_b)
    mods1 = adaln(cond8, l1_ada_w, l1_ada_b)
    tri = jnp.tril(jnp.ones((ROW_TILE, ROW_TILE), F32), -1).astype(BF16)
    fn = final_norm.reshape(1, d)

    w0 = _prep_l0(l0_w_in, l0_q_norm, l0_kv_norm, l0_w_uq, l0_w_uk, l0_w_uv, l0_w_out)
    n1 = l0_norm1.reshape(1, d)
    hp = x_prompt.reshape(1, n_p, d)
    q_p, k_p, v_p, ckv_p, kr_p, u_p = l0_inproj(hp, mods0, 0, n1, w0, None)
    q_s, k_s, v_s, _, _, u_s = l0_inproj(x_sample, mods0, 1, n1, w0, _l0_tables(ss))
    k_c, v_c = mla_ctx_kv(cache_l0_ckv, _pad_lanes(cache_l0_krope, LANES), w0)
    hw = MLA_HEADS * LANES
    att_p = attention(q_p.reshape(bp, sp, hw), k_p.reshape(bp, sp, hw), v_p.reshape(bp, sp, hw),
                      n_heads=MLA_HEADS, n_kv=MLA_HEADS, scale=MLA_SCALE, heads_per_step=MLA_HEADS)
    att_s = attention(q_s, jnp.concatenate([k_c, k_s], axis=1), jnp.concatenate([v_c, v_s], axis=1),
                      n_heads=MLA_HEADS, n_kv=MLA_HEADS, scale=MLA_SCALE, heads_per_step=LATENT_HEADS_PER_STEP)
    conv_p = conformer_conv(u_p.reshape(bp, sp, CONV_CH), l0_conv_w, l0_conv_b, l0_conv_ln_g, l0_conv_ln_b)
    conv_s = conformer_conv(u_s, l0_conv_w, l0_conv_b, l0_conv_ln_g, l0_conv_ln_b)
    groups = [dict(attn=att_p.reshape(1, n_p, hw), other=conv_p.reshape(1, n_p, CONV_CH), x=hp, mod_off=0,
                   wa=w0["wa"], wc=w0["wc"]),
              dict(attn=att_s, other=conv_s, x=x_sample, mod_off=1, wa=w0["wa"], wc=w0["wc"])]
    hp, hs = routed_ffn(groups, mods0, l0_norm2.reshape(1, d),
                        (l0_router_w, l0_router_b, l0_w1, l0_b1, l0_w2, l0_b2), tri, fn, final=False)
    new_l0_ckv = ckv_p.reshape(bp, sp, MLA_KV_RANK)
    new_l0_krope = kr_p.reshape(bp, sp, MLA_ROPE)

    w1p = _prep_l1(l1_w_in, l1_q_norm, l1_k_norm, l1_gmlp_ln_g, l1_gmlp_ln_b, l1_w_s, l1_b_s, l1_w_out)
    n1 = l1_norm1.reshape(1, d)
    q_p, k_p, vp_p, gat_p, kt_p, vt_p = l1_inproj(hp, mods1, 0, n1, w1p, None, ctx_seq=sp)
    q_s, k_s, vp_s, gat_s = l1_inproj(hs, mods1, 1, n1, w1p, _l1_tables(ss))
    qw = GQA_HEADS * LANES
    kw = GQA_KV_HEADS * LANES
    pad_kv = lambda t: _pad_lanes(t, LANES).reshape(bs, past, kw).astype(BF16)
    att_p = attention(q_p.reshape(bp, sp, qw), k_p.reshape(bp, sp, kw), vp_p.reshape(bp, sp, kw),
                      n_heads=GQA_HEADS, n_kv=GQA_KV_HEADS, scale=GQA_SCALE, heads_per_step=GQA_HEADS)
    att_s = attention(q_s, jnp.concatenate([pad_kv(cache_l1_k), k_s], axis=1),
                      jnp.concatenate([_with_sum_lane(pad_kv(cache_l1_v)), vp_s], axis=1),
                      n_heads=GQA_HEADS, n_kv=GQA_KV_HEADS, scale=GQA_SCALE, heads_per_step=LATENT_HEADS_PER_STEP)
    groups = [dict(attn=att_p.reshape(1, n_p, qw), other=gat_p, x=hp, mod_off=0, wa=w1p["wa"], wc=w1p["wc"]),
              dict(attn=att_s, other=gat_s, x=hs, mod_off=1, wa=w1p["wa"], wc=w1p["wc"])]
    yp, ys = routed_ffn(groups, mods1, l1_norm2.reshape(1, d),
                        (l1_router_w, l1_router_b, l1_w1, l1_b1, l1_w2, l1_b2), tri, fn, final=True)
    new_l1_k = jnp.transpose(kt_p, (0, 3, 1, 2))
    new_l1_v = jnp.transpose(vt_p.reshape(bp, GQA_KV_HEADS, GQA_HEAD_DIM, sp), (0, 3, 1, 2))
    return (yp.reshape(bp, sp, d), ys, new_l0_ckv, new_l0_krope, new_l1_k, new_l1_v)
```

```python
import functools
import math

import jax
import jax.numpy as jnp
from jax import lax
from jax.experimental import pallas as pl
from jax.experimental.pallas import tpu as pltpu
from jax.experimental.pallas import tpu_sc as plsc

F32 = jnp.float32
BF16 = jnp.bfloat16
HIGHEST = lax.Precision.HIGHEST

LANES = 128
SUBLANES = 8
VMEM_LIMIT = 56 * 1024 * 1024

D_MODEL = 1024
GRID_W = 64
ROPE_THETA = 10000.0
EPS = 1e-6
N_MOD = 6

MLA_HEADS = 8
MLA_NOPE = 64
MLA_ROPE = 32
MLA_V = 64
MLA_Q_RANK = 384
MLA_KV_RANK = 256
MLA_SCALE = 1.0 / math.sqrt(MLA_NOPE + MLA_ROPE)
CONV_CH = 512
CONV_WIDTH = 31
CONV_HALO = 16

GQA_HEADS = 8
GQA_KV_HEADS = 2
GQA_HEAD_DIM = 64
GQA_SCALE = 1.0 / math.sqrt(GQA_HEAD_DIM)
CHUNK = 128
GMLP_GROUPS = 4
GMLP_CH = 512

N_EXPERTS = 32
TOP_K = 4
D_EXPERT = 1024
SWIGLU_LIMIT = 7.0
SWIGLU_ALPHA = 1.702

ROW_TILE = 512
FFN_TILE = 256
SC_CORES = 2
SC_SUBCORES = 16
SC_WORKERS = SC_CORES * SC_SUBCORES
SC_CHUNK_BYTES = 256 * 1024
SC_MAX_INDICES = 128
COMBINE_TILE = 256
ATT_Q_TILE = 256
ATT_LONG_KEYS = 1024
V_SUM_LANE = 64
LATENT_HEADS_PER_STEP = 4
NEG_BIG = -1e30


def _params(sem, vmem=None):
    return pltpu.CompilerParams(dimension_semantics=sem, vmem_limit_bytes=vmem)


def _rms(x, g):
    return x * lax.rsqrt(jnp.mean(x * x, axis=-1, keepdims=True) + EPS) * g


ROW_WORDS = (SUBLANES // 2, LANES)


def _pack_rows(x_bf16):
    return pltpu.bitcast(x_bf16.reshape(x_bf16.shape[0], SUBLANES, LANES), jnp.int32)


def _unpack_rows(words):
    return pltpu.bitcast(words, BF16).reshape(words.shape[0], D_MODEL)


def _const_spec(shape):
    nd = len(shape)
    return pl.BlockSpec(shape, lambda *_: (0,) * nd)


def _adaln_kernel(c_ref, w_ref, b_ref, o_ref):
    c = c_ref[...]
    s = c * jax.nn.sigmoid(c)
    o_ref[...] = jnp.dot(s, w_ref[...], preferred_element_type=F32, precision=HIGHEST) + b_ref[...]


def adaln(cond8, ada_w, ada_b):
    d, n = ada_w.shape
    bn = n // 4
    m = pl.pallas_call(
        _adaln_kernel,
        out_shape=jax.ShapeDtypeStruct((SUBLANES, n), F32),
        grid=(n // bn,),
        in_specs=[_const_spec((SUBLANES, d)),
                  pl.BlockSpec((d, bn), lambda j: (0, j)),
                  pl.BlockSpec((1, bn), lambda j: (0, j))],
        out_specs=pl.BlockSpec((SUBLANES, bn), lambda j: (0, j)),
        compiler_params=_params(("arbitrary",), VMEM_LIMIT),
        name="adaln",
    )(cond8, ada_w, ada_b.reshape(1, n))
    m = m.reshape(SUBLANES, N_MOD, d)
    return jnp.pad(m, ((0, 0), (0, SUBLANES - N_MOD), (0, 0)))


def _l0_inproj_kernel(*refs, rope):
    if rope:
        (x_ref, m_ref, n1_ref, win_ref, qg_ref, kvg_ref, wuq_ref, wuqs_ref, wuk_ref, e_ref, wuv_ref,
         cq_ref, sq_ref, ck_ref, sk_ref, q_out, k_out, v_out, ckv_out, kr_out, u_out) = refs
    else:
        (x_ref, m_ref, n1_ref, win_ref, qg_ref, kvg_ref, wuq_ref, wuk_ref, e_ref, wuv_ref,
         q_out, k_out, v_out, ckv_out, kr_out, u_out) = refs
    x = x_ref[0]
    m = m_ref[0]
    h = _rms(x, n1_ref[...]) * (1.0 + m[1:2]) + m[0:1]
    z = jnp.dot(h.astype(BF16), win_ref[...], preferred_element_type=F32)
    c_q = z[:, 0:MLA_Q_RANK]
    c_kv = z[:, MLA_Q_RANK:MLA_Q_RANK + MLA_KV_RANK]
    kr_blk = z[:, 640:768]
    val = z[:, 768:768 + CONV_CH]
    gate = z[:, 768 + CONV_CH:768 + 2 * CONV_CH]

    cqn = _rms(c_q, qg_ref[...]).astype(BF16)
    q = jnp.dot(cqn, wuq_ref[...], preferred_element_type=F32)
    if rope:
        qs = jnp.dot(cqn, wuqs_ref[...], preferred_element_type=F32)
        cq = cq_ref[...]
        sq = sq_ref[...]
        for hd in range(MLA_HEADS):
            sl = slice(hd * LANES, (hd + 1) * LANES)
            q_out[0, :, sl] = (q[:, sl] * cq + qs[:, sl] * sq).astype(BF16)
        kr = kr_blk * ck_ref[...] + pltpu.roll(kr_blk, LANES - MLA_ROPE, 1) * sk_ref[...]
    else:
        q_out[0] = q.astype(BF16)
        kr = kr_blk

    ckv = _rms(c_kv, kvg_ref[...])
    ckv_out[0] = ckv
    kr_out[0] = kr_blk[:, 0:MLA_ROPE]
    ckv_b = ckv.astype(BF16)
    k = (jnp.dot(ckv_b, wuk_ref[...], preferred_element_type=F32)
         + jnp.dot(kr.astype(BF16), e_ref[...], preferred_element_type=F32))
    k_out[0] = k.astype(BF16)
    v_out[0] = _with_sum_lane(jnp.dot(ckv_b, wuv_ref[...], preferred_element_type=F32)).astype(BF16)
    u_out[0] = val * jax.nn.sigmoid(gate)


def l0_inproj(x, mods, mod_off, n1, w, tables):
    bm, sm, d = x.shape
    tr = min(ROW_TILE, sm)
    rope = tables is not None
    hp = MLA_HEADS * LANES
    row = lambda width: pl.BlockSpec((1, tr, width), lambda b, i: (b, i, 0))
    in_specs = [row(d),
                pl.BlockSpec((1, SUBLANES, d), lambda b, i: (b + mod_off, 0, 0)),
                _const_spec((1, d)), _const_spec(w["win"].shape),
                _const_spec((1, MLA_Q_RANK)), _const_spec((1, MLA_KV_RANK)),
                _const_spec(w["wuq"].shape)]
    args = [x, mods, n1, w["win"], w["qg"], w["kvg"], w["wuq"]]
    if rope:
        in_specs.append(_const_spec(w["wuqs"].shape))
        args.append(w["wuqs"])
    in_specs += [_const_spec(w["wuk"].shape), _const_spec(w["e"].shape), _const_spec(w["wuv"].shape)]
    args += [w["wuk"], w["e"], w["wuv"]]
    if rope:
        in_specs += [pl.BlockSpec((tr, LANES), lambda b, i: (i, 0))] * 4
        args += list(tables)
    out_shape = [jax.ShapeDtypeStruct((bm, sm, hp), BF16),
                 jax.ShapeDtypeStruct((bm, sm, hp), BF16),
                 jax.ShapeDtypeStruct((bm, sm, hp), BF16),
                 jax.ShapeDtypeStruct((bm, sm, MLA_KV_RANK), F32),
                 jax.ShapeDtypeStruct((bm, sm, MLA_ROPE), F32),
                 jax.ShapeDtypeStruct((bm, sm, CONV_CH), F32)]
    out_specs = [row(hp), row(hp), row(hp), row(MLA_KV_RANK), row(MLA_ROPE), row(CONV_CH)]
    return pl.pallas_call(
        functools.partial(_l0_inproj_kernel, rope=rope),
        out_shape=out_shape, grid=(bm, sm // tr), in_specs=in_specs, out_specs=out_specs,
        compiler_params=_params(("parallel", "parallel"), VMEM_LIMIT),
        name="l0_inproj_rope" if rope else "l0_inproj",
    )(*args)


def _mla_ctx_kv_kernel(ckv_ref, kr_ref, wuk_ref, e_ref, wuv_ref, k_out, v_out):
    ckv_b = ckv_ref[0].astype(BF16)
    k = (jnp.dot(ckv_b, wuk_ref[...], preferred_element_type=F32)
         + jnp.dot(kr_ref[0].astype(BF16), e_ref[...], preferred_element_type=F32))
    k_out[0] = k.astype(BF16)
    v_out[0] = _with_sum_lane(jnp.dot(ckv_b, wuv_ref[...], preferred_element_type=F32)).astype(BF16)


def mla_ctx_kv(ckv, kr128, w):
    b, s, _ = ckv.shape
    hp = MLA_HEADS * LANES
    blk = lambda width: pl.BlockSpec((1, s, width), lambda i: (i, 0, 0))
    return pl.pallas_call(
        _mla_ctx_kv_kernel,
        out_shape=[jax.ShapeDtypeStruct((b, s, hp), BF16)] * 2,
        grid=(b,),
        in_specs=[blk(MLA_KV_RANK), blk(LANES), _const_spec(w["wuk"].shape),
                  _const_spec(w["e"].shape), _const_spec(w["wuv"].shape)],
        out_specs=[blk(hp), blk(hp)],
        compiler_params=_params(("parallel",), VMEM_LIMIT),
        name="mla_ctx_kv",
    )(ckv, kr128, w["wuk"], w["e"], w["wuv"])


def _conv_kernel(prev_ref, cur_ref, next_ref, w_ref, b_ref, g_ref, beta_ref, o_ref, pad_ref, sh_ref, *, rb):
    i = pl.program_id(1)
    last = pl.num_programs(1) - 1
    zeros = jnp.zeros((CONV_HALO, CONV_CH), F32)
    pad_ref[0:CONV_HALO, :] = jnp.where(i == 0, zeros, prev_ref[0])
    pad_ref[CONV_HALO:CONV_HALO + rb, :] = cur_ref[0]
    pad_ref[CONV_HALO + rb:CONV_HALO + rb + CONV_HALO, :] = jnp.where(i == last, zeros, next_ref[0])
    span = rb + 2 * CONV_HALO - SUBLANES
    for r in range(1, SUBLANES):
        sh_ref[r] = pad_ref[r:r + span, :]
    w = w_ref[...]
    shift = CONV_HALO - CONV_WIDTH // 2
    acc = jnp.zeros((rb, CONV_CH), F32) + b_ref[...]
    for k in range(CONV_WIDTH):
        off = k + shift
        r, a = off % SUBLANES, off // SUBLANES * SUBLANES
        window = pad_ref[a:a + rb, :] if r == 0 else sh_ref[r, a:a + rb, :]
        acc = acc + window * w[k:k + 1, :]
    mu = jnp.mean(acc, axis=-1, keepdims=True)
    cen = acc - mu
    var = jnp.mean(cen * cen, axis=-1, keepdims=True)
    y = cen * lax.rsqrt(var + EPS) * g_ref[...] + beta_ref[...]
    o_ref[0] = (y * jax.nn.sigmoid(y)).astype(BF16)


def conformer_conv(u, conv_w, conv_b, ln_g, ln_b):
    b, s, c = u.shape
    rb = min(256, s)
    nh = rb // CONV_HALO
    n_halo_blocks = s // CONV_HALO
    wpad = jnp.pad(conv_w.reshape(CONV_WIDTH, c), ((0, 32 - CONV_WIDTH), (0, 0)))
    return pl.pallas_call(
        functools.partial(_conv_kernel, rb=rb),
        out_shape=jax.ShapeDtypeStruct((b, s, c), BF16),
        grid=(b, s // rb),
        in_specs=[pl.BlockSpec((1, CONV_HALO, c), lambda bi, i: (bi, jnp.maximum(i * nh - 1, 0), 0)),
                  pl.BlockSpec((1, rb, c), lambda bi, i: (bi, i, 0)),
                  pl.BlockSpec((1, CONV_HALO, c),
                               lambda bi, i: (bi, jnp.minimum((i + 1) * nh, n_halo_blocks - 1), 0)),
                  _const_spec((32, c)), _const_spec((1, c)), _const_spec((1, c)), _const_spec((1, c))],
        out_specs=pl.BlockSpec((1, rb, c), lambda bi, i: (bi, i, 0)),
        scratch_shapes=[pltpu.VMEM((rb + 2 * CONV_HALO, c), F32),
                        pltpu.VMEM((SUBLANES, rb + 2 * CONV_HALO - SUBLANES, c), F32)],
        compiler_params=_params(("parallel", "parallel"), VMEM_LIMIT),
        name="conformer_conv",
    )(u, u, u, wpad, conv_b.reshape(1, c), ln_g.reshape(1, c), ln_b.reshape(1, c))


def _with_sum_lane(v):
    lane = lax.broadcasted_iota(jnp.int32, (1, v.shape[-1]), 1)
    return v + ((lane & (LANES - 1)) == V_SUM_LANE).astype(v.dtype)


def _attn_kernel(q_ref, k_ref, v_ref, o_ref, *, heads, rep, scale, mxu_denominator):
    for hd in range(heads):
        g = hd // rep
        q = q_ref[0, :, hd * LANES:(hd + 1) * LANES]
        k = k_ref[0, :, g * LANES:(g + 1) * LANES]
        s = lax.dot_general(q, k, (((1,), (1,)), ((), ())), preferred_element_type=F32) * scale
        m = jnp.max(s, axis=-1, keepdims=True)
        if mxu_denominator:
            p = jnp.exp((s - m).astype(BF16))
            o = jnp.dot(p, v_ref[0, :, g * LANES:(g + 1) * LANES], preferred_element_type=F32)
            l = o[:, V_SUM_LANE:V_SUM_LANE + 1]
        else:
            p = jnp.exp(s - m)
            l = jnp.sum(p, axis=-1, keepdims=True)
            o = jnp.dot(p.astype(BF16), v_ref[0, :, g * LANES:(g + 1) * LANES], preferred_element_type=F32)
        o_ref[0, :, hd * LANES:(hd + 1) * LANES] = (o / l).astype(BF16)


def attention(q, k, v, *, n_heads, n_kv, scale, heads_per_step):
    b, sq, _ = q.shape
    sk = k.shape[1]
    rep = n_heads // n_kv
    tq = min(ATT_Q_TILE, sq)
    hb = heads_per_step
    grid = (b, n_heads // hb, sq // tq)
    if hb >= rep:
        kv_spec = pl.BlockSpec((1, sk, hb // rep * LANES), lambda bi, h, i: (bi, 0, h))
        kern_rep = rep
    else:
        assert rep % hb == 0
        kv_spec = pl.BlockSpec((1, sk, LANES), lambda bi, h, i: (bi, 0, h * hb // rep))
        kern_rep = hb
    kern = functools.partial(_attn_kernel, heads=hb, rep=kern_rep, scale=scale,
                             mxu_denominator=sk >= ATT_LONG_KEYS)
    q_spec = pl.BlockSpec((1, tq, heads_per_step * LANES), lambda bi, h, i: (bi, i, h))
    return pl.pallas_call(
        kern,
        out_shape=jax.ShapeDtypeStruct(q.shape, BF16),
        grid=grid, in_specs=[q_spec, kv_spec, kv_spec], out_specs=q_spec,
        compiler_params=_params(("parallel", "parallel", "parallel"), VMEM_LIMIT),
        name="attention",
    )(q, k, v)


def _l1_inproj_kernel(*refs, rope):
    if rope:
        (x_ref, m_ref, n1_ref, win_ref, wsw_ref, qg_ref, qgs_ref, kg_ref, kgs_ref, lng_ref, lnb_ref,
         ws_ref, bs_ref, c_ref, s_ref, q_out, k_out, vp_out, g_out) = refs
        kt_out = vt_out = None
    else:
        (x_ref, m_ref, n1_ref, win_ref, qg_ref, kg_ref, lng_ref, lnb_ref,
         ws_ref, bs_ref, q_out, k_out, vp_out, g_out, kt_out, vt_out) = refs
    x = x_ref[0]
    m = m_ref[0]
    hb = (_rms(x, n1_ref[...]) * (1.0 + m[1:2]) + m[0:1]).astype(BF16)
    z = jnp.dot(hb, win_ref[...], preferred_element_type=F32)
    qw = GQA_HEADS * LANES
    kw = GQA_KV_HEADS * LANES
    o_k, o_vp, o_v, o_u, o_vg = qw, qw + kw, qw + 2 * kw, qw + 2 * kw + LANES, qw + 2 * kw + LANES + GMLP_CH
    if rope:
        zs = jnp.dot(hb, wsw_ref[...], preferred_element_type=F32)
        cos = c_ref[...]
        sin = s_ref[...]

    def head(col, zcol, g_ref, gs_ref):
        t = z[:, col:col + LANES]
        r = lax.rsqrt(jnp.sum(t * t, axis=-1, keepdims=True) * (1.0 / GQA_HEAD_DIM) + EPS)
        normed = t * r * g_ref[...]
        if not rope:
            return normed, normed
        ts = zs[:, zcol:zcol + LANES]
        return normed, normed * cos + ts * r * gs_ref[...] * sin

    for hd in range(GQA_HEADS):
        _, rot = head(hd * LANES, hd * LANES, qg_ref, qgs_ref if rope else None)
        q_out[0, :, hd * LANES:(hd + 1) * LANES] = rot.astype(BF16)
    for hd in range(GQA_KV_HEADS):
        normed, rot = head(o_k + hd * LANES, qw + hd * LANES, kg_ref, kgs_ref if rope else None)
        k_out[0, :, hd * LANES:(hd + 1) * LANES] = rot.astype(BF16)
        if kt_out is not None:
            seq = kt_out.shape[-1]
            for s in range(kt_out.shape[0]):
                kt_out[s, hd] = normed[s * seq:(s + 1) * seq, :].T[:GQA_HEAD_DIM, :]
    vp_out[0] = _with_sum_lane(z[:, o_vp:o_vp + kw]).astype(BF16)
    if vt_out is not None:
        seq = vt_out.shape[-1]
        for s in range(vt_out.shape[0]):
            vt_out[s] = z[s * seq:(s + 1) * seq, o_v:o_v + LANES].T

    u = z[:, o_u:o_u + GMLP_CH]
    vg = z[:, o_vg:o_vg + GMLP_CH]
    mu = jnp.mean(vg, axis=-1, keepdims=True)
    cen = vg - mu
    var = jnp.mean(cen * cen, axis=-1, keepdims=True)
    vn = (cen * lax.rsqrt(var + EPS) * lng_ref[...] + lnb_ref[...]).astype(BF16)
    bias = bs_ref[...]
    rows = x.shape[0]
    for cidx in range(rows // CHUNK):
        r0 = cidx * CHUNK
        for g in range(GMLP_GROUPS):
            c0 = g * LANES
            mixed = jnp.dot(ws_ref[g], vn[r0:r0 + CHUNK, c0:c0 + LANES], preferred_element_type=F32)
            g_out[0, r0:r0 + CHUNK, c0:c0 + LANES] = (
                u[r0:r0 + CHUNK, c0:c0 + LANES] * (mixed + bias[:, c0:c0 + LANES])).astype(BF16)


def l1_inproj(x, mods, mod_off, n1, w, tables, ctx_seq=None):
    bm, sm, d = x.shape
    tr = min(ROW_TILE, sm)
    rope = tables is not None
    assert rope != (ctx_seq is not None)
    qw = GQA_HEADS * LANES
    kw = GQA_KV_HEADS * LANES
    row = lambda width: pl.BlockSpec((1, tr, width), lambda b, i: (b, i, 0))
    vec = _const_spec((1, LANES))
    in_specs = [row(d), pl.BlockSpec((1, SUBLANES, d), lambda b, i: (b + mod_off, 0, 0)),
                _const_spec((1, d)), _const_spec(w["win"].shape)]
    args = [x, mods, n1, w["win"]]
    if rope:
        in_specs += [_const_spec(w["wsw"].shape), vec, vec, vec, vec]
        args += [w["wsw"], w["qg"], w["qgs"], w["kg"], w["kgs"]]
    else:
        in_specs += [vec, vec]
        args += [w["qg"], w["kg"]]
    in_specs += [_const_spec((1, GMLP_CH)), _const_spec((1, GMLP_CH)),
                 _const_spec(w["ws"].shape), _const_spec((CHUNK, GMLP_CH))]
    args += [w["lng"], w["lnb"], w["ws"], w["bs"]]
    if rope:
        in_specs += [pl.BlockSpec((tr, LANES), lambda b, i: (i, 0))] * 2
        args += list(tables)
    out_shape = [jax.ShapeDtypeStruct((bm, sm, qw), BF16),
                 jax.ShapeDtypeStruct((bm, sm, kw), BF16),
                 jax.ShapeDtypeStruct((bm, sm, kw), BF16),
                 jax.ShapeDtypeStruct((bm, sm, GMLP_CH), BF16)]
    out_specs = [row(qw), row(kw), row(kw), row(GMLP_CH)]
    if not rope:
        assert bm == 1 and tr % ctx_seq == 0
        n_seq, per_step = sm // ctx_seq, tr // ctx_seq
        out_shape += [jax.ShapeDtypeStruct((n_seq, GQA_KV_HEADS, GQA_HEAD_DIM, ctx_seq), F32),
                      jax.ShapeDtypeStruct((n_seq, GQA_KV_HEADS * GQA_HEAD_DIM, ctx_seq), F32)]
        out_specs += [pl.BlockSpec((per_step, GQA_KV_HEADS, GQA_HEAD_DIM, ctx_seq), lambda b, i: (i, 0, 0, 0)),
                      pl.BlockSpec((per_step, GQA_KV_HEADS * GQA_HEAD_DIM, ctx_seq), lambda b, i: (i, 0, 0))]
    return pl.pallas_call(
        functools.partial(_l1_inproj_kernel, rope=rope),
        out_shape=out_shape, grid=(bm, sm // tr), in_specs=in_specs, out_specs=out_specs,
        compiler_params=_params(("parallel", "parallel"), VMEM_LIMIT),
        name="l1_inproj_rope" if rope else "l1_inproj",
    )(*args)


def _post_kernel(a_ref, c_ref, x_ref, m_ref, n2_ref, wa_ref, wc_ref, rwh_ref, rwl_ref, rb_ref, tri_ref, base_ref,
                 x1_out, xm_out, idx_out, wts_out, rank_out, cnt_out, run_ref):
    first =(pl.program_id(0) == 0) & (pl.program_id(1) == 0)

    @pl.when(first)
    def _():
        run_ref[...] = base_ref[...]

    m = m_ref[0]
    y = (jnp.dot(a_ref[0], wa_ref[...], preferred_element_type=F32)
         + jnp.dot(c_ref[0], wc_ref[...], preferred_element_type=F32))
    x1 = x_ref[0] + m[2:3] * y
    x1_out[0] = x1
    xm = _rms(x1, n2_ref[...]) * (1.0 + m[4:5]) + m[3:4]
    xh = xm.astype(BF16)
    xm_out[0] = _pack_rows(xh)

    xl = (xm - xh.astype(F32)).astype(BF16)
    logits = (jnp.dot(xh, rwh_ref[...], preferred_element_type=F32)
              + jnp.dot(xl, rwh_ref[...], preferred_element_type=F32)
              + jnp.dot(xh, rwl_ref[...], preferred_element_type=F32)) + rb_ref[...]
    rows = logits.shape[0]
    lane = lax.broadcasted_iota(jnp.int32, (rows, LANES), 1).astype(F32)
    work = logits
    vals, hots = [], []
    idx_acc = jnp.zeros((rows, LANES), F32)
    for k in range(TOP_K):
        top = jnp.max(work, axis=-1, keepdims=True)
        sel = jnp.min(jnp.where(work == top, lane, float(LANES)), axis=-1, keepdims=True)
        hot = lane == sel
        vals.append(top)
        hots.append(hot)
        idx_acc = idx_acc + jnp.where(lane == float(k), sel, 0.0)
        work = jnp.where(hot, -jnp.inf, work)
    exps = [jnp.exp(v - vals[0]) for v in vals]
    denom = exps[0] + exps[1] + exps[2] + exps[3]
    wcols = [jnp.broadcast_to(exps[k] / denom, (rows, LANES)) for k in range(TOP_K)]
    wcols += [jnp.zeros((rows, LANES), F32)] * (SUBLANES - TOP_K)
    wts = jnp.concatenate(wcols, axis=1).reshape(rows, SUBLANES, LANES)

    chosen = jnp.zeros((rows, LANES), F32)
    for hot in hots:
        chosen = chosen + hot.astype(F32)
    before = jnp.dot(tri_ref[...], chosen.astype(BF16), preferred_element_type=F32) + run_ref[0:1, :]
    rank = jnp.zeros((rows, LANES), F32)
    for k in range(TOP_K):
        rk = jnp.sum(jnp.where(hots[k], before, 0.0), axis=-1, keepdims=True)
        rank = rank + jnp.where(lane == float(k), rk, 0.0)
    run_ref[0:1, :] = run_ref[0:1, :] + jnp.sum(chosen, axis=0, keepdims=True)
    idx_out[...] = idx_acc.T[:SUBLANES, :].astype(jnp.int32)
    wts_out[0] = wts
    rank_out[...] = rank.T[:SUBLANES, :].astype(jnp.int32)
    cnt_out[...] = run_ref[...]


def post_mixer(attn, other, x, mods, mod_off, n2, wa, wc, rwh, rwl, rb, tri, base):
    bm, sm, d = x.shape
    tr = tri.shape[0]
    row = lambda width: pl.BlockSpec((1, tr, width), lambda b, i: (b, i, 0))
    tile_rows = pl.BlockSpec((1, tr) + ROW_WORDS, lambda b, i: (b, i, 0, 0))
    nb = sm // tr
    token_minor = pl.BlockSpec((SUBLANES, tr), lambda b, i: (0, b * nb + i))
    out_shape = [jax.ShapeDtypeStruct((bm, sm, d), F32),
                 jax.ShapeDtypeStruct((bm, sm) + ROW_WORDS, jnp.int32),
                 jax.ShapeDtypeStruct((SUBLANES, bm * sm), jnp.int32),
                 jax.ShapeDtypeStruct((bm, sm, SUBLANES, LANES), F32),
                 jax.ShapeDtypeStruct((SUBLANES, bm * sm), jnp.int32),
                 jax.ShapeDtypeStruct((SUBLANES, LANES), F32)]
    return pl.pallas_call(
        _post_kernel,
        out_shape=out_shape, grid=(bm, sm // tr),
        in_specs=[row(attn.shape[-1]), row(other.shape[-1]), row(d),
                  pl.BlockSpec((1, SUBLANES, d), lambda b, i: (b + mod_off, 0, 0)),
                  _const_spec((1, d)), _const_spec(wa.shape), _const_spec(wc.shape),
                  _const_spec(rwh.shape), _const_spec(rwl.shape), _const_spec((1, LANES)),
                  _const_spec(tri.shape), _const_spec((SUBLANES, LANES))],
        out_specs=[row(d), tile_rows, token_minor,
                   pl.BlockSpec((1, tr, SUBLANES, LANES), lambda b, i: (b, i, 0, 0)),
                   token_minor, _const_spec((SUBLANES, LANES))],
        scratch_shapes=[pltpu.VMEM((SUBLANES, LANES), F32)],
        compiler_params=_params(("arbitrary", "arbitrary"), VMEM_LIMIT),
        name="post_mixer_route",
    )(attn, other, x, mods, n2, wa, wc, rwh, rwl, rb, tri, base)


def _plan_kernel(off_ref, idx_ref, rank_ref, pos_out):
    idx = idx_ref[...]
    pos = rank_ref[...]
    for e in range(N_EXPERTS):
        pos = pos + jnp.where(idx == e, off_ref[e], 0)
    pos_out[...] = pos


def plan_positions(offsets, idx, rank):
    n = idx.shape[1]
    tr = min(2048, n)
    spec = pl.BlockSpec((SUBLANES, tr), lambda i, off: (0, i))
    return pl.pallas_call(
        _plan_kernel,
        out_shape=jax.ShapeDtypeStruct((SUBLANES, n), jnp.int32),
        grid_spec=pltpu.PrefetchScalarGridSpec(
            num_scalar_prefetch=1, grid=(n // tr,), in_specs=[spec, spec], out_specs=spec),
        compiler_params=_params(("parallel",)),
        name="plan_positions",
    )(offsets, idx, rank)


def _sc_worker_id():
    return lax.axis_index("s") * SC_CORES + lax.axis_index("c")


def _sc_chunk_rows(tile, dtype):
    row_bytes = math.prod(tile) * jnp.dtype(dtype).itemsize
    return min(SC_CHUNK_BYTES // row_bytes, SC_MAX_INDICES)


def dispatch_rows(xms, posk, n_rows):
    n_total = sum(x.shape[0] for x in xms)
    tile, dtype = xms[0].shape[1:], xms[0].dtype
    chunk = _sc_chunk_rows(tile, dtype)
    starts, s0 = [], 0
    for x in xms:
        assert x.shape[0] % (SC_WORKERS * chunk) == 0
        starts.append(s0)
        s0 += x.shape[0]
    mesh = plsc.VectorSubcoreMesh(core_axis_name="c", subcore_axis_name="s")

    @functools.partial(
        pl.kernel, mesh=mesh, out_type=jax.ShapeDtypeStruct((n_rows,) + tile, dtype),
        scratch_types=[pltpu.VMEM((chunk,), jnp.int32), pltpu.VMEM((chunk,) + tile, dtype),
                       pltpu.SemaphoreType.DMA],
        name="dispatch_rows_sc")
    def scatter(*refs):
        x_refs, pos_hbm, xs_hbm, idx_v, rows_v, sem = refs[:len(xms)], *refs[len(xms):]
        wid = _sc_worker_id()
        for x_hbm, start in zip(x_refs, starts):
            per_worker = x_hbm.shape[0] // SC_WORKERS

            @pl.loop(0, per_worker // chunk)
            def _(c):
                t0 = wid * per_worker + c * chunk
                pltpu.sync_copy(x_hbm.at[pl.ds(t0, chunk)], rows_v)
                for k in range(TOP_K):
                    pltpu.sync_copy(pos_hbm.at[pl.ds(k * n_total + start + t0, chunk)], idx_v)
                    pltpu.async_copy(rows_v, xs_hbm.at[idx_v], sem).wait()

    return scatter(*xms, posk)


def gather_rows(y, posk):
    n_pairs = posk.shape[0]
    tile, dtype = y.shape[1:], y.dtype
    chunk = _sc_chunk_rows(tile, dtype)
    per_worker = n_pairs // SC_WORKERS
    assert per_worker % chunk == 0
    mesh = plsc.VectorSubcoreMesh(core_axis_name="c", subcore_axis_name="s")

    @functools.partial(
        pl.kernel, mesh=mesh, out_type=jax.ShapeDtypeStruct((n_pairs,) + tile, dtype),
        scratch_types=[pltpu.VMEM((chunk,), jnp.int32), pltpu.VMEM((chunk,) + tile, dtype),
                       pltpu.SemaphoreType.DMA],
        name="gather_rows_sc")
    def gather(y_hbm, pos_hbm, out_hbm, idx_v, rows_v, sem):
        wid = _sc_worker_id()

        @pl.loop(0, per_worker // chunk)
        def _(c):
            base = wid * per_worker + c * chunk
            pltpu.sync_copy(pos_hbm.at[pl.ds(base, chunk)], idx_v)
            pltpu.async_copy(y_hbm.at[idx_v], rows_v, sem).wait()
            pltpu.sync_copy(rows_v, out_hbm.at[pl.ds(base, chunk)])

    return gather(y, posk)


def _ffn_kernel(te_ref, nu_ref, first_ref, slot_ref, nxt_ref, xs_ref, w1_hbm, b1_ref, w2_hbm, b2_ref, y_ref,
                w1f, w2f, w1b, w2b, sem):
    i = pl.program_id(0)

    def weight_copies(e, s):
        return (pltpu.make_async_copy(w1_hbm.at[e], w1f.at[s], sem.at[0, s]),
                pltpu.make_async_copy(w2_hbm.at[e], w2f.at[s], sem.at[1, s]))

    @pl.when(i < nu_ref[0])
    def _():
        s = slot_ref[i]

        @pl.when(first_ref[i] == 1)
        def _():
            @pl.when(i == 0)
            def _():
                for cp in weight_copies(te_ref[i], s):
                    cp.start()
            for cp in weight_copies(te_ref[i], s):
                cp.wait()

            @pl.when(nxt_ref[i] >= 0)
            def _():
                for cp in weight_copies(nxt_ref[i], 1 - s):
                    cp.start()
            for c in range(D_MODEL // LANES):
                w1b[c * LANES:(c + 1) * LANES, :] = w1f[s, c * LANES:(c + 1) * LANES, :].astype(BF16)
            for c in range(D_EXPERT // LANES):
                w2b[c * LANES:(c + 1) * LANES, :] = w2f[s, c * LANES:(c + 1) * LANES, :].astype(BF16)

        x = _unpack_rows(xs_ref[...])
        h = jnp.dot(x, w1b[...], preferred_element_type=F32) + b1_ref[0]
        g = jnp.minimum(h[:, :D_EXPERT], SWIGLU_LIMIT)
        lin = jnp.clip(h[:, D_EXPERT:], -SWIGLU_LIMIT, SWIGLU_LIMIT)
        a = (lin + 1.0) * (g * jax.nn.sigmoid(SWIGLU_ALPHA * g))
        y = jnp.dot(a.astype(BF16), w2b[...], preferred_element_type=F32) + b2_ref[0]
        y_ref[...] = _pack_rows(y.astype(BF16))

    @pl.when(i >= nu_ref[0])
    def _():
        y_ref[...] = jnp.zeros(y_ref.shape, y_ref.dtype)


def grouped_ffn(sched, xs, w1, b1, w2, b2):
    r = xs.shape[0]
    d = w1.shape[1]
    nt = r // FFN_TILE
    tile = (FFN_TILE,) + xs.shape[1:]
    rows = lambda i, te, nu, *_: (jnp.minimum(i, nu[0] - 1), 0, 0)
    bsel = lambda i, te, *_: (te[i], 0, 0)
    return pl.pallas_call(
        _ffn_kernel,
        out_shape=jax.ShapeDtypeStruct(xs.shape, xs.dtype),
        grid_spec=pltpu.PrefetchScalarGridSpec(
            num_scalar_prefetch=5, grid=(nt,),
            in_specs=[pl.BlockSpec(tile, rows),
                      pl.BlockSpec(memory_space=pl.ANY),
                      pl.BlockSpec((1, 1, 2 * D_EXPERT), bsel),
                      pl.BlockSpec(memory_space=pl.ANY),
                      pl.BlockSpec((1, 1, d), bsel)],
            out_specs=pl.BlockSpec(tile, lambda i, *_: (i, 0, 0)),
            scratch_shapes=[pltpu.VMEM((2, d, 2 * D_EXPERT), F32), pltpu.VMEM((2, D_EXPERT, d), F32),
                            pltpu.VMEM((d, 2 * D_EXPERT), BF16), pltpu.VMEM((D_EXPERT, d), BF16),
                            pltpu.SemaphoreType.DMA((2, 2))]),
        compiler_params=_params(("arbitrary",), VMEM_LIMIT),
        name="grouped_ffn",
    )(*sched, xs, w1, b1.reshape(N_EXPERTS, 1, -1), w2, b2.reshape(N_EXPERTS, 1, -1))


def _combine_kernel(x1_ref, wts_ref, m_ref, fn_ref, y0_ref, y1_ref, y2_ref, y3_ref, o_ref, *, final):
    w = wts_ref[0]
    rows = lambda ref: pltpu.bitcast(ref[...], BF16).astype(F32)
    acc = w[:, 0:1, :] * rows(y0_ref)
    for k, y_ref in ((1, y1_ref), (2, y2_ref), (3, y3_ref)):
        acc = acc + w[:, k:k + 1, :] * rows(y_ref)
    out = x1_ref[0] + m_ref[0][5:6] * acc.reshape(x1_ref.shape[1], D_MODEL)
    if final:
        out = _rms(out, fn_ref[...])
    o_ref[0] = out


def combine_rows(x1, wts, mods, mod_off, fn, yg, row_off, n_total, *, final):
    bm, sm, d = x1.shape
    tr = min(COMBINE_TILE, sm)
    nb = sm // tr
    row = lambda width: pl.BlockSpec((1, tr, width), lambda b, i: (b, i, 0))
    ysel = lambda k: pl.BlockSpec((tr,) + yg.shape[1:],
                                  lambda b, i: ((k * n_total + row_off) // tr + b * nb + i, 0, 0))
    return pl.pallas_call(
        functools.partial(_combine_kernel, final=final),
        out_shape=jax.ShapeDtypeStruct((bm, sm, d), F32),
        grid=(bm, nb),
        in_specs=[row(d), pl.BlockSpec((1, tr, SUBLANES, LANES), lambda b, i: (b, i, 0, 0)),
                  pl.BlockSpec((1, SUBLANES, d), lambda b, i: (b + mod_off, 0, 0)),
                  _const_spec((1, d))] + [ysel(k) for k in range(TOP_K)],
        out_specs=row(d),
        compiler_params=_params(("parallel", "parallel"), VMEM_LIMIT),
        name="combine_rows",
    )(x1, wts, mods, fn, yg, yg, yg, yg)


def _axial_angles(n_tokens, rot_dim):
    t = jnp.arange(n_tokens)
    rows = (t // GRID_W).astype(F32)
    cols = (t % GRID_W).astype(F32)
    n_freq = rot_dim // 4
    inv = ROPE_THETA ** (-jnp.arange(n_freq, dtype=F32) / n_freq)
    return jnp.concatenate([rows[:, None] * inv, cols[:, None] * inv], axis=-1)


def _lane_table(parts, n):
    cols = []
    for p in parts:
        cols.append(jnp.broadcast_to(jnp.asarray(p, F32), (n, p.shape[-1])) if hasattr(p, "shape") else p)
    return jnp.concatenate(cols, axis=-1)


def _swap_halves(w):
    half = w.shape[-1] // 2
    return jnp.concatenate([-w[..., half:], w[..., :half]], axis=-1)


def _prep_l0(w_in, q_norm, kv_norm, w_uq, w_uk, w_uv, w_out):
    d = w_in.shape[0]
    o_kr = MLA_Q_RANK + MLA_KV_RANK
    kr_cols = w_in[:, o_kr:o_kr + MLA_ROPE]
    win = jnp.concatenate(
        [w_in[:, :o_kr], kr_cols, _swap_halves(kr_cols), jnp.zeros((d, LANES - 2 * MLA_ROPE), F32),
         w_in[:, o_kr + MLA_ROPE:]], axis=1).astype(BF16)
    qk = MLA_NOPE + MLA_ROPE
    wuq3 = w_uq.reshape(MLA_Q_RANK, MLA_HEADS, qk)
    wuq = jnp.pad(wuq3, ((0, 0), (0, 0), (0, LANES - qk))).reshape(MLA_Q_RANK, -1).astype(BF16)
    wuqs3 = jnp.concatenate(
        [jnp.zeros((MLA_Q_RANK, MLA_HEADS, MLA_NOPE), F32), _swap_halves(wuq3[:, :, MLA_NOPE:]),
         jnp.zeros((MLA_Q_RANK, MLA_HEADS, LANES - qk), F32)], axis=-1)
    wuqs = wuqs3.reshape(MLA_Q_RANK, -1).astype(BF16)
    wuk3 = w_uk.reshape(MLA_KV_RANK, MLA_HEADS, MLA_NOPE)
    wuk = jnp.pad(wuk3, ((0, 0), (0, 0), (0, LANES - MLA_NOPE))).reshape(MLA_KV_RANK, -1).astype(BF16)
    wuv3 = w_uv.reshape(MLA_KV_RANK, MLA_HEADS, MLA_V)
    wuv = jnp.pad(wuv3, ((0, 0), (0, 0), (0, LANES - MLA_V))).reshape(MLA_KV_RANK, -1).astype(BF16)
    eye = jnp.eye(MLA_ROPE, dtype=F32)
    e_head = jnp.concatenate([jnp.zeros((MLA_ROPE, MLA_NOPE), F32), eye,
                              jnp.zeros((MLA_ROPE, LANES - qk), F32)], axis=1)
    e = jnp.pad(jnp.tile(e_head, (1, MLA_HEADS)), ((0, LANES - MLA_ROPE), (0, 0))).astype(BF16)
    wa3 = w_out[:MLA_HEADS * MLA_V].reshape(MLA_HEADS, MLA_V, d)
    wa = jnp.pad(wa3, ((0, 0), (0, LANES - MLA_V), (0, 0))).reshape(MLA_HEADS * LANES, d).astype(BF16)
    wc = w_out[MLA_HEADS * MLA_V:].astype(BF16)
    return dict(win=win, qg=q_norm.reshape(1, -1), kvg=kv_norm.reshape(1, -1), wuq=wuq, wuqs=wuqs,
                wuk=wuk, e=e, wuv=wuv, wa=wa, wc=wc)


def _l0_tables(n):
    ang = _axial_angles(n, MLA_ROPE)
    cos, sin = jnp.cos(ang), jnp.sin(ang)
    one = jnp.ones((n, 1), F32)
    zero = jnp.zeros((n, 1), F32)
    rest = LANES - MLA_NOPE - MLA_ROPE
    cq = jnp.concatenate([jnp.tile(one, (1, MLA_NOPE)), cos, cos, jnp.tile(one, (1, rest))], axis=1)
    sq = jnp.concatenate([jnp.tile(zero, (1, MLA_NOPE)), sin, sin, jnp.tile(zero, (1, rest))], axis=1)
    ck = jnp.concatenate([cos, cos, jnp.tile(zero, (1, LANES - MLA_ROPE))], axis=1)
    sk = jnp.concatenate([sin, sin, jnp.tile(zero, (1, LANES - MLA_ROPE))], axis=1)
    return cq, sq, ck, sk


def _pad_heads(w, n_heads, dim):
    d = w.shape[0]
    return jnp.pad(w.reshape(d, n_heads, dim), ((0, 0), (0, 0), (0, LANES - dim))).reshape(d, n_heads * LANES)


def _prep_l1(w_in, q_norm, k_norm, ln_g, ln_b, w_s, b_s, w_out):
    d = w_in.shape[0]
    qd = GQA_HEADS * GQA_HEAD_DIM
    kd = GQA_KV_HEADS * GQA_HEAD_DIM
    wq, wk, wv = w_in[:, :qd], w_in[:, qd:qd + kd], w_in[:, qd + kd:qd + 2 * kd]
    rest = w_in[:, qd + 2 * kd:]
    win = jnp.concatenate([_pad_heads(wq, GQA_HEADS, GQA_HEAD_DIM), _pad_heads(wk, GQA_KV_HEADS, GQA_HEAD_DIM),
                           _pad_heads(wv, GQA_KV_HEADS, GQA_HEAD_DIM), wv, rest], axis=1).astype(BF16)
    swq = _swap_halves(wq.reshape(d, GQA_HEADS, GQA_HEAD_DIM)).reshape(d, qd)
    swk = _swap_halves(wk.reshape(d, GQA_KV_HEADS, GQA_HEAD_DIM)).reshape(d, kd)
    wsw = jnp.concatenate([_pad_heads(swq, GQA_HEADS, GQA_HEAD_DIM),
                           _pad_heads(swk, GQA_KV_HEADS, GQA_HEAD_DIM)], axis=1).astype(BF16)
    half = GQA_HEAD_DIM // 2
    padg = lambda g: jnp.pad(g, (0, LANES - GQA_HEAD_DIM)).reshape(1, LANES)
    swapg = lambda g: jnp.concatenate([g[half:], g[:half]])
    wa3 = w_out[:qd].reshape(GQA_HEADS, GQA_HEAD_DIM, d)
    wa = jnp.pad(wa3, ((0, 0), (0, LANES - GQA_HEAD_DIM), (0, 0))).reshape(GQA_HEADS * LANES, d).astype(BF16)
    wc = w_out[qd:].astype(BF16)
    bs = jnp.repeat(b_s.T, LANES, axis=1)
    return dict(win=win, wsw=wsw, qg=padg(q_norm), qgs=padg(swapg(q_norm)), kg=padg(k_norm),
                kgs=padg(swapg(k_norm)), lng=ln_g.reshape(1, -1), lnb=ln_b.reshape(1, -1),
                ws=w_s.astype(BF16), bs=bs, wa=wa, wc=wc)


def _l1_tables(n):
    ang = _axial_angles(n, GQA_HEAD_DIM)
    cos, sin = jnp.cos(ang), jnp.sin(ang)
    pad = LANES - GQA_HEAD_DIM
    c = jnp.concatenate([cos, cos, jnp.ones((n, pad), F32)], axis=1)
    s = jnp.concatenate([sin, sin, jnp.zeros((n, pad), F32)], axis=1)
    return c, s


def _pad_lanes(x, width):
    return jnp.pad(x, [(0, 0)] * (x.ndim - 1) + [(0, width - x.shape[-1])])


def routed_ffn(groups, mods, n2, moe, tri, final_norm, *, final):
    router_w, router_b, w1, b1, w2, b2 = moe
    rw = _pad_lanes(router_w, LANES)
    rwh = rw.astype(BF16)
    rwl = (rw - rwh.astype(F32)).astype(BF16)
    rb = jnp.concatenate([router_b, jnp.full((LANES - N_EXPERTS,), NEG_BIG, F32)]).reshape(1, LANES)
    base = jnp.zeros((SUBLANES, LANES), F32)
    routed = []
    for g in groups:
        x1, xm, idx, wts, rank, base = post_mixer(
            g["attn"], g["other"], g["x"], mods, g["mod_off"], n2, g["wa"], g["wc"], rwh, rwl, rb, tri, base)
        routed.append((x1, xm, idx, wts, rank))
    n_total = sum(r[0].shape[0] * r[0].shape[1] for r in routed)
    nt = n_total * TOP_K // FFN_TILE + N_EXPERTS

    counts = base[0, :N_EXPERTS].astype(jnp.int32)
    tiles = (counts + FFN_TILE - 1) // FFN_TILE
    tile_end = jnp.cumsum(tiles)
    offsets = (tile_end - tiles) * FFN_TILE
    n_used = tile_end[-1:].astype(jnp.int32)
    experts = jnp.arange(N_EXPERTS)
    busy = tiles > 0
    slot_e = (jnp.cumsum(busy) - 1) % 2
    later = jnp.where(busy[None, :] & (experts[None, :] > experts[:, None]), experts[None, :], N_EXPERTS)
    nxt_e = jnp.min(later, axis=1)
    nxt_e = jnp.where(nxt_e == N_EXPERTS, -1, nxt_e)
    tile_expert = jnp.sum(jnp.arange(nt)[:, None] >= tile_end[None, :], axis=1)
    tile_expert = jnp.minimum(tile_expert, jnp.max(jnp.where(busy, experts, 0)))
    first = jnp.concatenate([jnp.ones((1,), bool), tile_expert[1:] != tile_expert[:-1]])
    of_tile = tile_expert[:, None] == experts[None, :]
    slot_t = jnp.sum(jnp.where(of_tile, slot_e[None, :], 0), axis=1)
    nxt_t = jnp.sum(jnp.where(of_tile, nxt_e[None, :], 0), axis=1)
    sched = tuple(a.astype(jnp.int32) for a in (tile_expert, n_used, first, slot_t, nxt_t))

    positions, xms = [], []
    for (x1, xm, idx, wts, rank) in routed:
        positions.append(plan_positions(offsets, idx, rank)[:TOP_K])
        xms.append(xm.reshape((idx.shape[1],) + ROW_WORDS))
    posk = jnp.concatenate(positions, axis=1).reshape(-1)
    xs = dispatch_rows(xms, posk, nt * FFN_TILE)
    y = grouped_ffn(sched, xs, w1, b1, w2, b2)
    outs = []
    for g, (x1, xm, idx, wts, rank), pos in zip(groups, routed, positions):
        yg = gather_rows(y, pos.reshape(-1))
        outs.append(combine_rows(x1, wts, mods, g["mod_off"], final_norm, yg, 0, pos.shape[1], final=final))
    return outs


def kernel(x_prompt, x_sample, cache_l0_ckv, cache_l0_krope, cache_l1_k, cache_l1_v, c, c_ctx,
           l0_ada_w, l0_ada_b, l0_norm1, l0_w_in, l0_q_norm, l0_kv_norm, l0_w_uq, l0_w_uk, l0_w_uv,
           l0_conv_w, l0_conv_b, l0_conv_ln_g, l0_conv_ln_b, l0_w_out, l0_norm2,
           l0_router_w, l0_router_b, l0_w1, l0_b1, l0_w2, l0_b2,
           l1_ada_w, l1_ada_b, l1_norm1, l1_w_in, l1_q_norm, l1_k_norm, l1_gmlp_ln_g, l1_gmlp_ln_b,
           l1_w_s, l1_b_s, l1_w_out, l1_norm2,
           l1_router_w, l1_router_b, l1_w1, l1_b1, l1_w2, l1_b2,
           final_norm):
    bp, sp, d = x_prompt.shape
    bs, ss, _ = x_sample.shape
    past = cache_l0_ckv.shape[1]
    n_p = bp * sp

    cond8 = jnp.concatenate([c_ctx[None], c, jnp.zeros((SUBLANES - 1 - bs, d), F32)], axis=0)
    mods0 = adaln(cond8, l0_ada_w, l0_ada_b)
    mods1 = adaln(cond8, l1_ada_w, l1_ada_b)
    tri = jnp.tril(jnp.ones((ROW_TILE, ROW_TILE), F32), -1).astype(BF16)
    fn = final_norm.reshape(1, d)

    w0 = _prep_l0(l0_w_in, l0_q_norm, l0_kv_norm, l0_w_uq, l0_w_uk, l0_w_uv, l0_w_out)
    n1 = l0_norm1.reshape(1, d)
    hp = x_prompt.reshape(1, n_p, d)
    q_p, k_p, v_p, ckv_p, kr_p, u_p = l0_inproj(hp, mods0, 0, n1, w0, None)
    q_s, k_s, v_s, _, _, u_s = l0_inproj(x_sample, mods0, 1, n1, w0, _l0_tables(ss))
    k_c, v_c = mla_ctx_kv(cache_l0_ckv, _pad_lanes(cache_l0_krope, LANES), w0)
    hw = MLA_HEADS * LANES
    att_p = attention(q_p.reshape(bp, sp, hw), k_p.reshape(bp, sp, hw), v_p.reshape(bp, sp, hw),
                      n_heads=MLA_HEADS, n_kv=MLA_HEADS, scale=MLA_SCALE, heads_per_step=MLA_HEADS)
    att_s = attention(q_s, jnp.concatenate([k_c, k_s], axis=1), jnp.concatenate([v_c, v_s], axis=1),
                      n_heads=MLA_HEADS, n_kv=MLA_HEADS, scale=MLA_SCALE, heads_per_step=LATENT_HEADS_PER_STEP)
    conv_p = conformer_conv(u_p.reshape(bp, sp, CONV_CH), l0_conv_w, l0_conv_b, l0_conv_ln_g, l0_conv_ln_b)
    conv_s = conformer_conv(u_s, l0_conv_w, l0_conv_b, l0_conv_ln_g, l0_conv_ln_b)
    groups = [dict(attn=att_p.reshape(1, n_p, hw), other=conv_p.reshape(1, n_p, CONV_CH), x=hp, mod_off=0,
                   wa=w0["wa"], wc=w0["wc"]),
              dict(attn=att_s, other=conv_s, x=x_sample, mod_off=1, wa=w0["wa"], wc=w0["wc"])]
    hp, hs = routed_ffn(groups, mods0, l0_norm2.reshape(1, d),
                        (l0_router_w, l0_router_b, l0_w1, l0_b1, l0_w2, l0_b2), tri, fn, final=False)
    new_l0_ckv = ckv_p.reshape(bp, sp, MLA_KV_RANK)
    new_l0_krope = kr_p.reshape(bp, sp, MLA_ROPE)

    w1p = _prep_l1(l1_w_in, l1_q_norm, l1_k_norm, l1_gmlp_ln_g, l1_gmlp_ln_b, l1_w_s, l1_b_s, l1_w_out)
    n1 = l1_norm1.reshape(1, d)
    q_p, k_p, vp_p, gat_p, kt_p, vt_p = l1_inproj(hp, mods1, 0, n1, w1p, None, ctx_seq=sp)
    q_s, k_s, vp_s, gat_s = l1_inproj(hs, mods1, 1, n1, w1p, _l1_tables(ss))
    qw = GQA_HEADS * LANES
    kw = GQA_KV_HEADS * LANES
    pad_kv = lambda t: _pad_lanes(t, LANES).reshape(bs, past, kw).astype(BF16)
    att_p = attention(q_p.reshape(bp, sp, qw), k_p.reshape(bp, sp, kw), vp_p.reshape(bp, sp, kw),
                      n_heads=GQA_HEADS, n_kv=GQA_KV_HEADS, scale=GQA_SCALE, heads_per_step=GQA_HEADS)
    att_s = attention(q_s, jnp.concatenate([pad_kv(cache_l1_k), k_s], axis=1),
                      jnp.concatenate([_with_sum_lane(pad_kv(cache_l1_v)), vp_s], axis=1),
                      n_heads=GQA_HEADS, n_kv=GQA_KV_HEADS, scale=GQA_SCALE, heads_per_step=LATENT_HEADS_PER_STEP)
    groups = [dict(attn=att_p.reshape(1, n_p, qw), other=gat_p, x=hp, mod_off=0, wa=w1p["wa"], wc=w1p["wc"]),
              dict(attn=att_s, other=gat_s, x=hs, mod_off=1, wa=w1p["wa"], wc=w1p["wc"])]
    yp, ys = routed_ffn(groups, mods1, l1_norm2.reshape(1, d),
                        (l1_router_w, l1_router_b, l1_w1, l1_b1, l1_w2, l1_b2), tri, fn, final=True)
    new_l1_k = jnp.transpose(kt_p, (0, 3, 1, 2))
    new_l1_v = jnp.transpose(vt_p.reshape(bp, GQA_KV_HEADS, GQA_HEAD_DIM, sp), (0, 3, 1, 2))
    return (yp.reshape(bp, sp, d), ys, new_l0_ckv, new_l0_krope, new_l1_k, new_l1_v)
```

```python
import functools
import math

import jax
import jax.numpy as jnp
from jax import lax
from jax.experimental import pallas as pl
from jax.experimental.pallas import tpu as pltpu
from jax.experimental.pallas import tpu_sc as plsc

F32 = jnp.float32
BF16 = jnp.bfloat16
HIGHEST = lax.Precision.HIGHEST

LANES = 128
SUBLANES = 8
VMEM_LIMIT = 56 * 1024 * 1024

D_MODEL = 1024
GRID_W = 64
ROPE_THETA = 10000.0
EPS = 1e-6
N_MOD = 6

MLA_HEADS = 8
MLA_NOPE = 64
MLA_ROPE = 32
MLA_V = 64
MLA_Q_RANK = 384
MLA_KV_RANK = 256
MLA_SCALE = 1.0 / math.sqrt(MLA_NOPE + MLA_ROPE)
CONV_CH = 512
CONV_WIDTH = 31
CONV_HALO = 16

GQA_HEADS = 8
GQA_KV_HEADS = 2
GQA_HEAD_DIM = 64
GQA_SCALE = 1.0 / math.sqrt(GQA_HEAD_DIM)
CHUNK = 128
GMLP_GROUPS = 4
GMLP_CH = 512

N_EXPERTS = 32
TOP_K = 4
D_EXPERT = 1024
SWIGLU_LIMIT = 7.0
SWIGLU_ALPHA = 1.702

ROW_TILE = 512
FFN_TILE = 256
SC_CORES = 2
SC_SUBCORES = 16
SC_WORKERS = SC_CORES * SC_SUBCORES
SC_CHUNK_BYTES = 256 * 1024
SC_MAX_INDICES = 128
COMBINE_TILE = 256
ATT_Q_TILE = 256
ATT_LONG_KEYS = 1024
V_SUM_LANE = 64
LATENT_HEADS_PER_STEP = 4
NEG_BIG = -1e30


def _params(sem, vmem=None):
    return pltpu.CompilerParams(dimension_semantics=sem, vmem_limit_bytes=vmem)


def _rms(x, g):
    return x * lax.rsqrt(jnp.mean(x * x, axis=-1, keepdims=True) + EPS) * g


ROW_WORDS = (SUBLANES // 2, LANES)


def _pack_rows(x_bf16):
    return pltpu.bitcast(x_bf16.reshape(x_bf16.shape[0], SUBLANES, LANES), jnp.int32)


def _unpack_rows(words):
    return pltpu.bitcast(words, BF16).reshape(words.shape[0], D_MODEL)


def _const_spec(shape):
    nd = len(shape)
    return pl.BlockSpec(shape, lambda *_: (0,) * nd)


def _adaln_kernel(c_ref, w_ref, b_ref, o_ref):
    c = c_ref[...]
    s = c * jax.nn.sigmoid(c)
    o_ref[...] = jnp.dot(s, w_ref[...], preferred_element_type=F32, precision=HIGHEST) + b_ref[...]


def adaln(cond8, ada_w, ada_b):
    d, n = ada_w.shape
    bn = n // 4
    m = pl.pallas_call(
        _adaln_kernel,
        out_shape=jax.ShapeDtypeStruct((SUBLANES, n), F32),
        grid=(n // bn,),
        in_specs=[_const_spec((SUBLANES, d)),
                  pl.BlockSpec((d, bn), lambda j: (0, j)),
                  pl.BlockSpec((1, bn), lambda j: (0, j))],
        out_specs=pl.BlockSpec((SUBLANES, bn), lambda j: (0, j)),
        compiler_params=_params(("arbitrary",), VMEM_LIMIT),
        name="adaln",
    )(cond8, ada_w, ada_b.reshape(1, n))
    m = m.reshape(SUBLANES, N_MOD, d)
    return jnp.pad(m, ((0, 0), (0, SUBLANES - N_MOD), (0, 0)))


def _l0_inproj_kernel(*refs, rope):
    if rope:
        (x_ref, m_ref, n1_ref, win_ref, qg_ref, kvg_ref, wuq_ref, wuqs_ref, wuk_ref, e_ref, wuv_ref,
         cq_ref, sq_ref, ck_ref, sk_ref, q_out, k_out, v_out, ckv_out, kr_out, u_out) = refs
    else:
        (x_ref, m_ref, n1_ref, win_ref, qg_ref, kvg_ref, wuq_ref, wuk_ref, e_ref, wuv_ref,
         q_out, k_out, v_out, ckv_out, kr_out, u_out) = refs
    x = x_ref[0]
    m = m_ref[0]
    h = _rms(x, n1_ref[...]) * (1.0 + m[1:2]) + m[0:1]
    z = jnp.dot(h.astype(BF16), win_ref[...], preferred_element_type=F32)
    c_q = z[:, 0:MLA_Q_RANK]
    c_kv = z[:, MLA_Q_RANK:MLA_Q_RANK + MLA_KV_RANK]
    kr_blk = z[:, 640:768]
    val = z[:, 768:768 + CONV_CH]
    gate = z[:, 768 + CONV_CH:768 + 2 * CONV_CH]

    cqn = _rms(c_q, qg_ref[...]).astype(BF16)
    q = jnp.dot(cqn, wuq_ref[...], preferred_element_type=F32)
    if rope:
        qs = jnp.dot(cqn, wuqs_ref[...], preferred_element_type=F32)
        cq = cq_ref[...]
        sq = sq_ref[...]
        for hd in range(MLA_HEADS):
            sl = slice(hd * LANES, (hd + 1) * LANES)
            q_out[0, :, sl] = (q[:, sl] * cq + qs[:, sl] * sq).astype(BF16)
        kr = kr_blk * ck_ref[...] + pltpu.roll(kr_blk, LANES - MLA_ROPE, 1) * sk_ref[...]
    else:
        q_out[0] = q.astype(BF16)
        kr = kr_blk

    ckv = _rms(c_kv, kvg_ref[...])
    ckv_out[0] = ckv
    kr_out[0] = kr_blk[:, 0:MLA_ROPE]
    ckv_b = ckv.astype(BF16)
    k = (jnp.dot(ckv_b, wuk_ref[...], preferred_element_type=F32)
         + jnp.dot(kr.astype(BF16), e_ref[...], preferred_element_type=F32))
    k_out[0] = k.astype(BF16)
    v_out[0] = _with_sum_lane(jnp.dot(ckv_b, wuv_ref[...], preferred_element_type=F32)).astype(BF16)
    u_out[0] = val * jax.nn.sigmoid(gate)


def l0_inproj(x, mods, mod_off, n1, w, tables):
    bm, sm, d = x.shape
    tr = min(ROW_TILE, sm)
    rope = tables is not None
    hp = MLA_HEADS * LANES
    row = lambda width: pl.BlockSpec((1, tr, width), lambda b, i: (b, i, 0))
    in_specs = [row(d),
                pl.BlockSpec((1, SUBLANES, d), lambda b, i: (b + mod_off, 0, 0)),
                _const_spec((1, d)), _const_spec(w["win"].shape),
                _const_spec((1, MLA_Q_RANK)), _const_spec((1, MLA_KV_RANK)),
                _const_spec(w["wuq"].shape)]
    args = [x, mods, n1, w["win"], w["qg"], w["kvg"], w["wuq"]]
    if rope:
        in_specs.append(_const_spec(w["wuqs"].shape))
        args.append(w["wuqs"])
    in_specs += [_const_spec(w["wuk"].shape), _const_spec(w["e"].shape), _const_spec(w["wuv"].shape)]
    args += [w["wuk"], w["e"], w["wuv"]]
    if rope:
        in_specs += [pl.BlockSpec((tr, LANES), lambda b, i: (i, 0))] * 4
        args += list(tables)
    out_shape = [jax.ShapeDtypeStruct((bm, sm, hp), BF16),
                 jax.ShapeDtypeStruct((bm, sm, hp), BF16),
                 jax.ShapeDtypeStruct((bm, sm, hp), BF16),
                 jax.ShapeDtypeStruct((bm, sm, MLA_KV_RANK), F32),
                 jax.ShapeDtypeStruct((bm, sm, MLA_ROPE), F32),
                 jax.ShapeDtypeStruct((bm, sm, CONV_CH), F32)]
    out_specs = [row(hp), row(hp), row(hp), row(MLA_KV_RANK), row(MLA_ROPE), row(CONV_CH)]
    return pl.pallas_call(
        functools.partial(_l0_inproj_kernel, rope=rope),
        out_shape=out_shape, grid=(bm, sm // tr), in_specs=in_specs, out_specs=out_specs,
        compiler_params=_params(("parallel", "parallel"), VMEM_LIMIT),
        name="l0_inproj_rope" if rope else "l0_inproj",
    )(*args)


def _mla_ctx_kv_kernel(ckv_ref, kr_ref, wuk_ref, e_ref, wuv_ref, k_out, v_out):
    ckv_b = ckv_ref[0].astype(BF16)
    k = (jnp.dot(ckv_b, wuk_ref[...], preferred_element_type=F32)
         + jnp.dot(kr_ref[0].astype(BF16), e_ref[...], preferred_element_type=F32))
    k_out[0] = k.astype(BF16)
    v_out[0] = _with_sum_lane(jnp.dot(ckv_b, wuv_ref[...], preferred_element_type=F32)).astype(BF16)


def mla_ctx_kv(ckv, kr128, w):
    b, s, _ = ckv.shape
    hp = MLA_HEADS * LANES
    blk = lambda width: pl.BlockSpec((1, s, width), lambda i: (i, 0, 0))
    return pl.pallas_call(
        _mla_ctx_kv_kernel,
        out_shape=[jax.ShapeDtypeStruct((b, s, hp), BF16)] * 2,
        grid=(b,),
        in_specs=[blk(MLA_KV_RANK), blk(LANES), _const_spec(w["wuk"].shape),
                  _const_spec(w["e"].shape), _const_spec(w["wuv"].shape)],
        out_specs=[blk(hp), blk(hp)],
        compiler_params=_params(("parallel",), VMEM_LIMIT),
        name="mla_ctx_kv",
    )(ckv, kr128, w["wuk"], w["e"], w["wuv"])


def _conv_kernel(prev_ref, cur_ref, next_ref, w_ref, b_ref, g_ref, beta_ref, o_ref, pad_ref, sh_ref, *, rb):
    i = pl.program_id(1)
    last = pl.num_programs(1) - 1
    zeros = jnp.zeros((CONV_HALO, CONV_CH), F32)
    pad_ref[0:CONV_HALO, :] = jnp.where(i == 0, zeros, prev_ref[0])
    pad_ref[CONV_HALO:CONV_HALO + rb, :] = cur_ref[0]
    pad_ref[CONV_HALO + rb:CONV_HALO + rb + CONV_HALO, :] = jnp.where(i == last, zeros, next_ref[0])
    span = rb + 2 * CONV_HALO - SUBLANES
    for r in range(1, SUBLANES):
        sh_ref[r] = pad_ref[r:r + span, :]
    w = w_ref[...]
    shift = CONV_HALO - CONV_WIDTH // 2
    acc = jnp.zeros((rb, CONV_CH), F32) + b_ref[...]
    for k in range(CONV_WIDTH):
        off = k + shift
        r, a = off % SUBLANES, off // SUBLANES * SUBLANES
        window = pad_ref[a:a + rb, :] if r == 0 else sh_ref[r, a:a + rb, :]
        acc = acc + window * w[k:k + 1, :]
    mu = jnp.mean(acc, axis=-1, keepdims=True)
    cen = acc - mu
    var = jnp.mean(cen * cen, axis=-1, keepdims=True)
    y = cen * lax.rsqrt(var + EPS) * g_ref[...] + beta_ref[...]
    o_ref[0] = (y * jax.nn.sigmoid(y)).astype(BF16)


def conformer_conv(u, conv_w, conv_b, ln_g, ln_b):
    b, s, c = u.shape
    rb = min(256, s)
    nh = rb // CONV_HALO
    n_halo_blocks = s // CONV_HALO
    wpad = jnp.pad(conv_w.reshape(CONV_WIDTH, c), ((0, 32 - CONV_WIDTH), (0, 0)))
    return pl.pallas_call(
        functools.partial(_conv_kernel, rb=rb),
        out_shape=jax.ShapeDtypeStruct((b, s, c), BF16),
        grid=(b, s // rb),
        in_specs=[pl.BlockSpec((1, CONV_HALO, c), lambda bi, i: (bi, jnp.maximum(i * nh - 1, 0), 0)),
                  pl.BlockSpec((1, rb, c), lambda bi, i: (bi, i, 0)),
                  pl.BlockSpec((1, CONV_HALO, c),
                               lambda bi, i: (bi, jnp.minimum((i + 1) * nh, n_halo_blocks - 1), 0)),
                  _const_spec((32, c)), _const_spec((1, c)), _const_spec((1, c)), _const_spec((1, c))],
        out_specs=pl.BlockSpec((1, rb, c), lambda bi, i: (bi, i, 0)),
        scratch_shapes=[pltpu.VMEM((rb + 2 * CONV_HALO, c), F32),
                        pltpu.VMEM((SUBLANES, rb + 2 * CONV_HALO - SUBLANES, c), F32)],
        compiler_params=_params(("parallel", "parallel"), VMEM_LIMIT),
        name="conformer_conv",
    )(u, u, u, wpad, conv_b.reshape(1, c), ln_g.reshape(1, c), ln_b.reshape(1, c))


def _with_sum_lane(v):
    lane = lax.broadcasted_iota(jnp.int32, (1, v.shape[-1]), 1)
    return v + ((lane & (LANES - 1)) == V_SUM_LANE).astype(v.dtype)


def _attn_kernel(q_ref, k_ref, v_ref, o_ref, *, heads, rep, scale, mxu_denominator):
    for hd in range(heads):
        g = hd // rep
        q = q_ref[0, :, hd * LANES:(hd + 1) * LANES]
        k = k_ref[0, :, g * LANES:(g + 1) * LANES]
        s = lax.dot_general(q, k, (((1,), (1,)), ((), ())), preferred_element_type=F32) * scale
        m = jnp.max(s, axis=-1, keepdims=True)
        if mxu_denominator:
            p = jnp.exp((s - m).astype(BF16))
            o = jnp.dot(p, v_ref[0, :, g * LANES:(g + 1) * LANES], preferred_element_type=F32)
            l = o[:, V_SUM_LANE:V_SUM_LANE + 1]
        else:
            p = jnp.exp(s - m)
            l = jnp.sum(p, axis=-1, keepdims=True)
            o = jnp.dot(p.astype(BF16), v_ref[0, :, g * LANES:(g + 1) * LANES], preferred_element_type=F32)
        o_ref[0, :, hd * LANES:(hd + 1) * LANES] = (o / l).astype(BF16)


def attention(q, k, v, *, n_heads, n_kv, scale, heads_per_step):
    b, sq, _ = q.shape
    sk = k.shape[1]
    rep = n_heads // n_kv
    tq = min(ATT_Q_TILE, sq)
    hb = heads_per_step
    grid = (b, n_heads // hb, sq // tq)
    if hb >= rep:
        kv_spec = pl.BlockSpec((1, sk, hb // rep * LANES), lambda bi, h, i: (bi, 0, h))
        kern_rep = rep
    else:
        assert rep % hb == 0
        kv_spec = pl.BlockSpec((1, sk, LANES), lambda bi, h, i: (bi, 0, h * hb // rep))
        kern_rep = hb
    kern = functools.partial(_attn_kernel, heads=hb, rep=kern_rep, scale=scale,
                             mxu_denominator=sk >= ATT_LONG_KEYS)
    q_spec = pl.BlockSpec((1, tq, heads_per_step * LANES), lambda bi, h, i: (bi, i, h))
    return pl.pallas_call(
        kern,
        out_shape=jax.ShapeDtypeStruct(q.shape, BF16),
        grid=grid, in_specs=[q_spec, kv_spec, kv_spec], out_specs=q_spec,
        compiler_params=_params(("parallel", "parallel", "parallel"), VMEM_LIMIT),
        name="attention",
    )(q, k, v)


def _l1_inproj_kernel(*refs, rope):
    if rope:
        (x_ref, m_ref, n1_ref, win_ref, wsw_ref, qg_ref, qgs_ref, kg_ref, kgs_ref, lng_ref, lnb_ref,
         ws_ref, bs_ref, c_ref, s_ref, q_out, k_out, vp_out, g_out) = refs
        kt_out = vt_out = None
    else:
        (x_ref, m_ref, n1_ref, win_ref, qg_ref, kg_ref, lng_ref, lnb_ref,
         ws_ref, bs_ref, q_out, k_out, vp_out, g_out, kt_out, vt_out) = refs
    x = x_ref[0]
    m = m_ref[0]
    hb = (_rms(x, n1_ref[...]) * (1.0 + m[1:2]) + m[0:1]).astype(BF16)
    z = jnp.dot(hb, win_ref[...], preferred_element_type=F32)
    qw = GQA_HEADS * LANES
    kw = GQA_KV_HEADS * LANES
    o_k, o_vp, o_v, o_u, o_vg = qw, qw + kw, qw + 2 * kw, qw + 2 * kw + LANES, qw + 2 * kw + LANES + GMLP_CH
    if rope:
        zs = jnp.dot(hb, wsw_ref[...], preferred_element_type=F32)
        cos = c_ref[...]
        sin = s_ref[...]

    def head(col, zcol, g_ref, gs_ref):
        t = z[:, col:col + LANES]
        r = lax.rsqrt(jnp.sum(t * t, axis=-1, keepdims=True) * (1.0 / GQA_HEAD_DIM) + EPS)
        normed = t * r * g_ref[...]
        if not rope:
            return normed, normed
        ts = zs[:, zcol:zcol + LANES]
        return normed, normed * cos + ts * r * gs_ref[...] * sin

    for hd in range(GQA_HEADS):
        _, rot = head(hd * LANES, hd * LANES, qg_ref, qgs_ref if rope else None)
        q_out[0, :, hd * LANES:(hd + 1) * LANES] = rot.astype(BF16)
    for hd in range(GQA_KV_HEADS):
        normed, rot = head(o_k + hd * LANES, qw + hd * LANES, kg_ref, kgs_ref if rope else None)
        k_out[0, :, hd * LANES:(hd + 1) * LANES] = rot.astype(BF16)
        if kt_out is not None:
            seq = kt_out.shape[-1]
            for s in range(kt_out.shape[0]):
                kt_out[s, hd] = normed[s * seq:(s + 1) * seq, :].T[:GQA_HEAD_DIM, :]
    vp_out[0] = _with_sum_lane(z[:, o_vp:o_vp + kw]).astype(BF16)
    if vt_out is not None:
        seq = vt_out.shape[-1]
        for s in range(vt_out.shape[0]):
            vt_out[s] = z[s * seq:(s + 1) * seq, o_v:o_v + LANES].T

    u = z[:, o_u:o_u + GMLP_CH]
    vg = z[:, o_vg:o_vg + GMLP_CH]
    mu = jnp.mean(vg, axis=-1, keepdims=True)
    cen = vg - mu
    var = jnp.mean(cen * cen, axis=-1, keepdims=True)
    vn = (cen * lax.rsqrt(var + EPS) * lng_ref[...] + lnb_ref[...]).astype(BF16)
    bias = bs_ref[...]
    rows = x.shape[0]
    for cidx in range(rows // CHUNK):
        r0 = cidx * CHUNK
        for g in range(GMLP_GROUPS):
            c0 = g * LANES
            mixed = jnp.dot(ws_ref[g], vn[r0:r0 + CHUNK, c0:c0 + LANES], preferred_element_type=F32)
            g_out[0, r0:r0 + CHUNK, c0:c0 + LANES] = (
                u[r0:r0 + CHUNK, c0:c0 + LANES] * (mixed + bias[:, c0:c0 + LANES])).astype(BF16)


def l1_inproj(x, mods, mod_off, n1, w, tables, ctx_seq=None):
    bm, sm, d = x.shape
    tr = min(ROW_TILE, sm)
    rope = tables is not None
    assert rope != (ctx_seq is not None)
    qw = GQA_HEADS * LANES
    kw = GQA_KV_HEADS * LANES
    row = lambda width: pl.BlockSpec((1, tr, width), lambda b, i: (b, i, 0))
    vec = _const_spec((1, LANES))
    in_specs = [row(d), pl.BlockSpec((1, SUBLANES, d), lambda b, i: (b + mod_off, 0, 0)),
                _const_spec((1, d)), _const_spec(w["win"].shape)]
    args = [x, mods, n1, w["win"]]
    if rope:
        in_specs += [_const_spec(w["wsw"].shape), vec, vec, vec, vec]
        args += [w["wsw"], w["qg"], w["qgs"], w["kg"], w["kgs"]]
    else:
        in_specs += [vec, vec]
        args += [w["qg"], w["kg"]]
    in_specs += [_const_spec((1, GMLP_CH)), _const_spec((1, GMLP_CH)),
                 _const_spec(w["ws"].shape), _const_spec((CHUNK, GMLP_CH))]
    args += [w["lng"], w["lnb"], w["ws"], w["bs"]]
    if rope:
        in_specs += [pl.BlockSpec((tr, LANES), lambda b, i: (i, 0))] * 2
        args += list(tables)
    out_shape = [jax.ShapeDtypeStruct((bm, sm, qw), BF16),
                 jax.ShapeDtypeStruct((bm, sm, kw), BF16),
                 jax.ShapeDtypeStruct((bm, sm, kw), BF16),
                 jax.ShapeDtypeStruct((bm, sm, GMLP_CH), BF16)]
    out_specs = [row(qw), row(kw), row(kw), row(GMLP_CH)]
    if not rope:
        assert bm == 1 and tr % ctx_seq == 0
        n_seq, per_step = sm // ctx_seq, tr // ctx_seq
        out_shape += [jax.ShapeDtypeStruct((n_seq, GQA_KV_HEADS, GQA_HEAD_DIM, ctx_seq), F32),
                      jax.ShapeDtypeStruct((n_seq, GQA_KV_HEADS * GQA_HEAD_DIM, ctx_seq), F32)]
        out_specs += [pl.BlockSpec((per_step, GQA_KV_HEADS, GQA_HEAD_DIM, ctx_seq), lambda b, i: (i, 0, 0, 0)),
                      pl.BlockSpec((per_step, GQA_KV_HEADS * GQA_HEAD_DIM, ctx_seq), lambda b, i: (i, 0, 0))]
    return pl.pallas_call(
        functools.partial(_l1_inproj_kernel, rope=rope),
        out_shape=out_shape, grid=(bm, sm // tr), in_specs=in_specs, out_specs=out_specs,
        compiler_params=_params(("parallel", "parallel"), VMEM_LIMIT),
        name="l1_inproj_rope" if rope else "l1_inproj",
    )(*args)


def _post_kernel(a_ref, c_ref, x_ref, m_ref, n2_ref, wa_ref, wc_ref, rwh_ref, rwl_ref, rb_ref, tri_ref, base_ref,
                 x1_out, xm_out, idx_out, wts_out, rank_out, cnt_out, run_ref):
    first =(pl.program_id(0) == 0) & (pl.program_id(1) == 0)

    @pl.when(first)
    def _():
        run_ref[...] = base_ref[...]

    m = m_ref[0]
    y = (jnp.dot(a_ref[0], wa_ref[...], preferred_element_type=F32)
         + jnp.dot(c_ref[0], wc_ref[...], preferred_element_type=F32))
    x1 = x_ref[0] + m[2:3] * y
    x1_out[0] = x1
    xm = _rms(x1, n2_ref[...]) * (1.0 + m[4:5]) + m[3:4]
    xh = xm.astype(BF16)
    xm_out[0] = _pack_rows(xh)

    xl = (xm - xh.astype(F32)).astype(BF16)
    logits = (jnp.dot(xh, rwh_ref[...], preferred_element_type=F32)
              + jnp.dot(xl, rwh_ref[...], preferred_element_type=F32)
              + jnp.dot(xh, rwl_ref[...], preferred_element_type=F32)) + rb_ref[...]
    rows = logits.shape[0]
    lane = lax.broadcasted_iota(jnp.int32, (rows, LANES), 1).astype(F32)
    work = logits
    vals, hots = [], []
    idx_acc = jnp.zeros((rows, LANES), F32)
    for k in range(TOP_K):
        top = jnp.max(work, axis=-1, keepdims=True)
        sel = jnp.min(jnp.where(work == top, lane, float(LANES)), axis=-1, keepdims=True)
        hot = lane == sel
        vals.append(top)
        hots.append(hot)
        idx_acc = idx_acc + jnp.where(lane == float(k), sel, 0.0)
        work = jnp.where(hot, -jnp.inf, work)
    exps = [jnp.exp(v - vals[0]) for v in vals]
    denom = exps[0] + exps[1] + exps[2] + exps[3]
    wcols = [jnp.broadcast_to(exps[k] / denom, (rows, LANES)) for k in range(TOP_K)]
    wcols += [jnp.zeros((rows, LANES), F32)] * (SUBLANES - TOP_K)
    wts = jnp.concatenate(wcols, axis=1).reshape(rows, SUBLANES, LANES)

    chosen = jnp.zeros((rows, LANES), F32)
    for hot in hots:
        chosen = chosen + hot.astype(F32)
    before = jnp.dot(tri_ref[...], chosen.astype(BF16), preferred_element_type=F32) + run_ref[0:1, :]
    rank = jnp.zeros((rows, LANES), F32)
    for k in range(TOP_K):
        rk = jnp.sum(jnp.where(hots[k], before, 0.0), axis=-1, keepdims=True)
        rank = rank + jnp.where(lane == float(k), rk, 0.0)
    run_ref[0:1, :] = run_ref[0:1, :] + jnp.sum(chosen, axis=0, keepdims=True)
    idx_out[...] = idx_acc.T[:SUBLANES, :].astype(jnp.int32)
    wts_out[0] = wts
    rank_out[...] = rank.T[:SUBLANES, :].astype(jnp.int32)
    cnt_out[...] = run_ref[...]


def post_mixer(attn, other, x, mods, mod_off, n2, wa, wc, rwh, rwl, rb, tri, base):
    bm, sm, d = x.shape
    tr = tri.shape[0]
    row = lambda width: pl.BlockSpec((1, tr, width), lambda b, i: (b, i, 0))
    tile_rows = pl.BlockSpec((1, tr) + ROW_WORDS, lambda b, i: (b, i, 0, 0))
    nb = sm // tr
    token_minor = pl.BlockSpec((SUBLANES, tr), lambda b, i: (0, b * nb + i))
    out_shape = [jax.ShapeDtypeStruct((bm, sm, d), F32),
                 jax.ShapeDtypeStruct((bm, sm) + ROW_WORDS, jnp.int32),
                 jax.ShapeDtypeStruct((SUBLANES, bm * sm), jnp.int32),
                 jax.ShapeDtypeStruct((bm, sm, SUBLANES, LANES), F32),
                 jax.ShapeDtypeStruct((SUBLANES, bm * sm), jnp.int32),
                 jax.ShapeDtypeStruct((SUBLANES, LANES), F32)]
    return pl.pallas_call(
        _post_kernel,
        out_shape=out_shape, grid=(bm, sm // tr),
        in_specs=[row(attn.shape[-1]), row(other.shape[-1]), row(d),
                  pl.BlockSpec((1, SUBLANES, d), lambda b, i: (b + mod_off, 0, 0)),
                  _const_spec((1, d)), _const_spec(wa.shape), _const_spec(wc.shape),
                  _const_spec(rwh.shape), _const_spec(rwl.shape), _const_spec((1, LANES)),
                  _const_spec(tri.shape), _const_spec((SUBLANES, LANES))],
        out_specs=[row(d), tile_rows, token_minor,
                   pl.BlockSpec((1, tr, SUBLANES, LANES), lambda b, i: (b, i, 0, 0)),
                   token_minor, _const_spec((SUBLANES, LANES))],
        scratch_shapes=[pltpu.VMEM((SUBLANES, LANES), F32)],
        compiler_params=_params(("arbitrary", "arbitrary"), VMEM_LIMIT),
        name="post_mixer_route",
    )(attn, other, x, mods, n2, wa, wc, rwh, rwl, rb, tri, base)


def _plan_kernel(off_ref, idx_ref, rank_ref, pos_out):
    idx = idx_ref[...]
    pos = rank_ref[...]
    for e in range(N_EXPERTS):
        pos = pos + jnp.where(idx == e, off_ref[e], 0)
    pos_out[...] = pos


def plan_positions(offsets, idx, rank):
    n = idx.shape[1]
    tr = min(2048, n)
    spec = pl.BlockSpec((SUBLANES, tr), lambda i, off: (0, i))
    return pl.pallas_call(
        _plan_kernel,
        out_shape=jax.ShapeDtypeStruct((SUBLANES, n), jnp.int32),
        grid_spec=pltpu.PrefetchScalarGridSpec(
            num_scalar_prefetch=1, grid=(n // tr,), in_specs=[spec, spec], out_specs=spec),
        compiler_params=_params(("parallel",)),
        name="plan_positions",
    )(offsets, idx, rank)


def _sc_worker_id():
    return lax.axis_index("s") * SC_CORES + lax.axis_index("c")


def _sc_chunk_rows(tile, dtype):
    row_bytes = math.prod(tile) * jnp.dtype(dtype).itemsize
    return min(SC_CHUNK_BYTES // row_bytes, SC_MAX_INDICES)


def dispatch_rows(xms, posk, n_rows):
    n_total = sum(x.shape[0] for x in xms)
    tile, dtype = xms[0].shape[1:], xms[0].dtype
    chunk = _sc_chunk_rows(tile, dtype)
    starts, s0 = [], 0
    for x in xms:
        assert x.shape[0] % (SC_WORKERS * chunk) == 0
        starts.append(s0)
        s0 += x.shape[0]
    mesh = plsc.VectorSubcoreMesh(core_axis_name="c", subcore_axis_name="s")

    @functools.partial(
        pl.kernel, mesh=mesh, out_type=jax.ShapeDtypeStruct((n_rows,) + tile, dtype),
        scratch_types=[pltpu.VMEM((chunk,), jnp.int32), pltpu.VMEM((chunk,) + tile, dtype),
                       pltpu.SemaphoreType.DMA],
        name="dispatch_rows_sc")
    def scatter(*refs):
        x_refs, pos_hbm, xs_hbm, idx_v, rows_v, sem = refs[:len(xms)], *refs[len(xms):]
        wid = _sc_worker_id()
        for x_hbm, start in zip(x_refs, starts):
            per_worker = x_hbm.shape[0] // SC_WORKERS

            @pl.loop(0, per_worker // chunk)
            def _(c):
                t0 = wid * per_worker + c * chunk
                pltpu.sync_copy(x_hbm.at[pl.ds(t0, chunk)], rows_v)
                for k in range(TOP_K):
                    pltpu.sync_copy(pos_hbm.at[pl.ds(k * n_total + start + t0, chunk)], idx_v)
                    pltpu.async_copy(rows_v, xs_hbm.at[idx_v], sem).wait()

    return scatter(*xms, posk)


def gather_rows(y, posk):
    n_pairs = posk.shape[0]
    tile, dtype = y.shape[1:], y.dtype
    chunk = _sc_chunk_rows(tile, dtype)
    per_worker = n_pairs // SC_WORKERS
    assert per_worker % chunk == 0
    mesh = plsc.VectorSubcoreMesh(core_axis_name="c", subcore_axis_name="s")

    @functools.partial(
        pl.kernel, mesh=mesh, out_type=jax.ShapeDtypeStruct((n_pairs,) + tile, dtype),
        scratch_types=[pltpu.VMEM((chunk,), jnp.int32), pltpu.VMEM((chunk,) + tile, dtype),
                       pltpu.SemaphoreType.DMA],
        name="gather_rows_sc")
    def gather(y_hbm, pos_hbm, out_hbm, idx_v, rows_v, sem):
        wid = _sc_worker_id()

        @pl.loop(0, per_worker // chunk)
        def _(c):
            base = wid * per_worker + c * chunk
            pltpu.sync_copy(pos_hbm.at[pl.ds(base, chunk)], idx_v)
            pltpu.async_copy(y_hbm.at[idx_v], rows_v, sem).wait()
            pltpu.sync_copy(rows_v, out_hbm.at[pl.ds(base, chunk)])

    return gather(y, posk)


def _ffn_kernel(te_ref, nu_ref, first_ref, slot_ref, nxt_ref, xs_ref, w1_hbm, b1_ref, w2_hbm, b2_ref, y_ref,
                w1f, w2f, w1b, w2b, xb, yb, sem):
    j = pl.program_id(0)
    nu = nu_ref[0]
    t = jnp.maximum(j - 1, 0)

    def weight_copies(e, s):
        return (pltpu.make_async_copy(w1_hbm.at[e], w1f.at[s], sem.at[0, s]),
                pltpu.make_async_copy(w2_hbm.at[e], w2f.at[s], sem.at[1, s]))

    def expert_weights():
        s = slot_ref[t]

        @pl.when(first_ref[t] == 1)
        def _():
            @pl.when(t == 0)
            def _():
                for cp in weight_copies(te_ref[t], s):
                    cp.start()
            for cp in weight_copies(te_ref[t], s):
                cp.wait()

            @pl.when(nxt_ref[t] >= 0)
            def _():
                for cp in weight_copies(nxt_ref[t], 1 - s):
                    cp.start()
            for c in range(D_MODEL // LANES):
                w1b[c * LANES:(c + 1) * LANES, :] = w1f[s, c * LANES:(c + 1) * LANES, :].astype(BF16)
            for c in range(D_EXPERT // LANES):
                w2b[c * LANES:(c + 1) * LANES, :] = w2f[s, c * LANES:(c + 1) * LANES, :].astype(BF16)

    def unpack():
        xb[j % 2] = _unpack_rows(xs_ref[...])

    def matmuls():
        h = jnp.dot(xb[(j + 1) % 2], w1b[...], preferred_element_type=F32) + b1_ref[0]
        g = jnp.minimum(h[:, :D_EXPERT], SWIGLU_LIMIT)
        lin = jnp.clip(h[:, D_EXPERT:], -SWIGLU_LIMIT, SWIGLU_LIMIT)
        a = (lin + 1.0) * (g * jax.nn.sigmoid(SWIGLU_ALPHA * g))
        y = jnp.dot(a.astype(BF16), w2b[...], preferred_element_type=F32) + b2_ref[0]
        yb[(j + 1) % 2] = y.astype(BF16)

    def pack():
        y_ref[...] = _pack_rows(yb[j % 2])

    steady = (j >= 2) & (j < nu)

    @pl.when(steady)
    def _():
        expert_weights()
        unpack()
        matmuls()
        pack()

    @pl.when(jnp.logical_not(steady))
    def _():
        @pl.when(j < nu)
        def _():
            unpack()

        @pl.when((j >= 1) & (j <= nu))
        def _():
            expert_weights()
            matmuls()

        @pl.when((j >= 2) & (j - 2 < nu))
        def _():
            pack()

        @pl.when(j - 2 >= nu)
        def _():
            y_ref[...] = jnp.zeros(y_ref.shape, y_ref.dtype)


def grouped_ffn(sched, xs, w1, b1, w2, b2):
    r = xs.shape[0]
    d = w1.shape[1]
    nt = r // FFN_TILE
    tile = (FFN_TILE,) + xs.shape[1:]
    rows = lambda j, te, nu, *_: (jnp.minimum(j, nu[0] - 1), 0, 0)
    bsel = lambda j, te, *_: (te[jnp.clip(j - 1, 0, nt - 1)], 0, 0)
    return pl.pallas_call(
        _ffn_kernel,
        out_shape=jax.ShapeDtypeStruct(xs.shape, xs.dtype),
        grid_spec=pltpu.PrefetchScalarGridSpec(
            num_scalar_prefetch=5, grid=(nt + 2,),
            in_specs=[pl.BlockSpec(tile, rows),
                      pl.BlockSpec(memory_space=pl.ANY),
                      pl.BlockSpec((1, 1, 2 * D_EXPERT), bsel),
                      pl.BlockSpec(memory_space=pl.ANY),
                      pl.BlockSpec((1, 1, d), bsel)],
            out_specs=pl.BlockSpec(tile, lambda j, *_: (jnp.maximum(j - 2, 0), 0, 0)),
            scratch_shapes=[pltpu.VMEM((2, d, 2 * D_EXPERT), F32), pltpu.VMEM((2, D_EXPERT, d), F32),
                            pltpu.VMEM((d, 2 * D_EXPERT), BF16), pltpu.VMEM((D_EXPERT, d), BF16),
                            pltpu.VMEM((2, FFN_TILE, d), BF16), pltpu.VMEM((2, FFN_TILE, d), BF16),
                            pltpu.SemaphoreType.DMA((2, 2))]),
        compiler_params=_params(("arbitrary",), VMEM_LIMIT),
        name="grouped_ffn",
    )(*sched, xs, w1, b1.reshape(N_EXPERTS, 1, -1), w2, b2.reshape(N_EXPERTS, 1, -1))


def _combine_kernel(x1_ref, wts_ref, m_ref, fn_ref, y0_ref, y1_ref, y2_ref, y3_ref, o_ref, *, final):
    w = wts_ref[0]
    rows = lambda ref: pltpu.bitcast(ref[...], BF16).astype(F32)
    acc = w[:, 0:1, :] * rows(y0_ref)
    for k, y_ref in ((1, y1_ref), (2, y2_ref), (3, y3_ref)):
        acc = acc + w[:, k:k + 1, :] * rows(y_ref)
    out = x1_ref[0] + m_ref[0][5:6] * acc.reshape(x1_ref.shape[1], D_MODEL)
    if final:
        out = _rms(out, fn_ref[...])
    o_ref[0] = out


def combine_rows(x1, wts, mods, mod_off, fn, yg, row_off, n_total, *, final):
    bm, sm, d = x1.shape
    tr = min(COMBINE_TILE, sm)
    nb = sm // tr
    row = lambda width: pl.BlockSpec((1, tr, width), lambda b, i: (b, i, 0))
    ysel = lambda k: pl.BlockSpec((tr,) + yg.shape[1:],
                                  lambda b, i: ((k * n_total + row_off) // tr + b * nb + i, 0, 0))
    return pl.pallas_call(
        functools.partial(_combine_kernel, final=final),
        out_shape=jax.ShapeDtypeStruct((bm, sm, d), F32),
        grid=(bm, nb),
        in_specs=[row(d), pl.BlockSpec((1, tr, SUBLANES, LANES), lambda b, i: (b, i, 0, 0)),
                  pl.BlockSpec((1, SUBLANES, d), lambda b, i: (b + mod_off, 0, 0)),
                  _const_spec((1, d))] + [ysel(k) for k in range(TOP_K)],
        out_specs=row(d),
        compiler_params=_params(("parallel", "parallel"), VMEM_LIMIT),
        name="combine_rows",
    )(x1, wts, mods, fn, yg, yg, yg, yg)


def _axial_angles(n_tokens, rot_dim):
    t = jnp.arange(n_tokens)
    rows = (t // GRID_W).astype(F32)
    cols = (t % GRID_W).astype(F32)
    n_freq = rot_dim // 4
    inv = ROPE_THETA ** (-jnp.arange(n_freq, dtype=F32) / n_freq)
    return jnp.concatenate([rows[:, None] * inv, cols[:, None] * inv], axis=-1)


def _lane_table(parts, n):
    cols = []
    for p in parts:
        cols.append(jnp.broadcast_to(jnp.asarray(p, F32), (n, p.shape[-1])) if hasattr(p, "shape") else p)
    return jnp.concatenate(cols, axis=-1)


def _swap_halves(w):
    half = w.shape[-1] // 2
    return jnp.concatenate([-w[..., half:], w[..., :half]], axis=-1)


def _prep_l0(w_in, q_norm, kv_norm, w_uq, w_uk, w_uv, w_out):
    d = w_in.shape[0]
    o_kr = MLA_Q_RANK + MLA_KV_RANK
    kr_cols = w_in[:, o_kr:o_kr + MLA_ROPE]
    win = jnp.concatenate(
        [w_in[:, :o_kr], kr_cols, _swap_halves(kr_cols), jnp.zeros((d, LANES - 2 * MLA_ROPE), F32),
         w_in[:, o_kr + MLA_ROPE:]], axis=1).astype(BF16)
    qk = MLA_NOPE + MLA_ROPE
    wuq3 = w_uq.reshape(MLA_Q_RANK, MLA_HEADS, qk)
    wuq = jnp.pad(wuq3, ((0, 0), (0, 0), (0, LANES - qk))).reshape(MLA_Q_RANK, -1).astype(BF16)
    wuqs3 = jnp.concatenate(
        [jnp.zeros((MLA_Q_RANK, MLA_HEADS, MLA_NOPE), F32), _swap_halves(wuq3[:, :, MLA_NOPE:]),
         jnp.zeros((MLA_Q_RANK, MLA_HEADS, LANES - qk), F32)], axis=-1)
    wuqs = wuqs3.reshape(MLA_Q_RANK, -1).astype(BF16)
    wuk3 = w_uk.reshape(MLA_KV_RANK, MLA_HEADS, MLA_NOPE)
    wuk = jnp.pad(wuk3, ((0, 0), (0, 0), (0, LANES - MLA_NOPE))).reshape(MLA_KV_RANK, -1).astype(BF16)
    wuv3 = w_uv.reshape(MLA_KV_RANK, MLA_HEADS, MLA_V)
    wuv = jnp.pad(wuv3, ((0, 0), (0, 0), (0, LANES - MLA_V))).reshape(MLA_KV_RANK, -1).astype(BF16)
    eye = jnp.eye(MLA_ROPE, dtype=F32)
    e_head = jnp.concatenate([jnp.zeros((MLA_ROPE, MLA_NOPE), F32), eye,
                              jnp.zeros((MLA_ROPE, LANES - qk), F32)], axis=1)
    e = jnp.pad(jnp.tile(e_head, (1, MLA_HEADS)), ((0, LANES - MLA_ROPE), (0, 0))).astype(BF16)
    wa3 = w_out[:MLA_HEADS * MLA_V].reshape(MLA_HEADS, MLA_V, d)
    wa = jnp.pad(wa3, ((0, 0), (0, LANES - MLA_V), (0, 0))).reshape(MLA_HEADS * LANES, d).astype(BF16)
    wc = w_out[MLA_HEADS * MLA_V:].astype(BF16)
    return dict(win=win, qg=q_norm.reshape(1, -1), kvg=kv_norm.reshape(1, -1), wuq=wuq, wuqs=wuqs,
                wuk=wuk, e=e, wuv=wuv, wa=wa, wc=wc)


def _l0_tables(n):
    ang = _axial_angles(n, MLA_ROPE)
    cos, sin = jnp.cos(ang), jnp.sin(ang)
    one = jnp.ones((n, 1), F32)
    zero = jnp.zeros((n, 1), F32)
    rest = LANES - MLA_NOPE - MLA_ROPE
    cq = jnp.concatenate([jnp.tile(one, (1, MLA_NOPE)), cos, cos, jnp.tile(one, (1, rest))], axis=1)
    sq = jnp.concatenate([jnp.tile(zero, (1, MLA_NOPE)), sin, sin, jnp.tile(zero, (1, rest))], axis=1)
    ck = jnp.concatenate([cos, cos, jnp.tile(zero, (1, LANES - MLA_ROPE))], axis=1)
    sk = jnp.concatenate([sin, sin, jnp.tile(zero, (1, LANES - MLA_ROPE))], axis=1)
    return cq, sq, ck, sk


def _pad_heads(w, n_heads, dim):
    d = w.shape[0]
    return jnp.pad(w.reshape(d, n_heads, dim), ((0, 0), (0, 0), (0, LANES - dim))).reshape(d, n_heads * LANES)


def _prep_l1(w_in, q_norm, k_norm, ln_g, ln_b, w_s, b_s, w_out):
    d = w_in.shape[0]
    qd = GQA_HEADS * GQA_HEAD_DIM
    kd = GQA_KV_HEADS * GQA_HEAD_DIM
    wq, wk, wv = w_in[:, :qd], w_in[:, qd:qd + kd], w_in[:, qd + kd:qd + 2 * kd]
    rest = w_in[:, qd + 2 * kd:]
    win = jnp.concatenate([_pad_heads(wq, GQA_HEADS, GQA_HEAD_DIM), _pad_heads(wk, GQA_KV_HEADS, GQA_HEAD_DIM),
                           _pad_heads(wv, GQA_KV_HEADS, GQA_HEAD_DIM), wv, rest], axis=1).astype(BF16)
    swq = _swap_halves(wq.reshape(d, GQA_HEADS, GQA_HEAD_DIM)).reshape(d, qd)
    swk = _swap_halves(wk.reshape(d, GQA_KV_HEADS, GQA_HEAD_DIM)).reshape(d, kd)
    wsw = jnp.concatenate([_pad_heads(swq, GQA_HEADS, GQA_HEAD_DIM),
                           _pad_heads(swk, GQA_KV_HEADS, GQA_HEAD_DIM)], axis=1).astype(BF16)
    half = GQA_HEAD_DIM // 2
    padg = lambda g: jnp.pad(g, (0, LANES - GQA_HEAD_DIM)).reshape(1, LANES)
    swapg = lambda g: jnp.concatenate([g[half:], g[:half]])
    wa3 = w_out[:qd].reshape(GQA_HEADS, GQA_HEAD_DIM, d)
    wa = jnp.pad(wa3, ((0, 0), (0, LANES - GQA_HEAD_DIM), (0, 0))).reshape(GQA_HEADS * LANES, d).astype(BF16)
    wc = w_out[qd:].astype(BF16)
    bs = jnp.repeat(b_s.T, LANES, axis=1)
    return dict(win=win, wsw=wsw, qg=padg(q_norm), qgs=padg(swapg(q_norm)), kg=padg(k_norm),
                kgs=padg(swapg(k_norm)), lng=ln_g.reshape(1, -1), lnb=ln_b.reshape(1, -1),
                ws=w_s.astype(BF16), bs=bs, wa=wa, wc=wc)


def _l1_tables(n):
    ang = _axial_angles(n, GQA_HEAD_DIM)
    cos, sin = jnp.cos(ang), jnp.sin(ang)
    pad = LANES - GQA_HEAD_DIM
    c = jnp.concatenate([cos, cos, jnp.ones((n, pad), F32)], axis=1)
    s = jnp.concatenate([sin, sin, jnp.zeros((n, pad), F32)], axis=1)
    return c, s


def _pad_lanes(x, width):
    return jnp.pad(x, [(0, 0)] * (x.ndim - 1) + [(0, width - x.shape[-1])])


def routed_ffn(groups, mods, n2, moe, tri, final_norm, *, final):
    router_w, router_b, w1, b1, w2, b2 = moe
    rw = _pad_lanes(router_w, LANES)
    rwh = rw.astype(BF16)
    rwl = (rw - rwh.astype(F32)).astype(BF16)
    rb = jnp.concatenate([router_b, jnp.full((LANES - N_EXPERTS,), NEG_BIG, F32)]).reshape(1, LANES)
    base = jnp.zeros((SUBLANES, LANES), F32)
    routed = []
    for g in groups:
        x1, xm, idx, wts, rank, base = post_mixer(
            g["attn"], g["other"], g["x"], mods, g["mod_off"], n2, g["wa"], g["wc"], rwh, rwl, rb, tri, base)
        routed.append((x1, xm, idx, wts, rank))
    n_total = sum(r[0].shape[0] * r[0].shape[1] for r in routed)
    nt = n_total * TOP_K // FFN_TILE + N_EXPERTS

    counts = base[0, :N_EXPERTS].astype(jnp.int32)
    tiles = (counts + FFN_TILE - 1) // FFN_TILE
    tile_end = jnp.cumsum(tiles)
    offsets = (tile_end - tiles) * FFN_TILE
    n_used = tile_end[-1:].astype(jnp.int32)
    experts = jnp.arange(N_EXPERTS)
    busy = tiles > 0
    slot_e = (jnp.cumsum(busy) - 1) % 2
    later = jnp.where(busy[None, :] & (experts[None, :] > experts[:, None]), experts[None, :], N_EXPERTS)
    nxt_e = jnp.min(later, axis=1)
    nxt_e = jnp.where(nxt_e == N_EXPERTS, -1, nxt_e)
    tile_expert = jnp.sum(jnp.arange(nt)[:, None] >= tile_end[None, :], axis=1)
    tile_expert = jnp.minimum(tile_expert, jnp.max(jnp.where(busy, experts, 0)))
    first = jnp.concatenate([jnp.ones((1,), bool), tile_expert[1:] != tile_expert[:-1]])
    of_tile = tile_expert[:, None] == experts[None, :]
    slot_t = jnp.sum(jnp.where(of_tile, slot_e[None, :], 0), axis=1)
    nxt_t = jnp.sum(jnp.where(of_tile, nxt_e[None, :], 0), axis=1)
    sched = tuple(a.astype(jnp.int32) for a in (tile_expert, n_used, first, slot_t, nxt_t))

    positions, xms = [], []
    for (x1, xm, idx, wts, rank) in routed:
        positions.append(plan_positions(offsets, idx, rank)[:TOP_K])
        xms.append(xm.reshape((idx.shape[1],) + ROW_WORDS))
    posk = jnp.concatenate(positions, axis=1).reshape(-1)
    xs = dispatch_rows(xms, posk, nt * FFN_TILE)
    y = grouped_ffn(sched, xs, w1, b1, w2, b2)
    outs = []
    for g, (x1, xm, idx, wts, rank), pos in zip(groups, routed, positions):
        yg = gather_rows(y, pos.reshape(-1))
        outs.append(combine_rows(x1, wts, mods, g["mod_off"], final_norm, yg, 0, pos.shape[1], final=final))
    return outs


def kernel(x_prompt, x_sample, cache_l0_ckv, cache_l0_krope, cache_l1_k, cache_l1_v, c, c_ctx,
           l0_ada_w, l0_ada_b, l0_norm1, l0_w_in, l0_q_norm, l0_kv_norm, l0_w_uq, l0_w_uk, l0_w_uv,
           l0_conv_w, l0_conv_b, l0_conv_ln_g, l0_conv_ln_b, l0_w_out, l0_norm2,
           l0_router_w, l0_router_b, l0_w1, l0_b1, l0_w2, l0_b2,
           l1_ada_w, l1_ada_b, l1_norm1, l1_w_in, l1_q_norm, l1_k_norm, l1_gmlp_ln_g, l1_gmlp_ln_b,
           l1_w_s, l1_b_s, l1_w_out, l1_norm2,
           l1_router_w, l1_router_b, l1_w1, l1_b1, l1_w2, l1_b2,
           final_norm):
    bp, sp, d = x_prompt.shape
    bs, ss, _ = x_sample.shape
    past = cache_l0_ckv.shape[1]
    n_p = bp * sp

    cond8 = jnp.concatenate([c_ctx[None], c, jnp.zeros((SUBLANES - 1 - bs, d), F32)], axis=0)
    mods0 = adaln(cond8, l0_ada_w, l0_ada_b)
    mods1 = adaln(cond8, l1_ada_w, l1_ada_b)
    tri = jnp.tril(jnp.ones((ROW_TILE, ROW_TILE), F32), -1).astype(BF16)
    fn = final_norm.reshape(1, d)

    w0 = _prep_l0(l0_w_in, l0_q_norm, l0_kv_norm, l0_w_uq, l0_w_uk, l0_w_uv, l0_w_out)
    n1 = l0_norm1.reshape(1, d)
    hp = x_prompt.reshape(1, n_p, d)
    q_p, k_p, v_p, ckv_p, kr_p, u_p = l0_inproj(hp, mods0, 0, n1, w0, None)
    q_s, k_s, v_s, _, _, u_s = l0_inproj(x_sample, mods0, 1, n1, w0, _l0_tables(ss))
    k_c, v_c = mla_ctx_kv(cache_l0_ckv, _pad_lanes(cache_l0_krope, LANES), w0)
    hw = MLA_HEADS * LANES
    att_p = attention(q_p.reshape(bp, sp, hw), k_p.reshape(bp, sp, hw), v_p.reshape(bp, sp, hw),
                      n_heads=MLA_HEADS, n_kv=MLA_HEADS, scale=MLA_SCALE, heads_per_step=MLA_HEADS)
    att_s = attention(q_s, jnp.concatenate([k_c, k_s], axis=1), jnp.concatenate([v_c, v_s], axis=1),
                      n_heads=MLA_HEADS, n_kv=MLA_HEADS, scale=MLA_SCALE, heads_per_step=LATENT_HEADS_PER_STEP)
    conv_p = conformer_conv(u_p.reshape(bp, sp, CONV_CH), l0_conv_w, l0_conv_b, l0_conv_ln_g, l0_conv_ln_b)
    conv_s = conformer_conv(u_s, l0_conv_w, l0_conv_b, l0_conv_ln_g, l0_conv_ln_b)
    groups = [dict(attn=att_p.reshape(1, n_p, hw), other=conv_p.reshape(1, n_p, CONV_CH), x=hp, mod_off=0,
                   wa=w0["wa"], wc=w0["wc"]),
              dict(attn=att_s, other=conv_s, x=x_sample, mod_off=1, wa=w0["wa"], wc=w0["wc"])]
    hp, hs = routed_ffn(groups, mods0, l0_norm2.reshape(1, d),
                        (l0_router_w, l0_router_b, l0_w1, l0_b1, l0_w2, l0_b2), tri, fn, final=False)
    new_l0_ckv = ckv_p.reshape(bp, sp, MLA_KV_RANK)
    new_l0_krope = kr_p.reshape(bp, sp, MLA_ROPE)

    w1p = _prep_l1(l1_w_in, l1_q_norm, l1_k_norm, l1_gmlp_ln_g, l1_gmlp_ln_b, l1_w_s, l1_b_s, l1_w_out)
    n1 = l1_norm1.reshape(1, d)
    q_p, k_p, vp_p, gat_p, kt_p, vt_p = l1_inproj(hp, mods1, 0, n1, w1p, None, ctx_seq=sp)
    q_s, k_s, vp_s, gat_s = l1_inproj(hs, mods1, 1, n1, w1p, _l1_tables(ss))
    qw = GQA_HEADS * LANES
    kw = GQA_KV_HEADS * LANES
    pad_kv = lambda t: _pad_lanes(t, LANES).reshape(bs, past, kw).astype(BF16)
    att_p = attention(q_p.reshape(bp, sp, qw), k_p.reshape(bp, sp, kw), vp_p.reshape(bp, sp, kw),
                      n_heads=GQA_HEADS, n_kv=GQA_KV_HEADS, scale=GQA_SCALE, heads_per_step=GQA_HEADS)
    att_s = attention(q_s, jnp.concatenate([pad_kv(cache_l1_k), k_s], axis=1),
                      jnp.concatenate([_with_sum_lane(pad_kv(cache_l1_v)), vp_s], axis=1),
                      n_heads=GQA_HEADS, n_kv=GQA_KV_HEADS, scale=GQA_SCALE, heads_per_step=LATENT_HEADS_PER_STEP)
    groups = [dict(attn=att_p.reshape(1, n_p, qw), other=gat_p, x=hp, mod_off=0, wa=w1p["wa"], wc=w1p["wc"]),
              dict(attn=att_s, other=gat_s, x=hs, mod_off=1, wa=w1p["wa"], wc=w1p["wc"])]
    yp, ys = routed_ffn(groups, mods1, l1_norm2.reshape(1, d),
                        (l1_router_w, l1_router_b, l1_w1, l1_b1, l1_w2, l1_b2), tri, fn, final=True)
    new_l1_k = jnp.transpose(kt_p, (0, 3, 1, 2))
    new_l1_v = jnp.transpose(vt_p.reshape(bp, GQA_KV_HEADS, GQA_HEAD_DIM, sp), (0, 3, 1, 2))
    return (yp.reshape(bp, sp, d), ys, new_l0_ckv, new_l0_krope, new_l1_k, new_l1_v)
```

```python
import functools
import math

import jax
import jax.numpy as jnp
from jax import lax
from jax.experimental import pallas as pl
from jax.experimental.pallas import tpu as pltpu
from jax.experimental.pallas import tpu_sc as plsc

F32 = jnp.float32
BF16 = jnp.bfloat16
HIGHEST = lax.Precision.HIGHEST

LANES = 128
SUBLANES = 8
VMEM_LIMIT = 56 * 1024 * 1024

D_MODEL = 1024
GRID_W = 64
ROPE_THETA = 10000.0
EPS = 1e-6
N_MOD = 6

MLA_HEADS = 8
MLA_NOPE = 64
MLA_ROPE = 32
MLA_V = 64
MLA_Q_RANK = 384
MLA_KV_RANK = 256
MLA_SCALE = 1.0 / math.sqrt(MLA_NOPE + MLA_ROPE)
CONV_CH = 512
CONV_WIDTH = 31
CONV_HALO = 16

GQA_HEADS = 8
GQA_KV_HEADS = 2
GQA_HEAD_DIM = 64
GQA_SCALE = 1.0 / math.sqrt(GQA_HEAD_DIM)
CHUNK = 128
GMLP_GROUPS = 4
GMLP_CH = 512

N_EXPERTS = 32
TOP_K = 4
D_EXPERT = 1024
SWIGLU_LIMIT = 7.0
SWIGLU_ALPHA = 1.702

ROW_TILE = 512
FFN_TILE = 256
SC_CORES = 2
SC_SUBCORES = 16
SC_WORKERS = SC_CORES * SC_SUBCORES
SC_CHUNK_BYTES = 256 * 1024
SC_MAX_INDICES = 128
COMBINE_TILE = 256
ATT_Q_TILE = 256
ATT_LONG_KEYS = 1024
V_SUM_LANE = 64
LATENT_HEADS_PER_STEP = 4
NEG_BIG = -1e30


def _params(sem, vmem=None):
    return pltpu.CompilerParams(dimension_semantics=sem, vmem_limit_bytes=vmem)


def _rms(x, g):
    return x * lax.rsqrt(jnp.mean(x * x, axis=-1, keepdims=True) + EPS) * g


ROW_WORDS = (SUBLANES // 2, LANES)


def _pack_rows(x_bf16):
    return pltpu.bitcast(x_bf16.reshape(x_bf16.shape[0], SUBLANES, LANES), jnp.int32)


def _unpack_rows(words):
    return pltpu.bitcast(words, BF16).reshape(words.shape[0], D_MODEL)


def _const_spec(shape):
    nd = len(shape)
    return pl.BlockSpec(shape, lambda *_: (0,) * nd)


def _adaln_kernel(c_ref, w_ref, b_ref, o_ref):
    c = c_ref[...]
    s = c * jax.nn.sigmoid(c)
    o_ref[...] = jnp.dot(s, w_ref[...], preferred_element_type=F32, precision=HIGHEST) + b_ref[...]


def adaln(cond8, ada_w, ada_b):
    d, n = ada_w.shape
    bn = n // 4
    m = pl.pallas_call(
        _adaln_kernel,
        out_shape=jax.ShapeDtypeStruct((SUBLANES, n), F32),
        grid=(n // bn,),
        in_specs=[_const_spec((SUBLANES, d)),
                  pl.BlockSpec((d, bn), lambda j: (0, j)),
                  pl.BlockSpec((1, bn), lambda j: (0, j))],
        out_specs=pl.BlockSpec((SUBLANES, bn), lambda j: (0, j)),
        compiler_params=_params(("arbitrary",), VMEM_LIMIT),
        name="adaln",
    )(cond8, ada_w, ada_b.reshape(1, n))
    m = m.reshape(SUBLANES, N_MOD, d)
    return jnp.pad(m, ((0, 0), (0, SUBLANES - N_MOD), (0, 0)))


def _l0_inproj_kernel(*refs, rope):
    if rope:
        (x_ref, m_ref, n1_ref, win_ref, qg_ref, kvg_ref, wuq_ref, wuqs_ref, wuk_ref, e_ref, wuv_ref,
         cq_ref, sq_ref, ck_ref, sk_ref, q_out, k_out, v_out, ckv_out, kr_out, u_out) = refs
    else:
        (x_ref, m_ref, n1_ref, win_ref, qg_ref, kvg_ref, wuq_ref, wuk_ref, e_ref, wuv_ref,
         q_out, k_out, v_out, ckv_out, kr_out, u_out) = refs
    x = x_ref[0]
    m = m_ref[0]
    h = _rms(x, n1_ref[...]) * (1.0 + m[1:2]) + m[0:1]
    z = jnp.dot(h.astype(BF16), win_ref[...], preferred_element_type=F32)
    c_q = z[:, 0:MLA_Q_RANK]
    c_kv = z[:, MLA_Q_RANK:MLA_Q_RANK + MLA_KV_RANK]
    kr_blk = z[:, 640:768]
    val = z[:, 768:768 + CONV_CH]
    gate = z[:, 768 + CONV_CH:768 + 2 * CONV_CH]

    cqn = _rms(c_q, qg_ref[...]).astype(BF16)
    q = jnp.dot(cqn, wuq_ref[...], preferred_element_type=F32)
    if rope:
        qs = jnp.dot(cqn, wuqs_ref[...], preferred_element_type=F32)
        cq = cq_ref[...]
        sq = sq_ref[...]
        for hd in range(MLA_HEADS):
            sl = slice(hd * LANES, (hd + 1) * LANES)
            q_out[0, :, sl] = ((q[:, sl] * cq + qs[:, sl] * sq) * MLA_SCALE).astype(BF16)
        kr = kr_blk * ck_ref[...] + pltpu.roll(kr_blk, LANES - MLA_ROPE, 1) * sk_ref[...]
    else:
        q_out[0] = (q * MLA_SCALE).astype(BF16)
        kr = kr_blk

    ckv = _rms(c_kv, kvg_ref[...])
    ckv_out[0] = ckv
    kr_out[0] = kr_blk[:, 0:MLA_ROPE]
    ckv_b = ckv.astype(BF16)
    k = (jnp.dot(ckv_b, wuk_ref[...], preferred_element_type=F32)
         + jnp.dot(kr.astype(BF16), e_ref[...], preferred_element_type=F32))
    k_out[0] = k.astype(BF16)
    v_out[0] = _with_sum_lane(jnp.dot(ckv_b, wuv_ref[...], preferred_element_type=F32)).astype(BF16)
    u_out[0] = val * jax.nn.sigmoid(gate)


def l0_inproj(x, mods, mod_off, n1, w, tables):
    bm, sm, d = x.shape
    tr = min(ROW_TILE, sm)
    rope = tables is not None
    hp = MLA_HEADS * LANES
    row = lambda width: pl.BlockSpec((1, tr, width), lambda b, i: (b, i, 0))
    in_specs = [row(d),
                pl.BlockSpec((1, SUBLANES, d), lambda b, i: (b + mod_off, 0, 0)),
                _const_spec((1, d)), _const_spec(w["win"].shape),
                _const_spec((1, MLA_Q_RANK)), _const_spec((1, MLA_KV_RANK)),
                _const_spec(w["wuq"].shape)]
    args = [x, mods, n1, w["win"], w["qg"], w["kvg"], w["wuq"]]
    if rope:
        in_specs.append(_const_spec(w["wuqs"].shape))
        args.append(w["wuqs"])
    in_specs += [_const_spec(w["wuk"].shape), _const_spec(w["e"].shape), _const_spec(w["wuv"].shape)]
    args += [w["wuk"], w["e"], w["wuv"]]
    if rope:
        in_specs += [pl.BlockSpec((tr, LANES), lambda b, i: (i, 0))] * 4
        args += list(tables)
    out_shape = [jax.ShapeDtypeStruct((bm, sm, hp), BF16),
                 jax.ShapeDtypeStruct((bm, sm, hp), BF16),
                 jax.ShapeDtypeStruct((bm, sm, hp), BF16),
                 jax.ShapeDtypeStruct((bm, sm, MLA_KV_RANK), F32),
                 jax.ShapeDtypeStruct((bm, sm, MLA_ROPE), F32),
                 jax.ShapeDtypeStruct((bm, sm, CONV_CH), F32)]
    out_specs = [row(hp), row(hp), row(hp), row(MLA_KV_RANK), row(MLA_ROPE), row(CONV_CH)]
    return pl.pallas_call(
        functools.partial(_l0_inproj_kernel, rope=rope),
        out_shape=out_shape, grid=(bm, sm // tr), in_specs=in_specs, out_specs=out_specs,
        compiler_params=_params(("parallel", "parallel"), VMEM_LIMIT),
        name="l0_inproj_rope" if rope else "l0_inproj",
    )(*args)


def _mla_ctx_kv_kernel(ckv_ref, kr_ref, wuk_ref, e_ref, wuv_ref, k_out, v_out):
    ckv_b = ckv_ref[0].astype(BF16)
    k = (jnp.dot(ckv_b, wuk_ref[...], preferred_element_type=F32)
         + jnp.dot(kr_ref[0].astype(BF16), e_ref[...], preferred_element_type=F32))
    k_out[0] = k.astype(BF16)
    v_out[0] = _with_sum_lane(jnp.dot(ckv_b, wuv_ref[...], preferred_element_type=F32)).astype(BF16)


def mla_ctx_kv(ckv, kr128, w):
    b, s, _ = ckv.shape
    hp = MLA_HEADS * LANES
    blk = lambda width: pl.BlockSpec((1, s, width), lambda i: (i, 0, 0))
    return pl.pallas_call(
        _mla_ctx_kv_kernel,
        out_shape=[jax.ShapeDtypeStruct((b, s, hp), BF16)] * 2,
        grid=(b,),
        in_specs=[blk(MLA_KV_RANK), blk(LANES), _const_spec(w["wuk"].shape),
                  _const_spec(w["e"].shape), _const_spec(w["wuv"].shape)],
        out_specs=[blk(hp), blk(hp)],
        compiler_params=_params(("parallel",), VMEM_LIMIT),
        name="mla_ctx_kv",
    )(ckv, kr128, w["wuk"], w["e"], w["wuv"])


def _conv_kernel(prev_ref, cur_ref, next_ref, w_ref, b_ref, g_ref, beta_ref, o_ref, pad_ref, sh_ref, *, rb):
    i = pl.program_id(1)
    last = pl.num_programs(1) - 1
    zeros = jnp.zeros((CONV_HALO, CONV_CH), F32)
    pad_ref[0:CONV_HALO, :] = jnp.where(i == 0, zeros, prev_ref[0])
    pad_ref[CONV_HALO:CONV_HALO + rb, :] = cur_ref[0]
    pad_ref[CONV_HALO + rb:CONV_HALO + rb + CONV_HALO, :] = jnp.where(i == last, zeros, next_ref[0])
    span = rb + 2 * CONV_HALO - SUBLANES
    for r in range(1, SUBLANES):
        sh_ref[r] = pad_ref[r:r + span, :]
    w = w_ref[...]
    shift = CONV_HALO - CONV_WIDTH // 2
    acc = jnp.zeros((rb, CONV_CH), F32) + b_ref[...]
    for k in range(CONV_WIDTH):
        off = k + shift
        r, a = off % SUBLANES, off // SUBLANES * SUBLANES
        window = pad_ref[a:a + rb, :] if r == 0 else sh_ref[r, a:a + rb, :]
        acc = acc + window * w[k:k + 1, :]
    mu = jnp.mean(acc, axis=-1, keepdims=True)
    cen = acc - mu
    var = jnp.mean(cen * cen, axis=-1, keepdims=True)
    y = cen * lax.rsqrt(var + EPS) * g_ref[...] + beta_ref[...]
    o_ref[0] = (y * jax.nn.sigmoid(y)).astype(BF16)


def conformer_conv(u, conv_w, conv_b, ln_g, ln_b):
    b, s, c = u.shape
    rb = min(256, s)
    nh = rb // CONV_HALO
    n_halo_blocks = s // CONV_HALO
    wpad = jnp.pad(conv_w.reshape(CONV_WIDTH, c), ((0, 32 - CONV_WIDTH), (0, 0)))
    return pl.pallas_call(
        functools.partial(_conv_kernel, rb=rb),
        out_shape=jax.ShapeDtypeStruct((b, s, c), BF16),
        grid=(b, s // rb),
        in_specs=[pl.BlockSpec((1, CONV_HALO, c), lambda bi, i: (bi, jnp.maximum(i * nh - 1, 0), 0)),
                  pl.BlockSpec((1, rb, c), lambda bi, i: (bi, i, 0)),
                  pl.BlockSpec((1, CONV_HALO, c),
                               lambda bi, i: (bi, jnp.minimum((i + 1) * nh, n_halo_blocks - 1), 0)),
                  _const_spec((32, c)), _const_spec((1, c)), _const_spec((1, c)), _const_spec((1, c))],
        out_specs=pl.BlockSpec((1, rb, c), lambda bi, i: (bi, i, 0)),
        scratch_shapes=[pltpu.VMEM((rb + 2 * CONV_HALO, c), F32),
                        pltpu.VMEM((SUBLANES, rb + 2 * CONV_HALO - SUBLANES, c), F32)],
        compiler_params=_params(("parallel", "parallel"), VMEM_LIMIT),
        name="conformer_conv",
    )(u, u, u, wpad, conv_b.reshape(1, c), ln_g.reshape(1, c), ln_b.reshape(1, c))


def _with_sum_lane(v):
    lane = lax.broadcasted_iota(jnp.int32, (1, v.shape[-1]), 1)
    return v + ((lane & (LANES - 1)) == V_SUM_LANE).astype(v.dtype)


def _attn_kernel(q_ref, k_ref, v_ref, o_ref, *, heads, rep, mxu_denominator):
    for hd in range(heads):
        g = hd // rep
        q = q_ref[0, :, hd * LANES:(hd + 1) * LANES]
        k = k_ref[0, :, g * LANES:(g + 1) * LANES]
        s = lax.dot_general(q, k, (((1,), (1,)), ((), ())), preferred_element_type=F32)
        m = jnp.max(s, axis=-1, keepdims=True)
        if mxu_denominator:
            p = jnp.exp((s - m).astype(BF16))
            o = jnp.dot(p, v_ref[0, :, g * LANES:(g + 1) * LANES], preferred_element_type=F32)
            l = o[:, V_SUM_LANE:V_SUM_LANE + 1]
        else:
            p = jnp.exp(s - m)
            l = jnp.sum(p, axis=-1, keepdims=True)
            o = jnp.dot(p.astype(BF16), v_ref[0, :, g * LANES:(g + 1) * LANES], preferred_element_type=F32)
        o_ref[0, :, hd * LANES:(hd + 1) * LANES] = (o / l).astype(BF16)


def attention(q, k, v, *, n_heads, n_kv, heads_per_step):
    b, sq, _ = q.shape
    sk = k.shape[1]
    rep = n_heads // n_kv
    tq = min(ATT_Q_TILE, sq)
    hb = heads_per_step
    grid = (b, n_heads // hb, sq // tq)
    if hb >= rep:
        kv_spec = pl.BlockSpec((1, sk, hb // rep * LANES), lambda bi, h, i: (bi, 0, h))
        kern_rep = rep
    else:
        assert rep % hb == 0
        kv_spec = pl.BlockSpec((1, sk, LANES), lambda bi, h, i: (bi, 0, h * hb // rep))
        kern_rep = hb
    kern = functools.partial(_attn_kernel, heads=hb, rep=kern_rep, mxu_denominator=sk >= ATT_LONG_KEYS)
    q_spec = pl.BlockSpec((1, tq, heads_per_step * LANES), lambda bi, h, i: (bi, i, h))
    return pl.pallas_call(
        kern,
        out_shape=jax.ShapeDtypeStruct(q.shape, BF16),
        grid=grid, in_specs=[q_spec, kv_spec, kv_spec], out_specs=q_spec,
        compiler_params=_params(("parallel", "parallel", "parallel"), VMEM_LIMIT),
        name="attention",
    )(q, k, v)


def _l1_inproj_kernel(*refs, rope):
    if rope:
        (x_ref, m_ref, n1_ref, win_ref, wsw_ref, qg_ref, qgs_ref, kg_ref, kgs_ref, lng_ref, lnb_ref,
         ws_ref, bs_ref, c_ref, s_ref, q_out, k_out, vp_out, g_out) = refs
        kt_out = vt_out = None
    else:
        (x_ref, m_ref, n1_ref, win_ref, qg_ref, kg_ref, lng_ref, lnb_ref,
         ws_ref, bs_ref, q_out, k_out, vp_out, g_out, kt_out, vt_out) = refs
    x = x_ref[0]
    m = m_ref[0]
    hb = (_rms(x, n1_ref[...]) * (1.0 + m[1:2]) + m[0:1]).astype(BF16)
    z = jnp.dot(hb, win_ref[...], preferred_element_type=F32)
    qw = GQA_HEADS * LANES
    kw = GQA_KV_HEADS * LANES
    o_k, o_vp, o_v, o_u, o_vg = qw, qw + kw, qw + 2 * kw, qw + 2 * kw + LANES, qw + 2 * kw + LANES + GMLP_CH
    if rope:
        zs = jnp.dot(hb, wsw_ref[...], preferred_element_type=F32)
        cos = c_ref[...]
        sin = s_ref[...]

    def head(col, zcol, g_ref, gs_ref):
        t = z[:, col:col + LANES]
        r = lax.rsqrt(jnp.sum(t * t, axis=-1, keepdims=True) * (1.0 / GQA_HEAD_DIM) + EPS)
        normed = t * r * g_ref[...]
        if not rope:
            return normed, normed
        ts = zs[:, zcol:zcol + LANES]
        return normed, normed * cos + ts * r * gs_ref[...] * sin

    for hd in range(GQA_HEADS):
        _, rot = head(hd * LANES, hd * LANES, qg_ref, qgs_ref if rope else None)
        q_out[0, :, hd * LANES:(hd + 1) * LANES] = (rot * GQA_SCALE).astype(BF16)
    for hd in range(GQA_KV_HEADS):
        normed, rot = head(o_k + hd * LANES, qw + hd * LANES, kg_ref, kgs_ref if rope else None)
        k_out[0, :, hd * LANES:(hd + 1) * LANES] = rot.astype(BF16)
        if kt_out is not None:
            seq = kt_out.shape[-1]
            for s in range(kt_out.shape[0]):
                kt_out[s, hd] = normed[s * seq:(s + 1) * seq, :].T[:GQA_HEAD_DIM, :]
    vp_out[0] = _with_sum_lane(z[:, o_vp:o_vp + kw]).astype(BF16)
    if vt_out is not None:
        seq = vt_out.shape[-1]
        for s in range(vt_out.shape[0]):
            vt_out[s] = z[s * seq:(s + 1) * seq, o_v:o_v + LANES].T

    u = z[:, o_u:o_u + GMLP_CH]
    vg = z[:, o_vg:o_vg + GMLP_CH]
    mu = jnp.mean(vg, axis=-1, keepdims=True)
    cen = vg - mu
    var = jnp.mean(cen * cen, axis=-1, keepdims=True)
    vn = (cen * lax.rsqrt(var + EPS) * lng_ref[...] + lnb_ref[...]).astype(BF16)
    bias = bs_ref[...]
    rows = x.shape[0]
    for cidx in range(rows // CHUNK):
        r0 = cidx * CHUNK
        for g in range(GMLP_GROUPS):
            c0 = g * LANES
            mixed = jnp.dot(ws_ref[g], vn[r0:r0 + CHUNK, c0:c0 + LANES], preferred_element_type=F32)
            g_out[0, r0:r0 + CHUNK, c0:c0 + LANES] = (
                u[r0:r0 + CHUNK, c0:c0 + LANES] * (mixed + bias[:, c0:c0 + LANES])).astype(BF16)


def l1_inproj(x, mods, mod_off, n1, w, tables, ctx_seq=None):
    bm, sm, d = x.shape
    tr = min(ROW_TILE, sm)
    rope = tables is not None
    assert rope != (ctx_seq is not None)
    qw = GQA_HEADS * LANES
    kw = GQA_KV_HEADS * LANES
    row = lambda width: pl.BlockSpec((1, tr, width), lambda b, i: (b, i, 0))
    vec = _const_spec((1, LANES))
    in_specs = [row(d), pl.BlockSpec((1, SUBLANES, d), lambda b, i: (b + mod_off, 0, 0)),
                _const_spec((1, d)), _const_spec(w["win"].shape)]
    args = [x, mods, n1, w["win"]]
    if rope:
        in_specs += [_const_spec(w["wsw"].shape), vec, vec, vec, vec]
        args += [w["wsw"], w["qg"], w["qgs"], w["kg"], w["kgs"]]
    else:
        in_specs += [vec, vec]
        args += [w["qg"], w["kg"]]
    in_specs += [_const_spec((1, GMLP_CH)), _const_spec((1, GMLP_CH)),
                 _const_spec(w["ws"].shape), _const_spec((CHUNK, GMLP_CH))]
    args += [w["lng"], w["lnb"], w["ws"], w["bs"]]
    if rope:
        in_specs += [pl.BlockSpec((tr, LANES), lambda b, i: (i, 0))] * 2
        args += list(tables)
    out_shape = [jax.ShapeDtypeStruct((bm, sm, qw), BF16),
                 jax.ShapeDtypeStruct((bm, sm, kw), BF16),
                 jax.ShapeDtypeStruct((bm, sm, kw), BF16),
                 jax.ShapeDtypeStruct((bm, sm, GMLP_CH), BF16)]
    out_specs = [row(qw), row(kw), row(kw), row(GMLP_CH)]
    if not rope:
        assert bm == 1 and tr % ctx_seq == 0
        n_seq, per_step = sm // ctx_seq, tr // ctx_seq
        out_shape += [jax.ShapeDtypeStruct((n_seq, GQA_KV_HEADS, GQA_HEAD_DIM, ctx_seq), F32),
                      jax.ShapeDtypeStruct((n_seq, GQA_KV_HEADS * GQA_HEAD_DIM, ctx_seq), F32)]
        out_specs += [pl.BlockSpec((per_step, GQA_KV_HEADS, GQA_HEAD_DIM, ctx_seq), lambda b, i: (i, 0, 0, 0)),
                      pl.BlockSpec((per_step, GQA_KV_HEADS * GQA_HEAD_DIM, ctx_seq), lambda b, i: (i, 0, 0))]
    return pl.pallas_call(
        functools.partial(_l1_inproj_kernel, rope=rope),
        out_shape=out_shape, grid=(bm, sm // tr), in_specs=in_specs, out_specs=out_specs,
        compiler_params=_params(("parallel", "parallel"), VMEM_LIMIT),
        name="l1_inproj_rope" if rope else "l1_inproj",
    )(*args)


def _post_kernel(a_ref, c_ref, x_ref, m_ref, n2_ref, wa_ref, wc_ref, rwh_ref, rwl_ref, rb_ref, tri_ref, base_ref,
                 x1_out, xm_out, idx_out, wts_out, rank_out, cnt_out, run_ref):
    first =(pl.program_id(0) == 0) & (pl.program_id(1) == 0)

    @pl.when(first)
    def _():
        run_ref[...] = base_ref[...]

    m = m_ref[0]
    y = (jnp.dot(a_ref[0], wa_ref[...], preferred_element_type=F32)
         + jnp.dot(c_ref[0], wc_ref[...], preferred_element_type=F32))
    x1 = x_ref[0] + m[2:3] * y
    x1_out[0] = x1
    xm = _rms(x1, n2_ref[...]) * (1.0 + m[4:5]) + m[3:4]
    xh = xm.astype(BF16)
    xm_out[0] = _pack_rows(xh)

    xl = (xm - xh.astype(F32)).astype(BF16)
    logits = (jnp.dot(xh, rwh_ref[...], preferred_element_type=F32)
              + jnp.dot(xl, rwh_ref[...], preferred_element_type=F32)
              + jnp.dot(xh, rwl_ref[...], preferred_element_type=F32)) + rb_ref[...]
    rows = logits.shape[0]
    lane = lax.broadcasted_iota(jnp.int32, (rows, LANES), 1).astype(F32)
    work = logits
    vals, hots = [], []
    idx_acc = jnp.zeros((rows, LANES), F32)
    for k in range(TOP_K):
        top = jnp.max(work, axis=-1, keepdims=True)
        sel = jnp.min(jnp.where(work == top, lane, float(LANES)), axis=-1, keepdims=True)
        hot = lane == sel
        vals.append(top)
        hots.append(hot)
        idx_acc = idx_acc + jnp.where(lane == float(k), sel, 0.0)
        work = jnp.where(hot, -jnp.inf, work)
    exps = [jnp.exp(v - vals[0]) for v in vals]
    denom = exps[0] + exps[1] + exps[2] + exps[3]
    wcols = [jnp.broadcast_to(exps[k] / denom, (rows, LANES)) for k in range(TOP_K)]
    wcols += [jnp.zeros((rows, LANES), F32)] * (SUBLANES - TOP_K)
    wts = jnp.concatenate(wcols, axis=1).reshape(rows, SUBLANES, LANES)

    chosen = jnp.zeros((rows, LANES), F32)
    for hot in hots:
        chosen = chosen + hot.astype(F32)
    before = jnp.dot(tri_ref[...], chosen.astype(BF16), preferred_element_type=F32) + run_ref[0:1, :]
    rank = jnp.zeros((rows, LANES), F32)
    for k in range(TOP_K):
        rk = jnp.sum(jnp.where(hots[k], before, 0.0), axis=-1, keepdims=True)
        rank = rank + jnp.where(lane == float(k), rk, 0.0)
    run_ref[0:1, :] = run_ref[0:1, :] + jnp.sum(chosen, axis=0, keepdims=True)
    idx_out[...] = idx_acc.T[:SUBLANES, :].astype(jnp.int32)
    wts_out[0] = wts
    rank_out[...] = rank.T[:SUBLANES, :].astype(jnp.int32)
    cnt_out[...] = run_ref[...]


def post_mixer(attn, other, x, mods, mod_off, n2, wa, wc, rwh, rwl, rb, tri, base):
    bm, sm, d = x.shape
    tr = tri.shape[0]
    row = lambda width: pl.BlockSpec((1, tr, width), lambda b, i: (b, i, 0))
    tile_rows = pl.BlockSpec((1, tr) + ROW_WORDS, lambda b, i: (b, i, 0, 0))
    nb = sm // tr
    token_minor = pl.BlockSpec((SUBLANES, tr), lambda b, i: (0, b * nb + i))
    out_shape = [jax.ShapeDtypeStruct((bm, sm, d), F32),
                 jax.ShapeDtypeStruct((bm, sm) + ROW_WORDS, jnp.int32),
                 jax.ShapeDtypeStruct((SUBLANES, bm * sm), jnp.int32),
                 jax.ShapeDtypeStruct((bm, sm, SUBLANES, LANES), F32),
                 jax.ShapeDtypeStruct((SUBLANES, bm * sm), jnp.int32),
                 jax.ShapeDtypeStruct((SUBLANES, LANES), F32)]
    return pl.pallas_call(
        _post_kernel,
        out_shape=out_shape, grid=(bm, sm // tr),
        in_specs=[row(attn.shape[-1]), row(other.shape[-1]), row(d),
                  pl.BlockSpec((1, SUBLANES, d), lambda b, i: (b + mod_off, 0, 0)),
                  _const_spec((1, d)), _const_spec(wa.shape), _const_spec(wc.shape),
                  _const_spec(rwh.shape), _const_spec(rwl.shape), _const_spec((1, LANES)),
                  _const_spec(tri.shape), _const_spec((SUBLANES, LANES))],
        out_specs=[row(d), tile_rows, token_minor,
                   pl.BlockSpec((1, tr, SUBLANES, LANES), lambda b, i: (b, i, 0, 0)),
                   token_minor, _const_spec((SUBLANES, LANES))],
        scratch_shapes=[pltpu.VMEM((SUBLANES, LANES), F32)],
        compiler_params=_params(("arbitrary", "arbitrary"), VMEM_LIMIT),
        name="post_mixer_route",
    )(attn, other, x, mods, n2, wa, wc, rwh, rwl, rb, tri, base)


def _plan_kernel(off_ref, idx_ref, rank_ref, pos_out):
    idx = idx_ref[...]
    pos = rank_ref[...]
    for e in range(N_EXPERTS):
        pos = pos + jnp.where(idx == e, off_ref[e], 0)
    pos_out[...] = pos


def plan_positions(offsets, idx, rank):
    n = idx.shape[1]
    tr = min(2048, n)
    spec = pl.BlockSpec((SUBLANES, tr), lambda i, off: (0, i))
    return pl.pallas_call(
        _plan_kernel,
        out_shape=jax.ShapeDtypeStruct((SUBLANES, n), jnp.int32),
        grid_spec=pltpu.PrefetchScalarGridSpec(
            num_scalar_prefetch=1, grid=(n // tr,), in_specs=[spec, spec], out_specs=spec),
        compiler_params=_params(("parallel",)),
        name="plan_positions",
    )(offsets, idx, rank)


def _sc_worker_id():
    return lax.axis_index("s") * SC_CORES + lax.axis_index("c")


def _sc_chunk_rows(tile, dtype):
    row_bytes = math.prod(tile) * jnp.dtype(dtype).itemsize
    return min(SC_CHUNK_BYTES // row_bytes, SC_MAX_INDICES)


def dispatch_rows(xms, posk, n_rows):
    n_total = sum(x.shape[0] for x in xms)
    tile, dtype = xms[0].shape[1:], xms[0].dtype
    chunk = _sc_chunk_rows(tile, dtype)
    starts, s0 = [], 0
    for x in xms:
        assert x.shape[0] % (SC_WORKERS * chunk) == 0
        starts.append(s0)
        s0 += x.shape[0]
    mesh = plsc.VectorSubcoreMesh(core_axis_name="c", subcore_axis_name="s")

    @functools.partial(
        pl.kernel, mesh=mesh, out_type=jax.ShapeDtypeStruct((n_rows,) + tile, dtype),
        scratch_types=[pltpu.VMEM((chunk,), jnp.int32), pltpu.VMEM((chunk,) + tile, dtype),
                       pltpu.SemaphoreType.DMA],
        name="dispatch_rows_sc")
    def scatter(*refs):
        x_refs, pos_hbm, xs_hbm, idx_v, rows_v, sem = refs[:len(xms)], *refs[len(xms):]
        wid = _sc_worker_id()
        for x_hbm, start in zip(x_refs, starts):
            per_worker = x_hbm.shape[0] // SC_WORKERS

            @pl.loop(0, per_worker // chunk)
            def _(c):
                t0 = wid * per_worker + c * chunk
                pltpu.sync_copy(x_hbm.at[pl.ds(t0, chunk)], rows_v)
                for k in range(TOP_K):
                    pltpu.sync_copy(pos_hbm.at[pl.ds(k * n_total + start + t0, chunk)], idx_v)
                    pltpu.async_copy(rows_v, xs_hbm.at[idx_v], sem).wait()

    return scatter(*xms, posk)


def gather_rows(y, posk):
    n_pairs = posk.shape[0]
    tile, dtype = y.shape[1:], y.dtype
    chunk = _sc_chunk_rows(tile, dtype)
    per_worker = n_pairs // SC_WORKERS
    assert per_worker % chunk == 0
    mesh = plsc.VectorSubcoreMesh(core_axis_name="c", subcore_axis_name="s")

    @functools.partial(
        pl.kernel, mesh=mesh, out_type=jax.ShapeDtypeStruct((n_pairs,) + tile, dtype),
        scratch_types=[pltpu.VMEM((chunk,), jnp.int32), pltpu.VMEM((chunk,) + tile, dtype),
                       pltpu.SemaphoreType.DMA],
        name="gather_rows_sc")
    def gather(y_hbm, pos_hbm, out_hbm, idx_v, rows_v, sem):
        wid = _sc_worker_id()

        @pl.loop(0, per_worker // chunk)
        def _(c):
            base = wid * per_worker + c * chunk
            pltpu.sync_copy(pos_hbm.at[pl.ds(base, chunk)], idx_v)
            pltpu.async_copy(y_hbm.at[idx_v], rows_v, sem).wait()
            pltpu.sync_copy(rows_v, out_hbm.at[pl.ds(base, chunk)])

    return gather(y, posk)


def _ffn_kernel(te_ref, nu_ref, first_ref, slot_ref, nxt_ref, xs_ref, w1_hbm, b1_ref, w2_hbm, b2_ref, y_ref,
                w1f, w2f, w1b, w2b, sem):
    i = pl.program_id(0)

    def weight_copies(e, s):
        return (pltpu.make_async_copy(w1_hbm.at[e], w1f.at[s], sem.at[0, s]),
                pltpu.make_async_copy(w2_hbm.at[e], w2f.at[s], sem.at[1, s]))

    @pl.when(i < nu_ref[0])
    def _():
        s = slot_ref[i]

        @pl.when(first_ref[i] == 1)
        def _():
            @pl.when(i == 0)
            def _():
                for cp in weight_copies(te_ref[i], s):
                    cp.start()
            for cp in weight_copies(te_ref[i], s):
                cp.wait()

            @pl.when(nxt_ref[i] >= 0)
            def _():
                for cp in weight_copies(nxt_ref[i], 1 - s):
                    cp.start()
            for c in range(D_MODEL // LANES):
                w1b[c * LANES:(c + 1) * LANES, :] = w1f[s, c * LANES:(c + 1) * LANES, :].astype(BF16)
            for c in range(D_EXPERT // LANES):
                w2b[c * LANES:(c + 1) * LANES, :] = w2f[s, c * LANES:(c + 1) * LANES, :].astype(BF16)

        x = _unpack_rows(xs_ref[...])
        h = jnp.dot(x, w1b[...], preferred_element_type=F32) + b1_ref[0]
        g = jnp.minimum(h[:, :D_EXPERT], SWIGLU_LIMIT)
        lin = jnp.clip(h[:, D_EXPERT:], -SWIGLU_LIMIT, SWIGLU_LIMIT)
        a = (lin + 1.0) * (g * jax.nn.sigmoid(SWIGLU_ALPHA * g))
        y = jnp.dot(a.astype(BF16), w2b[...], preferred_element_type=F32) + b2_ref[0]
        y_ref[...] = _pack_rows(y.astype(BF16))

    @pl.when(i >= nu_ref[0])
    def _():
        y_ref[...] = jnp.zeros(y_ref.shape, y_ref.dtype)


def grouped_ffn(sched, xs, w1, b1, w2, b2):
    r = xs.shape[0]
    d = w1.shape[1]
    nt = r // FFN_TILE
    tile = (FFN_TILE,) + xs.shape[1:]
    rows = lambda i, te, nu, *_: (jnp.minimum(i, nu[0] - 1), 0, 0)
    bsel = lambda i, te, *_: (te[i], 0, 0)
    return pl.pallas_call(
        _ffn_kernel,
        out_shape=jax.ShapeDtypeStruct(xs.shape, xs.dtype),
        grid_spec=pltpu.PrefetchScalarGridSpec(
            num_scalar_prefetch=5, grid=(nt,),
            in_specs=[pl.BlockSpec(tile, rows),
                      pl.BlockSpec(memory_space=pl.ANY),
                      pl.BlockSpec((1, 1, 2 * D_EXPERT), bsel),
                      pl.BlockSpec(memory_space=pl.ANY),
                      pl.BlockSpec((1, 1, d), bsel)],
            out_specs=pl.BlockSpec(tile, lambda i, *_: (i, 0, 0)),
            scratch_shapes=[pltpu.VMEM((2, d, 2 * D_EXPERT), F32), pltpu.VMEM((2, D_EXPERT, d), F32),
                            pltpu.VMEM((d, 2 * D_EXPERT), BF16), pltpu.VMEM((D_EXPERT, d), BF16),
                            pltpu.SemaphoreType.DMA((2, 2))]),
        compiler_params=_params(("arbitrary",), VMEM_LIMIT),
        name="grouped_ffn",
    )(*sched, xs, w1, b1.reshape(N_EXPERTS, 1, -1), w2, b2.reshape(N_EXPERTS, 1, -1))


def _combine_kernel(x1_ref, wts_ref, m_ref, fn_ref, y0_ref, y1_ref, y2_ref, y3_ref, o_ref, *, final):
    w = wts_ref[0]
    rows = lambda ref: pltpu.bitcast(ref[...], BF16).astype(F32)
    acc = w[:, 0:1, :] * rows(y0_ref)
    for k, y_ref in ((1, y1_ref), (2, y2_ref), (3, y3_ref)):
        acc = acc + w[:, k:k + 1, :] * rows(y_ref)
    out = x1_ref[0] + m_ref[0][5:6] * acc.reshape(x1_ref.shape[1], D_MODEL)
    if final:
        out = _rms(out, fn_ref[...])
    o_ref[0] = out


def combine_rows(x1, wts, mods, mod_off, fn, yg, row_off, n_total, *, final):
    bm, sm, d = x1.shape
    tr = min(COMBINE_TILE, sm)
    nb = sm // tr
    row = lambda width: pl.BlockSpec((1, tr, width), lambda b, i: (b, i, 0))
    ysel = lambda k: pl.BlockSpec((tr,) + yg.shape[1:],
                                  lambda b, i: ((k * n_total + row_off) // tr + b * nb + i, 0, 0))
    return pl.pallas_call(
        functools.partial(_combine_kernel, final=final),
        out_shape=jax.ShapeDtypeStruct((bm, sm, d), F32),
        grid=(bm, nb),
        in_specs=[row(d), pl.BlockSpec((1, tr, SUBLANES, LANES), lambda b, i: (b, i, 0, 0)),
                  pl.BlockSpec((1, SUBLANES, d), lambda b, i: (b + mod_off, 0, 0)),
                  _const_spec((1, d))] + [ysel(k) for k in range(TOP_K)],
        out_specs=row(d),
        compiler_params=_params(("parallel", "parallel"), VMEM_LIMIT),
        name="combine_rows",
    )(x1, wts, mods, fn, yg, yg, yg, yg)


def _axial_angles(n_tokens, rot_dim):
    t = jnp.arange(n_tokens)
    rows = (t // GRID_W).astype(F32)
    cols = (t % GRID_W).astype(F32)
    n_freq = rot_dim // 4
    inv = ROPE_THETA ** (-jnp.arange(n_freq, dtype=F32) / n_freq)
    return jnp.concatenate([rows[:, None] * inv, cols[:, None] * inv], axis=-1)


def _lane_table(parts, n):
    cols = []
    for p in parts:
        cols.append(jnp.broadcast_to(jnp.asarray(p, F32), (n, p.shape[-1])) if hasattr(p, "shape") else p)
    return jnp.concatenate(cols, axis=-1)


def _swap_halves(w):
    half = w.shape[-1] // 2
    return jnp.concatenate([-w[..., half:], w[..., :half]], axis=-1)


def _prep_l0(w_in, q_norm, kv_norm, w_uq, w_uk, w_uv, w_out):
    d = w_in.shape[0]
    o_kr = MLA_Q_RANK + MLA_KV_RANK
    kr_cols = w_in[:, o_kr:o_kr + MLA_ROPE]
    win = jnp.concatenate(
        [w_in[:, :o_kr], kr_cols, _swap_halves(kr_cols), jnp.zeros((d, LANES - 2 * MLA_ROPE), F32),
         w_in[:, o_kr + MLA_ROPE:]], axis=1).astype(BF16)
    qk = MLA_NOPE + MLA_ROPE
    wuq3 = w_uq.reshape(MLA_Q_RANK, MLA_HEADS, qk)
    wuq = jnp.pad(wuq3, ((0, 0), (0, 0), (0, LANES - qk))).reshape(MLA_Q_RANK, -1).astype(BF16)
    wuqs3 = jnp.concatenate(
        [jnp.zeros((MLA_Q_RANK, MLA_HEADS, MLA_NOPE), F32), _swap_halves(wuq3[:, :, MLA_NOPE:]),
         jnp.zeros((MLA_Q_RANK, MLA_HEADS, LANES - qk), F32)], axis=-1)
    wuqs = wuqs3.reshape(MLA_Q_RANK, -1).astype(BF16)
    wuk3 = w_uk.reshape(MLA_KV_RANK, MLA_HEADS, MLA_NOPE)
    wuk = jnp.pad(wuk3, ((0, 0), (0, 0), (0, LANES - MLA_NOPE))).reshape(MLA_KV_RANK, -1).astype(BF16)
    wuv3 = w_uv.reshape(MLA_KV_RANK, MLA_HEADS, MLA_V)
    wuv = jnp.pad(wuv3, ((0, 0), (0, 0), (0, LANES - MLA_V))).reshape(MLA_KV_RANK, -1).astype(BF16)
    eye = jnp.eye(MLA_ROPE, dtype=F32)
    e_head = jnp.concatenate([jnp.zeros((MLA_ROPE, MLA_NOPE), F32), eye,
                              jnp.zeros((MLA_ROPE, LANES - qk), F32)], axis=1)
    e = jnp.pad(jnp.tile(e_head, (1, MLA_HEADS)), ((0, LANES - MLA_ROPE), (0, 0))).astype(BF16)
    wa3 = w_out[:MLA_HEADS * MLA_V].reshape(MLA_HEADS, MLA_V, d)
    wa = jnp.pad(wa3, ((0, 0), (0, LANES - MLA_V), (0, 0))).reshape(MLA_HEADS * LANES, d).astype(BF16)
    wc = w_out[MLA_HEADS * MLA_V:].astype(BF16)
    return dict(win=win, qg=q_norm.reshape(1, -1), kvg=kv_norm.reshape(1, -1), wuq=wuq, wuqs=wuqs,
                wuk=wuk, e=e, wuv=wuv, wa=wa, wc=wc)


def _l0_tables(n):
    ang = _axial_angles(n, MLA_ROPE)
    cos, sin = jnp.cos(ang), jnp.sin(ang)
    one = jnp.ones((n, 1), F32)
    zero = jnp.zeros((n, 1), F32)
    rest = LANES - MLA_NOPE - MLA_ROPE
    cq = jnp.concatenate([jnp.tile(one, (1, MLA_NOPE)), cos, cos, jnp.tile(one, (1, rest))], axis=1)
    sq = jnp.concatenate([jnp.tile(zero, (1, MLA_NOPE)), sin, sin, jnp.tile(zero, (1, rest))], axis=1)
    ck = jnp.concatenate([cos, cos, jnp.tile(zero, (1, LANES - MLA_ROPE))], axis=1)
    sk = jnp.concatenate([sin, sin, jnp.tile(zero, (1, LANES - MLA_ROPE))], axis=1)
    return cq, sq, ck, sk


def _pad_heads(w, n_heads, dim):
    d = w.shape[0]
    return jnp.pad(w.reshape(d, n_heads, dim), ((0, 0), (0, 0), (0, LANES - dim))).reshape(d, n_heads * LANES)


def _prep_l1(w_in, q_norm, k_norm, ln_g, ln_b, w_s, b_s, w_out):
    d = w_in.shape[0]
    qd = GQA_HEADS * GQA_HEAD_DIM
    kd = GQA_KV_HEADS * GQA_HEAD_DIM
    wq, wk, wv = w_in[:, :qd], w_in[:, qd:qd + kd], w_in[:, qd + kd:qd + 2 * kd]
    rest = w_in[:, qd + 2 * kd:]
    win = jnp.concatenate([_pad_heads(wq, GQA_HEADS, GQA_HEAD_DIM), _pad_heads(wk, GQA_KV_HEADS, GQA_HEAD_DIM),
                           _pad_heads(wv, GQA_KV_HEADS, GQA_HEAD_DIM), wv, rest], axis=1).astype(BF16)
    swq = _swap_halves(wq.reshape(d, GQA_HEADS, GQA_HEAD_DIM)).reshape(d, qd)
    swk = _swap_halves(wk.reshape(d, GQA_KV_HEADS, GQA_HEAD_DIM)).reshape(d, kd)
    wsw = jnp.concatenate([_pad_heads(swq, GQA_HEADS, GQA_HEAD_DIM),
                           _pad_heads(swk, GQA_KV_HEADS, GQA_HEAD_DIM)], axis=1).astype(BF16)
    half = GQA_HEAD_DIM // 2
    padg = lambda g: jnp.pad(g, (0, LANES - GQA_HEAD_DIM)).reshape(1, LANES)
    swapg = lambda g: jnp.concatenate([g[half:], g[:half]])
    wa3 = w_out[:qd].reshape(GQA_HEADS, GQA_HEAD_DIM, d)
    wa = jnp.pad(wa3, ((0, 0), (0, LANES - GQA_HEAD_DIM), (0, 0))).reshape(GQA_HEADS * LANES, d).astype(BF16)
    wc = w_out[qd:].astype(BF16)
    bs = jnp.repeat(b_s.T, LANES, axis=1)
    return dict(win=win, wsw=wsw, qg=padg(q_norm), qgs=padg(swapg(q_norm)), kg=padg(k_norm),
                kgs=padg(swapg(k_norm)), lng=ln_g.reshape(1, -1), lnb=ln_b.reshape(1, -1),
                ws=w_s.astype(BF16), bs=bs, wa=wa, wc=wc)


def _l1_tables(n):
    ang = _axial_angles(n, GQA_HEAD_DIM)
    cos, sin = jnp.cos(ang), jnp.sin(ang)
    pad = LANES - GQA_HEAD_DIM
    c = jnp.concatenate([cos, cos, jnp.ones((n, pad), F32)], axis=1)
    s = jnp.concatenate([sin, sin, jnp.zeros((n, pad), F32)], axis=1)
    return c, s


def _pad_lanes(x, width):
    return jnp.pad(x, [(0, 0)] * (x.ndim - 1) + [(0, width - x.shape[-1])])


def routed_ffn(groups, mods, n2, moe, tri, final_norm, *, final):
    router_w, router_b, w1, b1, w2, b2 = moe
    rw = _pad_lanes(router_w, LANES)
    rwh = rw.astype(BF16)
    rwl = (rw - rwh.astype(F32)).astype(BF16)
    rb = jnp.concatenate([router_b, jnp.full((LANES - N_EXPERTS,), NEG_BIG, F32)]).reshape(1, LANES)
    base = jnp.zeros((SUBLANES, LANES), F32)
    routed = []
    for g in groups:
        x1, xm, idx, wts, rank, base = post_mixer(
            g["attn"], g["other"], g["x"], mods, g["mod_off"], n2, g["wa"], g["wc"], rwh, rwl, rb, tri, base)
        routed.append((x1, xm, idx, wts, rank))
    n_total = sum(r[0].shape[0] * r[0].shape[1] for r in routed)
    nt = n_total * TOP_K // FFN_TILE + N_EXPERTS

    counts = base[0, :N_EXPERTS].astype(jnp.int32)
    tiles = (counts + FFN_TILE - 1) // FFN_TILE
    tile_end = jnp.cumsum(tiles)
    offsets = (tile_end - tiles) * FFN_TILE
    n_used = tile_end[-1:].astype(jnp.int32)
    experts = jnp.arange(N_EXPERTS)
    busy = tiles > 0
    slot_e = (jnp.cumsum(busy) - 1) % 2
    later = jnp.where(busy[None, :] & (experts[None, :] > experts[:, None]), experts[None, :], N_EXPERTS)
    nxt_e = jnp.min(later, axis=1)
    nxt_e = jnp.where(nxt_e == N_EXPERTS, -1, nxt_e)
    tile_expert = jnp.sum(jnp.arange(nt)[:, None] >= tile_end[None, :], axis=1)
    tile_expert = jnp.minimum(tile_expert, jnp.max(jnp.where(busy, experts, 0)))
    first = jnp.concatenate([jnp.ones((1,), bool), tile_expert[1:] != tile_expert[:-1]])
    of_tile = tile_expert[:, None] == experts[None, :]
    slot_t = jnp.sum(jnp.where(of_tile, slot_e[None, :], 0), axis=1)
    nxt_t = jnp.sum(jnp.where(of_tile, nxt_e[None, :], 0), axis=1)
    sched = tuple(a.astype(jnp.int32) for a in (tile_expert, n_used, first, slot_t, nxt_t))

    positions, xms = [], []
    for (x1, xm, idx, wts, rank) in routed:
        positions.append(plan_positions(offsets, idx, rank)[:TOP_K])
        xms.append(xm.reshape((idx.shape[1],) + ROW_WORDS))
    posk = jnp.concatenate(positions, axis=1).reshape(-1)
    xs = dispatch_rows(xms, posk, nt * FFN_TILE)
    y = grouped_ffn(sched, xs, w1, b1, w2, b2)
    outs = []
    for g, (x1, xm, idx, wts, rank), pos in zip(groups, routed, positions):
        yg = gather_rows(y, pos.reshape(-1))
        outs.append(combine_rows(x1, wts, mods, g["mod_off"], final_norm, yg, 0, pos.shape[1], final=final))
    return outs


def kernel(x_prompt, x_sample, cache_l0_ckv, cache_l0_krope, cache_l1_k, cache_l1_v, c, c_ctx,
           l0_ada_w, l0_ada_b, l0_norm1, l0_w_in, l0_q_norm, l0_kv_norm, l0_w_uq, l0_w_uk, l0_w_uv,
           l0_conv_w, l0_conv_b, l0_conv_ln_g, l0_conv_ln_b, l0_w_out, l0_norm2,
           l0_router_w, l0_router_b, l0_w1, l0_b1, l0_w2, l0_b2,
           l1_ada_w, l1_ada_b, l1_norm1, l1_w_in, l1_q_norm, l1_k_norm, l1_gmlp_ln_g, l1_gmlp_ln_b,
           l1_w_s, l1_b_s, l1_w_out, l1_norm2,
           l1_router_w, l1_router_b, l1_w1, l1_b1, l1_w2, l1_b2,
           final_norm):
    bp, sp, d = x_prompt.shape
    bs, ss, _ = x_sample.shape
    past = cache_l0_ckv.shape[1]
    n_p = bp * sp

    cond8 = jnp.concatenate([c_ctx[None], c, jnp.zeros((SUBLANES - 1 - bs, d), F32)], axis=0)
    mods0 = adaln(cond8, l0_ada_w, l0_ada_b)
    mods1 = adaln(cond8, l1_ada_w, l1_ada_b)
    tri = jnp.tril(jnp.ones((ROW_TILE, ROW_TILE), F32), -1).astype(BF16)
    fn = final_norm.reshape(1, d)

    w0 = _prep_l0(l0_w_in, l0_q_norm, l0_kv_norm, l0_w_uq, l0_w_uk, l0_w_uv, l0_w_out)
    n1 = l0_norm1.reshape(1, d)
    hp = x_prompt.reshape(1, n_p, d)
    q_p, k_p, v_p, ckv_p, kr_p, u_p = l0_inproj(hp, mods0, 0, n1, w0, None)
    q_s, k_s, v_s, _, _, u_s = l0_inproj(x_sample, mods0, 1, n1, w0, _l0_tables(ss))
    k_c, v_c = mla_ctx_kv(cache_l0_ckv, _pad_lanes(cache_l0_krope, LANES), w0)
    hw = MLA_HEADS * LANES
    att_p = attention(q_p.reshape(bp, sp, hw), k_p.reshape(bp, sp, hw), v_p.reshape(bp, sp, hw),
                      n_heads=MLA_HEADS, n_kv=MLA_HEADS, heads_per_step=MLA_HEADS)
    att_s = attention(q_s, jnp.concatenate([k_c, k_s], axis=1), jnp.concatenate([v_c, v_s], axis=1),
                      n_heads=MLA_HEADS, n_kv=MLA_HEADS, heads_per_step=LATENT_HEADS_PER_STEP)
    conv_p = conformer_conv(u_p.reshape(bp, sp, CONV_CH), l0_conv_w, l0_conv_b, l0_conv_ln_g, l0_conv_ln_b)
    conv_s = conformer_conv(u_s, l0_conv_w, l0_conv_b, l0_conv_ln_g, l0_conv_ln_b)
    groups = [dict(attn=att_p.reshape(1, n_p, hw), other=conv_p.reshape(1, n_p, CONV_CH), x=hp, mod_off=0,
                   wa=w0["wa"], wc=w0["wc"]),
              dict(attn=att_s, other=conv_s, x=x_sample, mod_off=1, wa=w0["wa"], wc=w0["wc"])]
    hp, hs = routed_ffn(groups, mods0, l0_norm2.reshape(1, d),
                        (l0_router_w, l0_router_b, l0_w1, l0_b1, l0_w2, l0_b2), tri, fn, final=False)
    new_l0_ckv = ckv_p.reshape(bp, sp, MLA_KV_RANK)
    new_l0_krope = kr_p.reshape(bp, sp, MLA_ROPE)

    w1p = _prep_l1(l1_w_in, l1_q_norm, l1_k_norm, l1_gmlp_ln_g, l1_gmlp_ln_b, l1_w_s, l1_b_s, l1_w_out)
    n1 = l1_norm1.reshape(1, d)
    q_p, k_p, vp_p, gat_p, kt_p, vt_p = l1_inproj(hp, mods1, 0, n1, w1p, None, ctx_seq=sp)
    q_s, k_s, vp_s, gat_s = l1_inproj(hs, mods1, 1, n1, w1p, _l1_tables(ss))
    qw = GQA_HEADS * LANES
    kw = GQA_KV_HEADS * LANES
    pad_kv = lambda t: _pad_lanes(t, LANES).reshape(bs, past, kw).astype(BF16)
    att_p = attention(q_p.reshape(bp, sp, qw), k_p.reshape(bp, sp, kw), vp_p.reshape(bp, sp, kw),
                      n_heads=GQA_HEADS, n_kv=GQA_KV_HEADS, heads_per_step=GQA_HEADS)
    att_s = attention(q_s, jnp.concatenate([pad_kv(cache_l1_k), k_s], axis=1),
                      jnp.concatenate([_with_sum_lane(pad_kv(cache_l1_v)), vp_s], axis=1),
                      n_heads=GQA_HEADS, n_kv=GQA_KV_HEADS, heads_per_step=LATENT_HEADS_PER_STEP)
    groups = [dict(attn=att_p.reshape(1, n_p, qw), other=gat_p, x=hp, mod_off=0, wa=w1p["wa"], wc=w1p["wc"]),
              dict(attn=att_s, other=gat_s, x=hs, mod_off=1, wa=w1p["wa"], wc=w1p["wc"])]
    yp, ys = routed_ffn(groups, mods1, l1_norm2.reshape(1, d),
                        (l1_router_w, l1_router_b, l1_w1, l1_b1, l1_w2, l1_b2), tri, fn, final=True)
    new_l1_k = jnp.transpose(kt_p, (0, 3, 1, 2))
    new_l1_v = jnp.transpose(vt_p.reshape(bp, GQA_KV_HEADS, GQA_HEAD_DIM, sp), (0, 3, 1, 2))
    return (yp.reshape(bp, sp, d), ys, new_l0_ckv, new_l0_krope, new_l1_k, new_l1_v)
```

```python
import functools
import math

import jax
import jax.numpy as jnp
from jax import lax
from jax.experimental import pallas as pl
from jax.experimental.pallas import tpu as pltpu
from jax.experimental.pallas import tpu_sc as plsc

F32 = jnp.float32
BF16 = jnp.bfloat16
HIGHEST = lax.Precision.HIGHEST

LANES = 128
SUBLANES = 8
VMEM_LIMIT = 56 * 1024 * 1024

D_MODEL = 1024
GRID_W = 64
ROPE_THETA = 10000.0
EPS = 1e-6
N_MOD = 6

MLA_HEADS = 8
MLA_NOPE = 64
MLA_ROPE = 32
MLA_V = 64
MLA_Q_RANK = 384
MLA_KV_RANK = 256
MLA_SCALE = 1.0 / math.sqrt(MLA_NOPE + MLA_ROPE)
CONV_CH = 512
CONV_WIDTH = 31
CONV_HALO = 16

GQA_HEADS = 8
GQA_KV_HEADS = 2
GQA_HEAD_DIM = 64
GQA_SCALE = 1.0 / math.sqrt(GQA_HEAD_DIM)
CHUNK = 128
GMLP_GROUPS = 4
GMLP_CH = 512

N_EXPERTS = 32
TOP_K = 4
D_EXPERT = 1024
SWIGLU_LIMIT = 7.0
SWIGLU_ALPHA = 1.702

ROW_TILE = 512
POST_SUBTILES = 2
FFN_TILE = 256
SC_CORES = 2
SC_SUBCORES = 16
SC_WORKERS = SC_CORES * SC_SUBCORES
SC_CHUNK_BYTES = 256 * 1024
SC_MAX_INDICES = 128
COMBINE_TILE = 512
ATT_Q_TILE = 256
ATT_LONG_KEYS = 1024
V_SUM_LANE = 64
LATENT_HEADS_PER_STEP = 4
NEG_BIG = -1e30


def _params(sem, vmem=None):
    return pltpu.CompilerParams(dimension_semantics=sem, vmem_limit_bytes=vmem)


def _rms(x, g):
    return x * lax.rsqrt(jnp.mean(x * x, axis=-1, keepdims=True) + EPS) * g


ROW_WORDS = (SUBLANES // 2, LANES)


def _pack_rows(x_bf16):
    return pltpu.bitcast(x_bf16.reshape(x_bf16.shape[0], SUBLANES, LANES), jnp.int32)


def _unpack_rows(words):
    return pltpu.bitcast(words, BF16).reshape(words.shape[0], D_MODEL)


def _const_spec(shape):
    nd = len(shape)
    return pl.BlockSpec(shape, lambda *_: (0,) * nd)


def _adaln_kernel(c_ref, w_ref, b_ref, o_ref):
    c = c_ref[...]
    s = c * jax.nn.sigmoid(c)
    o_ref[...] = jnp.dot(s, w_ref[...], preferred_element_type=F32, precision=HIGHEST) + b_ref[...]


def adaln(cond8, ada_w, ada_b):
    d, n = ada_w.shape
    bn = n // 4
    m = pl.pallas_call(
        _adaln_kernel,
        out_shape=jax.ShapeDtypeStruct((SUBLANES, n), F32),
        grid=(n // bn,),
        in_specs=[_const_spec((SUBLANES, d)),
                  pl.BlockSpec((d, bn), lambda j: (0, j)),
                  pl.BlockSpec((1, bn), lambda j: (0, j))],
        out_specs=pl.BlockSpec((SUBLANES, bn), lambda j: (0, j)),
        compiler_params=_params(("arbitrary",), VMEM_LIMIT),
        name="adaln",
    )(cond8, ada_w, ada_b.reshape(1, n))
    m = m.reshape(SUBLANES, N_MOD, d)
    return jnp.pad(m, ((0, 0), (0, SUBLANES - N_MOD), (0, 0)))


def _l0_inproj_kernel(*refs, rope):
    if rope:
        (x_ref, m_ref, n1_ref, win_ref, qg_ref, kvg_ref, wuq_ref, wuqs_ref, wuk_ref, e_ref, wuv_ref,
         cq_ref, sq_ref, ck_ref, sk_ref, q_out, k_out, v_out, ckv_out, kr_out, u_out) = refs
    else:
        (x_ref, m_ref, n1_ref, win_ref, qg_ref, kvg_ref, wuq_ref, wuk_ref, e_ref, wuv_ref,
         q_out, k_out, v_out, ckv_out, kr_out, u_out) = refs
    x = x_ref[0]
    m = m_ref[0]
    h = _rms(x, n1_ref[...]) * (1.0 + m[1:2]) + m[0:1]
    z = jnp.dot(h.astype(BF16), win_ref[...], preferred_element_type=F32)
    c_q = z[:, 0:MLA_Q_RANK]
    c_kv = z[:, MLA_Q_RANK:MLA_Q_RANK + MLA_KV_RANK]
    kr_blk = z[:, 640:768]
    val = z[:, 768:768 + CONV_CH]
    gate = z[:, 768 + CONV_CH:768 + 2 * CONV_CH]

    cqn = _rms(c_q, qg_ref[...]).astype(BF16)
    q = jnp.dot(cqn, wuq_ref[...], preferred_element_type=F32)
    if rope:
        qs = jnp.dot(cqn, wuqs_ref[...], preferred_element_type=F32)
        cq = cq_ref[...]
        sq = sq_ref[...]
        for hd in range(MLA_HEADS):
            sl = slice(hd * LANES, (hd + 1) * LANES)
            q_out[0, :, sl] = ((q[:, sl] * cq + qs[:, sl] * sq) * MLA_SCALE).astype(BF16)
        kr = kr_blk * ck_ref[...] + pltpu.roll(kr_blk, LANES - MLA_ROPE, 1) * sk_ref[...]
    else:
        q_out[0] = (q * MLA_SCALE).astype(BF16)
        kr = kr_blk

    ckv = _rms(c_kv, kvg_ref[...])
    ckv_out[0] = ckv
    kr_out[0] = kr_blk[:, 0:MLA_ROPE]
    ckv_b = ckv.astype(BF16)
    k = (jnp.dot(ckv_b, wuk_ref[...], preferred_element_type=F32)
         + jnp.dot(kr.astype(BF16), e_ref[...], preferred_element_type=F32))
    k_out[0] = k.astype(BF16)
    v_out[0] = _with_sum_lane(jnp.dot(ckv_b, wuv_ref[...], preferred_element_type=F32)).astype(BF16)
    u_out[0] = val * jax.nn.sigmoid(gate)


def l0_inproj(x, mods, mod_off, n1, w, tables):
    bm, sm, d = x.shape
    tr = min(ROW_TILE, sm)
    rope = tables is not None
    hp = MLA_HEADS * LANES
    row = lambda width: pl.BlockSpec((1, tr, width), lambda b, i: (b, i, 0))
    in_specs = [row(d),
                pl.BlockSpec((1, SUBLANES, d), lambda b, i: (b + mod_off, 0, 0)),
                _const_spec((1, d)), _const_spec(w["win"].shape),
                _const_spec((1, MLA_Q_RANK)), _const_spec((1, MLA_KV_RANK)),
                _const_spec(w["wuq"].shape)]
    args = [x, mods, n1, w["win"], w["qg"], w["kvg"], w["wuq"]]
    if rope:
        in_specs.append(_const_spec(w["wuqs"].shape))
        args.append(w["wuqs"])
    in_specs += [_const_spec(w["wuk"].shape), _const_spec(w["e"].shape), _const_spec(w["wuv"].shape)]
    args += [w["wuk"], w["e"], w["wuv"]]
    if rope:
        in_specs += [pl.BlockSpec((tr, LANES), lambda b, i: (i, 0))] * 4
        args += list(tables)
    out_shape = [jax.ShapeDtypeStruct((bm, sm, hp), BF16),
                 jax.ShapeDtypeStruct((bm, sm, hp), BF16),
                 jax.ShapeDtypeStruct((bm, sm, hp), BF16),
                 jax.ShapeDtypeStruct((bm, sm, MLA_KV_RANK), F32),
                 jax.ShapeDtypeStruct((bm, sm, MLA_ROPE), F32),
                 jax.ShapeDtypeStruct((bm, sm, CONV_CH), F32)]
    out_specs = [row(hp), row(hp), row(hp), row(MLA_KV_RANK), row(MLA_ROPE), row(CONV_CH)]
    return pl.pallas_call(
        functools.partial(_l0_inproj_kernel, rope=rope),
        out_shape=out_shape, grid=(bm, sm // tr), in_specs=in_specs, out_specs=out_specs,
        compiler_params=_params(("parallel", "parallel"), VMEM_LIMIT),
        name="l0_inproj_rope" if rope else "l0_inproj",
    )(*args)


def _mla_ctx_kv_kernel(ckv_ref, kr_ref, wuk_ref, e_ref, wuv_ref, k_out, v_out):
    ckv_b = ckv_ref[0].astype(BF16)
    k = (jnp.dot(ckv_b, wuk_ref[...], preferred_element_type=F32)
         + jnp.dot(kr_ref[0].astype(BF16), e_ref[...], preferred_element_type=F32))
    k_out[0] = k.astype(BF16)
    v_out[0] = _with_sum_lane(jnp.dot(ckv_b, wuv_ref[...], preferred_element_type=F32)).astype(BF16)


def mla_ctx_kv(ckv, kr128, w):
    b, s, _ = ckv.shape
    hp = MLA_HEADS * LANES
    blk = lambda width: pl.BlockSpec((1, s, width), lambda i: (i, 0, 0))
    return pl.pallas_call(
        _mla_ctx_kv_kernel,
        out_shape=[jax.ShapeDtypeStruct((b, s, hp), BF16)] * 2,
        grid=(b,),
        in_specs=[blk(MLA_KV_RANK), blk(LANES), _const_spec(w["wuk"].shape),
                  _const_spec(w["e"].shape), _const_spec(w["wuv"].shape)],
        out_specs=[blk(hp), blk(hp)],
        compiler_params=_params(("parallel",), VMEM_LIMIT),
        name="mla_ctx_kv",
    )(ckv, kr128, w["wuk"], w["e"], w["wuv"])


def _conv_kernel(prev_ref, cur_ref, next_ref, w_ref, b_ref, g_ref, beta_ref, o_ref, pad_ref, sh_ref, *, rb):
    i = pl.program_id(1)
    last = pl.num_programs(1) - 1
    zeros = jnp.zeros((CONV_HALO, CONV_CH), F32)
    pad_ref[0:CONV_HALO, :] = jnp.where(i == 0, zeros, prev_ref[0])
    pad_ref[CONV_HALO:CONV_HALO + rb, :] = cur_ref[0]
    pad_ref[CONV_HALO + rb:CONV_HALO + rb + CONV_HALO, :] = jnp.where(i == last, zeros, next_ref[0])
    span = rb + 2 * CONV_HALO - SUBLANES
    for r in range(1, SUBLANES):
        sh_ref[r] = pad_ref[r:r + span, :]
    w = w_ref[...]
    shift = CONV_HALO - CONV_WIDTH // 2
    acc = jnp.zeros((rb, CONV_CH), F32) + b_ref[...]
    for k in range(CONV_WIDTH):
        off = k + shift
        r, a = off % SUBLANES, off // SUBLANES * SUBLANES
        window = pad_ref[a:a + rb, :] if r == 0 else sh_ref[r, a:a + rb, :]
        acc = acc + window * w[k:k + 1, :]
    mu = jnp.mean(acc, axis=-1, keepdims=True)
    cen = acc - mu
    var = jnp.mean(cen * cen, axis=-1, keepdims=True)
    y = cen * lax.rsqrt(var + EPS) * g_ref[...] + beta_ref[...]
    o_ref[0] = (y * jax.nn.sigmoid(y)).astype(BF16)


def conformer_conv(u, conv_w, conv_b, ln_g, ln_b):
    b, s, c = u.shape
    rb = min(256, s)
    nh = rb // CONV_HALO
    n_halo_blocks = s // CONV_HALO
    wpad = jnp.pad(conv_w.reshape(CONV_WIDTH, c), ((0, 32 - CONV_WIDTH), (0, 0)))
    return pl.pallas_call(
        functools.partial(_conv_kernel, rb=rb),
        out_shape=jax.ShapeDtypeStruct((b, s, c), BF16),
        grid=(b, s // rb),
        in_specs=[pl.BlockSpec((1, CONV_HALO, c), lambda bi, i: (bi, jnp.maximum(i * nh - 1, 0), 0)),
                  pl.BlockSpec((1, rb, c), lambda bi, i: (bi, i, 0)),
                  pl.BlockSpec((1, CONV_HALO, c),
                               lambda bi, i: (bi, jnp.minimum((i + 1) * nh, n_halo_blocks - 1), 0)),
                  _const_spec((32, c)), _const_spec((1, c)), _const_spec((1, c)), _const_spec((1, c))],
        out_specs=pl.BlockSpec((1, rb, c), lambda bi, i: (bi, i, 0)),
        scratch_shapes=[pltpu.VMEM((rb + 2 * CONV_HALO, c), F32),
                        pltpu.VMEM((SUBLANES, rb + 2 * CONV_HALO - SUBLANES, c), F32)],
        compiler_params=_params(("parallel", "parallel"), VMEM_LIMIT),
        name="conformer_conv",
    )(u, u, u, wpad, conv_b.reshape(1, c), ln_g.reshape(1, c), ln_b.reshape(1, c))


def _with_sum_lane(v):
    lane = lax.broadcasted_iota(jnp.int32, (1, v.shape[-1]), 1)
    return v + ((lane & (LANES - 1)) == V_SUM_LANE).astype(v.dtype)


def _attn_kernel(q_ref, k_ref, v_ref, o_ref, *, heads, rep, mxu_denominator):
    for hd in range(heads):
        g = hd // rep
        q = q_ref[0, :, hd * LANES:(hd + 1) * LANES]
        k = k_ref[0, :, g * LANES:(g + 1) * LANES]
        s = lax.dot_general(q, k, (((1,), (1,)), ((), ())), preferred_element_type=F32)
        m = jnp.max(s, axis=-1, keepdims=True)
        if mxu_denominator:
            p = jnp.exp((s - m).astype(BF16))
            o = jnp.dot(p, v_ref[0, :, g * LANES:(g + 1) * LANES], preferred_element_type=F32)
            l = o[:, V_SUM_LANE:V_SUM_LANE + 1]
        else:
            p = jnp.exp(s - m)
            l = jnp.sum(p, axis=-1, keepdims=True)
            o = jnp.dot(p.astype(BF16), v_ref[0, :, g * LANES:(g + 1) * LANES], preferred_element_type=F32)
        o_ref[0, :, hd * LANES:(hd + 1) * LANES] = (o / l).astype(BF16)


def attention(q, k, v, *, n_heads, n_kv, heads_per_step):
    b, sq, _ = q.shape
    sk = k.shape[1]
    rep = n_heads // n_kv
    tq = min(ATT_Q_TILE, sq)
    hb = heads_per_step
    grid = (b, n_heads // hb, sq // tq)
    if hb >= rep:
        kv_spec = pl.BlockSpec((1, sk, hb // rep * LANES), lambda bi, h, i: (bi, 0, h))
        kern_rep = rep
    else:
        assert rep % hb == 0
        kv_spec = pl.BlockSpec((1, sk, LANES), lambda bi, h, i: (bi, 0, h * hb // rep))
        kern_rep = hb
    kern = functools.partial(_attn_kernel, heads=hb, rep=kern_rep, mxu_denominator=sk >= ATT_LONG_KEYS)
    q_spec = pl.BlockSpec((1, tq, heads_per_step * LANES), lambda bi, h, i: (bi, i, h))
    return pl.pallas_call(
        kern,
        out_shape=jax.ShapeDtypeStruct(q.shape, BF16),
        grid=grid, in_specs=[q_spec, kv_spec, kv_spec], out_specs=q_spec,
        compiler_params=_params(("parallel", "parallel", "parallel"), VMEM_LIMIT),
        name="attention",
    )(q, k, v)


def _l1_inproj_kernel(*refs, rope):
    if rope:
        (x_ref, m_ref, n1_ref, win_ref, wsw_ref, qg_ref, qgs_ref, kg_ref, kgs_ref, lng_ref, lnb_ref,
         ws_ref, bs_ref, c_ref, s_ref, q_out, k_out, vp_out, g_out) = refs
        kt_out = vt_out = None
    else:
        (x_ref, m_ref, n1_ref, win_ref, qg_ref, kg_ref, lng_ref, lnb_ref,
         ws_ref, bs_ref, q_out, k_out, vp_out, g_out, kt_out, vt_out) = refs
    x = x_ref[0]
    m = m_ref[0]
    hb = (_rms(x, n1_ref[...]) * (1.0 + m[1:2]) + m[0:1]).astype(BF16)
    z = jnp.dot(hb, win_ref[...], preferred_element_type=F32)
    qw = GQA_HEADS * LANES
    kw = GQA_KV_HEADS * LANES
    o_k, o_vp, o_v, o_u, o_vg = qw, qw + kw, qw + 2 * kw, qw + 2 * kw + LANES, qw + 2 * kw + LANES + GMLP_CH
    if rope:
        zs = jnp.dot(hb, wsw_ref[...], preferred_element_type=F32)
        cos = c_ref[...]
        sin = s_ref[...]

    def head(col, zcol, g_ref, gs_ref):
        t = z[:, col:col + LANES]
        r = lax.rsqrt(jnp.sum(t * t, axis=-1, keepdims=True) * (1.0 / GQA_HEAD_DIM) + EPS)
        normed = t * r * g_ref[...]
        if not rope:
            return normed, normed
        ts = zs[:, zcol:zcol + LANES]
        return normed, normed * cos + ts * r * gs_ref[...] * sin

    for hd in range(GQA_HEADS):
        _, rot = head(hd * LANES, hd * LANES, qg_ref, qgs_ref if rope else None)
        q_out[0, :, hd * LANES:(hd + 1) * LANES] = (rot * GQA_SCALE).astype(BF16)
    for hd in range(GQA_KV_HEADS):
        normed, rot = head(o_k + hd * LANES, qw + hd * LANES, kg_ref, kgs_ref if rope else None)
        k_out[0, :, hd * LANES:(hd + 1) * LANES] = rot.astype(BF16)
        if kt_out is not None:
            seq = kt_out.shape[-1]
            for s in range(kt_out.shape[0]):
                kt_out[s, hd] = normed[s * seq:(s + 1) * seq, :].T[:GQA_HEAD_DIM, :]
    vp_out[0] = _with_sum_lane(z[:, o_vp:o_vp + kw]).astype(BF16)
    if vt_out is not None:
        seq = vt_out.shape[-1]
        for s in range(vt_out.shape[0]):
            vt_out[s] = z[s * seq:(s + 1) * seq, o_v:o_v + LANES].T

    u = z[:, o_u:o_u + GMLP_CH]
    vg = z[:, o_vg:o_vg + GMLP_CH]
    mu = jnp.mean(vg, axis=-1, keepdims=True)
    cen = vg - mu
    var = jnp.mean(cen * cen, axis=-1, keepdims=True)
    vn = (cen * lax.rsqrt(var + EPS) * lng_ref[...] + lnb_ref[...]).astype(BF16)
    bias = bs_ref[...]
    rows = x.shape[0]
    for cidx in range(rows // CHUNK):
        r0 = cidx * CHUNK
        for g in range(GMLP_GROUPS):
            c0 = g * LANES
            mixed = jnp.dot(ws_ref[g], vn[r0:r0 + CHUNK, c0:c0 + LANES], preferred_element_type=F32)
            g_out[0, r0:r0 + CHUNK, c0:c0 + LANES] = (
                u[r0:r0 + CHUNK, c0:c0 + LANES] * (mixed + bias[:, c0:c0 + LANES])).astype(BF16)


def l1_inproj(x, mods, mod_off, n1, w, tables, ctx_seq=None):
    bm, sm, d = x.shape
    tr = min(ROW_TILE, sm)
    rope = tables is not None
    assert rope != (ctx_seq is not None)
    qw = GQA_HEADS * LANES
    kw = GQA_KV_HEADS * LANES
    row = lambda width: pl.BlockSpec((1, tr, width), lambda b, i: (b, i, 0))
    vec = _const_spec((1, LANES))
    in_specs = [row(d), pl.BlockSpec((1, SUBLANES, d), lambda b, i: (b + mod_off, 0, 0)),
                _const_spec((1, d)), _const_spec(w["win"].shape)]
    args = [x, mods, n1, w["win"]]
    if rope:
        in_specs += [_const_spec(w["wsw"].shape), vec, vec, vec, vec]
        args += [w["wsw"], w["qg"], w["qgs"], w["kg"], w["kgs"]]
    else:
        in_specs += [vec, vec]
        args += [w["qg"], w["kg"]]
    in_specs += [_const_spec((1, GMLP_CH)), _const_spec((1, GMLP_CH)),
                 _const_spec(w["ws"].shape), _const_spec((CHUNK, GMLP_CH))]
    args += [w["lng"], w["lnb"], w["ws"], w["bs"]]
    if rope:
        in_specs += [pl.BlockSpec((tr, LANES), lambda b, i: (i, 0))] * 2
        args += list(tables)
    out_shape = [jax.ShapeDtypeStruct((bm, sm, qw), BF16),
                 jax.ShapeDtypeStruct((bm, sm, kw), BF16),
                 jax.ShapeDtypeStruct((bm, sm, kw), BF16),
                 jax.ShapeDtypeStruct((bm, sm, GMLP_CH), BF16)]
    out_specs = [row(qw), row(kw), row(kw), row(GMLP_CH)]
    if not rope:
        assert bm == 1 and tr % ctx_seq == 0
        n_seq, per_step = sm // ctx_seq, tr // ctx_seq
        out_shape += [jax.ShapeDtypeStruct((n_seq, GQA_KV_HEADS, GQA_HEAD_DIM, ctx_seq), F32),
                      jax.ShapeDtypeStruct((n_seq, GQA_KV_HEADS * GQA_HEAD_DIM, ctx_seq), F32)]
        out_specs += [pl.BlockSpec((per_step, GQA_KV_HEADS, GQA_HEAD_DIM, ctx_seq), lambda b, i: (i, 0, 0, 0)),
                      pl.BlockSpec((per_step, GQA_KV_HEADS * GQA_HEAD_DIM, ctx_seq), lambda b, i: (i, 0, 0))]
    return pl.pallas_call(
        functools.partial(_l1_inproj_kernel, rope=rope),
        out_shape=out_shape, grid=(bm, sm // tr), in_specs=in_specs, out_specs=out_specs,
        compiler_params=_params(("parallel", "parallel"), VMEM_LIMIT),
        name="l1_inproj_rope" if rope else "l1_inproj",
    )(*args)


def _post_kernel(a_ref, c_ref, x_ref, m_ref, n2_ref, wa_ref, wc_ref, rwh_ref, rwl_ref, rb_ref, tri_ref, base_ref,
                 x1_out, xm_out, idx_out, wts_out, rank_out, cnt_out, run_ref):
    first =(pl.program_id(0) == 0) & (pl.program_id(1) == 0)

    @pl.when(first)
    def _():
        run_ref[...] = base_ref[...]

    sub = tri_ref.shape[0]
    running = run_ref[0:1, :]
    for r0 in range(0, a_ref.shape[1], sub):
        running = _route_rows(slice(r0, r0 + sub), running, a_ref, c_ref, x_ref, m_ref, n2_ref, wa_ref, wc_ref,
                              rwh_ref, rwl_ref, rb_ref, tri_ref, x1_out, xm_out, idx_out, wts_out, rank_out)
    run_ref[0:1, :] = running
    cnt_out[...] = run_ref[...]


def _route_rows(rs, running, a_ref, c_ref, x_ref, m_ref, n2_ref, wa_ref, wc_ref, rwh_ref, rwl_ref, rb_ref, tri_ref,
                x1_out, xm_out, idx_out, wts_out, rank_out):
    m = m_ref[0]
    y = (jnp.dot(a_ref[0, rs, :], wa_ref[...], preferred_element_type=F32)
         + jnp.dot(c_ref[0, rs, :], wc_ref[...], preferred_element_type=F32))
    x1 = x_ref[0, rs, :] + m[2:3] * y
    x1_out[0, rs, :] = x1
    xm = _rms(x1, n2_ref[...]) * (1.0 + m[4:5]) + m[3:4]
    xh = xm.astype(BF16)
    xm_out[0, rs] = _pack_rows(xh)

    xl = (xm - xh.astype(F32)).astype(BF16)
    logits = (jnp.dot(xh, rwh_ref[...], preferred_element_type=F32)
              + jnp.dot(xl, rwh_ref[...], preferred_element_type=F32)
              + jnp.dot(xh, rwl_ref[...], preferred_element_type=F32)) + rb_ref[...]
    rows = logits.shape[0]
    lane = lax.broadcasted_iota(jnp.int32, (rows, LANES), 1).astype(F32)
    work = logits
    vals, hots = [], []
    idx_acc = jnp.zeros((rows, LANES), F32)
    for k in range(TOP_K):
        top = jnp.max(work, axis=-1, keepdims=True)
        sel = jnp.min(jnp.where(work == top, lane, float(LANES)), axis=-1, keepdims=True)
        hot = lane == sel
        vals.append(top)
        hots.append(hot)
        idx_acc = idx_acc + jnp.where(lane == float(k), sel, 0.0)
        work = jnp.where(hot, -jnp.inf, work)
    exps = [jnp.exp(v - vals[0]) for v in vals]
    denom = exps[0] + exps[1] + exps[2] + exps[3]
    wcols = [jnp.broadcast_to(exps[k] / denom, (rows, LANES)) for k in range(TOP_K)]
    wcols += [jnp.zeros((rows, LANES), F32)] * (SUBLANES - TOP_K)
    wts = jnp.concatenate(wcols, axis=1).reshape(rows, SUBLANES, LANES)

    chosen = jnp.zeros((rows, LANES), F32)
    for hot in hots:
        chosen = chosen + hot.astype(F32)
    before = jnp.dot(tri_ref[...], chosen.astype(BF16), preferred_element_type=F32) + running
    rank = jnp.zeros((rows, LANES), F32)
    for k in range(TOP_K):
        rk = jnp.sum(jnp.where(hots[k], before, 0.0), axis=-1, keepdims=True)
        rank = rank + jnp.where(lane == float(k), rk, 0.0)
    idx_out[:, rs] = idx_acc.T[:SUBLANES, :].astype(jnp.int32)
    wts_out[0, rs] = wts
    rank_out[:, rs] = rank.T[:SUBLANES, :].astype(jnp.int32)
    return running + jnp.sum(chosen, axis=0, keepdims=True)


def post_mixer(attn, other, x, mods, mod_off, n2, wa, wc, rwh, rwl, rb, tri, base):
    bm, sm, d = x.shape
    tr = POST_SUBTILES * tri.shape[0]
    assert sm % tr == 0
    row = lambda width: pl.BlockSpec((1, tr, width), lambda b, i: (b, i, 0))
    tile_rows = pl.BlockSpec((1, tr) + ROW_WORDS, lambda b, i: (b, i, 0, 0))
    nb = sm // tr
    token_minor = pl.BlockSpec((SUBLANES, tr), lambda b, i: (0, b * nb + i))
    out_shape = [jax.ShapeDtypeStruct((bm, sm, d), F32),
                 jax.ShapeDtypeStruct((bm, sm) + ROW_WORDS, jnp.int32),
                 jax.ShapeDtypeStruct((SUBLANES, bm * sm), jnp.int32),
                 jax.ShapeDtypeStruct((bm, sm, SUBLANES, LANES), F32),
                 jax.ShapeDtypeStruct((SUBLANES, bm * sm), jnp.int32),
                 jax.ShapeDtypeStruct((SUBLANES, LANES), F32)]
    return pl.pallas_call(
        _post_kernel,
        out_shape=out_shape, grid=(bm, sm // tr),
        in_specs=[row(attn.shape[-1]), row(other.shape[-1]), row(d),
                  pl.BlockSpec((1, SUBLANES, d), lambda b, i: (b + mod_off, 0, 0)),
                  _const_spec((1, d)), _const_spec(wa.shape), _const_spec(wc.shape),
                  _const_spec(rwh.shape), _const_spec(rwl.shape), _const_spec((1, LANES)),
                  _const_spec(tri.shape), _const_spec((SUBLANES, LANES))],
        out_specs=[row(d), tile_rows, token_minor,
                   pl.BlockSpec((1, tr, SUBLANES, LANES), lambda b, i: (b, i, 0, 0)),
                   token_minor, _const_spec((SUBLANES, LANES))],
        scratch_shapes=[pltpu.VMEM((SUBLANES, LANES), F32)],
        compiler_params=_params(("arbitrary", "arbitrary"), VMEM_LIMIT),
        name="post_mixer_route",
    )(attn, other, x, mods, n2, wa, wc, rwh, rwl, rb, tri, base)


def _plan_kernel(off_ref, idx_ref, rank_ref, pos_out):
    idx = idx_ref[...]
    pos = rank_ref[...]
    for e in range(N_EXPERTS):
        pos = pos + jnp.where(idx == e, off_ref[e], 0)
    pos_out[...] = pos


def plan_positions(offsets, idx, rank):
    n = idx.shape[1]
    tr = min(2048, n)
    spec = pl.BlockSpec((SUBLANES, tr), lambda i, off: (0, i))
    return pl.pallas_call(
        _plan_kernel,
        out_shape=jax.ShapeDtypeStruct((SUBLANES, n), jnp.int32),
        grid_spec=pltpu.PrefetchScalarGridSpec(
            num_scalar_prefetch=1, grid=(n // tr,), in_specs=[spec, spec], out_specs=spec),
        compiler_params=_params(("parallel",)),
        name="plan_positions",
    )(offsets, idx, rank)


def _sc_worker_id():
    return lax.axis_index("s") * SC_CORES + lax.axis_index("c")


def _sc_chunk_rows(tile, dtype):
    row_bytes = math.prod(tile) * jnp.dtype(dtype).itemsize
    return min(SC_CHUNK_BYTES // row_bytes, SC_MAX_INDICES)


def dispatch_rows(xms, posk, n_rows):
    n_total = sum(x.shape[0] for x in xms)
    tile, dtype = xms[0].shape[1:], xms[0].dtype
    chunk = _sc_chunk_rows(tile, dtype)
    starts, s0 = [], 0
    for x in xms:
        assert x.shape[0] % (SC_WORKERS * chunk) == 0
        starts.append(s0)
        s0 += x.shape[0]
    mesh = plsc.VectorSubcoreMesh(core_axis_name="c", subcore_axis_name="s")

    @functools.partial(
        pl.kernel, mesh=mesh, out_type=jax.ShapeDtypeStruct((n_rows,) + tile, dtype),
        scratch_types=[pltpu.VMEM((chunk,), jnp.int32), pltpu.VMEM((chunk,) + tile, dtype),
                       pltpu.SemaphoreType.DMA],
        name="dispatch_rows_sc")
    def scatter(*refs):
        x_refs, pos_hbm, xs_hbm, idx_v, rows_v, sem = refs[:len(xms)], *refs[len(xms):]
        wid = _sc_worker_id()
        for x_hbm, start in zip(x_refs, starts):
            per_worker = x_hbm.shape[0] // SC_WORKERS

            @pl.loop(0, per_worker // chunk)
            def _(c):
                t0 = wid * per_worker + c * chunk
                pltpu.sync_copy(x_hbm.at[pl.ds(t0, chunk)], rows_v)
                for k in range(TOP_K):
                    pltpu.sync_copy(pos_hbm.at[pl.ds(k * n_total + start + t0, chunk)], idx_v)
                    pltpu.async_copy(rows_v, xs_hbm.at[idx_v], sem).wait()

    return scatter(*xms, posk)


def gather_rows(y, posk):
    n_pairs = posk.shape[0]
    tile, dtype = y.shape[1:], y.dtype
    chunk = _sc_chunk_rows(tile, dtype)
    per_worker = n_pairs // SC_WORKERS
    assert per_worker % chunk == 0
    mesh = plsc.VectorSubcoreMesh(core_axis_name="c", subcore_axis_name="s")

    @functools.partial(
        pl.kernel, mesh=mesh, out_type=jax.ShapeDtypeStruct((n_pairs,) + tile, dtype),
        scratch_types=[pltpu.VMEM((chunk,), jnp.int32), pltpu.VMEM((chunk,) + tile, dtype),
                       pltpu.SemaphoreType.DMA],
        name="gather_rows_sc")
    def gather(y_hbm, pos_hbm, out_hbm, idx_v, rows_v, sem):
        wid = _sc_worker_id()

        @pl.loop(0, per_worker // chunk)
        def _(c):
            base = wid * per_worker + c * chunk
            pltpu.sync_copy(pos_hbm.at[pl.ds(base, chunk)], idx_v)
            pltpu.async_copy(y_hbm.at[idx_v], rows_v, sem).wait()
            pltpu.sync_copy(rows_v, out_hbm.at[pl.ds(base, chunk)])

    return gather(y, posk)


def _ffn_kernel(te_ref, nu_ref, first_ref, slot_ref, nxt_ref, xs_ref, w1_hbm, b1_ref, w2_hbm, b2_ref, y_ref,
                w1f, w2f, w1b, w2b, sem):
    i = pl.program_id(0)

    def weight_copies(e, s):
        return (pltpu.make_async_copy(w1_hbm.at[e], w1f.at[s], sem.at[0, s]),
                pltpu.make_async_copy(w2_hbm.at[e], w2f.at[s], sem.at[1, s]))

    @pl.when(i < nu_ref[0])
    def _():
        s = slot_ref[i]

        @pl.when(first_ref[i] == 1)
        def _():
            @pl.when(i == 0)
            def _():
                for cp in weight_copies(te_ref[i], s):
                    cp.start()
            for cp in weight_copies(te_ref[i], s):
                cp.wait()

            @pl.when(nxt_ref[i] >= 0)
            def _():
                for cp in weight_copies(nxt_ref[i], 1 - s):
                    cp.start()
            for c in range(D_MODEL // LANES):
                w1b[c * LANES:(c + 1) * LANES, :] = w1f[s, c * LANES:(c + 1) * LANES, :].astype(BF16)
            for c in range(D_EXPERT // LANES):
                w2b[c * LANES:(c + 1) * LANES, :] = w2f[s, c * LANES:(c + 1) * LANES, :].astype(BF16)

        x = _unpack_rows(xs_ref[...])
        h = jnp.dot(x, w1b[...], preferred_element_type=F32) + b1_ref[0]
        g = jnp.minimum(h[:, :D_EXPERT], SWIGLU_LIMIT)
        lin = jnp.clip(h[:, D_EXPERT:], -SWIGLU_LIMIT, SWIGLU_LIMIT)
        a = (lin + 1.0) * (g * jax.nn.sigmoid(SWIGLU_ALPHA * g))
        y = jnp.dot(a.astype(BF16), w2b[...], preferred_element_type=F32) + b2_ref[0]
        y_ref[...] = _pack_rows(y.astype(BF16))

    @pl.when(i >= nu_ref[0])
    def _():
        y_ref[...] = jnp.zeros(y_ref.shape, y_ref.dtype)


def grouped_ffn(sched, xs, w1, b1, w2, b2):
    r = xs.shape[0]
    d = w1.shape[1]
    nt = r // FFN_TILE
    tile = (FFN_TILE,) + xs.shape[1:]
    rows = lambda i, te, nu, *_: (jnp.minimum(i, nu[0] - 1), 0, 0)
    bsel = lambda i, te, *_: (te[i], 0, 0)
    return pl.pallas_call(
        _ffn_kernel,
        out_shape=jax.ShapeDtypeStruct(xs.shape, xs.dtype),
        grid_spec=pltpu.PrefetchScalarGridSpec(
            num_scalar_prefetch=5, grid=(nt,),
            in_specs=[pl.BlockSpec(tile, rows),
                      pl.BlockSpec(memory_space=pl.ANY),
                      pl.BlockSpec((1, 1, 2 * D_EXPERT), bsel),
                      pl.BlockSpec(memory_space=pl.ANY),
                      pl.BlockSpec((1, 1, d), bsel)],
            out_specs=pl.BlockSpec(tile, lambda i, *_: (i, 0, 0)),
            scratch_shapes=[pltpu.VMEM((2, d, 2 * D_EXPERT), F32), pltpu.VMEM((2, D_EXPERT, d), F32),
                            pltpu.VMEM((d, 2 * D_EXPERT), BF16), pltpu.VMEM((D_EXPERT, d), BF16),
                            pltpu.SemaphoreType.DMA((2, 2))]),
        compiler_params=_params(("arbitrary",), VMEM_LIMIT),
        name="grouped_ffn",
    )(*sched, xs, w1, b1.reshape(N_EXPERTS, 1, -1), w2, b2.reshape(N_EXPERTS, 1, -1))


def _combine_kernel(x1_ref, wts_ref, m_ref, fn_ref, y0_ref, y1_ref, y2_ref, y3_ref, o_ref, *, final):
    w = wts_ref[0]
    rows = lambda ref: pltpu.bitcast(ref[...], BF16).astype(F32)
    acc = w[:, 0:1, :] * rows(y0_ref)
    for k, y_ref in ((1, y1_ref), (2, y2_ref), (3, y3_ref)):
        acc = acc + w[:, k:k + 1, :] * rows(y_ref)
    out = x1_ref[0] + m_ref[0][5:6] * acc.reshape(x1_ref.shape[1], D_MODEL)
    if final:
        out = _rms(out, fn_ref[...])
    o_ref[0] = out


def combine_rows(x1, wts, mods, mod_off, fn, yg, row_off, n_total, *, final):
    bm, sm, d = x1.shape
    tr = min(COMBINE_TILE, sm)
    nb = sm // tr
    row = lambda width: pl.BlockSpec((1, tr, width), lambda b, i: (b, i, 0))
    ysel = lambda k: pl.BlockSpec((tr,) + yg.shape[1:],
                                  lambda b, i: ((k * n_total + row_off) // tr + b * nb + i, 0, 0))
    return pl.pallas_call(
        functools.partial(_combine_kernel, final=final),
        out_shape=jax.ShapeDtypeStruct((bm, sm, d), F32),
        grid=(bm, nb),
        in_specs=[row(d), pl.BlockSpec((1, tr, SUBLANES, LANES), lambda b, i: (b, i, 0, 0)),
                  pl.BlockSpec((1, SUBLANES, d), lambda b, i: (b + mod_off, 0, 0)),
                  _const_spec((1, d))] + [ysel(k) for k in range(TOP_K)],
        out_specs=row(d),
        compiler_params=_params(("parallel", "parallel"), VMEM_LIMIT),
        name="combine_rows",
    )(x1, wts, mods, fn, yg, yg, yg, yg)


def _axial_angles(n_tokens, rot_dim):
    t = jnp.arange(n_tokens)
    rows = (t // GRID_W).astype(F32)
    cols = (t % GRID_W).astype(F32)
    n_freq = rot_dim // 4
    inv = ROPE_THETA ** (-jnp.arange(n_freq, dtype=F32) / n_freq)
    return jnp.concatenate([rows[:, None] * inv, cols[:, None] * inv], axis=-1)


def _lane_table(parts, n):
    cols = []
    for p in parts:
        cols.append(jnp.broadcast_to(jnp.asarray(p, F32), (n, p.shape[-1])) if hasattr(p, "shape") else p)
    return jnp.concatenate(cols, axis=-1)


def _swap_halves(w):
    half = w.shape[-1] // 2
    return jnp.concatenate([-w[..., half:], w[..., :half]], axis=-1)


def _prep_l0(w_in, q_norm, kv_norm, w_uq, w_uk, w_uv, w_out):
    d = w_in.shape[0]
    o_kr = MLA_Q_RANK + MLA_KV_RANK
    kr_cols = w_in[:, o_kr:o_kr + MLA_ROPE]
    win = jnp.concatenate(
        [w_in[:, :o_kr], kr_cols, _swap_halves(kr_cols), jnp.zeros((d, LANES - 2 * MLA_ROPE), F32),
         w_in[:, o_kr + MLA_ROPE:]], axis=1).astype(BF16)
    qk = MLA_NOPE + MLA_ROPE
    wuq3 = w_uq.reshape(MLA_Q_RANK, MLA_HEADS, qk)
    wuq = jnp.pad(wuq3, ((0, 0), (0, 0), (0, LANES - qk))).reshape(MLA_Q_RANK, -1).astype(BF16)
    wuqs3 = jnp.concatenate(
        [jnp.zeros((MLA_Q_RANK, MLA_HEADS, MLA_NOPE), F32), _swap_halves(wuq3[:, :, MLA_NOPE:]),
         jnp.zeros((MLA_Q_RANK, MLA_HEADS, LANES - qk), F32)], axis=-1)
    wuqs = wuqs3.reshape(MLA_Q_RANK, -1).astype(BF16)
    wuk3 = w_uk.reshape(MLA_KV_RANK, MLA_HEADS, MLA_NOPE)
    wuk = jnp.pad(wuk3, ((0, 0), (0, 0), (0, LANES - MLA_NOPE))).reshape(MLA_KV_RANK, -1).astype(BF16)
    wuv3 = w_uv.reshape(MLA_KV_RANK, MLA_HEADS, MLA_V)
    wuv = jnp.pad(wuv3, ((0, 0), (0, 0), (0, LANES - MLA_V))).reshape(MLA_KV_RANK, -1).astype(BF16)
    eye = jnp.eye(MLA_ROPE, dtype=F32)
    e_head = jnp.concatenate([jnp.zeros((MLA_ROPE, MLA_NOPE), F32), eye,
                              jnp.zeros((MLA_ROPE, LANES - qk), F32)], axis=1)
    e = jnp.pad(jnp.tile(e_head, (1, MLA_HEADS)), ((0, LANES - MLA_ROPE), (0, 0))).astype(BF16)
    wa3 = w_out[:MLA_HEADS * MLA_V].reshape(MLA_HEADS, MLA_V, d)
    wa = jnp.pad(wa3, ((0, 0), (0, LANES - MLA_V), (0, 0))).reshape(MLA_HEADS * LANES, d).astype(BF16)
    wc = w_out[MLA_HEADS * MLA_V:].astype(BF16)
    return dict(win=win, qg=q_norm.reshape(1, -1), kvg=kv_norm.reshape(1, -1), wuq=wuq, wuqs=wuqs,
                wuk=wuk, e=e, wuv=wuv, wa=wa, wc=wc)


def _l0_tables(n):
    ang = _axial_angles(n, MLA_ROPE)
    cos, sin = jnp.cos(ang), jnp.sin(ang)
    one = jnp.ones((n, 1), F32)
    zero = jnp.zeros((n, 1), F32)
    rest = LANES - MLA_NOPE - MLA_ROPE
    cq = jnp.concatenate([jnp.tile(one, (1, MLA_NOPE)), cos, cos, jnp.tile(one, (1, rest))], axis=1)
    sq = jnp.concatenate([jnp.tile(zero, (1, MLA_NOPE)), sin, sin, jnp.tile(zero, (1, rest))], axis=1)
    ck = jnp.concatenate([cos, cos, jnp.tile(zero, (1, LANES - MLA_ROPE))], axis=1)
    sk = jnp.concatenate([sin, sin, jnp.tile(zero, (1, LANES - MLA_ROPE))], axis=1)
    return cq, sq, ck, sk


def _pad_heads(w, n_heads, dim):
    d = w.shape[0]
    return jnp.pad(w.reshape(d, n_heads, dim), ((0, 0), (0, 0), (0, LANES - dim))).reshape(d, n_heads * LANES)


def _prep_l1(w_in, q_norm, k_norm, ln_g, ln_b, w_s, b_s, w_out):
    d = w_in.shape[0]
    qd = GQA_HEADS * GQA_HEAD_DIM
    kd = GQA_KV_HEADS * GQA_HEAD_DIM
    wq, wk, wv = w_in[:, :qd], w_in[:, qd:qd + kd], w_in[:, qd + kd:qd + 2 * kd]
    rest = w_in[:, qd + 2 * kd:]
    win = jnp.concatenate([_pad_heads(wq, GQA_HEADS, GQA_HEAD_DIM), _pad_heads(wk, GQA_KV_HEADS, GQA_HEAD_DIM),
                           _pad_heads(wv, GQA_KV_HEADS, GQA_HEAD_DIM), wv, rest], axis=1).astype(BF16)
    swq = _swap_halves(wq.reshape(d, GQA_HEADS, GQA_HEAD_DIM)).reshape(d, qd)
    swk = _swap_halves(wk.reshape(d, GQA_KV_HEADS, GQA_HEAD_DIM)).reshape(d, kd)
    wsw = jnp.concatenate([_pad_heads(swq, GQA_HEADS, GQA_HEAD_DIM),
                           _pad_heads(swk, GQA_KV_HEADS, GQA_HEAD_DIM)], axis=1).astype(BF16)
    half = GQA_HEAD_DIM // 2
    padg = lambda g: jnp.pad(g, (0, LANES - GQA_HEAD_DIM)).reshape(1, LANES)
    swapg = lambda g: jnp.concatenate([g[half:], g[:half]])
    wa3 = w_out[:qd].reshape(GQA_HEADS, GQA_HEAD_DIM, d)
    wa = jnp.pad(wa3, ((0, 0), (0, LANES - GQA_HEAD_DIM), (0, 0))).reshape(GQA_HEADS * LANES, d).astype(BF16)
    wc = w_out[qd:].astype(BF16)
    bs = jnp.repeat(b_s.T, LANES, axis=1)
    return dict(win=win, wsw=wsw, qg=padg(q_norm), qgs=padg(swapg(q_norm)), kg=padg(k_norm),
                kgs=padg(swapg(k_norm)), lng=ln_g.reshape(1, -1), lnb=ln_b.reshape(1, -1),
                ws=w_s.astype(BF16), bs=bs, wa=wa, wc=wc)


def _l1_tables(n):
    ang = _axial_angles(n, GQA_HEAD_DIM)
    cos, sin = jnp.cos(ang), jnp.sin(ang)
    pad = LANES - GQA_HEAD_DIM
    c = jnp.concatenate([cos, cos, jnp.ones((n, pad), F32)], axis=1)
    s = jnp.concatenate([sin, sin, jnp.zeros((n, pad), F32)], axis=1)
    return c, s


def _pad_lanes(x, width):
    return jnp.pad(x, [(0, 0)] * (x.ndim - 1) + [(0, width - x.shape[-1])])


def routed_ffn(groups, mods, n2, moe, tri, final_norm, *, final):
    router_w, router_b, w1, b1, w2, b2 = moe
    rw = _pad_lanes(router_w, LANES)
    rwh = rw.astype(BF16)
    rwl = (rw - rwh.astype(F32)).astype(BF16)
    rb = jnp.concatenate([router_b, jnp.full((LANES - N_EXPERTS,), NEG_BIG, F32)]).reshape(1, LANES)
    base = jnp.zeros((SUBLANES, LANES), F32)
    routed = []
    for g in groups:
        x1, xm, idx, wts, rank, base = post_mixer(
            g["attn"], g["other"], g["x"], mods, g["mod_off"], n2, g["wa"], g["wc"], rwh, rwl, rb, tri, base)
        routed.append((x1, xm, idx, wts, rank))
    n_total = sum(r[0].shape[0] * r[0].shape[1] for r in routed)
    nt = n_total * TOP_K // FFN_TILE + N_EXPERTS

    counts = base[0, :N_EXPERTS].astype(jnp.int32)
    tiles = (counts + FFN_TILE - 1) // FFN_TILE
    tile_end = jnp.cumsum(tiles)
    offsets = (tile_end - tiles) * FFN_TILE
    n_used = tile_end[-1:].astype(jnp.int32)
    experts = jnp.arange(N_EXPERTS)
    busy = tiles > 0
    slot_e = (jnp.cumsum(busy) - 1) % 2
    later = jnp.where(busy[None, :] & (experts[None, :] > experts[:, None]), experts[None, :], N_EXPERTS)
    nxt_e = jnp.min(later, axis=1)
    nxt_e = jnp.where(nxt_e == N_EXPERTS, -1, nxt_e)
    tile_expert = jnp.sum(jnp.arange(nt)[:, None] >= tile_end[None, :], axis=1)
    tile_expert = jnp.minimum(tile_expert, jnp.max(jnp.where(busy, experts, 0)))
    first = jnp.concatenate([jnp.ones((1,), bool), tile_expert[1:] != tile_expert[:-1]])
    of_tile = tile_expert[:, None] == experts[None, :]
    slot_t = jnp.sum(jnp.where(of_tile, slot_e[None, :], 0), axis=1)
    nxt_t = jnp.sum(jnp.where(of_tile, nxt_e[None, :], 0), axis=1)
    sched = tuple(a.astype(jnp.int32) for a in (tile_expert, n_used, first, slot_t, nxt_t))

    positions, xms = [], []
    for (x1, xm, idx, wts, rank) in routed:
        positions.append(plan_positions(offsets, idx, rank)[:TOP_K])
        xms.append(xm.reshape((idx.shape[1],) + ROW_WORDS))
    posk = jnp.concatenate(positions, axis=1).reshape(-1)
    xs = dispatch_rows(xms, posk, nt * FFN_TILE)
    y = grouped_ffn(sched, xs, w1, b1, w2, b2)
    outs = []
    for g, (x1, xm, idx, wts, rank), pos in zip(groups, routed, positions):
        yg = gather_rows(y, pos.reshape(-1))
        outs.append(combine_rows(x1, wts, mods, g["mod_off"], final_norm, yg, 0, pos.shape[1], final=final))
    return outs


def kernel(x_prompt, x_sample, cache_l0_ckv, cache_l0_krope, cache_l1_k, cache_l1_v, c, c_ctx,
           l0_ada_w, l0_ada_b, l0_norm1, l0_w_in, l0_q_norm, l0_kv_norm, l0_w_uq, l0_w_uk, l0_w_uv,
           l0_conv_w, l0_conv_b, l0_conv_ln_g, l0_conv_ln_b, l0_w_out, l0_norm2,
           l0_router_w, l0_router_b, l0_w1, l0_b1, l0_w2, l0_b2,
           l1_ada_w, l1_ada_b, l1_norm1, l1_w_in, l1_q_norm, l1_k_norm, l1_gmlp_ln_g, l1_gmlp_ln_b,
           l1_w_s, l1_b_s, l1_w_out, l1_norm2,
           l1_router_w, l1_router_b, l1_w1, l1_b1, l1_w2, l1_b2,
           final_norm):
    bp, sp, d = x_prompt.shape
    bs, ss, _ = x_sample.shape
    past = cache_l0_ckv.shape[1]
    n_p = bp * sp

    cond8 = jnp.concatenate([c_ctx[None], c, jnp.zeros((SUBLANES - 1 - bs, d), F32)], axis=0)
    mods0 = adaln(cond8, l0_ada_w, l0_ada_b)
    mods1 = adaln(cond8, l1_ada_w, l1_ada_b)
    tri = jnp.tril(jnp.ones((ROW_TILE, ROW_TILE), F32), -1).astype(BF16)
    fn = final_norm.reshape(1, d)

    w0 = _prep_l0(l0_w_in, l0_q_norm, l0_kv_norm, l0_w_uq, l0_w_uk, l0_w_uv, l0_w_out)
    n1 = l0_norm1.reshape(1, d)
    hp = x_prompt.reshape(1, n_p, d)
    q_p, k_p, v_p, ckv_p, kr_p, u_p = l0_inproj(hp, mods0, 0, n1, w0, None)
    q_s, k_s, v_s, _, _, u_s = l0_inproj(x_sample, mods0, 1, n1, w0, _l0_tables(ss))
    k_c, v_c = mla_ctx_kv(cache_l0_ckv, _pad_lanes(cache_l0_krope, LANES), w0)
    hw = MLA_HEADS * LANES
    att_p = attention(q_p.reshape(bp, sp, hw), k_p.reshape(bp, sp, hw), v_p.reshape(bp, sp, hw),
                      n_heads=MLA_HEADS, n_kv=MLA_HEADS, heads_per_step=MLA_HEADS)
    att_s = attention(q_s, jnp.concatenate([k_c, k_s], axis=1), jnp.concatenate([v_c, v_s], axis=1),
                      n_heads=MLA_HEADS, n_kv=MLA_HEADS, heads_per_step=LATENT_HEADS_PER_STEP)
    conv_p = conformer_conv(u_p.reshape(bp, sp, CONV_CH), l0_conv_w, l0_conv_b, l0_conv_ln_g, l0_conv_ln_b)
    conv_s = conformer_conv(u_s, l0_conv_w, l0_conv_b, l0_conv_ln_g, l0_conv_ln_b)
    groups = [dict(attn=att_p.reshape(1, n_p, hw), other=conv_p.reshape(1, n_p, CONV_CH), x=hp, mod_off=0,
                   wa=w0["wa"], wc=w0["wc"]),
              dict(attn=att_s, other=conv_s, x=x_sample, mod_off=1, wa=w0["wa"], wc=w0["wc"])]
    hp, hs = routed_ffn(groups, mods0, l0_norm2.reshape(1, d),
                        (l0_router_w, l0_router_b, l0_w1, l0_b1, l0_w2, l0_b2), tri, fn, final=False)
    new_l0_ckv = ckv_p.reshape(bp, sp, MLA_KV_RANK)
    new_l0_krope = kr_p.reshape(bp, sp, MLA_ROPE)

    w1p = _prep_l1(l1_w_in, l1_q_norm, l1_k_norm, l1_gmlp_ln_g, l1_gmlp_ln_b, l1_w_s, l1_b_s, l1_w_out)
    n1 = l1_norm1.reshape(1, d)
    q_p, k_p, vp_p, gat_p, kt_p, vt_p = l1_inproj(hp, mods1, 0, n1, w1p, None, ctx_seq=sp)
    q_s, k_s, vp_s, gat_s = l1_inproj(hs, mods1, 1, n1, w1p, _l1_tables(ss))
    qw = GQA_HEADS * LANES
    kw = GQA_KV_HEADS * LANES
    pad_kv = lambda t: _pad_lanes(t, LANES).reshape(bs, past, kw).astype(BF16)
    att_p = attention(q_p.reshape(bp, sp, qw), k_p.reshape(bp, sp, kw), vp_p.reshape(bp, sp, kw),
                      n_heads=GQA_HEADS, n_kv=GQA_KV_HEADS, heads_per_step=GQA_HEADS)
    att_s = attention(q_s, jnp.concatenate([pad_kv(cache_l1_k), k_s], axis=1),
                      jnp.concatenate([_with_sum_lane(pad_kv(cache_l1_v)), vp_s], axis=1),
                      n_heads=GQA_HEADS, n_kv=GQA_KV_HEADS, heads_per_step=LATENT_HEADS_PER_STEP)
    groups = [dict(attn=att_p.reshape(1, n_p, qw), other=gat_p, x=hp, mod_off=0, wa=w1p["wa"], wc=w1p["wc"]),
              dict(attn=att_s, other=gat_s, x=hs, mod_off=1, wa=w1p["wa"], wc=w1p["wc"])]
    yp, ys = routed_ffn(groups, mods1, l1_norm2.reshape(1, d),
                        (l1_router_w, l1_router_b, l1_w1, l1_b1, l1_w2, l1_b2), tri, fn, final=True)
    new_l1_k = jnp.transpose(kt_p, (0, 3, 1, 2))
    new_l1_v = jnp.transpose(vt_p.reshape(bp, GQA_KV_HEADS, GQA_HEAD_DIM, sp), (0, 3, 1, 2))
    return (yp.reshape(bp, sp, d), ys, new_l0_ckv, new_l0_krope, new_l1_k, new_l1_v)
```

```python
import functools
import math

import jax
import jax.numpy as jnp
from jax import lax
from jax.experimental import pallas as pl
from jax.experimental.pallas import tpu as pltpu
from jax.experimental.pallas import tpu_sc as plsc

F32 = jnp.float32
BF16 = jnp.bfloat16
HIGHEST = lax.Precision.HIGHEST

LANES = 128
SUBLANES = 8
VMEM_LIMIT = 56 * 1024 * 1024

D_MODEL = 1024
GRID_W = 64
ROPE_THETA = 10000.0
EPS = 1e-6
N_MOD = 6

MLA_HEADS = 8
MLA_NOPE = 64
MLA_ROPE = 32
MLA_V = 64
MLA_Q_RANK = 384
MLA_KV_RANK = 256
MLA_SCALE = 1.0 / math.sqrt(MLA_NOPE + MLA_ROPE)
CONV_CH = 512
CONV_WIDTH = 31
CONV_HALO = 16

GQA_HEADS = 8
GQA_KV_HEADS = 2
GQA_HEAD_DIM = 64
GQA_SCALE = 1.0 / math.sqrt(GQA_HEAD_DIM)
CHUNK = 128
GMLP_GROUPS = 4
GMLP_CH = 512

N_EXPERTS = 32
TOP_K = 4
D_EXPERT = 1024
SWIGLU_LIMIT = 7.0
SWIGLU_ALPHA = 1.702

ROW_TILE = 512
POST_SUBTILES = 2
FFN_TILE = 256
SC_CORES = 2
SC_SUBCORES = 16
SC_WORKERS = SC_CORES * SC_SUBCORES
SC_CHUNK_BYTES = 256 * 1024
SC_MAX_INDICES = 128
COMBINE_TILE = 512
ATT_Q_TILE = 256
ATT_LONG_KEYS = 1024
V_SUM_LANE = 64
LATENT_HEADS_PER_STEP = 4
NEG_BIG = -1e30


def _params(sem, vmem=None):
    return pltpu.CompilerParams(dimension_semantics=sem, vmem_limit_bytes=vmem)


def _rms(x, g):
    return x * lax.rsqrt(jnp.mean(x * x, axis=-1, keepdims=True) + EPS) * g


ROW_WORDS = (SUBLANES // 2, LANES)


def _pack_rows(x_bf16):
    return pltpu.bitcast(x_bf16.reshape(x_bf16.shape[0], SUBLANES, LANES), jnp.int32)


def _unpack_rows(words):
    return pltpu.bitcast(words, BF16).reshape(words.shape[0], D_MODEL)


def _const_spec(shape):
    nd = len(shape)
    return pl.BlockSpec(shape, lambda *_: (0,) * nd)


def _adaln_kernel(c_ref, w_ref, b_ref, o_ref):
    c = c_ref[...]
    s = c * jax.nn.sigmoid(c)
    o_ref[...] = jnp.dot(s, w_ref[...], preferred_element_type=F32, precision=HIGHEST) + b_ref[...]


def adaln(cond8, ada_w, ada_b):
    d, n = ada_w.shape
    bn = n // 4
    m = pl.pallas_call(
        _adaln_kernel,
        out_shape=jax.ShapeDtypeStruct((SUBLANES, n), F32),
        grid=(n // bn,),
        in_specs=[_const_spec((SUBLANES, d)),
                  pl.BlockSpec((d, bn), lambda j: (0, j)),
                  pl.BlockSpec((1, bn), lambda j: (0, j))],
        out_specs=pl.BlockSpec((SUBLANES, bn), lambda j: (0, j)),
        compiler_params=_params(("arbitrary",), VMEM_LIMIT),
        name="adaln",
    )(cond8, ada_w, ada_b.reshape(1, n))
    m = m.reshape(SUBLANES, N_MOD, d)
    return jnp.pad(m, ((0, 0), (0, SUBLANES - N_MOD), (0, 0)))


def _l0_inproj_kernel(*refs, rope):
    if rope:
        (x_ref, m_ref, n1_ref, win_ref, qg_ref, kvg_ref, wuq_ref, wuk_ref, e_ref, wuv_ref,
         cq_ref, sq_ref, ck_ref, sk_ref, q_out, k_out, v_out, ckv_out, kr_out, u_out) = refs
    else:
        (x_ref, m_ref, n1_ref, win_ref, qg_ref, kvg_ref, wuq_ref, wuk_ref, e_ref, wuv_ref,
         q_out, k_out, v_out, ckv_out, kr_out, u_out) = refs
    x = x_ref[0]
    m = m_ref[0]
    h = _rms(x, n1_ref[...]) * (1.0 + m[1:2]) + m[0:1]
    z = jnp.dot(h.astype(BF16), win_ref[...], preferred_element_type=F32)
    c_q = z[:, 0:MLA_Q_RANK]
    c_kv = z[:, MLA_Q_RANK:MLA_Q_RANK + MLA_KV_RANK]
    kr_blk = z[:, 640:768]
    val = z[:, 768:768 + CONV_CH]
    gate = z[:, 768 + CONV_CH:768 + 2 * CONV_CH]

    cqn = _rms(c_q, qg_ref[...]).astype(BF16)
    q = jnp.dot(cqn, wuq_ref[...], preferred_element_type=F32)
    if rope:
        cq = cq_ref[...]
        sq = sq_ref[...]
        half = MLA_ROPE // 2
        lane = lax.broadcasted_iota(jnp.int32, (1, LANES), 1)
        first_half = (lane >= MLA_NOPE) & (lane < MLA_NOPE + half)
        for hd in range(MLA_HEADS):
            sl = slice(hd * LANES, (hd + 1) * LANES)
            blk = q[:, sl]
            partner = jnp.where(first_half, -pltpu.roll(blk, LANES - half, 1), pltpu.roll(blk, half, 1))
            q_out[0, :, sl] = ((blk * cq + partner * sq) * MLA_SCALE).astype(BF16)
        kr = kr_blk * ck_ref[...] + pltpu.roll(kr_blk, LANES - MLA_ROPE, 1) * sk_ref[...]
    else:
        q_out[0] = (q * MLA_SCALE).astype(BF16)
        kr = kr_blk

    ckv = _rms(c_kv, kvg_ref[...])
    ckv_out[0] = ckv
    kr_out[0] = kr_blk[:, 0:MLA_ROPE]
    ckv_b = ckv.astype(BF16)
    k = (jnp.dot(ckv_b, wuk_ref[...], preferred_element_type=F32)
         + jnp.dot(kr.astype(BF16), e_ref[...], preferred_element_type=F32))
    k_out[0] = k.astype(BF16)
    v_out[0] = _with_sum_lane(jnp.dot(ckv_b, wuv_ref[...], preferred_element_type=F32)).astype(BF16)
    u_out[0] = val * jax.nn.sigmoid(gate)


def l0_inproj(x, mods, mod_off, n1, w, tables):
    bm, sm, d = x.shape
    tr = min(ROW_TILE, sm)
    rope = tables is not None
    hp = MLA_HEADS * LANES
    row = lambda width: pl.BlockSpec((1, tr, width), lambda b, i: (b, i, 0))
    in_specs = [row(d),
                pl.BlockSpec((1, SUBLANES, d), lambda b, i: (b + mod_off, 0, 0)),
                _const_spec((1, d)), _const_spec(w["win"].shape),
                _const_spec((1, MLA_Q_RANK)), _const_spec((1, MLA_KV_RANK)),
                _const_spec(w["wuq"].shape)]
    args = [x, mods, n1, w["win"], w["qg"], w["kvg"], w["wuq"]]
    in_specs +=[_const_spec(w["wuk"].shape), _const_spec(w["e"].shape), _const_spec(w["wuv"].shape)]
    args += [w["wuk"], w["e"], w["wuv"]]
    if rope:
        in_specs += [pl.BlockSpec((tr, LANES), lambda b, i: (i, 0))] * 4
        args += list(tables)
    out_shape = [jax.ShapeDtypeStruct((bm, sm, hp), BF16),
                 jax.ShapeDtypeStruct((bm, sm, hp), BF16),
                 jax.ShapeDtypeStruct((bm, sm, hp), BF16),
                 jax.ShapeDtypeStruct((bm, sm, MLA_KV_RANK), F32),
                 jax.ShapeDtypeStruct((bm, sm, MLA_ROPE), F32),
                 jax.ShapeDtypeStruct((bm, sm, CONV_CH), F32)]
    out_specs = [row(hp), row(hp), row(hp), row(MLA_KV_RANK), row(MLA_ROPE), row(CONV_CH)]
    return pl.pallas_call(
        functools.partial(_l0_inproj_kernel, rope=rope),
        out_shape=out_shape, grid=(bm, sm // tr), in_specs=in_specs, out_specs=out_specs,
        compiler_params=_params(("parallel", "parallel"), VMEM_LIMIT),
        name="l0_inproj_rope" if rope else "l0_inproj",
    )(*args)


def _mla_ctx_kv_kernel(ckv_ref, kr_ref, wuk_ref, e_ref, wuv_ref, k_out, v_out):
    ckv_b = ckv_ref[0].astype(BF16)
    k = (jnp.dot(ckv_b, wuk_ref[...], preferred_element_type=F32)
         + jnp.dot(kr_ref[0].astype(BF16), e_ref[...], preferred_element_type=F32))
    k_out[0] = k.astype(BF16)
    v_out[0] = _with_sum_lane(jnp.dot(ckv_b, wuv_ref[...], preferred_element_type=F32)).astype(BF16)


def mla_ctx_kv(ckv, kr128, w):
    b, s, _ = ckv.shape
    hp = MLA_HEADS * LANES
    blk = lambda width: pl.BlockSpec((1, s, width), lambda i: (i, 0, 0))
    return pl.pallas_call(
        _mla_ctx_kv_kernel,
        out_shape=[jax.ShapeDtypeStruct((b, s, hp), BF16)] * 2,
        grid=(b,),
        in_specs=[blk(MLA_KV_RANK), blk(LANES), _const_spec(w["wuk"].shape),
                  _const_spec(w["e"].shape), _const_spec(w["wuv"].shape)],
        out_specs=[blk(hp), blk(hp)],
        compiler_params=_params(("parallel",), VMEM_LIMIT),
        name="mla_ctx_kv",
    )(ckv, kr128, w["wuk"], w["e"], w["wuv"])


def _conv_kernel(prev_ref, cur_ref, next_ref, w_ref, b_ref, g_ref, beta_ref, o_ref, pad_ref, sh_ref, *, rb):
    i = pl.program_id(1)
    last = pl.num_programs(1) - 1
    zeros = jnp.zeros((CONV_HALO, CONV_CH), F32)
    pad_ref[0:CONV_HALO, :] = jnp.where(i == 0, zeros, prev_ref[0])
    pad_ref[CONV_HALO:CONV_HALO + rb, :] = cur_ref[0]
    pad_ref[CONV_HALO + rb:CONV_HALO + rb + CONV_HALO, :] = jnp.where(i == last, zeros, next_ref[0])
    span = rb + 2 * CONV_HALO - SUBLANES
    for r in range(1, SUBLANES):
        sh_ref[r] = pad_ref[r:r + span, :]
    w = w_ref[...]
    shift = CONV_HALO - CONV_WIDTH // 2
    acc = jnp.zeros((rb, CONV_CH), F32) + b_ref[...]
    for k in range(CONV_WIDTH):
        off = k + shift
        r, a = off % SUBLANES, off // SUBLANES * SUBLANES
        window = pad_ref[a:a + rb, :] if r == 0 else sh_ref[r, a:a + rb, :]
        acc = acc + window * w[k:k + 1, :]
    mu = jnp.mean(acc, axis=-1, keepdims=True)
    cen = acc - mu
    var = jnp.mean(cen * cen, axis=-1, keepdims=True)
    y = cen * lax.rsqrt(var + EPS) * g_ref[...] + beta_ref[...]
    o_ref[0] = (y * jax.nn.sigmoid(y)).astype(BF16)


def conformer_conv(u, conv_w, conv_b, ln_g, ln_b):
    b, s, c = u.shape
    rb = min(256, s)
    nh = rb // CONV_HALO
    n_halo_blocks = s // CONV_HALO
    wpad = jnp.pad(conv_w.reshape(CONV_WIDTH, c), ((0, 32 - CONV_WIDTH), (0, 0)))
    return pl.pallas_call(
        functools.partial(_conv_kernel, rb=rb),
        out_shape=jax.ShapeDtypeStruct((b, s, c), BF16),
        grid=(b, s // rb),
        in_specs=[pl.BlockSpec((1, CONV_HALO, c), lambda bi, i: (bi, jnp.maximum(i * nh - 1, 0), 0)),
                  pl.BlockSpec((1, rb, c), lambda bi, i: (bi, i, 0)),
                  pl.BlockSpec((1, CONV_HALO, c),
                               lambda bi, i: (bi, jnp.minimum((i + 1) * nh, n_halo_blocks - 1), 0)),
                  _const_spec((32, c)), _const_spec((1, c)), _const_spec((1, c)), _const_spec((1, c))],
        out_specs=pl.BlockSpec((1, rb, c), lambda bi, i: (bi, i, 0)),
        scratch_shapes=[pltpu.VMEM((rb + 2 * CONV_HALO, c), F32),
                        pltpu.VMEM((SUBLANES, rb + 2 * CONV_HALO - SUBLANES, c), F32)],
        compiler_params=_params(("parallel", "parallel"), VMEM_LIMIT),
        name="conformer_conv",
    )(u, u, u, wpad, conv_b.reshape(1, c), ln_g.reshape(1, c), ln_b.reshape(1, c))


def _with_sum_lane(v):
    lane = lax.broadcasted_iota(jnp.int32, (1, v.shape[-1]), 1)
    return v + ((lane & (LANES - 1)) == V_SUM_LANE).astype(v.dtype)


def _attn_kernel(q_ref, k_ref, v_ref, o_ref, *, heads, rep, mxu_denominator):
    for hd in range(heads):
        g = hd // rep
        q = q_ref[0, :, hd * LANES:(hd + 1) * LANES]
        k = k_ref[0, :, g * LANES:(g + 1) * LANES]
        s = lax.dot_general(q, k, (((1,), (1,)), ((), ())), preferred_element_type=F32)
        m = jnp.max(s, axis=-1, keepdims=True)
        if mxu_denominator:
            p = jnp.exp((s - m).astype(BF16))
            o = jnp.dot(p, v_ref[0, :, g * LANES:(g + 1) * LANES], preferred_element_type=F32)
            l = o[:, V_SUM_LANE:V_SUM_LANE + 1]
        else:
            p = jnp.exp(s - m)
            l = jnp.sum(p, axis=-1, keepdims=True)
            o = jnp.dot(p.astype(BF16), v_ref[0, :, g * LANES:(g + 1) * LANES], preferred_element_type=F32)
        o_ref[0, :, hd * LANES:(hd + 1) * LANES] = (o / l).astype(BF16)


def attention(q, k, v, *, n_heads, n_kv, heads_per_step):
    b, sq, _ = q.shape
    sk = k.shape[1]
    rep = n_heads // n_kv
    tq = min(ATT_Q_TILE, sq)
    hb = heads_per_step
    grid = (b, n_heads // hb, sq // tq)
    if hb >= rep:
        kv_spec = pl.BlockSpec((1, sk, hb // rep * LANES), lambda bi, h, i: (bi, 0, h))
        kern_rep = rep
    else:
        assert rep % hb == 0
        kv_spec = pl.BlockSpec((1, sk, LANES), lambda bi, h, i: (bi, 0, h * hb // rep))
        kern_rep = hb
    kern = functools.partial(_attn_kernel, heads=hb, rep=kern_rep, mxu_denominator=sk >= ATT_LONG_KEYS)
    q_spec = pl.BlockSpec((1, tq, heads_per_step * LANES), lambda bi, h, i: (bi, i, h))
    return pl.pallas_call(
        kern,
        out_shape=jax.ShapeDtypeStruct(q.shape, BF16),
        grid=grid, in_specs=[q_spec, kv_spec, kv_spec], out_specs=q_spec,
        compiler_params=_params(("parallel", "parallel", "parallel"), VMEM_LIMIT),
        name="attention",
    )(q, k, v)


def _l1_inproj_kernel(*refs, rope):
    if rope:
        (x_ref, m_ref, n1_ref, win_ref, qg_ref, kg_ref, lng_ref, lnb_ref,
         ws_ref, bs_ref, c_ref, s_ref, q_out, k_out, vp_out, g_out) = refs
        kt_out = vt_out = None
    else:
        (x_ref, m_ref, n1_ref, win_ref, qg_ref, kg_ref, lng_ref, lnb_ref,
         ws_ref, bs_ref, q_out, k_out, vp_out, g_out, kt_out, vt_out) = refs
    x = x_ref[0]
    m = m_ref[0]
    hb = (_rms(x, n1_ref[...]) * (1.0 + m[1:2]) + m[0:1]).astype(BF16)
    z = jnp.dot(hb, win_ref[...], preferred_element_type=F32)
    qw = GQA_HEADS * LANES
    kw = GQA_KV_HEADS * LANES
    o_k, o_vp, o_v, o_u, o_vg = qw, qw + kw, qw + 2 * kw, qw + 2 * kw + LANES, qw + 2 * kw + LANES + GMLP_CH
    if rope:
        cos = c_ref[...]
        sin = s_ref[...]
        half = GQA_HEAD_DIM // 2
        first_half = lax.broadcasted_iota(jnp.int32, (1, LANES), 1) < half

    def head(col, g_ref):
        t = z[:, col:col + LANES]
        r = lax.rsqrt(jnp.sum(t * t, axis=-1, keepdims=True) * (1.0 / GQA_HEAD_DIM) + EPS)
        normed = t * r * g_ref[...]
        if not rope:
            return normed, normed
        partner = jnp.where(first_half, -pltpu.roll(normed, LANES - half, 1), pltpu.roll(normed, half, 1))
        return normed, normed * cos + partner * sin

    for hd in range(GQA_HEADS):
        _, rot = head(hd * LANES, qg_ref)
        q_out[0, :, hd * LANES:(hd + 1) * LANES] = (rot * GQA_SCALE).astype(BF16)
    for hd in range(GQA_KV_HEADS):
        normed, rot = head(o_k + hd * LANES, kg_ref)
        k_out[0, :, hd * LANES:(hd + 1) * LANES] = rot.astype(BF16)
        if kt_out is not None:
            seq = kt_out.shape[-1]
            for s in range(kt_out.shape[0]):
                kt_out[s, hd] = normed[s * seq:(s + 1) * seq, :].T[:GQA_HEAD_DIM, :]
    vp_out[0] = _with_sum_lane(z[:, o_vp:o_vp + kw]).astype(BF16)
    if vt_out is not None:
        seq = vt_out.shape[-1]
        for s in range(vt_out.shape[0]):
            vt_out[s] = z[s * seq:(s + 1) * seq, o_v:o_v + LANES].T

    u = z[:, o_u:o_u + GMLP_CH]
    vg = z[:, o_vg:o_vg + GMLP_CH]
    mu = jnp.mean(vg, axis=-1, keepdims=True)
    cen = vg - mu
    var = jnp.mean(cen * cen, axis=-1, keepdims=True)
    vn = (cen * lax.rsqrt(var + EPS) * lng_ref[...] + lnb_ref[...]).astype(BF16)
    bias = bs_ref[...]
    rows = x.shape[0]
    for cidx in range(rows // CHUNK):
        r0 = cidx * CHUNK
        for g in range(GMLP_GROUPS):
            c0 = g * LANES
            mixed = jnp.dot(ws_ref[g], vn[r0:r0 + CHUNK, c0:c0 + LANES], preferred_element_type=F32)
            g_out[0, r0:r0 + CHUNK, c0:c0 + LANES] = (
                u[r0:r0 + CHUNK, c0:c0 + LANES] * (mixed + bias[:, c0:c0 + LANES])).astype(BF16)


def l1_inproj(x, mods, mod_off, n1, w, tables, ctx_seq=None):
    bm, sm, d = x.shape
    tr = min(ROW_TILE, sm)
    rope = tables is not None
    assert rope != (ctx_seq is not None)
    qw = GQA_HEADS * LANES
    kw = GQA_KV_HEADS * LANES
    row = lambda width: pl.BlockSpec((1, tr, width), lambda b, i: (b, i, 0))
    vec = _const_spec((1, LANES))
    in_specs = [row(d), pl.BlockSpec((1, SUBLANES, d), lambda b, i: (b + mod_off, 0, 0)),
                _const_spec((1, d)), _const_spec(w["win"].shape)]
    args = [x, mods, n1, w["win"]]
    in_specs += [vec, vec]
    args += [w["qg"], w["kg"]]
    in_specs += [_const_spec((1, GMLP_CH)), _const_spec((1, GMLP_CH)),
                 _const_spec(w["ws"].shape), _const_spec((CHUNK, GMLP_CH))]
    args += [w["lng"], w["lnb"], w["ws"], w["bs"]]
    if rope:
        in_specs += [pl.BlockSpec((tr, LANES), lambda b, i: (i, 0))] * 2
        args += list(tables)
    out_shape = [jax.ShapeDtypeStruct((bm, sm, qw), BF16),
                 jax.ShapeDtypeStruct((bm, sm, kw), BF16),
                 jax.ShapeDtypeStruct((bm, sm, kw), BF16),
                 jax.ShapeDtypeStruct((bm, sm, GMLP_CH), BF16)]
    out_specs = [row(qw), row(kw), row(kw), row(GMLP_CH)]
    if not rope:
        assert bm == 1 and tr % ctx_seq == 0
        n_seq, per_step = sm // ctx_seq, tr // ctx_seq
        out_shape += [jax.ShapeDtypeStruct((n_seq, GQA_KV_HEADS, GQA_HEAD_DIM, ctx_seq), F32),
                      jax.ShapeDtypeStruct((n_seq, GQA_KV_HEADS * GQA_HEAD_DIM, ctx_seq), F32)]
        out_specs += [pl.BlockSpec((per_step, GQA_KV_HEADS, GQA_HEAD_DIM, ctx_seq), lambda b, i: (i, 0, 0, 0)),
                      pl.BlockSpec((per_step, GQA_KV_HEADS * GQA_HEAD_DIM, ctx_seq), lambda b, i: (i, 0, 0))]
    return pl.pallas_call(
        functools.partial(_l1_inproj_kernel, rope=rope),
        out_shape=out_shape, grid=(bm, sm // tr), in_specs=in_specs, out_specs=out_specs,
        compiler_params=_params(("parallel", "parallel"), VMEM_LIMIT),
        name="l1_inproj_rope" if rope else "l1_inproj",
    )(*args)


def _post_kernel(a_ref, c_ref, x_ref, m_ref, n2_ref, wa_ref, wc_ref, rwh_ref, rwl_ref, rb_ref, tri_ref, base_ref,
                 x1_out, xm_out, idx_out, wts_out, rank_out, cnt_out, run_ref):
    first =(pl.program_id(0) == 0) & (pl.program_id(1) == 0)

    @pl.when(first)
    def _():
        run_ref[...] = base_ref[...]

    sub = tri_ref.shape[0]
    running = run_ref[0:1, :]
    for r0 in range(0, a_ref.shape[1], sub):
        running = _route_rows(slice(r0, r0 + sub), running, a_ref, c_ref, x_ref, m_ref, n2_ref, wa_ref, wc_ref,
                              rwh_ref, rwl_ref, rb_ref, tri_ref, x1_out, xm_out, idx_out, wts_out, rank_out)
    run_ref[0:1, :] = running
    cnt_out[...] = run_ref[...]


def _route_rows(rs, running, a_ref, c_ref, x_ref, m_ref, n2_ref, wa_ref, wc_ref, rwh_ref, rwl_ref, rb_ref, tri_ref,
                x1_out, xm_out, idx_out, wts_out, rank_out):
    m = m_ref[0]
    y = (jnp.dot(a_ref[0, rs, :], wa_ref[...], preferred_element_type=F32)
         + jnp.dot(c_ref[0, rs, :], wc_ref[...], preferred_element_type=F32))
    x1 = x_ref[0, rs, :] + m[2:3] * y
    x1_out[0, rs, :] = x1
    xm = _rms(x1, n2_ref[...]) * (1.0 + m[4:5]) + m[3:4]
    xh = xm.astype(BF16)
    xm_out[0, rs] = _pack_rows(xh)

    xl = (xm - xh.astype(F32)).astype(BF16)
    logits = (jnp.dot(xh, rwh_ref[...], preferred_element_type=F32)
              + jnp.dot(xl, rwh_ref[...], preferred_element_type=F32)
              + jnp.dot(xh, rwl_ref[...], preferred_element_type=F32)) + rb_ref[...]
    rows = logits.shape[0]
    lane = lax.broadcasted_iota(jnp.int32, (rows, LANES), 1).astype(F32)
    work = logits
    vals, hots = [], []
    idx_acc = jnp.zeros((rows, LANES), F32)
    for k in range(TOP_K):
        top = jnp.max(work, axis=-1, keepdims=True)
        sel = jnp.min(jnp.where(work == top, lane, float(LANES)), axis=-1, keepdims=True)
        hot = lane == sel
        vals.append(top)
        hots.append(hot)
        idx_acc = idx_acc + jnp.where(lane == float(k), sel, 0.0)
        work = jnp.where(hot, -jnp.inf, work)
    exps = [jnp.exp(v - vals[0]) for v in vals]
    denom = exps[0] + exps[1] + exps[2] + exps[3]
    wcols = [jnp.broadcast_to(exps[k] / denom, (rows, LANES)) for k in range(TOP_K)]
    wcols += [jnp.zeros((rows, LANES), F32)] * (SUBLANES - TOP_K)
    wts = jnp.concatenate(wcols, axis=1).reshape(rows, SUBLANES, LANES)

    chosen = jnp.zeros((rows, LANES), F32)
    for hot in hots:
        chosen = chosen + hot.astype(F32)
    before = jnp.dot(tri_ref[...], chosen.astype(BF16), preferred_element_type=F32) + running
    rank = jnp.zeros((rows, LANES), F32)
    for k in range(TOP_K):
        rk = jnp.sum(jnp.where(hots[k], before, 0.0), axis=-1, keepdims=True)
        rank = rank + jnp.where(lane == float(k), rk, 0.0)
    idx_out[:, rs] = idx_acc.T[:SUBLANES, :].astype(jnp.int32)
    wts_out[0, rs] = wts
    rank_out[:, rs] = rank.T[:SUBLANES, :].astype(jnp.int32)
    return running + jnp.sum(chosen, axis=0, keepdims=True)


def post_mixer(attn, other, x, mods, mod_off, n2, wa, wc, rwh, rwl, rb, tri, base):
    bm, sm, d = x.shape
    tr = POST_SUBTILES * tri.shape[0]
    assert sm % tr == 0
    row = lambda width: pl.BlockSpec((1, tr, width), lambda b, i: (b, i, 0))
    tile_rows = pl.BlockSpec((1, tr) + ROW_WORDS, lambda b, i: (b, i, 0, 0))
    nb = sm // tr
    token_minor = pl.BlockSpec((SUBLANES, tr), lambda b, i: (0, b * nb + i))
    out_shape = [jax.ShapeDtypeStruct((bm, sm, d), F32),
                 jax.ShapeDtypeStruct((bm, sm) + ROW_WORDS, jnp.int32),
                 jax.ShapeDtypeStruct((SUBLANES, bm * sm), jnp.int32),
                 jax.ShapeDtypeStruct((bm, sm, SUBLANES, LANES), F32),
                 jax.ShapeDtypeStruct((SUBLANES, bm * sm), jnp.int32),
                 jax.ShapeDtypeStruct((SUBLANES, LANES), F32)]
    return pl.pallas_call(
        _post_kernel,
        out_shape=out_shape, grid=(bm, sm // tr),
        in_specs=[row(attn.shape[-1]), row(other.shape[-1]), row(d),
                  pl.BlockSpec((1, SUBLANES, d), lambda b, i: (b + mod_off, 0, 0)),
                  _const_spec((1, d)), _const_spec(wa.shape), _const_spec(wc.shape),
                  _const_spec(rwh.shape), _const_spec(rwl.shape), _const_spec((1, LANES)),
                  _const_spec(tri.shape), _const_spec((SUBLANES, LANES))],
        out_specs=[row(d), tile_rows, token_minor,
                   pl.BlockSpec((1, tr, SUBLANES, LANES), lambda b, i: (b, i, 0, 0)),
                   token_minor, _const_spec((SUBLANES, LANES))],
        scratch_shapes=[pltpu.VMEM((SUBLANES, LANES), F32)],
        compiler_params=_params(("arbitrary", "arbitrary"), VMEM_LIMIT),
        name="post_mixer_route",
    )(attn, other, x, mods, n2, wa, wc, rwh, rwl, rb, tri, base)


def _plan_kernel(off_ref, idx_ref, rank_ref, pos_out):
    idx = idx_ref[...]
    pos = rank_ref[...]
    for e in range(N_EXPERTS):
        pos = pos + jnp.where(idx == e, off_ref[e], 0)
    pos_out[...] = pos


def plan_positions(offsets, idx, rank):
    n = idx.shape[1]
    tr = min(2048, n)
    spec = pl.BlockSpec((SUBLANES, tr), lambda i, off: (0, i))
    return pl.pallas_call(
        _plan_kernel,
        out_shape=jax.ShapeDtypeStruct((SUBLANES, n), jnp.int32),
        grid_spec=pltpu.PrefetchScalarGridSpec(
            num_scalar_prefetch=1, grid=(n // tr,), in_specs=[spec, spec], out_specs=spec),
        compiler_params=_params(("parallel",)),
        name="plan_positions",
    )(offsets, idx, rank)


def _sc_worker_id():
    return lax.axis_index("s") * SC_CORES + lax.axis_index("c")


def _sc_chunk_rows(tile, dtype):
    row_bytes = math.prod(tile) * jnp.dtype(dtype).itemsize
    return min(SC_CHUNK_BYTES // row_bytes, SC_MAX_INDICES)


def dispatch_rows(xms, posk, n_rows):
    n_total = sum(x.shape[0] for x in xms)
    tile, dtype = xms[0].shape[1:], xms[0].dtype
    chunk = _sc_chunk_rows(tile, dtype)
    starts, s0 = [], 0
    for x in xms:
        assert x.shape[0] % (SC_WORKERS * chunk) == 0
        starts.append(s0)
        s0 += x.shape[0]
    mesh = plsc.VectorSubcoreMesh(core_axis_name="c", subcore_axis_name="s")

    @functools.partial(
        pl.kernel, mesh=mesh, out_type=jax.ShapeDtypeStruct((n_rows,) + tile, dtype),
        scratch_types=[pltpu.VMEM((chunk,), jnp.int32), pltpu.VMEM((chunk,) + tile, dtype),
                       pltpu.SemaphoreType.DMA],
        name="dispatch_rows_sc")
    def scatter(*refs):
        x_refs, pos_hbm, xs_hbm, idx_v, rows_v, sem = refs[:len(xms)], *refs[len(xms):]
        wid = _sc_worker_id()
        for x_hbm, start in zip(x_refs, starts):
            per_worker = x_hbm.shape[0] // SC_WORKERS

            @pl.loop(0, per_worker // chunk)
            def _(c):
                t0 = wid * per_worker + c * chunk
                pltpu.sync_copy(x_hbm.at[pl.ds(t0, chunk)], rows_v)
                for k in range(TOP_K):
                    pltpu.sync_copy(pos_hbm.at[pl.ds(k * n_total + start + t0, chunk)], idx_v)
                    pltpu.async_copy(rows_v, xs_hbm.at[idx_v], sem).wait()

    return scatter(*xms, posk)


def gather_rows(y, posk):
    n_pairs = posk.shape[0]
    tile, dtype = y.shape[1:], y.dtype
    chunk = _sc_chunk_rows(tile, dtype)
    per_worker = n_pairs // SC_WORKERS
    assert per_worker % chunk == 0
    mesh = plsc.VectorSubcoreMesh(core_axis_name="c", subcore_axis_name="s")

    @functools.partial(
        pl.kernel, mesh=mesh, out_type=jax.ShapeDtypeStruct((n_pairs,) + tile, dtype),
        scratch_types=[pltpu.VMEM((chunk,), jnp.int32), pltpu.VMEM((chunk,) + tile, dtype),
                       pltpu.SemaphoreType.DMA],
        name="gather_rows_sc")
    def gather(y_hbm, pos_hbm, out_hbm, idx_v, rows_v, sem):
        wid = _sc_worker_id()

        @pl.loop(0, per_worker // chunk)
        def _(c):
            base = wid * per_worker + c * chunk
            pltpu.sync_copy(pos_hbm.at[pl.ds(base, chunk)], idx_v)
            pltpu.async_copy(y_hbm.at[idx_v], rows_v, sem).wait()
            pltpu.sync_copy(rows_v, out_hbm.at[pl.ds(base, chunk)])

    return gather(y, posk)


def _ffn_kernel(te_ref, nu_ref, first_ref, slot_ref, nxt_ref, xs_ref, w1_hbm, b1_ref, w2_hbm, b2_ref, y_ref,
                w1f, w2f, w1b, w2b, sem):
    i = pl.program_id(0)

    def weight_copies(e, s):
        return (pltpu.make_async_copy(w1_hbm.at[e], w1f.at[s], sem.at[0, s]),
                pltpu.make_async_copy(w2_hbm.at[e], w2f.at[s], sem.at[1, s]))

    @pl.when(i < nu_ref[0])
    def _():
        s = slot_ref[i]

        @pl.when(first_ref[i] == 1)
        def _():
            @pl.when(i == 0)
            def _():
                for cp in weight_copies(te_ref[i], s):
                    cp.start()
            for cp in weight_copies(te_ref[i], s):
                cp.wait()

            @pl.when(nxt_ref[i] >= 0)
            def _():
                for cp in weight_copies(nxt_ref[i], 1 - s):
                    cp.start()
            for c in range(D_MODEL // LANES):
                w1b[c * LANES:(c + 1) * LANES, :] = w1f[s, c * LANES:(c + 1) * LANES, :].astype(BF16)
            for c in range(D_EXPERT // LANES):
                w2b[c * LANES:(c + 1) * LANES, :] = w2f[s, c * LANES:(c + 1) * LANES, :].astype(BF16)

        x = _unpack_rows(xs_ref[...])
        h = jnp.dot(x, w1b[...], preferred_element_type=F32) + b1_ref[0]
        g = jnp.minimum(h[:, :D_EXPERT], SWIGLU_LIMIT)
        lin = jnp.clip(h[:, D_EXPERT:], -SWIGLU_LIMIT, SWIGLU_LIMIT)
        a = (lin + 1.0) * (g * jax.nn.sigmoid(SWIGLU_ALPHA * g))
        y = jnp.dot(a.astype(BF16), w2b[...], preferred_element_type=F32) + b2_ref[0]
        y_ref[...] = _pack_rows(y.astype(BF16))

    @pl.when(i >= nu_ref[0])
    def _():
        y_ref[...] = jnp.zeros(y_ref.shape, y_ref.dtype)


def grouped_ffn(sched, xs, w1, b1, w2, b2):
    r = xs.shape[0]
    d = w1.shape[1]
    nt = r // FFN_TILE
    tile = (FFN_TILE,) + xs.shape[1:]
    rows = lambda i, te, nu, *_: (jnp.minimum(i, nu[0] - 1), 0, 0)
    bsel = lambda i, te, *_: (te[i], 0, 0)
    return pl.pallas_call(
        _ffn_kernel,
        out_shape=jax.ShapeDtypeStruct(xs.shape, xs.dtype),
        grid_spec=pltpu.PrefetchScalarGridSpec(
            num_scalar_prefetch=5, grid=(nt,),
            in_specs=[pl.BlockSpec(tile, rows),
                      pl.BlockSpec(memory_space=pl.ANY),
                      pl.BlockSpec((1, 1, 2 * D_EXPERT), bsel),
                      pl.BlockSpec(memory_space=pl.ANY),
                      pl.BlockSpec((1, 1, d), bsel)],
            out_specs=pl.BlockSpec(tile, lambda i, *_: (i, 0, 0)),
            scratch_shapes=[pltpu.VMEM((2, d, 2 * D_EXPERT), F32), pltpu.VMEM((2, D_EXPERT, d), F32),
                            pltpu.VMEM((d, 2 * D_EXPERT), BF16), pltpu.VMEM((D_EXPERT, d), BF16),
                            pltpu.SemaphoreType.DMA((2, 2))]),
        compiler_params=_params(("arbitrary",), VMEM_LIMIT),
        name="grouped_ffn",
    )(*sched, xs, w1, b1.reshape(N_EXPERTS, 1, -1), w2, b2.reshape(N_EXPERTS, 1, -1))


def _combine_kernel(x1_ref, wts_ref, m_ref, fn_ref, y0_ref, y1_ref, y2_ref, y3_ref, o_ref, *, final):
    w = wts_ref[0]
    rows = lambda ref: pltpu.bitcast(ref[...], BF16).astype(F32)
    acc = w[:, 0:1, :] * rows(y0_ref)
    for k, y_ref in ((1, y1_ref), (2, y2_ref), (3, y3_ref)):
        acc = acc + w[:, k:k + 1, :] * rows(y_ref)
    out = x1_ref[0] + m_ref[0][5:6] * acc.reshape(x1_ref.shape[1], D_MODEL)
    if final:
        out = _rms(out, fn_ref[...])
    o_ref[0] = out


def combine_rows(x1, wts, mods, mod_off, fn, yg, row_off, n_total, *, final):
    bm, sm, d = x1.shape
    tr = min(COMBINE_TILE, sm)
    nb = sm // tr
    row = lambda width: pl.BlockSpec((1, tr, width), lambda b, i: (b, i, 0))
    ysel = lambda k: pl.BlockSpec((tr,) + yg.shape[1:],
                                  lambda b, i: ((k * n_total + row_off) // tr + b * nb + i, 0, 0))
    return pl.pallas_call(
        functools.partial(_combine_kernel, final=final),
        out_shape=jax.ShapeDtypeStruct((bm, sm, d), F32),
        grid=(bm, nb),
        in_specs=[row(d), pl.BlockSpec((1, tr, SUBLANES, LANES), lambda b, i: (b, i, 0, 0)),
                  pl.BlockSpec((1, SUBLANES, d), lambda b, i: (b + mod_off, 0, 0)),
                  _const_spec((1, d))] + [ysel(k) for k in range(TOP_K)],
        out_specs=row(d),
        compiler_params=_params(("parallel", "parallel"), VMEM_LIMIT),
        name="combine_rows",
    )(x1, wts, mods, fn, yg, yg, yg, yg)


def _axial_angles(n_tokens, rot_dim):
    t = jnp.arange(n_tokens)
    rows = (t // GRID_W).astype(F32)
    cols = (t % GRID_W).astype(F32)
    n_freq = rot_dim // 4
    inv = ROPE_THETA ** (-jnp.arange(n_freq, dtype=F32) / n_freq)
    return jnp.concatenate([rows[:, None] * inv, cols[:, None] * inv], axis=-1)


def _lane_table(parts, n):
    cols = []
    for p in parts:
        cols.append(jnp.broadcast_to(jnp.asarray(p, F32), (n, p.shape[-1])) if hasattr(p, "shape") else p)
    return jnp.concatenate(cols, axis=-1)


def _swap_halves(w):
    half = w.shape[-1] // 2
    return jnp.concatenate([-w[..., half:], w[..., :half]], axis=-1)


def _prep_l0(w_in, q_norm, kv_norm, w_uq, w_uk, w_uv, w_out):
    d = w_in.shape[0]
    o_kr = MLA_Q_RANK + MLA_KV_RANK
    kr_cols = w_in[:, o_kr:o_kr + MLA_ROPE]
    win = jnp.concatenate(
        [w_in[:, :o_kr], kr_cols, _swap_halves(kr_cols), jnp.zeros((d, LANES - 2 * MLA_ROPE), F32),
         w_in[:, o_kr + MLA_ROPE:]], axis=1).astype(BF16)
    qk = MLA_NOPE + MLA_ROPE
    wuq3 = w_uq.reshape(MLA_Q_RANK, MLA_HEADS, qk)
    wuq = jnp.pad(wuq3, ((0, 0), (0, 0), (0, LANES - qk))).reshape(MLA_Q_RANK, -1).astype(BF16)
    wuk3 = w_uk.reshape(MLA_KV_RANK, MLA_HEADS, MLA_NOPE)
    wuk = jnp.pad(wuk3, ((0, 0), (0, 0), (0, LANES - MLA_NOPE))).reshape(MLA_KV_RANK, -1).astype(BF16)
    wuv3 = w_uv.reshape(MLA_KV_RANK, MLA_HEADS, MLA_V)
    wuv = jnp.pad(wuv3, ((0, 0), (0, 0), (0, LANES - MLA_V))).reshape(MLA_KV_RANK, -1).astype(BF16)
    eye = jnp.eye(MLA_ROPE, dtype=F32)
    e_head = jnp.concatenate([jnp.zeros((MLA_ROPE, MLA_NOPE), F32), eye,
                              jnp.zeros((MLA_ROPE, LANES - qk), F32)], axis=1)
    e = jnp.pad(jnp.tile(e_head, (1, MLA_HEADS)), ((0, LANES - MLA_ROPE), (0, 0))).astype(BF16)
    wa3 = w_out[:MLA_HEADS * MLA_V].reshape(MLA_HEADS, MLA_V, d)
    wa = jnp.pad(wa3, ((0, 0), (0, LANES - MLA_V), (0, 0))).reshape(MLA_HEADS * LANES, d).astype(BF16)
    wc = w_out[MLA_HEADS * MLA_V:].astype(BF16)
    return dict(win=win, qg=q_norm.reshape(1, -1), kvg=kv_norm.reshape(1, -1), wuq=wuq,
                wuk=wuk, e=e, wuv=wuv, wa=wa, wc=wc)


def _l0_tables(n):
    ang = _axial_angles(n, MLA_ROPE)
    cos, sin = jnp.cos(ang), jnp.sin(ang)
    one = jnp.ones((n, 1), F32)
    zero = jnp.zeros((n, 1), F32)
    rest = LANES - MLA_NOPE - MLA_ROPE
    cq = jnp.concatenate([jnp.tile(one, (1, MLA_NOPE)), cos, cos, jnp.tile(one, (1, rest))], axis=1)
    sq = jnp.concatenate([jnp.tile(zero, (1, MLA_NOPE)), sin, sin, jnp.tile(zero, (1, rest))], axis=1)
    ck = jnp.concatenate([cos, cos, jnp.tile(zero, (1, LANES - MLA_ROPE))], axis=1)
    sk = jnp.concatenate([sin, sin, jnp.tile(zero, (1, LANES - MLA_ROPE))], axis=1)
    return cq, sq, ck, sk


def _pad_heads(w, n_heads, dim):
    d = w.shape[0]
    return jnp.pad(w.reshape(d, n_heads, dim), ((0, 0), (0, 0), (0, LANES - dim))).reshape(d, n_heads * LANES)


def _prep_l1(w_in, q_norm, k_norm, ln_g, ln_b, w_s, b_s, w_out):
    d = w_in.shape[0]
    qd = GQA_HEADS * GQA_HEAD_DIM
    kd = GQA_KV_HEADS * GQA_HEAD_DIM
    wq, wk, wv = w_in[:, :qd], w_in[:, qd:qd + kd], w_in[:, qd + kd:qd + 2 * kd]
    rest = w_in[:, qd + 2 * kd:]
    win = jnp.concatenate([_pad_heads(wq, GQA_HEADS, GQA_HEAD_DIM), _pad_heads(wk, GQA_KV_HEADS, GQA_HEAD_DIM),
                           _pad_heads(wv, GQA_KV_HEADS, GQA_HEAD_DIM), wv, rest], axis=1).astype(BF16)
    padg = lambda g: jnp.pad(g, (0, LANES - GQA_HEAD_DIM)).reshape(1, LANES)
    wa3 = w_out[:qd].reshape(GQA_HEADS, GQA_HEAD_DIM, d)
    wa = jnp.pad(wa3, ((0, 0), (0, LANES - GQA_HEAD_DIM), (0, 0))).reshape(GQA_HEADS * LANES, d).astype(BF16)
    wc = w_out[qd:].astype(BF16)
    bs = jnp.repeat(b_s.T, LANES, axis=1)
    return dict(win=win, qg=padg(q_norm), kg=padg(k_norm), lng=ln_g.reshape(1, -1), lnb=ln_b.reshape(1, -1),
                ws=w_s.astype(BF16), bs=bs, wa=wa, wc=wc)


def _l1_tables(n):
    ang = _axial_angles(n, GQA_HEAD_DIM)
    cos, sin = jnp.cos(ang), jnp.sin(ang)
    pad = LANES - GQA_HEAD_DIM
    c = jnp.concatenate([cos, cos, jnp.ones((n, pad), F32)], axis=1)
    s = jnp.concatenate([sin, sin, jnp.zeros((n, pad), F32)], axis=1)
    return c, s


def _pad_lanes(x, width):
    return jnp.pad(x, [(0, 0)] * (x.ndim - 1) + [(0, width - x.shape[-1])])


def routed_ffn(groups, mods, n2, moe, tri, final_norm, *, final):
    router_w, router_b, w1, b1, w2, b2 = moe
    rw = _pad_lanes(router_w, LANES)
    rwh = rw.astype(BF16)
    rwl = (rw - rwh.astype(F32)).astype(BF16)
    rb = jnp.concatenate([router_b, jnp.full((LANES - N_EXPERTS,), NEG_BIG, F32)]).reshape(1, LANES)
    base = jnp.zeros((SUBLANES, LANES), F32)
    routed = []
    for g in groups:
        x1, xm, idx, wts, rank, base = post_mixer(
            g["attn"], g["other"], g["x"], mods, g["mod_off"], n2, g["wa"], g["wc"], rwh, rwl, rb, tri, base)
        routed.append((x1, xm, idx, wts, rank))
    n_total = sum(r[0].shape[0] * r[0].shape[1] for r in routed)
    nt = n_total * TOP_K // FFN_TILE + N_EXPERTS

    counts = base[0, :N_EXPERTS].astype(jnp.int32)
    tiles = (counts + FFN_TILE - 1) // FFN_TILE
    tile_end = jnp.cumsum(tiles)
    offsets = (tile_end - tiles) * FFN_TILE
    n_used = tile_end[-1:].astype(jnp.int32)
    experts = jnp.arange(N_EXPERTS)
    busy = tiles > 0
    slot_e = (jnp.cumsum(busy) - 1) % 2
    later = jnp.where(busy[None, :] & (experts[None, :] > experts[:, None]), experts[None, :], N_EXPERTS)
    nxt_e = jnp.min(later, axis=1)
    nxt_e = jnp.where(nxt_e == N_EXPERTS, -1, nxt_e)
    tile_expert = jnp.sum(jnp.arange(nt)[:, None] >= tile_end[None, :], axis=1)
    tile_expert = jnp.minimum(tile_expert, jnp.max(jnp.where(busy, experts, 0)))
    first = jnp.concatenate([jnp.ones((1,), bool), tile_expert[1:] != tile_expert[:-1]])
    of_tile = tile_expert[:, None] == experts[None, :]
    slot_t = jnp.sum(jnp.where(of_tile, slot_e[None, :], 0), axis=1)
    nxt_t = jnp.sum(jnp.where(of_tile, nxt_e[None, :], 0), axis=1)
    sched = tuple(a.astype(jnp.int32) for a in (tile_expert, n_used, first, slot_t, nxt_t))

    positions, xms = [], []
    for (x1, xm, idx, wts, rank) in routed:
        positions.append(plan_positions(offsets, idx, rank)[:TOP_K])
        xms.append(xm.reshape((idx.shape[1],) + ROW_WORDS))
    posk = jnp.concatenate(positions, axis=1).reshape(-1)
    xs = dispatch_rows(xms, posk, nt * FFN_TILE)
    y = grouped_ffn(sched, xs, w1, b1, w2, b2)
    outs = []
    for g, (x1, xm, idx, wts, rank), pos in zip(groups, routed, positions):
        yg = gather_rows(y, pos.reshape(-1))
        outs.append(combine_rows(x1, wts, mods, g["mod_off"], final_norm, yg, 0, pos.shape[1], final=final))
    return outs


def kernel(x_prompt, x_sample, cache_l0_ckv, cache_l0_krope, cache_l1_k, cache_l1_v, c, c_ctx,
           l0_ada_w, l0_ada_b, l0_norm1, l0_w_in, l0_q_norm, l0_kv_norm, l0_w_uq, l0_w_uk, l0_w_uv,
           l0_conv_w, l0_conv_b, l0_conv_ln_g, l0_conv_ln_b, l0_w_out, l0_norm2,
           l0_router_w, l0_router_b, l0_w1, l0_b1, l0_w2, l0_b2,
           l1_ada_w, l1_ada_b, l1_norm1, l1_w_in, l1_q_norm, l1_k_norm, l1_gmlp_ln_g, l1_gmlp_ln_b,
           l1_w_s, l1_b_s, l1_w_out, l1_norm2,
           l1_router_w, l1_router_b, l1_w1, l1_b1, l1_w2, l1_b2,
           final_norm):
    bp, sp, d = x_prompt.shape
    bs, ss, _ = x_sample.shape
    past = cache_l0_ckv.shape[1]
    n_p = bp * sp

    cond8 = jnp.concatenate([c_ctx[None], c, jnp.zeros((SUBLANES - 1 - bs, d), F32)], axis=0)
    mods0 = adaln(cond8, l0_ada_w, l0_ada_b)
    mods1 = adaln(cond8, l1_ada_w, l1_ada_b)
    tri = jnp.tril(jnp.ones((ROW_TILE, ROW_TILE), F32), -1).astype(BF16)
    fn = final_norm.reshape(1, d)

    w0 = _prep_l0(l0_w_in, l0_q_norm, l0_kv_norm, l0_w_uq, l0_w_uk, l0_w_uv, l0_w_out)
    n1 = l0_norm1.reshape(1, d)
    hp = x_prompt.reshape(1, n_p, d)
    q_p, k_p, v_p, ckv_p, kr_p, u_p = l0_inproj(hp, mods0, 0, n1, w0, None)
    q_s, k_s, v_s, _, _, u_s = l0_inproj(x_sample, mods0, 1, n1, w0, _l0_tables(ss))
    k_c, v_c = mla_ctx_kv(cache_l0_ckv, _pad_lanes(cache_l0_krope, LANES), w0)
    hw = MLA_HEADS * LANES
    att_p = attention(q_p.reshape(bp, sp, hw), k_p.reshape(bp, sp, hw), v_p.reshape(bp, sp, hw),
                      n_heads=MLA_HEADS, n_kv=MLA_HEADS, heads_per_step=MLA_HEADS)
    att_s = attention(q_s, jnp.concatenate([k_c, k_s], axis=1), jnp.concatenate([v_c, v_s], axis=1),
                      n_heads=MLA_HEADS, n_kv=MLA_HEADS, heads_per_step=LATENT_HEADS_PER_STEP)
    conv_p = conformer_conv(u_p.reshape(bp, sp, CONV_CH), l0_conv_w, l0_conv_b, l0_conv_ln_g, l0_conv_ln_b)
    conv_s = conformer_conv(u_s, l0_conv_w, l0_conv_b, l0_conv_ln_g, l0_conv_ln_b)
    groups = [dict(attn=att_p.reshape(1, n_p, hw), other=conv_p.reshape(1, n_p, CONV_CH), x=hp, mod_off=0,
                   wa=w0["wa"], wc=w0["wc"]),
              dict(attn=att_s, other=conv_s, x=x_sample, mod_off=1, wa=w0["wa"], wc=w0["wc"])]
    hp, hs = routed_ffn(groups, mods0, l0_norm2.reshape(1, d),
                        (l0_router_w, l0_router_b, l0_w1, l0_b1, l0_w2, l0_b2), tri, fn, final=False)
    new_l0_ckv = ckv_p.reshape(bp, sp, MLA_KV_RANK)
    new_l0_krope = kr_p.reshape(bp, sp, MLA_ROPE)

    w1p = _prep_l1(l1_w_in, l1_q_norm, l1_k_norm, l1_gmlp_ln_g, l1_gmlp_ln_b, l1_w_s, l1_b_s, l1_w_out)
    n1 = l1_norm1.reshape(1, d)
    q_p, k_p, vp_p, gat_p, kt_p, vt_p = l1_inproj(hp, mods1, 0, n1, w1p, None, ctx_seq=sp)
    q_s, k_s, vp_s, gat_s = l1_inproj(hs, mods1, 1, n1, w1p, _l1_tables(ss))
    qw = GQA_HEADS * LANES
    kw = GQA_KV_HEADS * LANES
    pad_kv = lambda t: _pad_lanes(t, LANES).reshape(bs, past, kw).astype(BF16)
    att_p = attention(q_p.reshape(bp, sp, qw), k_p.reshape(bp, sp, kw), vp_p.reshape(bp, sp, kw),
                      n_heads=GQA_HEADS, n_kv=GQA_KV_HEADS, heads_per_step=GQA_HEADS)
    att_s = attention(q_s, jnp.concatenate([pad_kv(cache_l1_k), k_s], axis=1),
                      jnp.concatenate([_with_sum_lane(pad_kv(cache_l1_v)), vp_s], axis=1),
                      n_heads=GQA_HEADS, n_kv=GQA_KV_HEADS, heads_per_step=LATENT_HEADS_PER_STEP)
    groups = [dict(attn=att_p.reshape(1, n_p, qw), other=gat_p, x=hp, mod_off=0, wa=w1p["wa"], wc=w1p["wc"]),
              dict(attn=att_s, other=gat_s, x=hs, mod_off=1, wa=w1p["wa"], wc=w1p["wc"])]
    yp, ys = routed_ffn(groups, mods1, l1_norm2.reshape(1, d),
                        (l1_router_w, l1_router_b, l1_w1, l1_b1, l1_w2, l1_b2), tri, fn, final=True)
    new_l1_k = jnp.transpose(kt_p, (0, 3, 1, 2))
    new_l1_v = jnp.transpose(vt_p.reshape(bp, GQA_KV_HEADS, GQA_HEAD_DIM, sp), (0, 3, 1, 2))
    return (yp.reshape(bp, sp, d), ys, new_l0_ckv, new_l0_krope, new_l1_k, new_l1_v)
```

```python
import functools
import math

import jax
import jax.numpy as jnp
from jax import lax
from jax.experimental import pallas as pl
from jax.experimental.pallas import tpu as pltpu
from jax.experimental.pallas import tpu_sc as plsc

F32 = jnp.float32
BF16 = jnp.bfloat16
HIGHEST = lax.Precision.HIGHEST

LANES = 128
SUBLANES = 8
VMEM_LIMIT = 56 * 1024 * 1024

D_MODEL = 1024
GRID_W = 64
ROPE_THETA = 10000.0
EPS = 1e-6
N_MOD = 6

MLA_HEADS = 8
MLA_NOPE = 64
MLA_ROPE = 32
MLA_V = 64
MLA_Q_RANK = 384
MLA_KV_RANK = 256
MLA_SCALE = 1.0 / math.sqrt(MLA_NOPE + MLA_ROPE)
CONV_CH = 512
CONV_WIDTH = 31
CONV_HALO = 16

GQA_HEADS = 8
GQA_KV_HEADS = 2
GQA_HEAD_DIM = 64
GQA_SCALE = 1.0 / math.sqrt(GQA_HEAD_DIM)
CHUNK = 128
GMLP_GROUPS = 4
GMLP_CH = 512

N_EXPERTS = 32
TOP_K = 4
D_EXPERT = 1024
SWIGLU_LIMIT = 7.0
SWIGLU_ALPHA = 1.702

ROW_TILE = 512
POST_SUBTILES = 2
FFN_TILE = 256
SC_CORES = 2
SC_SUBCORES = 16
SC_WORKERS = SC_CORES * SC_SUBCORES
SC_CHUNK_BYTES = 256 * 1024
SC_MAX_INDICES = 128
COMBINE_TILE = 512
ATT_Q_TILE = 256
ATT_LONG_KEYS = 1024
V_SUM_LANE = 64
LATENT_HEADS_PER_STEP = 4
NEG_BIG = -1e30


def _params(sem, vmem=None):
    return pltpu.CompilerParams(dimension_semantics=sem, vmem_limit_bytes=vmem)


def _rms(x, g):
    return x * lax.rsqrt(jnp.mean(x * x, axis=-1, keepdims=True) + EPS) * g


ROW_WORDS = (SUBLANES // 2, LANES)


def _pack_rows(x_bf16):
    return pltpu.bitcast(x_bf16.reshape(x_bf16.shape[0], SUBLANES, LANES), jnp.int32)


def _unpack_rows(words):
    return pltpu.bitcast(words, BF16).reshape(words.shape[0], D_MODEL)


def _const_spec(shape):
    nd = len(shape)
    return pl.BlockSpec(shape, lambda *_: (0,) * nd)


def _adaln_kernel(c_ref, w_ref, b_ref, o_ref):
    c = c_ref[...]
    s = c * jax.nn.sigmoid(c)
    o_ref[...] = jnp.dot(s, w_ref[...], preferred_element_type=F32, precision=HIGHEST) + b_ref[...]


def adaln(cond8, ada_w, ada_b):
    d, n = ada_w.shape
    bn = n // 4
    m = pl.pallas_call(
        _adaln_kernel,
        out_shape=jax.ShapeDtypeStruct((SUBLANES, n), F32),
        grid=(n // bn,),
        in_specs=[_const_spec((SUBLANES, d)),
                  pl.BlockSpec((d, bn), lambda j: (0, j)),
                  pl.BlockSpec((1, bn), lambda j: (0, j))],
        out_specs=pl.BlockSpec((SUBLANES, bn), lambda j: (0, j)),
        compiler_params=_params(("arbitrary",), VMEM_LIMIT),
        name="adaln",
    )(cond8, ada_w, ada_b.reshape(1, n))
    m = m.reshape(SUBLANES, N_MOD, d)
    return jnp.pad(m, ((0, 0), (0, SUBLANES - N_MOD), (0, 0)))


def _l0_inproj_kernel(*refs, rope):
    if rope:
        (x_ref, m_ref, n1_ref, win_ref, qg_ref, kvg_ref, wuq_ref, wuk_ref, e_ref, wuv_ref,
         cq_ref, sq_ref, ck_ref, sk_ref, q_out, k_out, v_out, ckv_out, kr_out, u_out) = refs
    else:
        (x_ref, m_ref, n1_ref, win_ref, qg_ref, kvg_ref, wuq_ref, wuk_ref, e_ref, wuv_ref,
         q_out, k_out, v_out, ckv_out, kr_out, u_out) = refs
    x = x_ref[0]
    m = m_ref[0]
    h = _rms(x, n1_ref[...]) * (1.0 + m[1:2]) + m[0:1]
    z = jnp.dot(h.astype(BF16), win_ref[...], preferred_element_type=F32)
    c_q = z[:, 0:MLA_Q_RANK]
    c_kv = z[:, MLA_Q_RANK:MLA_Q_RANK + MLA_KV_RANK]
    o_kr = MLA_Q_RANK + MLA_KV_RANK
    o_conv = o_kr + LANES
    kr_blk = z[:, o_kr:o_conv]
    val = z[:, o_conv:o_conv + CONV_CH]
    gate = z[:, o_conv + CONV_CH:o_conv + 2 * CONV_CH]

    cqn = _rms(c_q, qg_ref[...]).astype(BF16)
    q = jnp.dot(cqn, wuq_ref[...], preferred_element_type=F32)
    if rope:
        cq = cq_ref[...]
        sq = sq_ref[...]
        half = MLA_ROPE // 2
        lane = lax.broadcasted_iota(jnp.int32, (1, LANES), 1)
        first_half = (lane >= MLA_NOPE) & (lane < MLA_NOPE + half)
        for hd in range(MLA_HEADS):
            sl = slice(hd * LANES, (hd + 1) * LANES)
            blk = q[:, sl]
            partner = jnp.where(first_half, -pltpu.roll(blk, LANES - half, 1), pltpu.roll(blk, half, 1))
            q_out[0, :, sl] = ((blk * cq + partner * sq) * MLA_SCALE).astype(BF16)
        kr = kr_blk * ck_ref[...] + pltpu.roll(kr_blk, LANES - MLA_ROPE, 1) * sk_ref[...]
    else:
        q_out[0] = (q * MLA_SCALE).astype(BF16)
        kr = kr_blk

    ckv = _rms(c_kv, kvg_ref[...])
    ckv_out[0] = ckv
    kr_out[0] = kr_blk[:, 0:MLA_ROPE]
    ckv_b = ckv.astype(BF16)
    k = (jnp.dot(ckv_b, wuk_ref[...], preferred_element_type=F32)
         + jnp.dot(kr.astype(BF16), e_ref[...], preferred_element_type=F32))
    k_out[0] = k.astype(BF16)
    v_out[0] = _with_sum_lane(jnp.dot(ckv_b, wuv_ref[...], preferred_element_type=F32)).astype(BF16)
    u_out[0] = val * jax.nn.sigmoid(gate)


def l0_inproj(x, mods, mod_off, n1, w, tables):
    bm, sm, d = x.shape
    tr = min(ROW_TILE, sm)
    rope = tables is not None
    hp = MLA_HEADS * LANES
    row = lambda width: pl.BlockSpec((1, tr, width), lambda b, i: (b, i, 0))
    in_specs = [row(d),
                pl.BlockSpec((1, SUBLANES, d), lambda b, i: (b + mod_off, 0, 0)),
                _const_spec((1, d)), _const_spec(w["win"].shape),
                _const_spec((1, MLA_Q_RANK)), _const_spec((1, MLA_KV_RANK)),
                _const_spec(w["wuq"].shape)]
    args = [x, mods, n1, w["win"], w["qg"], w["kvg"], w["wuq"]]
    in_specs +=[_const_spec(w["wuk"].shape), _const_spec(w["e"].shape), _const_spec(w["wuv"].shape)]
    args += [w["wuk"], w["e"], w["wuv"]]
    if rope:
        in_specs += [pl.BlockSpec((tr, LANES), lambda b, i: (i, 0))] * 4
        args += list(tables)
    out_shape = [jax.ShapeDtypeStruct((bm, sm, hp), BF16),
                 jax.ShapeDtypeStruct((bm, sm, hp), BF16),
                 jax.ShapeDtypeStruct((bm, sm, hp), BF16),
                 jax.ShapeDtypeStruct((bm, sm, MLA_KV_RANK), F32),
                 jax.ShapeDtypeStruct((bm, sm, MLA_ROPE), F32),
                 jax.ShapeDtypeStruct((bm, sm, CONV_CH), F32)]
    out_specs = [row(hp), row(hp), row(hp), row(MLA_KV_RANK), row(MLA_ROPE), row(CONV_CH)]
    return pl.pallas_call(
        functools.partial(_l0_inproj_kernel, rope=rope),
        out_shape=out_shape, grid=(bm, sm // tr), in_specs=in_specs, out_specs=out_specs,
        compiler_params=_params(("parallel", "parallel"), VMEM_LIMIT),
        name="l0_inproj_rope" if rope else "l0_inproj",
    )(*args)


def _mla_ctx_kv_kernel(ckv_ref, kr_ref, wuk_ref, e_ref, wuv_ref, k_out, v_out):
    ckv_b = ckv_ref[0].astype(BF16)
    k = (jnp.dot(ckv_b, wuk_ref[...], preferred_element_type=F32)
         + jnp.dot(kr_ref[0].astype(BF16), e_ref[...], preferred_element_type=F32))
    k_out[0] = k.astype(BF16)
    v_out[0] = _with_sum_lane(jnp.dot(ckv_b, wuv_ref[...], preferred_element_type=F32)).astype(BF16)


def mla_ctx_kv(ckv, kr128, w):
    b, s, _ = ckv.shape
    hp = MLA_HEADS * LANES
    blk = lambda width: pl.BlockSpec((1, s, width), lambda i: (i, 0, 0))
    return pl.pallas_call(
        _mla_ctx_kv_kernel,
        out_shape=[jax.ShapeDtypeStruct((b, s, hp), BF16)] * 2,
        grid=(b,),
        in_specs=[blk(MLA_KV_RANK), blk(LANES), _const_spec(w["wuk"].shape),
                  _const_spec(w["e"].shape), _const_spec(w["wuv"].shape)],
        out_specs=[blk(hp), blk(hp)],
        compiler_params=_params(("parallel",), VMEM_LIMIT),
        name="mla_ctx_kv",
    )(ckv, kr128, w["wuk"], w["e"], w["wuv"])


def _conv_kernel(prev_ref, cur_ref, next_ref, w_ref, b_ref, g_ref, beta_ref, o_ref, pad_ref, sh_ref, *, rb):
    i = pl.program_id(1)
    last = pl.num_programs(1) - 1
    zeros = jnp.zeros((CONV_HALO, CONV_CH), F32)
    pad_ref[0:CONV_HALO, :] = jnp.where(i == 0, zeros, prev_ref[0])
    pad_ref[CONV_HALO:CONV_HALO + rb, :] = cur_ref[0]
    pad_ref[CONV_HALO + rb:CONV_HALO + rb + CONV_HALO, :] = jnp.where(i == last, zeros, next_ref[0])
    span = rb + 2 * CONV_HALO - SUBLANES
    for r in range(1, SUBLANES):
        sh_ref[r] = pad_ref[r:r + span, :]
    w = w_ref[...]
    shift = CONV_HALO - CONV_WIDTH // 2
    acc = jnp.zeros((rb, CONV_CH), F32) + b_ref[...]
    for k in range(CONV_WIDTH):
        off = k + shift
        r, a = off % SUBLANES, off // SUBLANES * SUBLANES
        window = pad_ref[a:a + rb, :] if r == 0 else sh_ref[r, a:a + rb, :]
        acc = acc + window * w[k:k + 1, :]
    mu = jnp.mean(acc, axis=-1, keepdims=True)
    cen = acc - mu
    var = jnp.mean(cen * cen, axis=-1, keepdims=True)
    y = cen * lax.rsqrt(var + EPS) * g_ref[...] + beta_ref[...]
    o_ref[0] = (y * jax.nn.sigmoid(y)).astype(BF16)


def conformer_conv(u, conv_w, conv_b, ln_g, ln_b):
    b, s, c = u.shape
    rb = min(256, s)
    nh = rb // CONV_HALO
    n_halo_blocks = s // CONV_HALO
    wpad = jnp.pad(conv_w.reshape(CONV_WIDTH, c), ((0, 32 - CONV_WIDTH), (0, 0)))
    return pl.pallas_call(
        functools.partial(_conv_kernel, rb=rb),
        out_shape=jax.ShapeDtypeStruct((b, s, c), BF16),
        grid=(b, s // rb),
        in_specs=[pl.BlockSpec((1, CONV_HALO, c), lambda bi, i: (bi, jnp.maximum(i * nh - 1, 0), 0)),
                  pl.BlockSpec((1, rb, c), lambda bi, i: (bi, i, 0)),
                  pl.BlockSpec((1, CONV_HALO, c),
                               lambda bi, i: (bi, jnp.minimum((i + 1) * nh, n_halo_blocks - 1), 0)),
                  _const_spec((32, c)), _const_spec((1, c)), _const_spec((1, c)), _const_spec((1, c))],
        out_specs=pl.BlockSpec((1, rb, c), lambda bi, i: (bi, i, 0)),
        scratch_shapes=[pltpu.VMEM((rb + 2 * CONV_HALO, c), F32),
                        pltpu.VMEM((SUBLANES, rb + 2 * CONV_HALO - SUBLANES, c), F32)],
        compiler_params=_params(("parallel", "parallel"), VMEM_LIMIT),
        name="conformer_conv",
    )(u, u, u, wpad, conv_b.reshape(1, c), ln_g.reshape(1, c), ln_b.reshape(1, c))


def _with_sum_lane(v):
    lane = lax.broadcasted_iota(jnp.int32, (1, v.shape[-1]), 1)
    return v + ((lane & (LANES - 1)) == V_SUM_LANE).astype(v.dtype)


def _attn_kernel(q_ref, k_ref, v_ref, o_ref, *, heads, rep, mxu_denominator):
    for hd in range(heads):
        g = hd // rep
        q = q_ref[0, :, hd * LANES:(hd + 1) * LANES]
        k = k_ref[0, :, g * LANES:(g + 1) * LANES]
        s = lax.dot_general(q, k, (((1,), (1,)), ((), ())), preferred_element_type=F32)
        m = jnp.max(s, axis=-1, keepdims=True)
        if mxu_denominator:
            p = jnp.exp((s - m).astype(BF16))
            o = jnp.dot(p, v_ref[0, :, g * LANES:(g + 1) * LANES], preferred_element_type=F32)
            l = o[:, V_SUM_LANE:V_SUM_LANE + 1]
        else:
            p = jnp.exp(s - m)
            l = jnp.sum(p, axis=-1, keepdims=True)
            o = jnp.dot(p.astype(BF16), v_ref[0, :, g * LANES:(g + 1) * LANES], preferred_element_type=F32)
        o_ref[0, :, hd * LANES:(hd + 1) * LANES] = (o / l).astype(BF16)


def attention(q, k, v, *, n_heads, n_kv, heads_per_step):
    b, sq, _ = q.shape
    sk = k.shape[1]
    rep = n_heads // n_kv
    tq = min(ATT_Q_TILE, sq)
    hb = heads_per_step
    grid = (b, n_heads // hb, sq // tq)
    if hb >= rep:
        kv_spec = pl.BlockSpec((1, sk, hb // rep * LANES), lambda bi, h, i: (bi, 0, h))
        kern_rep = rep
    else:
        assert rep % hb == 0
        kv_spec = pl.BlockSpec((1, sk, LANES), lambda bi, h, i: (bi, 0, h * hb // rep))
        kern_rep = hb
    kern = functools.partial(_attn_kernel, heads=hb, rep=kern_rep, mxu_denominator=sk >= ATT_LONG_KEYS)
    q_spec = pl.BlockSpec((1, tq, heads_per_step * LANES), lambda bi, h, i: (bi, i, h))
    return pl.pallas_call(
        kern,
        out_shape=jax.ShapeDtypeStruct(q.shape, BF16),
        grid=grid, in_specs=[q_spec, kv_spec, kv_spec], out_specs=q_spec,
        compiler_params=_params(("parallel", "parallel", "parallel"), VMEM_LIMIT),
        name="attention",
    )(q, k, v)


def _l1_inproj_kernel(*refs, rope):
    if rope:
        (x_ref, m_ref, n1_ref, win_ref, qg_ref, kg_ref, lng_ref, lnb_ref,
         ws_ref, bs_ref, c_ref, s_ref, q_out, k_out, vp_out, g_out) = refs
        kt_out = vt_out = None
    else:
        (x_ref, m_ref, n1_ref, win_ref, qg_ref, kg_ref, lng_ref, lnb_ref,
         ws_ref, bs_ref, q_out, k_out, vp_out, g_out, kt_out, vt_out) = refs
    x = x_ref[0]
    m = m_ref[0]
    hb = (_rms(x, n1_ref[...]) * (1.0 + m[1:2]) + m[0:1]).astype(BF16)
    z = jnp.dot(hb, win_ref[...], preferred_element_type=F32)
    qw = GQA_HEADS * LANES
    kw = GQA_KV_HEADS * LANES
    o_k, o_vp, o_v, o_u, o_vg = qw, qw + kw, qw + 2 * kw, qw + 2 * kw + LANES, qw + 2 * kw + LANES + GMLP_CH
    if rope:
        cos = c_ref[...]
        sin = s_ref[...]
        half = GQA_HEAD_DIM // 2
        first_half = lax.broadcasted_iota(jnp.int32, (1, LANES), 1) < half

    def head(col, g_ref):
        t = z[:, col:col + LANES]
        r = lax.rsqrt(jnp.sum(t * t, axis=-1, keepdims=True) * (1.0 / GQA_HEAD_DIM) + EPS)
        normed = t * r * g_ref[...]
        if not rope:
            return normed, normed
        partner = jnp.where(first_half, -pltpu.roll(normed, LANES - half, 1), pltpu.roll(normed, half, 1))
        return normed, normed * cos + partner * sin

    for hd in range(GQA_HEADS):
        _, rot = head(hd * LANES, qg_ref)
        q_out[0, :, hd * LANES:(hd + 1) * LANES] = (rot * GQA_SCALE).astype(BF16)
    for hd in range(GQA_KV_HEADS):
        normed, rot = head(o_k + hd * LANES, kg_ref)
        k_out[0, :, hd * LANES:(hd + 1) * LANES] = rot.astype(BF16)
        if kt_out is not None:
            seq = kt_out.shape[-1]
            for s in range(kt_out.shape[0]):
                kt_out[s, hd] = normed[s * seq:(s + 1) * seq, :].T[:GQA_HEAD_DIM, :]
    vp_out[0] = _with_sum_lane(z[:, o_vp:o_vp + kw]).astype(BF16)
    if vt_out is not None:
        seq = vt_out.shape[-1]
        for s in range(vt_out.shape[0]):
            vt_out[s] = z[s * seq:(s + 1) * seq, o_v:o_v + LANES].T

    u = z[:, o_u:o_u + GMLP_CH]
    vg = z[:, o_vg:o_vg + GMLP_CH]
    mu = jnp.mean(vg, axis=-1, keepdims=True)
    cen = vg - mu
    var = jnp.mean(cen * cen, axis=-1, keepdims=True)
    vn = (cen * lax.rsqrt(var + EPS) * lng_ref[...] + lnb_ref[...]).astype(BF16)
    bias = bs_ref[...]
    rows = x.shape[0]
    for cidx in range(rows // CHUNK):
        r0 = cidx * CHUNK
        for g in range(GMLP_GROUPS):
            c0 = g * LANES
            mixed = jnp.dot(ws_ref[g], vn[r0:r0 + CHUNK, c0:c0 + LANES], preferred_element_type=F32)
            g_out[0, r0:r0 + CHUNK, c0:c0 + LANES] = (
                u[r0:r0 + CHUNK, c0:c0 + LANES] * (mixed + bias[:, c0:c0 + LANES])).astype(BF16)


def l1_inproj(x, mods, mod_off, n1, w, tables, ctx_seq=None):
    bm, sm, d = x.shape
    tr = min(ROW_TILE, sm)
    rope = tables is not None
    assert rope != (ctx_seq is not None)
    qw = GQA_HEADS * LANES
    kw = GQA_KV_HEADS * LANES
    row = lambda width: pl.BlockSpec((1, tr, width), lambda b, i: (b, i, 0))
    vec = _const_spec((1, LANES))
    in_specs = [row(d), pl.BlockSpec((1, SUBLANES, d), lambda b, i: (b + mod_off, 0, 0)),
                _const_spec((1, d)), _const_spec(w["win"].shape)]
    args = [x, mods, n1, w["win"]]
    in_specs += [vec, vec]
    args += [w["qg"], w["kg"]]
    in_specs += [_const_spec((1, GMLP_CH)), _const_spec((1, GMLP_CH)),
                 _const_spec(w["ws"].shape), _const_spec((CHUNK, GMLP_CH))]
    args += [w["lng"], w["lnb"], w["ws"], w["bs"]]
    if rope:
        in_specs += [pl.BlockSpec((tr, LANES), lambda b, i: (i, 0))] * 2
        args += list(tables)
    out_shape = [jax.ShapeDtypeStruct((bm, sm, qw), BF16),
                 jax.ShapeDtypeStruct((bm, sm, kw), BF16),
                 jax.ShapeDtypeStruct((bm, sm, kw), BF16),
                 jax.ShapeDtypeStruct((bm, sm, GMLP_CH), BF16)]
    out_specs = [row(qw), row(kw), row(kw), row(GMLP_CH)]
    if not rope:
        assert bm == 1 and tr % ctx_seq == 0
        n_seq, per_step = sm // ctx_seq, tr // ctx_seq
        out_shape += [jax.ShapeDtypeStruct((n_seq, GQA_KV_HEADS, GQA_HEAD_DIM, ctx_seq), F32),
                      jax.ShapeDtypeStruct((n_seq, GQA_KV_HEADS * GQA_HEAD_DIM, ctx_seq), F32)]
        out_specs += [pl.BlockSpec((per_step, GQA_KV_HEADS, GQA_HEAD_DIM, ctx_seq), lambda b, i: (i, 0, 0, 0)),
                      pl.BlockSpec((per_step, GQA_KV_HEADS * GQA_HEAD_DIM, ctx_seq), lambda b, i: (i, 0, 0))]
    return pl.pallas_call(
        functools.partial(_l1_inproj_kernel, rope=rope),
        out_shape=out_shape, grid=(bm, sm // tr), in_specs=in_specs, out_specs=out_specs,
        compiler_params=_params(("parallel", "parallel"), VMEM_LIMIT),
        name="l1_inproj_rope" if rope else "l1_inproj",
    )(*args)


def _post_kernel(a_ref, c_ref, x_ref, m_ref, n2_ref, wa_ref, wc_ref, rwh_ref, rwl_ref, rb_ref, tri_ref, base_ref,
                 x1_out, xm_out, idx_out, wts_out, rank_out, cnt_out, run_ref):
    first =(pl.program_id(0) == 0) & (pl.program_id(1) == 0)

    @pl.when(first)
    def _():
        run_ref[...] = base_ref[...]

    sub = tri_ref.shape[0]
    running = run_ref[0:1, :]
    for r0 in range(0, a_ref.shape[1], sub):
        running = _route_rows(slice(r0, r0 + sub), running, a_ref, c_ref, x_ref, m_ref, n2_ref, wa_ref, wc_ref,
                              rwh_ref, rwl_ref, rb_ref, tri_ref, x1_out, xm_out, idx_out, wts_out, rank_out)
    run_ref[0:1, :] = running
    cnt_out[...] = run_ref[...]


def _route_rows(rs, running, a_ref, c_ref, x_ref, m_ref, n2_ref, wa_ref, wc_ref, rwh_ref, rwl_ref, rb_ref, tri_ref,
                x1_out, xm_out, idx_out, wts_out, rank_out):
    m = m_ref[0]
    y = (jnp.dot(a_ref[0, rs, :], wa_ref[...], preferred_element_type=F32)
         + jnp.dot(c_ref[0, rs, :], wc_ref[...], preferred_element_type=F32))
    x1 = x_ref[0, rs, :] + m[2:3] * y
    x1_out[0, rs, :] = x1
    xm = _rms(x1, n2_ref[...]) * (1.0 + m[4:5]) + m[3:4]
    xh = xm.astype(BF16)
    xm_out[0, rs] = _pack_rows(xh)

    xl = (xm - xh.astype(F32)).astype(BF16)
    logits = (jnp.dot(xh, rwh_ref[...], preferred_element_type=F32)
              + jnp.dot(xl, rwh_ref[...], preferred_element_type=F32)
              + jnp.dot(xh, rwl_ref[...], preferred_element_type=F32)) + rb_ref[...]
    rows = logits.shape[0]
    lane = lax.broadcasted_iota(jnp.int32, (rows, LANES), 1).astype(F32)
    work = logits
    vals, hots = [], []
    idx_acc = jnp.zeros((rows, LANES), F32)
    for k in range(TOP_K):
        top = jnp.max(work, axis=-1, keepdims=True)
        sel = jnp.min(jnp.where(work == top, lane, float(LANES)), axis=-1, keepdims=True)
        hot = lane == sel
        vals.append(top)
        hots.append(hot)
        idx_acc = idx_acc + jnp.where(lane == float(k), sel, 0.0)
        work = jnp.where(hot, -jnp.inf, work)
    exps = [jnp.exp(v - vals[0]) for v in vals]
    denom = exps[0] + exps[1] + exps[2] + exps[3]
    wcols = [jnp.broadcast_to(exps[k] / denom, (rows, LANES)) for k in range(TOP_K)]
    wcols += [jnp.zeros((rows, LANES), F32)] * (SUBLANES - TOP_K)
    wts = jnp.concatenate(wcols, axis=1).reshape(rows, SUBLANES, LANES)

    chosen = jnp.zeros((rows, LANES), F32)
    for hot in hots:
        chosen = chosen + hot.astype(F32)
    before = jnp.dot(tri_ref[...], chosen.astype(BF16), preferred_element_type=F32) + running
    rank = jnp.zeros((rows, LANES), F32)
    for k in range(TOP_K):
        rk = jnp.sum(jnp.where(hots[k], before, 0.0), axis=-1, keepdims=True)
        rank = rank + jnp.where(lane == float(k), rk, 0.0)
    idx_out[:, rs] = idx_acc.T[:SUBLANES, :].astype(jnp.int32)
    wts_out[0, rs] = wts
    rank_out[:, rs] = rank.T[:SUBLANES, :].astype(jnp.int32)
    return running + jnp.sum(chosen, axis=0, keepdims=True)


def post_mixer(attn, other, x, mods, mod_off, n2, wa, wc, rwh, rwl, rb, tri, base):
    bm, sm, d = x.shape
    tr = POST_SUBTILES * tri.shape[0]
    assert sm % tr == 0
    row = lambda width: pl.BlockSpec((1, tr, width), lambda b, i: (b, i, 0))
    tile_rows = pl.BlockSpec((1, tr) + ROW_WORDS, lambda b, i: (b, i, 0, 0))
    nb = sm // tr
    token_minor = pl.BlockSpec((SUBLANES, tr), lambda b, i: (0, b * nb + i))
    out_shape = [jax.ShapeDtypeStruct((bm, sm, d), F32),
                 jax.ShapeDtypeStruct((bm, sm) + ROW_WORDS, jnp.int32),
                 jax.ShapeDtypeStruct((SUBLANES, bm * sm), jnp.int32),
                 jax.ShapeDtypeStruct((bm, sm, SUBLANES, LANES), F32),
                 jax.ShapeDtypeStruct((SUBLANES, bm * sm), jnp.int32),
                 jax.ShapeDtypeStruct((SUBLANES, LANES), F32)]
    return pl.pallas_call(
        _post_kernel,
        out_shape=out_shape, grid=(bm, sm // tr),
        in_specs=[row(attn.shape[-1]), row(other.shape[-1]), row(d),
                  pl.BlockSpec((1, SUBLANES, d), lambda b, i: (b + mod_off, 0, 0)),
                  _const_spec((1, d)), _const_spec(wa.shape), _const_spec(wc.shape),
                  _const_spec(rwh.shape), _const_spec(rwl.shape), _const_spec((1, LANES)),
                  _const_spec(tri.shape), _const_spec((SUBLANES, LANES))],
        out_specs=[row(d), tile_rows, token_minor,
                   pl.BlockSpec((1, tr, SUBLANES, LANES), lambda b, i: (b, i, 0, 0)),
                   token_minor, _const_spec((SUBLANES, LANES))],
        scratch_shapes=[pltpu.VMEM((SUBLANES, LANES), F32)],
        compiler_params=_params(("arbitrary", "arbitrary"), VMEM_LIMIT),
        name="post_mixer_route",
    )(attn, other, x, mods, n2, wa, wc, rwh, rwl, rb, tri, base)


def _plan_kernel(off_ref, idx_ref, rank_ref, pos_out):
    idx = idx_ref[...]
    pos = rank_ref[...]
    for e in range(N_EXPERTS):
        pos = pos + jnp.where(idx == e, off_ref[e], 0)
    pos_out[...] = pos


def plan_positions(offsets, idx, rank):
    n = idx.shape[1]
    tr = min(2048, n)
    spec = pl.BlockSpec((SUBLANES, tr), lambda i, off: (0, i))
    return pl.pallas_call(
        _plan_kernel,
        out_shape=jax.ShapeDtypeStruct((SUBLANES, n), jnp.int32),
        grid_spec=pltpu.PrefetchScalarGridSpec(
            num_scalar_prefetch=1, grid=(n // tr,), in_specs=[spec, spec], out_specs=spec),
        compiler_params=_params(("parallel",)),
        name="plan_positions",
    )(offsets, idx, rank)


def _sc_worker_id():
    return lax.axis_index("s") * SC_CORES + lax.axis_index("c")


def _sc_chunk_rows(tile, dtype):
    row_bytes = math.prod(tile) * jnp.dtype(dtype).itemsize
    return min(SC_CHUNK_BYTES // row_bytes, SC_MAX_INDICES)


def dispatch_rows(xms, posk, n_rows):
    n_total = sum(x.shape[0] for x in xms)
    tile, dtype = xms[0].shape[1:], xms[0].dtype
    chunk = _sc_chunk_rows(tile, dtype)
    starts, s0 = [], 0
    for x in xms:
        assert x.shape[0] % (SC_WORKERS * chunk) == 0
        starts.append(s0)
        s0 += x.shape[0]
    mesh = plsc.VectorSubcoreMesh(core_axis_name="c", subcore_axis_name="s")

    @functools.partial(
        pl.kernel, mesh=mesh, out_type=jax.ShapeDtypeStruct((n_rows,) + tile, dtype),
        scratch_types=[pltpu.VMEM((chunk,), jnp.int32), pltpu.VMEM((chunk,) + tile, dtype),
                       pltpu.SemaphoreType.DMA],
        name="dispatch_rows_sc")
    def scatter(*refs):
        x_refs, pos_hbm, xs_hbm, idx_v, rows_v, sem = refs[:len(xms)], *refs[len(xms):]
        wid = _sc_worker_id()
        for x_hbm, start in zip(x_refs, starts):
            per_worker = x_hbm.shape[0] // SC_WORKERS

            @pl.loop(0, per_worker // chunk)
            def _(c):
                t0 = wid * per_worker + c * chunk
                pltpu.sync_copy(x_hbm.at[pl.ds(t0, chunk)], rows_v)
                for k in range(TOP_K):
                    pltpu.sync_copy(pos_hbm.at[pl.ds(k * n_total + start + t0, chunk)], idx_v)
                    pltpu.async_copy(rows_v, xs_hbm.at[idx_v], sem).wait()

    return scatter(*xms, posk)


def gather_rows(y, posk):
    n_pairs = posk.shape[0]
    tile, dtype = y.shape[1:], y.dtype
    chunk = _sc_chunk_rows(tile, dtype)
    per_worker = n_pairs // SC_WORKERS
    assert per_worker % chunk == 0
    mesh = plsc.VectorSubcoreMesh(core_axis_name="c", subcore_axis_name="s")

    @functools.partial(
        pl.kernel, mesh=mesh, out_type=jax.ShapeDtypeStruct((n_pairs,) + tile, dtype),
        scratch_types=[pltpu.VMEM((chunk,), jnp.int32), pltpu.VMEM((chunk,) + tile, dtype),
                       pltpu.SemaphoreType.DMA],
        name="gather_rows_sc")
    def gather(y_hbm, pos_hbm, out_hbm, idx_v, rows_v, sem):
        wid = _sc_worker_id()

        @pl.loop(0, per_worker // chunk)
        def _(c):
            base = wid * per_worker + c * chunk
            pltpu.sync_copy(pos_hbm.at[pl.ds(base, chunk)], idx_v)
            pltpu.async_copy(y_hbm.at[idx_v], rows_v, sem).wait()
            pltpu.sync_copy(rows_v, out_hbm.at[pl.ds(base, chunk)])

    return gather(y, posk)


def _ffn_kernel(te_ref, nu_ref, first_ref, slot_ref, nxt_ref, xs_ref, w1_hbm, b1_ref, w2_hbm, b2_ref, y_ref,
                w1f, w2f, w1b, w2b, sem):
    i = pl.program_id(0)

    def weight_copies(e, s):
        return (pltpu.make_async_copy(w1_hbm.at[e], w1f.at[s], sem.at[0, s]),
                pltpu.make_async_copy(w2_hbm.at[e], w2f.at[s], sem.at[1, s]))

    @pl.when(i < nu_ref[0])
    def _():
        s = slot_ref[i]

        @pl.when(first_ref[i] == 1)
        def _():
            @pl.when(i == 0)
            def _():
                for cp in weight_copies(te_ref[i], s):
                    cp.start()
            for cp in weight_copies(te_ref[i], s):
                cp.wait()

            @pl.when(nxt_ref[i] >= 0)
            def _():
                for cp in weight_copies(nxt_ref[i], 1 - s):
                    cp.start()
            for c in range(D_MODEL // LANES):
                w1b[c * LANES:(c + 1) * LANES, :] = w1f[s, c * LANES:(c + 1) * LANES, :].astype(BF16)
            for c in range(D_EXPERT // LANES):
                w2b[c * LANES:(c + 1) * LANES, :] = w2f[s, c * LANES:(c + 1) * LANES, :].astype(BF16)

        x = _unpack_rows(xs_ref[...])
        h = jnp.dot(x, w1b[...], preferred_element_type=F32) + b1_ref[0]
        g = jnp.minimum(h[:, :D_EXPERT], SWIGLU_LIMIT)
        lin = jnp.clip(h[:, D_EXPERT:], -SWIGLU_LIMIT, SWIGLU_LIMIT)
        a = (lin + 1.0) * (g * jax.nn.sigmoid(SWIGLU_ALPHA * g))
        y = jnp.dot(a.astype(BF16), w2b[...], preferred_element_type=F32) + b2_ref[0]
        y_ref[...] = _pack_rows(y.astype(BF16))

    @pl.when(i >= nu_ref[0])
    def _():
        y_ref[...] = jnp.zeros(y_ref.shape, y_ref.dtype)


def grouped_ffn(sched, xs, w1, b1, w2, b2):
    r = xs.shape[0]
    d = w1.shape[1]
    nt = r // FFN_TILE
    tile = (FFN_TILE,) + xs.shape[1:]
    rows = lambda i, te, nu, *_: (jnp.minimum(i, nu[0] - 1), 0, 0)
    bsel = lambda i, te, *_: (te[i], 0, 0)
    return pl.pallas_call(
        _ffn_kernel,
        out_shape=jax.ShapeDtypeStruct(xs.shape, xs.dtype),
        grid_spec=pltpu.PrefetchScalarGridSpec(
            num_scalar_prefetch=5, grid=(nt,),
            in_specs=[pl.BlockSpec(tile, rows),
                      pl.BlockSpec(memory_space=pl.ANY),
                      pl.BlockSpec((1, 1, 2 * D_EXPERT), bsel),
                      pl.BlockSpec(memory_space=pl.ANY),
                      pl.BlockSpec((1, 1, d), bsel)],
            out_specs=pl.BlockSpec(tile, lambda i, *_: (i, 0, 0)),
            scratch_shapes=[pltpu.VMEM((2, d, 2 * D_EXPERT), F32), pltpu.VMEM((2, D_EXPERT, d), F32),
                            pltpu.VMEM((d, 2 * D_EXPERT), BF16), pltpu.VMEM((D_EXPERT, d), BF16),
                            pltpu.SemaphoreType.DMA((2, 2))]),
        compiler_params=_params(("arbitrary",), VMEM_LIMIT),
        name="grouped_ffn",
    )(*sched, xs, w1, b1.reshape(N_EXPERTS, 1, -1), w2, b2.reshape(N_EXPERTS, 1, -1))


def _combine_kernel(x1_ref, wts_ref, m_ref, fn_ref, y0_ref, y1_ref, y2_ref, y3_ref, o_ref, *, final):
    w = wts_ref[0]
    rows = lambda ref: pltpu.bitcast(ref[...], BF16).astype(F32)
    acc = w[:, 0:1, :] * rows(y0_ref)
    for k, y_ref in ((1, y1_ref), (2, y2_ref), (3, y3_ref)):
        acc = acc + w[:, k:k + 1, :] * rows(y_ref)
    out = x1_ref[0] + m_ref[0][5:6] * acc.reshape(x1_ref.shape[1], D_MODEL)
    if final:
        out = _rms(out, fn_ref[...])
    o_ref[0] = out


def combine_rows(x1, wts, mods, mod_off, fn, yg, row_off, n_total, *, final):
    bm, sm, d = x1.shape
    tr = min(COMBINE_TILE, sm)
    nb = sm // tr
    row = lambda width: pl.BlockSpec((1, tr, width), lambda b, i: (b, i, 0))
    ysel = lambda k: pl.BlockSpec((tr,) + yg.shape[1:],
                                  lambda b, i: ((k * n_total + row_off) // tr + b * nb + i, 0, 0))
    return pl.pallas_call(
        functools.partial(_combine_kernel, final=final),
        out_shape=jax.ShapeDtypeStruct((bm, sm, d), F32),
        grid=(bm, nb),
        in_specs=[row(d), pl.BlockSpec((1, tr, SUBLANES, LANES), lambda b, i: (b, i, 0, 0)),
                  pl.BlockSpec((1, SUBLANES, d), lambda b, i: (b + mod_off, 0, 0)),
                  _const_spec((1, d))] + [ysel(k) for k in range(TOP_K)],
        out_specs=row(d),
        compiler_params=_params(("parallel", "parallel"), VMEM_LIMIT),
        name="combine_rows",
    )(x1, wts, mods, fn, yg, yg, yg, yg)


def _axial_angles(n_tokens, rot_dim):
    t = jnp.arange(n_tokens)
    rows = (t // GRID_W).astype(F32)
    cols = (t % GRID_W).astype(F32)
    n_freq = rot_dim // 4
    inv = ROPE_THETA ** (-jnp.arange(n_freq, dtype=F32) / n_freq)
    return jnp.concatenate([rows[:, None] * inv, cols[:, None] * inv], axis=-1)


def _swap_halves(w):
    half = w.shape[-1] // 2
    return jnp.concatenate([-w[..., half:], w[..., :half]], axis=-1)


def _prep_l0(w_in, q_norm, kv_norm, w_uq, w_uk, w_uv, w_out):
    d = w_in.shape[0]
    o_kr = MLA_Q_RANK + MLA_KV_RANK
    kr_cols = w_in[:, o_kr:o_kr + MLA_ROPE]
    win = jnp.concatenate(
        [w_in[:, :o_kr], kr_cols, _swap_halves(kr_cols), jnp.zeros((d, LANES - 2 * MLA_ROPE), F32),
         w_in[:, o_kr + MLA_ROPE:]], axis=1).astype(BF16)
    qk = MLA_NOPE + MLA_ROPE
    wuq3 = w_uq.reshape(MLA_Q_RANK, MLA_HEADS, qk)
    wuq = jnp.pad(wuq3, ((0, 0), (0, 0), (0, LANES - qk))).reshape(MLA_Q_RANK, -1).astype(BF16)
    wuk3 = w_uk.reshape(MLA_KV_RANK, MLA_HEADS, MLA_NOPE)
    wuk = jnp.pad(wuk3, ((0, 0), (0, 0), (0, LANES - MLA_NOPE))).reshape(MLA_KV_RANK, -1).astype(BF16)
    wuv3 = w_uv.reshape(MLA_KV_RANK, MLA_HEADS, MLA_V)
    wuv = jnp.pad(wuv3, ((0, 0), (0, 0), (0, LANES - MLA_V))).reshape(MLA_KV_RANK, -1).astype(BF16)
    eye = jnp.eye(MLA_ROPE, dtype=F32)
    e_head = jnp.concatenate([jnp.zeros((MLA_ROPE, MLA_NOPE), F32), eye,
                              jnp.zeros((MLA_ROPE, LANES - qk), F32)], axis=1)
    e = jnp.pad(jnp.tile(e_head, (1, MLA_HEADS)), ((0, LANES - MLA_ROPE), (0, 0))).astype(BF16)
    wa3 = w_out[:MLA_HEADS * MLA_V].reshape(MLA_HEADS, MLA_V, d)
    wa = jnp.pad(wa3, ((0, 0), (0, LANES - MLA_V), (0, 0))).reshape(MLA_HEADS * LANES, d).astype(BF16)
    wc = w_out[MLA_HEADS * MLA_V:].astype(BF16)
    return dict(win=win, qg=q_norm.reshape(1, -1), kvg=kv_norm.reshape(1, -1), wuq=wuq,
                wuk=wuk, e=e, wuv=wuv, wa=wa, wc=wc)


def _l0_tables(n):
    ang = _axial_angles(n, MLA_ROPE)
    cos, sin = jnp.cos(ang), jnp.sin(ang)
    one = jnp.ones((n, 1), F32)
    zero = jnp.zeros((n, 1), F32)
    rest = LANES - MLA_NOPE - MLA_ROPE
    cq = jnp.concatenate([jnp.tile(one, (1, MLA_NOPE)), cos, cos, jnp.tile(one, (1, rest))], axis=1)
    sq = jnp.concatenate([jnp.tile(zero, (1, MLA_NOPE)), sin, sin, jnp.tile(zero, (1, rest))], axis=1)
    ck = jnp.concatenate([cos, cos, jnp.tile(zero, (1, LANES - MLA_ROPE))], axis=1)
    sk = jnp.concatenate([sin, sin, jnp.tile(zero, (1, LANES - MLA_ROPE))], axis=1)
    return cq, sq, ck, sk


def _pad_heads(w, n_heads, dim):
    d = w.shape[0]
    return jnp.pad(w.reshape(d, n_heads, dim), ((0, 0), (0, 0), (0, LANES - dim))).reshape(d, n_heads * LANES)


def _prep_l1(w_in, q_norm, k_norm, ln_g, ln_b, w_s, b_s, w_out):
    d = w_in.shape[0]
    qd = GQA_HEADS * GQA_HEAD_DIM
    kd = GQA_KV_HEADS * GQA_HEAD_DIM
    wq, wk, wv = w_in[:, :qd], w_in[:, qd:qd + kd], w_in[:, qd + kd:qd + 2 * kd]
    rest = w_in[:, qd + 2 * kd:]
    win = jnp.concatenate([_pad_heads(wq, GQA_HEADS, GQA_HEAD_DIM), _pad_heads(wk, GQA_KV_HEADS, GQA_HEAD_DIM),
                           _pad_heads(wv, GQA_KV_HEADS, GQA_HEAD_DIM), wv, rest], axis=1).astype(BF16)
    padg = lambda g: jnp.pad(g, (0, LANES - GQA_HEAD_DIM)).reshape(1, LANES)
    wa3 = w_out[:qd].reshape(GQA_HEADS, GQA_HEAD_DIM, d)
    wa = jnp.pad(wa3, ((0, 0), (0, LANES - GQA_HEAD_DIM), (0, 0))).reshape(GQA_HEADS * LANES, d).astype(BF16)
    wc = w_out[qd:].astype(BF16)
    bs = jnp.repeat(b_s.T, LANES, axis=1)
    return dict(win=win, qg=padg(q_norm), kg=padg(k_norm), lng=ln_g.reshape(1, -1), lnb=ln_b.reshape(1, -1),
                ws=w_s.astype(BF16), bs=bs, wa=wa, wc=wc)


def _l1_tables(n):
    ang = _axial_angles(n, GQA_HEAD_DIM)
    cos, sin = jnp.cos(ang), jnp.sin(ang)
    pad = LANES - GQA_HEAD_DIM
    c = jnp.concatenate([cos, cos, jnp.ones((n, pad), F32)], axis=1)
    s = jnp.concatenate([sin, sin, jnp.zeros((n, pad), F32)], axis=1)
    return c, s


def _pad_lanes(x, width):
    return jnp.pad(x, [(0, 0)] * (x.ndim - 1) + [(0, width - x.shape[-1])])


def routed_ffn(groups, mods, n2, moe, tri, final_norm, *, final):
    router_w, router_b, w1, b1, w2, b2 = moe
    rw = _pad_lanes(router_w, LANES)
    rwh = rw.astype(BF16)
    rwl = (rw - rwh.astype(F32)).astype(BF16)
    rb = jnp.concatenate([router_b, jnp.full((LANES - N_EXPERTS,), NEG_BIG, F32)]).reshape(1, LANES)
    base = jnp.zeros((SUBLANES, LANES), F32)
    routed = []
    for g in groups:
        x1, xm, idx, wts, rank, base = post_mixer(
            g["attn"], g["other"], g["x"], mods, g["mod_off"], n2, g["wa"], g["wc"], rwh, rwl, rb, tri, base)
        routed.append((x1, xm, idx, wts, rank))
    n_total = sum(r[0].shape[0] * r[0].shape[1] for r in routed)
    nt = n_total * TOP_K // FFN_TILE + N_EXPERTS

    counts = base[0, :N_EXPERTS].astype(jnp.int32)
    tiles = (counts + FFN_TILE - 1) // FFN_TILE
    tile_end = jnp.cumsum(tiles)
    offsets = (tile_end - tiles) * FFN_TILE
    n_used = tile_end[-1:].astype(jnp.int32)
    experts = jnp.arange(N_EXPERTS)
    busy = tiles > 0
    slot_e = (jnp.cumsum(busy) - 1) % 2
    later = jnp.where(busy[None, :] & (experts[None, :] > experts[:, None]), experts[None, :], N_EXPERTS)
    nxt_e = jnp.min(later, axis=1)
    nxt_e = jnp.where(nxt_e == N_EXPERTS, -1, nxt_e)
    tile_expert = jnp.sum(jnp.arange(nt)[:, None] >= tile_end[None, :], axis=1)
    tile_expert = jnp.minimum(tile_expert, jnp.max(jnp.where(busy, experts, 0)))
    first = jnp.concatenate([jnp.ones((1,), bool), tile_expert[1:] != tile_expert[:-1]])
    of_tile = tile_expert[:, None] == experts[None, :]
    slot_t = jnp.sum(jnp.where(of_tile, slot_e[None, :], 0), axis=1)
    nxt_t = jnp.sum(jnp.where(of_tile, nxt_e[None, :], 0), axis=1)
    sched = tuple(a.astype(jnp.int32) for a in (tile_expert, n_used, first, slot_t, nxt_t))

    positions, xms = [], []
    for (x1, xm, idx, wts, rank) in routed:
        positions.append(plan_positions(offsets, idx, rank)[:TOP_K])
        xms.append(xm.reshape((idx.shape[1],) + ROW_WORDS))
    posk = jnp.concatenate(positions, axis=1).reshape(-1)
    xs = dispatch_rows(xms, posk, nt * FFN_TILE)
    y = grouped_ffn(sched, xs, w1, b1, w2, b2)
    outs = []
    for g, (x1, xm, idx, wts, rank), pos in zip(groups, routed, positions):
        yg = gather_rows(y, pos.reshape(-1))
        outs.append(combine_rows(x1, wts, mods, g["mod_off"], final_norm, yg, 0, pos.shape[1], final=final))
    return outs


def kernel(x_prompt, x_sample, cache_l0_ckv, cache_l0_krope, cache_l1_k, cache_l1_v, c, c_ctx,
           l0_ada_w, l0_ada_b, l0_norm1, l0_w_in, l0_q_norm, l0_kv_norm, l0_w_uq, l0_w_uk, l0_w_uv,
           l0_conv_w, l0_conv_b, l0_conv_ln_g, l0_conv_ln_b, l0_w_out, l0_norm2,
           l0_router_w, l0_router_b, l0_w1, l0_b1, l0_w2, l0_b2,
           l1_ada_w, l1_ada_b, l1_norm1, l1_w_in, l1_q_norm, l1_k_norm, l1_gmlp_ln_g, l1_gmlp_ln_b,
           l1_w_s, l1_b_s, l1_w_out, l1_norm2,
           l1_router_w, l1_router_b, l1_w1, l1_b1, l1_w2, l1_b2,
           final_norm):
    bp, sp, d = x_prompt.shape
    bs, ss, _ = x_sample.shape
    past = cache_l0_ckv.shape[1]
    n_p = bp * sp

    cond8 = jnp.concatenate([c_ctx[None], c, jnp.zeros((SUBLANES - 1 - bs, d), F32)], axis=0)
    mods0 = adaln(cond8, l0_ada_w, l0_ada_b)
    mods1 = adaln(cond8, l1_ada_w, l1_ada_b)
    tri = jnp.tril(jnp.ones((ROW_TILE, ROW_TILE), F32), -1).astype(BF16)
    fn = final_norm.reshape(1, d)

    w0 = _prep_l0(l0_w_in, l0_q_norm, l0_kv_norm, l0_w_uq, l0_w_uk, l0_w_uv, l0_w_out)
    n1 = l0_norm1.reshape(1, d)
    hp = x_prompt.reshape(1, n_p, d)
    q_p, k_p, v_p, ckv_p, kr_p, u_p = l0_inproj(hp, mods0, 0, n1, w0, None)
    q_s, k_s, v_s, _, _, u_s = l0_inproj(x_sample, mods0, 1, n1, w0, _l0_tables(ss))
    k_c, v_c = mla_ctx_kv(cache_l0_ckv, _pad_lanes(cache_l0_krope, LANES), w0)
    hw = MLA_HEADS * LANES
    att_p = attention(q_p.reshape(bp, sp, hw), k_p.reshape(bp, sp, hw), v_p.reshape(bp, sp, hw),
                      n_heads=MLA_HEADS, n_kv=MLA_HEADS, heads_per_step=MLA_HEADS)
    att_s = attention(q_s, jnp.concatenate([k_c, k_s], axis=1), jnp.concatenate([v_c, v_s], axis=1),
                      n_heads=MLA_HEADS, n_kv=MLA_HEADS, heads_per_step=LATENT_HEADS_PER_STEP)
    conv_p = conformer_conv(u_p.reshape(bp, sp, CONV_CH), l0_conv_w, l0_conv_b, l0_conv_ln_g, l0_conv_ln_b)
    conv_s = conformer_conv(u_s, l0_conv_w, l0_conv_b, l0_conv_ln_g, l0_conv_ln_b)
    groups = [dict(attn=att_p.reshape(1, n_p, hw), other=conv_p.reshape(1, n_p, CONV_CH), x=hp, mod_off=0,
                   wa=w0["wa"], wc=w0["wc"]),
              dict(attn=att_s, other=conv_s, x=x_sample, mod_off=1, wa=w0["wa"], wc=w0["wc"])]
    hp, hs = routed_ffn(groups, mods0, l0_norm2.reshape(1, d),
                        (l0_router_w, l0_router_b, l0_w1, l0_b1, l0_w2, l0_b2), tri, fn, final=False)
    new_l0_ckv = ckv_p.reshape(bp, sp, MLA_KV_RANK)
    new_l0_krope = kr_p.reshape(bp, sp, MLA_ROPE)

    w1p = _prep_l1(l1_w_in, l1_q_norm, l1_k_norm, l1_gmlp_ln_g, l1_gmlp_ln_b, l1_w_s, l1_b_s, l1_w_out)
    n1 = l1_norm1.reshape(1, d)
    q_p, k_p, vp_p, gat_p, kt_p, vt_p = l1_inproj(hp, mods1, 0, n1, w1p, None, ctx_seq=sp)
    q_s, k_s, vp_s, gat_s = l1_inproj(hs, mods1, 1, n1, w1p, _l1_tables(ss))
    qw = GQA_HEADS * LANES
    kw = GQA_KV_HEADS * LANES
    pad_kv = lambda t: _pad_lanes(t, LANES).reshape(bs, past, kw).astype(BF16)
    att_p = attention(q_p.reshape(bp, sp, qw), k_p.reshape(bp, sp, kw), vp_p.reshape(bp, sp, kw),
                      n_heads=GQA_HEADS, n_kv=GQA_KV_HEADS, heads_per_step=GQA_HEADS)
    att_s = attention(q_s, jnp.concatenate([pad_kv(cache_l1_k), k_s], axis=1),
                      jnp.concatenate([_with_sum_lane(pad_kv(cache_l1_v)), vp_s], axis=1),
                      n_heads=GQA_HEADS, n_kv=GQA_KV_HEADS, heads_per_step=LATENT_HEADS_PER_STEP)
    groups = [dict(attn=att_p.reshape(1, n_p, qw), other=gat_p, x=hp, mod_off=0, wa=w1p["wa"], wc=w1p["wc"]),
              dict(attn=att_s, other=gat_s, x=hs, mod_off=1, wa=w1p["wa"], wc=w1p["wc"])]
    yp, ys = routed_ffn(groups, mods1, l1_norm2.reshape(1, d),
                        (l1_router_w, l1_router_b, l1_w1, l1_b1, l1_w2, l1_b2), tri, fn, final=True)
    new_l1_k = jnp.transpose(kt_p, (0, 3, 1, 2))
    new_l1_v = jnp.transpose(vt_p.reshape(bp, GQA_KV_HEADS, GQA_HEAD_DIM, sp), (0, 3, 1, 2))
    return (yp.reshape(bp, sp, d), ys, new_l0_ckv, new_l0_krope, new_l1_k, new_l1_v)
```

```python
import functools
import math

import jax
import jax.numpy as jnp
from jax import lax
from jax.experimental import pallas as pl
from jax.experimental.pallas import tpu as pltpu
from jax.experimental.pallas import tpu_sc as plsc

F32 = jnp.float32
BF16 = jnp.bfloat16
HIGHEST = lax.Precision.HIGHEST

LANES = 128
SUBLANES = 8
VMEM_LIMIT = 56 * 1024 * 1024

D_MODEL = 1024
GRID_W = 64
ROPE_THETA = 10000.0
EPS = 1e-6
N_MOD = 6

MLA_HEADS = 8
MLA_NOPE = 64
MLA_ROPE = 32
MLA_V = 64
MLA_Q_RANK = 384
MLA_KV_RANK = 256
MLA_SCALE = 1.0 / math.sqrt(MLA_NOPE + MLA_ROPE)
CONV_CH = 512
CONV_WIDTH = 31
CONV_HALO = 16

GQA_HEADS = 8
GQA_KV_HEADS = 2
GQA_HEAD_DIM = 64
GQA_SCALE = 1.0 / math.sqrt(GQA_HEAD_DIM)
CHUNK = 128
GMLP_GROUPS = 4
GMLP_CH = 512

N_EXPERTS = 32
TOP_K = 4
D_EXPERT = 1024
SWIGLU_LIMIT = 7.0
SWIGLU_ALPHA = 1.702

ROW_TILE = 512
POST_SUBTILES = 2
FFN_TILE = 256
SC_CORES = 2
SC_SUBCORES = 16
SC_WORKERS = SC_CORES * SC_SUBCORES
SC_CHUNK_BYTES = 256 * 1024
SC_MAX_INDICES = 128
COMBINE_TILE = 512
ATT_Q_TILE = 256
ATT_LONG_KEYS = 1024
V_SUM_LANE = 64
LATENT_HEADS_PER_STEP = 4
NEG_BIG = -1e30


def _params(sem, vmem=None):
    return pltpu.CompilerParams(dimension_semantics=sem, vmem_limit_bytes=vmem)


def _rms(x, g):
    return x * lax.rsqrt(jnp.mean(x * x, axis=-1, keepdims=True) + EPS) * g


ROW_WORDS = (SUBLANES // 2, LANES)


def _pack_rows(x_bf16):
    return pltpu.bitcast(x_bf16.reshape(x_bf16.shape[0], SUBLANES, LANES), jnp.int32)


def _unpack_rows(words):
    return pltpu.bitcast(words, BF16).reshape(words.shape[0], D_MODEL)


def _const_spec(shape):
    nd = len(shape)
    return pl.BlockSpec(shape, lambda *_: (0,) * nd)


def _adaln_kernel(c_ref, w_ref, b_ref, o_ref):
    c = c_ref[...]
    s = c * jax.nn.sigmoid(c)
    o_ref[...] = jnp.dot(s, w_ref[...], preferred_element_type=F32, precision=HIGHEST) + b_ref[...]


def adaln(cond8, ada_w, ada_b):
    d, n = ada_w.shape
    bn = n // 4
    m = pl.pallas_call(
        _adaln_kernel,
        out_shape=jax.ShapeDtypeStruct((SUBLANES, n), F32),
        grid=(n // bn,),
        in_specs=[_const_spec((SUBLANES, d)),
                  pl.BlockSpec((d, bn), lambda j: (0, j)),
                  pl.BlockSpec((1, bn), lambda j: (0, j))],
        out_specs=pl.BlockSpec((SUBLANES, bn), lambda j: (0, j)),
        compiler_params=_params(("arbitrary",), VMEM_LIMIT),
        name="adaln",
    )(cond8, ada_w, ada_b.reshape(1, n))
    m = m.reshape(SUBLANES, N_MOD, d)
    return jnp.pad(m, ((0, 0), (0, SUBLANES - N_MOD), (0, 0)))


def _l0_inproj_kernel(*refs, rope):
    if rope:
        (x_ref, m_ref, n1_ref, win_ref, qg_ref, kvg_ref, wuq_ref, wuk_ref, e_ref, wuv_ref,
         cq_ref, sq_ref, ck_ref, sk_ref, q_out, k_out, v_out, ckv_out, kr_out, u_out) = refs
    else:
        (x_ref, m_ref, n1_ref, win_ref, qg_ref, kvg_ref, wuq_ref, wuk_ref, e_ref, wuv_ref,
         q_out, k_out, v_out, ckv_out, kr_out, u_out) = refs
    x = x_ref[0]
    m = m_ref[0]
    h = _rms(x, n1_ref[...]) * (1.0 + m[1:2]) + m[0:1]
    z = jnp.dot(h.astype(BF16), win_ref[...], preferred_element_type=F32)
    c_q = z[:, 0:MLA_Q_RANK]
    c_kv = z[:, MLA_Q_RANK:MLA_Q_RANK + MLA_KV_RANK]
    o_kr = MLA_Q_RANK + MLA_KV_RANK
    o_conv = o_kr + LANES
    kr_blk = z[:, o_kr:o_conv]
    val = z[:, o_conv:o_conv + CONV_CH]
    gate = z[:, o_conv + CONV_CH:o_conv + 2 * CONV_CH]

    cqn = _rms(c_q, qg_ref[...]).astype(BF16)
    q = jnp.dot(cqn, wuq_ref[...], preferred_element_type=F32)
    if rope:
        cq = cq_ref[...]
        sq = sq_ref[...]
        half = MLA_ROPE // 2
        lane = lax.broadcasted_iota(jnp.int32, (1, LANES), 1)
        first_half = (lane >= MLA_NOPE) & (lane < MLA_NOPE + half)
        for hd in range(MLA_HEADS):
            sl = slice(hd * LANES, (hd + 1) * LANES)
            blk = q[:, sl]
            partner = jnp.where(first_half, -pltpu.roll(blk, LANES - half, 1), pltpu.roll(blk, half, 1))
            q_out[0, :, sl] = ((blk * cq + partner * sq) * MLA_SCALE).astype(BF16)
        kr = kr_blk * ck_ref[...] + pltpu.roll(kr_blk, LANES - MLA_ROPE, 1) * sk_ref[...]
    else:
        q_out[0] = (q * MLA_SCALE).astype(BF16)
        kr = kr_blk

    ckv = _rms(c_kv, kvg_ref[...])
    ckv_out[0] = ckv
    kr_out[0] = kr_blk[:, 0:MLA_ROPE]
    ckv_b = ckv.astype(BF16)
    k = (jnp.dot(ckv_b, wuk_ref[...], preferred_element_type=F32)
         + jnp.dot(kr.astype(BF16), e_ref[...], preferred_element_type=F32))
    k_out[0] = k.astype(BF16)
    v_out[0] = _with_sum_lane(jnp.dot(ckv_b, wuv_ref[...], preferred_element_type=F32)).astype(BF16)
    u_out[0] = val * jax.nn.sigmoid(gate)


def l0_inproj(x, mods, mod_off, n1, w, tables):
    bm, sm, d = x.shape
    tr = min(ROW_TILE, sm)
    rope = tables is not None
    hp = MLA_HEADS * LANES
    row = lambda width: pl.BlockSpec((1, tr, width), lambda b, i: (b, i, 0))
    in_specs = [row(d),
                pl.BlockSpec((1, SUBLANES, d), lambda b, i: (b + mod_off, 0, 0)),
                _const_spec((1, d)), _const_spec(w["win"].shape),
                _const_spec((1, MLA_Q_RANK)), _const_spec((1, MLA_KV_RANK)),
                _const_spec(w["wuq"].shape)]
    args = [x, mods, n1, w["win"], w["qg"], w["kvg"], w["wuq"]]
    in_specs +=[_const_spec(w["wuk"].shape), _const_spec(w["e"].shape), _const_spec(w["wuv"].shape)]
    args += [w["wuk"], w["e"], w["wuv"]]
    if rope:
        in_specs += [pl.BlockSpec((tr, LANES), lambda b, i: (i, 0))] * 4
        args += list(tables)
    out_shape = [jax.ShapeDtypeStruct((bm, sm, hp), BF16),
                 jax.ShapeDtypeStruct((bm, sm, hp), BF16),
                 jax.ShapeDtypeStruct((bm, sm, hp), BF16),
                 jax.ShapeDtypeStruct((bm, sm, MLA_KV_RANK), F32),
                 jax.ShapeDtypeStruct((bm, sm, MLA_ROPE), F32),
                 jax.ShapeDtypeStruct((bm, sm, CONV_CH), F32)]
    out_specs = [row(hp), row(hp), row(hp), row(MLA_KV_RANK), row(MLA_ROPE), row(CONV_CH)]
    return pl.pallas_call(
        functools.partial(_l0_inproj_kernel, rope=rope),
        out_shape=out_shape, grid=(bm, sm // tr), in_specs=in_specs, out_specs=out_specs,
        compiler_params=_params(("parallel", "parallel"), VMEM_LIMIT),
        name="l0_inproj_rope" if rope else "l0_inproj",
    )(*args)


def _mla_ctx_kv_kernel(ckv_ref, kr_ref, wuk_ref, e_ref, wuv_ref, k_out, v_out):
    ckv_b = ckv_ref[0].astype(BF16)
    k = (jnp.dot(ckv_b, wuk_ref[...], preferred_element_type=F32)
         + jnp.dot(kr_ref[0].astype(BF16), e_ref[...], preferred_element_type=F32))
    k_out[0] = k.astype(BF16)
    v_out[0] = _with_sum_lane(jnp.dot(ckv_b, wuv_ref[...], preferred_element_type=F32)).astype(BF16)


def mla_ctx_kv(ckv, kr128, w):
    b, s, _ = ckv.shape
    hp = MLA_HEADS * LANES
    blk = lambda width: pl.BlockSpec((1, s, width), lambda i: (i, 0, 0))
    return pl.pallas_call(
        _mla_ctx_kv_kernel,
        out_shape=[jax.ShapeDtypeStruct((b, s, hp), BF16)] * 2,
        grid=(b,),
        in_specs=[blk(MLA_KV_RANK), blk(LANES), _const_spec(w["wuk"].shape),
                  _const_spec(w["e"].shape), _const_spec(w["wuv"].shape)],
        out_specs=[blk(hp), blk(hp)],
        compiler_params=_params(("parallel",), VMEM_LIMIT),
        name="mla_ctx_kv",
    )(ckv, kr128, w["wuk"], w["e"], w["wuv"])


def _conv_kernel(prev_ref, cur_ref, next_ref, w_ref, b_ref, g_ref, beta_ref, o_ref, pad_ref, sh_ref, *, rb):
    i = pl.program_id(1)
    last = pl.num_programs(1) - 1
    zeros = jnp.zeros((CONV_HALO, CONV_CH), F32)
    pad_ref[0:CONV_HALO, :] = jnp.where(i == 0, zeros, prev_ref[0])
    pad_ref[CONV_HALO:CONV_HALO + rb, :] = cur_ref[0]
    pad_ref[CONV_HALO + rb:CONV_HALO + rb + CONV_HALO, :] = jnp.where(i == last, zeros, next_ref[0])
    span = rb + 2 * CONV_HALO - SUBLANES
    for r in range(1, SUBLANES):
        sh_ref[r] = pad_ref[r:r + span, :]
    w = w_ref[...]
    shift = CONV_HALO - CONV_WIDTH // 2
    acc = jnp.zeros((rb, CONV_CH), F32) + b_ref[...]
    for k in range(CONV_WIDTH):
        off = k + shift
        r, a = off % SUBLANES, off // SUBLANES * SUBLANES
        window = pad_ref[a:a + rb, :] if r == 0 else sh_ref[r, a:a + rb, :]
        acc = acc + window * w[k:k + 1, :]
    mu = jnp.mean(acc, axis=-1, keepdims=True)
    cen = acc - mu
    var = jnp.mean(cen * cen, axis=-1, keepdims=True)
    y = cen * lax.rsqrt(var + EPS) * g_ref[...] + beta_ref[...]
    o_ref[0] = (y * jax.nn.sigmoid(y)).astype(BF16)


def conformer_conv(u, conv_w, conv_b, ln_g, ln_b):
    b, s, c = u.shape
    rb = min(256, s)
    nh = rb // CONV_HALO
    n_halo_blocks = s // CONV_HALO
    wpad = jnp.pad(conv_w.reshape(CONV_WIDTH, c), ((0, 32 - CONV_WIDTH), (0, 0)))
    return pl.pallas_call(
        functools.partial(_conv_kernel, rb=rb),
        out_shape=jax.ShapeDtypeStruct((b, s, c), BF16),
        grid=(b, s // rb),
        in_specs=[pl.BlockSpec((1, CONV_HALO, c), lambda bi, i: (bi, jnp.maximum(i * nh - 1, 0), 0)),
                  pl.BlockSpec((1, rb, c), lambda bi, i: (bi, i, 0)),
                  pl.BlockSpec((1, CONV_HALO, c),
                               lambda bi, i: (bi, jnp.minimum((i + 1) * nh, n_halo_blocks - 1), 0)),
                  _const_spec((32, c)), _const_spec((1, c)), _const_spec((1, c)), _const_spec((1, c))],
        out_specs=pl.BlockSpec((1, rb, c), lambda bi, i: (bi, i, 0)),
        scratch_shapes=[pltpu.VMEM((rb + 2 * CONV_HALO, c), F32),
                        pltpu.VMEM((SUBLANES, rb + 2 * CONV_HALO - SUBLANES, c), F32)],
        compiler_params=_params(("parallel", "parallel"), VMEM_LIMIT),
        name="conformer_conv",
    )(u, u, u, wpad, conv_b.reshape(1, c), ln_g.reshape(1, c), ln_b.reshape(1, c))


def _with_sum_lane(v):
    lane = lax.broadcasted_iota(jnp.int32, (1, v.shape[-1]), 1)
    return v + ((lane & (LANES - 1)) == V_SUM_LANE).astype(v.dtype)


def _attn_kernel(q_ref, k_ref, v_ref, o_ref, *, heads, rep, mxu_denominator):
    for hd in range(heads):
        g = hd // rep
        q = q_ref[0, :, hd * LANES:(hd + 1) * LANES]
        k = k_ref[0, :, g * LANES:(g + 1) * LANES]
        s = lax.dot_general(q, k, (((1,), (1,)), ((), ())), preferred_element_type=F32)
        m = jnp.max(s, axis=-1, keepdims=True)
        if mxu_denominator:
            p = jnp.exp((s - m).astype(BF16))
            o = jnp.dot(p, v_ref[0, :, g * LANES:(g + 1) * LANES], preferred_element_type=F32)
            l = o[:, V_SUM_LANE:V_SUM_LANE + 1]
        else:
            p = jnp.exp(s - m)
            l = jnp.sum(p, axis=-1, keepdims=True)
            o = jnp.dot(p.astype(BF16), v_ref[0, :, g * LANES:(g + 1) * LANES], preferred_element_type=F32)
        o_ref[0, :, hd * LANES:(hd + 1) * LANES] = (o / l).astype(BF16)


def attention(q, k, v, *, n_heads, n_kv, heads_per_step):
    b, sq, _ = q.shape
    sk = k.shape[1]
    rep = n_heads // n_kv
    tq = min(ATT_Q_TILE, sq)
    hb = heads_per_step
    grid = (b, n_heads // hb, sq // tq)
    if hb >= rep:
        kv_spec = pl.BlockSpec((1, sk, hb // rep * LANES), lambda bi, h, i: (bi, 0, h))
        kern_rep = rep
    else:
        assert rep % hb == 0
        kv_spec = pl.BlockSpec((1, sk, LANES), lambda bi, h, i: (bi, 0, h * hb // rep))
        kern_rep = hb
    kern = functools.partial(_attn_kernel, heads=hb, rep=kern_rep, mxu_denominator=sk >= ATT_LONG_KEYS)
    q_spec = pl.BlockSpec((1, tq, heads_per_step * LANES), lambda bi, h, i: (bi, i, h))
    return pl.pallas_call(
        kern,
        out_shape=jax.ShapeDtypeStruct(q.shape, BF16),
        grid=grid, in_specs=[q_spec, kv_spec, kv_spec], out_specs=q_spec,
        compiler_params=_params(("parallel", "parallel", "parallel"), VMEM_LIMIT),
        name="attention",
    )(q, k, v)


def _l1_inproj_kernel(*refs, rope):
    if rope:
        (x_ref, m_ref, n1_ref, win_ref, qg_ref, kg_ref, lng_ref, lnb_ref,
         ws_ref, bs_ref, c_ref, s_ref, q_out, k_out, vp_out, g_out) = refs
        kt_out = vt_out = None
    else:
        (x_ref, m_ref, n1_ref, win_ref, qg_ref, kg_ref, lng_ref, lnb_ref,
         ws_ref, bs_ref, q_out, k_out, vp_out, g_out, kt_out, vt_out) = refs
    x = x_ref[0]
    m = m_ref[0]
    hb = (_rms(x, n1_ref[...]) * (1.0 + m[1:2]) + m[0:1]).astype(BF16)
    z = jnp.dot(hb, win_ref[...], preferred_element_type=F32)
    qw = GQA_HEADS * LANES
    kw = GQA_KV_HEADS * LANES
    o_k, o_vp, o_v, o_u, o_vg = qw, qw + kw, qw + 2 * kw, qw + 2 * kw + LANES, qw + 2 * kw + LANES + GMLP_CH
    if rope:
        cos = c_ref[...]
        sin = s_ref[...]
        half = GQA_HEAD_DIM // 2
        first_half = lax.broadcasted_iota(jnp.int32, (1, LANES), 1) < half

    def head(col, g_ref):
        t = z[:, col:col + LANES]
        r = lax.rsqrt(jnp.sum(t * t, axis=-1, keepdims=True) * (1.0 / GQA_HEAD_DIM) + EPS)
        normed = t * r * g_ref[...]
        if not rope:
            return normed, normed
        partner = jnp.where(first_half, -pltpu.roll(normed, LANES - half, 1), pltpu.roll(normed, half, 1))
        return normed, normed * cos + partner * sin

    for hd in range(GQA_HEADS):
        _, rot = head(hd * LANES, qg_ref)
        q_out[0, :, hd * LANES:(hd + 1) * LANES] = (rot * GQA_SCALE).astype(BF16)
    for hd in range(GQA_KV_HEADS):
        normed, rot = head(o_k + hd * LANES, kg_ref)
        k_out[0, :, hd * LANES:(hd + 1) * LANES] = rot.astype(BF16)
        if kt_out is not None:
            seq = kt_out.shape[-1]
            for s in range(kt_out.shape[0]):
                kt_out[s, hd] = normed[s * seq:(s + 1) * seq, :].T[:GQA_HEAD_DIM, :]
    vp_out[0] = _with_sum_lane(z[:, o_vp:o_vp + kw]).astype(BF16)
    if vt_out is not None:
        seq = vt_out.shape[-1]
        for s in range(vt_out.shape[0]):
            vt_out[s] = z[s * seq:(s + 1) * seq, o_v:o_v + LANES].T

    u = z[:, o_u:o_u + GMLP_CH]
    vg = z[:, o_vg:o_vg + GMLP_CH]
    mu = jnp.mean(vg, axis=-1, keepdims=True)
    cen = vg - mu
    var = jnp.mean(cen * cen, axis=-1, keepdims=True)
    vn = (cen * lax.rsqrt(var + EPS) * lng_ref[...] + lnb_ref[...]).astype(BF16)
    bias = bs_ref[...]
    rows = x.shape[0]
    for cidx in range(rows // CHUNK):
        r0 = cidx * CHUNK
        for g in range(GMLP_GROUPS):
            c0 = g * LANES
            mixed = jnp.dot(ws_ref[g], vn[r0:r0 + CHUNK, c0:c0 + LANES], preferred_element_type=F32)
            g_out[0, r0:r0 + CHUNK, c0:c0 + LANES] = (
                u[r0:r0 + CHUNK, c0:c0 + LANES] * (mixed + bias[:, c0:c0 + LANES])).astype(BF16)


def l1_inproj(x, mods, mod_off, n1, w, tables, ctx_seq=None):
    bm, sm, d = x.shape
    tr = min(ROW_TILE, sm)
    rope = tables is not None
    assert rope != (ctx_seq is not None)
    qw = GQA_HEADS * LANES
    kw = GQA_KV_HEADS * LANES
    row = lambda width: pl.BlockSpec((1, tr, width), lambda b, i: (b, i, 0))
    vec = _const_spec((1, LANES))
    in_specs = [row(d), pl.BlockSpec((1, SUBLANES, d), lambda b, i: (b + mod_off, 0, 0)),
                _const_spec((1, d)), _const_spec(w["win"].shape)]
    args = [x, mods, n1, w["win"]]
    in_specs += [vec, vec]
    args += [w["qg"], w["kg"]]
    in_specs += [_const_spec((1, GMLP_CH)), _const_spec((1, GMLP_CH)),
                 _const_spec(w["ws"].shape), _const_spec((CHUNK, GMLP_CH))]
    args += [w["lng"], w["lnb"], w["ws"], w["bs"]]
    if rope:
        in_specs += [pl.BlockSpec((tr, LANES), lambda b, i: (i, 0))] * 2
        args += list(tables)
    out_shape = [jax.ShapeDtypeStruct((bm, sm, qw), BF16),
                 jax.ShapeDtypeStruct((bm, sm, kw), BF16),
                 jax.ShapeDtypeStruct((bm, sm, kw), BF16),
                 jax.ShapeDtypeStruct((bm, sm, GMLP_CH), BF16)]
    out_specs = [row(qw), row(kw), row(kw), row(GMLP_CH)]
    if not rope:
        assert bm == 1 and tr % ctx_seq == 0
        n_seq, per_step = sm // ctx_seq, tr // ctx_seq
        out_shape += [jax.ShapeDtypeStruct((n_seq, GQA_KV_HEADS, GQA_HEAD_DIM, ctx_seq), F32),
                      jax.ShapeDtypeStruct((n_seq, GQA_KV_HEADS * GQA_HEAD_DIM, ctx_seq), F32)]
        out_specs += [pl.BlockSpec((per_step, GQA_KV_HEADS, GQA_HEAD_DIM, ctx_seq), lambda b, i: (i, 0, 0, 0)),
                      pl.BlockSpec((per_step, GQA_KV_HEADS * GQA_HEAD_DIM, ctx_seq), lambda b, i: (i, 0, 0))]
    return pl.pallas_call(
        functools.partial(_l1_inproj_kernel, rope=rope),
        out_shape=out_shape, grid=(bm, sm // tr), in_specs=in_specs, out_specs=out_specs,
        compiler_params=_params(("parallel", "parallel"), VMEM_LIMIT),
        name="l1_inproj_rope" if rope else "l1_inproj",
    )(*args)


def _post_kernel(a_ref, c_ref, x_ref, m_ref, n2_ref, wa_ref, wc_ref, rw2_ref, rb_ref, tri_ref, base_ref,
                 x1_out, xm_out, idx_out, wts_out, rank_out, cnt_out, run_ref):
    first =(pl.program_id(0) == 0) & (pl.program_id(1) == 0)

    @pl.when(first)
    def _():
        run_ref[...] = base_ref[...]

    sub = tri_ref.shape[0]
    running = run_ref[0:1, :]
    for r0 in range(0, a_ref.shape[1], sub):
        running = _route_rows(slice(r0, r0 + sub), running, a_ref, c_ref, x_ref, m_ref, n2_ref, wa_ref, wc_ref,
                              rw2_ref, rb_ref, tri_ref, x1_out, xm_out, idx_out, wts_out, rank_out)
    run_ref[0:1, :] = running
    cnt_out[...] = run_ref[...]


def _route_rows(rs, running, a_ref, c_ref, x_ref, m_ref, n2_ref, wa_ref, wc_ref, rw2_ref, rb_ref, tri_ref,
                x1_out, xm_out, idx_out, wts_out, rank_out):
    m = m_ref[0]
    y = (jnp.dot(a_ref[0, rs, :], wa_ref[...], preferred_element_type=F32)
         + jnp.dot(c_ref[0, rs, :], wc_ref[...], preferred_element_type=F32))
    x1 = x_ref[0, rs, :] + m[2:3] * y
    x1_out[0, rs, :] = x1
    xm = _rms(x1, n2_ref[...]) * (1.0 + m[4:5]) + m[3:4]
    xh = xm.astype(BF16)
    xm_out[0, rs] = _pack_rows(xh)

    xl = (xm - xh.astype(F32)).astype(BF16)
    both = jnp.dot(xh, rw2_ref[...], preferred_element_type=F32)
    logits = (both[:, :LANES] + both[:, LANES:]
              + jnp.dot(xl, rw2_ref[:, :LANES], preferred_element_type=F32)) + rb_ref[...]
    rows = logits.shape[0]
    lane = lax.broadcasted_iota(jnp.int32, (rows, LANES), 1).astype(F32)
    work = logits
    vals, hots = [], []
    idx_acc = jnp.zeros((rows, LANES), F32)
    for k in range(TOP_K):
        top = jnp.max(work, axis=-1, keepdims=True)
        sel = jnp.min(jnp.where(work == top, lane, float(LANES)), axis=-1, keepdims=True)
        hot = lane == sel
        vals.append(top)
        hots.append(hot)
        idx_acc = idx_acc + jnp.where(lane == float(k), sel, 0.0)
        work = jnp.where(hot, -jnp.inf, work)
    exps = [jnp.exp(v - vals[0]) for v in vals]
    denom = exps[0] + exps[1] + exps[2] + exps[3]
    wcols = [jnp.broadcast_to(exps[k] / denom, (rows, LANES)) for k in range(TOP_K)]
    wcols += [jnp.zeros((rows, LANES), F32)] * (SUBLANES - TOP_K)
    wts = jnp.concatenate(wcols, axis=1).reshape(rows, SUBLANES, LANES)

    chosen = jnp.zeros((rows, LANES), F32)
    for hot in hots:
        chosen = chosen + hot.astype(F32)
    before = jnp.dot(tri_ref[...], chosen.astype(BF16), preferred_element_type=F32) + running
    rank = jnp.zeros((rows, LANES), F32)
    for k in range(TOP_K):
        rk = jnp.sum(jnp.where(hots[k], before, 0.0), axis=-1, keepdims=True)
        rank = rank + jnp.where(lane == float(k), rk, 0.0)
    idx_out[:, rs] = idx_acc.T[:SUBLANES, :].astype(jnp.int32)
    wts_out[0, rs] = wts
    rank_out[:, rs] = rank.T[:SUBLANES, :].astype(jnp.int32)
    return running + jnp.sum(chosen, axis=0, keepdims=True)


def post_mixer(attn, other, x, mods, mod_off, n2, wa, wc, rw2, rb, tri, base):
    bm, sm, d = x.shape
    tr = POST_SUBTILES * tri.shape[0]
    assert sm % tr == 0
    row = lambda width: pl.BlockSpec((1, tr, width), lambda b, i: (b, i, 0))
    tile_rows = pl.BlockSpec((1, tr) + ROW_WORDS, lambda b, i: (b, i, 0, 0))
    nb = sm // tr
    token_minor = pl.BlockSpec((SUBLANES, tr), lambda b, i: (0, b * nb + i))
    out_shape = [jax.ShapeDtypeStruct((bm, sm, d), F32),
                 jax.ShapeDtypeStruct((bm, sm) + ROW_WORDS, jnp.int32),
                 jax.ShapeDtypeStruct((SUBLANES, bm * sm), jnp.int32),
                 jax.ShapeDtypeStruct((bm, sm, SUBLANES, LANES), F32),
                 jax.ShapeDtypeStruct((SUBLANES, bm * sm), jnp.int32),
                 jax.ShapeDtypeStruct((SUBLANES, LANES), F32)]
    return pl.pallas_call(
        _post_kernel,
        out_shape=out_shape, grid=(bm, sm // tr),
        in_specs=[row(attn.shape[-1]), row(other.shape[-1]), row(d),
                  pl.BlockSpec((1, SUBLANES, d), lambda b, i: (b + mod_off, 0, 0)),
                  _const_spec((1, d)), _const_spec(wa.shape), _const_spec(wc.shape),
                  _const_spec(rw2.shape), _const_spec((1, LANES)),
                  _const_spec(tri.shape), _const_spec((SUBLANES, LANES))],
        out_specs=[row(d), tile_rows, token_minor,
                   pl.BlockSpec((1, tr, SUBLANES, LANES), lambda b, i: (b, i, 0, 0)),
                   token_minor, _const_spec((SUBLANES, LANES))],
        scratch_shapes=[pltpu.VMEM((SUBLANES, LANES), F32)],
        compiler_params=_params(("arbitrary", "arbitrary"), VMEM_LIMIT),
        name="post_mixer_route",
    )(attn, other, x, mods, n2, wa, wc, rw2, rb, tri, base)


def _plan_kernel(off_ref, idx_ref, rank_ref, pos_out):
    idx = idx_ref[...]
    pos = rank_ref[...]
    for e in range(N_EXPERTS):
        pos = pos + jnp.where(idx == e, off_ref[e], 0)
    pos_out[...] = pos


def plan_positions(offsets, idx, rank):
    n = idx.shape[1]
    tr = min(2048, n)
    spec = pl.BlockSpec((SUBLANES, tr), lambda i, off: (0, i))
    return pl.pallas_call(
        _plan_kernel,
        out_shape=jax.ShapeDtypeStruct((SUBLANES, n), jnp.int32),
        grid_spec=pltpu.PrefetchScalarGridSpec(
            num_scalar_prefetch=1, grid=(n // tr,), in_specs=[spec, spec], out_specs=spec),
        compiler_params=_params(("parallel",)),
        name="plan_positions",
    )(offsets, idx, rank)


def _sc_worker_id():
    return lax.axis_index("s") * SC_CORES + lax.axis_index("c")


def _sc_chunk_rows(tile, dtype):
    row_bytes = math.prod(tile) * jnp.dtype(dtype).itemsize
    return min(SC_CHUNK_BYTES // row_bytes, SC_MAX_INDICES)


def dispatch_rows(xms, posk, n_rows):
    n_total = sum(x.shape[0] for x in xms)
    tile, dtype = xms[0].shape[1:], xms[0].dtype
    chunk = _sc_chunk_rows(tile, dtype)
    starts, s0 = [], 0
    for x in xms:
        assert x.shape[0] % (SC_WORKERS * chunk) == 0
        starts.append(s0)
        s0 += x.shape[0]
    mesh = plsc.VectorSubcoreMesh(core_axis_name="c", subcore_axis_name="s")

    @functools.partial(
        pl.kernel, mesh=mesh, out_type=jax.ShapeDtypeStruct((n_rows,) + tile, dtype),
        scratch_types=[pltpu.VMEM((chunk,), jnp.int32), pltpu.VMEM((chunk,) + tile, dtype),
                       pltpu.SemaphoreType.DMA],
        name="dispatch_rows_sc")
    def scatter(*refs):
        x_refs, pos_hbm, xs_hbm, idx_v, rows_v, sem = refs[:len(xms)], *refs[len(xms):]
        wid = _sc_worker_id()
        for x_hbm, start in zip(x_refs, starts):
            per_worker = x_hbm.shape[0] // SC_WORKERS

            @pl.loop(0, per_worker // chunk)
            def _(c):
                t0 = wid * per_worker + c * chunk
                pltpu.sync_copy(x_hbm.at[pl.ds(t0, chunk)], rows_v)
                for k in range(TOP_K):
                    pltpu.sync_copy(pos_hbm.at[pl.ds(k * n_total + start + t0, chunk)], idx_v)
                    pltpu.async_copy(rows_v, xs_hbm.at[idx_v], sem).wait()

    return scatter(*xms, posk)


def gather_rows(y, posk):
    n_pairs = posk.shape[0]
    tile, dtype = y.shape[1:], y.dtype
    chunk = _sc_chunk_rows(tile, dtype)
    per_worker = n_pairs // SC_WORKERS
    assert per_worker % chunk == 0
    mesh = plsc.VectorSubcoreMesh(core_axis_name="c", subcore_axis_name="s")

    @functools.partial(
        pl.kernel, mesh=mesh, out_type=jax.ShapeDtypeStruct((n_pairs,) + tile, dtype),
        scratch_types=[pltpu.VMEM((chunk,), jnp.int32), pltpu.VMEM((chunk,) + tile, dtype),
                       pltpu.SemaphoreType.DMA],
        name="gather_rows_sc")
    def gather(y_hbm, pos_hbm, out_hbm, idx_v, rows_v, sem):
        wid = _sc_worker_id()

        @pl.loop(0, per_worker // chunk)
        def _(c):
            base = wid * per_worker + c * chunk
            pltpu.sync_copy(pos_hbm.at[pl.ds(base, chunk)], idx_v)
            pltpu.async_copy(y_hbm.at[idx_v], rows_v, sem).wait()
            pltpu.sync_copy(rows_v, out_hbm.at[pl.ds(base, chunk)])

    return gather(y, posk)


def _ffn_kernel(te_ref, nu_ref, first_ref, slot_ref, nxt_ref, xs_ref, w1_hbm, b1_ref, w2_hbm, b2_ref, y_ref,
                w1f, w2f, w1b, w2b, sem):
    i = pl.program_id(0)

    def weight_copies(e, s):
        return (pltpu.make_async_copy(w1_hbm.at[e], w1f.at[s], sem.at[0, s]),
                pltpu.make_async_copy(w2_hbm.at[e], w2f.at[s], sem.at[1, s]))

    @pl.when(i < nu_ref[0])
    def _():
        s = slot_ref[i]

        @pl.when(first_ref[i] == 1)
        def _():
            @pl.when(i == 0)
            def _():
                for cp in weight_copies(te_ref[i], s):
                    cp.start()
            for cp in weight_copies(te_ref[i], s):
                cp.wait()

            @pl.when(nxt_ref[i] >= 0)
            def _():
                for cp in weight_copies(nxt_ref[i], 1 - s):
                    cp.start()
            for c in range(D_MODEL // LANES):
                w1b[c * LANES:(c + 1) * LANES, :] = w1f[s, c * LANES:(c + 1) * LANES, :].astype(BF16)
            for c in range(D_EXPERT // LANES):
                w2b[c * LANES:(c + 1) * LANES, :] = w2f[s, c * LANES:(c + 1) * LANES, :].astype(BF16)

        x = _unpack_rows(xs_ref[...])
        h = jnp.dot(x, w1b[...], preferred_element_type=F32) + b1_ref[0]
        g = jnp.minimum(h[:, :D_EXPERT], SWIGLU_LIMIT)
        lin = jnp.clip(h[:, D_EXPERT:], -SWIGLU_LIMIT, SWIGLU_LIMIT)
        a = (lin + 1.0) * (g * jax.nn.sigmoid(SWIGLU_ALPHA * g))
        y = jnp.dot(a.astype(BF16), w2b[...], preferred_element_type=F32) + b2_ref[0]
        y_ref[...] = _pack_rows(y.astype(BF16))

    @pl.when(i >= nu_ref[0])
    def _():
        y_ref[...] = jnp.zeros(y_ref.shape, y_ref.dtype)


def grouped_ffn(sched, xs, w1, b1, w2, b2):
    r = xs.shape[0]
    d = w1.shape[1]
    nt = r // FFN_TILE
    tile = (FFN_TILE,) + xs.shape[1:]
    rows = lambda i, te, nu, *_: (jnp.minimum(i, nu[0] - 1), 0, 0)
    bsel = lambda i, te, *_: (te[i], 0, 0)
    return pl.pallas_call(
        _ffn_kernel,
        out_shape=jax.ShapeDtypeStruct(xs.shape, xs.dtype),
        grid_spec=pltpu.PrefetchScalarGridSpec(
            num_scalar_prefetch=5, grid=(nt,),
            in_specs=[pl.BlockSpec(tile, rows),
                      pl.BlockSpec(memory_space=pl.ANY),
                      pl.BlockSpec((1, 1, 2 * D_EXPERT), bsel),
                      pl.BlockSpec(memory_space=pl.ANY),
                      pl.BlockSpec((1, 1, d), bsel)],
            out_specs=pl.BlockSpec(tile, lambda i, *_: (i, 0, 0)),
            scratch_shapes=[pltpu.VMEM((2, d, 2 * D_EXPERT), F32), pltpu.VMEM((2, D_EXPERT, d), F32),
                            pltpu.VMEM((d, 2 * D_EXPERT), BF16), pltpu.VMEM((D_EXPERT, d), BF16),
                            pltpu.SemaphoreType.DMA((2, 2))]),
        compiler_params=_params(("arbitrary",), VMEM_LIMIT),
        name="grouped_ffn",
    )(*sched, xs, w1, b1.reshape(N_EXPERTS, 1, -1), w2, b2.reshape(N_EXPERTS, 1, -1))


def _combine_kernel(x1_ref, wts_ref, m_ref, fn_ref, y0_ref, y1_ref, y2_ref, y3_ref, o_ref, *, final):
    w = wts_ref[0]
    rows = lambda ref: pltpu.bitcast(ref[...], BF16).astype(F32)
    acc = w[:, 0:1, :] * rows(y0_ref)
    for k, y_ref in ((1, y1_ref), (2, y2_ref), (3, y3_ref)):
        acc = acc + w[:, k:k + 1, :] * rows(y_ref)
    out = x1_ref[0] + m_ref[0][5:6] * acc.reshape(x1_ref.shape[1], D_MODEL)
    if final:
        out = _rms(out, fn_ref[...])
    o_ref[0] = out


def combine_rows(x1, wts, mods, mod_off, fn, yg, row_off, n_total, *, final):
    bm, sm, d = x1.shape
    tr = min(COMBINE_TILE, sm)
    nb = sm // tr
    row = lambda width: pl.BlockSpec((1, tr, width), lambda b, i: (b, i, 0))
    ysel = lambda k: pl.BlockSpec((tr,) + yg.shape[1:],
                                  lambda b, i: ((k * n_total + row_off) // tr + b * nb + i, 0, 0))
    return pl.pallas_call(
        functools.partial(_combine_kernel, final=final),
        out_shape=jax.ShapeDtypeStruct((bm, sm, d), F32),
        grid=(bm, nb),
        in_specs=[row(d), pl.BlockSpec((1, tr, SUBLANES, LANES), lambda b, i: (b, i, 0, 0)),
                  pl.BlockSpec((1, SUBLANES, d), lambda b, i: (b + mod_off, 0, 0)),
                  _const_spec((1, d))] + [ysel(k) for k in range(TOP_K)],
        out_specs=row(d),
        compiler_params=_params(("parallel", "parallel"), VMEM_LIMIT),
        name="combine_rows",
    )(x1, wts, mods, fn, yg, yg, yg, yg)


def _axial_angles(n_tokens, rot_dim):
    t = jnp.arange(n_tokens)
    rows = (t // GRID_W).astype(F32)
    cols = (t % GRID_W).astype(F32)
    n_freq = rot_dim // 4
    inv = ROPE_THETA ** (-jnp.arange(n_freq, dtype=F32) / n_freq)
    return jnp.concatenate([rows[:, None] * inv, cols[:, None] * inv], axis=-1)


def _swap_halves(w):
    half = w.shape[-1] // 2
    return jnp.concatenate([-w[..., half:], w[..., :half]], axis=-1)


def _prep_l0(w_in, q_norm, kv_norm, w_uq, w_uk, w_uv, w_out):
    d = w_in.shape[0]
    o_kr = MLA_Q_RANK + MLA_KV_RANK
    kr_cols = w_in[:, o_kr:o_kr + MLA_ROPE]
    win = jnp.concatenate(
        [w_in[:, :o_kr], kr_cols, _swap_halves(kr_cols), jnp.zeros((d, LANES - 2 * MLA_ROPE), F32),
         w_in[:, o_kr + MLA_ROPE:]], axis=1).astype(BF16)
    qk = MLA_NOPE + MLA_ROPE
    wuq3 = w_uq.reshape(MLA_Q_RANK, MLA_HEADS, qk)
    wuq = jnp.pad(wuq3, ((0, 0), (0, 0), (0, LANES - qk))).reshape(MLA_Q_RANK, -1).astype(BF16)
    wuk3 = w_uk.reshape(MLA_KV_RANK, MLA_HEADS, MLA_NOPE)
    wuk = jnp.pad(wuk3, ((0, 0), (0, 0), (0, LANES - MLA_NOPE))).reshape(MLA_KV_RANK, -1).astype(BF16)
    wuv3 = w_uv.reshape(MLA_KV_RANK, MLA_HEADS, MLA_V)
    wuv = jnp.pad(wuv3, ((0, 0), (0, 0), (0, LANES - MLA_V))).reshape(MLA_KV_RANK, -1).astype(BF16)
    eye = jnp.eye(MLA_ROPE, dtype=F32)
    e_head = jnp.concatenate([jnp.zeros((MLA_ROPE, MLA_NOPE), F32), eye,
                              jnp.zeros((MLA_ROPE, LANES - qk), F32)], axis=1)
    e = jnp.pad(jnp.tile(e_head, (1, MLA_HEADS)), ((0, LANES - MLA_ROPE), (0, 0))).astype(BF16)
    wa3 = w_out[:MLA_HEADS * MLA_V].reshape(MLA_HEADS, MLA_V, d)
    wa = jnp.pad(wa3, ((0, 0), (0, LANES - MLA_V), (0, 0))).reshape(MLA_HEADS * LANES, d).astype(BF16)
    wc = w_out[MLA_HEADS * MLA_V:].astype(BF16)
    return dict(win=win, qg=q_norm.reshape(1, -1), kvg=kv_norm.reshape(1, -1), wuq=wuq,
                wuk=wuk, e=e, wuv=wuv, wa=wa, wc=wc)


def _l0_tables(n):
    ang = _axial_angles(n, MLA_ROPE)
    cos, sin = jnp.cos(ang), jnp.sin(ang)
    one = jnp.ones((n, 1), F32)
    zero = jnp.zeros((n, 1), F32)
    rest = LANES - MLA_NOPE - MLA_ROPE
    cq = jnp.concatenate([jnp.tile(one, (1, MLA_NOPE)), cos, cos, jnp.tile(one, (1, rest))], axis=1)
    sq = jnp.concatenate([jnp.tile(zero, (1, MLA_NOPE)), sin, sin, jnp.tile(zero, (1, rest))], axis=1)
    ck = jnp.concatenate([cos, cos, jnp.tile(zero, (1, LANES - MLA_ROPE))], axis=1)
    sk = jnp.concatenate([sin, sin, jnp.tile(zero, (1, LANES - MLA_ROPE))], axis=1)
    return cq, sq, ck, sk


def _pad_heads(w, n_heads, dim):
    d = w.shape[0]
    return jnp.pad(w.reshape(d, n_heads, dim), ((0, 0), (0, 0), (0, LANES - dim))).reshape(d, n_heads * LANES)


def _prep_l1(w_in, q_norm, k_norm, ln_g, ln_b, w_s, b_s, w_out):
    d = w_in.shape[0]
    qd = GQA_HEADS * GQA_HEAD_DIM
    kd = GQA_KV_HEADS * GQA_HEAD_DIM
    wq, wk, wv = w_in[:, :qd], w_in[:, qd:qd + kd], w_in[:, qd + kd:qd + 2 * kd]
    rest = w_in[:, qd + 2 * kd:]
    win = jnp.concatenate([_pad_heads(wq, GQA_HEADS, GQA_HEAD_DIM), _pad_heads(wk, GQA_KV_HEADS, GQA_HEAD_DIM),
                           _pad_heads(wv, GQA_KV_HEADS, GQA_HEAD_DIM), wv, rest], axis=1).astype(BF16)
    padg = lambda g: jnp.pad(g, (0, LANES - GQA_HEAD_DIM)).reshape(1, LANES)
    wa3 = w_out[:qd].reshape(GQA_HEADS, GQA_HEAD_DIM, d)
    wa = jnp.pad(wa3, ((0, 0), (0, LANES - GQA_HEAD_DIM), (0, 0))).reshape(GQA_HEADS * LANES, d).astype(BF16)
    wc = w_out[qd:].astype(BF16)
    bs = jnp.repeat(b_s.T, LANES, axis=1)
    return dict(win=win, qg=padg(q_norm), kg=padg(k_norm), lng=ln_g.reshape(1, -1), lnb=ln_b.reshape(1, -1),
                ws=w_s.astype(BF16), bs=bs, wa=wa, wc=wc)


def _l1_tables(n):
    ang = _axial_angles(n, GQA_HEAD_DIM)
    cos, sin = jnp.cos(ang), jnp.sin(ang)
    pad = LANES - GQA_HEAD_DIM
    c = jnp.concatenate([cos, cos, jnp.ones((n, pad), F32)], axis=1)
    s = jnp.concatenate([sin, sin, jnp.zeros((n, pad), F32)], axis=1)
    return c, s


def _pad_lanes(x, width):
    return jnp.pad(x, [(0, 0)] * (x.ndim - 1) + [(0, width - x.shape[-1])])


def routed_ffn(groups, mods, n2, moe, tri, final_norm, *, final):
    router_w, router_b, w1, b1, w2, b2 = moe
    rw = _pad_lanes(router_w, LANES)
    rwh = rw.astype(BF16)
    rwl = (rw - rwh.astype(F32)).astype(BF16)
    rw2 = jnp.concatenate([rwh, rwl], axis=1)
    rb = jnp.concatenate([router_b, jnp.full((LANES - N_EXPERTS,), NEG_BIG, F32)]).reshape(1, LANES)
    base = jnp.zeros((SUBLANES, LANES), F32)
    routed = []
    for g in groups:
        x1, xm, idx, wts, rank, base = post_mixer(
            g["attn"], g["other"], g["x"], mods, g["mod_off"], n2, g["wa"], g["wc"], rw2, rb, tri, base)
        routed.append((x1, xm, idx, wts, rank))
    n_total = sum(r[0].shape[0] * r[0].shape[1] for r in routed)
    nt = n_total * TOP_K // FFN_TILE + N_EXPERTS

    counts = base[0, :N_EXPERTS].astype(jnp.int32)
    tiles = (counts + FFN_TILE - 1) // FFN_TILE
    tile_end = jnp.cumsum(tiles)
    offsets = (tile_end - tiles) * FFN_TILE
    n_used = tile_end[-1:].astype(jnp.int32)
    experts = jnp.arange(N_EXPERTS)
    busy = tiles > 0
    slot_e = (jnp.cumsum(busy) - 1) % 2
    later = jnp.where(busy[None, :] & (experts[None, :] > experts[:, None]), experts[None, :], N_EXPERTS)
    nxt_e = jnp.min(later, axis=1)
    nxt_e = jnp.where(nxt_e == N_EXPERTS, -1, nxt_e)
    tile_expert = jnp.sum(jnp.arange(nt)[:, None] >= tile_end[None, :], axis=1)
    tile_expert = jnp.minimum(tile_expert, jnp.max(jnp.where(busy, experts, 0)))
    first = jnp.concatenate([jnp.ones((1,), bool), tile_expert[1:] != tile_expert[:-1]])
    of_tile = tile_expert[:, None] == experts[None, :]
    slot_t = jnp.sum(jnp.where(of_tile, slot_e[None, :], 0), axis=1)
    nxt_t = jnp.sum(jnp.where(of_tile, nxt_e[None, :], 0), axis=1)
    sched = tuple(a.astype(jnp.int32) for a in (tile_expert, n_used, first, slot_t, nxt_t))

    positions, xms = [], []
    for (x1, xm, idx, wts, rank) in routed:
        positions.append(plan_positions(offsets, idx, rank)[:TOP_K])
        xms.append(xm.reshape((idx.shape[1],) + ROW_WORDS))
    posk = jnp.concatenate(positions, axis=1).reshape(-1)
    xs = dispatch_rows(xms, posk, nt * FFN_TILE)
    y = grouped_ffn(sched, xs, w1, b1, w2, b2)
    outs = []
    for g, (x1, xm, idx, wts, rank), pos in zip(groups, routed, positions):
        yg = gather_rows(y, pos.reshape(-1))
        outs.append(combine_rows(x1, wts, mods, g["mod_off"], final_norm, yg, 0, pos.shape[1], final=final))
    return outs


def kernel(x_prompt, x_sample, cache_l0_ckv, cache_l0_krope, cache_l1_k, cache_l1_v, c, c_ctx,
           l0_ada_w, l0_ada_b, l0_norm1, l0_w_in, l0_q_norm, l0_kv_norm, l0_w_uq, l0_w_uk, l0_w_uv,
           l0_conv_w, l0_conv_b, l0_conv_ln_g, l0_conv_ln_b, l0_w_out, l0_norm2,
           l0_router_w, l0_router_b, l0_w1, l0_b1, l0_w2, l0_b2,
           l1_ada_w, l1_ada_b, l1_norm1, l1_w_in, l1_q_norm, l1_k_norm, l1_gmlp_ln_g, l1_gmlp_ln_b,
           l1_w_s, l1_b_s, l1_w_out, l1_norm2,
           l1_router_w, l1_router_b, l1_w1, l1_b1, l1_w2, l1_b2,
           final_norm):
    bp, sp, d = x_prompt.shape
    bs, ss, _ = x_sample.shape
    past = cache_l0_ckv.shape[1]
    n_p = bp * sp

    cond8 = jnp.concatenate([c_ctx[None], c, jnp.zeros((SUBLANES - 1 - bs, d), F32)], axis=0)
    mods0 = adaln(cond8, l0_ada_w, l0_ada_b)
    mods1 = adaln(cond8, l1_ada_w, l1_ada_b)
    tri = jnp.tril(jnp.ones((ROW_TILE, ROW_TILE), F32), -1).astype(BF16)
    fn = final_norm.reshape(1, d)

    w0 = _prep_l0(l0_w_in, l0_q_norm, l0_kv_norm, l0_w_uq, l0_w_uk, l0_w_uv, l0_w_out)
    n1 = l0_norm1.reshape(1, d)
    hp = x_prompt.reshape(1, n_p, d)
    q_p, k_p, v_p, ckv_p, kr_p, u_p = l0_inproj(hp, mods0, 0, n1, w0, None)
    q_s, k_s, v_s, _, _, u_s = l0_inproj(x_sample, mods0, 1, n1, w0, _l0_tables(ss))
    k_c, v_c = mla_ctx_kv(cache_l0_ckv, _pad_lanes(cache_l0_krope, LANES), w0)
    hw = MLA_HEADS * LANES
    att_p = attention(q_p.reshape(bp, sp, hw), k_p.reshape(bp, sp, hw), v_p.reshape(bp, sp, hw),
                      n_heads=MLA_HEADS, n_kv=MLA_HEADS, heads_per_step=MLA_HEADS)
    att_s = attention(q_s, jnp.concatenate([k_c, k_s], axis=1), jnp.concatenate([v_c, v_s], axis=1),
                      n_heads=MLA_HEADS, n_kv=MLA_HEADS, heads_per_step=LATENT_HEADS_PER_STEP)
    conv_p = conformer_conv(u_p.reshape(bp, sp, CONV_CH), l0_conv_w, l0_conv_b, l0_conv_ln_g, l0_conv_ln_b)
    conv_s = conformer_conv(u_s, l0_conv_w, l0_conv_b, l0_conv_ln_g, l0_conv_ln_b)
    groups = [dict(attn=att_p.reshape(1, n_p, hw), other=conv_p.reshape(1, n_p, CONV_CH), x=hp, mod_off=0,
                   wa=w0["wa"], wc=w0["wc"]),
              dict(attn=att_s, other=conv_s, x=x_sample, mod_off=1, wa=w0["wa"], wc=w0["wc"])]
    hp, hs = routed_ffn(groups, mods0, l0_norm2.reshape(1, d),
                        (l0_router_w, l0_router_b, l0_w1, l0_b1, l0_w2, l0_b2), tri, fn, final=False)
    new_l0_ckv = ckv_p.reshape(bp, sp, MLA_KV_RANK)
    new_l0_krope = kr_p.reshape(bp, sp, MLA_ROPE)

    w1p = _prep_l1(l1_w_in, l1_q_norm, l1_k_norm, l1_gmlp_ln_g, l1_gmlp_ln_b, l1_w_s, l1_b_s, l1_w_out)
    n1 = l1_norm1.reshape(1, d)
    q_p, k_p, vp_p, gat_p, kt_p, vt_p = l1_inproj(hp, mods1, 0, n1, w1p, None, ctx_seq=sp)
    q_s, k_s, vp_s, gat_s = l1_inproj(hs, mods1, 1, n1, w1p, _l1_tables(ss))
    qw = GQA_HEADS * LANES
    kw = GQA_KV_HEADS * LANES
    pad_kv = lambda t: _pad_lanes(t, LANES).reshape(bs, past, kw).astype(BF16)
    att_p = attention(q_p.reshape(bp, sp, qw), k_p.reshape(bp, sp, kw), vp_p.reshape(bp, sp, kw),
                      n_heads=GQA_HEADS, n_kv=GQA_KV_HEADS, heads_per_step=GQA_HEADS)
    att_s = attention(q_s, jnp.concatenate([pad_kv(cache_l1_k), k_s], axis=1),
                      jnp.concatenate([_with_sum_lane(pad_kv(cache_l1_v)), vp_s], axis=1),
                      n_heads=GQA_HEADS, n_kv=GQA_KV_HEADS, heads_per_step=LATENT_HEADS_PER_STEP)
    groups = [dict(attn=att_p.reshape(1, n_p, qw), other=gat_p, x=hp, mod_off=0, wa=w1p["wa"], wc=w1p["wc"]),
              dict(attn=att_s, other=gat_s, x=hs, mod_off=1, wa=w1p["wa"], wc=w1p["wc"])]
    yp, ys = routed_ffn(groups, mods1, l1_norm2.reshape(1, d),
                        (l1_router_w, l1_router_b, l1_w1, l1_b1, l1_w2, l1_b2), tri, fn, final=True)
    new_l1_k = jnp.transpose(kt_p, (0, 3, 1, 2))
    new_l1_v = jnp.transpose(vt_p.reshape(bp, GQA_KV_HEADS, GQA_HEAD_DIM, sp), (0, 3, 1, 2))
    return (yp.reshape(bp, sp, d), ys, new_l0_ckv, new_l0_krope, new_l1_k, new_l1_v)
```

```python
import functools
import math

import jax
import jax.numpy as jnp
from jax import lax
from jax.experimental import pallas as pl
from jax.experimental.pallas import tpu as pltpu
from jax.experimental.pallas import tpu_sc as plsc

F32 = jnp.float32
BF16 = jnp.bfloat16
HIGHEST = lax.Precision.HIGHEST

LANES = 128
SUBLANES = 8
VMEM_LIMIT = 56 * 1024 * 1024

D_MODEL = 1024
GRID_W = 64
ROPE_THETA = 10000.0
EPS = 1e-6
N_MOD = 6

MLA_HEADS = 8
MLA_NOPE = 64
MLA_ROPE = 32
MLA_V = 64
MLA_Q_RANK = 384
MLA_KV_RANK = 256
MLA_SCALE = 1.0 / math.sqrt(MLA_NOPE + MLA_ROPE)
CONV_CH = 512
CONV_WIDTH = 31
CONV_HALO = 16

GQA_HEADS = 8
GQA_KV_HEADS = 2
GQA_HEAD_DIM = 64
GQA_SCALE = 1.0 / math.sqrt(GQA_HEAD_DIM)
CHUNK = 128
GMLP_GROUPS = 4
GMLP_CH = 512

N_EXPERTS = 32
TOP_K = 4
D_EXPERT = 1024
SWIGLU_LIMIT = 7.0
SWIGLU_ALPHA = 1.702

ROW_TILE = 512
POST_SUBTILES = 2
FFN_TILE = 256
SC_CORES = 2
SC_SUBCORES = 16
SC_WORKERS = SC_CORES * SC_SUBCORES
SC_CHUNK_BYTES = 256 * 1024
SC_MAX_INDICES = 128
COMBINE_TILE = 512
ATT_Q_TILE = 256
ATT_LONG_KEYS = 1024
V_SUM_LANE = 64
LATENT_HEADS_PER_STEP = 4
NEG_BIG = -1e30


def _params(sem, vmem=None):
    return pltpu.CompilerParams(dimension_semantics=sem, vmem_limit_bytes=vmem)


def _rms(x, g):
    return x * lax.rsqrt(jnp.mean(x * x, axis=-1, keepdims=True) + EPS) * g


ROW_WORDS = (SUBLANES // 2, LANES)


def _pack_rows(x_bf16):
    return pltpu.bitcast(x_bf16.reshape(x_bf16.shape[0], SUBLANES, LANES), jnp.int32)


def _unpack_rows(words):
    return pltpu.bitcast(words, BF16).reshape(words.shape[0], D_MODEL)


def _const_spec(shape):
    nd = len(shape)
    return pl.BlockSpec(shape, lambda *_: (0,) * nd)


def _adaln_kernel(c_ref, w_ref, b_ref, o_ref):
    c = c_ref[...]
    s = c * jax.nn.sigmoid(c)
    o_ref[...] = jnp.dot(s, w_ref[...], preferred_element_type=F32, precision=HIGHEST) + b_ref[...]


def adaln(cond8, ada_w, ada_b):
    d, n = ada_w.shape
    bn = n // 4
    m = pl.pallas_call(
        _adaln_kernel,
        out_shape=jax.ShapeDtypeStruct((SUBLANES, n), F32),
        grid=(n // bn,),
        in_specs=[_const_spec((SUBLANES, d)),
                  pl.BlockSpec((d, bn), lambda j: (0, j)),
                  pl.BlockSpec((1, bn), lambda j: (0, j))],
        out_specs=pl.BlockSpec((SUBLANES, bn), lambda j: (0, j)),
        compiler_params=_params(("arbitrary",), VMEM_LIMIT),
        name="adaln",
    )(cond8, ada_w, ada_b.reshape(1, n))
    m = m.reshape(SUBLANES, N_MOD, d)
    return jnp.pad(m, ((0, 0), (0, SUBLANES - N_MOD), (0, 0)))


def _l0_inproj_kernel(*refs, rope):
    if not rope:
        _l0_rows(*refs, rope=False)
        return
    wuk_ref, e_ref, wuv_ref, cckv_ref, ckr_ref = refs[7:12]
    row_refs = refs[:10] + refs[12:]
    k_out, v_out = row_refs[15], row_refs[16]
    i = pl.program_id(1)

    @pl.when(i == 0)
    def _():
        ckv_b = cckv_ref[0].astype(BF16)
        k = (jnp.dot(ckv_b, wuk_ref[...], preferred_element_type=F32)
             + jnp.dot(ckr_ref[0].astype(BF16), e_ref[...], preferred_element_type=F32))
        k_out[0] = k.astype(BF16)
        v_out[0] = _with_sum_lane(jnp.dot(ckv_b, wuv_ref[...], preferred_element_type=F32)).astype(BF16)

    @pl.when(i > 0)
    def _():
        _l0_rows(*row_refs, rope=True)


def _l0_rows(*refs, rope):
    if rope:
        (x_ref, m_ref, n1_ref, win_ref, qg_ref, kvg_ref, wuq_ref, wuk_ref, e_ref, wuv_ref,
         cq_ref, sq_ref, ck_ref, sk_ref, q_out, k_out, v_out, ckv_out, kr_out, u_out) = refs
    else:
        (x_ref, m_ref, n1_ref, win_ref, qg_ref, kvg_ref, wuq_ref, wuk_ref, e_ref, wuv_ref,
         q_out, k_out, v_out, ckv_out, kr_out, u_out) = refs
    x = x_ref[0]
    m = m_ref[0]
    h = _rms(x, n1_ref[...]) * (1.0 + m[1:2]) + m[0:1]
    z = jnp.dot(h.astype(BF16), win_ref[...], preferred_element_type=F32)
    c_q = z[:, 0:MLA_Q_RANK]
    c_kv = z[:, MLA_Q_RANK:MLA_Q_RANK + MLA_KV_RANK]
    o_kr = MLA_Q_RANK + MLA_KV_RANK
    o_conv = o_kr + LANES
    kr_blk = z[:, o_kr:o_conv]
    val = z[:, o_conv:o_conv + CONV_CH]
    gate = z[:, o_conv + CONV_CH:o_conv + 2 * CONV_CH]

    cqn = _rms(c_q, qg_ref[...]).astype(BF16)
    q = jnp.dot(cqn, wuq_ref[...], preferred_element_type=F32)
    if rope:
        cq = cq_ref[...]
        sq = sq_ref[...]
        half = MLA_ROPE // 2
        lane = lax.broadcasted_iota(jnp.int32, (1, LANES), 1)
        first_half = (lane >= MLA_NOPE) & (lane < MLA_NOPE + half)
        for hd in range(MLA_HEADS):
            sl = slice(hd * LANES, (hd + 1) * LANES)
            blk = q[:, sl]
            partner = jnp.where(first_half, -pltpu.roll(blk, LANES - half, 1), pltpu.roll(blk, half, 1))
            q_out[0, :, sl] = ((blk * cq + partner * sq) * MLA_SCALE).astype(BF16)
        kr = kr_blk * ck_ref[...] + pltpu.roll(kr_blk, LANES - MLA_ROPE, 1) * sk_ref[...]
    else:
        q_out[0] = (q * MLA_SCALE).astype(BF16)
        kr = kr_blk

    ckv = _rms(c_kv, kvg_ref[...])
    ckv_out[0] = ckv
    kr_out[0] = kr_blk[:, 0:MLA_ROPE]
    ckv_b = ckv.astype(BF16)
    k = (jnp.dot(ckv_b, wuk_ref[...], preferred_element_type=F32)
         + jnp.dot(kr.astype(BF16), e_ref[...], preferred_element_type=F32))
    k_out[0] = k.astype(BF16)
    v_out[0] = _with_sum_lane(jnp.dot(ckv_b, wuv_ref[...], preferred_element_type=F32)).astype(BF16)
    u_out[0] = val * jax.nn.sigmoid(gate)


def l0_inproj(x, mods, mod_off, n1, w, tables, cache=None):
    bm, sm, d = x.shape
    tr = min(ROW_TILE, sm)
    rope = tables is not None
    assert rope == (cache is not None)
    hp = MLA_HEADS * LANES
    lead = 1 if rope else 0
    past = cache[0].shape[1] if rope else 0
    assert past == lead * tr
    rows = lambda b, i: (b, jnp.maximum(i - lead, 0), 0)
    row = lambda width: pl.BlockSpec((1, tr, width), rows)
    in_specs = [row(d),
                pl.BlockSpec((1, SUBLANES, d), lambda b, i: (b + mod_off, 0, 0)),
                _const_spec((1, d)), _const_spec(w["win"].shape),
                _const_spec((1, MLA_Q_RANK)), _const_spec((1, MLA_KV_RANK)),
                _const_spec(w["wuq"].shape)]
    args = [x, mods, n1, w["win"], w["qg"], w["kvg"], w["wuq"]]
    in_specs +=[_const_spec(w["wuk"].shape), _const_spec(w["e"].shape), _const_spec(w["wuv"].shape)]
    args += [w["wuk"], w["e"], w["wuv"]]
    if rope:
        in_specs += [pl.BlockSpec((1, past, c.shape[-1]), lambda b, i: (b, 0, 0)) for c in cache]
        args += list(cache)
        in_specs += [pl.BlockSpec((tr, LANES), lambda b, i: (jnp.maximum(i - lead, 0), 0))] * 4
        args += list(tables)
    out_shape = [jax.ShapeDtypeStruct((bm, sm, hp), BF16),
                 jax.ShapeDtypeStruct((bm, past + sm, hp), BF16),
                 jax.ShapeDtypeStruct((bm, past + sm, hp), BF16),
                 jax.ShapeDtypeStruct((bm, sm, MLA_KV_RANK), F32),
                 jax.ShapeDtypeStruct((bm, sm, MLA_ROPE), F32),
                 jax.ShapeDtypeStruct((bm, sm, CONV_CH), F32)]
    kv_rows = pl.BlockSpec((1, tr, hp), lambda b, i: (b, i, 0))
    out_specs = [row(hp), kv_rows, kv_rows, row(MLA_KV_RANK), row(MLA_ROPE), row(CONV_CH)]
    return pl.pallas_call(
        functools.partial(_l0_inproj_kernel, rope=rope),
        out_shape=out_shape, grid=(bm, lead + sm // tr), in_specs=in_specs, out_specs=out_specs,
        compiler_params=_params(("parallel", "arbitrary"), VMEM_LIMIT),
        name="l0_inproj_rope" if rope else "l0_inproj",
    )(*args)


def _conv_kernel(prev_ref, cur_ref, next_ref, w_ref, b_ref, g_ref, beta_ref, o_ref, pad_ref, sh_ref, *, rb):
    i = pl.program_id(1)
    last = pl.num_programs(1) - 1
    zeros = jnp.zeros((CONV_HALO, CONV_CH), F32)
    pad_ref[0:CONV_HALO, :] = jnp.where(i == 0, zeros, prev_ref[0])
    pad_ref[CONV_HALO:CONV_HALO + rb, :] = cur_ref[0]
    pad_ref[CONV_HALO + rb:CONV_HALO + rb + CONV_HALO, :] = jnp.where(i == last, zeros, next_ref[0])
    span = rb + 2 * CONV_HALO - SUBLANES
    for r in range(1, SUBLANES):
        sh_ref[r] = pad_ref[r:r + span, :]
    w = w_ref[...]
    shift = CONV_HALO - CONV_WIDTH // 2
    acc = jnp.zeros((rb, CONV_CH), F32) + b_ref[...]
    for k in range(CONV_WIDTH):
        off = k + shift
        r, a = off % SUBLANES, off // SUBLANES * SUBLANES
        window = pad_ref[a:a + rb, :] if r == 0 else sh_ref[r, a:a + rb, :]
        acc = acc + window * w[k:k + 1, :]
    mu = jnp.mean(acc, axis=-1, keepdims=True)
    cen = acc - mu
    var = jnp.mean(cen * cen, axis=-1, keepdims=True)
    y = cen * lax.rsqrt(var + EPS) * g_ref[...] + beta_ref[...]
    o_ref[0] = (y * jax.nn.sigmoid(y)).astype(BF16)


def conformer_conv(u, conv_w, conv_b, ln_g, ln_b):
    b, s, c = u.shape
    rb = min(256, s)
    nh = rb // CONV_HALO
    n_halo_blocks = s // CONV_HALO
    wpad = jnp.pad(conv_w.reshape(CONV_WIDTH, c), ((0, 32 - CONV_WIDTH), (0, 0)))
    return pl.pallas_call(
        functools.partial(_conv_kernel, rb=rb),
        out_shape=jax.ShapeDtypeStruct((b, s, c), BF16),
        grid=(b, s // rb),
        in_specs=[pl.BlockSpec((1, CONV_HALO, c), lambda bi, i: (bi, jnp.maximum(i * nh - 1, 0), 0)),
                  pl.BlockSpec((1, rb, c), lambda bi, i: (bi, i, 0)),
                  pl.BlockSpec((1, CONV_HALO, c),
                               lambda bi, i: (bi, jnp.minimum((i + 1) * nh, n_halo_blocks - 1), 0)),
                  _const_spec((32, c)), _const_spec((1, c)), _const_spec((1, c)), _const_spec((1, c))],
        out_specs=pl.BlockSpec((1, rb, c), lambda bi, i: (bi, i, 0)),
        scratch_shapes=[pltpu.VMEM((rb + 2 * CONV_HALO, c), F32),
                        pltpu.VMEM((SUBLANES, rb + 2 * CONV_HALO - SUBLANES, c), F32)],
        compiler_params=_params(("parallel", "parallel"), VMEM_LIMIT),
        name="conformer_conv",
    )(u, u, u, wpad, conv_b.reshape(1, c), ln_g.reshape(1, c), ln_b.reshape(1, c))


def _with_sum_lane(v):
    lane = lax.broadcasted_iota(jnp.int32, (1, v.shape[-1]), 1)
    return v + ((lane & (LANES - 1)) == V_SUM_LANE).astype(v.dtype)


def _attn_kernel(q_ref, k_ref, v_ref, o_ref, *, heads, rep, mxu_denominator):
    for hd in range(heads):
        g = hd // rep
        q = q_ref[0, :, hd * LANES:(hd + 1) * LANES]
        k = k_ref[0, :, g * LANES:(g + 1) * LANES]
        s = lax.dot_general(q, k, (((1,), (1,)), ((), ())), preferred_element_type=F32)
        m = jnp.max(s, axis=-1, keepdims=True)
        if mxu_denominator:
            p = jnp.exp((s - m).astype(BF16))
            o = jnp.dot(p, v_ref[0, :, g * LANES:(g + 1) * LANES], preferred_element_type=F32)
            l = o[:, V_SUM_LANE:V_SUM_LANE + 1]
        else:
            p = jnp.exp(s - m)
            l = jnp.sum(p, axis=-1, keepdims=True)
            o = jnp.dot(p.astype(BF16), v_ref[0, :, g * LANES:(g + 1) * LANES], preferred_element_type=F32)
        o_ref[0, :, hd * LANES:(hd + 1) * LANES] = (o / l).astype(BF16)


def attention(q, k, v, *, n_heads, n_kv, heads_per_step):
    b, sq, _ = q.shape
    sk = k.shape[1]
    rep = n_heads // n_kv
    tq = min(ATT_Q_TILE, sq)
    hb = heads_per_step
    grid = (b, n_heads // hb, sq // tq)
    if hb >= rep:
        kv_spec = pl.BlockSpec((1, sk, hb // rep * LANES), lambda bi, h, i: (bi, 0, h))
        kern_rep = rep
    else:
        assert rep % hb == 0
        kv_spec = pl.BlockSpec((1, sk, LANES), lambda bi, h, i: (bi, 0, h * hb // rep))
        kern_rep = hb
    kern = functools.partial(_attn_kernel, heads=hb, rep=kern_rep, mxu_denominator=sk >= ATT_LONG_KEYS)
    q_spec = pl.BlockSpec((1, tq, heads_per_step * LANES), lambda bi, h, i: (bi, i, h))
    return pl.pallas_call(
        kern,
        out_shape=jax.ShapeDtypeStruct(q.shape, BF16),
        grid=grid, in_specs=[q_spec, kv_spec, kv_spec], out_specs=q_spec,
        compiler_params=_params(("parallel", "parallel", "parallel"), VMEM_LIMIT),
        name="attention",
    )(q, k, v)


def _l1_inproj_kernel(*refs, rope):
    if rope:
        (x_ref, m_ref, n1_ref, win_ref, qg_ref, kg_ref, lng_ref, lnb_ref,
         ws_ref, bs_ref, c_ref, s_ref, q_out, k_out, vp_out, g_out) = refs
        kt_out = vt_out = None
    else:
        (x_ref, m_ref, n1_ref, win_ref, qg_ref, kg_ref, lng_ref, lnb_ref,
         ws_ref, bs_ref, q_out, k_out, vp_out, g_out, kt_out, vt_out) = refs
    x = x_ref[0]
    m = m_ref[0]
    hb = (_rms(x, n1_ref[...]) * (1.0 + m[1:2]) + m[0:1]).astype(BF16)
    z = jnp.dot(hb, win_ref[...], preferred_element_type=F32)
    qw = GQA_HEADS * LANES
    kw = GQA_KV_HEADS * LANES
    o_k, o_vp, o_v, o_u, o_vg = qw, qw + kw, qw + 2 * kw, qw + 2 * kw + LANES, qw + 2 * kw + LANES + GMLP_CH
    if rope:
        cos = c_ref[...]
        sin = s_ref[...]
        half = GQA_HEAD_DIM // 2
        first_half = lax.broadcasted_iota(jnp.int32, (1, LANES), 1) < half

    def head(col, g_ref):
        t = z[:, col:col + LANES]
        r = lax.rsqrt(jnp.sum(t * t, axis=-1, keepdims=True) * (1.0 / GQA_HEAD_DIM) + EPS)
        normed = t * r * g_ref[...]
        if not rope:
            return normed, normed
        partner = jnp.where(first_half, -pltpu.roll(normed, LANES - half, 1), pltpu.roll(normed, half, 1))
        return normed, normed * cos + partner * sin

    for hd in range(GQA_HEADS):
        _, rot = head(hd * LANES, qg_ref)
        q_out[0, :, hd * LANES:(hd + 1) * LANES] = (rot * GQA_SCALE).astype(BF16)
    for hd in range(GQA_KV_HEADS):
        normed, rot = head(o_k + hd * LANES, kg_ref)
        k_out[0, :, hd * LANES:(hd + 1) * LANES] = rot.astype(BF16)
        if kt_out is not None:
            seq = kt_out.shape[-1]
            for s in range(kt_out.shape[0]):
                kt_out[s, hd] = normed[s * seq:(s + 1) * seq, :].T[:GQA_HEAD_DIM, :]
    vp_out[0] = _with_sum_lane(z[:, o_vp:o_vp + kw]).astype(BF16)
    if vt_out is not None:
        seq = vt_out.shape[-1]
        for s in range(vt_out.shape[0]):
            vt_out[s] = z[s * seq:(s + 1) * seq, o_v:o_v + LANES].T

    u = z[:, o_u:o_u + GMLP_CH]
    vg = z[:, o_vg:o_vg + GMLP_CH]
    mu = jnp.mean(vg, axis=-1, keepdims=True)
    cen = vg - mu
    var = jnp.mean(cen * cen, axis=-1, keepdims=True)
    vn = (cen * lax.rsqrt(var + EPS) * lng_ref[...] + lnb_ref[...]).astype(BF16)
    bias = bs_ref[...]
    rows = x.shape[0]
    for cidx in range(rows // CHUNK):
        r0 = cidx * CHUNK
        for g in range(GMLP_GROUPS):
            c0 = g * LANES
            mixed = jnp.dot(ws_ref[g], vn[r0:r0 + CHUNK, c0:c0 + LANES], preferred_element_type=F32)
            g_out[0, r0:r0 + CHUNK, c0:c0 + LANES] = (
                u[r0:r0 + CHUNK, c0:c0 + LANES] * (mixed + bias[:, c0:c0 + LANES])).astype(BF16)


def l1_inproj(x, mods, mod_off, n1, w, tables, ctx_seq=None):
    bm, sm, d = x.shape
    tr = min(ROW_TILE, sm)
    rope = tables is not None
    assert rope != (ctx_seq is not None)
    qw = GQA_HEADS * LANES
    kw = GQA_KV_HEADS * LANES
    row = lambda width: pl.BlockSpec((1, tr, width), lambda b, i: (b, i, 0))
    vec = _const_spec((1, LANES))
    in_specs = [row(d), pl.BlockSpec((1, SUBLANES, d), lambda b, i: (b + mod_off, 0, 0)),
                _const_spec((1, d)), _const_spec(w["win"].shape)]
    args = [x, mods, n1, w["win"]]
    in_specs += [vec, vec]
    args += [w["qg"], w["kg"]]
    in_specs += [_const_spec((1, GMLP_CH)), _const_spec((1, GMLP_CH)),
                 _const_spec(w["ws"].shape), _const_spec((CHUNK, GMLP_CH))]
    args += [w["lng"], w["lnb"], w["ws"], w["bs"]]
    if rope:
        in_specs += [pl.BlockSpec((tr, LANES), lambda b, i: (i, 0))] * 2
        args += list(tables)
    out_shape = [jax.ShapeDtypeStruct((bm, sm, qw), BF16),
                 jax.ShapeDtypeStruct((bm, sm, kw), BF16),
                 jax.ShapeDtypeStruct((bm, sm, kw), BF16),
                 jax.ShapeDtypeStruct((bm, sm, GMLP_CH), BF16)]
    out_specs = [row(qw), row(kw), row(kw), row(GMLP_CH)]
    if not rope:
        assert bm == 1 and tr % ctx_seq == 0
        n_seq, per_step = sm // ctx_seq, tr // ctx_seq
        out_shape += [jax.ShapeDtypeStruct((n_seq, GQA_KV_HEADS, GQA_HEAD_DIM, ctx_seq), F32),
                      jax.ShapeDtypeStruct((n_seq, GQA_KV_HEADS * GQA_HEAD_DIM, ctx_seq), F32)]
        out_specs += [pl.BlockSpec((per_step, GQA_KV_HEADS, GQA_HEAD_DIM, ctx_seq), lambda b, i: (i, 0, 0, 0)),
                      pl.BlockSpec((per_step, GQA_KV_HEADS * GQA_HEAD_DIM, ctx_seq), lambda b, i: (i, 0, 0))]
    return pl.pallas_call(
        functools.partial(_l1_inproj_kernel, rope=rope),
        out_shape=out_shape, grid=(bm, sm // tr), in_specs=in_specs, out_specs=out_specs,
        compiler_params=_params(("parallel", "parallel"), VMEM_LIMIT),
        name="l1_inproj_rope" if rope else "l1_inproj",
    )(*args)


def _post_kernel(a_ref, c_ref, x_ref, m_ref, n2_ref, wa_ref, wc_ref, rw2_ref, rb_ref, tri_ref, base_ref,
                 x1_out, xm_out, idx_out, wts_out, rank_out, cnt_out, run_ref):
    first =(pl.program_id(0) == 0) & (pl.program_id(1) == 0)

    @pl.when(first)
    def _():
        run_ref[...] = base_ref[...]

    sub = tri_ref.shape[0]
    running = run_ref[0:1, :]
    for r0 in range(0, a_ref.shape[1], sub):
        running = _route_rows(slice(r0, r0 + sub), running, a_ref, c_ref, x_ref, m_ref, n2_ref, wa_ref, wc_ref,
                              rw2_ref, rb_ref, tri_ref, x1_out, xm_out, idx_out, wts_out, rank_out)
    run_ref[0:1, :] = running
    cnt_out[...] = run_ref[...]


def _route_rows(rs, running, a_ref, c_ref, x_ref, m_ref, n2_ref, wa_ref, wc_ref, rw2_ref, rb_ref, tri_ref,
                x1_out, xm_out, idx_out, wts_out, rank_out):
    m = m_ref[0]
    y = (jnp.dot(a_ref[0, rs, :], wa_ref[...], preferred_element_type=F32)
         + jnp.dot(c_ref[0, rs, :], wc_ref[...], preferred_element_type=F32))
    x1 = x_ref[0, rs, :] + m[2:3] * y
    x1_out[0, rs, :] = x1
    xm = _rms(x1, n2_ref[...]) * (1.0 + m[4:5]) + m[3:4]
    xh = xm.astype(BF16)
    xm_out[0, rs] = _pack_rows(xh)

    xl = (xm - xh.astype(F32)).astype(BF16)
    both = jnp.dot(xh, rw2_ref[...], preferred_element_type=F32)
    logits = (both[:, :LANES] + both[:, LANES:]
              + jnp.dot(xl, rw2_ref[:, :LANES], preferred_element_type=F32)) + rb_ref[...]
    rows = logits.shape[0]
    lane = lax.broadcasted_iota(jnp.int32, (rows, LANES), 1).astype(F32)
    work = logits
    vals, hots = [], []
    idx_acc = jnp.zeros((rows, LANES), F32)
    for k in range(TOP_K):
        top = jnp.max(work, axis=-1, keepdims=True)
        sel = jnp.min(jnp.where(work == top, lane, float(LANES)), axis=-1, keepdims=True)
        hot = lane == sel
        vals.append(top)
        hots.append(hot)
        idx_acc = idx_acc + jnp.where(lane == float(k), sel, 0.0)
        work = jnp.where(hot, -jnp.inf, work)
    exps = [jnp.exp(v - vals[0]) for v in vals]
    denom = exps[0] + exps[1] + exps[2] + exps[3]
    wcols = [jnp.broadcast_to(exps[k] / denom, (rows, LANES)) for k in range(TOP_K)]
    wcols += [jnp.zeros((rows, LANES), F32)] * (SUBLANES - TOP_K)
    wts = jnp.concatenate(wcols, axis=1).reshape(rows, SUBLANES, LANES)

    chosen = jnp.zeros((rows, LANES), F32)
    for hot in hots:
        chosen = chosen + hot.astype(F32)
    before = jnp.dot(tri_ref[...], chosen.astype(BF16), preferred_element_type=F32) + running
    rank = jnp.zeros((rows, LANES), F32)
    for k in range(TOP_K):
        rk = jnp.sum(jnp.where(hots[k], before, 0.0), axis=-1, keepdims=True)
        rank = rank + jnp.where(lane == float(k), rk, 0.0)
    idx_out[:, rs] = idx_acc.T[:SUBLANES, :].astype(jnp.int32)
    wts_out[0, rs] = wts
    rank_out[:, rs] = rank.T[:SUBLANES, :].astype(jnp.int32)
    return running + jnp.sum(chosen, axis=0, keepdims=True)


def post_mixer(attn, other, x, mods, mod_off, n2, wa, wc, rw2, rb, tri, base):
    bm, sm, d = x.shape
    tr = POST_SUBTILES * tri.shape[0]
    assert sm % tr == 0
    row = lambda width: pl.BlockSpec((1, tr, width), lambda b, i: (b, i, 0))
    tile_rows = pl.BlockSpec((1, tr) + ROW_WORDS, lambda b, i: (b, i, 0, 0))
    nb = sm // tr
    token_minor = pl.BlockSpec((SUBLANES, tr), lambda b, i: (0, b * nb + i))
    out_shape = [jax.ShapeDtypeStruct((bm, sm, d), F32),
                 jax.ShapeDtypeStruct((bm, sm) + ROW_WORDS, jnp.int32),
                 jax.ShapeDtypeStruct((SUBLANES, bm * sm), jnp.int32),
                 jax.ShapeDtypeStruct((bm, sm, SUBLANES, LANES), F32),
                 jax.ShapeDtypeStruct((SUBLANES, bm * sm), jnp.int32),
                 jax.ShapeDtypeStruct((SUBLANES, LANES), F32)]
    return pl.pallas_call(
        _post_kernel,
        out_shape=out_shape, grid=(bm, sm // tr),
        in_specs=[row(attn.shape[-1]), row(other.shape[-1]), row(d),
                  pl.BlockSpec((1, SUBLANES, d), lambda b, i: (b + mod_off, 0, 0)),
                  _const_spec((1, d)), _const_spec(wa.shape), _const_spec(wc.shape),
                  _const_spec(rw2.shape), _const_spec((1, LANES)),
                  _const_spec(tri.shape), _const_spec((SUBLANES, LANES))],
        out_specs=[row(d), tile_rows, token_minor,
                   pl.BlockSpec((1, tr, SUBLANES, LANES), lambda b, i: (b, i, 0, 0)),
                   token_minor, _const_spec((SUBLANES, LANES))],
        scratch_shapes=[pltpu.VMEM((SUBLANES, LANES), F32)],
        compiler_params=_params(("arbitrary", "arbitrary"), VMEM_LIMIT),
        name="post_mixer_route",
    )(attn, other, x, mods, n2, wa, wc, rw2, rb, tri, base)


def _plan_kernel(off_ref, idx_ref, rank_ref, pos_out):
    idx = idx_ref[...]
    pos = rank_ref[...]
    for e in range(N_EXPERTS):
        pos = pos + jnp.where(idx == e, off_ref[e], 0)
    pos_out[...] = pos


def plan_positions(offsets, idx, rank):
    n = idx.shape[1]
    tr = min(2048, n)
    spec = pl.BlockSpec((SUBLANES, tr), lambda i, off: (0, i))
    return pl.pallas_call(
        _plan_kernel,
        out_shape=jax.ShapeDtypeStruct((SUBLANES, n), jnp.int32),
        grid_spec=pltpu.PrefetchScalarGridSpec(
            num_scalar_prefetch=1, grid=(n // tr,), in_specs=[spec, spec], out_specs=spec),
        compiler_params=_params(("parallel",)),
        name="plan_positions",
    )(offsets, idx, rank)


def _sc_worker_id():
    return lax.axis_index("s") * SC_CORES + lax.axis_index("c")


def _sc_chunk_rows(tile, dtype):
    row_bytes = math.prod(tile) * jnp.dtype(dtype).itemsize
    return min(SC_CHUNK_BYTES // row_bytes, SC_MAX_INDICES)


def dispatch_rows(xms, posk, n_rows):
    n_total = sum(x.shape[0] for x in xms)
    tile, dtype = xms[0].shape[1:], xms[0].dtype
    chunk = _sc_chunk_rows(tile, dtype)
    starts, s0 = [], 0
    for x in xms:
        assert x.shape[0] % (SC_WORKERS * chunk) == 0
        starts.append(s0)
        s0 += x.shape[0]
    mesh = plsc.VectorSubcoreMesh(core_axis_name="c", subcore_axis_name="s")

    @functools.partial(
        pl.kernel, mesh=mesh, out_type=jax.ShapeDtypeStruct((n_rows,) + tile, dtype),
        scratch_types=[pltpu.VMEM((chunk,), jnp.int32), pltpu.VMEM((chunk,) + tile, dtype),
                       pltpu.SemaphoreType.DMA],
        name="dispatch_rows_sc")
    def scatter(*refs):
        x_refs, pos_hbm, xs_hbm, idx_v, rows_v, sem = refs[:len(xms)], *refs[len(xms):]
        wid = _sc_worker_id()
        for x_hbm, start in zip(x_refs, starts):
            per_worker = x_hbm.shape[0] // SC_WORKERS

            @pl.loop(0, per_worker // chunk)
            def _(c):
                t0 = wid * per_worker + c * chunk
                pltpu.sync_copy(x_hbm.at[pl.ds(t0, chunk)], rows_v)
                for k in range(TOP_K):
                    pltpu.sync_copy(pos_hbm.at[pl.ds(k * n_total + start + t0, chunk)], idx_v)
                    pltpu.async_copy(rows_v, xs_hbm.at[idx_v], sem).wait()

    return scatter(*xms, posk)


def gather_rows(y, posk):
    n_pairs = posk.shape[0]
    tile, dtype = y.shape[1:], y.dtype
    chunk = _sc_chunk_rows(tile, dtype)
    per_worker = n_pairs // SC_WORKERS
    assert per_worker % chunk == 0
    mesh = plsc.VectorSubcoreMesh(core_axis_name="c", subcore_axis_name="s")

    @functools.partial(
        pl.kernel, mesh=mesh, out_type=jax.ShapeDtypeStruct((n_pairs,) + tile, dtype),
        scratch_types=[pltpu.VMEM((chunk,), jnp.int32), pltpu.VMEM((chunk,) + tile, dtype),
                       pltpu.SemaphoreType.DMA],
        name="gather_rows_sc")
    def gather(y_hbm, pos_hbm, out_hbm, idx_v, rows_v, sem):
        wid = _sc_worker_id()

        @pl.loop(0, per_worker // chunk)
        def _(c):
            base = wid * per_worker + c * chunk
            pltpu.sync_copy(pos_hbm.at[pl.ds(base, chunk)], idx_v)
            pltpu.async_copy(y_hbm.at[idx_v], rows_v, sem).wait()
            pltpu.sync_copy(rows_v, out_hbm.at[pl.ds(base, chunk)])

    return gather(y, posk)


def _ffn_kernel(te_ref, nu_ref, first_ref, slot_ref, nxt_ref, xs_ref, w1_hbm, b1_ref, w2_hbm, b2_ref, y_ref,
                w1f, w2f, w1b, w2b, sem):
    i = pl.program_id(0)

    def weight_copies(e, s):
        return (pltpu.make_async_copy(w1_hbm.at[e], w1f.at[s], sem.at[0, s]),
                pltpu.make_async_copy(w2_hbm.at[e], w2f.at[s], sem.at[1, s]))

    @pl.when(i < nu_ref[0])
    def _():
        s = slot_ref[i]

        @pl.when(first_ref[i] == 1)
        def _():
            @pl.when(i == 0)
            def _():
                for cp in weight_copies(te_ref[i], s):
                    cp.start()
            for cp in weight_copies(te_ref[i], s):
                cp.wait()

            @pl.when(nxt_ref[i] >= 0)
            def _():
                for cp in weight_copies(nxt_ref[i], 1 - s):
                    cp.start()
            for c in range(D_MODEL // LANES):
                w1b[c * LANES:(c + 1) * LANES, :] = w1f[s, c * LANES:(c + 1) * LANES, :].astype(BF16)
            for c in range(D_EXPERT // LANES):
                w2b[c * LANES:(c + 1) * LANES, :] = w2f[s, c * LANES:(c + 1) * LANES, :].astype(BF16)

        x = _unpack_rows(xs_ref[...])
        h = jnp.dot(x, w1b[...], preferred_element_type=F32) + b1_ref[0]
        g = jnp.minimum(h[:, :D_EXPERT], SWIGLU_LIMIT)
        lin = jnp.clip(h[:, D_EXPERT:], -SWIGLU_LIMIT, SWIGLU_LIMIT)
        a = (lin + 1.0) * (g * jax.nn.sigmoid(SWIGLU_ALPHA * g))
        y = jnp.dot(a.astype(BF16), w2b[...], preferred_element_type=F32) + b2_ref[0]
        y_ref[...] = _pack_rows(y.astype(BF16))

    @pl.when(i >= nu_ref[0])
    def _():
        y_ref[...] = jnp.zeros(y_ref.shape, y_ref.dtype)


def grouped_ffn(sched, xs, w1, b1, w2, b2):
    r = xs.shape[0]
    d = w1.shape[1]
    nt = r // FFN_TILE
    tile = (FFN_TILE,) + xs.shape[1:]
    rows = lambda i, te, nu, *_: (jnp.minimum(i, nu[0] - 1), 0, 0)
    bsel = lambda i, te, *_: (te[i], 0, 0)
    return pl.pallas_call(
        _ffn_kernel,
        out_shape=jax.ShapeDtypeStruct(xs.shape, xs.dtype),
        grid_spec=pltpu.PrefetchScalarGridSpec(
            num_scalar_prefetch=5, grid=(nt,),
            in_specs=[pl.BlockSpec(tile, rows),
                      pl.BlockSpec(memory_space=pl.ANY),
                      pl.BlockSpec((1, 1, 2 * D_EXPERT), bsel),
                      pl.BlockSpec(memory_space=pl.ANY),
                      pl.BlockSpec((1, 1, d), bsel)],
            out_specs=pl.BlockSpec(tile, lambda i, *_: (i, 0, 0)),
            scratch_shapes=[pltpu.VMEM((2, d, 2 * D_EXPERT), F32), pltpu.VMEM((2, D_EXPERT, d), F32),
                            pltpu.VMEM((d, 2 * D_EXPERT), BF16), pltpu.VMEM((D_EXPERT, d), BF16),
                            pltpu.SemaphoreType.DMA((2, 2))]),
        compiler_params=_params(("arbitrary",), VMEM_LIMIT),
        name="grouped_ffn",
    )(*sched, xs, w1, b1.reshape(N_EXPERTS, 1, -1), w2, b2.reshape(N_EXPERTS, 1, -1))


def _combine_kernel(x1_ref, wts_ref, m_ref, fn_ref, y0_ref, y1_ref, y2_ref, y3_ref, o_ref, *, final):
    w = wts_ref[0]
    rows = lambda ref: pltpu.bitcast(ref[...], BF16).astype(F32)
    acc = w[:, 0:1, :] * rows(y0_ref)
    for k, y_ref in ((1, y1_ref), (2, y2_ref), (3, y3_ref)):
        acc = acc + w[:, k:k + 1, :] * rows(y_ref)
    out = x1_ref[0] + m_ref[0][5:6] * acc.reshape(x1_ref.shape[1], D_MODEL)
    if final:
        out = _rms(out, fn_ref[...])
    o_ref[0] = out


def combine_rows(x1, wts, mods, mod_off, fn, yg, row_off, n_total, *, final):
    bm, sm, d = x1.shape
    tr = min(COMBINE_TILE, sm)
    nb = sm // tr
    row = lambda width: pl.BlockSpec((1, tr, width), lambda b, i: (b, i, 0))
    ysel = lambda k: pl.BlockSpec((tr,) + yg.shape[1:],
                                  lambda b, i: ((k * n_total + row_off) // tr + b * nb + i, 0, 0))
    return pl.pallas_call(
        functools.partial(_combine_kernel, final=final),
        out_shape=jax.ShapeDtypeStruct((bm, sm, d), F32),
        grid=(bm, nb),
        in_specs=[row(d), pl.BlockSpec((1, tr, SUBLANES, LANES), lambda b, i: (b, i, 0, 0)),
                  pl.BlockSpec((1, SUBLANES, d), lambda b, i: (b + mod_off, 0, 0)),
                  _const_spec((1, d))] + [ysel(k) for k in range(TOP_K)],
        out_specs=row(d),
        compiler_params=_params(("parallel", "parallel"), VMEM_LIMIT),
        name="combine_rows",
    )(x1, wts, mods, fn, yg, yg, yg, yg)


def _axial_angles(n_tokens, rot_dim):
    t = jnp.arange(n_tokens)
    rows = (t // GRID_W).astype(F32)
    cols = (t % GRID_W).astype(F32)
    n_freq = rot_dim // 4
    inv = ROPE_THETA ** (-jnp.arange(n_freq, dtype=F32) / n_freq)
    return jnp.concatenate([rows[:, None] * inv, cols[:, None] * inv], axis=-1)


def _swap_halves(w):
    half = w.shape[-1] // 2
    return jnp.concatenate([-w[..., half:], w[..., :half]], axis=-1)


def _prep_l0(w_in, q_norm, kv_norm, w_uq, w_uk, w_uv, w_out):
    d = w_in.shape[0]
    o_kr = MLA_Q_RANK + MLA_KV_RANK
    kr_cols = w_in[:, o_kr:o_kr + MLA_ROPE]
    win = jnp.concatenate(
        [w_in[:, :o_kr], kr_cols, _swap_halves(kr_cols), jnp.zeros((d, LANES - 2 * MLA_ROPE), F32),
         w_in[:, o_kr + MLA_ROPE:]], axis=1).astype(BF16)
    qk = MLA_NOPE + MLA_ROPE
    wuq3 = w_uq.reshape(MLA_Q_RANK, MLA_HEADS, qk)
    wuq = jnp.pad(wuq3, ((0, 0), (0, 0), (0, LANES - qk))).reshape(MLA_Q_RANK, -1).astype(BF16)
    wuk3 = w_uk.reshape(MLA_KV_RANK, MLA_HEADS, MLA_NOPE)
    wuk = jnp.pad(wuk3, ((0, 0), (0, 0), (0, LANES - MLA_NOPE))).reshape(MLA_KV_RANK, -1).astype(BF16)
    wuv3 = w_uv.reshape(MLA_KV_RANK, MLA_HEADS, MLA_V)
    wuv = jnp.pad(wuv3, ((0, 0), (0, 0), (0, LANES - MLA_V))).reshape(MLA_KV_RANK, -1).astype(BF16)
    eye = jnp.eye(MLA_ROPE, dtype=F32)
    e_head = jnp.concatenate([jnp.zeros((MLA_ROPE, MLA_NOPE), F32), eye,
                              jnp.zeros((MLA_ROPE, LANES - qk), F32)], axis=1)
    e = jnp.pad(jnp.tile(e_head, (1, MLA_HEADS)), ((0, LANES - MLA_ROPE), (0, 0))).astype(BF16)
    wa3 = w_out[:MLA_HEADS * MLA_V].reshape(MLA_HEADS, MLA_V, d)
    wa = jnp.pad(wa3, ((0, 0), (0, LANES - MLA_V), (0, 0))).reshape(MLA_HEADS * LANES, d).astype(BF16)
    wc = w_out[MLA_HEADS * MLA_V:].astype(BF16)
    return dict(win=win, qg=q_norm.reshape(1, -1), kvg=kv_norm.reshape(1, -1), wuq=wuq,
                wuk=wuk, e=e, wuv=wuv, wa=wa, wc=wc)


def _l0_tables(n):
    ang = _axial_angles(n, MLA_ROPE)
    cos, sin = jnp.cos(ang), jnp.sin(ang)
    one = jnp.ones((n, 1), F32)
    zero = jnp.zeros((n, 1), F32)
    rest = LANES - MLA_NOPE - MLA_ROPE
    cq = jnp.concatenate([jnp.tile(one, (1, MLA_NOPE)), cos, cos, jnp.tile(one, (1, rest))], axis=1)
    sq = jnp.concatenate([jnp.tile(zero, (1, MLA_NOPE)), sin, sin, jnp.tile(zero, (1, rest))], axis=1)
    ck = jnp.concatenate([cos, cos, jnp.tile(zero, (1, LANES - MLA_ROPE))], axis=1)
    sk = jnp.concatenate([sin, sin, jnp.tile(zero, (1, LANES - MLA_ROPE))], axis=1)
    return cq, sq, ck, sk


def _pad_heads(w, n_heads, dim):
    d = w.shape[0]
    return jnp.pad(w.reshape(d, n_heads, dim), ((0, 0), (0, 0), (0, LANES - dim))).reshape(d, n_heads * LANES)


def _prep_l1(w_in, q_norm, k_norm, ln_g, ln_b, w_s, b_s, w_out):
    d = w_in.shape[0]
    qd = GQA_HEADS * GQA_HEAD_DIM
    kd = GQA_KV_HEADS * GQA_HEAD_DIM
    wq, wk, wv = w_in[:, :qd], w_in[:, qd:qd + kd], w_in[:, qd + kd:qd + 2 * kd]
    rest = w_in[:, qd + 2 * kd:]
    win = jnp.concatenate([_pad_heads(wq, GQA_HEADS, GQA_HEAD_DIM), _pad_heads(wk, GQA_KV_HEADS, GQA_HEAD_DIM),
                           _pad_heads(wv, GQA_KV_HEADS, GQA_HEAD_DIM), wv, rest], axis=1).astype(BF16)
    padg = lambda g: jnp.pad(g, (0, LANES - GQA_HEAD_DIM)).reshape(1, LANES)
    wa3 = w_out[:qd].reshape(GQA_HEADS, GQA_HEAD_DIM, d)
    wa = jnp.pad(wa3, ((0, 0), (0, LANES - GQA_HEAD_DIM), (0, 0))).reshape(GQA_HEADS * LANES, d).astype(BF16)
    wc = w_out[qd:].astype(BF16)
    bs = jnp.repeat(b_s.T, LANES, axis=1)
    return dict(win=win, qg=padg(q_norm), kg=padg(k_norm), lng=ln_g.reshape(1, -1), lnb=ln_b.reshape(1, -1),
                ws=w_s.astype(BF16), bs=bs, wa=wa, wc=wc)


def _l1_tables(n):
    ang = _axial_angles(n, GQA_HEAD_DIM)
    cos, sin = jnp.cos(ang), jnp.sin(ang)
    pad = LANES - GQA_HEAD_DIM
    c = jnp.concatenate([cos, cos, jnp.ones((n, pad), F32)], axis=1)
    s = jnp.concatenate([sin, sin, jnp.zeros((n, pad), F32)], axis=1)
    return c, s


def _pad_lanes(x, width):
    return jnp.pad(x, [(0, 0)] * (x.ndim - 1) + [(0, width - x.shape[-1])])


def routed_ffn(groups, mods, n2, moe, tri, final_norm, *, final):
    router_w, router_b, w1, b1, w2, b2 = moe
    rw = _pad_lanes(router_w, LANES)
    rwh = rw.astype(BF16)
    rwl = (rw - rwh.astype(F32)).astype(BF16)
    rw2 = jnp.concatenate([rwh, rwl], axis=1)
    rb = jnp.concatenate([router_b, jnp.full((LANES - N_EXPERTS,), NEG_BIG, F32)]).reshape(1, LANES)
    base = jnp.zeros((SUBLANES, LANES), F32)
    routed = []
    for g in groups:
        x1, xm, idx, wts, rank, base = post_mixer(
            g["attn"], g["other"], g["x"], mods, g["mod_off"], n2, g["wa"], g["wc"], rw2, rb, tri, base)
        routed.append((x1, xm, idx, wts, rank))
    n_total = sum(r[0].shape[0] * r[0].shape[1] for r in routed)
    nt = n_total * TOP_K // FFN_TILE + N_EXPERTS

    counts = base[0, :N_EXPERTS].astype(jnp.int32)
    tiles = (counts + FFN_TILE - 1) // FFN_TILE
    tile_end = jnp.cumsum(tiles)
    offsets = (tile_end - tiles) * FFN_TILE
    n_used = tile_end[-1:].astype(jnp.int32)
    experts = jnp.arange(N_EXPERTS)
    busy = tiles > 0
    slot_e = (jnp.cumsum(busy) - 1) % 2
    later = jnp.where(busy[None, :] & (experts[None, :] > experts[:, None]), experts[None, :], N_EXPERTS)
    nxt_e = jnp.min(later, axis=1)
    nxt_e = jnp.where(nxt_e == N_EXPERTS, -1, nxt_e)
    tile_expert = jnp.sum(jnp.arange(nt)[:, None] >= tile_end[None, :], axis=1)
    tile_expert = jnp.minimum(tile_expert, jnp.max(jnp.where(busy, experts, 0)))
    first = jnp.concatenate([jnp.ones((1,), bool), tile_expert[1:] != tile_expert[:-1]])
    of_tile = tile_expert[:, None] == experts[None, :]
    slot_t = jnp.sum(jnp.where(of_tile, slot_e[None, :], 0), axis=1)
    nxt_t = jnp.sum(jnp.where(of_tile, nxt_e[None, :], 0), axis=1)
    sched = tuple(a.astype(jnp.int32) for a in (tile_expert, n_used, first, slot_t, nxt_t))

    positions, xms = [], []
    for (x1, xm, idx, wts, rank) in routed:
        positions.append(plan_positions(offsets, idx, rank)[:TOP_K])
        xms.append(xm.reshape((idx.shape[1],) + ROW_WORDS))
    posk = jnp.concatenate(positions, axis=1).reshape(-1)
    xs = dispatch_rows(xms, posk, nt * FFN_TILE)
    y = grouped_ffn(sched, xs, w1, b1, w2, b2)
    outs = []
    for g, (x1, xm, idx, wts, rank), pos in zip(groups, routed, positions):
        yg = gather_rows(y, pos.reshape(-1))
        outs.append(combine_rows(x1, wts, mods, g["mod_off"], final_norm, yg, 0, pos.shape[1], final=final))
    return outs


def kernel(x_prompt, x_sample, cache_l0_ckv, cache_l0_krope, cache_l1_k, cache_l1_v, c, c_ctx,
           l0_ada_w, l0_ada_b, l0_norm1, l0_w_in, l0_q_norm, l0_kv_norm, l0_w_uq, l0_w_uk, l0_w_uv,
           l0_conv_w, l0_conv_b, l0_conv_ln_g, l0_conv_ln_b, l0_w_out, l0_norm2,
           l0_router_w, l0_router_b, l0_w1, l0_b1, l0_w2, l0_b2,
           l1_ada_w, l1_ada_b, l1_norm1, l1_w_in, l1_q_norm, l1_k_norm, l1_gmlp_ln_g, l1_gmlp_ln_b,
           l1_w_s, l1_b_s, l1_w_out, l1_norm2,
           l1_router_w, l1_router_b, l1_w1, l1_b1, l1_w2, l1_b2,
           final_norm):
    bp, sp, d = x_prompt.shape
    bs, ss, _ = x_sample.shape
    past = cache_l0_ckv.shape[1]
    n_p = bp * sp

    cond8 = jnp.concatenate([c_ctx[None], c, jnp.zeros((SUBLANES - 1 - bs, d), F32)], axis=0)
    mods0 = adaln(cond8, l0_ada_w, l0_ada_b)
    mods1 = adaln(cond8, l1_ada_w, l1_ada_b)
    tri = jnp.tril(jnp.ones((ROW_TILE, ROW_TILE), F32), -1).astype(BF16)
    fn = final_norm.reshape(1, d)

    w0 = _prep_l0(l0_w_in, l0_q_norm, l0_kv_norm, l0_w_uq, l0_w_uk, l0_w_uv, l0_w_out)
    n1 = l0_norm1.reshape(1, d)
    hp = x_prompt.reshape(1, n_p, d)
    q_p, k_p, v_p, ckv_p, kr_p, u_p = l0_inproj(hp, mods0, 0, n1, w0, None)
    q_s, k_s, v_s, _, _, u_s = l0_inproj(x_sample, mods0, 1, n1, w0, _l0_tables(ss),
                                         cache=(cache_l0_ckv, _pad_lanes(cache_l0_krope, LANES)))
    hw = MLA_HEADS * LANES
    att_p = attention(q_p.reshape(bp, sp, hw), k_p.reshape(bp, sp, hw), v_p.reshape(bp, sp, hw),
                      n_heads=MLA_HEADS, n_kv=MLA_HEADS, heads_per_step=MLA_HEADS)
    att_s = attention(q_s, k_s, v_s,
                      n_heads=MLA_HEADS, n_kv=MLA_HEADS, heads_per_step=LATENT_HEADS_PER_STEP)
    conv_p = conformer_conv(u_p.reshape(bp, sp, CONV_CH), l0_conv_w, l0_conv_b, l0_conv_ln_g, l0_conv_ln_b)
    conv_s = conformer_conv(u_s, l0_conv_w, l0_conv_b, l0_conv_ln_g, l0_conv_ln_b)
    groups = [dict(attn=att_p.reshape(1, n_p, hw), other=conv_p.reshape(1, n_p, CONV_CH), x=hp, mod_off=0,
                   wa=w0["wa"], wc=w0["wc"]),
              dict(attn=att_s, other=conv_s, x=x_sample, mod_off=1, wa=w0["wa"], wc=w0["wc"])]
    hp, hs = routed_ffn(groups, mods0, l0_norm2.reshape(1, d),
                        (l0_router_w, l0_router_b, l0_w1, l0_b1, l0_w2, l0_b2), tri, fn, final=False)
    new_l0_ckv = ckv_p.reshape(bp, sp, MLA_KV_RANK)
    new_l0_krope = kr_p.reshape(bp, sp, MLA_ROPE)

    w1p = _prep_l1(l1_w_in, l1_q_norm, l1_k_norm, l1_gmlp_ln_g, l1_gmlp_ln_b, l1_w_s, l1_b_s, l1_w_out)
    n1 = l1_norm1.reshape(1, d)
    q_p, k_p, vp_p, gat_p, kt_p, vt_p = l1_inproj(hp, mods1, 0, n1, w1p, None, ctx_seq=sp)
    q_s, k_s, vp_s, gat_s = l1_inproj(hs, mods1, 1, n1, w1p, _l1_tables(ss))
    qw = GQA_HEADS * LANES
    kw = GQA_KV_HEADS * LANES
    pad_kv = lambda t: _pad_lanes(t, LANES).reshape(bs, past, kw).astype(BF16)
    att_p = attention(q_p.reshape(bp, sp, qw), k_p.reshape(bp, sp, kw), vp_p.reshape(bp, sp, kw),
                      n_heads=GQA_HEADS, n_kv=GQA_KV_HEADS, heads_per_step=GQA_HEADS)
    att_s = attention(q_s, jnp.concatenate([pad_kv(cache_l1_k), k_s], axis=1),
                      jnp.concatenate([_with_sum_lane(pad_kv(cache_l1_v)), vp_s], axis=1),
                      n_heads=GQA_HEADS, n_kv=GQA_KV_HEADS, heads_per_step=LATENT_HEADS_PER_STEP)
    groups = [dict(attn=att_p.reshape(1, n_p, qw), other=gat_p, x=hp, mod_off=0, wa=w1p["wa"], wc=w1p["wc"]),
              dict(attn=att_s, other=gat_s, x=hs, mod_off=1, wa=w1p["wa"], wc=w1p["wc"])]
    yp, ys = routed_ffn(groups, mods1, l1_norm2.reshape(1, d),
                        (l1_router_w, l1_router_b, l1_w1, l1_b1, l1_w2, l1_b2), tri, fn, final=True)
    new_l1_k = jnp.transpose(kt_p, (0, 3, 1, 2))
    new_l1_v = jnp.transpose(vt_p.reshape(bp, GQA_KV_HEADS, GQA_HEAD_DIM, sp), (0, 3, 1, 2))
    return (yp.reshape(bp, sp, d), ys, new_l0_ckv, new_l0_krope, new_l1_k, new_l1_v)
```

```python
import functools
import math

import jax
import jax.numpy as jnp
from jax import lax
from jax.experimental import pallas as pl
from jax.experimental.pallas import tpu as pltpu
from jax.experimental.pallas import tpu_sc as plsc

F32 = jnp.float32
BF16 = jnp.bfloat16
HIGHEST = lax.Precision.HIGHEST

LANES = 128
SUBLANES = 8
VMEM_LIMIT = 56 * 1024 * 1024

D_MODEL = 1024
GRID_W = 64
ROPE_THETA = 10000.0
EPS = 1e-6
N_MOD = 6

MLA_HEADS = 8
MLA_NOPE = 64
MLA_ROPE = 32
MLA_V = 64
MLA_Q_RANK = 384
MLA_KV_RANK = 256
MLA_SCALE = 1.0 / math.sqrt(MLA_NOPE + MLA_ROPE)
CONV_CH = 512
CONV_WIDTH = 31
CONV_HALO = 16

GQA_HEADS = 8
GQA_KV_HEADS = 2
GQA_HEAD_DIM = 64
GQA_SCALE = 1.0 / math.sqrt(GQA_HEAD_DIM)
CHUNK = 128
GMLP_GROUPS = 4
GMLP_CH = 512

N_EXPERTS = 32
TOP_K = 4
D_EXPERT = 1024
SWIGLU_LIMIT = 7.0
SWIGLU_ALPHA = 1.702

ROW_TILE = 512
POST_SUBTILES = 2
FFN_TILE = 256
SC_CORES = 2
SC_SUBCORES = 16
SC_WORKERS = SC_CORES * SC_SUBCORES
SC_CHUNK_BYTES = 256 * 1024
SC_MAX_INDICES = 128
COMBINE_TILE = 512
COMBINE_PARTS = 2
ATT_Q_TILE = 256
ATT_LONG_KEYS = 1024
V_SUM_LANE = 64
LATENT_HEADS_PER_STEP = 4
NEG_BIG = -1e30


def _params(sem, vmem=None):
    return pltpu.CompilerParams(dimension_semantics=sem, vmem_limit_bytes=vmem)


def _rms(x, g):
    return x * lax.rsqrt(jnp.mean(x * x, axis=-1, keepdims=True) + EPS) * g


ROW_WORDS = (SUBLANES // 2, LANES)


def _pack_rows(x_bf16):
    return pltpu.bitcast(x_bf16.reshape(x_bf16.shape[0], SUBLANES, LANES), jnp.int32)


def _unpack_rows(words):
    return pltpu.bitcast(words, BF16).reshape(words.shape[0], D_MODEL)


def _const_spec(shape):
    nd = len(shape)
    return pl.BlockSpec(shape, lambda *_: (0,) * nd)


def _adaln_kernel(c_ref, w_ref, b_ref, o_ref):
    c = c_ref[...]
    s = c * jax.nn.sigmoid(c)
    o_ref[...] = jnp.dot(s, w_ref[...], preferred_element_type=F32, precision=HIGHEST) + b_ref[...]


def adaln(cond8, ada_w, ada_b):
    d, n = ada_w.shape
    bn = n // 4
    m = pl.pallas_call(
        _adaln_kernel,
        out_shape=jax.ShapeDtypeStruct((SUBLANES, n), F32),
        grid=(n // bn,),
        in_specs=[_const_spec((SUBLANES, d)),
                  pl.BlockSpec((d, bn), lambda j: (0, j)),
                  pl.BlockSpec((1, bn), lambda j: (0, j))],
        out_specs=pl.BlockSpec((SUBLANES, bn), lambda j: (0, j)),
        compiler_params=_params(("arbitrary",), VMEM_LIMIT),
        name="adaln",
    )(cond8, ada_w, ada_b.reshape(1, n))
    m = m.reshape(SUBLANES, N_MOD, d)
    return jnp.pad(m, ((0, 0), (0, SUBLANES - N_MOD), (0, 0)))


def _l0_inproj_kernel(*refs, rope):
    if not rope:
        _l0_rows(*refs, rope=False)
        return
    wuk_ref, e_ref, wuv_ref, cckv_ref, ckr_ref = refs[7:12]
    row_refs = refs[:10] + refs[12:]
    k_out, v_out = row_refs[15], row_refs[16]
    i = pl.program_id(1)

    @pl.when(i == 0)
    def _():
        ckv_b = cckv_ref[0].astype(BF16)
        k = (jnp.dot(ckv_b, wuk_ref[...], preferred_element_type=F32)
             + jnp.dot(ckr_ref[0].astype(BF16), e_ref[...], preferred_element_type=F32))
        k_out[0] = k.astype(BF16)
        v_out[0] = _with_sum_lane(jnp.dot(ckv_b, wuv_ref[...], preferred_element_type=F32)).astype(BF16)

    @pl.when(i > 0)
    def _():
        _l0_rows(*row_refs, rope=True)


def _l0_rows(*refs, rope):
    if rope:
        (x_ref, m_ref, n1_ref, win_ref, qg_ref, kvg_ref, wuq_ref, wuk_ref, e_ref, wuv_ref,
         cq_ref, sq_ref, ck_ref, sk_ref, q_out, k_out, v_out, ckv_out, kr_out, u_out) = refs
    else:
        (x_ref, m_ref, n1_ref, win_ref, qg_ref, kvg_ref, wuq_ref, wuk_ref, e_ref, wuv_ref,
         q_out, k_out, v_out, ckv_out, kr_out, u_out) = refs
    x = x_ref[0]
    m = m_ref[0]
    h = _rms(x, n1_ref[...]) * (1.0 + m[1:2]) + m[0:1]
    z = jnp.dot(h.astype(BF16), win_ref[...], preferred_element_type=F32)
    c_q = z[:, 0:MLA_Q_RANK]
    c_kv = z[:, MLA_Q_RANK:MLA_Q_RANK + MLA_KV_RANK]
    o_kr = MLA_Q_RANK + MLA_KV_RANK
    o_conv = o_kr + LANES
    kr_blk = z[:, o_kr:o_conv]
    val = z[:, o_conv:o_conv + CONV_CH]
    gate = z[:, o_conv + CONV_CH:o_conv + 2 * CONV_CH]

    cqn = _rms(c_q, qg_ref[...]).astype(BF16)
    q = jnp.dot(cqn, wuq_ref[...], preferred_element_type=F32)
    if rope:
        cq = cq_ref[...]
        sq = sq_ref[...]
        half = MLA_ROPE // 2
        lane = lax.broadcasted_iota(jnp.int32, (1, LANES), 1)
        first_half = (lane >= MLA_NOPE) & (lane < MLA_NOPE + half)
        for hd in range(MLA_HEADS):
            sl = slice(hd * LANES, (hd + 1) * LANES)
            blk = q[:, sl]
            partner = jnp.where(first_half, -pltpu.roll(blk, LANES - half, 1), pltpu.roll(blk, half, 1))
            q_out[0, :, sl] = ((blk * cq + partner * sq) * MLA_SCALE).astype(BF16)
        kr = kr_blk * ck_ref[...] + pltpu.roll(kr_blk, LANES - MLA_ROPE, 1) * sk_ref[...]
    else:
        q_out[0] = (q * MLA_SCALE).astype(BF16)
        kr = kr_blk

    ckv = _rms(c_kv, kvg_ref[...])
    ckv_out[0] = ckv
    kr_out[0] = kr_blk[:, 0:MLA_ROPE]
    ckv_b = ckv.astype(BF16)
    k = (jnp.dot(ckv_b, wuk_ref[...], preferred_element_type=F32)
         + jnp.dot(kr.astype(BF16), e_ref[...], preferred_element_type=F32))
    k_out[0] = k.astype(BF16)
    v_out[0] = _with_sum_lane(jnp.dot(ckv_b, wuv_ref[...], preferred_element_type=F32)).astype(BF16)
    u_out[0] = val * jax.nn.sigmoid(gate)


def l0_inproj(x, mods, mod_off, n1, w, tables, cache=None):
    bm, sm, d = x.shape
    tr = min(ROW_TILE, sm)
    rope = tables is not None
    assert rope == (cache is not None)
    hp = MLA_HEADS * LANES
    lead = 1 if rope else 0
    past = cache[0].shape[1] if rope else 0
    assert past == lead * tr
    rows = lambda b, i: (b, jnp.maximum(i - lead, 0), 0)
    row = lambda width: pl.BlockSpec((1, tr, width), rows)
    in_specs = [row(d),
                pl.BlockSpec((1, SUBLANES, d), lambda b, i: (b + mod_off, 0, 0)),
                _const_spec((1, d)), _const_spec(w["win"].shape),
                _const_spec((1, MLA_Q_RANK)), _const_spec((1, MLA_KV_RANK)),
                _const_spec(w["wuq"].shape)]
    args = [x, mods, n1, w["win"], w["qg"], w["kvg"], w["wuq"]]
    in_specs +=[_const_spec(w["wuk"].shape), _const_spec(w["e"].shape), _const_spec(w["wuv"].shape)]
    args += [w["wuk"], w["e"], w["wuv"]]
    if rope:
        in_specs += [pl.BlockSpec((1, past, c.shape[-1]), lambda b, i: (b, 0, 0)) for c in cache]
        args += list(cache)
        in_specs += [pl.BlockSpec((tr, LANES), lambda b, i: (jnp.maximum(i - lead, 0), 0))] * 4
        args += list(tables)
    out_shape = [jax.ShapeDtypeStruct((bm, sm, hp), BF16),
                 jax.ShapeDtypeStruct((bm, past + sm, hp), BF16),
                 jax.ShapeDtypeStruct((bm, past + sm, hp), BF16),
                 jax.ShapeDtypeStruct((bm, sm, MLA_KV_RANK), F32),
                 jax.ShapeDtypeStruct((bm, sm, MLA_ROPE), F32),
                 jax.ShapeDtypeStruct((bm, sm, CONV_CH), F32)]
    kv_rows = pl.BlockSpec((1, tr, hp), lambda b, i: (b, i, 0))
    out_specs = [row(hp), kv_rows, kv_rows, row(MLA_KV_RANK), row(MLA_ROPE), row(CONV_CH)]
    return pl.pallas_call(
        functools.partial(_l0_inproj_kernel, rope=rope),
        out_shape=out_shape, grid=(bm, lead + sm // tr), in_specs=in_specs, out_specs=out_specs,
        compiler_params=_params(("parallel", "arbitrary"), VMEM_LIMIT),
        name="l0_inproj_rope" if rope else "l0_inproj",
    )(*args)


def _conv_kernel(prev_ref, cur_ref, next_ref, w_ref, b_ref, g_ref, beta_ref, o_ref, pad_ref, sh_ref, *, rb):
    i = pl.program_id(1)
    last = pl.num_programs(1) - 1
    zeros = jnp.zeros((CONV_HALO, CONV_CH), F32)
    pad_ref[0:CONV_HALO, :] = jnp.where(i == 0, zeros, prev_ref[0])
    pad_ref[CONV_HALO:CONV_HALO + rb, :] = cur_ref[0]
    pad_ref[CONV_HALO + rb:CONV_HALO + rb + CONV_HALO, :] = jnp.where(i == last, zeros, next_ref[0])
    span = rb + 2 * CONV_HALO - SUBLANES
    for r in range(1, SUBLANES):
        sh_ref[r] = pad_ref[r:r + span, :]
    w = w_ref[...]
    shift = CONV_HALO - CONV_WIDTH // 2
    acc = jnp.zeros((rb, CONV_CH), F32) + b_ref[...]
    for k in range(CONV_WIDTH):
        off = k + shift
        r, a = off % SUBLANES, off // SUBLANES * SUBLANES
        window = pad_ref[a:a + rb, :] if r == 0 else sh_ref[r, a:a + rb, :]
        acc = acc + window * w[k:k + 1, :]
    mu = jnp.mean(acc, axis=-1, keepdims=True)
    cen = acc - mu
    var = jnp.mean(cen * cen, axis=-1, keepdims=True)
    y = cen * lax.rsqrt(var + EPS) * g_ref[...] + beta_ref[...]
    o_ref[0] = (y * jax.nn.sigmoid(y)).astype(BF16)


def conformer_conv(u, conv_w, conv_b, ln_g, ln_b):
    b, s, c = u.shape
    rb = min(256, s)
    nh = rb // CONV_HALO
    n_halo_blocks = s // CONV_HALO
    wpad = jnp.pad(conv_w.reshape(CONV_WIDTH, c), ((0, 32 - CONV_WIDTH), (0, 0)))
    return pl.pallas_call(
        functools.partial(_conv_kernel, rb=rb),
        out_shape=jax.ShapeDtypeStruct((b, s, c), BF16),
        grid=(b, s // rb),
        in_specs=[pl.BlockSpec((1, CONV_HALO, c), lambda bi, i: (bi, jnp.maximum(i * nh - 1, 0), 0)),
                  pl.BlockSpec((1, rb, c), lambda bi, i: (bi, i, 0)),
                  pl.BlockSpec((1, CONV_HALO, c),
                               lambda bi, i: (bi, jnp.minimum((i + 1) * nh, n_halo_blocks - 1), 0)),
                  _const_spec((32, c)), _const_spec((1, c)), _const_spec((1, c)), _const_spec((1, c))],
        out_specs=pl.BlockSpec((1, rb, c), lambda bi, i: (bi, i, 0)),
        scratch_shapes=[pltpu.VMEM((rb + 2 * CONV_HALO, c), F32),
                        pltpu.VMEM((SUBLANES, rb + 2 * CONV_HALO - SUBLANES, c), F32)],
        compiler_params=_params(("parallel", "parallel"), VMEM_LIMIT),
        name="conformer_conv",
    )(u, u, u, wpad, conv_b.reshape(1, c), ln_g.reshape(1, c), ln_b.reshape(1, c))


def _with_sum_lane(v):
    lane = lax.broadcasted_iota(jnp.int32, (1, v.shape[-1]), 1)
    return v + ((lane & (LANES - 1)) == V_SUM_LANE).astype(v.dtype)


def _attn_kernel(q_ref, k_ref, v_ref, o_ref, *, heads, rep, mxu_denominator):
    for hd in range(heads):
        g = hd // rep
        q = q_ref[0, :, hd * LANES:(hd + 1) * LANES]
        k = k_ref[0, :, g * LANES:(g + 1) * LANES]
        s = lax.dot_general(q, k, (((1,), (1,)), ((), ())), preferred_element_type=F32)
        m = jnp.max(s, axis=-1, keepdims=True)
        if mxu_denominator:
            p = jnp.exp((s - m).astype(BF16))
            o = jnp.dot(p, v_ref[0, :, g * LANES:(g + 1) * LANES], preferred_element_type=F32)
            l = o[:, V_SUM_LANE:V_SUM_LANE + 1]
        else:
            p = jnp.exp(s - m)
            l = jnp.sum(p, axis=-1, keepdims=True)
            o = jnp.dot(p.astype(BF16), v_ref[0, :, g * LANES:(g + 1) * LANES], preferred_element_type=F32)
        o_ref[0, :, hd * LANES:(hd + 1) * LANES] = (o / l).astype(BF16)


def attention(q, k, v, *, n_heads, n_kv, heads_per_step):
    b, sq, _ = q.shape
    sk = k.shape[1]
    rep = n_heads // n_kv
    tq = min(ATT_Q_TILE, sq)
    hb = heads_per_step
    grid = (b, n_heads // hb, sq // tq)
    if hb >= rep:
        kv_spec = pl.BlockSpec((1, sk, hb // rep * LANES), lambda bi, h, i: (bi, 0, h))
        kern_rep = rep
    else:
        assert rep % hb == 0
        kv_spec = pl.BlockSpec((1, sk, LANES), lambda bi, h, i: (bi, 0, h * hb // rep))
        kern_rep = hb
    kern = functools.partial(_attn_kernel, heads=hb, rep=kern_rep, mxu_denominator=sk >= ATT_LONG_KEYS)
    q_spec = pl.BlockSpec((1, tq, heads_per_step * LANES), lambda bi, h, i: (bi, i, h))
    return pl.pallas_call(
        kern,
        out_shape=jax.ShapeDtypeStruct(q.shape, BF16),
        grid=grid, in_specs=[q_spec, kv_spec, kv_spec], out_specs=q_spec,
        compiler_params=_params(("parallel", "parallel", "parallel"), VMEM_LIMIT),
        name="attention",
    )(q, k, v)


def _l1_inproj_kernel(*refs, rope):
    if rope:
        (x_ref, m_ref, n1_ref, win_ref, qg_ref, kg_ref, lng_ref, lnb_ref,
         ws_ref, bs_ref, c_ref, s_ref, q_out, k_out, vp_out, g_out) = refs
        kt_out = vt_out = None
    else:
        (x_ref, m_ref, n1_ref, win_ref, qg_ref, kg_ref, lng_ref, lnb_ref,
         ws_ref, bs_ref, q_out, k_out, vp_out, g_out, kt_out, vt_out) = refs
    x = x_ref[0]
    m = m_ref[0]
    hb = (_rms(x, n1_ref[...]) * (1.0 + m[1:2]) + m[0:1]).astype(BF16)
    z = jnp.dot(hb, win_ref[...], preferred_element_type=F32)
    qw = GQA_HEADS * LANES
    kw = GQA_KV_HEADS * LANES
    o_k, o_vp, o_v, o_u, o_vg = qw, qw + kw, qw + 2 * kw, qw + 2 * kw + LANES, qw + 2 * kw + LANES + GMLP_CH
    if rope:
        cos = c_ref[...]
        sin = s_ref[...]
        half = GQA_HEAD_DIM // 2
        first_half = lax.broadcasted_iota(jnp.int32, (1, LANES), 1) < half

    def head(col, g_ref):
        t = z[:, col:col + LANES]
        r = lax.rsqrt(jnp.sum(t * t, axis=-1, keepdims=True) * (1.0 / GQA_HEAD_DIM) + EPS)
        normed = t * r * g_ref[...]
        if not rope:
            return normed, normed
        partner = jnp.where(first_half, -pltpu.roll(normed, LANES - half, 1), pltpu.roll(normed, half, 1))
        return normed, normed * cos + partner * sin

    for hd in range(GQA_HEADS):
        _, rot = head(hd * LANES, qg_ref)
        q_out[0, :, hd * LANES:(hd + 1) * LANES] = (rot * GQA_SCALE).astype(BF16)
    for hd in range(GQA_KV_HEADS):
        normed, rot = head(o_k + hd * LANES, kg_ref)
        k_out[0, :, hd * LANES:(hd + 1) * LANES] = rot.astype(BF16)
        if kt_out is not None:
            seq = kt_out.shape[-1]
            for s in range(kt_out.shape[0]):
                kt_out[s, hd] = normed[s * seq:(s + 1) * seq, :].T[:GQA_HEAD_DIM, :]
    vp_out[0] = _with_sum_lane(z[:, o_vp:o_vp + kw]).astype(BF16)
    if vt_out is not None:
        seq = vt_out.shape[-1]
        for s in range(vt_out.shape[0]):
            vt_out[s] = z[s * seq:(s + 1) * seq, o_v:o_v + LANES].T

    u = z[:, o_u:o_u + GMLP_CH]
    vg = z[:, o_vg:o_vg + GMLP_CH]
    mu = jnp.mean(vg, axis=-1, keepdims=True)
    cen = vg - mu
    var = jnp.mean(cen * cen, axis=-1, keepdims=True)
    vn = (cen * lax.rsqrt(var + EPS) * lng_ref[...] + lnb_ref[...]).astype(BF16)
    bias = bs_ref[...]
    rows = x.shape[0]
    for cidx in range(rows // CHUNK):
        r0 = cidx * CHUNK
        for g in range(GMLP_GROUPS):
            c0 = g * LANES
            mixed = jnp.dot(ws_ref[g], vn[r0:r0 + CHUNK, c0:c0 + LANES], preferred_element_type=F32)
            g_out[0, r0:r0 + CHUNK, c0:c0 + LANES] = (
                u[r0:r0 + CHUNK, c0:c0 + LANES] * (mixed + bias[:, c0:c0 + LANES])).astype(BF16)


def l1_inproj(x, mods, mod_off, n1, w, tables, ctx_seq=None):
    bm, sm, d = x.shape
    tr = min(ROW_TILE, sm)
    rope = tables is not None
    assert rope != (ctx_seq is not None)
    qw = GQA_HEADS * LANES
    kw = GQA_KV_HEADS * LANES
    row = lambda width: pl.BlockSpec((1, tr, width), lambda b, i: (b, i, 0))
    vec = _const_spec((1, LANES))
    in_specs = [row(d), pl.BlockSpec((1, SUBLANES, d), lambda b, i: (b + mod_off, 0, 0)),
                _const_spec((1, d)), _const_spec(w["win"].shape)]
    args = [x, mods, n1, w["win"]]
    in_specs += [vec, vec]
    args += [w["qg"], w["kg"]]
    in_specs += [_const_spec((1, GMLP_CH)), _const_spec((1, GMLP_CH)),
                 _const_spec(w["ws"].shape), _const_spec((CHUNK, GMLP_CH))]
    args += [w["lng"], w["lnb"], w["ws"], w["bs"]]
    if rope:
        in_specs += [pl.BlockSpec((tr, LANES), lambda b, i: (i, 0))] * 2
        args += list(tables)
    out_shape = [jax.ShapeDtypeStruct((bm, sm, qw), BF16),
                 jax.ShapeDtypeStruct((bm, sm, kw), BF16),
                 jax.ShapeDtypeStruct((bm, sm, kw), BF16),
                 jax.ShapeDtypeStruct((bm, sm, GMLP_CH), BF16)]
    out_specs = [row(qw), row(kw), row(kw), row(GMLP_CH)]
    if not rope:
        assert bm == 1 and tr % ctx_seq == 0
        n_seq, per_step = sm // ctx_seq, tr // ctx_seq
        out_shape += [jax.ShapeDtypeStruct((n_seq, GQA_KV_HEADS, GQA_HEAD_DIM, ctx_seq), F32),
                      jax.ShapeDtypeStruct((n_seq, GQA_KV_HEADS * GQA_HEAD_DIM, ctx_seq), F32)]
        out_specs += [pl.BlockSpec((per_step, GQA_KV_HEADS, GQA_HEAD_DIM, ctx_seq), lambda b, i: (i, 0, 0, 0)),
                      pl.BlockSpec((per_step, GQA_KV_HEADS * GQA_HEAD_DIM, ctx_seq), lambda b, i: (i, 0, 0))]
    return pl.pallas_call(
        functools.partial(_l1_inproj_kernel, rope=rope),
        out_shape=out_shape, grid=(bm, sm // tr), in_specs=in_specs, out_specs=out_specs,
        compiler_params=_params(("parallel", "parallel"), VMEM_LIMIT),
        name="l1_inproj_rope" if rope else "l1_inproj",
    )(*args)


def _post_kernel(a_ref, c_ref, x_ref, m_ref, n2_ref, wa_ref, wc_ref, rw2_ref, rb_ref, tri_ref, base_ref,
                 x1_out, xm_out, idx_out, wts_out, rank_out, cnt_out, run_ref):
    first =(pl.program_id(0) == 0) & (pl.program_id(1) == 0)

    @pl.when(first)
    def _():
        run_ref[...] = base_ref[...]

    sub = tri_ref.shape[0]
    running = run_ref[0:1, :]
    for r0 in range(0, a_ref.shape[1], sub):
        running = _route_rows(slice(r0, r0 + sub), running, a_ref, c_ref, x_ref, m_ref, n2_ref, wa_ref, wc_ref,
                              rw2_ref, rb_ref, tri_ref, x1_out, xm_out, idx_out, wts_out, rank_out)
    run_ref[0:1, :] = running
    cnt_out[...] = run_ref[...]


def _route_rows(rs, running, a_ref, c_ref, x_ref, m_ref, n2_ref, wa_ref, wc_ref, rw2_ref, rb_ref, tri_ref,
                x1_out, xm_out, idx_out, wts_out, rank_out):
    m = m_ref[0]
    y = (jnp.dot(a_ref[0, rs, :], wa_ref[...], preferred_element_type=F32)
         + jnp.dot(c_ref[0, rs, :], wc_ref[...], preferred_element_type=F32))
    x1 = x_ref[0, rs, :] + m[2:3] * y
    x1_out[0, rs, :] = x1
    xm = _rms(x1, n2_ref[...]) * (1.0 + m[4:5]) + m[3:4]
    xh = xm.astype(BF16)
    xm_out[0, rs] = _pack_rows(xh)

    xl = (xm - xh.astype(F32)).astype(BF16)
    both = jnp.dot(xh, rw2_ref[...], preferred_element_type=F32)
    logits = (both[:, :LANES] + both[:, LANES:]
              + jnp.dot(xl, rw2_ref[:, :LANES], preferred_element_type=F32)) + rb_ref[...]
    rows = logits.shape[0]
    lane = lax.broadcasted_iota(jnp.int32, (rows, LANES), 1).astype(F32)
    work = logits
    vals, hots = [], []
    idx_acc = jnp.zeros((rows, LANES), F32)
    for k in range(TOP_K):
        top = jnp.max(work, axis=-1, keepdims=True)
        sel = jnp.min(jnp.where(work == top, lane, float(LANES)), axis=-1, keepdims=True)
        hot = lane == sel
        vals.append(top)
        hots.append(hot)
        idx_acc = idx_acc + jnp.where(lane == float(k), sel, 0.0)
        work = jnp.where(hot, -jnp.inf, work)
    exps = [jnp.exp(v - vals[0]) for v in vals]
    denom = exps[0] + exps[1] + exps[2] + exps[3]
    wcols = [jnp.broadcast_to(exps[k] / denom, (rows, LANES)) for k in range(TOP_K)]
    wcols += [jnp.zeros((rows, LANES), F32)] * (SUBLANES - TOP_K)
    wts = jnp.concatenate(wcols, axis=1).reshape(rows, SUBLANES, LANES)

    chosen = jnp.zeros((rows, LANES), F32)
    for hot in hots:
        chosen = chosen + hot.astype(F32)
    before = jnp.dot(tri_ref[...], chosen.astype(BF16), preferred_element_type=F32) + running
    rank = jnp.zeros((rows, LANES), F32)
    for k in range(TOP_K):
        rk = jnp.sum(jnp.where(hots[k], before, 0.0), axis=-1, keepdims=True)
        rank = rank + jnp.where(lane == float(k), rk, 0.0)
    idx_out[:, rs] = idx_acc.T[:SUBLANES, :].astype(jnp.int32)
    wts_out[0, rs] = wts
    rank_out[:, rs] = rank.T[:SUBLANES, :].astype(jnp.int32)
    return running + jnp.sum(chosen, axis=0, keepdims=True)


def post_mixer(attn, other, x, mods, mod_off, n2, wa, wc, rw2, rb, tri, base):
    bm, sm, d = x.shape
    tr = POST_SUBTILES * tri.shape[0]
    assert sm % tr == 0
    row = lambda width: pl.BlockSpec((1, tr, width), lambda b, i: (b, i, 0))
    tile_rows = pl.BlockSpec((1, tr) + ROW_WORDS, lambda b, i: (b, i, 0, 0))
    nb = sm // tr
    token_minor = pl.BlockSpec((SUBLANES, tr), lambda b, i: (0, b * nb + i))
    out_shape = [jax.ShapeDtypeStruct((bm, sm, d), F32),
                 jax.ShapeDtypeStruct((bm, sm) + ROW_WORDS, jnp.int32),
                 jax.ShapeDtypeStruct((SUBLANES, bm * sm), jnp.int32),
                 jax.ShapeDtypeStruct((bm, sm, SUBLANES, LANES), F32),
                 jax.ShapeDtypeStruct((SUBLANES, bm * sm), jnp.int32),
                 jax.ShapeDtypeStruct((SUBLANES, LANES), F32)]
    return pl.pallas_call(
        _post_kernel,
        out_shape=out_shape, grid=(bm, sm // tr),
        in_specs=[row(attn.shape[-1]), row(other.shape[-1]), row(d),
                  pl.BlockSpec((1, SUBLANES, d), lambda b, i: (b + mod_off, 0, 0)),
                  _const_spec((1, d)), _const_spec(wa.shape), _const_spec(wc.shape),
                  _const_spec(rw2.shape), _const_spec((1, LANES)),
                  _const_spec(tri.shape), _const_spec((SUBLANES, LANES))],
        out_specs=[row(d), tile_rows, token_minor,
                   pl.BlockSpec((1, tr, SUBLANES, LANES), lambda b, i: (b, i, 0, 0)),
                   token_minor, _const_spec((SUBLANES, LANES))],
        scratch_shapes=[pltpu.VMEM((SUBLANES, LANES), F32)],
        compiler_params=_params(("arbitrary", "arbitrary"), VMEM_LIMIT),
        name="post_mixer_route",
    )(attn, other, x, mods, n2, wa, wc, rw2, rb, tri, base)


def _plan_kernel(off_ref, idx_ref, rank_ref, pos_out):
    idx = idx_ref[...]
    pos = rank_ref[...]
    for e in range(N_EXPERTS):
        pos = pos + jnp.where(idx == e, off_ref[e], 0)
    pos_out[...] = pos


def plan_positions(offsets, idx, rank):
    n = idx.shape[1]
    tr = min(2048, n)
    spec = pl.BlockSpec((SUBLANES, tr), lambda i, off: (0, i))
    return pl.pallas_call(
        _plan_kernel,
        out_shape=jax.ShapeDtypeStruct((SUBLANES, n), jnp.int32),
        grid_spec=pltpu.PrefetchScalarGridSpec(
            num_scalar_prefetch=1, grid=(n // tr,), in_specs=[spec, spec], out_specs=spec),
        compiler_params=_params(("parallel",)),
        name="plan_positions",
    )(offsets, idx, rank)


def _sc_worker_id():
    return lax.axis_index("s") * SC_CORES + lax.axis_index("c")


def _sc_chunk_rows(tile, dtype):
    row_bytes = math.prod(tile) * jnp.dtype(dtype).itemsize
    return min(SC_CHUNK_BYTES // row_bytes, SC_MAX_INDICES)


def dispatch_rows(xms, posk, n_rows):
    n_total = sum(x.shape[0] for x in xms)
    tile, dtype = xms[0].shape[1:], xms[0].dtype
    chunk = _sc_chunk_rows(tile, dtype)
    starts, s0 = [], 0
    for x in xms:
        assert x.shape[0] % (SC_WORKERS * chunk) == 0
        starts.append(s0)
        s0 += x.shape[0]
    mesh = plsc.VectorSubcoreMesh(core_axis_name="c", subcore_axis_name="s")

    @functools.partial(
        pl.kernel, mesh=mesh, out_type=jax.ShapeDtypeStruct((n_rows,) + tile, dtype),
        scratch_types=[pltpu.VMEM((chunk,), jnp.int32), pltpu.VMEM((chunk,) + tile, dtype),
                       pltpu.SemaphoreType.DMA],
        name="dispatch_rows_sc")
    def scatter(*refs):
        x_refs, pos_hbm, xs_hbm, idx_v, rows_v, sem = refs[:len(xms)], *refs[len(xms):]
        wid = _sc_worker_id()
        for x_hbm, start in zip(x_refs, starts):
            per_worker = x_hbm.shape[0] // SC_WORKERS

            @pl.loop(0, per_worker // chunk)
            def _(c):
                t0 = wid * per_worker + c * chunk
                pltpu.sync_copy(x_hbm.at[pl.ds(t0, chunk)], rows_v)
                for k in range(TOP_K):
                    pltpu.sync_copy(pos_hbm.at[pl.ds(k * n_total + start + t0, chunk)], idx_v)
                    pltpu.async_copy(rows_v, xs_hbm.at[idx_v], sem).wait()

    return scatter(*xms, posk)


def gather_rows(y, posk):
    n_pairs = posk.shape[0]
    tile, dtype = y.shape[1:], y.dtype
    chunk = _sc_chunk_rows(tile, dtype)
    per_worker = n_pairs // SC_WORKERS
    assert per_worker % chunk == 0
    mesh = plsc.VectorSubcoreMesh(core_axis_name="c", subcore_axis_name="s")

    @functools.partial(
        pl.kernel, mesh=mesh, out_type=jax.ShapeDtypeStruct((n_pairs,) + tile, dtype),
        scratch_types=[pltpu.VMEM((chunk,), jnp.int32), pltpu.VMEM((chunk,) + tile, dtype),
                       pltpu.SemaphoreType.DMA],
        name="gather_rows_sc")
    def gather(y_hbm, pos_hbm, out_hbm, idx_v, rows_v, sem):
        wid = _sc_worker_id()

        @pl.loop(0, per_worker // chunk)
        def _(c):
            base = wid * per_worker + c * chunk
            pltpu.sync_copy(pos_hbm.at[pl.ds(base, chunk)], idx_v)
            pltpu.async_copy(y_hbm.at[idx_v], rows_v, sem).wait()
            pltpu.sync_copy(rows_v, out_hbm.at[pl.ds(base, chunk)])

    return gather(y, posk)


def _ffn_kernel(te_ref, nu_ref, first_ref, slot_ref, nxt_ref, xs_ref, w1_hbm, b1_ref, w2_hbm, b2_ref, y_ref,
                w1f, w2f, w1b, w2b, sem):
    i = pl.program_id(0)

    def weight_copies(e, s):
        return (pltpu.make_async_copy(w1_hbm.at[e], w1f.at[s], sem.at[0, s]),
                pltpu.make_async_copy(w2_hbm.at[e], w2f.at[s], sem.at[1, s]))

    @pl.when(i < nu_ref[0])
    def _():
        s = slot_ref[i]

        @pl.when(first_ref[i] == 1)
        def _():
            @pl.when(i == 0)
            def _():
                for cp in weight_copies(te_ref[i], s):
                    cp.start()
            for cp in weight_copies(te_ref[i], s):
                cp.wait()

            @pl.when(nxt_ref[i] >= 0)
            def _():
                for cp in weight_copies(nxt_ref[i], 1 - s):
                    cp.start()
            for c in range(D_MODEL // LANES):
                w1b[c * LANES:(c + 1) * LANES, :] = w1f[s, c * LANES:(c + 1) * LANES, :].astype(BF16)
            for c in range(D_EXPERT // LANES):
                w2b[c * LANES:(c + 1) * LANES, :] = w2f[s, c * LANES:(c + 1) * LANES, :].astype(BF16)

        x = _unpack_rows(xs_ref[...])
        h = jnp.dot(x, w1b[...], preferred_element_type=F32) + b1_ref[0]
        g = jnp.minimum(h[:, :D_EXPERT], SWIGLU_LIMIT)
        lin = jnp.clip(h[:, D_EXPERT:], -SWIGLU_LIMIT, SWIGLU_LIMIT)
        a = (lin + 1.0) * (g * jax.nn.sigmoid(SWIGLU_ALPHA * g))
        y = jnp.dot(a.astype(BF16), w2b[...], preferred_element_type=F32) + b2_ref[0]
        y_ref[...] = _pack_rows(y.astype(BF16))

    @pl.when(i >= nu_ref[0])
    def _():
        y_ref[...] = jnp.zeros(y_ref.shape, y_ref.dtype)


def grouped_ffn(sched, xs, w1, b1, w2, b2):
    r = xs.shape[0]
    d = w1.shape[1]
    nt = r // FFN_TILE
    tile = (FFN_TILE,) + xs.shape[1:]
    rows = lambda i, te, nu, *_: (jnp.minimum(i, nu[0] - 1), 0, 0)
    bsel = lambda i, te, *_: (te[i], 0, 0)
    return pl.pallas_call(
        _ffn_kernel,
        out_shape=jax.ShapeDtypeStruct(xs.shape, xs.dtype),
        grid_spec=pltpu.PrefetchScalarGridSpec(
            num_scalar_prefetch=5, grid=(nt,),
            in_specs=[pl.BlockSpec(tile, rows),
                      pl.BlockSpec(memory_space=pl.ANY),
                      pl.BlockSpec((1, 1, 2 * D_EXPERT), bsel),
                      pl.BlockSpec(memory_space=pl.ANY),
                      pl.BlockSpec((1, 1, d), bsel)],
            out_specs=pl.BlockSpec(tile, lambda i, *_: (i, 0, 0)),
            scratch_shapes=[pltpu.VMEM((2, d, 2 * D_EXPERT), F32), pltpu.VMEM((2, D_EXPERT, d), F32),
                            pltpu.VMEM((d, 2 * D_EXPERT), BF16), pltpu.VMEM((D_EXPERT, d), BF16),
                            pltpu.SemaphoreType.DMA((2, 2))]),
        compiler_params=_params(("arbitrary",), VMEM_LIMIT),
        name="grouped_ffn",
    )(*sched, xs, w1, b1.reshape(N_EXPERTS, 1, -1), w2, b2.reshape(N_EXPERTS, 1, -1))


def _combine_kernel(x1_ref, wts_ref, m_ref, fn_ref, y0_ref, y1_ref, y2_ref, y3_ref, *rest, final):
    o_ref = rest[-1]
    w = wts_ref[0]
    rows = lambda ref: pltpu.bitcast(ref[...], BF16).astype(F32)
    acc = w[:, 0:1, :] * rows(y0_ref)
    for k, y_ref in ((1, y1_ref), (2, y2_ref), (3, y3_ref)):
        acc = acc + w[:, k:k + 1, :] * rows(y_ref)
    out = x1_ref[0] + m_ref[0][5:6] * acc.reshape(x1_ref.shape[1], D_MODEL)
    if final:
        out = _rms(out, fn_ref[...])
    o_ref[0] = out


def combine_rows(x1, wts, mods, mod_off, fn, yg, part, prev, *, final):
    bm, sm, d = x1.shape
    tr = min(COMBINE_TILE, sm)
    blocks = bm * (sm // tr)
    per_part = blocks // COMBINE_PARTS
    nb = sm // tr
    n_part = per_part * tr
    where = lambda j: ((part * per_part + j) // nb, (part * per_part + j) % nb)
    row = lambda width: pl.BlockSpec((1, tr, width), lambda j: (*where(j), 0))
    ysel = lambda k: pl.BlockSpec((tr,) + yg.shape[1:], lambda j: (k * n_part // tr + j, 0, 0))
    in_specs = [row(d), pl.BlockSpec((1, tr, SUBLANES, LANES), lambda j: (*where(j), 0, 0)),
                pl.BlockSpec((1, SUBLANES, d), lambda j: (where(j)[0] + mod_off, 0, 0)),
                _const_spec((1, d))] + [ysel(k) for k in range(TOP_K)]
    args = [x1, wts, mods, fn, yg, yg, yg, yg]
    aliases = {}
    if prev is not None:
        in_specs.append(pl.BlockSpec(memory_space=pl.ANY))
        args.append(prev)
        aliases = {len(args) - 1: 0}
    return pl.pallas_call(
        functools.partial(_combine_kernel, final=final),
        out_shape=jax.ShapeDtypeStruct((bm, sm, d), F32),
        grid=(per_part,),
        in_specs=in_specs,
        out_specs=row(d),
        input_output_aliases=aliases,
        compiler_params=_params(("parallel",), VMEM_LIMIT),
        name="combine_rows",
    )(*args)


def _axial_angles(n_tokens, rot_dim):
    t = jnp.arange(n_tokens)
    rows = (t // GRID_W).astype(F32)
    cols = (t % GRID_W).astype(F32)
    n_freq = rot_dim // 4
    inv = ROPE_THETA ** (-jnp.arange(n_freq, dtype=F32) / n_freq)
    return jnp.concatenate([rows[:, None] * inv, cols[:, None] * inv], axis=-1)


def _swap_halves(w):
    half = w.shape[-1] // 2
    return jnp.concatenate([-w[..., half:], w[..., :half]], axis=-1)


def _prep_l0(w_in, q_norm, kv_norm, w_uq, w_uk, w_uv, w_out):
    d = w_in.shape[0]
    o_kr = MLA_Q_RANK + MLA_KV_RANK
    kr_cols = w_in[:, o_kr:o_kr + MLA_ROPE]
    win = jnp.concatenate(
        [w_in[:, :o_kr], kr_cols, _swap_halves(kr_cols), jnp.zeros((d, LANES - 2 * MLA_ROPE), F32),
         w_in[:, o_kr + MLA_ROPE:]], axis=1).astype(BF16)
    qk = MLA_NOPE + MLA_ROPE
    wuq3 = w_uq.reshape(MLA_Q_RANK, MLA_HEADS, qk)
    wuq = jnp.pad(wuq3, ((0, 0), (0, 0), (0, LANES - qk))).reshape(MLA_Q_RANK, -1).astype(BF16)
    wuk3 = w_uk.reshape(MLA_KV_RANK, MLA_HEADS, MLA_NOPE)
    wuk = jnp.pad(wuk3, ((0, 0), (0, 0), (0, LANES - MLA_NOPE))).reshape(MLA_KV_RANK, -1).astype(BF16)
    wuv3 = w_uv.reshape(MLA_KV_RANK, MLA_HEADS, MLA_V)
    wuv = jnp.pad(wuv3, ((0, 0), (0, 0), (0, LANES - MLA_V))).reshape(MLA_KV_RANK, -1).astype(BF16)
    eye = jnp.eye(MLA_ROPE, dtype=F32)
    e_head = jnp.concatenate([jnp.zeros((MLA_ROPE, MLA_NOPE), F32), eye,
                              jnp.zeros((MLA_ROPE, LANES - qk), F32)], axis=1)
    e = jnp.pad(jnp.tile(e_head, (1, MLA_HEADS)), ((0, LANES - MLA_ROPE), (0, 0))).astype(BF16)
    wa3 = w_out[:MLA_HEADS * MLA_V].reshape(MLA_HEADS, MLA_V, d)
    wa = jnp.pad(wa3, ((0, 0), (0, LANES - MLA_V), (0, 0))).reshape(MLA_HEADS * LANES, d).astype(BF16)
    wc = w_out[MLA_HEADS * MLA_V:].astype(BF16)
    return dict(win=win, qg=q_norm.reshape(1, -1), kvg=kv_norm.reshape(1, -1), wuq=wuq,
                wuk=wuk, e=e, wuv=wuv, wa=wa, wc=wc)


def _l0_tables(n):
    ang = _axial_angles(n, MLA_ROPE)
    cos, sin = jnp.cos(ang), jnp.sin(ang)
    one = jnp.ones((n, 1), F32)
    zero = jnp.zeros((n, 1), F32)
    rest = LANES - MLA_NOPE - MLA_ROPE
    cq = jnp.concatenate([jnp.tile(one, (1, MLA_NOPE)), cos, cos, jnp.tile(one, (1, rest))], axis=1)
    sq = jnp.concatenate([jnp.tile(zero, (1, MLA_NOPE)), sin, sin, jnp.tile(zero, (1, rest))], axis=1)
    ck = jnp.concatenate([cos, cos, jnp.tile(zero, (1, LANES - MLA_ROPE))], axis=1)
    sk = jnp.concatenate([sin, sin, jnp.tile(zero, (1, LANES - MLA_ROPE))], axis=1)
    return cq, sq, ck, sk


def _pad_heads(w, n_heads, dim):
    d = w.shape[0]
    return jnp.pad(w.reshape(d, n_heads, dim), ((0, 0), (0, 0), (0, LANES - dim))).reshape(d, n_heads * LANES)


def _prep_l1(w_in, q_norm, k_norm, ln_g, ln_b, w_s, b_s, w_out):
    d = w_in.shape[0]
    qd = GQA_HEADS * GQA_HEAD_DIM
    kd = GQA_KV_HEADS * GQA_HEAD_DIM
    wq, wk, wv = w_in[:, :qd], w_in[:, qd:qd + kd], w_in[:, qd + kd:qd + 2 * kd]
    rest = w_in[:, qd + 2 * kd:]
    win = jnp.concatenate([_pad_heads(wq, GQA_HEADS, GQA_HEAD_DIM), _pad_heads(wk, GQA_KV_HEADS, GQA_HEAD_DIM),
                           _pad_heads(wv, GQA_KV_HEADS, GQA_HEAD_DIM), wv, rest], axis=1).astype(BF16)
    padg = lambda g: jnp.pad(g, (0, LANES - GQA_HEAD_DIM)).reshape(1, LANES)
    wa3 = w_out[:qd].reshape(GQA_HEADS, GQA_HEAD_DIM, d)
    wa = jnp.pad(wa3, ((0, 0), (0, LANES - GQA_HEAD_DIM), (0, 0))).reshape(GQA_HEADS * LANES, d).astype(BF16)
    wc = w_out[qd:].astype(BF16)
    bs = jnp.repeat(b_s.T, LANES, axis=1)
    return dict(win=win, qg=padg(q_norm), kg=padg(k_norm), lng=ln_g.reshape(1, -1), lnb=ln_b.reshape(1, -1),
                ws=w_s.astype(BF16), bs=bs, wa=wa, wc=wc)


def _l1_tables(n):
    ang = _axial_angles(n, GQA_HEAD_DIM)
    cos, sin = jnp.cos(ang), jnp.sin(ang)
    pad = LANES - GQA_HEAD_DIM
    c = jnp.concatenate([cos, cos, jnp.ones((n, pad), F32)], axis=1)
    s = jnp.concatenate([sin, sin, jnp.zeros((n, pad), F32)], axis=1)
    return c, s


def _pad_lanes(x, width):
    return jnp.pad(x, [(0, 0)] * (x.ndim - 1) + [(0, width - x.shape[-1])])


def routed_ffn(groups, mods, n2, moe, tri, final_norm, *, final):
    router_w, router_b, w1, b1, w2, b2 = moe
    rw = _pad_lanes(router_w, LANES)
    rwh = rw.astype(BF16)
    rwl = (rw - rwh.astype(F32)).astype(BF16)
    rw2 = jnp.concatenate([rwh, rwl], axis=1)
    rb = jnp.concatenate([router_b, jnp.full((LANES - N_EXPERTS,), NEG_BIG, F32)]).reshape(1, LANES)
    base = jnp.zeros((SUBLANES, LANES), F32)
    routed = []
    for g in groups:
        x1, xm, idx, wts, rank, base = post_mixer(
            g["attn"], g["other"], g["x"], mods, g["mod_off"], n2, g["wa"], g["wc"], rw2, rb, tri, base)
        routed.append((x1, xm, idx, wts, rank))
    n_total = sum(r[0].shape[0] * r[0].shape[1] for r in routed)
    nt = n_total * TOP_K // FFN_TILE + N_EXPERTS

    counts = base[0, :N_EXPERTS].astype(jnp.int32)
    tiles = (counts + FFN_TILE - 1) // FFN_TILE
    tile_end = jnp.cumsum(tiles)
    offsets = (tile_end - tiles) * FFN_TILE
    n_used = tile_end[-1:].astype(jnp.int32)
    experts = jnp.arange(N_EXPERTS)
    busy = tiles > 0
    slot_e = (jnp.cumsum(busy) - 1) % 2
    later = jnp.where(busy[None, :] & (experts[None, :] > experts[:, None]), experts[None, :], N_EXPERTS)
    nxt_e = jnp.min(later, axis=1)
    nxt_e = jnp.where(nxt_e == N_EXPERTS, -1, nxt_e)
    tile_expert = jnp.sum(jnp.arange(nt)[:, None] >= tile_end[None, :], axis=1)
    tile_expert = jnp.minimum(tile_expert, jnp.max(jnp.where(busy, experts, 0)))
    first = jnp.concatenate([jnp.ones((1,), bool), tile_expert[1:] != tile_expert[:-1]])
    of_tile = tile_expert[:, None] == experts[None, :]
    slot_t = jnp.sum(jnp.where(of_tile, slot_e[None, :], 0), axis=1)
    nxt_t = jnp.sum(jnp.where(of_tile, nxt_e[None, :], 0), axis=1)
    sched = tuple(a.astype(jnp.int32) for a in (tile_expert, n_used, first, slot_t, nxt_t))

    positions, xms = [], []
    for (x1, xm, idx, wts, rank) in routed:
        positions.append(plan_positions(offsets, idx, rank)[:TOP_K])
        xms.append(xm.reshape((idx.shape[1],) + ROW_WORDS))
    posk = jnp.concatenate(positions, axis=1).reshape(-1)
    xs = dispatch_rows(xms, posk, nt * FFN_TILE)
    y = grouped_ffn(sched, xs, w1, b1, w2, b2)
    outs = []
    for g, (x1, xm, idx, wts, rank), pos in zip(groups, routed, positions):
        n_part = pos.shape[1] // COMBINE_PARTS
        out = None
        for part in range(COMBINE_PARTS):
            yg = gather_rows(y, pos[:, part * n_part:(part + 1) * n_part].reshape(-1))
            out = combine_rows(x1, wts, mods, g["mod_off"], final_norm, yg, part, out, final=final)
        outs.append(out)
    return outs


def kernel(x_prompt, x_sample, cache_l0_ckv, cache_l0_krope, cache_l1_k, cache_l1_v, c, c_ctx,
           l0_ada_w, l0_ada_b, l0_norm1, l0_w_in, l0_q_norm, l0_kv_norm, l0_w_uq, l0_w_uk, l0_w_uv,
           l0_conv_w, l0_conv_b, l0_conv_ln_g, l0_conv_ln_b, l0_w_out, l0_norm2,
           l0_router_w, l0_router_b, l0_w1, l0_b1, l0_w2, l0_b2,
           l1_ada_w, l1_ada_b, l1_norm1, l1_w_in, l1_q_norm, l1_k_norm, l1_gmlp_ln_g, l1_gmlp_ln_b,
           l1_w_s, l1_b_s, l1_w_out, l1_norm2,
           l1_router_w, l1_router_b, l1_w1, l1_b1, l1_w2, l1_b2,
           final_norm):
    bp, sp, d = x_prompt.shape
    bs, ss, _ = x_sample.shape
    past = cache_l0_ckv.shape[1]
    n_p = bp * sp

    cond8 = jnp.concatenate([c_ctx[None], c, jnp.zeros((SUBLANES - 1 - bs, d), F32)], axis=0)
    mods0 = adaln(cond8, l0_ada_w, l0_ada_b)
    mods1 = adaln(cond8, l1_ada_w, l1_ada_b)
    tri = jnp.tril(jnp.ones((ROW_TILE, ROW_TILE), F32), -1).astype(BF16)
    fn = final_norm.reshape(1, d)

    w0 = _prep_l0(l0_w_in, l0_q_norm, l0_kv_norm, l0_w_uq, l0_w_uk, l0_w_uv, l0_w_out)
    n1 = l0_norm1.reshape(1, d)
    hp = x_prompt.reshape(1, n_p, d)
    q_p, k_p, v_p, ckv_p, kr_p, u_p = l0_inproj(hp, mods0, 0, n1, w0, None)
    q_s, k_s, v_s, _, _, u_s = l0_inproj(x_sample, mods0, 1, n1, w0, _l0_tables(ss),
                                         cache=(cache_l0_ckv, _pad_lanes(cache_l0_krope, LANES)))
    hw = MLA_HEADS * LANES
    att_p = attention(q_p.reshape(bp, sp, hw), k_p.reshape(bp, sp, hw), v_p.reshape(bp, sp, hw),
                      n_heads=MLA_HEADS, n_kv=MLA_HEADS, heads_per_step=MLA_HEADS)
    att_s = attention(q_s, k_s, v_s,
                      n_heads=MLA_HEADS, n_kv=MLA_HEADS, heads_per_step=LATENT_HEADS_PER_STEP)
    conv_p = conformer_conv(u_p.reshape(bp, sp, CONV_CH), l0_conv_w, l0_conv_b, l0_conv_ln_g, l0_conv_ln_b)
    conv_s = conformer_conv(u_s, l0_conv_w, l0_conv_b, l0_conv_ln_g, l0_conv_ln_b)
    groups = [dict(attn=att_p.reshape(1, n_p, hw), other=conv_p.reshape(1, n_p, CONV_CH), x=hp, mod_off=0,
                   wa=w0["wa"], wc=w0["wc"]),
              dict(attn=att_s, other=conv_s, x=x_sample, mod_off=1, wa=w0["wa"], wc=w0["wc"])]
    hp, hs = routed_ffn(groups, mods0, l0_norm2.reshape(1, d),
                        (l0_router_w, l0_router_b, l0_w1, l0_b1, l0_w2, l0_b2), tri, fn, final=False)
    new_l0_ckv = ckv_p.reshape(bp, sp, MLA_KV_RANK)
    new_l0_krope = kr_p.reshape(bp, sp, MLA_ROPE)

    w1p = _prep_l1(l1_w_in, l1_q_norm, l1_k_norm, l1_gmlp_ln_g, l1_gmlp_ln_b, l1_w_s, l1_b_s, l1_w_out)
    n1 = l1_norm1.reshape(1, d)
    q_p, k_p, vp_p, gat_p, kt_p, vt_p = l1_inproj(hp, mods1, 0, n1, w1p, None, ctx_seq=sp)
    q_s, k_s, vp_s, gat_s = l1_inproj(hs, mods1, 1, n1, w1p, _l1_tables(ss))
    qw = GQA_HEADS * LANES
    kw = GQA_KV_HEADS * LANES
    pad_kv = lambda t: _pad_lanes(t, LANES).reshape(bs, past, kw).astype(BF16)
    att_p = attention(q_p.reshape(bp, sp, qw), k_p.reshape(bp, sp, kw), vp_p.reshape(bp, sp, kw),
                      n_heads=GQA_HEADS, n_kv=GQA_KV_HEADS, heads_per_step=GQA_HEADS)
    att_s = attention(q_s, jnp.concatenate([pad_kv(cache_l1_k), k_s], axis=1),
                      jnp.concatenate([_with_sum_lane(pad_kv(cache_l1_v)), vp_s], axis=1),
                      n_heads=GQA_HEADS, n_kv=GQA_KV_HEADS, heads_per_step=LATENT_HEADS_PER_STEP)
    groups = [dict(attn=att_p.reshape(1, n_p, qw), other=gat_p, x=hp, mod_off=0, wa=w1p["wa"], wc=w1p["wc"]),
              dict(attn=att_s, other=gat_s, x=hs, mod_off=1, wa=w1p["wa"], wc=w1p["wc"])]
    yp, ys = routed_ffn(groups, mods1, l1_norm2.reshape(1, d),
                        (l1_router_w, l1_router_b, l1_w1, l1_b1, l1_w2, l1_b2), tri, fn, final=True)
    new_l1_k = jnp.transpose(kt_p, (0, 3, 1, 2))
    new_l1_v = jnp.transpose(vt_p.reshape(bp, GQA_KV_HEADS, GQA_HEAD_DIM, sp), (0, 3, 1, 2))
    return (yp.reshape(bp, sp, d), ys, new_l0_ckv, new_l0_krope, new_l1_k, new_l1_v)
```

```python
import functools
import math

import jax
import jax.numpy as jnp
from jax import lax
from jax.experimental import pallas as pl
from jax.experimental.pallas import tpu as pltpu
from jax.experimental.pallas import tpu_sc as plsc

F32 = jnp.float32
BF16 = jnp.bfloat16
HIGHEST = lax.Precision.HIGHEST

LANES = 128
SUBLANES = 8
VMEM_LIMIT = 56 * 1024 * 1024

D_MODEL = 1024
GRID_W = 64
ROPE_THETA = 10000.0
EPS = 1e-6
N_MOD = 6

MLA_HEADS = 8
MLA_NOPE = 64
MLA_ROPE = 32
MLA_V = 64
MLA_Q_RANK = 384
MLA_KV_RANK = 256
MLA_SCALE = 1.0 / math.sqrt(MLA_NOPE + MLA_ROPE)
CONV_CH = 512
CONV_WIDTH = 31
CONV_HALO = 16

GQA_HEADS = 8
GQA_KV_HEADS = 2
GQA_HEAD_DIM = 64
GQA_SCALE = 1.0 / math.sqrt(GQA_HEAD_DIM)
CHUNK = 128
GMLP_GROUPS = 4
GMLP_CH = 512

N_EXPERTS = 32
TOP_K = 4
D_EXPERT = 1024
SWIGLU_LIMIT = 7.0
SWIGLU_ALPHA = 1.702

ROW_TILE = 512
POST_SUBTILES = 2
FFN_TILE = 256
SC_CORES = 2
SC_SUBCORES = 16
SC_WORKERS = SC_CORES * SC_SUBCORES
SC_CHUNK_BYTES = 256 * 1024
SC_MAX_INDICES = 128
COMBINE_TILE = 512
ATT_Q_TILE = 256
ATT_LONG_KEYS = 1024
V_SUM_LANE = 64
LATENT_HEADS_PER_STEP = 4
NEG_BIG = -1e30


def _params(sem, vmem=None):
    return pltpu.CompilerParams(dimension_semantics=sem, vmem_limit_bytes=vmem)


def _rms(x, g):
    return x * lax.rsqrt(jnp.mean(x * x, axis=-1, keepdims=True) + EPS) * g


ROW_WORDS = (SUBLANES // 2, LANES)


def _pack_rows(x_bf16):
    return pltpu.bitcast(x_bf16.reshape(x_bf16.shape[0], SUBLANES, LANES), jnp.int32)


def _unpack_rows(words):
    return pltpu.bitcast(words, BF16).reshape(words.shape[0], D_MODEL)


def _const_spec(shape):
    nd = len(shape)
    return pl.BlockSpec(shape, lambda *_: (0,) * nd)


def _adaln_kernel(c_ref, w_ref, b_ref, o_ref):
    c = c_ref[...]
    s = c * jax.nn.sigmoid(c)
    o_ref[...] = jnp.dot(s, w_ref[...], preferred_element_type=F32, precision=HIGHEST) + b_ref[...]


def adaln(cond8, ada_w, ada_b):
    d, n = ada_w.shape
    bn = n // 4
    m = pl.pallas_call(
        _adaln_kernel,
        out_shape=jax.ShapeDtypeStruct((SUBLANES, n), F32),
        grid=(n // bn,),
        in_specs=[_const_spec((SUBLANES, d)),
                  pl.BlockSpec((d, bn), lambda j: (0, j)),
                  pl.BlockSpec((1, bn), lambda j: (0, j))],
        out_specs=pl.BlockSpec((SUBLANES, bn), lambda j: (0, j)),
        compiler_params=_params(("arbitrary",), VMEM_LIMIT),
        name="adaln",
    )(cond8, ada_w, ada_b.reshape(1, n))
    m = m.reshape(SUBLANES, N_MOD, d)
    return jnp.pad(m, ((0, 0), (0, SUBLANES - N_MOD), (0, 0)))


def _l0_inproj_kernel(*refs, rope):
    if not rope:
        _l0_rows(*refs, rope=False)
        return
    wuk_ref, e_ref, wuv_ref, cckv_ref, ckr_ref = refs[7:12]
    row_refs = refs[:10] + refs[12:]
    k_out, v_out = row_refs[15], row_refs[16]
    i = pl.program_id(1)

    @pl.when(i == 0)
    def _():
        ckv_b = cckv_ref[0].astype(BF16)
        k = (jnp.dot(ckv_b, wuk_ref[...], preferred_element_type=F32)
             + jnp.dot(ckr_ref[0].astype(BF16), e_ref[...], preferred_element_type=F32))
        k_out[0] = k.astype(BF16)
        v_out[0] = _with_sum_lane(jnp.dot(ckv_b, wuv_ref[...], preferred_element_type=F32)).astype(BF16)

    @pl.when(i > 0)
    def _():
        _l0_rows(*row_refs, rope=True)


def _l0_rows(*refs, rope):
    if rope:
        (x_ref, m_ref, n1_ref, win_ref, qg_ref, kvg_ref, wuq_ref, wuk_ref, e_ref, wuv_ref,
         cq_ref, sq_ref, ck_ref, sk_ref, q_out, k_out, v_out, ckv_out, kr_out, u_out) = refs
    else:
        (x_ref, m_ref, n1_ref, win_ref, qg_ref, kvg_ref, wuq_ref, wuk_ref, e_ref, wuv_ref,
         q_out, k_out, v_out, ckv_out, kr_out, u_out) = refs
    x = x_ref[0]
    m = m_ref[0]
    h = _rms(x, n1_ref[...]) * (1.0 + m[1:2]) + m[0:1]
    z = jnp.dot(h.astype(BF16), win_ref[...], preferred_element_type=F32)
    c_q = z[:, 0:MLA_Q_RANK]
    c_kv = z[:, MLA_Q_RANK:MLA_Q_RANK + MLA_KV_RANK]
    o_kr = MLA_Q_RANK + MLA_KV_RANK
    o_conv = o_kr + LANES
    kr_blk = z[:, o_kr:o_conv]
    val = z[:, o_conv:o_conv + CONV_CH]
    gate = z[:, o_conv + CONV_CH:o_conv + 2 * CONV_CH]

    cqn = _rms(c_q, qg_ref[...]).astype(BF16)
    q = jnp.dot(cqn, wuq_ref[...], preferred_element_type=F32)
    if rope:
        cq = cq_ref[...]
        sq = sq_ref[...]
        half = MLA_ROPE // 2
        lane = lax.broadcasted_iota(jnp.int32, (1, LANES), 1)
        first_half = (lane >= MLA_NOPE) & (lane < MLA_NOPE + half)
        for hd in range(MLA_HEADS):
            sl = slice(hd * LANES, (hd + 1) * LANES)
            blk = q[:, sl]
            partner = jnp.where(first_half, -pltpu.roll(blk, LANES - half, 1), pltpu.roll(blk, half, 1))
            q_out[0, :, sl] = ((blk * cq + partner * sq) * MLA_SCALE).astype(BF16)
        kr = kr_blk * ck_ref[...] + pltpu.roll(kr_blk, LANES - MLA_ROPE, 1) * sk_ref[...]
    else:
        q_out[0] = (q * MLA_SCALE).astype(BF16)
        kr = kr_blk

    ckv = _rms(c_kv, kvg_ref[...])
    ckv_out[0] = ckv
    kr_out[0] = kr_blk[:, 0:MLA_ROPE]
    ckv_b = ckv.astype(BF16)
    k = (jnp.dot(ckv_b, wuk_ref[...], preferred_element_type=F32)
         + jnp.dot(kr.astype(BF16), e_ref[...], preferred_element_type=F32))
    k_out[0] = k.astype(BF16)
    v_out[0] = _with_sum_lane(jnp.dot(ckv_b, wuv_ref[...], preferred_element_type=F32)).astype(BF16)
    u_out[0] = val * jax.nn.sigmoid(gate)


def l0_inproj(x, mods, mod_off, n1, w, tables, cache=None):
    bm, sm, d = x.shape
    tr = min(ROW_TILE, sm)
    rope = tables is not None
    assert rope == (cache is not None)
    hp = MLA_HEADS * LANES
    lead = 1 if rope else 0
    past = cache[0].shape[1] if rope else 0
    assert past == lead * tr
    rows = lambda b, i: (b, jnp.maximum(i - lead, 0), 0)
    row = lambda width: pl.BlockSpec((1, tr, width), rows)
    in_specs = [row(d),
                pl.BlockSpec((1, SUBLANES, d), lambda b, i: (b + mod_off, 0, 0)),
                _const_spec((1, d)), _const_spec(w["win"].shape),
                _const_spec((1, MLA_Q_RANK)), _const_spec((1, MLA_KV_RANK)),
                _const_spec(w["wuq"].shape)]
    args = [x, mods, n1, w["win"], w["qg"], w["kvg"], w["wuq"]]
    in_specs +=[_const_spec(w["wuk"].shape), _const_spec(w["e"].shape), _const_spec(w["wuv"].shape)]
    args += [w["wuk"], w["e"], w["wuv"]]
    if rope:
        in_specs += [pl.BlockSpec((1, past, c.shape[-1]), lambda b, i: (b, 0, 0)) for c in cache]
        args += list(cache)
        in_specs += [pl.BlockSpec((tr, LANES), lambda b, i: (jnp.maximum(i - lead, 0), 0))] * 4
        args += list(tables)
    out_shape = [jax.ShapeDtypeStruct((bm, sm, hp), BF16),
                 jax.ShapeDtypeStruct((bm, past + sm, hp), BF16),
                 jax.ShapeDtypeStruct((bm, past + sm, hp), BF16),
                 jax.ShapeDtypeStruct((bm, sm, MLA_KV_RANK), F32),
                 jax.ShapeDtypeStruct((bm, sm, MLA_ROPE), F32),
                 jax.ShapeDtypeStruct((bm, sm, CONV_CH), F32)]
    kv_rows = pl.BlockSpec((1, tr, hp), lambda b, i: (b, i, 0))
    out_specs = [row(hp), kv_rows, kv_rows, row(MLA_KV_RANK), row(MLA_ROPE), row(CONV_CH)]
    return pl.pallas_call(
        functools.partial(_l0_inproj_kernel, rope=rope),
        out_shape=out_shape, grid=(bm, lead + sm // tr), in_specs=in_specs, out_specs=out_specs,
        compiler_params=_params(("parallel", "arbitrary"), VMEM_LIMIT),
        name="l0_inproj_rope" if rope else "l0_inproj",
    )(*args)


def _conv_kernel(prev_ref, cur_ref, next_ref, w_ref, b_ref, g_ref, beta_ref, o_ref, pad_ref, sh_ref, *, rb):
    i = pl.program_id(1)
    last = pl.num_programs(1) - 1
    zeros = jnp.zeros((CONV_HALO, CONV_CH), F32)
    pad_ref[0:CONV_HALO, :] = jnp.where(i == 0, zeros, prev_ref[0])
    pad_ref[CONV_HALO:CONV_HALO + rb, :] = cur_ref[0]
    pad_ref[CONV_HALO + rb:CONV_HALO + rb + CONV_HALO, :] = jnp.where(i == last, zeros, next_ref[0])
    span = rb + 2 * CONV_HALO - SUBLANES
    for r in range(1, SUBLANES):
        sh_ref[r] = pad_ref[r:r + span, :]
    w = w_ref[...]
    shift = CONV_HALO - CONV_WIDTH // 2
    acc = jnp.zeros((rb, CONV_CH), F32) + b_ref[...]
    for k in range(CONV_WIDTH):
        off = k + shift
        r, a = off % SUBLANES, off // SUBLANES * SUBLANES
        window = pad_ref[a:a + rb, :] if r == 0 else sh_ref[r, a:a + rb, :]
        acc = acc + window * w[k:k + 1, :]
    mu = jnp.mean(acc, axis=-1, keepdims=True)
    cen = acc - mu
    var = jnp.mean(cen * cen, axis=-1, keepdims=True)
    y = cen * lax.rsqrt(var + EPS) * g_ref[...] + beta_ref[...]
    o_ref[0] = (y * jax.nn.sigmoid(y)).astype(BF16)


def conformer_conv(u, conv_w, conv_b, ln_g, ln_b):
    b, s, c = u.shape
    rb = min(256, s)
    nh = rb // CONV_HALO
    n_halo_blocks = s // CONV_HALO
    wpad = jnp.pad(conv_w.reshape(CONV_WIDTH, c), ((0, 32 - CONV_WIDTH), (0, 0)))
    return pl.pallas_call(
        functools.partial(_conv_kernel, rb=rb),
        out_shape=jax.ShapeDtypeStruct((b, s, c), BF16),
        grid=(b, s // rb),
        in_specs=[pl.BlockSpec((1, CONV_HALO, c), lambda bi, i: (bi, jnp.maximum(i * nh - 1, 0), 0)),
                  pl.BlockSpec((1, rb, c), lambda bi, i: (bi, i, 0)),
                  pl.BlockSpec((1, CONV_HALO, c),
                               lambda bi, i: (bi, jnp.minimum((i + 1) * nh, n_halo_blocks - 1), 0)),
                  _const_spec((32, c)), _const_spec((1, c)), _const_spec((1, c)), _const_spec((1, c))],
        out_specs=pl.BlockSpec((1, rb, c), lambda bi, i: (bi, i, 0)),
        scratch_shapes=[pltpu.VMEM((rb + 2 * CONV_HALO, c), F32),
                        pltpu.VMEM((SUBLANES, rb + 2 * CONV_HALO - SUBLANES, c), F32)],
        compiler_params=_params(("parallel", "parallel"), VMEM_LIMIT),
        name="conformer_conv",
    )(u, u, u, wpad, conv_b.reshape(1, c), ln_g.reshape(1, c), ln_b.reshape(1, c))


def _with_sum_lane(v):
    lane = lax.broadcasted_iota(jnp.int32, (1, v.shape[-1]), 1)
    return v + ((lane & (LANES - 1)) == V_SUM_LANE).astype(v.dtype)


def _attn_kernel(q_ref, k_ref, v_ref, o_ref, *, heads, rep, mxu_denominator):
    for hd in range(heads):
        g = hd // rep
        q = q_ref[0, :, hd * LANES:(hd + 1) * LANES]
        k = k_ref[0, :, g * LANES:(g + 1) * LANES]
        s = lax.dot_general(q, k, (((1,), (1,)), ((), ())), preferred_element_type=F32)
        m = jnp.max(s, axis=-1, keepdims=True)
        if mxu_denominator:
            p = jnp.exp((s - m).astype(BF16))
            o = jnp.dot(p, v_ref[0, :, g * LANES:(g + 1) * LANES], preferred_element_type=F32)
            l = o[:, V_SUM_LANE:V_SUM_LANE + 1]
        else:
            p = jnp.exp(s - m)
            l = jnp.sum(p, axis=-1, keepdims=True)
            o = jnp.dot(p.astype(BF16), v_ref[0, :, g * LANES:(g + 1) * LANES], preferred_element_type=F32)
        o_ref[0, :, hd * LANES:(hd + 1) * LANES] = (o / l).astype(BF16)


def attention(q, k, v, *, n_heads, n_kv, heads_per_step):
    b, sq, _ = q.shape
    sk = k.shape[1]
    rep = n_heads // n_kv
    tq = min(ATT_Q_TILE, sq)
    hb = heads_per_step
    grid = (b, n_heads // hb, sq // tq)
    if hb >= rep:
        kv_spec = pl.BlockSpec((1, sk, hb // rep * LANES), lambda bi, h, i: (bi, 0, h))
        kern_rep = rep
    else:
        assert rep % hb == 0
        kv_spec = pl.BlockSpec((1, sk, LANES), lambda bi, h, i: (bi, 0, h * hb // rep))
        kern_rep = hb
    kern = functools.partial(_attn_kernel, heads=hb, rep=kern_rep, mxu_denominator=sk >= ATT_LONG_KEYS)
    q_spec = pl.BlockSpec((1, tq, heads_per_step * LANES), lambda bi, h, i: (bi, i, h))
    return pl.pallas_call(
        kern,
        out_shape=jax.ShapeDtypeStruct(q.shape, BF16),
        grid=grid, in_specs=[q_spec, kv_spec, kv_spec], out_specs=q_spec,
        compiler_params=_params(("parallel", "parallel", "parallel"), VMEM_LIMIT),
        name="attention",
    )(q, k, v)


def _l1_inproj_kernel(*refs, rope):
    if rope:
        (x_ref, m_ref, n1_ref, win_ref, qg_ref, kg_ref, lng_ref, lnb_ref,
         ws_ref, bs_ref, c_ref, s_ref, q_out, k_out, vp_out, g_out) = refs
        kt_out = vt_out = None
    else:
        (x_ref, m_ref, n1_ref, win_ref, qg_ref, kg_ref, lng_ref, lnb_ref,
         ws_ref, bs_ref, q_out, k_out, vp_out, g_out, kt_out, vt_out) = refs
    x = x_ref[0]
    m = m_ref[0]
    hb = (_rms(x, n1_ref[...]) * (1.0 + m[1:2]) + m[0:1]).astype(BF16)
    z = jnp.dot(hb, win_ref[...], preferred_element_type=F32)
    qw = GQA_HEADS * LANES
    kw = GQA_KV_HEADS * LANES
    o_k, o_vp, o_v, o_u, o_vg = qw, qw + kw, qw + 2 * kw, qw + 2 * kw + LANES, qw + 2 * kw + LANES + GMLP_CH
    if rope:
        cos = c_ref[...]
        sin = s_ref[...]
        half = GQA_HEAD_DIM // 2
        first_half = lax.broadcasted_iota(jnp.int32, (1, LANES), 1) < half

    def head(col, g_ref):
        t = z[:, col:col + LANES]
        r = lax.rsqrt(jnp.sum(t * t, axis=-1, keepdims=True) * (1.0 / GQA_HEAD_DIM) + EPS)
        normed = t * r * g_ref[...]
        if not rope:
            return normed, normed
        partner = jnp.where(first_half, -pltpu.roll(normed, LANES - half, 1), pltpu.roll(normed, half, 1))
        return normed, normed * cos + partner * sin

    for hd in range(GQA_HEADS):
        _, rot = head(hd * LANES, qg_ref)
        q_out[0, :, hd * LANES:(hd + 1) * LANES] = (rot * GQA_SCALE).astype(BF16)
    for hd in range(GQA_KV_HEADS):
        normed, rot = head(o_k + hd * LANES, kg_ref)
        k_out[0, :, hd * LANES:(hd + 1) * LANES] = rot.astype(BF16)
        if kt_out is not None:
            seq = kt_out.shape[-1]
            for s in range(kt_out.shape[0]):
                kt_out[s, hd] = normed[s * seq:(s + 1) * seq, :].T[:GQA_HEAD_DIM, :]
    vp_out[0] = _with_sum_lane(z[:, o_vp:o_vp + kw]).astype(BF16)
    if vt_out is not None:
        seq = vt_out.shape[-1]
        for s in range(vt_out.shape[0]):
            vt_out[s] = z[s * seq:(s + 1) * seq, o_v:o_v + LANES].T

    u = z[:, o_u:o_u + GMLP_CH]
    vg = z[:, o_vg:o_vg + GMLP_CH]
    mu = jnp.mean(vg, axis=-1, keepdims=True)
    cen = vg - mu
    var = jnp.mean(cen * cen, axis=-1, keepdims=True)
    vn = (cen * lax.rsqrt(var + EPS) * lng_ref[...] + lnb_ref[...]).astype(BF16)
    bias = bs_ref[...]
    rows = x.shape[0]
    for cidx in range(rows // CHUNK):
        r0 = cidx * CHUNK
        for g in range(GMLP_GROUPS):
            c0 = g * LANES
            mixed = jnp.dot(ws_ref[g], vn[r0:r0 + CHUNK, c0:c0 + LANES], preferred_element_type=F32)
            g_out[0, r0:r0 + CHUNK, c0:c0 + LANES] = (
                u[r0:r0 + CHUNK, c0:c0 + LANES] * (mixed + bias[:, c0:c0 + LANES])).astype(BF16)


def l1_inproj(x, mods, mod_off, n1, w, tables, ctx_seq=None):
    bm, sm, d = x.shape
    tr = min(ROW_TILE, sm)
    rope = tables is not None
    assert rope != (ctx_seq is not None)
    qw = GQA_HEADS * LANES
    kw = GQA_KV_HEADS * LANES
    row = lambda width: pl.BlockSpec((1, tr, width), lambda b, i: (b, i, 0))
    vec = _const_spec((1, LANES))
    in_specs = [row(d), pl.BlockSpec((1, SUBLANES, d), lambda b, i: (b + mod_off, 0, 0)),
                _const_spec((1, d)), _const_spec(w["win"].shape)]
    args = [x, mods, n1, w["win"]]
    in_specs += [vec, vec]
    args += [w["qg"], w["kg"]]
    in_specs += [_const_spec((1, GMLP_CH)), _const_spec((1, GMLP_CH)),
                 _const_spec(w["ws"].shape), _const_spec((CHUNK, GMLP_CH))]
    args += [w["lng"], w["lnb"], w["ws"], w["bs"]]
    if rope:
        in_specs += [pl.BlockSpec((tr, LANES), lambda b, i: (i, 0))] * 2
        args += list(tables)
    out_shape = [jax.ShapeDtypeStruct((bm, sm, qw), BF16),
                 jax.ShapeDtypeStruct((bm, sm, kw), BF16),
                 jax.ShapeDtypeStruct((bm, sm, kw), BF16),
                 jax.ShapeDtypeStruct((bm, sm, GMLP_CH), BF16)]
    out_specs = [row(qw), row(kw), row(kw), row(GMLP_CH)]
    if not rope:
        assert bm == 1 and tr % ctx_seq == 0
        n_seq, per_step = sm // ctx_seq, tr // ctx_seq
        out_shape += [jax.ShapeDtypeStruct((n_seq, GQA_KV_HEADS, GQA_HEAD_DIM, ctx_seq), F32),
                      jax.ShapeDtypeStruct((n_seq, GQA_KV_HEADS * GQA_HEAD_DIM, ctx_seq), F32)]
        out_specs += [pl.BlockSpec((per_step, GQA_KV_HEADS, GQA_HEAD_DIM, ctx_seq), lambda b, i: (i, 0, 0, 0)),
                      pl.BlockSpec((per_step, GQA_KV_HEADS * GQA_HEAD_DIM, ctx_seq), lambda b, i: (i, 0, 0))]
    return pl.pallas_call(
        functools.partial(_l1_inproj_kernel, rope=rope),
        out_shape=out_shape, grid=(bm, sm // tr), in_specs=in_specs, out_specs=out_specs,
        compiler_params=_params(("parallel", "parallel"), VMEM_LIMIT),
        name="l1_inproj_rope" if rope else "l1_inproj",
    )(*args)


def _post_kernel(a_ref, c_ref, x_ref, m_ref, n2_ref, wa_ref, wc_ref, rw2_ref, rb_ref, tri_ref, base_ref,
                 x1_out, xm_out, idx_out, wts_out, rank_out, cnt_out, run_ref):
    first =(pl.program_id(0) == 0) & (pl.program_id(1) == 0)

    @pl.when(first)
    def _():
        run_ref[...] = base_ref[...]

    sub = tri_ref.shape[0]
    running = run_ref[0:1, :]
    for r0 in range(0, a_ref.shape[1], sub):
        running = _route_rows(slice(r0, r0 + sub), running, a_ref, c_ref, x_ref, m_ref, n2_ref, wa_ref, wc_ref,
                              rw2_ref, rb_ref, tri_ref, x1_out, xm_out, idx_out, wts_out, rank_out)
    run_ref[0:1, :] = running
    cnt_out[...] = run_ref[...]


def _route_rows(rs, running, a_ref, c_ref, x_ref, m_ref, n2_ref, wa_ref, wc_ref, rw2_ref, rb_ref, tri_ref,
                x1_out, xm_out, idx_out, wts_out, rank_out):
    m = m_ref[0]
    y = (jnp.dot(a_ref[0, rs, :], wa_ref[...], preferred_element_type=F32)
         + jnp.dot(c_ref[0, rs, :], wc_ref[...], preferred_element_type=F32))
    x1 = x_ref[0, rs, :] + m[2:3] * y
    x1_out[0, rs, :] = x1
    xm = _rms(x1, n2_ref[...]) * (1.0 + m[4:5]) + m[3:4]
    xh = xm.astype(BF16)
    xm_out[0, rs] = _pack_rows(xh)

    xl = (xm - xh.astype(F32)).astype(BF16)
    both = jnp.dot(xh, rw2_ref[...], preferred_element_type=F32)
    logits = (both[:, :LANES] + both[:, LANES:]
              + jnp.dot(xl, rw2_ref[:, :LANES], preferred_element_type=F32)) + rb_ref[...]
    rows = logits.shape[0]
    lane = lax.broadcasted_iota(jnp.int32, (rows, LANES), 1).astype(F32)
    work = logits
    vals, hots = [], []
    idx_acc = jnp.zeros((rows, LANES), F32)
    for k in range(TOP_K):
        top = jnp.max(work, axis=-1, keepdims=True)
        sel = jnp.min(jnp.where(work == top, lane, float(LANES)), axis=-1, keepdims=True)
        hot = lane == sel
        vals.append(top)
        hots.append(hot)
        idx_acc = idx_acc + jnp.where(lane == float(k), sel, 0.0)
        work = jnp.where(hot, -jnp.inf, work)
    exps = [jnp.exp(v - vals[0]) for v in vals]
    denom = exps[0] + exps[1] + exps[2] + exps[3]
    wts = jnp.zeros((rows, LANES), F32)
    for k in range(TOP_K):
        wts = wts + jnp.where(lane == float(k), exps[k] / denom, 0.0)

    chosen = jnp.zeros((rows, LANES), F32)
    for hot in hots:
        chosen = chosen + hot.astype(F32)
    before = jnp.dot(tri_ref[...], chosen.astype(BF16), preferred_element_type=F32) + running
    rank = jnp.zeros((rows, LANES), F32)
    for k in range(TOP_K):
        rk = jnp.sum(jnp.where(hots[k], before, 0.0), axis=-1, keepdims=True)
        rank = rank + jnp.where(lane == float(k), rk, 0.0)
    idx_out[:, rs] = idx_acc.T[:SUBLANES, :].astype(jnp.int32)
    wts_out[0, rs] = wts
    rank_out[:, rs] = rank.T[:SUBLANES, :].astype(jnp.int32)
    return running + jnp.sum(chosen, axis=0, keepdims=True)


def post_mixer(attn, other, x, mods, mod_off, n2, wa, wc, rw2, rb, tri, base):
    bm, sm, d = x.shape
    tr = POST_SUBTILES * tri.shape[0]
    assert sm % tr == 0
    row = lambda width: pl.BlockSpec((1, tr, width), lambda b, i: (b, i, 0))
    tile_rows = pl.BlockSpec((1, tr) + ROW_WORDS, lambda b, i: (b, i, 0, 0))
    nb = sm // tr
    token_minor = pl.BlockSpec((SUBLANES, tr), lambda b, i: (0, b * nb + i))
    out_shape = [jax.ShapeDtypeStruct((bm, sm, d), F32),
                 jax.ShapeDtypeStruct((bm, sm) + ROW_WORDS, jnp.int32),
                 jax.ShapeDtypeStruct((SUBLANES, bm * sm), jnp.int32),
                 jax.ShapeDtypeStruct((bm, sm, LANES), F32),
                 jax.ShapeDtypeStruct((SUBLANES, bm * sm), jnp.int32),
                 jax.ShapeDtypeStruct((SUBLANES, LANES), F32)]
    return pl.pallas_call(
        _post_kernel,
        out_shape=out_shape, grid=(bm, sm // tr),
        in_specs=[row(attn.shape[-1]), row(other.shape[-1]), row(d),
                  pl.BlockSpec((1, SUBLANES, d), lambda b, i: (b + mod_off, 0, 0)),
                  _const_spec((1, d)), _const_spec(wa.shape), _const_spec(wc.shape),
                  _const_spec(rw2.shape), _const_spec((1, LANES)),
                  _const_spec(tri.shape), _const_spec((SUBLANES, LANES))],
        out_specs=[row(d), tile_rows, token_minor, row(LANES),
                   token_minor, _const_spec((SUBLANES, LANES))],
        scratch_shapes=[pltpu.VMEM((SUBLANES, LANES), F32)],
        compiler_params=_params(("arbitrary", "arbitrary"), VMEM_LIMIT),
        name="post_mixer_route",
    )(attn, other, x, mods, n2, wa, wc, rw2, rb, tri, base)


def _plan_kernel(off_ref, idx_ref, rank_ref, pos_out):
    idx = idx_ref[...]
    pos = rank_ref[...]
    for e in range(N_EXPERTS):
        pos = pos + jnp.where(idx == e, off_ref[e], 0)
    pos_out[...] = pos


def plan_positions(offsets, idx, rank):
    n = idx.shape[1]
    tr = min(2048, n)
    spec = pl.BlockSpec((SUBLANES, tr), lambda i, off: (0, i))
    return pl.pallas_call(
        _plan_kernel,
        out_shape=jax.ShapeDtypeStruct((SUBLANES, n), jnp.int32),
        grid_spec=pltpu.PrefetchScalarGridSpec(
            num_scalar_prefetch=1, grid=(n // tr,), in_specs=[spec, spec], out_specs=spec),
        compiler_params=_params(("parallel",)),
        name="plan_positions",
    )(offsets, idx, rank)


def _sc_worker_id():
    return lax.axis_index("s") * SC_CORES + lax.axis_index("c")


def _sc_chunk_rows(tile, dtype):
    row_bytes = math.prod(tile) * jnp.dtype(dtype).itemsize
    return min(SC_CHUNK_BYTES // row_bytes, SC_MAX_INDICES)


def dispatch_rows(xms, posk, n_rows):
    n_total = sum(x.shape[0] for x in xms)
    tile, dtype = xms[0].shape[1:], xms[0].dtype
    chunk = _sc_chunk_rows(tile, dtype)
    starts, s0 = [], 0
    for x in xms:
        assert x.shape[0] % (SC_WORKERS * chunk) == 0
        starts.append(s0)
        s0 += x.shape[0]
    mesh = plsc.VectorSubcoreMesh(core_axis_name="c", subcore_axis_name="s")

    @functools.partial(
        pl.kernel, mesh=mesh, out_type=jax.ShapeDtypeStruct((n_rows,) + tile, dtype),
        scratch_types=[pltpu.VMEM((chunk,), jnp.int32), pltpu.VMEM((chunk,) + tile, dtype),
                       pltpu.SemaphoreType.DMA],
        name="dispatch_rows_sc")
    def scatter(*refs):
        x_refs, pos_hbm, xs_hbm, idx_v, rows_v, sem = refs[:len(xms)], *refs[len(xms):]
        wid = _sc_worker_id()
        for x_hbm, start in zip(x_refs, starts):
            per_worker = x_hbm.shape[0] // SC_WORKERS

            @pl.loop(0, per_worker // chunk)
            def _(c):
                t0 = wid * per_worker + c * chunk
                pltpu.sync_copy(x_hbm.at[pl.ds(t0, chunk)], rows_v)
                for k in range(TOP_K):
                    pltpu.sync_copy(pos_hbm.at[pl.ds(k * n_total + start + t0, chunk)], idx_v)
                    pltpu.async_copy(rows_v, xs_hbm.at[idx_v], sem).wait()

    return scatter(*xms, posk)


def gather_rows(y, posk):
    n_pairs = posk.shape[0]
    tile, dtype = y.shape[1:], y.dtype
    chunk = _sc_chunk_rows(tile, dtype)
    per_worker = n_pairs // SC_WORKERS
    assert per_worker % chunk == 0
    mesh = plsc.VectorSubcoreMesh(core_axis_name="c", subcore_axis_name="s")

    @functools.partial(
        pl.kernel, mesh=mesh, out_type=jax.ShapeDtypeStruct((n_pairs,) + tile, dtype),
        scratch_types=[pltpu.VMEM((chunk,), jnp.int32), pltpu.VMEM((chunk,) + tile, dtype),
                       pltpu.SemaphoreType.DMA],
        name="gather_rows_sc")
    def gather(y_hbm, pos_hbm, out_hbm, idx_v, rows_v, sem):
        wid = _sc_worker_id()

        @pl.loop(0, per_worker // chunk)
        def _(c):
            base = wid * per_worker + c * chunk
            pltpu.sync_copy(pos_hbm.at[pl.ds(base, chunk)], idx_v)
            pltpu.async_copy(y_hbm.at[idx_v], rows_v, sem).wait()
            pltpu.sync_copy(rows_v, out_hbm.at[pl.ds(base, chunk)])

    return gather(y, posk)


def _ffn_kernel(te_ref, nu_ref, first_ref, slot_ref, nxt_ref, xs_ref, w1_hbm, b1_ref, w2_hbm, b2_ref, y_ref,
                w1f, w2f, w1b, w2b, sem):
    i = pl.program_id(0)

    def weight_copies(e, s):
        return (pltpu.make_async_copy(w1_hbm.at[e], w1f.at[s], sem.at[0, s]),
                pltpu.make_async_copy(w2_hbm.at[e], w2f.at[s], sem.at[1, s]))

    @pl.when(i < nu_ref[0])
    def _():
        s = slot_ref[i]

        @pl.when(first_ref[i] == 1)
        def _():
            @pl.when(i == 0)
            def _():
                for cp in weight_copies(te_ref[i], s):
                    cp.start()
            for cp in weight_copies(te_ref[i], s):
                cp.wait()

            @pl.when(nxt_ref[i] >= 0)
            def _():
                for cp in weight_copies(nxt_ref[i], 1 - s):
                    cp.start()
            for c in range(D_MODEL // LANES):
                w1b[c * LANES:(c + 1) * LANES, :] = w1f[s, c * LANES:(c + 1) * LANES, :].astype(BF16)
            for c in range(D_EXPERT // LANES):
                w2b[c * LANES:(c + 1) * LANES, :] = w2f[s, c * LANES:(c + 1) * LANES, :].astype(BF16)

        x = _unpack_rows(xs_ref[...])
        h = jnp.dot(x, w1b[...], preferred_element_type=F32) + b1_ref[0]
        g = jnp.minimum(h[:, :D_EXPERT], SWIGLU_LIMIT)
        lin = jnp.clip(h[:, D_EXPERT:], -SWIGLU_LIMIT, SWIGLU_LIMIT)
        a = (lin + 1.0) * (g * jax.nn.sigmoid(SWIGLU_ALPHA * g))
        y = jnp.dot(a.astype(BF16), w2b[...], preferred_element_type=F32) + b2_ref[0]
        y_ref[...] = _pack_rows(y.astype(BF16))

    @pl.when(i >= nu_ref[0])
    def _():
        y_ref[...] = jnp.zeros(y_ref.shape, y_ref.dtype)


def grouped_ffn(sched, xs, w1, b1, w2, b2):
    r = xs.shape[0]
    d = w1.shape[1]
    nt = r // FFN_TILE
    tile = (FFN_TILE,) + xs.shape[1:]
    rows = lambda i, te, nu, *_: (jnp.minimum(i, nu[0] - 1), 0, 0)
    bsel = lambda i, te, *_: (te[i], 0, 0)
    return pl.pallas_call(
        _ffn_kernel,
        out_shape=jax.ShapeDtypeStruct(xs.shape, xs.dtype),
        grid_spec=pltpu.PrefetchScalarGridSpec(
            num_scalar_prefetch=5, grid=(nt,),
            in_specs=[pl.BlockSpec(tile, rows),
                      pl.BlockSpec(memory_space=pl.ANY),
                      pl.BlockSpec((1, 1, 2 * D_EXPERT), bsel),
                      pl.BlockSpec(memory_space=pl.ANY),
                      pl.BlockSpec((1, 1, d), bsel)],
            out_specs=pl.BlockSpec(tile, lambda i, *_: (i, 0, 0)),
            scratch_shapes=[pltpu.VMEM((2, d, 2 * D_EXPERT), F32), pltpu.VMEM((2, D_EXPERT, d), F32),
                            pltpu.VMEM((d, 2 * D_EXPERT), BF16), pltpu.VMEM((D_EXPERT, d), BF16),
                            pltpu.SemaphoreType.DMA((2, 2))]),
        compiler_params=_params(("arbitrary",), VMEM_LIMIT),
        name="grouped_ffn",
    )(*sched, xs, w1, b1.reshape(N_EXPERTS, 1, -1), w2, b2.reshape(N_EXPERTS, 1, -1))


def _combine_kernel(x1_ref, wts_ref, m_ref, fn_ref, y0_ref, y1_ref, y2_ref, y3_ref, o_ref, *, final):
    w2 = wts_ref[0]
    wcols = [jnp.broadcast_to(w2[:, k:k + 1], w2.shape) for k in range(TOP_K)]
    wcols += [jnp.zeros(w2.shape, F32)] * (SUBLANES - TOP_K)
    w = jnp.concatenate(wcols, axis=1).reshape(w2.shape[0], SUBLANES, LANES)
    rows = lambda ref: pltpu.bitcast(ref[...], BF16).astype(F32)
    acc = w[:, 0:1, :] * rows(y0_ref)
    for k, y_ref in ((1, y1_ref), (2, y2_ref), (3, y3_ref)):
        acc = acc + w[:, k:k + 1, :] * rows(y_ref)
    out = x1_ref[0] + m_ref[0][5:6] * acc.reshape(x1_ref.shape[1], D_MODEL)
    if final:
        out = _rms(out, fn_ref[...])
    o_ref[0] = out


def combine_rows(x1, wts, mods, mod_off, fn, yg, row_off, n_total, *, final):
    bm, sm, d = x1.shape
    tr = min(COMBINE_TILE, sm)
    nb = sm // tr
    row = lambda width: pl.BlockSpec((1, tr, width), lambda b, i: (b, i, 0))
    ysel = lambda k: pl.BlockSpec((tr,) + yg.shape[1:],
                                  lambda b, i: ((k * n_total + row_off) // tr + b * nb + i, 0, 0))
    return pl.pallas_call(
        functools.partial(_combine_kernel, final=final),
        out_shape=jax.ShapeDtypeStruct((bm, sm, d), F32),
        grid=(bm, nb),
        in_specs=[row(d), row(LANES),
                  pl.BlockSpec((1, SUBLANES, d), lambda b, i: (b + mod_off, 0, 0)),
                  _const_spec((1, d))] + [ysel(k) for k in range(TOP_K)],
        out_specs=row(d),
        compiler_params=_params(("parallel", "parallel"), VMEM_LIMIT),
        name="combine_rows",
    )(x1, wts, mods, fn, yg, yg, yg, yg)


def _axial_angles(n_tokens, rot_dim):
    t = jnp.arange(n_tokens)
    rows = (t // GRID_W).astype(F32)
    cols = (t % GRID_W).astype(F32)
    n_freq = rot_dim // 4
    inv = ROPE_THETA ** (-jnp.arange(n_freq, dtype=F32) / n_freq)
    return jnp.concatenate([rows[:, None] * inv, cols[:, None] * inv], axis=-1)


def _swap_halves(w):
    half = w.shape[-1] // 2
    return jnp.concatenate([-w[..., half:], w[..., :half]], axis=-1)


def _prep_l0(w_in, q_norm, kv_norm, w_uq, w_uk, w_uv, w_out):
    d = w_in.shape[0]
    o_kr = MLA_Q_RANK + MLA_KV_RANK
    kr_cols = w_in[:, o_kr:o_kr + MLA_ROPE]
    win = jnp.concatenate(
        [w_in[:, :o_kr], kr_cols, _swap_halves(kr_cols), jnp.zeros((d, LANES - 2 * MLA_ROPE), F32),
         w_in[:, o_kr + MLA_ROPE:]], axis=1).astype(BF16)
    qk = MLA_NOPE + MLA_ROPE
    wuq3 = w_uq.reshape(MLA_Q_RANK, MLA_HEADS, qk)
    wuq = jnp.pad(wuq3, ((0, 0), (0, 0), (0, LANES - qk))).reshape(MLA_Q_RANK, -1).astype(BF16)
    wuk3 = w_uk.reshape(MLA_KV_RANK, MLA_HEADS, MLA_NOPE)
    wuk = jnp.pad(wuk3, ((0, 0), (0, 0), (0, LANES - MLA_NOPE))).reshape(MLA_KV_RANK, -1).astype(BF16)
    wuv3 = w_uv.reshape(MLA_KV_RANK, MLA_HEADS, MLA_V)
    wuv = jnp.pad(wuv3, ((0, 0), (0, 0), (0, LANES - MLA_V))).reshape(MLA_KV_RANK, -1).astype(BF16)
    eye = jnp.eye(MLA_ROPE, dtype=F32)
    e_head = jnp.concatenate([jnp.zeros((MLA_ROPE, MLA_NOPE), F32), eye,
                              jnp.zeros((MLA_ROPE, LANES - qk), F32)], axis=1)
    e = jnp.pad(jnp.tile(e_head, (1, MLA_HEADS)), ((0, LANES - MLA_ROPE), (0, 0))).astype(BF16)
    wa3 = w_out[:MLA_HEADS * MLA_V].reshape(MLA_HEADS, MLA_V, d)
    wa = jnp.pad(wa3, ((0, 0), (0, LANES - MLA_V), (0, 0))).reshape(MLA_HEADS * LANES, d).astype(BF16)
    wc = w_out[MLA_HEADS * MLA_V:].astype(BF16)
    return dict(win=win, qg=q_norm.reshape(1, -1), kvg=kv_norm.reshape(1, -1), wuq=wuq,
                wuk=wuk, e=e, wuv=wuv, wa=wa, wc=wc)


def _l0_tables(n):
    ang = _axial_angles(n, MLA_ROPE)
    cos, sin = jnp.cos(ang), jnp.sin(ang)
    one = jnp.ones((n, 1), F32)
    zero = jnp.zeros((n, 1), F32)
    rest = LANES - MLA_NOPE - MLA_ROPE
    cq = jnp.concatenate([jnp.tile(one, (1, MLA_NOPE)), cos, cos, jnp.tile(one, (1, rest))], axis=1)
    sq = jnp.concatenate([jnp.tile(zero, (1, MLA_NOPE)), sin, sin, jnp.tile(zero, (1, rest))], axis=1)
    ck = jnp.concatenate([cos, cos, jnp.tile(zero, (1, LANES - MLA_ROPE))], axis=1)
    sk = jnp.concatenate([sin, sin, jnp.tile(zero, (1, LANES - MLA_ROPE))], axis=1)
    return cq, sq, ck, sk


def _pad_heads(w, n_heads, dim):
    d = w.shape[0]
    return jnp.pad(w.reshape(d, n_heads, dim), ((0, 0), (0, 0), (0, LANES - dim))).reshape(d, n_heads * LANES)


def _prep_l1(w_in, q_norm, k_norm, ln_g, ln_b, w_s, b_s, w_out):
    d = w_in.shape[0]
    qd = GQA_HEADS * GQA_HEAD_DIM
    kd = GQA_KV_HEADS * GQA_HEAD_DIM
    wq, wk, wv = w_in[:, :qd], w_in[:, qd:qd + kd], w_in[:, qd + kd:qd + 2 * kd]
    rest = w_in[:, qd + 2 * kd:]
    win = jnp.concatenate([_pad_heads(wq, GQA_HEADS, GQA_HEAD_DIM), _pad_heads(wk, GQA_KV_HEADS, GQA_HEAD_DIM),
                           _pad_heads(wv, GQA_KV_HEADS, GQA_HEAD_DIM), wv, rest], axis=1).astype(BF16)
    padg = lambda g: jnp.pad(g, (0, LANES - GQA_HEAD_DIM)).reshape(1, LANES)
    wa3 = w_out[:qd].reshape(GQA_HEADS, GQA_HEAD_DIM, d)
    wa = jnp.pad(wa3, ((0, 0), (0, LANES - GQA_HEAD_DIM), (0, 0))).reshape(GQA_HEADS * LANES, d).astype(BF16)
    wc = w_out[qd:].astype(BF16)
    bs = jnp.repeat(b_s.T, LANES, axis=1)
    return dict(win=win, qg=padg(q_norm), kg=padg(k_norm), lng=ln_g.reshape(1, -1), lnb=ln_b.reshape(1, -1),
                ws=w_s.astype(BF16), bs=bs, wa=wa, wc=wc)


def _l1_tables(n):
    ang = _axial_angles(n, GQA_HEAD_DIM)
    cos, sin = jnp.cos(ang), jnp.sin(ang)
    pad = LANES - GQA_HEAD_DIM
    c = jnp.concatenate([cos, cos, jnp.ones((n, pad), F32)], axis=1)
    s = jnp.concatenate([sin, sin, jnp.zeros((n, pad), F32)], axis=1)
    return c, s


def _pad_lanes(x, width):
    return jnp.pad(x, [(0, 0)] * (x.ndim - 1) + [(0, width - x.shape[-1])])


def routed_ffn(groups, mods, n2, moe, tri, final_norm, *, final):
    router_w, router_b, w1, b1, w2, b2 = moe
    rw = _pad_lanes(router_w, LANES)
    rwh = rw.astype(BF16)
    rwl = (rw - rwh.astype(F32)).astype(BF16)
    rw2 = jnp.concatenate([rwh, rwl], axis=1)
    rb = jnp.concatenate([router_b, jnp.full((LANES - N_EXPERTS,), NEG_BIG, F32)]).reshape(1, LANES)
    base = jnp.zeros((SUBLANES, LANES), F32)
    routed = []
    for g in groups:
        x1, xm, idx, wts, rank, base = post_mixer(
            g["attn"], g["other"], g["x"], mods, g["mod_off"], n2, g["wa"], g["wc"], rw2, rb, tri, base)
        routed.append((x1, xm, idx, wts, rank))
    n_total = sum(r[0].shape[0] * r[0].shape[1] for r in routed)
    nt = n_total * TOP_K // FFN_TILE + N_EXPERTS

    counts = base[0, :N_EXPERTS].astype(jnp.int32)
    tiles = (counts + FFN_TILE - 1) // FFN_TILE
    tile_end = jnp.cumsum(tiles)
    offsets = (tile_end - tiles) * FFN_TILE
    n_used = tile_end[-1:].astype(jnp.int32)
    experts = jnp.arange(N_EXPERTS)
    busy = tiles > 0
    slot_e = (jnp.cumsum(busy) - 1) % 2
    later = jnp.where(busy[None, :] & (experts[None, :] > experts[:, None]), experts[None, :], N_EXPERTS)
    nxt_e = jnp.min(later, axis=1)
    nxt_e = jnp.where(nxt_e == N_EXPERTS, -1, nxt_e)
    tile_expert = jnp.sum(jnp.arange(nt)[:, None] >= tile_end[None, :], axis=1)
    tile_expert = jnp.minimum(tile_expert, jnp.max(jnp.where(busy, experts, 0)))
    first = jnp.concatenate([jnp.ones((1,), bool), tile_expert[1:] != tile_expert[:-1]])
    of_tile = tile_expert[:, None] == experts[None, :]
    slot_t = jnp.sum(jnp.where(of_tile, slot_e[None, :], 0), axis=1)
    nxt_t = jnp.sum(jnp.where(of_tile, nxt_e[None, :], 0), axis=1)
    sched = tuple(a.astype(jnp.int32) for a in (tile_expert, n_used, first, slot_t, nxt_t))

    positions, xms = [], []
    for (x1, xm, idx, wts, rank) in routed:
        positions.append(plan_positions(offsets, idx, rank)[:TOP_K])
        xms.append(xm.reshape((idx.shape[1],) + ROW_WORDS))
    posk = jnp.concatenate(positions, axis=1).reshape(-1)
    xs = dispatch_rows(xms, posk, nt * FFN_TILE)
    y = grouped_ffn(sched, xs, w1, b1, w2, b2)
    outs = []
    for g, (x1, xm, idx, wts, rank), pos in zip(groups, routed, positions):
        yg = gather_rows(y, pos.reshape(-1))
        outs.append(combine_rows(x1, wts, mods, g["mod_off"], final_norm, yg, 0, pos.shape[1], final=final))
    return outs


def kernel(x_prompt, x_sample, cache_l0_ckv, cache_l0_krope, cache_l1_k, cache_l1_v, c, c_ctx,
           l0_ada_w, l0_ada_b, l0_norm1, l0_w_in, l0_q_norm, l0_kv_norm, l0_w_uq, l0_w_uk, l0_w_uv,
           l0_conv_w, l0_conv_b, l0_conv_ln_g, l0_conv_ln_b, l0_w_out, l0_norm2,
           l0_router_w, l0_router_b, l0_w1, l0_b1, l0_w2, l0_b2,
           l1_ada_w, l1_ada_b, l1_norm1, l1_w_in, l1_q_norm, l1_k_norm, l1_gmlp_ln_g, l1_gmlp_ln_b,
           l1_w_s, l1_b_s, l1_w_out, l1_norm2,
           l1_router_w, l1_router_b, l1_w1, l1_b1, l1_w2, l1_b2,
           final_norm):
    bp, sp, d = x_prompt.shape
    bs, ss, _ = x_sample.shape
    past = cache_l0_ckv.shape[1]
    n_p = bp * sp

    cond8 = jnp.concatenate([c_ctx[None], c, jnp.zeros((SUBLANES - 1 - bs, d), F32)], axis=0)
    mods0 = adaln(cond8, l0_ada_w, l0_ada_b)
    mods1 = adaln(cond8, l1_ada_w, l1_ada_b)
    tri = jnp.tril(jnp.ones((ROW_TILE, ROW_TILE), F32), -1).astype(BF16)
    fn = final_norm.reshape(1, d)

    w0 = _prep_l0(l0_w_in, l0_q_norm, l0_kv_norm, l0_w_uq, l0_w_uk, l0_w_uv, l0_w_out)
    n1 = l0_norm1.reshape(1, d)
    hp = x_prompt.reshape(1, n_p, d)
    q_p, k_p, v_p, ckv_p, kr_p, u_p = l0_inproj(hp, mods0, 0, n1, w0, None)
    q_s, k_s, v_s, _, _, u_s = l0_inproj(x_sample, mods0, 1, n1, w0, _l0_tables(ss),
                                         cache=(cache_l0_ckv, _pad_lanes(cache_l0_krope, LANES)))
    hw = MLA_HEADS * LANES
    att_p = attention(q_p.reshape(bp, sp, hw), k_p.reshape(bp, sp, hw), v_p.reshape(bp, sp, hw),
                      n_heads=MLA_HEADS, n_kv=MLA_HEADS, heads_per_step=MLA_HEADS)
    att_s = attention(q_s, k_s, v_s,
                      n_heads=MLA_HEADS, n_kv=MLA_HEADS, heads_per_step=LATENT_HEADS_PER_STEP)
    conv_p = conformer_conv(u_p.reshape(bp, sp, CONV_CH), l0_conv_w, l0_conv_b, l0_conv_ln_g, l0_conv_ln_b)
    conv_s = conformer_conv(u_s, l0_conv_w, l0_conv_b, l0_conv_ln_g, l0_conv_ln_b)
    groups = [dict(attn=att_p.reshape(1, n_p, hw), other=conv_p.reshape(1, n_p, CONV_CH), x=hp, mod_off=0,
                   wa=w0["wa"], wc=w0["wc"]),
              dict(attn=att_s, other=conv_s, x=x_sample, mod_off=1, wa=w0["wa"], wc=w0["wc"])]
    hp, hs = routed_ffn(groups, mods0, l0_norm2.reshape(1, d),
                        (l0_router_w, l0_router_b, l0_w1, l0_b1, l0_w2, l0_b2), tri, fn, final=False)
    new_l0_ckv = ckv_p.reshape(bp, sp, MLA_KV_RANK)
    new_l0_krope = kr_p.reshape(bp, sp, MLA_ROPE)

    w1p = _prep_l1(l1_w_in, l1_q_norm, l1_k_norm, l1_gmlp_ln_g, l1_gmlp_ln_b, l1_w_s, l1_b_s, l1_w_out)
    n1 = l1_norm1.reshape(1, d)
    q_p, k_p, vp_p, gat_p, kt_p, vt_p = l1_inproj(hp, mods1, 0, n1, w1p, None, ctx_seq=sp)
    q_s, k_s, vp_s, gat_s = l1_inproj(hs, mods1, 1, n1, w1p, _l1_tables(ss))
    qw = GQA_HEADS * LANES
    kw = GQA_KV_HEADS * LANES
    pad_kv = lambda t: _pad_lanes(t, LANES).reshape(bs, past, kw).astype(BF16)
    att_p = attention(q_p.reshape(bp, sp, qw), k_p.reshape(bp, sp, kw), vp_p.reshape(bp, sp, kw),
                      n_heads=GQA_HEADS, n_kv=GQA_KV_HEADS, heads_per_step=GQA_HEADS)
    att_s = attention(q_s, jnp.concatenate([pad_kv(cache_l1_k), k_s], axis=1),
                      jnp.concatenate([_with_sum_lane(pad_kv(cache_l1_v)), vp_s], axis=1),
                      n_heads=GQA_HEADS, n_kv=GQA_KV_HEADS, heads_per_step=LATENT_HEADS_PER_STEP)
    groups = [dict(attn=att_p.reshape(1, n_p, qw), other=gat_p, x=hp, mod_off=0, wa=w1p["wa"], wc=w1p["wc"]),
              dict(attn=att_s, other=gat_s, x=hs, mod_off=1, wa=w1p["wa"], wc=w1p["wc"])]
    yp, ys = routed_ffn(groups, mods1, l1_norm2.reshape(1, d),
                        (l1_router_w, l1_router_b, l1_w1, l1_b1, l1_w2, l1_b2), tri, fn, final=True)
    new_l1_k = jnp.transpose(kt_p, (0, 3, 1, 2))
    new_l1_v = jnp.transpose(vt_p.reshape(bp, GQA_KV_HEADS, GQA_HEAD_DIM, sp), (0, 3, 1, 2))
    return (yp.reshape(bp, sp, d), ys, new_l0_ckv, new_l0_krope, new_l1_k, new_l1_v)
```
